```python
import jax, jax.numpy as jnp
from jax import lax
import numpy as np

D_MODEL = 2048
BATCH = 8
SEQ = 2048
DEPTH = 1

PLE_DIM = 256
D_FF = 5632
FOX_HEADS = 8
FOX_HEAD_DIM = 128
FOX_WIDTH = FOX_HEADS * FOX_HEAD_DIM
HGRN_HEADS = 8
HGRN_KEY_DIM = 128
HGRN_VAL_DIM = 128
HGRN_KEY_WIDTH = HGRN_HEADS * HGRN_KEY_DIM
HGRN_VAL_WIDTH = HGRN_HEADS * HGRN_VAL_DIM
Q_BLOCK = 128
CHUNK = 64
NORM_EPS = 1e-6
MACARON_SCALE = 0.5
SPLIT_SIZES = (FOX_WIDTH, FOX_WIDTH, FOX_WIDTH, FOX_HEADS,
               HGRN_KEY_WIDTH, HGRN_KEY_WIDTH, HGRN_VAL_WIDTH, HGRN_VAL_WIDTH,
               D_MODEL, D_MODEL)
IN_WIDTH = int(sum(SPLIT_SIZES))
SPLIT_POINTS = tuple(int(v) for v in np.cumsum(SPLIT_SIZES)[:-1])

kernel_name = 'hybrid_fox_hgrn2_macaron_sandwich_ple'


def rms_norm(x, g):
    xf = x.astype(jnp.float32)
    y = xf * lax.rsqrt(jnp.mean(xf * xf, axis=-1, keepdims=True) + NORM_EPS)
    return (y * g.astype(jnp.float32)).astype(x.dtype)


def swiglu(u, w_gate, w_up, w_down):
    return (jax.nn.silu(u @ w_gate) * (u @ w_up)) @ w_down


def fox_attention(q, k, v, log_f):
    b, s, h, dh = q.shape
    nb = s // Q_BLOCK
    c = jnp.cumsum(log_f, axis=1).transpose(0, 2, 1)
    qb = q.reshape(b, nb, Q_BLOCK, h, dh).transpose(1, 0, 2, 3, 4)
    cb = c.reshape(b, h, nb, Q_BLOCK).transpose(2, 0, 1, 3)
    kpos = jnp.arange(s)
    scale = FOX_HEAD_DIM ** -0.5

    def block(args):
        qi, ci, bi = args
        logits = jnp.einsum('bqhd,bkhd->bhqk', qi, k,
                            preferred_element_type=jnp.float32) * scale
        logits = logits + ci[:, :, :, None] - c[:, :, None, :]
        qpos = bi * Q_BLOCK + jnp.arange(Q_BLOCK)
        causal = kpos[None, :] <= qpos[:, None]
        logits = jnp.where(causal, logits, -jnp.inf)
        probs = jax.nn.softmax(logits, axis=-1)
        return jnp.einsum('bhqk,bkhd->bqhd', probs.astype(v.dtype), v)

    out = lax.map(block, (qb, cb, jnp.arange(nb)))
    return out.transpose(1, 0, 2, 3, 4).reshape(b, s, h, dh)


def hgrn2_recurrence(q, k, v, log_f):
    b, s, h, dk = q.shape
    dv = v.shape[-1]
    n = s // CHUNK

    def chunks(t):
        return t.astype(jnp.float32).reshape(b, n, CHUNK, h, t.shape[-1]).transpose(1, 0, 3, 2, 4)

    causal = jnp.tril(jnp.ones((CHUNK, CHUNK), dtype=bool))

    def step(state, inp):
        qc, kc, vc, lfc = inp
        cum = jnp.cumsum(lfc, axis=2)
        inter = jnp.einsum('bhtk,bhkv->bhtv', qc * jnp.exp(cum), state)
        rel = jnp.where(causal[:, :, None],
                        cum[:, :, :, None, :] - cum[:, :, None, :, :], -jnp.inf)
        scores = jnp.einsum('bhtk,bhsk,bhtsk->bhts', qc, kc, jnp.exp(rel))
        intra = jnp.einsum('bhts,bhsv->bhtv', scores, vc)
        last = cum[:, :, -1:, :]
        new_state = (state * jnp.exp(last[:, :, 0, :, None])
                     + jnp.einsum('bhsk,bhsv->bhkv', kc * jnp.exp(last - cum), vc))
        return new_state, inter + intra

    state0 = jnp.zeros((b, h, dk, dv), jnp.float32)
    _, out = lax.scan(step, state0, (chunks(q), chunks(k), chunks(v), chunks(log_f)))
    return out.transpose(1, 0, 3, 2, 4).reshape(b, s, h, dv)


def token_mixing(u, w_in, fox_f_bias, lower_bound, hgrn_norm_g, w_proj_fox, w_proj_hgrn, w_out):
    b, s, _ = u.shape
    proj = u @ w_in
    q_a, k_a, v_a, f_a, q_b, f_b, i_b, g_b, gate_a, gate_b = jnp.split(proj, SPLIT_POINTS, axis=-1)
    log_fa = jax.nn.log_sigmoid((f_a + fox_f_bias).astype(jnp.float32))
    y_a = fox_attention(q_a.reshape(b, s, FOX_HEADS, FOX_HEAD_DIM),
                        k_a.reshape(b, s, FOX_HEADS, FOX_HEAD_DIM),
                        v_a.reshape(b, s, FOX_HEADS, FOX_HEAD_DIM), log_fa)
    y_a = y_a.reshape(b, s, FOX_WIDTH) @ w_proj_fox
    f = lower_bound + (1.0 - lower_bound) * jax.nn.sigmoid(f_b.astype(jnp.float32))
    o_b = hgrn2_recurrence(jax.nn.silu(q_b).reshape(b, s, HGRN_HEADS, HGRN_KEY_DIM),
                           (1.0 - f).reshape(b, s, HGRN_HEADS, HGRN_KEY_DIM),
                           i_b.reshape(b, s, HGRN_HEADS, HGRN_VAL_DIM),
                           jnp.log(f).reshape(b, s, HGRN_HEADS, HGRN_KEY_DIM))
    o_b = rms_norm(o_b, hgrn_norm_g).astype(u.dtype) * jax.nn.silu(g_b.reshape(b, s, HGRN_HEADS, HGRN_VAL_DIM))
    y_b = o_b.reshape(b, s, HGRN_VAL_WIDTH) @ w_proj_hgrn
    merged = jax.nn.sigmoid(gate_a) * y_a + jax.nn.sigmoid(gate_b) * y_b
    return merged @ w_out


def _fwd_setup_inputs(seed: int = 0) -> dict:
    key = jax.random.key(seed)
    ks = jax.random.split(key, 25)
    f32 = jnp.float32

    def w(k, shape, fan_in):
        return jax.random.normal(k, shape, f32) * (fan_in ** -0.5)

    def gain(k, shape):
        return 1.0 + 0.02 * jax.random.normal(k, shape, f32)

    L, D, F = DEPTH, D_MODEL, D_FF
    return {
        'x': jax.random.normal(ks[0], (BATCH, SEQ, D), f32),
        'p': jax.random.normal(ks[1], (L, BATCH, SEQ, PLE_DIM), f32),
        'ffn1_pre_g': gain(ks[2], (L, D)),
        'ffn1_post_g': gain(ks[3], (L, D)),
        'ffn1_w_gate': w(ks[4], (L, D, F), D),
        'ffn1_w_up': w(ks[5], (L, D, F), D),
        'ffn1_w_down': w(ks[6], (L, F, D), F),
        'mix_pre_g': gain(ks[7], (L, D)),
        'mix_post_g': gain(ks[8], (L, D)),
        'mix_w_in': w(ks[9], (L, D, IN_WIDTH), D),
        'fox_f_bias': 0.1 * jax.random.normal(ks[10], (L, FOX_HEADS), f32),
        'hgrn_lb_logits': 0.1 * jax.random.normal(ks[11], (L + 1, HGRN_KEY_WIDTH), f32),
        'hgrn_norm_g': gain(ks[12], (L, HGRN_VAL_DIM)),
        'mix_w_proj_fox': w(ks[13], (L, FOX_WIDTH, D), FOX_WIDTH),
        'mix_w_proj_hgrn': w(ks[14], (L, HGRN_VAL_WIDTH, D), HGRN_VAL_WIDTH),
        'mix_w_out': w(ks[15], (L, D, D), D),
        'ffn2_pre_g': gain(ks[16], (L, D)),
        'ffn2_post_g': gain(ks[17], (L, D)),
        'ffn2_w_gate': w(ks[18], (L, D, F), D),
        'ffn2_w_up': w(ks[19], (L, D, F), D),
        'ffn2_w_down': w(ks[20], (L, F, D), F),
        'ple_pre_g': gain(ks[21], (L, D)),
        'ple_post_g': gain(ks[22], (L, D)),
        'ple_w_gate': w(ks[23], (L, D, D), D),
        'ple_w_proj': w(ks[24], (L, PLE_DIM, D), PLE_DIM),
    }


def _fwd_reference(x, p, ffn1_pre_g, ffn1_post_g, ffn1_w_gate, ffn1_w_up, ffn1_w_down,
              mix_pre_g, mix_post_g, mix_w_in, fox_f_bias, hgrn_lb_logits, hgrn_norm_g,
              mix_w_proj_fox, mix_w_proj_hgrn, mix_w_out,
              ffn2_pre_g, ffn2_post_g, ffn2_w_gate, ffn2_w_up, ffn2_w_down,
              ple_pre_g, ple_post_g, ple_w_gate, ple_w_proj):
    lower_bounds = jnp.cumsum(jax.nn.softmax(hgrn_lb_logits.astype(jnp.float32), axis=0), axis=0)
    h = x
    for i in range(DEPTH):
        h = h + MACARON_SCALE * rms_norm(
            swiglu(rms_norm(h, ffn1_pre_g[i]), ffn1_w_gate[i], ffn1_w_up[i], ffn1_w_down[i]),
            ffn1_post_g[i])
        h = h + rms_norm(
            token_mixing(rms_norm(h, mix_pre_g[i]), mix_w_in[i], fox_f_bias[i], lower_bounds[i],
                         hgrn_norm_g[i], mix_w_proj_fox[i], mix_w_proj_hgrn[i], mix_w_out[i]),
            mix_post_g[i])
        h = h + MACARON_SCALE * rms_norm(
            swiglu(rms_norm(h, ffn2_pre_g[i]), ffn2_w_gate[i], ffn2_w_up[i], ffn2_w_down[i]),
            ffn2_post_g[i])
        u = rms_norm(h, ple_pre_g[i])
        h = h + rms_norm(jax.nn.sigmoid(u @ ple_w_gate[i]) * (p[i] @ ple_w_proj[i]), ple_post_g[i])
    return h


import jax as _jax
import jax.numpy as _jnp

TWIN_FORMAT = 'train_step'
FWD_PARAMS = ['x', 'p', 'ffn1_pre_g', 'ffn1_post_g', 'ffn1_w_gate', 'ffn1_w_up', 'ffn1_w_down', 'mix_pre_g', 'mix_post_g', 'mix_w_in', 'fox_f_bias', 'hgrn_lb_logits', 'hgrn_norm_g', 'mix_w_proj_fox', 'mix_w_proj_hgrn', 'mix_w_out', 'ffn2_pre_g', 'ffn2_post_g', 'ffn2_w_gate', 'ffn2_w_up', 'ffn2_w_down', 'ple_pre_g', 'ple_post_g', 'ple_w_gate', 'ple_w_proj']
TWIN_WEIGHTS = ['ffn1_pre_g', 'ffn1_post_g', 'ffn1_w_gate', 'ffn1_w_up', 'ffn1_w_down', 'mix_pre_g', 'mix_post_g', 'mix_w_in', 'fox_f_bias', 'hgrn_lb_logits', 'hgrn_norm_g', 'mix_w_proj_fox', 'mix_w_proj_hgrn', 'mix_w_out', 'ffn2_pre_g', 'ffn2_post_g', 'ffn2_w_gate', 'ffn2_w_up', 'ffn2_w_down', 'ple_pre_g', 'ple_post_g', 'ple_w_gate', 'ple_w_proj']
TWIN_DIFF_INPUT = 'x'
TWIN_INPUTS = ['x', 'p', 'ffn1_pre_g', 'ffn1_post_g', 'ffn1_w_gate', 'ffn1_w_up', 'ffn1_w_down', 'mix_pre_g', 'mix_post_g', 'mix_w_in', 'fox_f_bias', 'hgrn_lb_logits', 'hgrn_norm_g', 'mix_w_proj_fox', 'mix_w_proj_hgrn', 'mix_w_out', 'ffn2_pre_g', 'ffn2_post_g', 'ffn2_w_gate', 'ffn2_w_up', 'ffn2_w_down', 'ple_pre_g', 'ple_post_g', 'ple_w_gate', 'ple_w_proj', 'loss_target', 'm_ffn1_pre_g', 'm_ffn1_post_g', 'm_ffn1_w_gate', 'm_ffn1_w_up', 'm_ffn1_w_down', 'm_mix_pre_g', 'm_mix_post_g', 'm_mix_w_in', 'm_fox_f_bias', 'm_hgrn_lb_logits', 'm_hgrn_norm_g', 'm_mix_w_proj_fox', 'm_mix_w_proj_hgrn', 'm_mix_w_out', 'm_ffn2_pre_g', 'm_ffn2_post_g', 'm_ffn2_w_gate', 'm_ffn2_w_up', 'm_ffn2_w_down', 'm_ple_pre_g', 'm_ple_post_g', 'm_ple_w_gate', 'm_ple_w_proj', 'v_ffn1_pre_g', 'v_ffn1_post_g', 'v_ffn1_w_gate', 'v_ffn1_w_up', 'v_ffn1_w_down', 'v_mix_pre_g', 'v_mix_post_g', 'v_mix_w_in', 'v_fox_f_bias', 'v_hgrn_lb_logits', 'v_hgrn_norm_g', 'v_mix_w_proj_fox', 'v_mix_w_proj_hgrn', 'v_mix_w_out', 'v_ffn2_pre_g', 'v_ffn2_post_g', 'v_ffn2_w_gate', 'v_ffn2_w_up', 'v_ffn2_w_down', 'v_ple_pre_g', 'v_ple_post_g', 'v_ple_w_gate', 'v_ple_w_proj']
TWIN_OUTPUTS = ['loss', 'grad_x', 'grad_ffn1_pre_g', 'grad_ffn1_post_g', 'grad_ffn1_w_gate', 'grad_ffn1_w_up', 'grad_ffn1_w_down', 'grad_mix_pre_g', 'grad_mix_post_g', 'grad_mix_w_in', 'grad_fox_f_bias', 'grad_hgrn_lb_logits', 'grad_hgrn_norm_g', 'grad_mix_w_proj_fox', 'grad_mix_w_proj_hgrn', 'grad_mix_w_out', 'grad_ffn2_pre_g', 'grad_ffn2_post_g', 'grad_ffn2_w_gate', 'grad_ffn2_w_up', 'grad_ffn2_w_down', 'grad_ple_pre_g', 'grad_ple_post_g', 'grad_ple_w_gate', 'grad_ple_w_proj', 'delta_ffn1_pre_g', 'delta_ffn1_post_g', 'delta_ffn1_w_gate', 'delta_ffn1_w_up', 'delta_ffn1_w_down', 'delta_mix_pre_g', 'delta_mix_post_g', 'delta_mix_w_in', 'delta_fox_f_bias', 'delta_hgrn_lb_logits', 'delta_hgrn_norm_g', 'delta_mix_w_proj_fox', 'delta_mix_w_proj_hgrn', 'delta_mix_w_out', 'delta_ffn2_pre_g', 'delta_ffn2_post_g', 'delta_ffn2_w_gate', 'delta_ffn2_w_up', 'delta_ffn2_w_down', 'delta_ple_pre_g', 'delta_ple_post_g', 'delta_ple_w_gate', 'delta_ple_w_proj', 'new_m_ffn1_pre_g', 'new_m_ffn1_post_g', 'new_m_ffn1_w_gate', 'new_m_ffn1_w_up', 'new_m_ffn1_w_down', 'new_m_mix_pre_g', 'new_m_mix_post_g', 'new_m_mix_w_in', 'new_m_fox_f_bias', 'new_m_hgrn_lb_logits', 'new_m_hgrn_norm_g', 'new_m_mix_w_proj_fox', 'new_m_mix_w_proj_hgrn', 'new_m_mix_w_out', 'new_m_ffn2_pre_g', 'new_m_ffn2_post_g', 'new_m_ffn2_w_gate', 'new_m_ffn2_w_up', 'new_m_ffn2_w_down', 'new_m_ple_pre_g', 'new_m_ple_post_g', 'new_m_ple_w_gate', 'new_m_ple_w_proj', 'new_v_ffn1_pre_g', 'new_v_ffn1_post_g', 'new_v_ffn1_w_gate', 'new_v_ffn1_w_up', 'new_v_ffn1_w_down', 'new_v_mix_pre_g', 'new_v_mix_post_g', 'new_v_mix_w_in', 'new_v_fox_f_bias', 'new_v_hgrn_lb_logits', 'new_v_hgrn_norm_g', 'new_v_mix_w_proj_fox', 'new_v_mix_w_proj_hgrn', 'new_v_mix_w_out', 'new_v_ffn2_pre_g', 'new_v_ffn2_post_g', 'new_v_ffn2_w_gate', 'new_v_ffn2_w_up', 'new_v_ffn2_w_down', 'new_v_ple_pre_g', 'new_v_ple_post_g', 'new_v_ple_w_gate', 'new_v_ple_w_proj']
TWIN_LEAF_KINDS = {'loss': 'loss', 'grad_x': 'grad_x', 'grad_ffn1_pre_g': 'grad_w', 'grad_ffn1_post_g': 'grad_w', 'grad_ffn1_w_gate': 'grad_w', 'grad_ffn1_w_up': 'grad_w', 'grad_ffn1_w_down': 'grad_w', 'grad_mix_pre_g': 'grad_w', 'grad_mix_post_g': 'grad_w', 'grad_mix_w_in': 'grad_w', 'grad_fox_f_bias': 'grad_w', 'grad_hgrn_lb_logits': 'grad_w', 'grad_hgrn_norm_g': 'grad_w', 'grad_mix_w_proj_fox': 'grad_w', 'grad_mix_w_proj_hgrn': 'grad_w', 'grad_mix_w_out': 'grad_w', 'grad_ffn2_pre_g': 'grad_w', 'grad_ffn2_post_g': 'grad_w', 'grad_ffn2_w_gate': 'grad_w', 'grad_ffn2_w_up': 'grad_w', 'grad_ffn2_w_down': 'grad_w', 'grad_ple_pre_g': 'grad_w', 'grad_ple_post_g': 'grad_w', 'grad_ple_w_gate': 'grad_w', 'grad_ple_w_proj': 'grad_w', 'delta_ffn1_pre_g': 'delta_w', 'delta_ffn1_post_g': 'delta_w', 'delta_ffn1_w_gate': 'delta_w', 'delta_ffn1_w_up': 'delta_w', 'delta_ffn1_w_down': 'delta_w', 'delta_mix_pre_g': 'delta_w', 'delta_mix_post_g': 'delta_w', 'delta_mix_w_in': 'delta_w', 'delta_fox_f_bias': 'delta_w', 'delta_hgrn_lb_logits': 'delta_w', 'delta_hgrn_norm_g': 'delta_w', 'delta_mix_w_proj_fox': 'delta_w', 'delta_mix_w_proj_hgrn': 'delta_w', 'delta_mix_w_out': 'delta_w', 'delta_ffn2_pre_g': 'delta_w', 'delta_ffn2_post_g': 'delta_w', 'delta_ffn2_w_gate': 'delta_w', 'delta_ffn2_w_up': 'delta_w', 'delta_ffn2_w_down': 'delta_w', 'delta_ple_pre_g': 'delta_w', 'delta_ple_post_g': 'delta_w', 'delta_ple_w_gate': 'delta_w', 'delta_ple_w_proj': 'delta_w', 'new_m_ffn1_pre_g': 'new_m', 'new_m_ffn1_post_g': 'new_m', 'new_m_ffn1_w_gate': 'new_m', 'new_m_ffn1_w_up': 'new_m', 'new_m_ffn1_w_down': 'new_m', 'new_m_mix_pre_g': 'new_m', 'new_m_mix_post_g': 'new_m', 'new_m_mix_w_in': 'new_m', 'new_m_fox_f_bias': 'new_m', 'new_m_hgrn_lb_logits': 'new_m', 'new_m_hgrn_norm_g': 'new_m', 'new_m_mix_w_proj_fox': 'new_m', 'new_m_mix_w_proj_hgrn': 'new_m', 'new_m_mix_w_out': 'new_m', 'new_m_ffn2_pre_g': 'new_m', 'new_m_ffn2_post_g': 'new_m', 'new_m_ffn2_w_gate': 'new_m', 'new_m_ffn2_w_up': 'new_m', 'new_m_ffn2_w_down': 'new_m', 'new_m_ple_pre_g': 'new_m', 'new_m_ple_post_g': 'new_m', 'new_m_ple_w_gate': 'new_m', 'new_m_ple_w_proj': 'new_m', 'new_v_ffn1_pre_g': 'new_v', 'new_v_ffn1_post_g': 'new_v', 'new_v_ffn1_w_gate': 'new_v', 'new_v_ffn1_w_up': 'new_v', 'new_v_ffn1_w_down': 'new_v', 'new_v_mix_pre_g': 'new_v', 'new_v_mix_post_g': 'new_v', 'new_v_mix_w_in': 'new_v', 'new_v_fox_f_bias': 'new_v', 'new_v_hgrn_lb_logits': 'new_v', 'new_v_hgrn_norm_g': 'new_v', 'new_v_mix_w_proj_fox': 'new_v', 'new_v_mix_w_proj_hgrn': 'new_v', 'new_v_mix_w_out': 'new_v', 'new_v_ffn2_pre_g': 'new_v', 'new_v_ffn2_post_g': 'new_v', 'new_v_ffn2_w_gate': 'new_v', 'new_v_ffn2_w_up': 'new_v', 'new_v_ffn2_w_down': 'new_v', 'new_v_ple_pre_g': 'new_v', 'new_v_ple_post_g': 'new_v', 'new_v_ple_w_gate': 'new_v', 'new_v_ple_w_proj': 'new_v'}


def _forward(args):
    return _fwd_reference(*[args[k] for k in FWD_PARAMS])


def _output_shape():
    out = _jax.eval_shape(lambda: _forward(_fwd_setup_inputs(0)))
    return out.shape, out.dtype

N_MICROBATCH = 1
ADAM_LR = 0.001
ADAM_B1 = 0.9
ADAM_B2 = 0.999
ADAM_EPS = 1e-08
ADAM_WD = 0.01
ADAM_STEP = 10
PER_EXAMPLE_BATCH_AXIS = {'x': 0, 'p': 1, 'loss_target': 0}
SHARED_INPUTS = []
_WEIGHT_DTYPES = {'ffn1_pre_g': _jnp.float32, 'ffn1_post_g': _jnp.float32, 'ffn1_w_gate': _jnp.float32, 'ffn1_w_up': _jnp.float32, 'ffn1_w_down': _jnp.float32, 'mix_pre_g': _jnp.float32, 'mix_post_g': _jnp.float32, 'mix_w_in': _jnp.float32, 'fox_f_bias': _jnp.float32, 'hgrn_lb_logits': _jnp.float32, 'hgrn_norm_g': _jnp.float32, 'mix_w_proj_fox': _jnp.float32, 'mix_w_proj_hgrn': _jnp.float32, 'mix_w_out': _jnp.float32, 'ffn2_pre_g': _jnp.float32, 'ffn2_post_g': _jnp.float32, 'ffn2_w_gate': _jnp.float32, 'ffn2_w_up': _jnp.float32, 'ffn2_w_down': _jnp.float32, 'ple_pre_g': _jnp.float32, 'ple_post_g': _jnp.float32, 'ple_w_gate': _jnp.float32, 'ple_w_proj': _jnp.float32}
MOMENT_SCALE = {'ffn1_pre_g': 1.761723e-01, 'ffn1_post_g': 1.980191e+00, 'ffn1_w_gate': 7.087049e-02, 'ffn1_w_up': 7.022546e-02, 'ffn1_w_down': 1.168173e-01, 'mix_pre_g': 1.999790e-01, 'mix_post_g': 8.005690e+00, 'mix_w_in': 8.779651e-02, 'fox_f_bias': 9.325291e-01, 'hgrn_lb_logits': 1.240989e-02, 'hgrn_norm_g': 3.793640e-01, 'mix_w_proj_fox': 1.231231e-01, 'mix_w_proj_hgrn': 9.411231e-02, 'mix_w_out': 1.555293e-01, 'ffn2_pre_g': 1.045356e-01, 'ffn2_post_g': 1.992675e+00, 'ffn2_w_gate': 4.122785e-02, 'ffn2_w_up': 4.624052e-02, 'ffn2_w_down': 7.665011e-02, 'ple_pre_g': 4.611802e-02, 'ple_post_g': 8.021379e+00, 'ple_w_gate': 4.576865e-02, 'ple_w_proj': 1.178375e-01}


def _to_microbatches(a, axis):
    t = _jnp.moveaxis(a, axis, 0)
    t = t.reshape((N_MICROBATCH, t.shape[0] // N_MICROBATCH) + t.shape[1:])
    return _jnp.moveaxis(t, 1, axis + 1)


def setup_inputs(seed: int = 0) -> dict:
    inp = _fwd_setup_inputs(seed)
    key = _jax.random.fold_in(_jax.random.key(seed), 7919)
    shape, _ = _output_shape()
    out = dict(inp)
    out["loss_target"] = _jax.random.normal(_jax.random.fold_in(key, 0), shape, _jnp.float32)
    for i, name in enumerate(TWIN_WEIGHTS):
        w = inp[name].astype(_jnp.float32)
        if MOMENT_SCALE is None:
            s = _jnp.sqrt(_jnp.mean(_jnp.square(w)) + 1e-30)
        else:
            s = MOMENT_SCALE[name]
        km, kv = _jax.random.split(_jax.random.fold_in(key, i + 1))
        out[name] = w
        out["m_" + name] = s * _jax.random.normal(km, w.shape, _jnp.float32)
        out["v_" + name] = (s * s) * _jax.random.uniform(kv, w.shape, _jnp.float32, 0.5, 1.5)
    if N_MICROBATCH > 1:
        for name, axis in PER_EXAMPLE_BATCH_AXIS.items():
            out[name] = _to_microbatches(out[name], axis)
    return {'x': out['x'], 'p': out['p'], 'ffn1_pre_g': out['ffn1_pre_g'], 'ffn1_post_g': out['ffn1_post_g'], 'ffn1_w_gate': out['ffn1_w_gate'], 'ffn1_w_up': out['ffn1_w_up'], 'ffn1_w_down': out['ffn1_w_down'], 'mix_pre_g': out['mix_pre_g'], 'mix_post_g': out['mix_post_g'], 'mix_w_in': out['mix_w_in'], 'fox_f_bias': out['fox_f_bias'], 'hgrn_lb_logits': out['hgrn_lb_logits'], 'hgrn_norm_g': out['hgrn_norm_g'], 'mix_w_proj_fox': out['mix_w_proj_fox'], 'mix_w_proj_hgrn': out['mix_w_proj_hgrn'], 'mix_w_out': out['mix_w_out'], 'ffn2_pre_g': out['ffn2_pre_g'], 'ffn2_post_g': out['ffn2_post_g'], 'ffn2_w_gate': out['ffn2_w_gate'], 'ffn2_w_up': out['ffn2_w_up'], 'ffn2_w_down': out['ffn2_w_down'], 'ple_pre_g': out['ple_pre_g'], 'ple_post_g': out['ple_post_g'], 'ple_w_gate': out['ple_w_gate'], 'ple_w_proj': out['ple_w_proj'], 'loss_target': out['loss_target'], 'm_ffn1_pre_g': out['m_ffn1_pre_g'], 'm_ffn1_post_g': out['m_ffn1_post_g'], 'm_ffn1_w_gate': out['m_ffn1_w_gate'], 'm_ffn1_w_up': out['m_ffn1_w_up'], 'm_ffn1_w_down': out['m_ffn1_w_down'], 'm_mix_pre_g': out['m_mix_pre_g'], 'm_mix_post_g': out['m_mix_post_g'], 'm_mix_w_in': out['m_mix_w_in'], 'm_fox_f_bias': out['m_fox_f_bias'], 'm_hgrn_lb_logits': out['m_hgrn_lb_logits'], 'm_hgrn_norm_g': out['m_hgrn_norm_g'], 'm_mix_w_proj_fox': out['m_mix_w_proj_fox'], 'm_mix_w_proj_hgrn': out['m_mix_w_proj_hgrn'], 'm_mix_w_out': out['m_mix_w_out'], 'm_ffn2_pre_g': out['m_ffn2_pre_g'], 'm_ffn2_post_g': out['m_ffn2_post_g'], 'm_ffn2_w_gate': out['m_ffn2_w_gate'], 'm_ffn2_w_up': out['m_ffn2_w_up'], 'm_ffn2_w_down': out['m_ffn2_w_down'], 'm_ple_pre_g': out['m_ple_pre_g'], 'm_ple_post_g': out['m_ple_post_g'], 'm_ple_w_gate': out['m_ple_w_gate'], 'm_ple_w_proj': out['m_ple_w_proj'], 'v_ffn1_pre_g': out['v_ffn1_pre_g'], 'v_ffn1_post_g': out['v_ffn1_post_g'], 'v_ffn1_w_gate': out['v_ffn1_w_gate'], 'v_ffn1_w_up': out['v_ffn1_w_up'], 'v_ffn1_w_down': out['v_ffn1_w_down'], 'v_mix_pre_g': out['v_mix_pre_g'], 'v_mix_post_g': out['v_mix_post_g'], 'v_mix_w_in': out['v_mix_w_in'], 'v_fox_f_bias': out['v_fox_f_bias'], 'v_hgrn_lb_logits': out['v_hgrn_lb_logits'], 'v_hgrn_norm_g': out['v_hgrn_norm_g'], 'v_mix_w_proj_fox': out['v_mix_w_proj_fox'], 'v_mix_w_proj_hgrn': out['v_mix_w_proj_hgrn'], 'v_mix_w_out': out['v_mix_w_out'], 'v_ffn2_pre_g': out['v_ffn2_pre_g'], 'v_ffn2_post_g': out['v_ffn2_post_g'], 'v_ffn2_w_gate': out['v_ffn2_w_gate'], 'v_ffn2_w_up': out['v_ffn2_w_up'], 'v_ffn2_w_down': out['v_ffn2_w_down'], 'v_ple_pre_g': out['v_ple_pre_g'], 'v_ple_post_g': out['v_ple_post_g'], 'v_ple_w_gate': out['v_ple_w_gate'], 'v_ple_w_proj': out['v_ple_w_proj']}


def _loss(weights, diff, rest, loss_target):
    with _jax.named_scope("forward"):
        args = {**rest, TWIN_DIFF_INPUT: diff, **{k: w.astype(_WEIGHT_DTYPES[k]) for k, w in weights.items()}}
        y = _forward(args)
    with _jax.named_scope("loss_head"):
        err = _jnp.square(y.astype(_jnp.float32) - loss_target)
        return 0.5 * _jnp.sum(_jnp.mean(err, axis=-1)) if err.ndim else 0.5 * err


def _adamw(w, g, m, v):
    m = ADAM_B1 * m + (1.0 - ADAM_B1) * g
    v = ADAM_B2 * v + (1.0 - ADAM_B2) * _jnp.square(g)
    m_hat = m / (1.0 - ADAM_B1 ** ADAM_STEP)
    v_hat = v / (1.0 - ADAM_B2 ** ADAM_STEP)
    delta = -ADAM_LR * (m_hat / (_jnp.sqrt(v_hat) + ADAM_EPS) + ADAM_WD * w)
    return delta, m, v


def reference(x, p, ffn1_pre_g, ffn1_post_g, ffn1_w_gate, ffn1_w_up, ffn1_w_down, mix_pre_g, mix_post_g, mix_w_in, fox_f_bias, hgrn_lb_logits, hgrn_norm_g, mix_w_proj_fox, mix_w_proj_hgrn, mix_w_out, ffn2_pre_g, ffn2_post_g, ffn2_w_gate, ffn2_w_up, ffn2_w_down, ple_pre_g, ple_post_g, ple_w_gate, ple_w_proj, loss_target, m_ffn1_pre_g, m_ffn1_post_g, m_ffn1_w_gate, m_ffn1_w_up, m_ffn1_w_down, m_mix_pre_g, m_mix_post_g, m_mix_w_in, m_fox_f_bias, m_hgrn_lb_logits, m_hgrn_norm_g, m_mix_w_proj_fox, m_mix_w_proj_hgrn, m_mix_w_out, m_ffn2_pre_g, m_ffn2_post_g, m_ffn2_w_gate, m_ffn2_w_up, m_ffn2_w_down, m_ple_pre_g, m_ple_post_g, m_ple_w_gate, m_ple_w_proj, v_ffn1_pre_g, v_ffn1_post_g, v_ffn1_w_gate, v_ffn1_w_up, v_ffn1_w_down, v_mix_pre_g, v_mix_post_g, v_mix_w_in, v_fox_f_bias, v_hgrn_lb_logits, v_hgrn_norm_g, v_mix_w_proj_fox, v_mix_w_proj_hgrn, v_mix_w_out, v_ffn2_pre_g, v_ffn2_post_g, v_ffn2_w_gate, v_ffn2_w_up, v_ffn2_w_down, v_ple_pre_g, v_ple_post_g, v_ple_w_gate, v_ple_w_proj):
    given = dict(x=x, p=p, ffn1_pre_g=ffn1_pre_g, ffn1_post_g=ffn1_post_g, ffn1_w_gate=ffn1_w_gate, ffn1_w_up=ffn1_w_up, ffn1_w_down=ffn1_w_down, mix_pre_g=mix_pre_g, mix_post_g=mix_post_g, mix_w_in=mix_w_in, fox_f_bias=fox_f_bias, hgrn_lb_logits=hgrn_lb_logits, hgrn_norm_g=hgrn_norm_g, mix_w_proj_fox=mix_w_proj_fox, mix_w_proj_hgrn=mix_w_proj_hgrn, mix_w_out=mix_w_out, ffn2_pre_g=ffn2_pre_g, ffn2_post_g=ffn2_post_g, ffn2_w_gate=ffn2_w_gate, ffn2_w_up=ffn2_w_up, ffn2_w_down=ffn2_w_down, ple_pre_g=ple_pre_g, ple_post_g=ple_post_g, ple_w_gate=ple_w_gate, ple_w_proj=ple_w_proj, loss_target=loss_target, m_ffn1_pre_g=m_ffn1_pre_g, m_ffn1_post_g=m_ffn1_post_g, m_ffn1_w_gate=m_ffn1_w_gate, m_ffn1_w_up=m_ffn1_w_up, m_ffn1_w_down=m_ffn1_w_down, m_mix_pre_g=m_mix_pre_g, m_mix_post_g=m_mix_post_g, m_mix_w_in=m_mix_w_in, m_fox_f_bias=m_fox_f_bias, m_hgrn_lb_logits=m_hgrn_lb_logits, m_hgrn_norm_g=m_hgrn_norm_g, m_mix_w_proj_fox=m_mix_w_proj_fox, m_mix_w_proj_hgrn=m_mix_w_proj_hgrn, m_mix_w_out=m_mix_w_out, m_ffn2_pre_g=m_ffn2_pre_g, m_ffn2_post_g=m_ffn2_post_g, m_ffn2_w_gate=m_ffn2_w_gate, m_ffn2_w_up=m_ffn2_w_up, m_ffn2_w_down=m_ffn2_w_down, m_ple_pre_g=m_ple_pre_g, m_ple_post_g=m_ple_post_g, m_ple_w_gate=m_ple_w_gate, m_ple_w_proj=m_ple_w_proj, v_ffn1_pre_g=v_ffn1_pre_g, v_ffn1_post_g=v_ffn1_post_g, v_ffn1_w_gate=v_ffn1_w_gate, v_ffn1_w_up=v_ffn1_w_up, v_ffn1_w_down=v_ffn1_w_down, v_mix_pre_g=v_mix_pre_g, v_mix_post_g=v_mix_post_g, v_mix_w_in=v_mix_w_in, v_fox_f_bias=v_fox_f_bias, v_hgrn_lb_logits=v_hgrn_lb_logits, v_hgrn_norm_g=v_hgrn_norm_g, v_mix_w_proj_fox=v_mix_w_proj_fox, v_mix_w_proj_hgrn=v_mix_w_proj_hgrn, v_mix_w_out=v_mix_w_out, v_ffn2_pre_g=v_ffn2_pre_g, v_ffn2_post_g=v_ffn2_post_g, v_ffn2_w_gate=v_ffn2_w_gate, v_ffn2_w_up=v_ffn2_w_up, v_ffn2_w_down=v_ffn2_w_down, v_ple_pre_g=v_ple_pre_g, v_ple_post_g=v_ple_post_g, v_ple_w_gate=v_ple_w_gate, v_ple_w_proj=v_ple_w_proj)
    weights = {n: given[n] for n in TWIN_WEIGHTS}
    shared = {n: given[n] for n in SHARED_INPUTS}
    per_example = {n: given[n] for n in ['x', 'p']}
    grad_fn = _jax.value_and_grad(_loss, argnums=(0, 1))

    def one_microbatch(ex, loss_target):
        ex = dict(ex)
        diff = ex.pop(TWIN_DIFF_INPUT)
        return grad_fn(weights, diff, {**shared, **ex}, loss_target)

    if N_MICROBATCH == 1:
        loss, (grad_w, grad_x) = one_microbatch(per_example, given["loss_target"])
    else:
        def body(carry, xs):
            loss_sum, grad_sum = carry
            l_k, (gw_k, gx_k) = one_microbatch(xs[0], xs[1])
            with _jax.named_scope("update"):
                return (loss_sum + l_k, _jax.tree.map(_jnp.add, grad_sum, gw_k)), gx_k

        init = (_jnp.zeros((), _jnp.float32), _jax.tree.map(_jnp.zeros_like, weights))
        (loss, grad_w), grad_x = _jax.lax.scan(body, init, (per_example, given["loss_target"]))
    with _jax.named_scope("update"):
        delta_w, new_m, new_v = {}, {}, {}
        for n in TWIN_WEIGHTS:
            delta_w[n], new_m[n], new_v[n] = _adamw(weights[n], grad_w[n], given["m_" + n], given["v_" + n])
    return (loss, grad_x, *[grad_w[n] for n in TWIN_WEIGHTS], *[delta_w[n] for n in TWIN_WEIGHTS],
            *[new_m[n] for n in TWIN_WEIGHTS], *[new_v[n] for n in TWIN_WEIGHTS])
```

```python
import functools

import jax
import jax.numpy as jnp
from jax import lax
from jax.experimental import pallas as pl
from jax.experimental.pallas import tpu as pltpu

F32 = jnp.float32
BF16 = jnp.bfloat16

D_MODEL = 2048
SEQ = 2048
D_FF = 5632
PLE_DIM = 256
HEADS = 8
HEAD_DIM = 128
WIDTH = HEADS * HEAD_DIM
CHUNK = 64
SUB = 16
NORM_EPS = 1e-6
MACARON_SCALE = 0.5
N_CHIPS = 4

ADAM_LR = 0.001
ADAM_B1 = 0.9
ADAM_B2 = 0.999
ADAM_EPS = 1e-08
ADAM_WD = 0.01
ADAM_STEP = 10

LANES = 128
VMEM_LIMIT = 56 * 1024 * 1024
NEG_BIG = -1e30
MESH = pl.DeviceIdType.MESH


def _pick(n, cands):
    for c in cands:
        if c <= n and n % c == 0:
            return c
    return n


def _params(sem, vmem=VMEM_LIMIT):
    return pltpu.CompilerParams(dimension_semantics=sem, vmem_limit_bytes=vmem)


def _sigmoid(x):
    return 1.0 / (1.0 + jnp.exp(-x))


def _silu(x):
    return x * _sigmoid(x)


def _silu_grad(x):
    s = _sigmoid(x)
    return s * (1.0 + x * (1.0 - s))


_DN = {"nn": (((1,), (0,)), ((), ())), "nt": (((1,), (1,)), ((), ())), "tn": (((0,), (0,)), ((), ()))}


def _mm(a, b, *, mode, grid, a_spec, b_spec, o_spec, out_shape, acc_shape, name):
    nk = grid[2]
    dn = _DN[mode]

    def body(a_ref, b_ref, o_ref, acc_ref):
        k = pl.program_id(2)

        @pl.when(k == 0)
        def _():
            acc_ref[...] = jnp.zeros_like(acc_ref)

        acc_ref[...] += lax.dot_general(a_ref[...].astype(BF16), b_ref[...].astype(BF16), dn,
                                        preferred_element_type=F32)

        @pl.when(k == nk - 1)
        def _():
            o_ref[...] = acc_ref[...].astype(o_ref.dtype)

    return pl.pallas_call(
        body, name=name, grid=grid, in_specs=[a_spec, b_spec], out_specs=o_spec, out_shape=out_shape,
        scratch_shapes=[pltpu.VMEM(acc_shape, F32)],
        compiler_params=_params(("parallel", "parallel", "arbitrary")),
    )(a, b)


def mm_nn_2d(a, b, out_dtype, name):
    m, kk = a.shape
    n = b.shape[1]
    tm, tn, tk = _pick(m, (512, 256, 128)), _pick(n, (1024, 512, 256, 128)), _pick(kk, (2048, 1408, 1024, 512, 256, 128))
    return _mm(a, b, mode="nn", grid=(m // tm, n // tn, kk // tk),
               a_spec=pl.BlockSpec((tm, tk), lambda i, j, k: (i, k)),
               b_spec=pl.BlockSpec((tk, tn), lambda i, j, k: (k, j)),
               o_spec=pl.BlockSpec((tm, tn), lambda i, j, k: (i, j)),
               out_shape=jax.ShapeDtypeStruct((m, n), out_dtype), acc_shape=(tm, tn), name=name)


def mm_nt_2d(a, b, out_dtype, name):
    m, c = a.shape
    n = b.shape[0]
    tm, tn, tk = _pick(m, (512, 256, 128)), _pick(n, (1408, 1024, 512, 256, 128)), _pick(c, (2048, 1408, 1024, 512, 256, 128))
    return _mm(a, b, mode="nt", grid=(m // tm, n // tn, c // tk),
               a_spec=pl.BlockSpec((tm, tk), lambda i, j, k: (i, k)),
               b_spec=pl.BlockSpec((tn, tk), lambda i, j, k: (j, k)),
               o_spec=pl.BlockSpec((tm, tn), lambda i, j, k: (i, j)),
               out_shape=jax.ShapeDtypeStruct((m, n), out_dtype), acc_shape=(tm, tn), name=name)


def mm_tn_2d(a, b, out_dtype, name):
    c, m = a.shape
    n = b.shape[1]
    tm, tn, tk = _pick(m, (1408, 1024, 512, 256, 128)), _pick(n, (1024, 512, 256, 128)), _pick(c, (2048, 1024, 512, 256, 128))
    return _mm(a, b, mode="tn", grid=(m // tm, n // tn, c // tk),
               a_spec=pl.BlockSpec((tk, tm), lambda i, j, k: (k, i)),
               b_spec=pl.BlockSpec((tk, tn), lambda i, j, k: (k, j)),
               o_spec=pl.BlockSpec((tm, tn), lambda i, j, k: (i, j)),
               out_shape=jax.ShapeDtypeStruct((m, n), out_dtype), acc_shape=(tm, tn), name=name)


def mm_nn_col(a, w, out_dtype, name):
    m, kk = a.shape
    g, jn, _, ns = w.shape
    tm, tk = _pick(m, (512, 256, 128)), _pick(kk, (2048, 1024, 512, 256, 128))
    return _mm(a, w, mode="nn", grid=(m // tm, g * jn, kk // tk),
               a_spec=pl.BlockSpec((tm, tk), lambda i, j, k: (i, k)),
               b_spec=pl.BlockSpec((None, None, tk, ns), lambda i, j, k: (j // jn, j % jn, k, 0)),
               o_spec=pl.BlockSpec((None, tm, ns), lambda i, j, k: (j // jn, i, j % jn)),
               out_shape=jax.ShapeDtypeStruct((g, m, jn * ns), out_dtype), acc_shape=(tm, ns), name=name)


def mm_nt_col(a, w, out_dtype, name):
    g, m, _ = a.shape
    _, jn, kk, ns = w.shape
    tm, tn = _pick(m, (512, 256, 128)), _pick(kk, (1024, 512, 256, 128))
    return _mm(a, w, mode="nt", grid=(m // tm, kk // tn, g * jn),
               a_spec=pl.BlockSpec((None, tm, ns), lambda i, j, k: (k // jn, i, k % jn)),
               b_spec=pl.BlockSpec((None, None, tn, ns), lambda i, j, k: (k // jn, k % jn, j, 0)),
               o_spec=pl.BlockSpec((tm, tn), lambda i, j, k: (i, j)),
               out_shape=jax.ShapeDtypeStruct((m, kk), out_dtype), acc_shape=(tm, tn), name=name)


def mm_tn_col(a, b, jn, out_dtype, name):
    c, kk = a.shape
    g, _, n = b.shape
    ns = n // jn
    tm, tk = _pick(kk, (512, 256, 128)), _pick(c, (2048, 1024, 512, 256, 128))
    return _mm(a, b, mode="tn", grid=(kk // tm, g * jn, c // tk),
               a_spec=pl.BlockSpec((tk, tm), lambda i, j, k: (k, i)),
               b_spec=pl.BlockSpec((None, tk, ns), lambda i, j, k: (j // jn, k, j % jn)),
               o_spec=pl.BlockSpec((None, None, tm, ns), lambda i, j, k: (j // jn, j % jn, i, 0)),
               out_shape=jax.ShapeDtypeStruct((g, jn, kk, ns), out_dtype), acc_shape=(tm, ns), name=name)


def _rstd(x):
    return lax.rsqrt(jnp.mean(x * x, axis=-1, keepdims=True) + NORM_EPS)


def _rms_bwd(x, g, dy):
    r = _rstd(x)
    xn = x * r
    dyg = dy * g
    dx = r * (dyg - xn * jnp.mean(dyg * xn, axis=-1, keepdims=True))
    return dx, jnp.sum(dy * xn, axis=0, keepdims=True)


def _row_tile(t):
    return _pick(t, (256, 128, 64, 32, 16, 8))


def norm_in(h, g, name):
    t, d = h.shape
    tr = _row_tile(t)

    def body(h_ref, g_ref, u_ref):
        x = h_ref[...]
        u_ref[...] = (x * _rstd(x) * g_ref[...]).astype(BF16)

    return pl.pallas_call(
        body, name=name, grid=(t // tr,),
        in_specs=[pl.BlockSpec((tr, d), lambda i: (i, 0)), pl.BlockSpec((1, d), lambda i: (0, 0))],
        out_specs=pl.BlockSpec((tr, d), lambda i: (i, 0)),
        out_shape=jax.ShapeDtypeStruct((t, d), BF16), compiler_params=_params(("parallel",)),
    )(h, g)


def resid_post(h, y, g, scale, name):
    t, d = h.shape
    tr = _row_tile(t)

    def body(h_ref, y_ref, g_ref, o_ref):
        yv = y_ref[...]
        o_ref[...] = h_ref[...] + scale * (yv * _rstd(yv) * g_ref[...])

    row = pl.BlockSpec((tr, d), lambda i: (i, 0))
    return pl.pallas_call(
        body, name=name, grid=(t // tr,), in_specs=[row, row, pl.BlockSpec((1, d), lambda i: (0, 0))],
        out_specs=row, out_shape=jax.ShapeDtypeStruct((t, d), F32), compiler_params=_params(("parallel",)),
    )(h, y, g)


def post_bwd(dh, y, g, scale, name):
    t, d = dh.shape
    tr = _row_tile(t)

    def body(dh_ref, y_ref, g_ref, dy_ref, dg_ref):
        @pl.when(pl.program_id(0) == 0)
        def _():
            dg_ref[...] = jnp.zeros_like(dg_ref)

        dx, dg = _rms_bwd(y_ref[...], g_ref[...], scale * dh_ref[...])
        dy_ref[...] = dx.astype(BF16)
        dg_ref[...] += dg

    row = pl.BlockSpec((tr, d), lambda i: (i, 0))
    vec = pl.BlockSpec((1, d), lambda i: (0, 0))
    return pl.pallas_call(
        body, name=name, grid=(t // tr,), in_specs=[row, row, vec], out_specs=[row, vec],
        out_shape=[jax.ShapeDtypeStruct((t, d), BF16), jax.ShapeDtypeStruct((1, d), F32)],
        compiler_params=_params(("arbitrary",)),
    )(dh, y, g)


def pre_bwd(dh, h, g, dus, name):
    t, d = dh.shape
    tr = _row_tile(t)
    n_du = len(dus)

    def body(*refs):
        dh_ref, h_ref, g_ref = refs[:3]
        du_refs = refs[3:3 + n_du]
        o_ref, dg_ref = refs[3 + n_du:]

        @pl.when(pl.program_id(0) == 0)
        def _():
            dg_ref[...] = jnp.zeros_like(dg_ref)

        du = du_refs[0][...]
        for r in du_refs[1:]:
            du = du + r[...]
        dx, dg = _rms_bwd(h_ref[...], g_ref[...], du)
        o_ref[...] = dh_ref[...] + dx
        dg_ref[...] += dg

    row = pl.BlockSpec((tr, d), lambda i: (i, 0))
    vec = pl.BlockSpec((1, d), lambda i: (0, 0))
    return pl.pallas_call(
        body, name=name, grid=(t // tr,), in_specs=[row, row, vec] + [row] * n_du, out_specs=[row, vec],
        out_shape=[jax.ShapeDtypeStruct((t, d), F32), jax.ShapeDtypeStruct((1, d), F32)],
        compiler_params=_params(("arbitrary",)),
    )(dh, h, g, *dus)


def _ew_tiles(t, f):
    return _pick(t, (256, 128, 64, 32, 16, 8)), _pick(f, (1408, 1024, 512, 256, 128))


def swiglu_act(gu, name):
    _, t, f = gu.shape
    tr, tc = _ew_tiles(t, f)

    def body(gu_ref, o_ref):
        o_ref[...] = (_silu(gu_ref[0]) * gu_ref[1]).astype(BF16)

    return pl.pallas_call(
        body, name=name, grid=(t // tr, f // tc),
        in_specs=[pl.BlockSpec((2, tr, tc), lambda i, j: (0, i, j))],
        out_specs=pl.BlockSpec((tr, tc), lambda i, j: (i, j)),
        out_shape=jax.ShapeDtypeStruct((t, f), BF16), compiler_params=_params(("parallel", "parallel")),
    )(gu)


def swiglu_bwd(dact, gu, name):
    _, t, f = gu.shape
    tr, tc = _ew_tiles(t, f)

    def body(da_ref, gu_ref, o_ref):
        da = da_ref[...]
        gate = gu_ref[0]
        o_ref[0] = (da * gu_ref[1] * _silu_grad(gate)).astype(BF16)
        o_ref[1] = (da * _silu(gate)).astype(BF16)

    return pl.pallas_call(
        body, name=name, grid=(t // tr, f // tc),
        in_specs=[pl.BlockSpec((tr, tc), lambda i, j: (i, j)), pl.BlockSpec((2, tr, tc), lambda i, j: (0, i, j))],
        out_specs=pl.BlockSpec((2, tr, tc), lambda i, j: (0, i, j)),
        out_shape=jax.ShapeDtypeStruct((2, t, f), BF16), compiler_params=_params(("parallel", "parallel")),
    )(dact, gu)


def _col_blocks():
    w = WIDTH // LANES
    return dict(q_a=0, k_a=w, v_a=2 * w, q_b=3 * w, f_b=4 * w, i_b=5 * w, g_b=6 * w, gate_a=7 * w,
                gate_b=7 * w + D_MODEL // LANES)


def _tri(n, lower):
    r = lax.broadcasted_iota(jnp.int32, (n, n), 0)
    c = lax.broadcasted_iota(jnp.int32, (n, n), 1)
    return jnp.where((r >= c) if lower else (r <= c), 1.0, 0.0).astype(F32)


def _dot_hi(a, b):
    return jnp.dot(a, b, precision=lax.Precision.HIGHEST, preferred_element_type=F32)


def fox_prep(fa, bias, name):
    t, w = fa.shape
    tb = _pick(t, (256, 128, 64))

    def body(fa_ref, b_ref, c_ref, carry_ref):
        @pl.when(pl.program_id(0) == 0)
        def _():
            carry_ref[...] = jnp.zeros_like(carry_ref)

        z = fa_ref[...] + b_ref[...]
        lf = jnp.minimum(z, 0.0) - jnp.log(1.0 + jnp.exp(-jnp.abs(z)))
        c = _dot_hi(_tri(tb, True), lf) + carry_ref[...]
        c_ref[...] = c
        carry_ref[...] = carry_ref[...] + jnp.sum(lf, axis=0, keepdims=True)

    return pl.pallas_call(
        body, name=name, grid=(t // tb,),
        in_specs=[pl.BlockSpec((tb, w), lambda i: (i, 0)), pl.BlockSpec((1, w), lambda i: (0, 0))],
        out_specs=pl.BlockSpec((tb, w), lambda i: (i, 0)),
        out_shape=jax.ShapeDtypeStruct((t, w), F32), scratch_shapes=[pltpu.VMEM((1, w), F32)],
        compiler_params=_params(("arbitrary",)),
    )(fa, bias)


def fox_post_bwd(dc, fa, bias, name):
    t, w = fa.shape
    tb = _pick(t, (256, 128, 64))
    nb = t // tb

    def body(dc_ref, fa_ref, b_ref, dfa_ref, db_ref, carry_ref):
        @pl.when(pl.program_id(0) == 0)
        def _():
            carry_ref[...] = jnp.zeros_like(carry_ref)
            db_ref[...] = jnp.zeros_like(db_ref)

        dcv = dc_ref[...]
        dlf = _dot_hi(_tri(tb, False), dcv) + carry_ref[...]
        z = fa_ref[...] + b_ref[...]
        dz = dlf * _sigmoid(-z)
        dfa_ref[...] = dz.astype(BF16)
        db_ref[...] += jnp.sum(dz, axis=0, keepdims=True)
        carry_ref[...] = carry_ref[...] + jnp.sum(dcv, axis=0, keepdims=True)

    rev = pl.BlockSpec((tb, w), lambda i: (nb - 1 - i, 0))
    vec = pl.BlockSpec((1, w), lambda i: (0, 0))
    return pl.pallas_call(
        body, name=name, grid=(nb,), in_specs=[rev, rev, vec], out_specs=[rev, vec],
        out_shape=[jax.ShapeDtypeStruct((t, w), BF16), jax.ShapeDtypeStruct((1, w), F32)],
        scratch_shapes=[pltpu.VMEM((1, w), F32)], compiler_params=_params(("arbitrary",)),
    )(dc, fa, bias)


def _fox_probs(q_ref, k_ref, cc_ref, cr_ref, qi, tq, t):
    scale = HEAD_DIM ** -0.5
    s = lax.dot_general(q_ref[...].astype(BF16), k_ref[...].astype(BF16), _DN["nt"], preferred_element_type=F32)
    logits = s * scale + cc_ref[...] - cr_ref[...]
    qpos = qi * tq + lax.broadcasted_iota(jnp.int32, (tq, t), 0)
    kpos = lax.broadcasted_iota(jnp.int32, (tq, t), 1)
    logits = jnp.where(kpos <= qpos, logits, NEG_BIG)
    m = jnp.max(logits, axis=-1, keepdims=True)
    p = jnp.exp(logits - m)
    return p / jnp.sum(p, axis=-1, keepdims=True)


def fox_fwd(proj, c_col, c_row, name):
    t = proj.shape[0]
    tq = _pick(t, (256, 128))
    cb = _col_blocks()
    dh = HEAD_DIM

    def body(q_ref, k_ref, v_ref, cc_ref, cr_ref, o_ref):
        p = _fox_probs(q_ref, k_ref, cc_ref, cr_ref, pl.program_id(1), tq, t)
        o_ref[...] = jnp.dot(p.astype(BF16), v_ref[...].astype(BF16), preferred_element_type=F32).astype(BF16)

    return pl.pallas_call(
        body, name=name, grid=(HEADS, t // tq),
        in_specs=[pl.BlockSpec((tq, dh), lambda h, i: (i, cb["q_a"] + h)),
                  pl.BlockSpec((t, dh), lambda h, i: (0, cb["k_a"] + h)),
                  pl.BlockSpec((t, dh), lambda h, i: (0, cb["v_a"] + h)),
                  pl.BlockSpec((None, tq, 1), lambda h, i: (h, i, 0)),
                  pl.BlockSpec((None, 1, t), lambda h, i: (h, 0, 0))],
        out_specs=pl.BlockSpec((tq, dh), lambda h, i: (i, h)),
        out_shape=jax.ShapeDtypeStruct((t, WIDTH), BF16), compiler_params=_params(("parallel", "parallel")),
    )(proj, proj, proj, c_col, c_row)


def fox_bwd(proj, c_col, c_row, do, name):
    t = proj.shape[0]
    tq = _pick(t, (256, 128))
    cb = _col_blocks()
    dh = HEAD_DIM
    scale = HEAD_DIM ** -0.5

    def body(q_ref, k_ref, v_ref, cc_ref, cr_ref, do_ref, dq_ref, dk_ref, dv_ref, dcc_ref, dcr_ref):
        @pl.when(pl.program_id(1) == 0)
        def _():
            dk_ref[...] = jnp.zeros_like(dk_ref)
            dv_ref[...] = jnp.zeros_like(dv_ref)
            dcr_ref[...] = jnp.zeros_like(dcr_ref)

        p = _fox_probs(q_ref, k_ref, cc_ref, cr_ref, pl.program_id(1), tq, t)
        dov = do_ref[...].astype(BF16)
        kb = k_ref[...].astype(BF16)
        dv_ref[...] += lax.dot_general(p.astype(BF16), dov, _DN["tn"], preferred_element_type=F32)
        dp = lax.dot_general(dov, v_ref[...].astype(BF16), _DN["nt"], preferred_element_type=F32)
        ds = p * (dp - jnp.sum(p * dp, axis=-1, keepdims=True))
        dcc_ref[...] = jnp.sum(ds, axis=-1, keepdims=True)
        dcr_ref[...] -= jnp.sum(ds, axis=0, keepdims=True)
        dss = (ds * scale).astype(BF16)
        dq_ref[...] = jnp.dot(dss, kb, preferred_element_type=F32).astype(BF16)
        dk_ref[...] += lax.dot_general(dss, q_ref[...].astype(BF16), _DN["tn"], preferred_element_type=F32)

    return pl.pallas_call(
        body, name=name, grid=(HEADS, t // tq),
        in_specs=[pl.BlockSpec((tq, dh), lambda h, i: (i, cb["q_a"] + h)),
                  pl.BlockSpec((t, dh), lambda h, i: (0, cb["k_a"] + h)),
                  pl.BlockSpec((t, dh), lambda h, i: (0, cb["v_a"] + h)),
                  pl.BlockSpec((None, tq, 1), lambda h, i: (h, i, 0)),
                  pl.BlockSpec((None, 1, t), lambda h, i: (h, 0, 0)),
                  pl.BlockSpec((tq, dh), lambda h, i: (i, h))],
        out_specs=[pl.BlockSpec((tq, dh), lambda h, i: (i, h)),
                   pl.BlockSpec((t, dh), lambda h, i: (0, h)),
                   pl.BlockSpec((t, dh), lambda h, i: (0, h)),
                   pl.BlockSpec((None, tq, 1), lambda h, i: (h, i, 0)),
                   pl.BlockSpec((None, 1, t), lambda h, i: (h, 0, 0))],
        out_shape=[jax.ShapeDtypeStruct((t, WIDTH), BF16), jax.ShapeDtypeStruct((t, WIDTH), F32),
                   jax.ShapeDtypeStruct((t, WIDTH), F32), jax.ShapeDtypeStruct((HEADS, t, 1), F32),
                   jax.ShapeDtypeStruct((HEADS, 1, t), F32)],
        compiler_params=_params(("parallel", "arbitrary")),
    )(proj, proj, proj, c_col, c_row, do)


def _lower_bound(lg_ref):
    l0 = lg_ref[0:1, :]
    l1 = lg_ref[1:2, :]
    m = jnp.maximum(l0, l1)
    e0 = jnp.exp(l0 - m)
    e1 = jnp.exp(l1 - m)
    return e0 / (e0 + e1)


def _hgrn_inputs(qb_ref, fb_ref, lg_ref, q_s, k_s, cum_s):
    lb = _lower_bound(lg_ref)
    sig = _sigmoid(fb_ref[...])
    f = lb + (1.0 - lb) * sig
    q_s[...] = _silu(qb_ref[...])
    k_s[...] = 1.0 - f
    cum_s[...] = _dot_hi(_tri(CHUNK, True), jnp.log(f))
    return lb, sig, f


def _boundary(cum_s, a):
    if a == 0:
        return jnp.zeros((1, HEAD_DIM), F32)
    return cum_s[pl.ds(SUB * a - 1, 1), :]


def _hgrn_scores(q_s, k_s, cum_s):
    cum = cum_s[...]
    kk = k_s[...]
    lane = lax.broadcasted_iota(jnp.int32, (SUB, CHUNK), 1)
    row = lax.broadcasted_iota(jnp.int32, (SUB, 1), 0)
    blocks = []
    for a in range(CHUNK // SUB):
        rows = pl.ds(SUB * a, SUB)
        ca = _boundary(cum_s, a)
        cum_a = cum_s[rows, :]
        q_a = q_s[rows, :]
        qa = q_a * jnp.exp(cum_a - ca)
        ka = kk * jnp.exp(jnp.minimum(ca - cum, 0.0))
        blk = lax.dot_general(qa, ka, _DN["nt"], preferred_element_type=F32)
        blk = jnp.where(lane < SUB * a, blk, 0.0)
        for s in range(SUB):
            r = SUB * a + s
            e = jnp.exp(jnp.minimum(cum_a - cum_s[pl.ds(r, 1), :], 0.0))
            col = jnp.sum(q_a * k_s[pl.ds(r, 1), :] * e, axis=-1, keepdims=True)
            col = jnp.where(row >= s, col, 0.0)
            blk = jnp.where(lane == r, col, blk)
        blocks.append(blk)
    return jnp.concatenate(blocks, axis=0)


def hgrn_fwd(proj, lb_logits, name):
    t = proj.shape[0]
    n = t // CHUNK
    cb = _col_blocks()
    dh = HEAD_DIM

    def body(qb_ref, fb_ref, ib_ref, lg_ref, o_ref, st_ref, state, q_s, k_s, cum_s):
        @pl.when(pl.program_id(1) == 0)
        def _():
            state[...] = jnp.zeros_like(state)

        _hgrn_inputs(qb_ref, fb_ref, lg_ref, q_s, k_s, cum_s)
        st = state[...]
        st_ref[...] = st
        cum = cum_s[...]
        v = ib_ref[...]
        qe = q_s[...] * jnp.exp(cum)
        inter = lax.dot_general(qe, st, _DN["nt"], preferred_element_type=F32)
        a_mat = _hgrn_scores(q_s, k_s, cum_s)
        o_ref[...] = inter + jnp.dot(a_mat, v, preferred_element_type=F32)
        last = cum_s[pl.ds(CHUNK - 1, 1), :]
        kd = k_s[...] * jnp.exp(last - cum)
        state[...] = st * jnp.exp(last) + lax.dot_general(v, kd, _DN["tn"], preferred_element_type=F32)

    blk = lambda off: pl.BlockSpec((CHUNK, dh), lambda h, i: (i, off + h))
    return pl.pallas_call(
        body, name=name, grid=(HEADS, n),
        in_specs=[blk(cb["q_b"]), blk(cb["f_b"]), blk(cb["i_b"]), pl.BlockSpec((2, dh), lambda h, i: (0, h))],
        out_specs=[pl.BlockSpec((CHUNK, dh), lambda h, i: (i, h)),
                   pl.BlockSpec((None, None, dh, dh), lambda h, i: (h, i, 0, 0))],
        out_shape=[jax.ShapeDtypeStruct((t, WIDTH), F32), jax.ShapeDtypeStruct((HEADS, n, dh, dh), F32)],
        scratch_shapes=[pltpu.VMEM((dh, dh), F32)] + [pltpu.VMEM((CHUNK, dh), F32)] * 3,
        compiler_params=_params(("parallel", "arbitrary")),
    )(proj, proj, proj, lb_logits)


def hgrn_bwd(proj, lb_logits, states, do, name):
    t = proj.shape[0]
    n = t // CHUNK
    cb = _col_blocks()
    dh = HEAD_DIM
    nsub = CHUNK // SUB

    def body(qb_ref, fb_ref, ib_ref, lg_ref, st_ref, do_ref, dqb_ref, dfb_ref, dib_ref, dlb_ref,
             dstate, q_s, k_s, cum_s, da_s, dq_s, dk_s):
        @pl.when(pl.program_id(1) == 0)
        def _():
            dstate[...] = jnp.zeros_like(dstate)
            dlb_ref[...] = jnp.zeros_like(dlb_ref)

        lb, sig, f = _hgrn_inputs(qb_ref, fb_ref, lg_ref, q_s, k_s, cum_s)
        st = st_ref[...]
        dst = dstate[...]
        cum = cum_s[...]
        q = q_s[...]
        kk = k_s[...]
        v = ib_ref[...]
        dov = do_ref[...]
        e_cum = jnp.exp(cum)
        qe = q * e_cum
        last = cum_s[pl.ds(CHUNK - 1, 1), :]
        e_last = jnp.exp(last)
        e_tail = jnp.exp(last - cum)
        kd = kk * e_tail

        a_mat = _hgrn_scores(q_s, k_s, cum_s)
        tri = _tri(CHUNK, True)
        da_s[...] = lax.dot_general(dov, v, _DN["nt"], preferred_element_type=F32) * tri
        dv = (lax.dot_general(a_mat, dov, _DN["tn"], preferred_element_type=F32)
              + lax.dot_general(kd, dst, _DN["nt"], preferred_element_type=F32))
        dk_state = jnp.dot(v, dst, preferred_element_type=F32) * e_tail
        dq_inter = jnp.dot(dov, st, preferred_element_type=F32) * e_cum
        dstate[...] = dst * e_last + lax.dot_general(dov, qe, _DN["tn"], preferred_element_type=F32)

        lane = lax.broadcasted_iota(jnp.int32, (SUB, CHUNK), 1)
        row = lax.broadcasted_iota(jnp.int32, (SUB, 1), 0)
        dk_s[...] = jnp.zeros_like(dk_s)
        for a in range(nsub):
            rows = pl.ds(SUB * a, SUB)
            ca = _boundary(cum_s, a)
            cum_a = cum_s[rows, :]
            q_a = q_s[rows, :]
            ea = jnp.exp(cum_a - ca)
            eb = jnp.exp(jnp.minimum(ca - cum, 0.0))
            da_a = da_s[rows, :]
            da_off = jnp.where(lane < SUB * a, da_a, 0.0)
            dq_a = ea * jnp.dot(da_off, kk * eb, preferred_element_type=F32)
            dk_s[...] += eb * lax.dot_general(da_off, q_a * ea, _DN["tn"], preferred_element_type=F32)
            dk_rows = jnp.zeros((SUB, dh), F32)
            for s in range(SUB):
                r = SUB * a + s
                e = jnp.exp(jnp.minimum(cum_a - cum_s[pl.ds(r, 1), :], 0.0))
                dcol = jnp.sum(jnp.where(lane == r, da_a, 0.0), axis=-1, keepdims=True)
                dcol = jnp.where(row >= s, dcol, 0.0)
                w = dcol * e
                dq_a = dq_a + w * k_s[pl.ds(r, 1), :]
                dk_rows = jnp.where(row == s, jnp.sum(w * q_a, axis=0, keepdims=True), dk_rows)
            dq_s[rows, :] = dq_a
            dk_s[rows, :] += dk_rows

        dq = dq_inter + dq_s[...]
        dk = dk_s[...] + dk_state
        d_last = (jnp.sum(dst * st, axis=0, keepdims=True) * e_last
                  + jnp.sum(kk * dk_state, axis=0, keepdims=True))
        rowc = lax.broadcasted_iota(jnp.int32, (CHUNK, 1), 0)
        dcum = q * dq - kk * dk + jnp.where(rowc == CHUNK - 1, d_last, 0.0)
        dg = _dot_hi(_tri(CHUNK, False), dcum)
        df = dg / f - dk
        dqb_ref[...] = (dq * _silu_grad(qb_ref[...])).astype(BF16)
        dfb_ref[...] = (df * (1.0 - lb) * sig * (1.0 - sig)).astype(BF16)
        dib_ref[...] = dv.astype(BF16)
        dlb_ref[...] += jnp.sum(df * (1.0 - sig), axis=0, keepdims=True)

    blk = lambda off: pl.BlockSpec((CHUNK, dh), lambda h, i: (n - 1 - i, off + h))
    out_blk = pl.BlockSpec((CHUNK, dh), lambda h, i: (n - 1 - i, h))
    return pl.pallas_call(
        body, name=name, grid=(HEADS, n),
        in_specs=[blk(cb["q_b"]), blk(cb["f_b"]), blk(cb["i_b"]), pl.BlockSpec((2, dh), lambda h, i: (0, h)),
                  pl.BlockSpec((None, None, dh, dh), lambda h, i: (h, n - 1 - i, 0, 0)), out_blk],
        out_specs=[out_blk, out_blk, out_blk, pl.BlockSpec((1, dh), lambda h, i: (0, h))],
        out_shape=[jax.ShapeDtypeStruct((t, WIDTH), BF16)] * 3 + [jax.ShapeDtypeStruct((1, WIDTH), F32)],
        scratch_shapes=[pltpu.VMEM((dh, dh), F32)] + [pltpu.VMEM((CHUNK, dh), F32)] * 3
        + [pltpu.VMEM((CHUNK, CHUNK), F32)] + [pltpu.VMEM((CHUNK, dh), F32)] * 2,
        compiler_params=_params(("parallel", "arbitrary")),
    )(proj, proj, proj, lb_logits, states, do)


def lb_bwd(dlb, lb_logits, name):
    def body(dlb_ref, lg_ref, o_ref):
        p0 = _lower_bound(lg_ref)
        d0 = dlb_ref[...] * p0 * (1.0 - p0)
        o_ref[0:1, :] = d0
        o_ref[1:2, :] = -d0

    return pl.pallas_call(body, name=name, out_shape=jax.ShapeDtypeStruct(lb_logits.shape, F32))(dlb, lb_logits)


def gnorm_fwd(o_raw, proj, norm_g, name):
    t = o_raw.shape[0]
    tr = _row_tile(t)
    cb = _col_blocks()
    dh = HEAD_DIM

    def body(o_ref, gb_ref, g_ref, y_ref):
        x = o_ref[...]
        y_ref[...] = (x * _rstd(x) * g_ref[...] * _silu(gb_ref[...])).astype(BF16)

    return pl.pallas_call(
        body, name=name, grid=(t // tr, HEADS),
        in_specs=[pl.BlockSpec((tr, dh), lambda i, h: (i, h)), pl.BlockSpec((tr, dh), lambda i, h: (i, cb["g_b"] + h)),
                  pl.BlockSpec((1, dh), lambda i, h: (0, 0))],
        out_specs=pl.BlockSpec((tr, dh), lambda i, h: (i, h)),
        out_shape=jax.ShapeDtypeStruct((t, WIDTH), BF16), compiler_params=_params(("parallel", "parallel")),
    )(o_raw, proj, norm_g)


def gnorm_bwd(dy, o_raw, proj, norm_g, name):
    t = o_raw.shape[0]
    tr = _row_tile(t)
    cb = _col_blocks()
    dh = HEAD_DIM

    def body(dy_ref, o_ref, gb_ref, g_ref, do_ref, dgb_ref, dg_ref):
        @pl.when((pl.program_id(0) == 0) & (pl.program_id(1) == 0))
        def _():
            dg_ref[...] = jnp.zeros_like(dg_ref)

        x = o_ref[...]
        gb = gb_ref[...]
        dyv = dy_ref[...]
        g = g_ref[...]
        dx, dg = _rms_bwd(x, g, dyv * _silu(gb))
        do_ref[...] = dx
        dgb_ref[...] = (dyv * (x * _rstd(x) * g) * _silu_grad(gb)).astype(BF16)
        dg_ref[...] += dg

    hb = pl.BlockSpec((tr, dh), lambda i, h: (i, h))
    vec = pl.BlockSpec((1, dh), lambda i, h: (0, 0))
    return pl.pallas_call(
        body, name=name, grid=(t // tr, HEADS),
        in_specs=[hb, hb, pl.BlockSpec((tr, dh), lambda i, h: (i, cb["g_b"] + h)), vec],
        out_specs=[hb, hb, vec],
        out_shape=[jax.ShapeDtypeStruct((t, WIDTH), F32), jax.ShapeDtypeStruct((t, WIDTH), BF16),
                   jax.ShapeDtypeStruct((1, dh), F32)],
        compiler_params=_params(("arbitrary", "arbitrary")),
    )(dy, o_raw, proj, norm_g)


def merge_fwd(proj, y, name):
    _, t, d = y.shape
    tr = _row_tile(t)
    tc = _pick(d, (1024, 512, 256, 128))
    cb = _col_blocks()
    ga, gb = cb["gate_a"] * LANES // tc, cb["gate_b"] * LANES // tc

    def body(ga_ref, gb_ref, y_ref, o_ref):
        o_ref[...] = (_sigmoid(ga_ref[...]) * y_ref[0] + _sigmoid(gb_ref[...]) * y_ref[1]).astype(BF16)

    return pl.pallas_call(
        body, name=name, grid=(t // tr, d // tc),
        in_specs=[pl.BlockSpec((tr, tc), lambda i, j: (i, ga + j)), pl.BlockSpec((tr, tc), lambda i, j: (i, gb + j)),
                  pl.BlockSpec((2, tr, tc), lambda i, j: (0, i, j))],
        out_specs=pl.BlockSpec((tr, tc), lambda i, j: (i, j)),
        out_shape=jax.ShapeDtypeStruct((t, d), BF16), compiler_params=_params(("parallel", "parallel")),
    )(proj, proj, y)


def merge_bwd(dm, proj, y, name):
    _, t, d = y.shape
    tr = _row_tile(t)
    tc = _pick(d, (1024, 512, 256, 128))
    cb = _col_blocks()
    ga, gb = cb["gate_a"] * LANES // tc, cb["gate_b"] * LANES // tc

    def body(dm_ref, ga_ref, gb_ref, y_ref, dg_ref, dy_ref):
        dmv = dm_ref[...]
        for idx, g_ref in enumerate((ga_ref, gb_ref)):
            s = _sigmoid(g_ref[...])
            dg_ref[idx] = (dmv * y_ref[idx] * s * (1.0 - s)).astype(BF16)
            dy_ref[idx] = (dmv * s).astype(BF16)

    pair = pl.BlockSpec((2, tr, tc), lambda i, j: (0, i, j))
    return pl.pallas_call(
        body, name=name, grid=(t // tr, d // tc),
        in_specs=[pl.BlockSpec((tr, tc), lambda i, j: (i, j)), pl.BlockSpec((tr, tc), lambda i, j: (i, ga + j)),
                  pl.BlockSpec((tr, tc), lambda i, j: (i, gb + j)), pair],
        out_specs=[pair, pair],
        out_shape=[jax.ShapeDtypeStruct((2, t, d), BF16)] * 2, compiler_params=_params(("parallel", "parallel")),
    )(dm, proj, proj, y)


def ple_tail(h, a, b, g, target, name):
    t, d = h.shape
    tr = _row_tile(t)

    def body(h_ref, a_ref, b_ref, g_ref, t_ref, loss_ref, dh_ref, da_ref, db_ref, dg_ref):
        @pl.when(pl.program_id(0) == 0)
        def _():
            loss_ref[...] = jnp.zeros_like(loss_ref)
            dg_ref[...] = jnp.zeros_like(dg_ref)

        s = _sigmoid(a_ref[...])
        bv = b_ref[...]
        z = s * bv
        gv = g_ref[...]
        err = h_ref[...] + z * _rstd(z) * gv - t_ref[...]
        loss_ref[...] += 0.5 * jnp.sum(jnp.sum(err * err, axis=-1, keepdims=True), axis=0, keepdims=True) / d
        dh = err / d
        dh_ref[...] = dh
        dz, dg = _rms_bwd(z, gv, dh)
        da_ref[...] = (dz * bv * s * (1.0 - s)).astype(BF16)
        db_ref[...] = (dz * s).astype(BF16)
        dg_ref[...] += dg

    row = pl.BlockSpec((tr, d), lambda i: (i, 0))
    vec = pl.BlockSpec((1, d), lambda i: (0, 0))
    return pl.pallas_call(
        body, name=name, grid=(t // tr,), in_specs=[row, row, row, vec, row],
        out_specs=[pl.BlockSpec((1, 1), lambda i: (0, 0)), row, row, row, vec],
        out_shape=[jax.ShapeDtypeStruct((1, 1), F32), jax.ShapeDtypeStruct((t, d), F32),
                   jax.ShapeDtypeStruct((t, d), BF16), jax.ShapeDtypeStruct((t, d), BF16),
                   jax.ShapeDtypeStruct((1, d), F32)],
        compiler_params=_params(("arbitrary",)),
    )(h, a, b, g, target)


def _ffn_fwd(h, pre_g, post_g, w_gu, w_down, tag):
    u = norm_in(h, pre_g, tag + "_norm")
    gu = mm_nn_col(u, w_gu, F32, tag + "_gate_up")
    act = swiglu_act(gu, tag + "_act")
    y = mm_nn_2d(act, w_down, F32, tag + "_down")
    out = resid_post(h, y, post_g, MACARON_SCALE, tag + "_out")
    return out, (h, u, gu, act, y)


def _ffn_bwd(dh, saved, pre_g, post_g, w_gu, w_down, tag):
    h, u, gu, act, y = saved
    dy, d_post = post_bwd(dh, y, post_g, MACARON_SCALE, tag + "_post_bwd")
    dact = mm_nt_2d(dy, w_down, F32, tag + "_dact")
    d_down = mm_tn_2d(act, dy, F32, tag + "_dw_down")
    dgu = swiglu_bwd(dact, gu, tag + "_act_bwd")
    du = mm_nt_col(dgu, w_gu, F32, tag + "_du")
    d_gu = mm_tn_col(u, dgu, N_CHIPS, F32, tag + "_dw_gate_up")
    dh_in, d_pre = pre_bwd(dh, h, pre_g, [du], tag + "_pre_bwd")
    return dh_in, d_pre, d_post, d_gu, d_down


def _heads_col(a):
    t = a.shape[0]
    at = a[:, :HEADS].T
    return at.reshape(HEADS, t, 1), at.reshape(HEADS, 1, t)


def layer_step(x, p, target, gains, fox_bias, lb_logits, norm_g, w):
    t = x.shape[0]
    h1, s1 = _ffn_fwd(x, gains["ffn1_pre"], gains["ffn1_post"], w["gu1"], w["down1"], "ffn1")

    u2 = norm_in(h1, gains["mix_pre"], "mix_norm")
    proj = mm_nn_2d(u2, w["in_main"], F32, "mix_in")
    fa = mm_nn_2d(u2, w["in_fa"], F32, "mix_in_fa")
    c = fox_prep(fa, fox_bias, "fox_prep")
    c_col, c_row = _heads_col(c)
    o_a = fox_fwd(proj, c_col, c_row, "fox_fwd")
    o_raw, states = hgrn_fwd(proj, lb_logits, "hgrn_fwd")
    o_b = gnorm_fwd(o_raw, proj, norm_g, "hgrn_norm")
    o_ab = jnp.stack([o_a, o_b])
    y_ab = _mm_branches(o_ab, w["proj"], "mix_proj")
    merged = merge_fwd(proj, y_ab, "mix_merge")
    mo = mm_nn_2d(merged, w["out"], F32, "mix_out")
    h2 = resid_post(h1, mo, gains["mix_post"], 1.0, "mix_resid")

    h3, s3 = _ffn_fwd(h2, gains["ffn2_pre"], gains["ffn2_post"], w["gu2"], w["down2"], "ffn2")

    u4 = norm_in(h3, gains["ple_pre"], "ple_norm")
    a4 = mm_nn_2d(u4, w["ple_gate"], F32, "ple_gate")
    b4 = mm_nn_col(p, w["ple_proj"], F32, "ple_proj")[0]
    loss, dh4, da4, db4, d_ple_post = ple_tail(h3, a4, b4, gains["ple_post"], target, "ple_tail")

    du4 = mm_nt_2d(da4, w["ple_gate"], F32, "ple_du")
    d_ple_gate = mm_tn_2d(u4, da4, F32, "ple_dw_gate")
    d_ple_proj = mm_tn_col(p, db4[None], N_CHIPS, F32, "ple_dw_proj")
    dh3, d_ple_pre = pre_bwd(dh4, h3, gains["ple_pre"], [du4], "ple_pre_bwd")

    dh2, d_f2_pre, d_f2_post, d_gu2, d_down2 = _ffn_bwd(
        dh3, s3, gains["ffn2_pre"], gains["ffn2_post"], w["gu2"], w["down2"], "ffn2")

    dmo, d_mix_post = post_bwd(dh2, mo, gains["mix_post"], 1.0, "mix_post_bwd")
    dmerged = mm_nt_2d(dmo, w["out"], F32, "mix_dmerged")
    d_out = mm_tn_2d(merged, dmo, F32, "mix_dw_out")
    dgate, dy_ab = merge_bwd(dmerged, proj, y_ab, "mix_merge_bwd")
    do_ab = _mm_branches_bwd(dy_ab, w["proj"], "mix_do")
    d_proj = _mm_branches_dw(o_ab, dy_ab, "mix_dw_proj")
    do_raw, dg_b, d_norm_g = gnorm_bwd(do_ab[1], o_raw, proj, norm_g, "hgrn_norm_bwd")
    dq_b, df_b, di_b, dlb = hgrn_bwd(proj, lb_logits, states, do_raw, "hgrn_bwd")
    d_lb_logits = lb_bwd(dlb, lb_logits, "lb_bwd")
    dq_a, dk_a, dv_a, dc_col, dc_row = fox_bwd(proj, c_col, c_row, do_ab[0], "fox_bwd")
    dc = (dc_col.reshape(HEADS, t) + dc_row.reshape(HEADS, t)).T
    dc = jnp.pad(dc, ((0, 0), (0, LANES - HEADS)))
    dfa, d_fox_bias = fox_post_bwd(dc, fa, fox_bias, "fox_post_bwd")
    dproj = jnp.concatenate([dq_a, dk_a.astype(BF16), dv_a.astype(BF16), dq_b, df_b, di_b, dg_b,
                             dgate[0], dgate[1]], axis=1)
    du2a = mm_nt_2d(dproj, w["in_main"], F32, "mix_du")
    du2b = mm_nt_2d(dfa, w["in_fa"], F32, "mix_du_fa")
    d_in_main = mm_tn_2d(u2, dproj, F32, "mix_dw_in")
    d_in_fa = mm_tn_2d(u2, dfa, F32, "mix_dw_in_fa")
    dh1, d_mix_pre = pre_bwd(dh2, h1, gains["mix_pre"], [du2a, du2b], "mix_pre_bwd")

    dx, d_f1_pre, d_f1_post, d_gu1, d_down1 = _ffn_bwd(
        dh1, s1, gains["ffn1_pre"], gains["ffn1_post"], w["gu1"], w["down1"], "ffn1")

    big = dict(gu1=d_gu1, down1=d_down1, in_main=d_in_main, in_fa=d_in_fa, proj=d_proj, out=d_out,
               gu2=d_gu2, down2=d_down2, ple_gate=d_ple_gate, ple_proj=d_ple_proj)
    small = dict(ffn1_pre=d_f1_pre, ffn1_post=d_f1_post, mix_pre=d_mix_pre, mix_post=d_mix_post,
                 ffn2_pre=d_f2_pre, ffn2_post=d_f2_post, ple_pre=d_ple_pre, ple_post=d_ple_post,
                 fox_bias=d_fox_bias, lb_logits=d_lb_logits, norm_g=d_norm_g)
    return loss, dx, big, small


def _mm_branches(o_ab, w_proj, name):
    g, t, kk = o_ab.shape
    _, jn, _, ns = w_proj.shape
    tm = _pick(t, (512, 256, 128))
    return _mm(o_ab, w_proj, mode="nn", grid=(t // tm, g * jn, 1),
               a_spec=pl.BlockSpec((None, tm, kk), lambda i, j, k: (j // jn, i, 0)),
               b_spec=pl.BlockSpec((None, None, kk, ns), lambda i, j, k: (j // jn, j % jn, 0, 0)),
               o_spec=pl.BlockSpec((None, tm, ns), lambda i, j, k: (j // jn, i, j % jn)),
               out_shape=jax.ShapeDtypeStruct((g, t, jn * ns), F32), acc_shape=(tm, ns), name=name)


def _mm_branches_bwd(dy_ab, w_proj, name):
    g, t, _ = dy_ab.shape
    _, jn, kk, ns = w_proj.shape
    tm = _pick(t, (512, 256, 128))
    return _mm(dy_ab, w_proj, mode="nt", grid=(t // tm, g, jn),
               a_spec=pl.BlockSpec((None, tm, ns), lambda i, j, k: (j, i, k)),
               b_spec=pl.BlockSpec((None, None, kk, ns), lambda i, j, k: (j, k, 0, 0)),
               o_spec=pl.BlockSpec((None, tm, kk), lambda i, j, k: (j, i, 0)),
               out_shape=jax.ShapeDtypeStruct((g, t, kk), F32), acc_shape=(tm, kk), name=name)


def _mm_branches_dw(o_ab, dy_ab, name):
    g, t, kk = o_ab.shape
    d = dy_ab.shape[2]
    jn = N_CHIPS
    ns = d // jn
    return _mm(o_ab, dy_ab, mode="tn", grid=(1, g * jn, 1),
               a_spec=pl.BlockSpec((None, t, kk), lambda i, j, k: (j // jn, 0, 0)),
               b_spec=pl.BlockSpec((None, t, ns), lambda i, j, k: (j // jn, 0, j % jn)),
               o_spec=pl.BlockSpec((None, None, kk, ns), lambda i, j, k: (j // jn, j % jn, 0, 0)),
               out_shape=jax.ShapeDtypeStruct((g, jn, kk, ns), F32), acc_shape=(kk, ns), name=name)


HBM_SPEC = pl.BlockSpec(memory_space=pl.ANY)


def _place():
    x, y, c = lax.axis_index("x"), lax.axis_index("y"), lax.axis_index("c")
    chips = [(1 - x, y), (x, 1 - y), (1 - x, 1 - y)]
    return x, y, c, chips


def _half(rows, which):
    return pl.ds(which * (rows // 2), rows // 2)


def gather_weights(groups):
    flat = [(gi, g, s) for gi, grp in enumerate(groups) for g, s in enumerate(grp)]
    n_in = len(flat)
    n_out = len(groups)

    def body(*refs):
        ins = refs[:n_in]
        outs = refs[n_in:n_in + n_out]
        send_sems, recv_sems, local_sems = refs[n_in + n_out:]
        x, y, c, chips = _place()
        j_me = 2 * x + y
        sibling = (x, y, 1 - c)

        def full(i):
            gi, g, _ = flat[i]
            return outs[gi].at[g]

        def remote(i, k, src, dst, to):
            return pltpu.make_async_remote_copy(src_ref=src, dst_ref=dst, send_sem=send_sems.at[6 * i + k],
                                                recv_sem=recv_sems.at[6 * i + k], device_id=to, device_id_type=MESH)

        local, first, passed = [], [], []
        for i in range(n_in):
            rows = ins[i].shape[0]
            mine = _half(rows, c)
            cp = pltpu.make_async_copy(ins[i], full(i).at[j_me], local_sems.at[i])
            cp.start()
            local.append(cp)
            for k, chip in enumerate(chips):
                cp = remote(i, k, ins[i].at[mine], full(i).at[j_me, mine], (*chip, c))
                cp.start()
                first.append(cp)
        for i in range(n_in):
            rows = ins[i].shape[0]
            mine = _half(rows, c)
            for k, (px, py) in enumerate(chips):
                landed = full(i).at[2 * px + py, mine]
                remote(i, k, landed, landed, (px, py, c)).wait_recv()
                cp = remote(i, 3 + k, landed, landed, sibling)
                cp.start()
                passed.append(cp)
        for i in range(n_in):
            rows = ins[i].shape[0]
            other = _half(rows, 1 - c)
            for k, (px, py) in enumerate(chips):
                landed = full(i).at[2 * px + py, other]
                remote(i, 3 + k, landed, landed, sibling).wait_recv()
        for cp in first + passed:
            cp.wait_send()
        for cp in local:
            cp.wait()

    out_shape = [jax.ShapeDtypeStruct((len(grp), N_CHIPS) + grp[0].shape, grp[0].dtype) for grp in groups]
    return pl.pallas_call(
        body, name="gather_weights", in_specs=[HBM_SPEC] * n_in, out_specs=[HBM_SPEC] * n_out, out_shape=out_shape,
        scratch_shapes=[pltpu.SemaphoreType.DMA((6 * n_in,)), pltpu.SemaphoreType.DMA((6 * n_in,)),
                        pltpu.SemaphoreType.DMA((n_in,))],
    )(*[s for _, _, s in flat])


def pair_exchange(grads):
    n = len(grads)

    def body(*refs):
        ins, outs = refs[:n], refs[n:2 * n]
        send_sems, recv_sems = refs[2 * n:]
        x, y, c, _ = _place()
        copies = []
        for i in range(n):
            rows = ins[i].shape[2]
            cp = pltpu.make_async_remote_copy(
                src_ref=ins[i].at[:, :, _half(rows, 1 - c), :], dst_ref=outs[i], send_sem=send_sems.at[i],
                recv_sem=recv_sems.at[i], device_id=(x, y, 1 - c), device_id_type=MESH)
            cp.start()
            copies.append(cp)
        for cp in copies:
            cp.wait()

    out_shape = [jax.ShapeDtypeStruct(g.shape[:2] + (g.shape[2] // 2, g.shape[3]), g.dtype) for g in grads]
    return pl.pallas_call(
        body, name="pair_exchange", in_specs=[HBM_SPEC] * n, out_specs=[HBM_SPEC] * n, out_shape=out_shape,
        scratch_shapes=[pltpu.SemaphoreType.DMA((n,)), pltpu.SemaphoreType.DMA((n,))],
    )(*grads)


def chip_exchange(parts):
    n = len(parts)

    def body(*refs):
        ins, outs = refs[:n], refs[n:2 * n]
        send_sems, recv_sems = refs[2 * n:]
        x, y, c, chips = _place()
        copies = []
        for i in range(n):
            for k, (px, py) in enumerate(chips):
                cp = pltpu.make_async_remote_copy(
                    src_ref=ins[i].at[:, 2 * px + py], dst_ref=outs[i].at[k], send_sem=send_sems.at[3 * i + k],
                    recv_sem=recv_sems.at[3 * i + k], device_id=(px, py, c), device_id_type=MESH)
                cp.start()
                copies.append(cp)
        for cp in copies:
            cp.wait()

    out_shape = [jax.ShapeDtypeStruct((3, p.shape[0]) + p.shape[2:], p.dtype) for p in parts]
    return pl.pallas_call(
        body, name="chip_exchange", in_specs=[HBM_SPEC] * n, out_specs=[HBM_SPEC] * n, out_shape=out_shape,
        scratch_shapes=[pltpu.SemaphoreType.DMA((3 * n,)), pltpu.SemaphoreType.DMA((3 * n,))],
    )(*parts)


def pair_broadcast(halves):
    n = len(halves)

    def body(*refs):
        ins, outs = refs[:n], refs[n:2 * n]
        send_sems, recv_sems, local_sems = refs[2 * n:]
        x, y, c, _ = _place()
        copies, local = [], []
        for i in range(n):
            rows = outs[i].shape[1]
            mine = outs[i].at[:, _half(rows, c), :]
            lc = pltpu.make_async_copy(ins[i], mine, local_sems.at[i])
            lc.start()
            local.append(lc)
            cp = pltpu.make_async_remote_copy(src_ref=ins[i], dst_ref=mine, send_sem=send_sems.at[i],
                                              recv_sem=recv_sems.at[i], device_id=(x, y, 1 - c), device_id_type=MESH)
            cp.start()
            copies.append(cp)
        for i, cp in enumerate(copies):
            cp.wait_send()
            rows = outs[i].shape[1]
            theirs = outs[i].at[:, _half(rows, 1 - c), :]
            pltpu.make_async_remote_copy(src_ref=ins[i], dst_ref=theirs, send_sem=send_sems.at[i],
                                         recv_sem=recv_sems.at[i], device_id=(x, y, 1 - c),
                                         device_id_type=MESH).wait_recv()
        for lc in local:
            lc.wait()

    out_shape = [jax.ShapeDtypeStruct((h.shape[0], 2 * h.shape[1], h.shape[2]), h.dtype) for h in halves]
    return pl.pallas_call(
        body, name="pair_broadcast", in_specs=[HBM_SPEC] * n, out_specs=[HBM_SPEC] * n, out_shape=out_shape,
        scratch_shapes=[pltpu.SemaphoreType.DMA((n,)), pltpu.SemaphoreType.DMA((n,)), pltpu.SemaphoreType.DMA((n,))],
    )(*halves)


N_DEV = 8
SLAB_ROWS = 16


def allreduce_small(slab):
    def body(x_ref, o_ref, land, send_sems, recv_sems):
        x, y, c, _ = _place()
        me = 4 * x + 2 * y + c
        land[me] = x_ref[...]
        copies = []
        for d in range(1, N_DEV):
            to = (me + d) % N_DEV
            cp = pltpu.make_async_remote_copy(
                src_ref=x_ref, dst_ref=land.at[me], send_sem=send_sems.at[d - 1], recv_sem=recv_sems.at[me],
                device_id=(to // 4, (to // 2) % 2, to % 2), device_id_type=MESH)
            cp.start()
            copies.append(cp)
        for d in range(1, N_DEV):
            frm = (me + d) % N_DEV
            pltpu.make_async_remote_copy(
                src_ref=x_ref, dst_ref=land.at[frm], send_sem=send_sems.at[d - 1], recv_sem=recv_sems.at[frm],
                device_id=(frm // 4, (frm // 2) % 2, frm % 2), device_id_type=MESH).wait_recv()
        for cp in copies:
            cp.wait_send()
        acc = land[0]
        for s in range(1, N_DEV):
            acc = acc + land[s]
        o_ref[...] = acc

    vm = pl.BlockSpec(memory_space=pltpu.VMEM)
    return pl.pallas_call(
        body, name="allreduce_small", in_specs=[vm], out_specs=vm, out_shape=jax.ShapeDtypeStruct(slab.shape, F32),
        scratch_shapes=[pltpu.VMEM((N_DEV,) + slab.shape, F32), pltpu.SemaphoreType.DMA((N_DEV - 1,)),
                        pltpu.SemaphoreType.DMA((N_DEV,))],
    )(slab)


def _grid_spec(n_prefetch, grid, in_specs, out_specs):
    return pltpu.PrefetchScalarGridSpec(num_scalar_prefetch=n_prefetch, grid=grid, in_specs=in_specs,
                                        out_specs=out_specs)


def pair_add(core, grad, recv, name):
    g, j, h, cc = recv.shape
    tr = _pick(h, (256, 128, 64, 32, 16))
    nb = h // tr

    def body(core_ref, a_ref, b_ref, o32_ref, o16_ref):
        s = a_ref[...] + b_ref[...]
        o32_ref[...] = s
        o16_ref[...] = s.astype(BF16)

    blk = pl.BlockSpec((None, None, tr, cc), lambda a, i, core_ref: (a // j, a % j, i, 0))
    own = pl.BlockSpec((None, None, tr, cc), lambda a, i, core_ref: (a // j, a % j, core_ref[0] * nb + i, 0))
    return pl.pallas_call(
        body, name=name, grid_spec=_grid_spec(1, (g * j, nb), [own, blk], [blk, blk]),
        out_shape=[jax.ShapeDtypeStruct(recv.shape, F32), jax.ShapeDtypeStruct(recv.shape, BF16)],
        compiler_params=_params(("parallel", "parallel")),
    )(core, grad, recv)


def chip_add(chip, part, recv, name):
    g, j, h, cc = part.shape
    tr = _pick(h, (256, 128, 64, 32, 16))

    def body(chip_ref, a_ref, b_ref, o_ref):
        s = a_ref[...]
        for k in range(3):
            s = s + b_ref[k].astype(F32)
        o_ref[...] = s

    return pl.pallas_call(
        body, name=name,
        grid_spec=_grid_spec(1, (g, h // tr),
                             [pl.BlockSpec((None, None, tr, cc), lambda a, i, chip_ref: (a, chip_ref[0], i, 0)),
                              pl.BlockSpec((3, None, tr, cc), lambda a, i, chip_ref: (0, a, i, 0))],
                             pl.BlockSpec((None, tr, cc), lambda a, i, chip_ref: (a, i, 0))),
        out_shape=jax.ShapeDtypeStruct((g, h, cc), F32), compiler_params=_params(("parallel", "parallel")),
    )(chip, part, recv)


def _adam_math(w, g, m, v):
    m2 = ADAM_B1 * m + (1.0 - ADAM_B1) * g
    v2 = ADAM_B2 * v + (1.0 - ADAM_B2) * (g * g)
    m_hat = m2 / (1.0 - ADAM_B1 ** ADAM_STEP)
    v_hat = v2 / (1.0 - ADAM_B2 ** ADAM_STEP)
    delta = -ADAM_LR * (m_hat / (jnp.sqrt(v_hat) + ADAM_EPS) + ADAM_WD * w)
    return delta, m2, v2


def adamw(grad, idx, w, m, v, name):
    _, r, cc = w.shape
    tr = _pick(r, (128, 64, 32, 16, 8))

    def body(g_ref, w_ref, m_ref, v_ref, go_ref, d_ref, mo_ref, vo_ref):
        g = g_ref[...]
        delta, m2, v2 = _adam_math(w_ref[...], g, m_ref[...], v_ref[...])
        go_ref[...] = g
        d_ref[...] = delta
        mo_ref[...] = m2
        vo_ref[...] = v2

    blk = pl.BlockSpec((None, tr, cc), lambda i: (0, i, 0))
    return pl.pallas_call(
        body, name=name, grid=(r // tr,),
        in_specs=[pl.BlockSpec((None, tr, cc), lambda i: (idx, i, 0)), blk, blk, blk], out_specs=[blk] * 4,
        out_shape=[jax.ShapeDtypeStruct(w.shape, F32)] * 4, compiler_params=_params(("parallel",)),
    )(grad, w, m, v)


def adamw_small(g, w, m, v):
    def body(g_ref, w_ref, m_ref, v_ref, d_ref, mo_ref, vo_ref):
        delta, m2, v2 = _adam_math(w_ref[...], g_ref[...], m_ref[...], v_ref[...])
        d_ref[...] = delta
        mo_ref[...] = m2
        vo_ref[...] = v2

    return pl.pallas_call(body, name="adamw_small", out_shape=[jax.ShapeDtypeStruct(w.shape, F32)] * 3)(g, w, m, v)


GAINS = ("ffn1_pre", "ffn1_post", "mix_pre", "mix_post", "ffn2_pre", "ffn2_post", "ple_pre", "ple_post")
WEIGHTS = ("ffn1_pre_g", "ffn1_post_g", "ffn1_w_gate", "ffn1_w_up", "ffn1_w_down", "mix_pre_g", "mix_post_g",
           "mix_w_in", "fox_f_bias", "hgrn_lb_logits", "hgrn_norm_g", "mix_w_proj_fox", "mix_w_proj_hgrn",
           "mix_w_out", "ffn2_pre_g", "ffn2_post_g", "ffn2_w_gate", "ffn2_w_up", "ffn2_w_down", "ple_pre_g",
           "ple_post_g", "ple_w_gate", "ple_w_proj")
GROUPS = (("gu1", ("ffn1_w_gate", "ffn1_w_up")), ("down1", ("ffn1_w_down",)), ("win", ("mix_w_in",)),
          ("proj", ("mix_w_proj_fox", "mix_w_proj_hgrn")), ("out", ("mix_w_out",)),
          ("gu2", ("ffn2_w_gate", "ffn2_w_up")), ("down2", ("ffn2_w_down",)), ("ple_gate", ("ple_w_gate",)),
          ("ple_proj", ("ple_w_proj",)))


def _pad_row(a, width):
    a = a.reshape(1, -1)
    return jnp.pad(a, ((0, 0), (0, width - a.shape[1])))


def _pack_small(vals):
    d = D_MODEL
    rows = [vals[n + "_g"].reshape(1, d) for n in GAINS]
    rows.append(_pad_row(vals["fox_f_bias"], d))
    lg = vals["hgrn_lb_logits"]
    rows += [_pad_row(lg[0], d), _pad_row(lg[1], d), _pad_row(vals["hgrn_norm_g"], d)]
    slab = jnp.concatenate(rows, axis=0)
    return jnp.pad(slab, ((0, SLAB_ROWS - slab.shape[0]), (0, 0)))


def _unpack_small(slab):
    out = {n + "_g": slab[i:i + 1] for i, n in enumerate(GAINS)}
    out["fox_f_bias"] = slab[8:9, :HEADS]
    out["hgrn_lb_logits"] = slab[9:11, :WIDTH]
    out["hgrn_norm_g"] = slab[11:12, :HEAD_DIM]
    return out


def _split_in(win):
    lo = 3 * WIDTH
    main = jnp.concatenate([win[:, :lo], win[:, lo + HEADS:]], axis=1)
    fa = jnp.pad(win[:, lo:lo + HEADS], ((0, 0), (0, LANES - HEADS)))
    return main, fa


def _join_in(main, fa):
    lo = 3 * WIDTH
    return jnp.concatenate([main[:, :lo], fa[:, :HEADS], main[:, lo:]], axis=1)


def kernel(x, p, ffn1_pre_g, ffn1_post_g, ffn1_w_gate, ffn1_w_up, ffn1_w_down, mix_pre_g, mix_post_g, mix_w_in, fox_f_bias, hgrn_lb_logits, hgrn_norm_g, mix_w_proj_fox, mix_w_proj_hgrn, mix_w_out, ffn2_pre_g, ffn2_post_g, ffn2_w_gate, ffn2_w_up, ffn2_w_down, ple_pre_g, ple_post_g, ple_w_gate, ple_w_proj, loss_target, m_ffn1_pre_g, m_ffn1_post_g, m_ffn1_w_gate, m_ffn1_w_up, m_ffn1_w_down, m_mix_pre_g, m_mix_post_g, m_mix_w_in, m_fox_f_bias, m_hgrn_lb_logits, m_hgrn_norm_g, m_mix_w_proj_fox, m_mix_w_proj_hgrn, m_mix_w_out, m_ffn2_pre_g, m_ffn2_post_g, m_ffn2_w_gate, m_ffn2_w_up, m_ffn2_w_down, m_ple_pre_g, m_ple_post_g, m_ple_w_gate, m_ple_w_proj, v_ffn1_pre_g, v_ffn1_post_g, v_ffn1_w_gate, v_ffn1_w_up, v_ffn1_w_down, v_mix_pre_g, v_mix_post_g, v_mix_w_in, v_fox_f_bias, v_hgrn_lb_logits, v_hgrn_norm_g, v_mix_w_proj_fox, v_mix_w_proj_hgrn, v_mix_w_out, v_ffn2_pre_g, v_ffn2_post_g, v_ffn2_w_gate, v_ffn2_w_up, v_ffn2_w_down, v_ple_pre_g, v_ple_post_g, v_ple_w_gate, v_ple_w_proj):
    args = dict(locals())
    wts = {n: args[n] for n in WEIGHTS}
    mom = {n: args["m_" + n] for n in WEIGHTS}
    var = {n: args["v_" + n] for n in WEIGHTS}
    d = D_MODEL

    gathered = gather_weights([[wts[n][0].astype(BF16) for n in names] for _, names in GROUPS])
    full = {name: g for (name, _), g in zip(GROUPS, gathered)}
    win = full["win"][0].transpose(1, 0, 2).reshape(d, -1)
    in_main, in_fa = _split_in(win)
    w = dict(gu1=full["gu1"], down1=full["down1"].reshape(-1, d), in_main=in_main, in_fa=in_fa, proj=full["proj"],
             out=full["out"].reshape(-1, d), gu2=full["gu2"], down2=full["down2"].reshape(-1, d),
             ple_gate=full["ple_gate"].reshape(-1, d), ple_proj=full["ple_proj"])

    gains = {n: wts[n + "_g"] for n in GAINS}
    loss, dx, big, small = layer_step(x[0], p[0, 0].astype(BF16), loss_target[0], gains,
                                      _pad_row(fox_f_bias, LANES), hgrn_lb_logits, hgrn_norm_g, w)

    d_win = _join_in(big["in_main"], big["in_fa"])
    rows4 = lambda a: a.reshape(1, N_CHIPS, a.shape[0] // N_CHIPS, a.shape[1])
    grads = [big["gu1"], rows4(big["down1"]), d_win.reshape(d, N_CHIPS, -1).transpose(1, 0, 2)[None], big["proj"],
             rows4(big["out"]), big["gu2"], rows4(big["down2"]), rows4(big["ple_gate"]), big["ple_proj"]]

    core = lax.axis_index("c").astype(jnp.int32).reshape(1)
    chip = (2 * lax.axis_index("x") + lax.axis_index("y")).astype(jnp.int32).reshape(1)
    from_sibling = pair_exchange(grads)
    sums = [pair_add(core, g, r, "pair_add_" + name) for (name, _), g, r in zip(GROUPS, grads, from_sibling)]
    from_chips = chip_exchange([s16 for _, s16 in sums])
    halves = [chip_add(chip, s32, r, "chip_add_" + name) for (name, _), (s32, _), r in zip(GROUPS, sums, from_chips)]
    reduced = pair_broadcast(halves)

    out_g, out_d, out_m, out_v = {}, {}, {}, {}
    for (_, names), red in zip(GROUPS, reduced):
        for idx, n in enumerate(names):
            out_g[n], out_d[n], out_m[n], out_v[n] = adamw(red, idx, wts[n], mom[n], var[n], "adamw_" + n)

    small_named = {n + "_g": small[n] for n in GAINS}
    small_named.update(fox_f_bias=small["fox_bias"][:, :HEADS], hgrn_lb_logits=small["lb_logits"],
                       hgrn_norm_g=small["norm_g"])
    g_small = allreduce_small(_pack_small(small_named))
    d_small, m_small, v_small = adamw_small(g_small, _pack_small(wts), _pack_small(mom), _pack_small(var))
    for dst, slab in ((out_g, g_small), (out_d, d_small), (out_m, m_small), (out_v, v_small)):
        dst.update(_unpack_small(slab))

    total = lax.psum(loss[0, 0], ("x", "y", "c"))
    return (total, dx[None], *[out_g[n] for n in WEIGHTS], *[out_d[n] for n in WEIGHTS],
            *[out_m[n] for n in WEIGHTS], *[out_v[n] for n in WEIGHTS])
```

```python
import functools

import jax
import jax.numpy as jnp
from jax import lax
from jax.experimental import pallas as pl
from jax.experimental.pallas import tpu as pltpu

F32 = jnp.float32
BF16 = jnp.bfloat16

D_MODEL = 2048
SEQ = 2048
D_FF = 5632
PLE_DIM = 256
HEADS = 8
HEAD_DIM = 128
WIDTH = HEADS * HEAD_DIM
CHUNK = 64
SUB = 16
NORM_EPS = 1e-6
MACARON_SCALE = 0.5
N_CHIPS = 4

ADAM_LR = 0.001
ADAM_B1 = 0.9
ADAM_B2 = 0.999
ADAM_EPS = 1e-08
ADAM_WD = 0.01
ADAM_STEP = 10

LANES = 128
VMEM_LIMIT = 56 * 1024 * 1024
NEG_BIG = -1e30
MESH = pl.DeviceIdType.MESH


def _pick(n, cands):
    for c in cands:
        if c <= n and n % c == 0:
            return c
    return n


def _params(sem, vmem=VMEM_LIMIT):
    return pltpu.CompilerParams(dimension_semantics=sem, vmem_limit_bytes=vmem)


def _sigmoid(x):
    return 1.0 / (1.0 + jnp.exp(-x))


def _silu(x):
    return x * _sigmoid(x)


def _silu_grad(x):
    s = _sigmoid(x)
    return s * (1.0 + x * (1.0 - s))


_DN = {"nn": (((1,), (0,)), ((), ())), "nt": (((1,), (1,)), ((), ())), "tn": (((0,), (0,)), ((), ()))}


def _mm(a, b, *, mode, grid, a_spec, b_spec, o_spec, out_shape, acc_shape, name):
    nk = grid[2]
    dn = _DN[mode]

    def body(a_ref, b_ref, o_ref, acc_ref):
        k = pl.program_id(2)

        @pl.when(k == 0)
        def _():
            acc_ref[...] = jnp.zeros_like(acc_ref)

        acc_ref[...] += lax.dot_general(a_ref[...].astype(BF16), b_ref[...].astype(BF16), dn,
                                        preferred_element_type=F32)

        @pl.when(k == nk - 1)
        def _():
            o_ref[...] = acc_ref[...].astype(o_ref.dtype)

    return pl.pallas_call(
        body, name=name, grid=grid, in_specs=[a_spec, b_spec], out_specs=o_spec, out_shape=out_shape,
        scratch_shapes=[pltpu.VMEM(acc_shape, F32)],
        compiler_params=_params(("parallel", "parallel", "arbitrary")),
    )(a, b)


def mm_nn_2d(a, b, out_dtype, name):
    m, kk = a.shape
    n = b.shape[1]
    tm, tn, tk = _pick(m, (512, 256, 128)), _pick(n, (1024, 512, 256, 128)), _pick(kk, (2048, 1408, 1024, 512, 256, 128))
    return _mm(a, b, mode="nn", grid=(m // tm, n // tn, kk // tk),
               a_spec=pl.BlockSpec((tm, tk), lambda i, j, k: (i, k)),
               b_spec=pl.BlockSpec((tk, tn), lambda i, j, k: (k, j)),
               o_spec=pl.BlockSpec((tm, tn), lambda i, j, k: (i, j)),
               out_shape=jax.ShapeDtypeStruct((m, n), out_dtype), acc_shape=(tm, tn), name=name)


def mm_nt_2d(a, b, out_dtype, name):
    m, c = a.shape
    n = b.shape[0]
    tm, tn, tk = _pick(m, (512, 256, 128)), _pick(n, (1408, 1024, 512, 256, 128)), _pick(c, (2048, 1408, 1024, 512, 256, 128))
    return _mm(a, b, mode="nt", grid=(m // tm, n // tn, c // tk),
               a_spec=pl.BlockSpec((tm, tk), lambda i, j, k: (i, k)),
               b_spec=pl.BlockSpec((tn, tk), lambda i, j, k: (j, k)),
               o_spec=pl.BlockSpec((tm, tn), lambda i, j, k: (i, j)),
               out_shape=jax.ShapeDtypeStruct((m, n), out_dtype), acc_shape=(tm, tn), name=name)


def mm_tn_2d(a, b, out_dtype, name):
    c, m = a.shape
    n = b.shape[1]
    tm, tn, tk = _pick(m, (1408, 1024, 512, 256, 128)), _pick(n, (1024, 512, 256, 128)), _pick(c, (2048, 1024, 512, 256, 128))
    return _mm(a, b, mode="tn", grid=(m // tm, n // tn, c // tk),
               a_spec=pl.BlockSpec((tk, tm), lambda i, j, k: (k, i)),
               b_spec=pl.BlockSpec((tk, tn), lambda i, j, k: (k, j)),
               o_spec=pl.BlockSpec((tm, tn), lambda i, j, k: (i, j)),
               out_shape=jax.ShapeDtypeStruct((m, n), out_dtype), acc_shape=(tm, tn), name=name)


def mm_nn_col(a, w, out_dtype, name):
    m, kk = a.shape
    g, jn, _, ns = w.shape
    tm, tk = _pick(m, (512, 256, 128)), _pick(kk, (2048, 1024, 512, 256, 128))
    return _mm(a, w, mode="nn", grid=(m // tm, g * jn, kk // tk),
               a_spec=pl.BlockSpec((tm, tk), lambda i, j, k: (i, k)),
               b_spec=pl.BlockSpec((None, None, tk, ns), lambda i, j, k: (j // jn, j % jn, k, 0)),
               o_spec=pl.BlockSpec((None, tm, ns), lambda i, j, k: (j // jn, i, j % jn)),
               out_shape=jax.ShapeDtypeStruct((g, m, jn * ns), out_dtype), acc_shape=(tm, ns), name=name)


def mm_nt_col(a, w, out_dtype, name):
    g, m, _ = a.shape
    _, jn, kk, ns = w.shape
    tm, tn = _pick(m, (512, 256, 128)), _pick(kk, (1024, 512, 256, 128))
    return _mm(a, w, mode="nt", grid=(m // tm, kk // tn, g * jn),
               a_spec=pl.BlockSpec((None, tm, ns), lambda i, j, k: (k // jn, i, k % jn)),
               b_spec=pl.BlockSpec((None, None, tn, ns), lambda i, j, k: (k // jn, k % jn, j, 0)),
               o_spec=pl.BlockSpec((tm, tn), lambda i, j, k: (i, j)),
               out_shape=jax.ShapeDtypeStruct((m, kk), out_dtype), acc_shape=(tm, tn), name=name)


def mm_tn_col(a, b, jn, out_dtype, name):
    c, kk = a.shape
    g, _, n = b.shape
    ns = n // jn
    tm, tk = _pick(kk, (512, 256, 128)), _pick(c, (2048, 1024, 512, 256, 128))
    return _mm(a, b, mode="tn", grid=(kk // tm, g * jn, c // tk),
               a_spec=pl.BlockSpec((tk, tm), lambda i, j, k: (k, i)),
               b_spec=pl.BlockSpec((None, tk, ns), lambda i, j, k: (j // jn, k, j % jn)),
               o_spec=pl.BlockSpec((None, None, tm, ns), lambda i, j, k: (j // jn, j % jn, i, 0)),
               out_shape=jax.ShapeDtypeStruct((g, jn, kk, ns), out_dtype), acc_shape=(tm, ns), name=name)


def _rstd(x):
    return lax.rsqrt(jnp.mean(x * x, axis=-1, keepdims=True) + NORM_EPS)


def _rms_bwd(x, g, dy):
    r = _rstd(x)
    xn = x * r
    dyg = dy * g
    dx = r * (dyg - xn * jnp.mean(dyg * xn, axis=-1, keepdims=True))
    return dx, jnp.sum(dy * xn, axis=0, keepdims=True)


def _row_tile(t):
    return _pick(t, (256, 128, 64, 32, 16, 8))


def norm_in(h, g, name):
    t, d = h.shape
    tr = _row_tile(t)

    def body(h_ref, g_ref, u_ref):
        x = h_ref[...]
        u_ref[...] = (x * _rstd(x) * g_ref[...]).astype(BF16)

    return pl.pallas_call(
        body, name=name, grid=(t // tr,),
        in_specs=[pl.BlockSpec((tr, d), lambda i: (i, 0)), pl.BlockSpec((1, d), lambda i: (0, 0))],
        out_specs=pl.BlockSpec((tr, d), lambda i: (i, 0)),
        out_shape=jax.ShapeDtypeStruct((t, d), BF16), compiler_params=_params(("parallel",)),
    )(h, g)


def resid_post(h, y, g, scale, name):
    t, d = h.shape
    tr = _row_tile(t)

    def body(h_ref, y_ref, g_ref, o_ref):
        yv = y_ref[...]
        o_ref[...] = h_ref[...] + scale * (yv * _rstd(yv) * g_ref[...])

    row = pl.BlockSpec((tr, d), lambda i: (i, 0))
    return pl.pallas_call(
        body, name=name, grid=(t // tr,), in_specs=[row, row, pl.BlockSpec((1, d), lambda i: (0, 0))],
        out_specs=row, out_shape=jax.ShapeDtypeStruct((t, d), F32), compiler_params=_params(("parallel",)),
    )(h, y, g)


def post_bwd(dh, y, g, scale, name):
    t, d = dh.shape
    tr = _row_tile(t)

    def body(dh_ref, y_ref, g_ref, dy_ref, dg_ref):
        @pl.when(pl.program_id(0) == 0)
        def _():
            dg_ref[...] = jnp.zeros_like(dg_ref)

        dx, dg = _rms_bwd(y_ref[...], g_ref[...], scale * dh_ref[...])
        dy_ref[...] = dx.astype(BF16)
        dg_ref[...] += dg

    row = pl.BlockSpec((tr, d), lambda i: (i, 0))
    vec = pl.BlockSpec((1, d), lambda i: (0, 0))
    return pl.pallas_call(
        body, name=name, grid=(t // tr,), in_specs=[row, row, vec], out_specs=[row, vec],
        out_shape=[jax.ShapeDtypeStruct((t, d), BF16), jax.ShapeDtypeStruct((1, d), F32)],
        compiler_params=_params(("arbitrary",)),
    )(dh, y, g)


def pre_bwd(dh, h, g, dus, name):
    t, d = dh.shape
    tr = _row_tile(t)
    n_du = len(dus)

    def body(*refs):
        dh_ref, h_ref, g_ref = refs[:3]
        du_refs = refs[3:3 + n_du]
        o_ref, dg_ref = refs[3 + n_du:]

        @pl.when(pl.program_id(0) == 0)
        def _():
            dg_ref[...] = jnp.zeros_like(dg_ref)

        du = du_refs[0][...]
        for r in du_refs[1:]:
            du = du + r[...]
        dx, dg = _rms_bwd(h_ref[...], g_ref[...], du)
        o_ref[...] = dh_ref[...] + dx
        dg_ref[...] += dg

    row = pl.BlockSpec((tr, d), lambda i: (i, 0))
    vec = pl.BlockSpec((1, d), lambda i: (0, 0))
    return pl.pallas_call(
        body, name=name, grid=(t // tr,), in_specs=[row, row, vec] + [row] * n_du, out_specs=[row, vec],
        out_shape=[jax.ShapeDtypeStruct((t, d), F32), jax.ShapeDtypeStruct((1, d), F32)],
        compiler_params=_params(("arbitrary",)),
    )(dh, h, g, *dus)


def _ew_tiles(t, f):
    return _pick(t, (256, 128, 64, 32, 16, 8)), _pick(f, (1408, 1024, 512, 256, 128))


def swiglu_act(gu, name):
    _, t, f = gu.shape
    tr, tc = _ew_tiles(t, f)

    def body(gu_ref, o_ref):
        o_ref[...] = (_silu(gu_ref[0]) * gu_ref[1]).astype(BF16)

    return pl.pallas_call(
        body, name=name, grid=(t // tr, f // tc),
        in_specs=[pl.BlockSpec((2, tr, tc), lambda i, j: (0, i, j))],
        out_specs=pl.BlockSpec((tr, tc), lambda i, j: (i, j)),
        out_shape=jax.ShapeDtypeStruct((t, f), BF16), compiler_params=_params(("parallel", "parallel")),
    )(gu)


def swiglu_bwd(dact, gu, name):
    _, t, f = gu.shape
    tr, tc = _ew_tiles(t, f)

    def body(da_ref, gu_ref, o_ref):
        da = da_ref[...]
        gate = gu_ref[0]
        o_ref[0] = (da * gu_ref[1] * _silu_grad(gate)).astype(BF16)
        o_ref[1] = (da * _silu(gate)).astype(BF16)

    return pl.pallas_call(
        body, name=name, grid=(t // tr, f // tc),
        in_specs=[pl.BlockSpec((tr, tc), lambda i, j: (i, j)), pl.BlockSpec((2, tr, tc), lambda i, j: (0, i, j))],
        out_specs=pl.BlockSpec((2, tr, tc), lambda i, j: (0, i, j)),
        out_shape=jax.ShapeDtypeStruct((2, t, f), BF16), compiler_params=_params(("parallel", "parallel")),
    )(dact, gu)


def _col_blocks():
    w = WIDTH // LANES
    return dict(q_a=0, k_a=w, v_a=2 * w, q_b=3 * w, f_b=4 * w, i_b=5 * w, g_b=6 * w, gate_a=7 * w,
                gate_b=7 * w + D_MODEL // LANES)


def _tri(n, lower):
    r = lax.broadcasted_iota(jnp.int32, (n, n), 0)
    c = lax.broadcasted_iota(jnp.int32, (n, n), 1)
    return jnp.where((r >= c) if lower else (r <= c), 1.0, 0.0).astype(F32)


def _dot_hi(a, b):
    return jnp.dot(a, b, precision=lax.Precision.HIGHEST, preferred_element_type=F32)


def fox_prep(fa, bias, name):
    t, w = fa.shape
    tb = _pick(t, (256, 128, 64))

    def body(fa_ref, b_ref, c_ref, carry_ref):
        @pl.when(pl.program_id(0) == 0)
        def _():
            carry_ref[...] = jnp.zeros_like(carry_ref)

        z = fa_ref[...] + b_ref[...]
        lf = jnp.minimum(z, 0.0) - jnp.log(1.0 + jnp.exp(-jnp.abs(z)))
        c = _dot_hi(_tri(tb, True), lf) + carry_ref[...]
        c_ref[...] = c
        carry_ref[...] = carry_ref[...] + jnp.sum(lf, axis=0, keepdims=True)

    return pl.pallas_call(
        body, name=name, grid=(t // tb,),
        in_specs=[pl.BlockSpec((tb, w), lambda i: (i, 0)), pl.BlockSpec((1, w), lambda i: (0, 0))],
        out_specs=pl.BlockSpec((tb, w), lambda i: (i, 0)),
        out_shape=jax.ShapeDtypeStruct((t, w), F32), scratch_shapes=[pltpu.VMEM((1, w), F32)],
        compiler_params=_params(("arbitrary",)),
    )(fa, bias)


def fox_post_bwd(dc, fa, bias, name):
    t, w = fa.shape
    tb = _pick(t, (256, 128, 64))
    nb = t // tb

    def body(dc_ref, fa_ref, b_ref, dfa_ref, db_ref, carry_ref):
        @pl.when(pl.program_id(0) == 0)
        def _():
            carry_ref[...] = jnp.zeros_like(carry_ref)
            db_ref[...] = jnp.zeros_like(db_ref)

        dcv = dc_ref[...]
        dlf = _dot_hi(_tri(tb, False), dcv) + carry_ref[...]
        z = fa_ref[...] + b_ref[...]
        dz = dlf * _sigmoid(-z)
        dfa_ref[...] = dz.astype(BF16)
        db_ref[...] += jnp.sum(dz, axis=0, keepdims=True)
        carry_ref[...] = carry_ref[...] + jnp.sum(dcv, axis=0, keepdims=True)

    rev = pl.BlockSpec((tb, w), lambda i: (nb - 1 - i, 0))
    vec = pl.BlockSpec((1, w), lambda i: (0, 0))
    return pl.pallas_call(
        body, name=name, grid=(nb,), in_specs=[rev, rev, vec], out_specs=[rev, vec],
        out_shape=[jax.ShapeDtypeStruct((t, w), BF16), jax.ShapeDtypeStruct((1, w), F32)],
        scratch_shapes=[pltpu.VMEM((1, w), F32)], compiler_params=_params(("arbitrary",)),
    )(dc, fa, bias)


def _fox_probs(q_ref, k_ref, cc_ref, cr_ref, qi, tq, t):
    scale = HEAD_DIM ** -0.5
    s = lax.dot_general(q_ref[...].astype(BF16), k_ref[...].astype(BF16), _DN["nt"], preferred_element_type=F32)
    logits = s * scale + cc_ref[...] - cr_ref[...]
    qpos = qi * tq + lax.broadcasted_iota(jnp.int32, (tq, t), 0)
    kpos = lax.broadcasted_iota(jnp.int32, (tq, t), 1)
    logits = jnp.where(kpos <= qpos, logits, NEG_BIG)
    m = jnp.max(logits, axis=-1, keepdims=True)
    p = jnp.exp(logits - m)
    return p / jnp.sum(p, axis=-1, keepdims=True)


def fox_fwd(proj, c_col, c_row, name):
    t = proj.shape[0]
    tq = _pick(t, (256, 128))
    cb = _col_blocks()
    dh = HEAD_DIM

    def body(q_ref, k_ref, v_ref, cc_ref, cr_ref, o_ref):
        p = _fox_probs(q_ref, k_ref, cc_ref, cr_ref, pl.program_id(1), tq, t)
        o_ref[...] = jnp.dot(p.astype(BF16), v_ref[...].astype(BF16), preferred_element_type=F32).astype(BF16)

    return pl.pallas_call(
        body, name=name, grid=(HEADS, t // tq),
        in_specs=[pl.BlockSpec((tq, dh), lambda h, i: (i, cb["q_a"] + h)),
                  pl.BlockSpec((t, dh), lambda h, i: (0, cb["k_a"] + h)),
                  pl.BlockSpec((t, dh), lambda h, i: (0, cb["v_a"] + h)),
                  pl.BlockSpec((None, tq, 1), lambda h, i: (h, i, 0)),
                  pl.BlockSpec((None, 1, t), lambda h, i: (h, 0, 0))],
        out_specs=pl.BlockSpec((tq, dh), lambda h, i: (i, h)),
        out_shape=jax.ShapeDtypeStruct((t, WIDTH), BF16), compiler_params=_params(("parallel", "parallel")),
    )(proj, proj, proj, c_col, c_row)


def fox_bwd(proj, c_col, c_row, do, name):
    t = proj.shape[0]
    tq = _pick(t, (256, 128))
    cb = _col_blocks()
    dh = HEAD_DIM
    scale = HEAD_DIM ** -0.5

    def body(q_ref, k_ref, v_ref, cc_ref, cr_ref, do_ref, dq_ref, dk_ref, dv_ref, dcc_ref, dcr_ref):
        @pl.when(pl.program_id(1) == 0)
        def _():
            dk_ref[...] = jnp.zeros_like(dk_ref)
            dv_ref[...] = jnp.zeros_like(dv_ref)
            dcr_ref[...] = jnp.zeros_like(dcr_ref)

        p = _fox_probs(q_ref, k_ref, cc_ref, cr_ref, pl.program_id(1), tq, t)
        dov = do_ref[...].astype(BF16)
        kb = k_ref[...].astype(BF16)
        dv_ref[...] += lax.dot_general(p.astype(BF16), dov, _DN["tn"], preferred_element_type=F32)
        dp = lax.dot_general(dov, v_ref[...].astype(BF16), _DN["nt"], preferred_element_type=F32)
        ds = p * (dp - jnp.sum(p * dp, axis=-1, keepdims=True))
        dcc_ref[...] = jnp.sum(ds, axis=-1, keepdims=True)
        dcr_ref[...] -= jnp.sum(ds, axis=0, keepdims=True)
        dss = (ds * scale).astype(BF16)
        dq_ref[...] = jnp.dot(dss, kb, preferred_element_type=F32).astype(BF16)
        dk_ref[...] += lax.dot_general(dss, q_ref[...].astype(BF16), _DN["tn"], preferred_element_type=F32)

    return pl.pallas_call(
        body, name=name, grid=(HEADS, t // tq),
        in_specs=[pl.BlockSpec((tq, dh), lambda h, i: (i, cb["q_a"] + h)),
                  pl.BlockSpec((t, dh), lambda h, i: (0, cb["k_a"] + h)),
                  pl.BlockSpec((t, dh), lambda h, i: (0, cb["v_a"] + h)),
                  pl.BlockSpec((None, tq, 1), lambda h, i: (h, i, 0)),
                  pl.BlockSpec((None, 1, t), lambda h, i: (h, 0, 0)),
                  pl.BlockSpec((tq, dh), lambda h, i: (i, h))],
        out_specs=[pl.BlockSpec((tq, dh), lambda h, i: (i, h)),
                   pl.BlockSpec((t, dh), lambda h, i: (0, h)),
                   pl.BlockSpec((t, dh), lambda h, i: (0, h)),
                   pl.BlockSpec((None, tq, 1), lambda h, i: (h, i, 0)),
                   pl.BlockSpec((None, 1, t), lambda h, i: (h, 0, 0))],
        out_shape=[jax.ShapeDtypeStruct((t, WIDTH), BF16), jax.ShapeDtypeStruct((t, WIDTH), F32),
                   jax.ShapeDtypeStruct((t, WIDTH), F32), jax.ShapeDtypeStruct((HEADS, t, 1), F32),
                   jax.ShapeDtypeStruct((HEADS, 1, t), F32)],
        compiler_params=_params(("parallel", "arbitrary")),
    )(proj, proj, proj, c_col, c_row, do)


def _lower_bound(lg_ref):
    l0 = lg_ref[0:1, :]
    l1 = lg_ref[1:2, :]
    m = jnp.maximum(l0, l1)
    e0 = jnp.exp(l0 - m)
    e1 = jnp.exp(l1 - m)
    return e0 / (e0 + e1)


def _hgrn_inputs(qb_ref, fb_ref, lg_ref, q_s, k_s, cum_s):
    lb = _lower_bound(lg_ref)
    sig = _sigmoid(fb_ref[...])
    f = lb + (1.0 - lb) * sig
    q_s[...] = _silu(qb_ref[...])
    k_s[...] = 1.0 - f
    cum_s[...] = _dot_hi(_tri(CHUNK, True), jnp.log(f))
    return lb, sig, f


def _boundary(cum_s, a):
    if a == 0:
        return jnp.zeros((1, HEAD_DIM), F32)
    return cum_s[pl.ds(SUB * a - 1, 1), :]


def _hgrn_scores(q_s, k_s, cum_s):
    cum = cum_s[...]
    kk = k_s[...]
    lane = lax.broadcasted_iota(jnp.int32, (SUB, CHUNK), 1)
    row = lax.broadcasted_iota(jnp.int32, (SUB, 1), 0)
    blocks = []
    for a in range(CHUNK // SUB):
        rows = pl.ds(SUB * a, SUB)
        ca = _boundary(cum_s, a)
        cum_a = cum_s[rows, :]
        q_a = q_s[rows, :]
        qa = q_a * jnp.exp(cum_a - ca)
        ka = kk * jnp.exp(jnp.minimum(ca - cum, 0.0))
        blk = lax.dot_general(qa, ka, _DN["nt"], preferred_element_type=F32)
        blk = jnp.where(lane < SUB * a, blk, 0.0)
        for s in range(SUB):
            r = SUB * a + s
            e = jnp.exp(jnp.minimum(cum_a - cum_s[pl.ds(r, 1), :], 0.0))
            col = jnp.sum(q_a * k_s[pl.ds(r, 1), :] * e, axis=-1, keepdims=True)
            col = jnp.where(row >= s, col, 0.0)
            blk = jnp.where(lane == r, col, blk)
        blocks.append(blk)
    return jnp.concatenate(blocks, axis=0)


def hgrn_fwd(proj, lb_logits, name):
    t = proj.shape[0]
    n = t // CHUNK
    cb = _col_blocks()
    dh = HEAD_DIM

    def body(qb_ref, fb_ref, ib_ref, lg_ref, o_ref, st_ref, state, q_s, k_s, cum_s):
        @pl.when(pl.program_id(1) == 0)
        def _():
            state[...] = jnp.zeros_like(state)

        _hgrn_inputs(qb_ref, fb_ref, lg_ref, q_s, k_s, cum_s)
        st = state[...]
        st_ref[...] = st
        cum = cum_s[...]
        v = ib_ref[...]
        qe = q_s[...] * jnp.exp(cum)
        inter = lax.dot_general(qe, st, _DN["nt"], preferred_element_type=F32)
        a_mat = _hgrn_scores(q_s, k_s, cum_s)
        o_ref[...] = inter + jnp.dot(a_mat, v, preferred_element_type=F32)
        last = cum_s[pl.ds(CHUNK - 1, 1), :]
        kd = k_s[...] * jnp.exp(last - cum)
        state[...] = st * jnp.exp(last) + lax.dot_general(v, kd, _DN["tn"], preferred_element_type=F32)

    blk = lambda off: pl.BlockSpec((CHUNK, dh), lambda h, i: (i, off + h))
    return pl.pallas_call(
        body, name=name, grid=(HEADS, n),
        in_specs=[blk(cb["q_b"]), blk(cb["f_b"]), blk(cb["i_b"]), pl.BlockSpec((2, dh), lambda h, i: (0, h))],
        out_specs=[pl.BlockSpec((CHUNK, dh), lambda h, i: (i, h)),
                   pl.BlockSpec((None, None, dh, dh), lambda h, i: (h, i, 0, 0))],
        out_shape=[jax.ShapeDtypeStruct((t, WIDTH), F32), jax.ShapeDtypeStruct((HEADS, n, dh, dh), F32)],
        scratch_shapes=[pltpu.VMEM((dh, dh), F32)] + [pltpu.VMEM((CHUNK, dh), F32)] * 3,
        compiler_params=_params(("parallel", "arbitrary")),
    )(proj, proj, proj, lb_logits)


def hgrn_bwd(proj, lb_logits, states, do, name):
    t = proj.shape[0]
    n = t // CHUNK
    cb = _col_blocks()
    dh = HEAD_DIM
    nsub = CHUNK // SUB

    def body(qb_ref, fb_ref, ib_ref, lg_ref, st_ref, do_ref, dqb_ref, dfb_ref, dib_ref, dlb_ref,
             dstate, q_s, k_s, cum_s, da_s, dq_s, dk_s):
        @pl.when(pl.program_id(1) == 0)
        def _():
            dstate[...] = jnp.zeros_like(dstate)
            dlb_ref[...] = jnp.zeros_like(dlb_ref)

        lb, sig, f = _hgrn_inputs(qb_ref, fb_ref, lg_ref, q_s, k_s, cum_s)
        st = st_ref[...]
        dst = dstate[...]
        cum = cum_s[...]
        q = q_s[...]
        kk = k_s[...]
        v = ib_ref[...]
        dov = do_ref[...]
        e_cum = jnp.exp(cum)
        qe = q * e_cum
        last = cum_s[pl.ds(CHUNK - 1, 1), :]
        e_last = jnp.exp(last)
        e_tail = jnp.exp(last - cum)
        kd = kk * e_tail

        a_mat = _hgrn_scores(q_s, k_s, cum_s)
        tri = _tri(CHUNK, True)
        da_s[...] = lax.dot_general(dov, v, _DN["nt"], preferred_element_type=F32) * tri
        dv = (lax.dot_general(a_mat, dov, _DN["tn"], preferred_element_type=F32)
              + lax.dot_general(kd, dst, _DN["nt"], preferred_element_type=F32))
        dk_state = jnp.dot(v, dst, preferred_element_type=F32) * e_tail
        dq_inter = jnp.dot(dov, st, preferred_element_type=F32) * e_cum
        dstate[...] = dst * e_last + lax.dot_general(dov, qe, _DN["tn"], preferred_element_type=F32)

        lane = lax.broadcasted_iota(jnp.int32, (SUB, CHUNK), 1)
        row = lax.broadcasted_iota(jnp.int32, (SUB, 1), 0)
        dk_s[...] = jnp.zeros_like(dk_s)
        for a in range(nsub):
            rows = pl.ds(SUB * a, SUB)
            ca = _boundary(cum_s, a)
            cum_a = cum_s[rows, :]
            q_a = q_s[rows, :]
            ea = jnp.exp(cum_a - ca)
            eb = jnp.exp(jnp.minimum(ca - cum, 0.0))
            da_a = da_s[rows, :]
            da_off = jnp.where(lane < SUB * a, da_a, 0.0)
            dq_a = ea * jnp.dot(da_off, kk * eb, preferred_element_type=F32)
            dk_s[...] += eb * lax.dot_general(da_off, q_a * ea, _DN["tn"], preferred_element_type=F32)
            dk_rows = jnp.zeros((SUB, dh), F32)
            for s in range(SUB):
                r = SUB * a + s
                e = jnp.exp(jnp.minimum(cum_a - cum_s[pl.ds(r, 1), :], 0.0))
                dcol = jnp.sum(jnp.where(lane == r, da_a, 0.0), axis=-1, keepdims=True)
                dcol = jnp.where(row >= s, dcol, 0.0)
                w = dcol * e
                dq_a = dq_a + w * k_s[pl.ds(r, 1), :]
                dk_rows = jnp.where(row == s, jnp.sum(w * q_a, axis=0, keepdims=True), dk_rows)
            dq_s[rows, :] = dq_a
            dk_s[rows, :] += dk_rows

        dq = dq_inter + dq_s[...]
        dk = dk_s[...] + dk_state
        d_last = (jnp.sum(dst * st, axis=0, keepdims=True) * e_last
                  + jnp.sum(kk * dk_state, axis=0, keepdims=True))
        rowc = lax.broadcasted_iota(jnp.int32, (CHUNK, 1), 0)
        dcum = q * dq - kk * dk + jnp.where(rowc == CHUNK - 1, d_last, 0.0)
        dg = _dot_hi(_tri(CHUNK, False), dcum)
        df = dg / f - dk
        dqb_ref[...] = (dq * _silu_grad(qb_ref[...])).astype(BF16)
        dfb_ref[...] = (df * (1.0 - lb) * sig * (1.0 - sig)).astype(BF16)
        dib_ref[...] = dv.astype(BF16)
        dlb_ref[...] += jnp.sum(df * (1.0 - sig), axis=0, keepdims=True)

    blk = lambda off: pl.BlockSpec((CHUNK, dh), lambda h, i: (n - 1 - i, off + h))
    out_blk = pl.BlockSpec((CHUNK, dh), lambda h, i: (n - 1 - i, h))
    return pl.pallas_call(
        body, name=name, grid=(HEADS, n),
        in_specs=[blk(cb["q_b"]), blk(cb["f_b"]), blk(cb["i_b"]), pl.BlockSpec((2, dh), lambda h, i: (0, h)),
                  pl.BlockSpec((None, None, dh, dh), lambda h, i: (h, n - 1 - i, 0, 0)), out_blk],
        out_specs=[out_blk, out_blk, out_blk, pl.BlockSpec((1, dh), lambda h, i: (0, h))],
        out_shape=[jax.ShapeDtypeStruct((t, WIDTH), BF16)] * 3 + [jax.ShapeDtypeStruct((1, WIDTH), F32)],
        scratch_shapes=[pltpu.VMEM((dh, dh), F32)] + [pltpu.VMEM((CHUNK, dh), F32)] * 3
        + [pltpu.VMEM((CHUNK, CHUNK), F32)] + [pltpu.VMEM((CHUNK, dh), F32)] * 2,
        compiler_params=_params(("parallel", "arbitrary")),
    )(proj, proj, proj, lb_logits, states, do)


def lb_bwd(dlb, lb_logits, name):
    def body(dlb_ref, lg_ref, o_ref):
        p0 = _lower_bound(lg_ref)
        d0 = dlb_ref[...] * p0 * (1.0 - p0)
        o_ref[0:1, :] = d0
        o_ref[1:2, :] = -d0

    return pl.pallas_call(body, name=name, out_shape=jax.ShapeDtypeStruct(lb_logits.shape, F32))(dlb, lb_logits)


def gnorm_fwd(o_raw, proj, norm_g, name):
    t = o_raw.shape[0]
    tr = _row_tile(t)
    cb = _col_blocks()
    dh = HEAD_DIM

    def body(o_ref, gb_ref, g_ref, y_ref):
        x = o_ref[...]
        y_ref[...] = (x * _rstd(x) * g_ref[...] * _silu(gb_ref[...])).astype(BF16)

    return pl.pallas_call(
        body, name=name, grid=(t // tr, HEADS),
        in_specs=[pl.BlockSpec((tr, dh), lambda i, h: (i, h)), pl.BlockSpec((tr, dh), lambda i, h: (i, cb["g_b"] + h)),
                  pl.BlockSpec((1, dh), lambda i, h: (0, 0))],
        out_specs=pl.BlockSpec((tr, dh), lambda i, h: (i, h)),
        out_shape=jax.ShapeDtypeStruct((t, WIDTH), BF16), compiler_params=_params(("parallel", "parallel")),
    )(o_raw, proj, norm_g)


def gnorm_bwd(dy, o_raw, proj, norm_g, name):
    t = o_raw.shape[0]
    tr = _row_tile(t)
    cb = _col_blocks()
    dh = HEAD_DIM

    def body(dy_ref, o_ref, gb_ref, g_ref, do_ref, dgb_ref, dg_ref):
        @pl.when((pl.program_id(0) == 0) & (pl.program_id(1) == 0))
        def _():
            dg_ref[...] = jnp.zeros_like(dg_ref)

        x = o_ref[...]
        gb = gb_ref[...]
        dyv = dy_ref[...]
        g = g_ref[...]
        dx, dg = _rms_bwd(x, g, dyv * _silu(gb))
        do_ref[...] = dx
        dgb_ref[...] = (dyv * (x * _rstd(x) * g) * _silu_grad(gb)).astype(BF16)
        dg_ref[...] += dg

    hb = pl.BlockSpec((tr, dh), lambda i, h: (i, h))
    vec = pl.BlockSpec((1, dh), lambda i, h: (0, 0))
    return pl.pallas_call(
        body, name=name, grid=(t // tr, HEADS),
        in_specs=[hb, hb, pl.BlockSpec((tr, dh), lambda i, h: (i, cb["g_b"] + h)), vec],
        out_specs=[hb, hb, vec],
        out_shape=[jax.ShapeDtypeStruct((t, WIDTH), F32), jax.ShapeDtypeStruct((t, WIDTH), BF16),
                   jax.ShapeDtypeStruct((1, dh), F32)],
        compiler_params=_params(("arbitrary", "arbitrary")),
    )(dy, o_raw, proj, norm_g)


def merge_fwd(proj, y, name):
    _, t, d = y.shape
    tr = _row_tile(t)
    tc = _pick(d, (1024, 512, 256, 128))
    cb = _col_blocks()
    ga, gb = cb["gate_a"] * LANES // tc, cb["gate_b"] * LANES // tc

    def body(ga_ref, gb_ref, y_ref, o_ref):
        o_ref[...] = (_sigmoid(ga_ref[...]) * y_ref[0] + _sigmoid(gb_ref[...]) * y_ref[1]).astype(BF16)

    return pl.pallas_call(
        body, name=name, grid=(t // tr, d // tc),
        in_specs=[pl.BlockSpec((tr, tc), lambda i, j: (i, ga + j)), pl.BlockSpec((tr, tc), lambda i, j: (i, gb + j)),
                  pl.BlockSpec((2, tr, tc), lambda i, j: (0, i, j))],
        out_specs=pl.BlockSpec((tr, tc), lambda i, j: (i, j)),
        out_shape=jax.ShapeDtypeStruct((t, d), BF16), compiler_params=_params(("parallel", "parallel")),
    )(proj, proj, y)


def merge_bwd(dm, proj, y, name):
    _, t, d = y.shape
    tr = _row_tile(t)
    tc = _pick(d, (1024, 512, 256, 128))
    cb = _col_blocks()
    ga, gb = cb["gate_a"] * LANES // tc, cb["gate_b"] * LANES // tc

    def body(dm_ref, ga_ref, gb_ref, y_ref, dg_ref, dy_ref):
        dmv = dm_ref[...]
        for idx, g_ref in enumerate((ga_ref, gb_ref)):
            s = _sigmoid(g_ref[...])
            dg_ref[idx] = (dmv * y_ref[idx] * s * (1.0 - s)).astype(BF16)
            dy_ref[idx] = (dmv * s).astype(BF16)

    pair = pl.BlockSpec((2, tr, tc), lambda i, j: (0, i, j))
    return pl.pallas_call(
        body, name=name, grid=(t // tr, d // tc),
        in_specs=[pl.BlockSpec((tr, tc), lambda i, j: (i, j)), pl.BlockSpec((tr, tc), lambda i, j: (i, ga + j)),
                  pl.BlockSpec((tr, tc), lambda i, j: (i, gb + j)), pair],
        out_specs=[pair, pair],
        out_shape=[jax.ShapeDtypeStruct((2, t, d), BF16)] * 2, compiler_params=_params(("parallel", "parallel")),
    )(dm, proj, proj, y)


def ple_tail(h, a, b, g, target, name):
    t, d = h.shape
    tr = _row_tile(t)

    def body(h_ref, a_ref, b_ref, g_ref, t_ref, loss_ref, dh_ref, da_ref, db_ref, dg_ref):
        @pl.when(pl.program_id(0) == 0)
        def _():
            loss_ref[...] = jnp.zeros_like(loss_ref)
            dg_ref[...] = jnp.zeros_like(dg_ref)

        s = _sigmoid(a_ref[...])
        bv = b_ref[...]
        z = s * bv
        gv = g_ref[...]
        err = h_ref[...] + z * _rstd(z) * gv - t_ref[...]
        loss_ref[...] += 0.5 * jnp.sum(jnp.sum(err * err, axis=-1, keepdims=True), axis=0, keepdims=True) / d
        dh = err / d
        dh_ref[...] = dh
        dz, dg = _rms_bwd(z, gv, dh)
        da_ref[...] = (dz * bv * s * (1.0 - s)).astype(BF16)
        db_ref[...] = (dz * s).astype(BF16)
        dg_ref[...] += dg

    row = pl.BlockSpec((tr, d), lambda i: (i, 0))
    vec = pl.BlockSpec((1, d), lambda i: (0, 0))
    return pl.pallas_call(
        body, name=name, grid=(t // tr,), in_specs=[row, row, row, vec, row],
        out_specs=[pl.BlockSpec((1, 1), lambda i: (0, 0)), row, row, row, vec],
        out_shape=[jax.ShapeDtypeStruct((1, 1), F32), jax.ShapeDtypeStruct((t, d), F32),
                   jax.ShapeDtypeStruct((t, d), BF16), jax.ShapeDtypeStruct((t, d), BF16),
                   jax.ShapeDtypeStruct((1, d), F32)],
        compiler_params=_params(("arbitrary",)),
    )(h, a, b, g, target)


def _ffn_fwd(h, pre_g, post_g, w_gu, w_down, tag):
    u = norm_in(h, pre_g, tag + "_norm")
    gu = mm_nn_col(u, w_gu, F32, tag + "_gate_up")
    act = swiglu_act(gu, tag + "_act")
    y = mm_nn_2d(act, w_down, F32, tag + "_down")
    out = resid_post(h, y, post_g, MACARON_SCALE, tag + "_out")
    return out, (h, u, gu, act, y)


def _ffn_bwd(dh, saved, pre_g, post_g, w_gu, w_down, tag):
    h, u, gu, act, y = saved
    dy, d_post = post_bwd(dh, y, post_g, MACARON_SCALE, tag + "_post_bwd")
    dact = mm_nt_2d(dy, w_down, F32, tag + "_dact")
    d_down = mm_tn_2d(act, dy, F32, tag + "_dw_down")
    dgu = swiglu_bwd(dact, gu, tag + "_act_bwd")
    du = mm_nt_col(dgu, w_gu, F32, tag + "_du")
    d_gu = mm_tn_col(u, dgu, N_CHIPS, F32, tag + "_dw_gate_up")
    dh_in, d_pre = pre_bwd(dh, h, pre_g, [du], tag + "_pre_bwd")
    return dh_in, d_pre, d_post, d_gu, d_down


def _heads_col(a):
    t = a.shape[0]
    at = a[:, :HEADS].T
    return at.reshape(HEADS, t, 1), at.reshape(HEADS, 1, t)


def layer_step(x, p, target, gains, fox_bias, lb_logits, norm_g, w):
    t = x.shape[0]
    h1, s1 = _ffn_fwd(x, gains["ffn1_pre"], gains["ffn1_post"], w["gu1"], w["down1"], "ffn1")

    u2 = norm_in(h1, gains["mix_pre"], "mix_norm")
    proj = mm_nt_2d(u2, w["in_main"], F32, "mix_in")
    fa = mm_nt_2d(u2, w["in_fa"], F32, "mix_in_fa")
    c = fox_prep(fa, fox_bias, "fox_prep")
    c_col, c_row = _heads_col(c)
    o_a = fox_fwd(proj, c_col, c_row, "fox_fwd")
    o_raw, states = hgrn_fwd(proj, lb_logits, "hgrn_fwd")
    o_b = gnorm_fwd(o_raw, proj, norm_g, "hgrn_norm")
    o_ab = jnp.stack([o_a, o_b])
    y_ab = _mm_branches(o_ab, w["proj"], "mix_proj")
    merged = merge_fwd(proj, y_ab, "mix_merge")
    mo = mm_nn_2d(merged, w["out"], F32, "mix_out")
    h2 = resid_post(h1, mo, gains["mix_post"], 1.0, "mix_resid")

    h3, s3 = _ffn_fwd(h2, gains["ffn2_pre"], gains["ffn2_post"], w["gu2"], w["down2"], "ffn2")

    u4 = norm_in(h3, gains["ple_pre"], "ple_norm")
    a4 = mm_nn_2d(u4, w["ple_gate"], F32, "ple_gate")
    b4 = mm_nn_col(p, w["ple_proj"], F32, "ple_proj")[0]
    loss, dh4, da4, db4, d_ple_post = ple_tail(h3, a4, b4, gains["ple_post"], target, "ple_tail")

    du4 = mm_nt_2d(da4, w["ple_gate"], F32, "ple_du")
    d_ple_gate = mm_tn_2d(u4, da4, F32, "ple_dw_gate")
    d_ple_proj = mm_tn_col(p, db4[None], N_CHIPS, F32, "ple_dw_proj")
    dh3, d_ple_pre = pre_bwd(dh4, h3, gains["ple_pre"], [du4], "ple_pre_bwd")

    dh2, d_f2_pre, d_f2_post, d_gu2, d_down2 = _ffn_bwd(
        dh3, s3, gains["ffn2_pre"], gains["ffn2_post"], w["gu2"], w["down2"], "ffn2")

    dmo, d_mix_post = post_bwd(dh2, mo, gains["mix_post"], 1.0, "mix_post_bwd")
    dmerged = mm_nt_2d(dmo, w["out"], F32, "mix_dmerged")
    d_out = mm_tn_2d(merged, dmo, F32, "mix_dw_out")
    dgate, dy_ab = merge_bwd(dmerged, proj, y_ab, "mix_merge_bwd")
    do_ab = _mm_branches_bwd(dy_ab, w["proj"], "mix_do")
    d_proj = _mm_branches_dw(o_ab, dy_ab, "mix_dw_proj")
    do_raw, dg_b, d_norm_g = gnorm_bwd(do_ab[1], o_raw, proj, norm_g, "hgrn_norm_bwd")
    dq_b, df_b, di_b, dlb = hgrn_bwd(proj, lb_logits, states, do_raw, "hgrn_bwd")
    d_lb_logits = lb_bwd(dlb, lb_logits, "lb_bwd")
    dq_a, dk_a, dv_a, dc_col, dc_row = fox_bwd(proj, c_col, c_row, do_ab[0], "fox_bwd")
    dc = (dc_col.reshape(HEADS, t) + dc_row.reshape(HEADS, t)).T
    dc = jnp.pad(dc, ((0, 0), (0, LANES - HEADS)))
    dfa, d_fox_bias = fox_post_bwd(dc, fa, fox_bias, "fox_post_bwd")
    dproj = jnp.concatenate([dq_a, dk_a.astype(BF16), dv_a.astype(BF16), dq_b, df_b, di_b, dg_b,
                             dgate[0], dgate[1]], axis=1)
    du2a = mm_nn_2d(dproj, w["in_main"], F32, "mix_du")
    du2b = mm_nn_2d(dfa, w["in_fa"], F32, "mix_du_fa")
    d_in_main = mm_tn_2d(dproj, u2, F32, "mix_dw_in")
    d_in_fa = mm_tn_2d(dfa, u2, F32, "mix_dw_in_fa")
    dh1, d_mix_pre = pre_bwd(dh2, h1, gains["mix_pre"], [du2a, du2b], "mix_pre_bwd")

    dx, d_f1_pre, d_f1_post, d_gu1, d_down1 = _ffn_bwd(
        dh1, s1, gains["ffn1_pre"], gains["ffn1_post"], w["gu1"], w["down1"], "ffn1")

    big = dict(gu1=d_gu1, down1=d_down1, in_main=d_in_main, in_fa=d_in_fa, proj=d_proj, out=d_out,
               gu2=d_gu2, down2=d_down2, ple_gate=d_ple_gate, ple_proj=d_ple_proj)
    small = dict(ffn1_pre=d_f1_pre, ffn1_post=d_f1_post, mix_pre=d_mix_pre, mix_post=d_mix_post,
                 ffn2_pre=d_f2_pre, ffn2_post=d_f2_post, ple_pre=d_ple_pre, ple_post=d_ple_post,
                 fox_bias=d_fox_bias, lb_logits=d_lb_logits, norm_g=d_norm_g)
    return loss, dx, big, small


def _mm_branches(o_ab, w_proj, name):
    g, t, kk = o_ab.shape
    _, jn, _, ns = w_proj.shape
    tm = _pick(t, (512, 256, 128))
    return _mm(o_ab, w_proj, mode="nn", grid=(t // tm, g * jn, 1),
               a_spec=pl.BlockSpec((None, tm, kk), lambda i, j, k: (j // jn, i, 0)),
               b_spec=pl.BlockSpec((None, None, kk, ns), lambda i, j, k: (j // jn, j % jn, 0, 0)),
               o_spec=pl.BlockSpec((None, tm, ns), lambda i, j, k: (j // jn, i, j % jn)),
               out_shape=jax.ShapeDtypeStruct((g, t, jn * ns), F32), acc_shape=(tm, ns), name=name)


def _mm_branches_bwd(dy_ab, w_proj, name):
    g, t, _ = dy_ab.shape
    _, jn, kk, ns = w_proj.shape
    tm = _pick(t, (512, 256, 128))
    return _mm(dy_ab, w_proj, mode="nt", grid=(t // tm, g, jn),
               a_spec=pl.BlockSpec((None, tm, ns), lambda i, j, k: (j, i, k)),
               b_spec=pl.BlockSpec((None, None, kk, ns), lambda i, j, k: (j, k, 0, 0)),
               o_spec=pl.BlockSpec((None, tm, kk), lambda i, j, k: (j, i, 0)),
               out_shape=jax.ShapeDtypeStruct((g, t, kk), F32), acc_shape=(tm, kk), name=name)


def _mm_branches_dw(o_ab, dy_ab, name):
    g, t, kk = o_ab.shape
    d = dy_ab.shape[2]
    jn = N_CHIPS
    ns = d // jn
    return _mm(o_ab, dy_ab, mode="tn", grid=(1, g * jn, 1),
               a_spec=pl.BlockSpec((None, t, kk), lambda i, j, k: (j // jn, 0, 0)),
               b_spec=pl.BlockSpec((None, t, ns), lambda i, j, k: (j // jn, 0, j % jn)),
               o_spec=pl.BlockSpec((None, None, kk, ns), lambda i, j, k: (j // jn, j % jn, 0, 0)),
               out_shape=jax.ShapeDtypeStruct((g, jn, kk, ns), F32), acc_shape=(kk, ns), name=name)


HBM_SPEC = pl.BlockSpec(memory_space=pl.ANY)


def _place():
    x, y, c = lax.axis_index("x"), lax.axis_index("y"), lax.axis_index("c")
    chips = [(1 - x, y), (x, 1 - y), (1 - x, 1 - y)]
    return x, y, c, chips


def _half(shape, which, axis):
    n = shape[-2 + axis] // 2
    cut = pl.ds(which * n, n)
    return (cut, slice(None)) if axis == 0 else (slice(None), cut)


def _half_shape(shape, axis):
    s = list(shape)
    s[len(s) - 2 + axis] //= 2
    return tuple(s)


def gather_weights(groups, axes):
    flat = [(gi, g, s, axes[gi]) for gi, grp in enumerate(groups) for g, s in enumerate(grp)]
    n_in = len(flat)
    n_out = len(groups)
    per = 7

    def body(*refs):
        ins = refs[:n_in]
        outs = refs[n_in:n_in + n_out]
        send_sems, recv_sems = refs[n_in + n_out:]
        x, y, c, chips = _place()
        j_me = 2 * x + y
        sibling = (x, y, 1 - c)

        def full(i):
            gi, g, _, _ = flat[i]
            return outs[gi].at[g]

        def remote(i, k, src, dst, to):
            return pltpu.make_async_remote_copy(src_ref=src, dst_ref=dst, send_sem=send_sems.at[per * i + k],
                                                recv_sem=recv_sems.at[per * i + k], device_id=to, device_id_type=MESH)

        first, passed = [], []
        for i in range(n_in):
            mine = _half(ins[i].shape, c, flat[i][3])
            for k, chip in enumerate(chips):
                cp = remote(i, k, ins[i].at[mine], full(i).at[(j_me,) + mine], (*chip, c))
                cp.start()
                first.append(cp)
            cp = remote(i, 6, ins[i], full(i).at[j_me], sibling)
            cp.start()
            first.append(cp)
        for i in range(n_in):
            mine = _half(ins[i].shape, c, flat[i][3])
            for k, (px, py) in enumerate(chips):
                landed = full(i).at[(2 * px + py,) + mine]
                remote(i, k, landed, landed, (px, py, c)).wait_recv()
                cp = remote(i, 3 + k, landed, landed, sibling)
                cp.start()
                passed.append(cp)
        for i in range(n_in):
            other = _half(ins[i].shape, 1 - c, flat[i][3])
            for k, (px, py) in enumerate(chips):
                landed = full(i).at[(2 * px + py,) + other]
                remote(i, 3 + k, landed, landed, sibling).wait_recv()
            remote(i, 6, ins[i], full(i).at[j_me], sibling).wait_recv()
        for cp in first + passed:
            cp.wait_send()

    out_shape = [jax.ShapeDtypeStruct((len(grp), N_CHIPS) + grp[0].shape, grp[0].dtype) for grp in groups]
    return pl.pallas_call(
        body, name="gather_weights", in_specs=[HBM_SPEC] * n_in, out_specs=[HBM_SPEC] * n_out, out_shape=out_shape,
        scratch_shapes=[pltpu.SemaphoreType.DMA((per * n_in,)), pltpu.SemaphoreType.DMA((per * n_in,))],
    )(*[s for _, _, s, _ in flat])


def pair_exchange(grads, axes):
    n = len(grads)

    def body(*refs):
        ins, outs = refs[:n], refs[n:2 * n]
        send_sems, recv_sems = refs[2 * n:]
        x, y, c, _ = _place()
        copies = []
        for i in range(n):
            theirs = (slice(None), slice(None)) + _half(ins[i].shape, 1 - c, axes[i])
            cp = pltpu.make_async_remote_copy(
                src_ref=ins[i].at[theirs], dst_ref=outs[i], send_sem=send_sems.at[i],
                recv_sem=recv_sems.at[i], device_id=(x, y, 1 - c), device_id_type=MESH)
            cp.start()
            copies.append(cp)
        for cp in copies:
            cp.wait()

    out_shape = [jax.ShapeDtypeStruct(_half_shape(g.shape, a), g.dtype) for g, a in zip(grads, axes)]
    return pl.pallas_call(
        body, name="pair_exchange", in_specs=[HBM_SPEC] * n, out_specs=[HBM_SPEC] * n, out_shape=out_shape,
        scratch_shapes=[pltpu.SemaphoreType.DMA((n,)), pltpu.SemaphoreType.DMA((n,))],
    )(*grads)


def chip_exchange(parts):
    n = len(parts)

    def body(*refs):
        ins, outs = refs[:n], refs[n:2 * n]
        send_sems, recv_sems = refs[2 * n:]
        x, y, c, chips = _place()
        copies = []
        for i in range(n):
            for k, (px, py) in enumerate(chips):
                cp = pltpu.make_async_remote_copy(
                    src_ref=ins[i].at[:, 2 * px + py], dst_ref=outs[i].at[k], send_sem=send_sems.at[3 * i + k],
                    recv_sem=recv_sems.at[3 * i + k], device_id=(px, py, c), device_id_type=MESH)
                cp.start()
                copies.append(cp)
        for cp in copies:
            cp.wait()

    out_shape = [jax.ShapeDtypeStruct((3, p.shape[0]) + p.shape[2:], p.dtype) for p in parts]
    return pl.pallas_call(
        body, name="chip_exchange", in_specs=[HBM_SPEC] * n, out_specs=[HBM_SPEC] * n, out_shape=out_shape,
        scratch_shapes=[pltpu.SemaphoreType.DMA((3 * n,)), pltpu.SemaphoreType.DMA((3 * n,))],
    )(*parts)


def pair_broadcast(bufs, axes):
    n = len(bufs)

    def body(*refs):
        ins, outs = refs[:n], refs[n:2 * n]
        send_sems, recv_sems = refs[2 * n:]
        x, y, c, _ = _place()
        copies = []
        for i in range(n):
            mine = (slice(None),) + _half(outs[i].shape, c, axes[i])
            cp = pltpu.make_async_remote_copy(src_ref=outs[i].at[mine], dst_ref=outs[i].at[mine],
                                              send_sem=send_sems.at[i], recv_sem=recv_sems.at[i],
                                              device_id=(x, y, 1 - c), device_id_type=MESH)
            cp.start()
            copies.append(cp)
        for i, cp in enumerate(copies):
            cp.wait_send()
            theirs = (slice(None),) + _half(outs[i].shape, 1 - c, axes[i])
            pltpu.make_async_remote_copy(src_ref=outs[i].at[theirs], dst_ref=outs[i].at[theirs],
                                         send_sem=send_sems.at[i], recv_sem=recv_sems.at[i],
                                         device_id=(x, y, 1 - c), device_id_type=MESH).wait_recv()

    return pl.pallas_call(
        body, name="pair_broadcast", in_specs=[HBM_SPEC] * n, out_specs=[HBM_SPEC] * n,
        out_shape=[jax.ShapeDtypeStruct(b.shape, b.dtype) for b in bufs],
        input_output_aliases={i: i for i in range(n)},
        scratch_shapes=[pltpu.SemaphoreType.DMA((n,)), pltpu.SemaphoreType.DMA((n,))],
    )(*bufs)


N_DEV = 8
SLAB_ROWS = 16


def allreduce_small(slab):
    def body(x_ref, o_ref, land, send_sems, recv_sems):
        x, y, c, _ = _place()
        me = 4 * x + 2 * y + c
        land[me] = x_ref[...]
        copies = []
        for d in range(1, N_DEV):
            to = (me + d) % N_DEV
            cp = pltpu.make_async_remote_copy(
                src_ref=x_ref, dst_ref=land.at[me], send_sem=send_sems.at[d - 1], recv_sem=recv_sems.at[me],
                device_id=(to // 4, (to // 2) % 2, to % 2), device_id_type=MESH)
            cp.start()
            copies.append(cp)
        for d in range(1, N_DEV):
            frm = (me + d) % N_DEV
            pltpu.make_async_remote_copy(
                src_ref=x_ref, dst_ref=land.at[frm], send_sem=send_sems.at[d - 1], recv_sem=recv_sems.at[frm],
                device_id=(frm // 4, (frm // 2) % 2, frm % 2), device_id_type=MESH).wait_recv()
        for cp in copies:
            cp.wait_send()
        acc = land[0]
        for s in range(1, N_DEV):
            acc = acc + land[s]
        o_ref[...] = acc

    vm = pl.BlockSpec(memory_space=pltpu.VMEM)
    return pl.pallas_call(
        body, name="allreduce_small", in_specs=[vm], out_specs=vm, out_shape=jax.ShapeDtypeStruct(slab.shape, F32),
        scratch_shapes=[pltpu.VMEM((N_DEV,) + slab.shape, F32), pltpu.SemaphoreType.DMA((N_DEV - 1,)),
                        pltpu.SemaphoreType.DMA((N_DEV,))],
    )(slab)


BLOCK_BYTES = 3 * 1024 * 1024


def _tiles_2d(r, c, budget=BLOCK_BYTES):
    if r % 8 == 0:
        tc = c if c % LANES else _pick(c, (2048, 1408, 1024, 512, 256, 128))
        tr = 8
        for cand in (512, 256, 128, 64, 32, 16, 8):
            if r % cand == 0 and cand * tc * 4 <= budget:
                tr = cand
                break
        return tr, tc
    tc = LANES
    for cand in (1024, 512, 256, 128):
        if c % cand == 0 and r * cand * 4 <= budget:
            tc = cand
            break
    return r, tc


def _grid_spec(grid, in_specs, out_specs):
    return pltpu.PrefetchScalarGridSpec(num_scalar_prefetch=1, grid=grid, in_specs=in_specs, out_specs=out_specs)


def _own(axis, nr, nc):
    if axis == 0:
        return lambda i, j, where: (where[1] * nr + i, j)
    return lambda i, j, where: (i, where[1] * nc + j)


def pair_add(where, grad, recv, axis, name):
    g, jn, hr, hc = recv.shape
    tr, tc = _tiles_2d(hr, hc)
    nr, nc = hr // tr, hc // tc
    own = _own(axis, nr, nc)

    def body(where_ref, a_ref, b_ref, o32_ref, o16_ref):
        s = a_ref[...] + b_ref[...]
        o32_ref[...] = s
        o16_ref[...] = s.astype(BF16)

    blk = pl.BlockSpec((None, None, tr, tc), lambda a, i, j, where: (a // jn, a % jn, i, j))
    mine = pl.BlockSpec((None, None, tr, tc), lambda a, i, j, where: (a // jn, a % jn) + own(i, j, where))
    return pl.pallas_call(
        body, name=name, grid_spec=_grid_spec((g * jn, nr, nc), [mine, blk], [blk, blk]),
        out_shape=[jax.ShapeDtypeStruct(recv.shape, F32), jax.ShapeDtypeStruct(recv.shape, BF16)],
        compiler_params=_params(("parallel", "parallel", "parallel")),
    )(where, grad, recv)


def chip_add(where, part, recv, axis, name):
    g, jn, hr, hc = part.shape
    tr, tc = _tiles_2d(hr, hc)
    nr, nc = hr // tr, hc // tc
    own = _own(axis, nr, nc)
    full = (g, 2 * hr, hc) if axis == 0 else (g, hr, 2 * hc)

    def body(where_ref, a_ref, b_ref, o_ref):
        s = a_ref[...]
        for k in range(3):
            s = s + b_ref[k].astype(F32)
        o_ref[...] = s

    return pl.pallas_call(
        body, name=name,
        grid_spec=_grid_spec((g, nr, nc),
                             [pl.BlockSpec((None, None, tr, tc), lambda a, i, j, where: (a, where[0], i, j)),
                              pl.BlockSpec((3, None, tr, tc), lambda a, i, j, where: (0, a, i, j))],
                             pl.BlockSpec((None, tr, tc), lambda a, i, j, where: (a,) + own(i, j, where))),
        out_shape=jax.ShapeDtypeStruct(full, F32), compiler_params=_params(("parallel", "parallel", "parallel")),
    )(where, part, recv)


def _adam_math(w, g, m, v):
    m2 = ADAM_B1 * m + (1.0 - ADAM_B1) * g
    v2 = ADAM_B2 * v + (1.0 - ADAM_B2) * (g * g)
    m_hat = m2 / (1.0 - ADAM_B1 ** ADAM_STEP)
    v_hat = v2 / (1.0 - ADAM_B2 ** ADAM_STEP)
    delta = -ADAM_LR * (m_hat / (jnp.sqrt(v_hat) + ADAM_EPS) + ADAM_WD * w)
    return delta, m2, v2


def adamw(grad, idx, w, m, v, name):
    _, r, cc = w.shape
    tr, tc = _tiles_2d(r, cc, BLOCK_BYTES // 2)

    def body(g_ref, w_ref, m_ref, v_ref, go_ref, d_ref, mo_ref, vo_ref):
        g = g_ref[...]
        delta, m2, v2 = _adam_math(w_ref[...], g, m_ref[...], v_ref[...])
        go_ref[...] = g
        d_ref[...] = delta
        mo_ref[...] = m2
        vo_ref[...] = v2

    blk = pl.BlockSpec((None, tr, tc), lambda i, j: (0, i, j))
    return pl.pallas_call(
        body, name=name, grid=(r // tr, cc // tc),
        in_specs=[pl.BlockSpec((None, tr, tc), lambda i, j: (idx, i, j)), blk, blk, blk], out_specs=[blk] * 4,
        out_shape=[jax.ShapeDtypeStruct(w.shape, F32)] * 4, compiler_params=_params(("parallel", "parallel")),
    )(grad, w, m, v)


def adamw_small(g, w, m, v):
    def body(g_ref, w_ref, m_ref, v_ref, d_ref, mo_ref, vo_ref):
        delta, m2, v2 = _adam_math(w_ref[...], g_ref[...], m_ref[...], v_ref[...])
        d_ref[...] = delta
        mo_ref[...] = m2
        vo_ref[...] = v2

    return pl.pallas_call(body, name="adamw_small", out_shape=[jax.ShapeDtypeStruct(w.shape, F32)] * 3)(g, w, m, v)


GAINS = ("ffn1_pre", "ffn1_post", "mix_pre", "mix_post", "ffn2_pre", "ffn2_post", "ple_pre", "ple_post")
WEIGHTS = ("ffn1_pre_g", "ffn1_post_g", "ffn1_w_gate", "ffn1_w_up", "ffn1_w_down", "mix_pre_g", "mix_post_g",
           "mix_w_in", "fox_f_bias", "hgrn_lb_logits", "hgrn_norm_g", "mix_w_proj_fox", "mix_w_proj_hgrn",
           "mix_w_out", "ffn2_pre_g", "ffn2_post_g", "ffn2_w_gate", "ffn2_w_up", "ffn2_w_down", "ple_pre_g",
           "ple_post_g", "ple_w_gate", "ple_w_proj")
GROUPS = (("gu1", ("ffn1_w_gate", "ffn1_w_up"), 0), ("down1", ("ffn1_w_down",), 0), ("win", ("mix_w_in",), 1),
          ("proj", ("mix_w_proj_fox", "mix_w_proj_hgrn"), 0), ("out", ("mix_w_out",), 0),
          ("gu2", ("ffn2_w_gate", "ffn2_w_up"), 0), ("down2", ("ffn2_w_down",), 0), ("ple_gate", ("ple_w_gate",), 0),
          ("ple_proj", ("ple_w_proj",), 0))
TRANSPOSED = ("mix_w_in",)


def _pad_row(a, width):
    a = a.reshape(1, -1)
    return jnp.pad(a, ((0, 0), (0, width - a.shape[1])))


def _pack_small(vals):
    d = D_MODEL
    rows = [vals[n + "_g"].reshape(1, d) for n in GAINS]
    rows.append(_pad_row(vals["fox_f_bias"], d))
    lg = vals["hgrn_lb_logits"]
    rows += [_pad_row(lg[0], d), _pad_row(lg[1], d), _pad_row(vals["hgrn_norm_g"], d)]
    slab = jnp.concatenate(rows, axis=0)
    return jnp.pad(slab, ((0, SLAB_ROWS - slab.shape[0]), (0, 0)))


def _unpack_small(slab):
    out = {n + "_g": slab[i:i + 1] for i, n in enumerate(GAINS)}
    out["fox_f_bias"] = slab[8:9, :HEADS]
    out["hgrn_lb_logits"] = slab[9:11, :WIDTH]
    out["hgrn_norm_g"] = slab[11:12, :HEAD_DIM]
    return out


def _split_in(win_t):
    lo = 3 * WIDTH
    main = jnp.concatenate([win_t[:lo], win_t[lo + HEADS:]], axis=0)
    fa = jnp.pad(win_t[lo:lo + HEADS], ((0, LANES - HEADS), (0, 0)))
    return main, fa


def _join_in(main, fa):
    lo = 3 * WIDTH
    return jnp.concatenate([main[:lo], fa[:HEADS], main[lo:]], axis=0)


def _as_block(name, a):
    return jnp.swapaxes(a, 1, 2) if name in TRANSPOSED else a


def kernel(x, p, ffn1_pre_g, ffn1_post_g, ffn1_w_gate, ffn1_w_up, ffn1_w_down, mix_pre_g, mix_post_g, mix_w_in, fox_f_bias, hgrn_lb_logits, hgrn_norm_g, mix_w_proj_fox, mix_w_proj_hgrn, mix_w_out, ffn2_pre_g, ffn2_post_g, ffn2_w_gate, ffn2_w_up, ffn2_w_down, ple_pre_g, ple_post_g, ple_w_gate, ple_w_proj, loss_target, m_ffn1_pre_g, m_ffn1_post_g, m_ffn1_w_gate, m_ffn1_w_up, m_ffn1_w_down, m_mix_pre_g, m_mix_post_g, m_mix_w_in, m_fox_f_bias, m_hgrn_lb_logits, m_hgrn_norm_g, m_mix_w_proj_fox, m_mix_w_proj_hgrn, m_mix_w_out, m_ffn2_pre_g, m_ffn2_post_g, m_ffn2_w_gate, m_ffn2_w_up, m_ffn2_w_down, m_ple_pre_g, m_ple_post_g, m_ple_w_gate, m_ple_w_proj, v_ffn1_pre_g, v_ffn1_post_g, v_ffn1_w_gate, v_ffn1_w_up, v_ffn1_w_down, v_mix_pre_g, v_mix_post_g, v_mix_w_in, v_fox_f_bias, v_hgrn_lb_logits, v_hgrn_norm_g, v_mix_w_proj_fox, v_mix_w_proj_hgrn, v_mix_w_out, v_ffn2_pre_g, v_ffn2_post_g, v_ffn2_w_gate, v_ffn2_w_up, v_ffn2_w_down, v_ple_pre_g, v_ple_post_g, v_ple_w_gate, v_ple_w_proj):
    args = dict(locals())
    wts = {n: args[n] for n in WEIGHTS}
    mom = {n: args["m_" + n] for n in WEIGHTS}
    var = {n: args["v_" + n] for n in WEIGHTS}
    d = D_MODEL
    axes = [a for _, _, a in GROUPS]

    gathered = gather_weights([[_as_block(n, wts[n])[0].astype(BF16) for n in names] for _, names, _ in GROUPS], axes)
    full = {name: g for (name, _, _), g in zip(GROUPS, gathered)}
    in_main, in_fa = _split_in(full["win"].reshape(-1, d))
    w = dict(gu1=full["gu1"], down1=full["down1"].reshape(-1, d), in_main=in_main, in_fa=in_fa, proj=full["proj"],
             out=full["out"].reshape(-1, d), gu2=full["gu2"], down2=full["down2"].reshape(-1, d),
             ple_gate=full["ple_gate"].reshape(-1, d), ple_proj=full["ple_proj"])

    gains = {n: wts[n + "_g"] for n in GAINS}
    loss, dx, big, small = layer_step(x[0], p[0, 0].astype(BF16), loss_target[0], gains,
                                      _pad_row(fox_f_bias, LANES), hgrn_lb_logits, hgrn_norm_g, w)

    rows4 = lambda a: a.reshape(1, N_CHIPS, a.shape[0] // N_CHIPS, a.shape[1])
    grads = [big["gu1"], rows4(big["down1"]), rows4(_join_in(big["in_main"], big["in_fa"])), big["proj"],
             rows4(big["out"]), big["gu2"], rows4(big["down2"]), rows4(big["ple_gate"]), big["ple_proj"]]

    where = jnp.stack([2 * lax.axis_index("x") + lax.axis_index("y"), lax.axis_index("c")]).astype(jnp.int32)
    from_sibling = pair_exchange(grads, axes)
    sums = [pair_add(where, g, r, a, "pair_add_" + name)
            for (name, _, a), g, r in zip(GROUPS, grads, from_sibling)]
    from_chips = chip_exchange([s16 for _, s16 in sums])
    halves = [chip_add(where, s32, r, a, "chip_add_" + name)
              for (name, _, a), (s32, _), r in zip(GROUPS, sums, from_chips)]
    reduced = pair_broadcast(halves, axes)

    out_g, out_d, out_m, out_v = {}, {}, {}, {}
    for (_, names, _), red in zip(GROUPS, reduced):
        for idx, n in enumerate(names):
            res = adamw(red, idx, _as_block(n, wts[n]), _as_block(n, mom[n]), _as_block(n, var[n]), "adamw_" + n)
            out_g[n], out_d[n], out_m[n], out_v[n] = [_as_block(n, r) for r in res]

    small_named = {n + "_g": small[n] for n in GAINS}
    small_named.update(fox_f_bias=small["fox_bias"][:, :HEADS], hgrn_lb_logits=small["lb_logits"],
                       hgrn_norm_g=small["norm_g"])
    g_small = allreduce_small(_pack_small(small_named))
    d_small, m_small, v_small = adamw_small(g_small, _pack_small(wts), _pack_small(mom), _pack_small(var))
    for dst, slab in ((out_g, g_small), (out_d, d_small), (out_m, m_small), (out_v, v_small)):
        dst.update(_unpack_small(slab))

    total = lax.psum(loss[0, 0], ("x", "y", "c"))
    return (total, dx[None], *[out_g[n] for n in WEIGHTS], *[out_d[n] for n in WEIGHTS],
            *[out_m[n] for n in WEIGHTS], *[out_v[n] for n in WEIGHTS])
```

```python
import functools

import jax
import jax.numpy as jnp
from jax import lax
from jax.experimental import pallas as pl
from jax.experimental.pallas import tpu as pltpu

F32 = jnp.float32
BF16 = jnp.bfloat16

D_MODEL = 2048
SEQ = 2048
D_FF = 5632
PLE_DIM = 256
HEADS = 8
HEAD_DIM = 128
WIDTH = HEADS * HEAD_DIM
CHUNK = 64
SUB = 16
NORM_EPS = 1e-6
MACARON_SCALE = 0.5
N_CHIPS = 4

ADAM_LR = 0.001
ADAM_B1 = 0.9
ADAM_B2 = 0.999
ADAM_EPS = 1e-08
ADAM_WD = 0.01
ADAM_STEP = 10

LANES = 128
VMEM_LIMIT = 56 * 1024 * 1024
NEG_BIG = -1e30
MESH = pl.DeviceIdType.MESH


def _pick(n, cands):
    for c in cands:
        if c <= n and n % c == 0:
            return c
    return n


def _params(sem, vmem=VMEM_LIMIT):
    return pltpu.CompilerParams(dimension_semantics=sem, vmem_limit_bytes=vmem)


def _sigmoid(x):
    return 1.0 / (1.0 + jnp.exp(-x))


def _silu(x):
    return x * _sigmoid(x)


def _silu_grad(x):
    s = _sigmoid(x)
    return s * (1.0 + x * (1.0 - s))


_DN = {"nn": (((1,), (0,)), ((), ())), "nt": (((1,), (1,)), ((), ())), "tn": (((0,), (0,)), ((), ()))}


def _mm(a, b, *, mode, grid, a_spec, b_spec, o_spec, out_shape, acc_shape, name):
    nk = grid[2]
    dn = _DN[mode]

    def body(a_ref, b_ref, o_ref, acc_ref):
        k = pl.program_id(2)

        @pl.when(k == 0)
        def _():
            acc_ref[...] = jnp.zeros_like(acc_ref)

        acc_ref[...] += lax.dot_general(a_ref[...].astype(BF16), b_ref[...].astype(BF16), dn,
                                        preferred_element_type=F32)

        @pl.when(k == nk - 1)
        def _():
            o_ref[...] = acc_ref[...].astype(o_ref.dtype)

    return pl.pallas_call(
        body, name=name, grid=grid, in_specs=[a_spec, b_spec], out_specs=o_spec, out_shape=out_shape,
        scratch_shapes=[pltpu.VMEM(acc_shape, F32)],
        compiler_params=_params(("parallel", "parallel", "arbitrary")),
    )(a, b)


def mm_nn_2d(a, b, out_dtype, name):
    m, kk = a.shape
    n = b.shape[1]
    tm, tn, tk = _pick(m, (512, 256, 128)), _pick(n, (1024, 512, 256, 128)), _pick(kk, (2048, 1408, 1024, 512, 256, 128))
    return _mm(a, b, mode="nn", grid=(m // tm, n // tn, kk // tk),
               a_spec=pl.BlockSpec((tm, tk), lambda i, j, k: (i, k)),
               b_spec=pl.BlockSpec((tk, tn), lambda i, j, k: (k, j)),
               o_spec=pl.BlockSpec((tm, tn), lambda i, j, k: (i, j)),
               out_shape=jax.ShapeDtypeStruct((m, n), out_dtype), acc_shape=(tm, tn), name=name)


def mm_nt_2d(a, b, out_dtype, name):
    m, c = a.shape
    n = b.shape[0]
    tm, tn, tk = _pick(m, (512, 256, 128)), _pick(n, (1408, 1024, 512, 256, 128)), _pick(c, (2048, 1408, 1024, 512, 256, 128))
    return _mm(a, b, mode="nt", grid=(m // tm, n // tn, c // tk),
               a_spec=pl.BlockSpec((tm, tk), lambda i, j, k: (i, k)),
               b_spec=pl.BlockSpec((tn, tk), lambda i, j, k: (j, k)),
               o_spec=pl.BlockSpec((tm, tn), lambda i, j, k: (i, j)),
               out_shape=jax.ShapeDtypeStruct((m, n), out_dtype), acc_shape=(tm, tn), name=name)


def mm_tn_2d(a, b, out_dtype, name):
    c, m = a.shape
    n = b.shape[1]
    tm, tn, tk = _pick(m, (1408, 1024, 512, 256, 128)), _pick(n, (1024, 512, 256, 128)), _pick(c, (2048, 1024, 512, 256, 128))
    return _mm(a, b, mode="tn", grid=(m // tm, n // tn, c // tk),
               a_spec=pl.BlockSpec((tk, tm), lambda i, j, k: (k, i)),
               b_spec=pl.BlockSpec((tk, tn), lambda i, j, k: (k, j)),
               o_spec=pl.BlockSpec((tm, tn), lambda i, j, k: (i, j)),
               out_shape=jax.ShapeDtypeStruct((m, n), out_dtype), acc_shape=(tm, tn), name=name)


def mm_nn_col(a, w, out_dtype, name):
    m, kk = a.shape
    g, jn, _, ns = w.shape
    tm, tk = _pick(m, (512, 256, 128)), _pick(kk, (2048, 1024, 512, 256, 128))
    return _mm(a, w, mode="nn", grid=(m // tm, g * jn, kk // tk),
               a_spec=pl.BlockSpec((tm, tk), lambda i, j, k: (i, k)),
               b_spec=pl.BlockSpec((None, None, tk, ns), lambda i, j, k: (j // jn, j % jn, k, 0)),
               o_spec=pl.BlockSpec((None, tm, ns), lambda i, j, k: (j // jn, i, j % jn)),
               out_shape=jax.ShapeDtypeStruct((g, m, jn * ns), out_dtype), acc_shape=(tm, ns), name=name)


def mm_nt_col(a, w, out_dtype, name):
    g, m, _ = a.shape
    _, jn, kk, ns = w.shape
    tm, tn = _pick(m, (512, 256, 128)), _pick(kk, (1024, 512, 256, 128))
    return _mm(a, w, mode="nt", grid=(m // tm, kk // tn, g * jn),
               a_spec=pl.BlockSpec((None, tm, ns), lambda i, j, k: (k // jn, i, k % jn)),
               b_spec=pl.BlockSpec((None, None, tn, ns), lambda i, j, k: (k // jn, k % jn, j, 0)),
               o_spec=pl.BlockSpec((tm, tn), lambda i, j, k: (i, j)),
               out_shape=jax.ShapeDtypeStruct((m, kk), out_dtype), acc_shape=(tm, tn), name=name)


def mm_tn_col(a, b, jn, out_dtype, name):
    c, kk = a.shape
    g, _, n = b.shape
    ns = n // jn
    tm, tk = _pick(kk, (512, 256, 128)), _pick(c, (2048, 1024, 512, 256, 128))
    return _mm(a, b, mode="tn", grid=(kk // tm, g * jn, c // tk),
               a_spec=pl.BlockSpec((tk, tm), lambda i, j, k: (k, i)),
               b_spec=pl.BlockSpec((None, tk, ns), lambda i, j, k: (j // jn, k, j % jn)),
               o_spec=pl.BlockSpec((None, None, tm, ns), lambda i, j, k: (j // jn, j % jn, i, 0)),
               out_shape=jax.ShapeDtypeStruct((g, jn, kk, ns), out_dtype), acc_shape=(tm, ns), name=name)


def _rstd(x):
    return lax.rsqrt(jnp.mean(x * x, axis=-1, keepdims=True) + NORM_EPS)


def _rms_bwd(x, g, dy):
    r = _rstd(x)
    xn = x * r
    dyg = dy * g
    dx = r * (dyg - xn * jnp.mean(dyg * xn, axis=-1, keepdims=True))
    return dx, jnp.sum(dy * xn, axis=0, keepdims=True)


def _row_tile(t):
    return _pick(t, (256, 128, 64, 32, 16, 8))


def norm_in(h, g, name):
    t, d = h.shape
    tr = _row_tile(t)

    def body(h_ref, g_ref, u_ref):
        x = h_ref[...]
        u_ref[...] = (x * _rstd(x) * g_ref[...]).astype(BF16)

    return pl.pallas_call(
        body, name=name, grid=(t // tr,),
        in_specs=[pl.BlockSpec((tr, d), lambda i: (i, 0)), pl.BlockSpec((1, d), lambda i: (0, 0))],
        out_specs=pl.BlockSpec((tr, d), lambda i: (i, 0)),
        out_shape=jax.ShapeDtypeStruct((t, d), BF16), compiler_params=_params(("parallel",)),
    )(h, g)


def resid_post(h, y, g, scale, name):
    t, d = h.shape
    tr = _row_tile(t)

    def body(h_ref, y_ref, g_ref, o_ref):
        yv = y_ref[...]
        o_ref[...] = h_ref[...] + scale * (yv * _rstd(yv) * g_ref[...])

    row = pl.BlockSpec((tr, d), lambda i: (i, 0))
    return pl.pallas_call(
        body, name=name, grid=(t // tr,), in_specs=[row, row, pl.BlockSpec((1, d), lambda i: (0, 0))],
        out_specs=row, out_shape=jax.ShapeDtypeStruct((t, d), F32), compiler_params=_params(("parallel",)),
    )(h, y, g)


def post_bwd(dh, y, g, scale, name):
    t, d = dh.shape
    tr = _row_tile(t)

    def body(dh_ref, y_ref, g_ref, dy_ref, dg_ref):
        @pl.when(pl.program_id(0) == 0)
        def _():
            dg_ref[...] = jnp.zeros_like(dg_ref)

        dx, dg = _rms_bwd(y_ref[...], g_ref[...], scale * dh_ref[...])
        dy_ref[...] = dx.astype(BF16)
        dg_ref[...] += dg

    row = pl.BlockSpec((tr, d), lambda i: (i, 0))
    vec = pl.BlockSpec((1, d), lambda i: (0, 0))
    return pl.pallas_call(
        body, name=name, grid=(t // tr,), in_specs=[row, row, vec], out_specs=[row, vec],
        out_shape=[jax.ShapeDtypeStruct((t, d), BF16), jax.ShapeDtypeStruct((1, d), F32)],
        compiler_params=_params(("arbitrary",)),
    )(dh, y, g)


def pre_bwd(dh, h, g, dus, name):
    t, d = dh.shape
    tr = _row_tile(t)
    n_du = len(dus)

    def body(*refs):
        dh_ref, h_ref, g_ref = refs[:3]
        du_refs = refs[3:3 + n_du]
        o_ref, dg_ref = refs[3 + n_du:]

        @pl.when(pl.program_id(0) == 0)
        def _():
            dg_ref[...] = jnp.zeros_like(dg_ref)

        du = du_refs[0][...]
        for r in du_refs[1:]:
            du = du + r[...]
        dx, dg = _rms_bwd(h_ref[...], g_ref[...], du)
        o_ref[...] = dh_ref[...] + dx
        dg_ref[...] += dg

    row = pl.BlockSpec((tr, d), lambda i: (i, 0))
    vec = pl.BlockSpec((1, d), lambda i: (0, 0))
    return pl.pallas_call(
        body, name=name, grid=(t // tr,), in_specs=[row, row, vec] + [row] * n_du, out_specs=[row, vec],
        out_shape=[jax.ShapeDtypeStruct((t, d), F32), jax.ShapeDtypeStruct((1, d), F32)],
        compiler_params=_params(("arbitrary",)),
    )(dh, h, g, *dus)


def _ew_tiles(t, f):
    return _pick(t, (256, 128, 64, 32, 16, 8)), _pick(f, (1408, 1024, 512, 256, 128))


def swiglu_act(gu, name):
    _, t, f = gu.shape
    tr, tc = _ew_tiles(t, f)

    def body(gu_ref, o_ref):
        o_ref[...] = (_silu(gu_ref[0]) * gu_ref[1]).astype(BF16)

    return pl.pallas_call(
        body, name=name, grid=(t // tr, f // tc),
        in_specs=[pl.BlockSpec((2, tr, tc), lambda i, j: (0, i, j))],
        out_specs=pl.BlockSpec((tr, tc), lambda i, j: (i, j)),
        out_shape=jax.ShapeDtypeStruct((t, f), BF16), compiler_params=_params(("parallel", "parallel")),
    )(gu)


def swiglu_bwd(dact, gu, name):
    _, t, f = gu.shape
    tr, tc = _ew_tiles(t, f)

    def body(da_ref, gu_ref, o_ref):
        da = da_ref[...]
        gate = gu_ref[0]
        o_ref[0] = (da * gu_ref[1] * _silu_grad(gate)).astype(BF16)
        o_ref[1] = (da * _silu(gate)).astype(BF16)

    return pl.pallas_call(
        body, name=name, grid=(t // tr, f // tc),
        in_specs=[pl.BlockSpec((tr, tc), lambda i, j: (i, j)), pl.BlockSpec((2, tr, tc), lambda i, j: (0, i, j))],
        out_specs=pl.BlockSpec((2, tr, tc), lambda i, j: (0, i, j)),
        out_shape=jax.ShapeDtypeStruct((2, t, f), BF16), compiler_params=_params(("parallel", "parallel")),
    )(dact, gu)


def _col_blocks():
    w = WIDTH // LANES
    return dict(q_a=0, k_a=w, v_a=2 * w, q_b=3 * w, f_b=4 * w, i_b=5 * w, g_b=6 * w, gate_a=7 * w,
                gate_b=7 * w + D_MODEL // LANES)


def _tri(n, lower):
    r = lax.broadcasted_iota(jnp.int32, (n, n), 0)
    c = lax.broadcasted_iota(jnp.int32, (n, n), 1)
    return jnp.where((r >= c) if lower else (r <= c), 1.0, 0.0).astype(F32)


def _dot_hi(a, b):
    return jnp.dot(a, b, precision=lax.Precision.HIGHEST, preferred_element_type=F32)


def fox_prep(fa, bias, name):
    t, w = fa.shape
    tb = _pick(t, (256, 128, 64))

    def body(fa_ref, b_ref, c_ref, carry_ref):
        @pl.when(pl.program_id(0) == 0)
        def _():
            carry_ref[...] = jnp.zeros_like(carry_ref)

        z = fa_ref[...] + b_ref[...]
        lf = jnp.minimum(z, 0.0) - jnp.log(1.0 + jnp.exp(-jnp.abs(z)))
        c = _dot_hi(_tri(tb, True), lf) + carry_ref[...]
        c_ref[...] = c
        carry_ref[...] = carry_ref[...] + jnp.sum(lf, axis=0, keepdims=True)

    return pl.pallas_call(
        body, name=name, grid=(t // tb,),
        in_specs=[pl.BlockSpec((tb, w), lambda i: (i, 0)), pl.BlockSpec((1, w), lambda i: (0, 0))],
        out_specs=pl.BlockSpec((tb, w), lambda i: (i, 0)),
        out_shape=jax.ShapeDtypeStruct((t, w), F32), scratch_shapes=[pltpu.VMEM((1, w), F32)],
        compiler_params=_params(("arbitrary",)),
    )(fa, bias)


def fox_post_bwd(dc, fa, bias, name):
    t, w = fa.shape
    tb = _pick(t, (256, 128, 64))
    nb = t // tb

    def body(dc_ref, fa_ref, b_ref, dfa_ref, db_ref, carry_ref):
        @pl.when(pl.program_id(0) == 0)
        def _():
            carry_ref[...] = jnp.zeros_like(carry_ref)
            db_ref[...] = jnp.zeros_like(db_ref)

        dcv = dc_ref[...]
        dlf = _dot_hi(_tri(tb, False), dcv) + carry_ref[...]
        z = fa_ref[...] + b_ref[...]
        dz = dlf * _sigmoid(-z)
        dfa_ref[...] = dz.astype(BF16)
        db_ref[...] += jnp.sum(dz, axis=0, keepdims=True)
        carry_ref[...] = carry_ref[...] + jnp.sum(dcv, axis=0, keepdims=True)

    rev = pl.BlockSpec((tb, w), lambda i: (nb - 1 - i, 0))
    vec = pl.BlockSpec((1, w), lambda i: (0, 0))
    return pl.pallas_call(
        body, name=name, grid=(nb,), in_specs=[rev, rev, vec], out_specs=[rev, vec],
        out_shape=[jax.ShapeDtypeStruct((t, w), BF16), jax.ShapeDtypeStruct((1, w), F32)],
        scratch_shapes=[pltpu.VMEM((1, w), F32)], compiler_params=_params(("arbitrary",)),
    )(dc, fa, bias)


def _fox_probs(q_ref, k_ref, cc_ref, cr_ref, qi, tq, t):
    scale = HEAD_DIM ** -0.5
    s = lax.dot_general(q_ref[...].astype(BF16), k_ref[...].astype(BF16), _DN["nt"], preferred_element_type=F32)
    logits = s * scale + cc_ref[...] - cr_ref[...]
    qpos = qi * tq + lax.broadcasted_iota(jnp.int32, (tq, t), 0)
    kpos = lax.broadcasted_iota(jnp.int32, (tq, t), 1)
    logits = jnp.where(kpos <= qpos, logits, NEG_BIG)
    m = jnp.max(logits, axis=-1, keepdims=True)
    p = jnp.exp(logits - m)
    return p / jnp.sum(p, axis=-1, keepdims=True)


def fox_fwd(proj, c_col, c_row, name):
    t = proj.shape[0]
    tq = _pick(t, (256, 128))
    cb = _col_blocks()
    dh = HEAD_DIM

    def body(q_ref, k_ref, v_ref, cc_ref, cr_ref, o_ref):
        p = _fox_probs(q_ref, k_ref, cc_ref, cr_ref, pl.program_id(1), tq, t)
        o_ref[...] = jnp.dot(p.astype(BF16), v_ref[...].astype(BF16), preferred_element_type=F32).astype(BF16)

    return pl.pallas_call(
        body, name=name, grid=(HEADS, t // tq),
        in_specs=[pl.BlockSpec((tq, dh), lambda h, i: (i, cb["q_a"] + h)),
                  pl.BlockSpec((t, dh), lambda h, i: (0, cb["k_a"] + h)),
                  pl.BlockSpec((t, dh), lambda h, i: (0, cb["v_a"] + h)),
                  pl.BlockSpec((None, tq, 1), lambda h, i: (h, i, 0)),
                  pl.BlockSpec((None, 1, t), lambda h, i: (h, 0, 0))],
        out_specs=pl.BlockSpec((tq, dh), lambda h, i: (i, h)),
        out_shape=jax.ShapeDtypeStruct((t, WIDTH), BF16), compiler_params=_params(("parallel", "parallel")),
    )(proj, proj, proj, c_col, c_row)


def fox_bwd(proj, c_col, c_row, do, name):
    t = proj.shape[0]
    tq = _pick(t, (256, 128))
    cb = _col_blocks()
    dh = HEAD_DIM
    scale = HEAD_DIM ** -0.5

    def body(q_ref, k_ref, v_ref, cc_ref, cr_ref, do_ref, dq_ref, dk_ref, dv_ref, dcc_ref, dcr_ref):
        @pl.when(pl.program_id(1) == 0)
        def _():
            dk_ref[...] = jnp.zeros_like(dk_ref)
            dv_ref[...] = jnp.zeros_like(dv_ref)
            dcr_ref[...] = jnp.zeros_like(dcr_ref)

        p = _fox_probs(q_ref, k_ref, cc_ref, cr_ref, pl.program_id(1), tq, t)
        dov = do_ref[...].astype(BF16)
        kb = k_ref[...].astype(BF16)
        dv_ref[...] += lax.dot_general(p.astype(BF16), dov, _DN["tn"], preferred_element_type=F32)
        dp = lax.dot_general(dov, v_ref[...].astype(BF16), _DN["nt"], preferred_element_type=F32)
        ds = p * (dp - jnp.sum(p * dp, axis=-1, keepdims=True))
        dcc_ref[...] = jnp.sum(ds, axis=-1, keepdims=True)
        dcr_ref[...] -= jnp.sum(ds, axis=0, keepdims=True)
        dss = (ds * scale).astype(BF16)
        dq_ref[...] = jnp.dot(dss, kb, preferred_element_type=F32).astype(BF16)
        dk_ref[...] += lax.dot_general(dss, q_ref[...].astype(BF16), _DN["tn"], preferred_element_type=F32)

    return pl.pallas_call(
        body, name=name, grid=(HEADS, t // tq),
        in_specs=[pl.BlockSpec((tq, dh), lambda h, i: (i, cb["q_a"] + h)),
                  pl.BlockSpec((t, dh), lambda h, i: (0, cb["k_a"] + h)),
                  pl.BlockSpec((t, dh), lambda h, i: (0, cb["v_a"] + h)),
                  pl.BlockSpec((None, tq, 1), lambda h, i: (h, i, 0)),
                  pl.BlockSpec((None, 1, t), lambda h, i: (h, 0, 0)),
                  pl.BlockSpec((tq, dh), lambda h, i: (i, h))],
        out_specs=[pl.BlockSpec((tq, dh), lambda h, i: (i, h)),
                   pl.BlockSpec((t, dh), lambda h, i: (0, h)),
                   pl.BlockSpec((t, dh), lambda h, i: (0, h)),
                   pl.BlockSpec((None, tq, 1), lambda h, i: (h, i, 0)),
                   pl.BlockSpec((None, 1, t), lambda h, i: (h, 0, 0))],
        out_shape=[jax.ShapeDtypeStruct((t, WIDTH), BF16), jax.ShapeDtypeStruct((t, WIDTH), F32),
                   jax.ShapeDtypeStruct((t, WIDTH), F32), jax.ShapeDtypeStruct((HEADS, t, 1), F32),
                   jax.ShapeDtypeStruct((HEADS, 1, t), F32)],
        compiler_params=_params(("parallel", "arbitrary")),
    )(proj, proj, proj, c_col, c_row, do)


def _lower_bound(lg_ref):
    l0 = lg_ref[0:1, :]
    l1 = lg_ref[1:2, :]
    m = jnp.maximum(l0, l1)
    e0 = jnp.exp(l0 - m)
    e1 = jnp.exp(l1 - m)
    return e0 / (e0 + e1)


def _hgrn_inputs(qb_ref, fb_ref, lg_ref, q_s, k_s, cum_s):
    lb = _lower_bound(lg_ref)
    sig = _sigmoid(fb_ref[...])
    f = lb + (1.0 - lb) * sig
    q_s[...] = _silu(qb_ref[...])
    k_s[...] = 1.0 - f
    cum_s[...] = _dot_hi(_tri(CHUNK, True), jnp.log(f))
    return lb, sig, f


def _boundary(cum_s, a):
    if a == 0:
        return jnp.zeros((1, HEAD_DIM), F32)
    return cum_s[pl.ds(SUB * a - 1, 1), :]


def _hgrn_scores(q_s, k_s, cum_s):
    cum = cum_s[...]
    kk = k_s[...]
    lane = lax.broadcasted_iota(jnp.int32, (SUB, CHUNK), 1)
    row = lax.broadcasted_iota(jnp.int32, (SUB, 1), 0)
    blocks = []
    for a in range(CHUNK // SUB):
        rows = pl.ds(SUB * a, SUB)
        ca = _boundary(cum_s, a)
        cum_a = cum_s[rows, :]
        q_a = q_s[rows, :]
        qa = q_a * jnp.exp(cum_a - ca)
        ka = kk * jnp.exp(jnp.minimum(ca - cum, 0.0))
        blk = lax.dot_general(qa, ka, _DN["nt"], preferred_element_type=F32)
        blk = jnp.where(lane < SUB * a, blk, 0.0)
        for s in range(SUB):
            r = SUB * a + s
            e = jnp.exp(jnp.minimum(cum_a - cum_s[pl.ds(r, 1), :], 0.0))
            col = jnp.sum(q_a * k_s[pl.ds(r, 1), :] * e, axis=-1, keepdims=True)
            col = jnp.where(row >= s, col, 0.0)
            blk = jnp.where(lane == r, col, blk)
        blocks.append(blk)
    return jnp.concatenate(blocks, axis=0)


def hgrn_fwd(proj, lb_logits, name):
    t = proj.shape[0]
    n = t // CHUNK
    cb = _col_blocks()
    dh = HEAD_DIM

    def body(qb_ref, fb_ref, ib_ref, lg_ref, o_ref, st_ref, state, q_s, k_s, cum_s):
        @pl.when(pl.program_id(1) == 0)
        def _():
            state[...] = jnp.zeros_like(state)

        _hgrn_inputs(qb_ref, fb_ref, lg_ref, q_s, k_s, cum_s)
        st = state[...]
        st_ref[...] = st
        cum = cum_s[...]
        v = ib_ref[...]
        qe = q_s[...] * jnp.exp(cum)
        inter = lax.dot_general(qe, st, _DN["nt"], preferred_element_type=F32)
        a_mat = _hgrn_scores(q_s, k_s, cum_s)
        o_ref[...] = inter + jnp.dot(a_mat, v, preferred_element_type=F32)
        last = cum_s[pl.ds(CHUNK - 1, 1), :]
        kd = k_s[...] * jnp.exp(last - cum)
        state[...] = st * jnp.exp(last) + lax.dot_general(v, kd, _DN["tn"], preferred_element_type=F32)

    blk = lambda off: pl.BlockSpec((CHUNK, dh), lambda h, i: (i, off + h))
    return pl.pallas_call(
        body, name=name, grid=(HEADS, n),
        in_specs=[blk(cb["q_b"]), blk(cb["f_b"]), blk(cb["i_b"]), pl.BlockSpec((2, dh), lambda h, i: (0, h))],
        out_specs=[pl.BlockSpec((CHUNK, dh), lambda h, i: (i, h)),
                   pl.BlockSpec((None, None, dh, dh), lambda h, i: (h, i, 0, 0))],
        out_shape=[jax.ShapeDtypeStruct((t, WIDTH), F32), jax.ShapeDtypeStruct((HEADS, n, dh, dh), F32)],
        scratch_shapes=[pltpu.VMEM((dh, dh), F32)] + [pltpu.VMEM((CHUNK, dh), F32)] * 3,
        compiler_params=_params(("parallel", "arbitrary")),
    )(proj, proj, proj, lb_logits)


def hgrn_bwd(proj, lb_logits, states, do, name):
    t = proj.shape[0]
    n = t // CHUNK
    cb = _col_blocks()
    dh = HEAD_DIM
    nsub = CHUNK // SUB

    def body(qb_ref, fb_ref, ib_ref, lg_ref, st_ref, do_ref, dqb_ref, dfb_ref, dib_ref, dlb_ref,
             dstate, q_s, k_s, cum_s, da_s, dq_s, dk_s):
        @pl.when(pl.program_id(1) == 0)
        def _():
            dstate[...] = jnp.zeros_like(dstate)
            dlb_ref[...] = jnp.zeros_like(dlb_ref)

        lb, sig, f = _hgrn_inputs(qb_ref, fb_ref, lg_ref, q_s, k_s, cum_s)
        st = st_ref[...]
        dst = dstate[...]
        cum = cum_s[...]
        q = q_s[...]
        kk = k_s[...]
        v = ib_ref[...]
        dov = do_ref[...]
        e_cum = jnp.exp(cum)
        qe = q * e_cum
        last = cum_s[pl.ds(CHUNK - 1, 1), :]
        e_last = jnp.exp(last)
        e_tail = jnp.exp(last - cum)
        kd = kk * e_tail

        a_mat = _hgrn_scores(q_s, k_s, cum_s)
        tri = _tri(CHUNK, True)
        da_s[...] = lax.dot_general(dov, v, _DN["nt"], preferred_element_type=F32) * tri
        dv = (lax.dot_general(a_mat, dov, _DN["tn"], preferred_element_type=F32)
              + lax.dot_general(kd, dst, _DN["nt"], preferred_element_type=F32))
        dk_state = jnp.dot(v, dst, preferred_element_type=F32) * e_tail
        dq_inter = jnp.dot(dov, st, preferred_element_type=F32) * e_cum
        dstate[...] = dst * e_last + lax.dot_general(dov, qe, _DN["tn"], preferred_element_type=F32)

        lane = lax.broadcasted_iota(jnp.int32, (SUB, CHUNK), 1)
        row = lax.broadcasted_iota(jnp.int32, (SUB, 1), 0)
        dk_s[...] = jnp.zeros_like(dk_s)
        for a in range(nsub):
            rows = pl.ds(SUB * a, SUB)
            ca = _boundary(cum_s, a)
            cum_a = cum_s[rows, :]
            q_a = q_s[rows, :]
            ea = jnp.exp(cum_a - ca)
            eb = jnp.exp(jnp.minimum(ca - cum, 0.0))
            da_a = da_s[rows, :]
            da_off = jnp.where(lane < SUB * a, da_a, 0.0)
            dq_a = ea * jnp.dot(da_off, kk * eb, preferred_element_type=F32)
            dk_s[...] += eb * lax.dot_general(da_off, q_a * ea, _DN["tn"], preferred_element_type=F32)
            dk_rows = jnp.zeros((SUB, dh), F32)
            for s in range(SUB):
                r = SUB * a + s
                e = jnp.exp(jnp.minimum(cum_a - cum_s[pl.ds(r, 1), :], 0.0))
                dcol = jnp.sum(jnp.where(lane == r, da_a, 0.0), axis=-1, keepdims=True)
                dcol = jnp.where(row >= s, dcol, 0.0)
                w = dcol * e
                dq_a = dq_a + w * k_s[pl.ds(r, 1), :]
                dk_rows = jnp.where(row == s, jnp.sum(w * q_a, axis=0, keepdims=True), dk_rows)
            dq_s[rows, :] = dq_a
            dk_s[rows, :] += dk_rows

        dq = dq_inter + dq_s[...]
        dk = dk_s[...] + dk_state
        d_last = (jnp.sum(dst * st, axis=0, keepdims=True) * e_last
                  + jnp.sum(kk * dk_state, axis=0, keepdims=True))
        rowc = lax.broadcasted_iota(jnp.int32, (CHUNK, 1), 0)
        dcum = q * dq - kk * dk + jnp.where(rowc == CHUNK - 1, d_last, 0.0)
        dg = _dot_hi(_tri(CHUNK, False), dcum)
        df = dg / f - dk
        dqb_ref[...] = (dq * _silu_grad(qb_ref[...])).astype(BF16)
        dfb_ref[...] = (df * (1.0 - lb) * sig * (1.0 - sig)).astype(BF16)
        dib_ref[...] = dv.astype(BF16)
        dlb_ref[...] += jnp.sum(df * (1.0 - sig), axis=0, keepdims=True)

    blk = lambda off: pl.BlockSpec((CHUNK, dh), lambda h, i: (n - 1 - i, off + h))
    out_blk = pl.BlockSpec((CHUNK, dh), lambda h, i: (n - 1 - i, h))
    return pl.pallas_call(
        body, name=name, grid=(HEADS, n),
        in_specs=[blk(cb["q_b"]), blk(cb["f_b"]), blk(cb["i_b"]), pl.BlockSpec((2, dh), lambda h, i: (0, h)),
                  pl.BlockSpec((None, None, dh, dh), lambda h, i: (h, n - 1 - i, 0, 0)), out_blk],
        out_specs=[out_blk, out_blk, out_blk, pl.BlockSpec((1, dh), lambda h, i: (0, h))],
        out_shape=[jax.ShapeDtypeStruct((t, WIDTH), BF16)] * 3 + [jax.ShapeDtypeStruct((1, WIDTH), F32)],
        scratch_shapes=[pltpu.VMEM((dh, dh), F32)] + [pltpu.VMEM((CHUNK, dh), F32)] * 3
        + [pltpu.VMEM((CHUNK, CHUNK), F32)] + [pltpu.VMEM((CHUNK, dh), F32)] * 2,
        compiler_params=_params(("parallel", "arbitrary")),
    )(proj, proj, proj, lb_logits, states, do)


def lb_bwd(dlb, lb_logits, name):
    def body(dlb_ref, lg_ref, o_ref):
        p0 = _lower_bound(lg_ref)
        d0 = dlb_ref[...] * p0 * (1.0 - p0)
        o_ref[0:1, :] = d0
        o_ref[1:2, :] = -d0

    return pl.pallas_call(body, name=name, out_shape=jax.ShapeDtypeStruct(lb_logits.shape, F32))(dlb, lb_logits)


def gnorm_fwd(o_raw, proj, norm_g, name):
    t = o_raw.shape[0]
    tr = _row_tile(t)
    cb = _col_blocks()
    dh = HEAD_DIM

    def body(o_ref, gb_ref, g_ref, y_ref):
        x = o_ref[...]
        y_ref[...] = (x * _rstd(x) * g_ref[...] * _silu(gb_ref[...])).astype(BF16)

    return pl.pallas_call(
        body, name=name, grid=(t // tr, HEADS),
        in_specs=[pl.BlockSpec((tr, dh), lambda i, h: (i, h)), pl.BlockSpec((tr, dh), lambda i, h: (i, cb["g_b"] + h)),
                  pl.BlockSpec((1, dh), lambda i, h: (0, 0))],
        out_specs=pl.BlockSpec((tr, dh), lambda i, h: (i, h)),
        out_shape=jax.ShapeDtypeStruct((t, WIDTH), BF16), compiler_params=_params(("parallel", "parallel")),
    )(o_raw, proj, norm_g)


def gnorm_bwd(dy, o_raw, proj, norm_g, name):
    t = o_raw.shape[0]
    tr = _row_tile(t)
    cb = _col_blocks()
    dh = HEAD_DIM

    def body(dy_ref, o_ref, gb_ref, g_ref, do_ref, dgb_ref, dg_ref):
        @pl.when((pl.program_id(0) == 0) & (pl.program_id(1) == 0))
        def _():
            dg_ref[...] = jnp.zeros_like(dg_ref)

        x = o_ref[...]
        gb = gb_ref[...]
        dyv = dy_ref[...]
        g = g_ref[...]
        dx, dg = _rms_bwd(x, g, dyv * _silu(gb))
        do_ref[...] = dx
        dgb_ref[...] = (dyv * (x * _rstd(x) * g) * _silu_grad(gb)).astype(BF16)
        dg_ref[...] += dg

    hb = pl.BlockSpec((tr, dh), lambda i, h: (i, h))
    vec = pl.BlockSpec((1, dh), lambda i, h: (0, 0))
    return pl.pallas_call(
        body, name=name, grid=(t // tr, HEADS),
        in_specs=[hb, hb, pl.BlockSpec((tr, dh), lambda i, h: (i, cb["g_b"] + h)), vec],
        out_specs=[hb, hb, vec],
        out_shape=[jax.ShapeDtypeStruct((t, WIDTH), F32), jax.ShapeDtypeStruct((t, WIDTH), BF16),
                   jax.ShapeDtypeStruct((1, dh), F32)],
        compiler_params=_params(("arbitrary", "arbitrary")),
    )(dy, o_raw, proj, norm_g)


def merge_fwd(proj, y, name):
    _, t, d = y.shape
    tr = _row_tile(t)
    tc = _pick(d, (1024, 512, 256, 128))
    cb = _col_blocks()
    ga, gb = cb["gate_a"] * LANES // tc, cb["gate_b"] * LANES // tc

    def body(ga_ref, gb_ref, y_ref, o_ref):
        o_ref[...] = (_sigmoid(ga_ref[...]) * y_ref[0] + _sigmoid(gb_ref[...]) * y_ref[1]).astype(BF16)

    return pl.pallas_call(
        body, name=name, grid=(t // tr, d // tc),
        in_specs=[pl.BlockSpec((tr, tc), lambda i, j: (i, ga + j)), pl.BlockSpec((tr, tc), lambda i, j: (i, gb + j)),
                  pl.BlockSpec((2, tr, tc), lambda i, j: (0, i, j))],
        out_specs=pl.BlockSpec((tr, tc), lambda i, j: (i, j)),
        out_shape=jax.ShapeDtypeStruct((t, d), BF16), compiler_params=_params(("parallel", "parallel")),
    )(proj, proj, y)


def merge_bwd(dm, proj, y, name):
    _, t, d = y.shape
    tr = _row_tile(t)
    tc = _pick(d, (1024, 512, 256, 128))
    cb = _col_blocks()
    ga, gb = cb["gate_a"] * LANES // tc, cb["gate_b"] * LANES // tc

    def body(dm_ref, ga_ref, gb_ref, y_ref, dg_ref, dy_ref):
        dmv = dm_ref[...]
        for idx, g_ref in enumerate((ga_ref, gb_ref)):
            s = _sigmoid(g_ref[...])
            dg_ref[idx] = (dmv * y_ref[idx] * s * (1.0 - s)).astype(BF16)
            dy_ref[idx] = (dmv * s).astype(BF16)

    pair = pl.BlockSpec((2, tr, tc), lambda i, j: (0, i, j))
    return pl.pallas_call(
        body, name=name, grid=(t // tr, d // tc),
        in_specs=[pl.BlockSpec((tr, tc), lambda i, j: (i, j)), pl.BlockSpec((tr, tc), lambda i, j: (i, ga + j)),
                  pl.BlockSpec((tr, tc), lambda i, j: (i, gb + j)), pair],
        out_specs=[pair, pair],
        out_shape=[jax.ShapeDtypeStruct((2, t, d), BF16)] * 2, compiler_params=_params(("parallel", "parallel")),
    )(dm, proj, proj, y)


def ple_tail(h, a, b, g, target, name):
    t, d = h.shape
    tr = _row_tile(t)

    def body(h_ref, a_ref, b_ref, g_ref, t_ref, loss_ref, dh_ref, da_ref, db_ref, dg_ref):
        @pl.when(pl.program_id(0) == 0)
        def _():
            loss_ref[...] = jnp.zeros_like(loss_ref)
            dg_ref[...] = jnp.zeros_like(dg_ref)

        s = _sigmoid(a_ref[...])
        bv = b_ref[...]
        z = s * bv
        gv = g_ref[...]
        err = h_ref[...] + z * _rstd(z) * gv - t_ref[...]
        loss_ref[...] += 0.5 * jnp.sum(jnp.sum(err * err, axis=-1, keepdims=True), axis=0, keepdims=True) / d
        dh = err / d
        dh_ref[...] = dh
        dz, dg = _rms_bwd(z, gv, dh)
        da_ref[...] = (dz * bv * s * (1.0 - s)).astype(BF16)
        db_ref[...] = (dz * s).astype(BF16)
        dg_ref[...] += dg

    row = pl.BlockSpec((tr, d), lambda i: (i, 0))
    vec = pl.BlockSpec((1, d), lambda i: (0, 0))
    return pl.pallas_call(
        body, name=name, grid=(t // tr,), in_specs=[row, row, row, vec, row],
        out_specs=[pl.BlockSpec((1, 1), lambda i: (0, 0)), row, row, row, vec],
        out_shape=[jax.ShapeDtypeStruct((1, 1), F32), jax.ShapeDtypeStruct((t, d), F32),
                   jax.ShapeDtypeStruct((t, d), BF16), jax.ShapeDtypeStruct((t, d), BF16),
                   jax.ShapeDtypeStruct((1, d), F32)],
        compiler_params=_params(("arbitrary",)),
    )(h, a, b, g, target)


def _ffn_fwd(h, pre_g, post_g, get_w, idx, tag):
    u = norm_in(h, pre_g, tag + "_norm")
    gu = mm_nn_col(u, get_w("gu" + idx, h), F32, tag + "_gate_up")
    act = swiglu_act(gu, tag + "_act")
    y = mm_nn_2d(act, get_w("down" + idx, gu), F32, tag + "_down")
    out = resid_post(h, y, post_g, MACARON_SCALE, tag + "_out")
    return out, (h, u, gu, act, y)


def _ffn_bwd(dh, saved, pre_g, post_g, get_w, emit, idx, tag):
    h, u, gu, act, y = saved
    dy, d_post = post_bwd(dh, y, post_g, MACARON_SCALE, tag + "_post_bwd")
    emit("down" + idx, mm_tn_2d(act, dy, F32, tag + "_dw_down"))
    dact = mm_nt_2d(dy, get_w("down" + idx), F32, tag + "_dact")
    dgu = swiglu_bwd(dact, gu, tag + "_act_bwd")
    emit("gu" + idx, mm_tn_col(u, dgu, N_CHIPS, F32, tag + "_dw_gate_up"))
    du = mm_nt_col(dgu, get_w("gu" + idx), F32, tag + "_du")
    dh_in, d_pre = pre_bwd(dh, h, pre_g, [du], tag + "_pre_bwd")
    return dh_in, d_pre, d_post


def _heads_col(a):
    t = a.shape[0]
    at = a[:, :HEADS].T
    return at.reshape(HEADS, t, 1), at.reshape(HEADS, 1, t)


def layer_step(x, p, target, gains, fox_bias, lb_logits, norm_g, get_w, emit):
    t = x.shape[0]
    h1, s1 = _ffn_fwd(x, gains["ffn1_pre"], gains["ffn1_post"], get_w, "1", "ffn1")

    u2 = norm_in(h1, gains["mix_pre"], "mix_norm")
    proj = mm_nt_2d(u2, get_w("in_main", h1), F32, "mix_in")
    fa = mm_nt_2d(u2, get_w("in_fa"), F32, "mix_in_fa")
    c = fox_prep(fa, fox_bias, "fox_prep")
    c_col, c_row = _heads_col(c)
    o_a = fox_fwd(proj, c_col, c_row, "fox_fwd")
    o_raw, states = hgrn_fwd(proj, lb_logits, "hgrn_fwd")
    o_b = gnorm_fwd(o_raw, proj, norm_g, "hgrn_norm")
    o_ab = jnp.stack([o_a, o_b])
    y_ab = _mm_branches(o_ab, get_w("proj", proj), "mix_proj")
    merged = merge_fwd(proj, y_ab, "mix_merge")
    mo = mm_nn_2d(merged, get_w("out"), F32, "mix_out")
    h2 = resid_post(h1, mo, gains["mix_post"], 1.0, "mix_resid")

    h3, s3 = _ffn_fwd(h2, gains["ffn2_pre"], gains["ffn2_post"], get_w, "2", "ffn2")

    u4 = norm_in(h3, gains["ple_pre"], "ple_norm")
    a4 = mm_nn_2d(u4, get_w("ple_gate"), F32, "ple_gate")
    b4 = mm_nn_col(p, get_w("ple_proj"), F32, "ple_proj")[0]
    loss, dh4, da4, db4, d_ple_post = ple_tail(h3, a4, b4, gains["ple_post"], target, "ple_tail")

    emit("ple_gate", mm_tn_2d(u4, da4, F32, "ple_dw_gate"))
    emit("ple_proj", mm_tn_col(p, db4[None], N_CHIPS, F32, "ple_dw_proj"))
    du4 = mm_nt_2d(da4, get_w("ple_gate"), F32, "ple_du")
    dh3, d_ple_pre = pre_bwd(dh4, h3, gains["ple_pre"], [du4], "ple_pre_bwd")

    dh2, d_f2_pre, d_f2_post = _ffn_bwd(dh3, s3, gains["ffn2_pre"], gains["ffn2_post"], get_w, emit, "2", "ffn2")

    dmo, d_mix_post = post_bwd(dh2, mo, gains["mix_post"], 1.0, "mix_post_bwd")
    emit("out", mm_tn_2d(merged, dmo, F32, "mix_dw_out"))
    dmerged = mm_nt_2d(dmo, get_w("out"), F32, "mix_dmerged")
    dgate, dy_ab = merge_bwd(dmerged, proj, y_ab, "mix_merge_bwd")
    emit("proj", _mm_branches_dw(o_ab, dy_ab, "mix_dw_proj"))
    do_ab = _mm_branches_bwd(dy_ab, get_w("proj"), "mix_do")
    do_raw, dg_b, d_norm_g = gnorm_bwd(do_ab[1], o_raw, proj, norm_g, "hgrn_norm_bwd")
    dq_b, df_b, di_b, dlb = hgrn_bwd(proj, lb_logits, states, do_raw, "hgrn_bwd")
    d_lb_logits = lb_bwd(dlb, lb_logits, "lb_bwd")
    dq_a, dk_a, dv_a, dc_col, dc_row = fox_bwd(proj, c_col, c_row, do_ab[0], "fox_bwd")
    dc = (dc_col.reshape(HEADS, t) + dc_row.reshape(HEADS, t)).T
    dc = jnp.pad(dc, ((0, 0), (0, LANES - HEADS)))
    dfa, d_fox_bias = fox_post_bwd(dc, fa, fox_bias, "fox_post_bwd")
    dproj = jnp.concatenate([dq_a, dk_a.astype(BF16), dv_a.astype(BF16), dq_b, df_b, di_b, dg_b,
                             dgate[0], dgate[1]], axis=1)
    emit("in_main", mm_tn_2d(dproj, u2, F32, "mix_dw_in"))
    emit("in_fa", mm_tn_2d(dfa, u2, F32, "mix_dw_in_fa"))
    du2a = mm_nn_2d(dproj, get_w("in_main"), F32, "mix_du")
    du2b = mm_nn_2d(dfa, get_w("in_fa"), F32, "mix_du_fa")
    dh1, d_mix_pre = pre_bwd(dh2, h1, gains["mix_pre"], [du2a, du2b], "mix_pre_bwd")

    dx, d_f1_pre, d_f1_post = _ffn_bwd(dh1, s1, gains["ffn1_pre"], gains["ffn1_post"], get_w, emit, "1", "ffn1")

    small = dict(ffn1_pre=d_f1_pre, ffn1_post=d_f1_post, mix_pre=d_mix_pre, mix_post=d_mix_post,
                 ffn2_pre=d_f2_pre, ffn2_post=d_f2_post, ple_pre=d_ple_pre, ple_post=d_ple_post,
                 fox_bias=d_fox_bias, lb_logits=d_lb_logits, norm_g=d_norm_g)
    return loss, dx, small


def _mm_branches(o_ab, w_proj, name):
    g, t, kk = o_ab.shape
    _, jn, _, ns = w_proj.shape
    tm = _pick(t, (512, 256, 128))
    return _mm(o_ab, w_proj, mode="nn", grid=(t // tm, g * jn, 1),
               a_spec=pl.BlockSpec((None, tm, kk), lambda i, j, k: (j // jn, i, 0)),
               b_spec=pl.BlockSpec((None, None, kk, ns), lambda i, j, k: (j // jn, j % jn, 0, 0)),
               o_spec=pl.BlockSpec((None, tm, ns), lambda i, j, k: (j // jn, i, j % jn)),
               out_shape=jax.ShapeDtypeStruct((g, t, jn * ns), F32), acc_shape=(tm, ns), name=name)


def _mm_branches_bwd(dy_ab, w_proj, name):
    g, t, _ = dy_ab.shape
    _, jn, kk, ns = w_proj.shape
    tm = _pick(t, (512, 256, 128))
    return _mm(dy_ab, w_proj, mode="nt", grid=(t // tm, g, jn),
               a_spec=pl.BlockSpec((None, tm, ns), lambda i, j, k: (j, i, k)),
               b_spec=pl.BlockSpec((None, None, kk, ns), lambda i, j, k: (j, k, 0, 0)),
               o_spec=pl.BlockSpec((None, tm, kk), lambda i, j, k: (j, i, 0)),
               out_shape=jax.ShapeDtypeStruct((g, t, kk), F32), acc_shape=(tm, kk), name=name)


def _mm_branches_dw(o_ab, dy_ab, name):
    g, t, kk = o_ab.shape
    d = dy_ab.shape[2]
    jn = N_CHIPS
    ns = d // jn
    return _mm(o_ab, dy_ab, mode="tn", grid=(1, g * jn, 1),
               a_spec=pl.BlockSpec((None, t, kk), lambda i, j, k: (j // jn, 0, 0)),
               b_spec=pl.BlockSpec((None, t, ns), lambda i, j, k: (j // jn, 0, j % jn)),
               o_spec=pl.BlockSpec((None, None, kk, ns), lambda i, j, k: (j // jn, j % jn, 0, 0)),
               out_shape=jax.ShapeDtypeStruct((g, jn, kk, ns), F32), acc_shape=(kk, ns), name=name)


HBM_SPEC = pl.BlockSpec(memory_space=pltpu.HBM)
SEM_SPEC = pl.BlockSpec(memory_space=pltpu.SEMAPHORE)
ANY_SPEC = pl.BlockSpec(memory_space=pl.ANY)
EFFECT = pltpu.SideEffectType.DATAFLOW_SIDE_EFFECTING


def _in_hbm(a):
    return pltpu.with_memory_space_constraint(a, pltpu.HBM)


def _place():
    x, y, c = lax.axis_index("x"), lax.axis_index("y"), lax.axis_index("c")
    chips = [(1 - x, y), (x, 1 - y), (1 - x, 1 - y)]
    return x, y, c, chips


def _half(shape, which, axis):
    n = shape[-2 + axis] // 2
    cut = pl.ds(which * n, n)
    return (cut, slice(None)) if axis == 0 else (slice(None), cut)


def _half_shape(shape, axis):
    s = list(shape)
    s[len(s) - 2 + axis] //= 2
    return tuple(s)


def _remote(src, dst, send_sems, recv_sems, k, to):
    return pltpu.make_async_remote_copy(src_ref=src, dst_ref=dst, send_sem=send_sems.at[k], recv_sem=recv_sems.at[k],
                                        device_id=to, device_id_type=MESH)


def split_start(name, srcs, lands, counts, copies):
    ns, nl, nset = len(srcs), len(lands), len(counts)

    def body(*refs):
        src_refs, land_refs = refs[:ns], refs[ns:ns + nl]
        sems = refs[ns + nl:ns + nl + 2 * nset]
        for s, plan in enumerate(copies(src_refs, land_refs)):
            for k, (src, dst, to) in enumerate(plan):
                _remote(src, dst, sems[2 * s], sems[2 * s + 1], k, to).start()

    out_shape = []
    for n in counts:
        out_shape += [pltpu.SemaphoreType.DMA((n,)), pltpu.SemaphoreType.DMA((n,))]
    out_shape += [pltpu.HBM(a.shape, a.dtype) for a in list(srcs) + list(lands)]
    res = pl.pallas_call(
        body, name=name, out_shape=tuple(out_shape), in_specs=[HBM_SPEC] * (ns + nl),
        out_specs=tuple([SEM_SPEC] * (2 * nset) + [HBM_SPEC] * (ns + nl)),
        input_output_aliases={i: 2 * nset + i for i in range(ns + nl)},
        compiler_params=pltpu.CompilerParams(has_side_effects=EFFECT),
    )(*[_in_hbm(a) for a in list(srcs) + list(lands)])
    sems = [(res[2 * s], res[2 * s + 1]) for s in range(nset)]
    return sems, list(res[2 * nset:2 * nset + ns]), list(res[2 * nset + ns:])


def split_wait(name, srcs, lands, sems, afters, copies):
    ns, nl, na = len(srcs), len(lands), len(afters)

    def body(*refs):
        src_refs, land_refs = refs[:ns], refs[ns:ns + nl]
        send_sems, recv_sems = refs[ns + nl:ns + nl + 2]
        for k, (src, dst, to) in enumerate(copies(src_refs, land_refs)):
            cp = _remote(src, dst, send_sems, recv_sems, k, to)
            cp.wait_send()
            cp.wait_recv()

    res = pl.pallas_call(
        body, name=name, out_shape=tuple(pltpu.HBM(a.shape, a.dtype) for a in list(srcs) + list(lands)),
        in_specs=[HBM_SPEC] * (ns + nl) + [SEM_SPEC, SEM_SPEC] + [ANY_SPEC] * na,
        out_specs=tuple([HBM_SPEC] * (ns + nl)), input_output_aliases={i: i for i in range(ns + nl)},
        compiler_params=pltpu.CompilerParams(has_side_effects=EFFECT),
    )(*srcs, *lands, sems[0], sems[1], *afters)
    return list(res[ns:])


def _gather_plan(blocks):
    def copies(src_refs, land_refs):
        x, y, c, chips = _place()
        j_me = 2 * x + y
        plan = []
        for si, li, g, axis in blocks:
            src, land = src_refs[si], land_refs[li].at[g]
            mine = _half(src.shape, c, axis)
            for px, py in chips:
                plan.append((src.at[mine], land.at[(j_me,) + mine], (px, py, c)))
            plan.append((src, land.at[j_me], (x, y, 1 - c)))
        return plan
    return copies


def _gather_arrivals(blocks):
    def copies(src_refs, land_refs):
        x, y, c, chips = _place()
        j_me = 2 * x + y
        plan = []
        for si, li, g, axis in blocks:
            src, land = src_refs[si], land_refs[li].at[g]
            mine = _half(src.shape, c, axis)
            for px, py in chips:
                plan.append((src.at[mine], land.at[(2 * px + py,) + mine], (px, py, c)))
            plan.append((src, land.at[j_me], (x, y, 1 - c)))
        return plan
    return copies


def gather_pass(name, lands, blocks):
    n = len(lands)

    def body(*refs):
        outs = refs[n:2 * n]
        send_sems, recv_sems = refs[2 * n:]
        x, y, c, chips = _place()
        sent = []
        for i, (li, g, axis) in enumerate(blocks):
            land = outs[li].at[g]
            mine = _half(land.shape[1:], c, axis)
            for k, (px, py) in enumerate(chips):
                part = land.at[(2 * px + py,) + mine]
                cp = _remote(part, part, send_sems, recv_sems, 3 * i + k, (x, y, 1 - c))
                cp.start()
                sent.append(cp)
        for i, (li, g, axis) in enumerate(blocks):
            land = outs[li].at[g]
            other = _half(land.shape[1:], 1 - c, axis)
            for k, (px, py) in enumerate(chips):
                part = land.at[(2 * px + py,) + other]
                _remote(part, part, send_sems, recv_sems, 3 * i + k, (x, y, 1 - c)).wait_recv()
        for cp in sent:
            cp.wait_send()

    m = 3 * len(blocks)
    return pl.pallas_call(
        body, name=name, in_specs=[ANY_SPEC] * n, out_specs=[ANY_SPEC] * n,
        out_shape=[jax.ShapeDtypeStruct(a.shape, a.dtype) for a in lands],
        input_output_aliases={i: i for i in range(n)},
        scratch_shapes=[pltpu.SemaphoreType.DMA((m,)), pltpu.SemaphoreType.DMA((m,))],
    )(*lands)


def pair_exchange(name, grads, axes):
    n = len(grads)

    def body(*refs):
        ins, outs = refs[:n], refs[n:2 * n]
        send_sems, recv_sems = refs[2 * n:]
        x, y, c, _ = _place()
        copies = []
        for i in range(n):
            theirs = (slice(None), slice(None)) + _half(ins[i].shape, 1 - c, axes[i])
            cp = _remote(ins[i].at[theirs], outs[i], send_sems, recv_sems, i, (x, y, 1 - c))
            cp.start()
            copies.append(cp)
        for cp in copies:
            cp.wait()

    out_shape = [jax.ShapeDtypeStruct(_half_shape(g.shape, a), g.dtype) for g, a in zip(grads, axes)]
    return pl.pallas_call(
        body, name=name, in_specs=[ANY_SPEC] * n, out_specs=[ANY_SPEC] * n, out_shape=out_shape,
        scratch_shapes=[pltpu.SemaphoreType.DMA((n,)), pltpu.SemaphoreType.DMA((n,))],
    )(*grads)


def _scatter_plan(n):
    def copies(src_refs, land_refs):
        x, y, c, chips = _place()
        return [(src_refs[i].at[:, 2 * px + py], land_refs[i].at[k], (px, py, c))
                for i in range(n) for k, (px, py) in enumerate(chips)]
    return copies


def pair_broadcast(name, bufs, axes):
    n = len(bufs)

    def body(*refs):
        outs = refs[n:2 * n]
        send_sems, recv_sems = refs[2 * n:]
        x, y, c, _ = _place()
        copies = []
        for i in range(n):
            mine = (slice(None),) + _half(outs[i].shape, c, axes[i])
            cp = _remote(outs[i].at[mine], outs[i].at[mine], send_sems, recv_sems, i, (x, y, 1 - c))
            cp.start()
            copies.append(cp)
        for i, cp in enumerate(copies):
            cp.wait_send()
            theirs = (slice(None),) + _half(outs[i].shape, 1 - c, axes[i])
            _remote(outs[i].at[theirs], outs[i].at[theirs], send_sems, recv_sems, i, (x, y, 1 - c)).wait_recv()

    return pl.pallas_call(
        body, name=name, in_specs=[ANY_SPEC] * n, out_specs=[ANY_SPEC] * n,
        out_shape=[jax.ShapeDtypeStruct(b.shape, b.dtype) for b in bufs],
        input_output_aliases={i: i for i in range(n)},
        scratch_shapes=[pltpu.SemaphoreType.DMA((n,)), pltpu.SemaphoreType.DMA((n,))],
    )(*bufs)


N_DEV = 8
SLAB_ROWS = 16


def allreduce_small(slab):
    def body(x_ref, o_ref, land, send_sems, recv_sems):
        x, y, c, _ = _place()
        me = 4 * x + 2 * y + c
        land[me] = x_ref[...]
        copies = []
        for d in range(1, N_DEV):
            to = (me + d) % N_DEV
            cp = pltpu.make_async_remote_copy(
                src_ref=x_ref, dst_ref=land.at[me], send_sem=send_sems.at[d - 1], recv_sem=recv_sems.at[me],
                device_id=(to // 4, (to // 2) % 2, to % 2), device_id_type=MESH)
            cp.start()
            copies.append(cp)
        for d in range(1, N_DEV):
            frm = (me + d) % N_DEV
            pltpu.make_async_remote_copy(
                src_ref=x_ref, dst_ref=land.at[frm], send_sem=send_sems.at[d - 1], recv_sem=recv_sems.at[frm],
                device_id=(frm // 4, (frm // 2) % 2, frm % 2), device_id_type=MESH).wait_recv()
        for cp in copies:
            cp.wait_send()
        acc = land[0]
        for s in range(1, N_DEV):
            acc = acc + land[s]
        o_ref[...] = acc

    vm = pl.BlockSpec(memory_space=pltpu.VMEM)
    return pl.pallas_call(
        body, name="allreduce_small", in_specs=[vm], out_specs=vm, out_shape=jax.ShapeDtypeStruct(slab.shape, F32),
        scratch_shapes=[pltpu.VMEM((N_DEV,) + slab.shape, F32), pltpu.SemaphoreType.DMA((N_DEV - 1,)),
                        pltpu.SemaphoreType.DMA((N_DEV,))],
    )(slab)


BLOCK_BYTES = 3 * 1024 * 1024


def _tiles_2d(r, c, budget=BLOCK_BYTES):
    if r % 8 == 0:
        tc = c if c % LANES else _pick(c, (2048, 1408, 1024, 512, 256, 128))
        tr = 8
        for cand in (512, 256, 128, 64, 32, 16, 8):
            if r % cand == 0 and cand * tc * 4 <= budget:
                tr = cand
                break
        return tr, tc
    tc = LANES
    for cand in (1024, 512, 256, 128):
        if c % cand == 0 and r * cand * 4 <= budget:
            tc = cand
            break
    return r, tc


def _grid_spec(grid, in_specs, out_specs):
    return pltpu.PrefetchScalarGridSpec(num_scalar_prefetch=1, grid=grid, in_specs=in_specs, out_specs=out_specs)


def _own(axis, nr, nc):
    if axis == 0:
        return lambda i, j, where: (where[1] * nr + i, j)
    return lambda i, j, where: (i, where[1] * nc + j)


def pair_add(where, grad, recv, axis, name):
    g, jn, hr, hc = recv.shape
    tr, tc = _tiles_2d(hr, hc)
    nr, nc = hr // tr, hc // tc
    own = _own(axis, nr, nc)

    def body(where_ref, a_ref, b_ref, o32_ref, o16_ref):
        s = a_ref[...] + b_ref[...]
        o32_ref[...] = s
        o16_ref[...] = s.astype(BF16)

    blk = pl.BlockSpec((None, None, tr, tc), lambda a, i, j, where: (a // jn, a % jn, i, j))
    mine = pl.BlockSpec((None, None, tr, tc), lambda a, i, j, where: (a // jn, a % jn) + own(i, j, where))
    return pl.pallas_call(
        body, name=name, grid_spec=_grid_spec((g * jn, nr, nc), [mine, blk], [blk, blk]),
        out_shape=[jax.ShapeDtypeStruct(recv.shape, F32), jax.ShapeDtypeStruct(recv.shape, BF16)],
        compiler_params=_params(("parallel", "parallel", "parallel")),
    )(where, grad, recv)


def chip_add(where, part, recv, axis, name):
    g, jn, hr, hc = part.shape
    tr, tc = _tiles_2d(hr, hc)
    nr, nc = hr // tr, hc // tc
    own = _own(axis, nr, nc)
    full = (g, 2 * hr, hc) if axis == 0 else (g, hr, 2 * hc)

    def body(where_ref, a_ref, b_ref, o_ref):
        s = a_ref[...]
        for k in range(3):
            s = s + b_ref[k].astype(F32)
        o_ref[...] = s

    return pl.pallas_call(
        body, name=name,
        grid_spec=_grid_spec((g, nr, nc),
                             [pl.BlockSpec((None, None, tr, tc), lambda a, i, j, where: (a, where[0], i, j)),
                              pl.BlockSpec((3, None, tr, tc), lambda a, i, j, where: (0, a, i, j))],
                             pl.BlockSpec((None, tr, tc), lambda a, i, j, where: (a,) + own(i, j, where))),
        out_shape=jax.ShapeDtypeStruct(full, F32), compiler_params=_params(("parallel", "parallel", "parallel")),
    )(where, part, recv)


def _adam_math(w, g, m, v):
    m2 = ADAM_B1 * m + (1.0 - ADAM_B1) * g
    v2 = ADAM_B2 * v + (1.0 - ADAM_B2) * (g * g)
    m_hat = m2 / (1.0 - ADAM_B1 ** ADAM_STEP)
    v_hat = v2 / (1.0 - ADAM_B2 ** ADAM_STEP)
    delta = -ADAM_LR * (m_hat / (jnp.sqrt(v_hat) + ADAM_EPS) + ADAM_WD * w)
    return delta, m2, v2


def adamw(grad, idx, w, m, v, name):
    _, r, cc = w.shape
    tr, tc = _tiles_2d(r, cc, BLOCK_BYTES // 2)

    def body(g_ref, w_ref, m_ref, v_ref, go_ref, d_ref, mo_ref, vo_ref):
        g = g_ref[...]
        delta, m2, v2 = _adam_math(w_ref[...], g, m_ref[...], v_ref[...])
        go_ref[...] = g
        d_ref[...] = delta
        mo_ref[...] = m2
        vo_ref[...] = v2

    blk = pl.BlockSpec((None, tr, tc), lambda i, j: (0, i, j))
    return pl.pallas_call(
        body, name=name, grid=(r // tr, cc // tc),
        in_specs=[pl.BlockSpec((None, tr, tc), lambda i, j: (idx, i, j)), blk, blk, blk], out_specs=[blk] * 4,
        out_shape=[jax.ShapeDtypeStruct(w.shape, F32)] * 4, compiler_params=_params(("parallel", "parallel")),
    )(grad, w, m, v)


def adamw_small(g, w, m, v):
    def body(g_ref, w_ref, m_ref, v_ref, d_ref, mo_ref, vo_ref):
        delta, m2, v2 = _adam_math(w_ref[...], g_ref[...], m_ref[...], v_ref[...])
        d_ref[...] = delta
        mo_ref[...] = m2
        vo_ref[...] = v2

    return pl.pallas_call(body, name="adamw_small", out_shape=[jax.ShapeDtypeStruct(w.shape, F32)] * 3)(g, w, m, v)


GAINS = ("ffn1_pre", "ffn1_post", "mix_pre", "mix_post", "ffn2_pre", "ffn2_post", "ple_pre", "ple_post")
WEIGHTS = ("ffn1_pre_g", "ffn1_post_g", "ffn1_w_gate", "ffn1_w_up", "ffn1_w_down", "mix_pre_g", "mix_post_g",
           "mix_w_in", "fox_f_bias", "hgrn_lb_logits", "hgrn_norm_g", "mix_w_proj_fox", "mix_w_proj_hgrn",
           "mix_w_out", "ffn2_pre_g", "ffn2_post_g", "ffn2_w_gate", "ffn2_w_up", "ffn2_w_down", "ple_pre_g",
           "ple_post_g", "ple_w_gate", "ple_w_proj")
GROUPS = dict(gu1=(("ffn1_w_gate", "ffn1_w_up"), 0), down1=(("ffn1_w_down",), 0), win=(("mix_w_in",), 1),
              proj=(("mix_w_proj_fox", "mix_w_proj_hgrn"), 0), out=(("mix_w_out",), 0),
              gu2=(("ffn2_w_gate", "ffn2_w_up"), 0), down2=(("ffn2_w_down",), 0), ple_gate=(("ple_w_gate",), 0),
              ple_proj=(("ple_w_proj",), 0))
TRANSPOSED = ("mix_w_in",)
ROW_BLOCKS = ("down1", "down2", "out", "ple_gate")
GATHER_SETS = (("gu1",), ("down1",), ("win", "proj", "out"), ("gu2", "down2", "ple_gate", "ple_proj"))
REDUCE_SETS = (("ple_gate", "ple_proj", "down2", "gu2"), ("out", "proj", "win"), ("down1",), ("gu1",))


def _pad_row(a, width):
    a = a.reshape(1, -1)
    return jnp.pad(a, ((0, 0), (0, width - a.shape[1])))


def _pack_small(vals):
    d = D_MODEL
    rows = [vals[n + "_g"].reshape(1, d) for n in GAINS]
    rows.append(_pad_row(vals["fox_f_bias"], d))
    lg = vals["hgrn_lb_logits"]
    rows += [_pad_row(lg[0], d), _pad_row(lg[1], d), _pad_row(vals["hgrn_norm_g"], d)]
    slab = jnp.concatenate(rows, axis=0)
    return jnp.pad(slab, ((0, SLAB_ROWS - slab.shape[0]), (0, 0)))


def _unpack_small(slab):
    out = {n + "_g": slab[i:i + 1] for i, n in enumerate(GAINS)}
    out["fox_f_bias"] = slab[8:9, :HEADS]
    out["hgrn_lb_logits"] = slab[9:11, :WIDTH]
    out["hgrn_norm_g"] = slab[11:12, :HEAD_DIM]
    return out


def _split_in(win_t):
    lo = 3 * WIDTH
    main = jnp.concatenate([win_t[:lo], win_t[lo + HEADS:]], axis=0)
    fa = jnp.pad(win_t[lo:lo + HEADS], ((0, LANES - HEADS), (0, 0)))
    return main, fa


def _join_in(main, fa):
    lo = 3 * WIDTH
    return jnp.concatenate([main[:lo], fa[:HEADS], main[lo:]], axis=0)


def _as_block(name, a):
    return jnp.swapaxes(a, 1, 2) if name in TRANSPOSED else a


def kernel(x, p, ffn1_pre_g, ffn1_post_g, ffn1_w_gate, ffn1_w_up, ffn1_w_down, mix_pre_g, mix_post_g, mix_w_in, fox_f_bias, hgrn_lb_logits, hgrn_norm_g, mix_w_proj_fox, mix_w_proj_hgrn, mix_w_out, ffn2_pre_g, ffn2_post_g, ffn2_w_gate, ffn2_w_up, ffn2_w_down, ple_pre_g, ple_post_g, ple_w_gate, ple_w_proj, loss_target, m_ffn1_pre_g, m_ffn1_post_g, m_ffn1_w_gate, m_ffn1_w_up, m_ffn1_w_down, m_mix_pre_g, m_mix_post_g, m_mix_w_in, m_fox_f_bias, m_hgrn_lb_logits, m_hgrn_norm_g, m_mix_w_proj_fox, m_mix_w_proj_hgrn, m_mix_w_out, m_ffn2_pre_g, m_ffn2_post_g, m_ffn2_w_gate, m_ffn2_w_up, m_ffn2_w_down, m_ple_pre_g, m_ple_post_g, m_ple_w_gate, m_ple_w_proj, v_ffn1_pre_g, v_ffn1_post_g, v_ffn1_w_gate, v_ffn1_w_up, v_ffn1_w_down, v_mix_pre_g, v_mix_post_g, v_mix_w_in, v_fox_f_bias, v_hgrn_lb_logits, v_hgrn_norm_g, v_mix_w_proj_fox, v_mix_w_proj_hgrn, v_mix_w_out, v_ffn2_pre_g, v_ffn2_post_g, v_ffn2_w_gate, v_ffn2_w_up, v_ffn2_w_down, v_ple_pre_g, v_ple_post_g, v_ple_w_gate, v_ple_w_proj):
    args = dict(locals())
    wts = {n: args[n] for n in WEIGHTS}
    mom = {n: args["m_" + n] for n in WEIGHTS}
    var = {n: args["v_" + n] for n in WEIGHTS}
    d = D_MODEL
    where = jnp.stack([2 * lax.axis_index("x") + lax.axis_index("y"), lax.axis_index("c")]).astype(jnp.int32)

    order = [g for s in GATHER_SETS for g in s]
    srcs, lands, plans = [], [], []
    for s in GATHER_SETS:
        blocks = []
        for g in s:
            names, axis = GROUPS[g]
            for pos, n in enumerate(names):
                blocks.append((len(srcs), order.index(g), pos, axis))
                srcs.append(_as_block(n, wts[n])[0].astype(BF16))
        plans.append(blocks)
    for g in order:
        names, _ = GROUPS[g]
        r, c = _as_block(names[0], wts[names[0]]).shape[1:]
        lands.append(lax.empty((len(names), N_CHIPS, r, c), BF16))
    sems, srcs, lands = split_start(
        "gather_start", srcs, lands, [4 * len(b) for b in plans],
        lambda sr, lr: [_gather_plan(b)(sr, lr) for b in plans])
    full = {}

    def land_set(si, after):
        blocks = plans[si]
        s_idx = sorted({b[0] for b in blocks})
        l_idx = sorted({b[1] for b in blocks})
        local = [(s_idx.index(a), l_idx.index(b), g, ax) for a, b, g, ax in blocks]
        got = split_wait("gather_wait_%d" % si, [srcs[i] for i in s_idx], [lands[i] for i in l_idx], sems[si],
                         [] if after is None else [after], _gather_arrivals(local))
        got = gather_pass("gather_pass_%d" % si, got, [(b, g, ax) for _, b, g, ax in local])
        for i, arr in zip(l_idx, got):
            g = order[i]
            if g == "win":
                full["win"] = arr
                full["in_main"], full["in_fa"] = _split_in(arr.reshape(-1, d))
            else:
                full[g] = arr.reshape(-1, d) if g in ROW_BLOCKS else arr

    def get_w(key, after=None):
        g = "win" if key in ("in_main", "in_fa") else key
        if g not in full:
            land_set([g in s for s in GATHER_SETS].index(True), after)
        return full[key]

    grads, started = {}, {}
    rows4 = lambda a: a.reshape(1, N_CHIPS, a.shape[0] // N_CHIPS, a.shape[1])

    def emit(key, grad):
        if key in ("in_main", "in_fa"):
            grads[key] = grad
            if "in_main" not in grads or "in_fa" not in grads:
                return
            key, grad = "win", rows4(_join_in(grads["in_main"], grads["in_fa"]))
        grads[key] = grad if grad.ndim == 4 else rows4(grad)
        for si, s in enumerate(REDUCE_SETS):
            if key in s and all(g in grads for g in s):
                axes = [GROUPS[g][1] for g in s]
                recv = pair_exchange("pair_exchange_%d" % si, [grads[g] for g in s], axes)
                sums = [pair_add(where, grads[g], r, a, "pair_add_" + g) for g, r, a in zip(s, recv, axes)]
                parts = [s16 for _, s16 in sums]
                zones = [lax.empty((3, q.shape[0]) + q.shape[2:], BF16) for q in parts]
                plan = _scatter_plan(len(s))
                sem, parts, zones = split_start("scatter_start_%d" % si, parts, zones, [3 * len(s)],
                                                lambda sr, lr: [plan(sr, lr)])
                started[si] = (sem[0], parts, zones, [s32 for s32, _ in sums], axes, plan)

    gains = {n: wts[n + "_g"] for n in GAINS}
    loss, dx, small = layer_step(x[0], p[0, 0].astype(BF16), loss_target[0], gains, _pad_row(fox_f_bias, LANES),
                                 hgrn_lb_logits, hgrn_norm_g, get_w, emit)

    small_named = {n + "_g": small[n] for n in GAINS}
    small_named.update(fox_f_bias=small["fox_bias"][:, :HEADS], hgrn_lb_logits=small["lb_logits"],
                       hgrn_norm_g=small["norm_g"])
    g_small = allreduce_small(_pack_small(small_named))
    d_small, m_small, v_small = adamw_small(g_small, _pack_small(wts), _pack_small(mom), _pack_small(var))

    out_g, out_d, out_m, out_v = {}, {}, {}, {}
    after = d_small
    for si, s in enumerate(REDUCE_SETS):
        sem, parts, zones, sums32, axes, plan = started[si]
        zones = split_wait("scatter_wait_%d" % si, parts, zones, sem, [dx, after], plan)
        halves = [chip_add(where, s32, z, a, "chip_add_" + g) for g, s32, z, a in zip(s, sums32, zones, axes)]
        reduced = pair_broadcast("pair_broadcast_%d" % si, halves, axes)
        for g, red in zip(s, reduced):
            for idx, n in enumerate(GROUPS[g][0]):
                res = adamw(red, idx, _as_block(n, wts[n]), _as_block(n, mom[n]), _as_block(n, var[n]), "adamw_" + n)
                out_g[n], out_d[n], out_m[n], out_v[n] = [_as_block(n, r) for r in res]
                after = res[1]

    for dst, slab in ((out_g, g_small), (out_d, d_small), (out_m, m_small), (out_v, v_small)):
        dst.update(_unpack_small(slab))

    total = lax.psum(loss[0, 0], ("x", "y", "c"))
    return (total, dx[None], *[out_g[n] for n in WEIGHTS], *[out_d[n] for n in WEIGHTS],
            *[out_m[n] for n in WEIGHTS], *[out_v[n] for n in WEIGHTS])
```

```python
import functools

import jax
import jax.numpy as jnp
from jax import lax
from jax.experimental import pallas as pl
from jax.experimental.pallas import tpu as pltpu

F32 = jnp.float32
BF16 = jnp.bfloat16

D_MODEL = 2048
SEQ = 2048
D_FF = 5632
PLE_DIM = 256
HEADS = 8
HEAD_DIM = 128
WIDTH = HEADS * HEAD_DIM
CHUNK = 64
SUB = 16
NORM_EPS = 1e-6
MACARON_SCALE = 0.5
N_CHIPS = 4

ADAM_LR = 0.001
ADAM_B1 = 0.9
ADAM_B2 = 0.999
ADAM_EPS = 1e-08
ADAM_WD = 0.01
ADAM_STEP = 10

LANES = 128
VMEM_LIMIT = 56 * 1024 * 1024
NEG_BIG = -1e30
MESH = pl.DeviceIdType.MESH


def _pick(n, cands):
    for c in cands:
        if c <= n and n % c == 0:
            return c
    return n


def _params(sem, vmem=VMEM_LIMIT):
    return pltpu.CompilerParams(dimension_semantics=sem, vmem_limit_bytes=vmem)


def _sigmoid(x):
    return 1.0 / (1.0 + jnp.exp(-x))


def _silu(x):
    return x * _sigmoid(x)


def _silu_grad(x):
    s = _sigmoid(x)
    return s * (1.0 + x * (1.0 - s))


_DN = {"nn": (((1,), (0,)), ((), ())), "nt": (((1,), (1,)), ((), ())), "tn": (((0,), (0,)), ((), ()))}


def _mm(a, b, *, mode, grid, a_spec, b_spec, o_spec, out_shape, acc_shape, name):
    nk = grid[2]
    dn = _DN[mode]

    def body(a_ref, b_ref, o_ref, acc_ref):
        k = pl.program_id(2)

        @pl.when(k == 0)
        def _():
            acc_ref[...] = jnp.zeros_like(acc_ref)

        acc_ref[...] += lax.dot_general(a_ref[...].astype(BF16), b_ref[...].astype(BF16), dn,
                                        preferred_element_type=F32)

        @pl.when(k == nk - 1)
        def _():
            o_ref[...] = acc_ref[...].astype(o_ref.dtype)

    return pl.pallas_call(
        body, name=name, grid=grid, in_specs=[a_spec, b_spec], out_specs=o_spec, out_shape=out_shape,
        scratch_shapes=[pltpu.VMEM(acc_shape, F32)],
        compiler_params=_params(("parallel", "parallel", "arbitrary")),
    )(a, b)


def mm_nn_2d(a, b, out_dtype, name):
    m, kk = a.shape
    n = b.shape[1]
    tm, tn, tk = _pick(m, (512, 256, 128)), _pick(n, (1024, 512, 256, 128)), _pick(kk, (2048, 1408, 1024, 512, 256, 128))
    return _mm(a, b, mode="nn", grid=(m // tm, n // tn, kk // tk),
               a_spec=pl.BlockSpec((tm, tk), lambda i, j, k: (i, k)),
               b_spec=pl.BlockSpec((tk, tn), lambda i, j, k: (k, j)),
               o_spec=pl.BlockSpec((tm, tn), lambda i, j, k: (i, j)),
               out_shape=jax.ShapeDtypeStruct((m, n), out_dtype), acc_shape=(tm, tn), name=name)


def mm_nt_2d(a, b, out_dtype, name):
    m, c = a.shape
    n = b.shape[0]
    tm, tn, tk = _pick(m, (512, 256, 128)), _pick(n, (1408, 1024, 512, 256, 128)), _pick(c, (2048, 1408, 1024, 512, 256, 128))
    return _mm(a, b, mode="nt", grid=(m // tm, n // tn, c // tk),
               a_spec=pl.BlockSpec((tm, tk), lambda i, j, k: (i, k)),
               b_spec=pl.BlockSpec((tn, tk), lambda i, j, k: (j, k)),
               o_spec=pl.BlockSpec((tm, tn), lambda i, j, k: (i, j)),
               out_shape=jax.ShapeDtypeStruct((m, n), out_dtype), acc_shape=(tm, tn), name=name)


def mm_tn_2d(a, b, out_dtype, name):
    c, m = a.shape
    n = b.shape[1]
    tm, tn, tk = _pick(m, (1408, 1024, 512, 256, 128)), _pick(n, (1024, 512, 256, 128)), _pick(c, (2048, 1024, 512, 256, 128))
    return _mm(a, b, mode="tn", grid=(m // tm, n // tn, c // tk),
               a_spec=pl.BlockSpec((tk, tm), lambda i, j, k: (k, i)),
               b_spec=pl.BlockSpec((tk, tn), lambda i, j, k: (k, j)),
               o_spec=pl.BlockSpec((tm, tn), lambda i, j, k: (i, j)),
               out_shape=jax.ShapeDtypeStruct((m, n), out_dtype), acc_shape=(tm, tn), name=name)


def mm_nn_col(a, w, out_dtype, name):
    m, kk = a.shape
    g, jn, _, ns = w.shape
    tm, tk = _pick(m, (512, 256, 128)), _pick(kk, (2048, 1024, 512, 256, 128))
    return _mm(a, w, mode="nn", grid=(m // tm, g * jn, kk // tk),
               a_spec=pl.BlockSpec((tm, tk), lambda i, j, k: (i, k)),
               b_spec=pl.BlockSpec((None, None, tk, ns), lambda i, j, k: (j // jn, j % jn, k, 0)),
               o_spec=pl.BlockSpec((None, tm, ns), lambda i, j, k: (j // jn, i, j % jn)),
               out_shape=jax.ShapeDtypeStruct((g, m, jn * ns), out_dtype), acc_shape=(tm, ns), name=name)


def mm_nt_col(a, w, out_dtype, name):
    g, m, _ = a.shape
    _, jn, kk, ns = w.shape
    tm, tn = _pick(m, (512, 256, 128)), _pick(kk, (1024, 512, 256, 128))
    return _mm(a, w, mode="nt", grid=(m // tm, kk // tn, g * jn),
               a_spec=pl.BlockSpec((None, tm, ns), lambda i, j, k: (k // jn, i, k % jn)),
               b_spec=pl.BlockSpec((None, None, tn, ns), lambda i, j, k: (k // jn, k % jn, j, 0)),
               o_spec=pl.BlockSpec((tm, tn), lambda i, j, k: (i, j)),
               out_shape=jax.ShapeDtypeStruct((m, kk), out_dtype), acc_shape=(tm, tn), name=name)


def mm_tn_col(a, b, jn, out_dtype, name):
    c, kk = a.shape
    g, _, n = b.shape
    ns = n // jn
    tm, tk = _pick(kk, (512, 256, 128)), _pick(c, (2048, 1024, 512, 256, 128))
    return _mm(a, b, mode="tn", grid=(kk // tm, g * jn, c // tk),
               a_spec=pl.BlockSpec((tk, tm), lambda i, j, k: (k, i)),
               b_spec=pl.BlockSpec((None, tk, ns), lambda i, j, k: (j // jn, k, j % jn)),
               o_spec=pl.BlockSpec((None, None, tm, ns), lambda i, j, k: (j // jn, j % jn, i, 0)),
               out_shape=jax.ShapeDtypeStruct((g, jn, kk, ns), out_dtype), acc_shape=(tm, ns), name=name)


def _rstd(x):
    return lax.rsqrt(jnp.mean(x * x, axis=-1, keepdims=True) + NORM_EPS)


def _rms_bwd(x, g, dy):
    r = _rstd(x)
    xn = x * r
    dyg = dy * g
    dx = r * (dyg - xn * jnp.mean(dyg * xn, axis=-1, keepdims=True))
    return dx, jnp.sum(dy * xn, axis=0, keepdims=True)


def _row_tile(t):
    return _pick(t, (256, 128, 64, 32, 16, 8))


def norm_in(h, g, name):
    t, d = h.shape
    tr = _row_tile(t)

    def body(h_ref, g_ref, u_ref):
        x = h_ref[...]
        u_ref[...] = (x * _rstd(x) * g_ref[...]).astype(BF16)

    return pl.pallas_call(
        body, name=name, grid=(t // tr,),
        in_specs=[pl.BlockSpec((tr, d), lambda i: (i, 0)), pl.BlockSpec((1, d), lambda i: (0, 0))],
        out_specs=pl.BlockSpec((tr, d), lambda i: (i, 0)),
        out_shape=jax.ShapeDtypeStruct((t, d), BF16), compiler_params=_params(("parallel",)),
    )(h, g)


def resid_post(h, y, g, scale, name):
    t, d = h.shape
    tr = _row_tile(t)

    def body(h_ref, y_ref, g_ref, o_ref):
        yv = y_ref[...]
        o_ref[...] = h_ref[...] + scale * (yv * _rstd(yv) * g_ref[...])

    row = pl.BlockSpec((tr, d), lambda i: (i, 0))
    return pl.pallas_call(
        body, name=name, grid=(t // tr,), in_specs=[row, row, pl.BlockSpec((1, d), lambda i: (0, 0))],
        out_specs=row, out_shape=jax.ShapeDtypeStruct((t, d), F32), compiler_params=_params(("parallel",)),
    )(h, y, g)


def post_bwd(dh, y, g, scale, name):
    t, d = dh.shape
    tr = _row_tile(t)

    def body(dh_ref, y_ref, g_ref, dy_ref, dg_ref):
        @pl.when(pl.program_id(0) == 0)
        def _():
            dg_ref[...] = jnp.zeros_like(dg_ref)

        dx, dg = _rms_bwd(y_ref[...], g_ref[...], scale * dh_ref[...])
        dy_ref[...] = dx.astype(BF16)
        dg_ref[...] += dg

    row = pl.BlockSpec((tr, d), lambda i: (i, 0))
    vec = pl.BlockSpec((1, d), lambda i: (0, 0))
    return pl.pallas_call(
        body, name=name, grid=(t // tr,), in_specs=[row, row, vec], out_specs=[row, vec],
        out_shape=[jax.ShapeDtypeStruct((t, d), BF16), jax.ShapeDtypeStruct((1, d), F32)],
        compiler_params=_params(("arbitrary",)),
    )(dh, y, g)


def pre_bwd(dh, h, g, dus, name):
    t, d = dh.shape
    tr = _row_tile(t)
    n_du = len(dus)

    def body(*refs):
        dh_ref, h_ref, g_ref = refs[:3]
        du_refs = refs[3:3 + n_du]
        o_ref, dg_ref = refs[3 + n_du:]

        @pl.when(pl.program_id(0) == 0)
        def _():
            dg_ref[...] = jnp.zeros_like(dg_ref)

        du = du_refs[0][...]
        for r in du_refs[1:]:
            du = du + r[...]
        dx, dg = _rms_bwd(h_ref[...], g_ref[...], du)
        o_ref[...] = dh_ref[...] + dx
        dg_ref[...] += dg

    row = pl.BlockSpec((tr, d), lambda i: (i, 0))
    vec = pl.BlockSpec((1, d), lambda i: (0, 0))
    return pl.pallas_call(
        body, name=name, grid=(t // tr,), in_specs=[row, row, vec] + [row] * n_du, out_specs=[row, vec],
        out_shape=[jax.ShapeDtypeStruct((t, d), F32), jax.ShapeDtypeStruct((1, d), F32)],
        compiler_params=_params(("arbitrary",)),
    )(dh, h, g, *dus)


def _ew_tiles(t, f):
    return _pick(t, (256, 128, 64, 32, 16, 8)), _pick(f, (1408, 1024, 512, 256, 128))


def swiglu_act(gu, name):
    _, t, f = gu.shape
    tr, tc = _ew_tiles(t, f)

    def body(gu_ref, o_ref):
        o_ref[...] = (_silu(gu_ref[0]) * gu_ref[1]).astype(BF16)

    return pl.pallas_call(
        body, name=name, grid=(t // tr, f // tc),
        in_specs=[pl.BlockSpec((2, tr, tc), lambda i, j: (0, i, j))],
        out_specs=pl.BlockSpec((tr, tc), lambda i, j: (i, j)),
        out_shape=jax.ShapeDtypeStruct((t, f), BF16), compiler_params=_params(("parallel", "parallel")),
    )(gu)


def swiglu_bwd(dact, gu, name):
    _, t, f = gu.shape
    tr, tc = _ew_tiles(t, f)

    def body(da_ref, gu_ref, o_ref):
        da = da_ref[...]
        gate = gu_ref[0]
        o_ref[0] = (da * gu_ref[1] * _silu_grad(gate)).astype(BF16)
        o_ref[1] = (da * _silu(gate)).astype(BF16)

    return pl.pallas_call(
        body, name=name, grid=(t // tr, f // tc),
        in_specs=[pl.BlockSpec((tr, tc), lambda i, j: (i, j)), pl.BlockSpec((2, tr, tc), lambda i, j: (0, i, j))],
        out_specs=pl.BlockSpec((2, tr, tc), lambda i, j: (0, i, j)),
        out_shape=jax.ShapeDtypeStruct((2, t, f), BF16), compiler_params=_params(("parallel", "parallel")),
    )(dact, gu)


def _col_blocks():
    w = WIDTH // LANES
    return dict(q_a=0, k_a=w, v_a=2 * w, q_b=3 * w, f_b=4 * w, i_b=5 * w, g_b=6 * w, gate_a=7 * w,
                gate_b=7 * w + D_MODEL // LANES)


def _tri(n, lower):
    r = lax.broadcasted_iota(jnp.int32, (n, n), 0)
    c = lax.broadcasted_iota(jnp.int32, (n, n), 1)
    return jnp.where((r >= c) if lower else (r <= c), 1.0, 0.0).astype(F32)


def _dot_hi(a, b):
    return jnp.dot(a, b, precision=lax.Precision.HIGHEST, preferred_element_type=F32)


def fox_prep(fa, bias, name):
    t, w = fa.shape
    tb = _pick(t, (256, 128, 64))

    def body(fa_ref, b_ref, c_ref, carry_ref):
        @pl.when(pl.program_id(0) == 0)
        def _():
            carry_ref[...] = jnp.zeros_like(carry_ref)

        z = fa_ref[...] + b_ref[...]
        lf = jnp.minimum(z, 0.0) - jnp.log(1.0 + jnp.exp(-jnp.abs(z)))
        c = _dot_hi(_tri(tb, True), lf) + carry_ref[...]
        c_ref[...] = c
        carry_ref[...] = carry_ref[...] + jnp.sum(lf, axis=0, keepdims=True)

    return pl.pallas_call(
        body, name=name, grid=(t // tb,),
        in_specs=[pl.BlockSpec((tb, w), lambda i: (i, 0)), pl.BlockSpec((1, w), lambda i: (0, 0))],
        out_specs=pl.BlockSpec((tb, w), lambda i: (i, 0)),
        out_shape=jax.ShapeDtypeStruct((t, w), F32), scratch_shapes=[pltpu.VMEM((1, w), F32)],
        compiler_params=_params(("arbitrary",)),
    )(fa, bias)


def fox_post_bwd(dc, fa, bias, name):
    t, w = fa.shape
    tb = _pick(t, (256, 128, 64))
    nb = t // tb

    def body(dc_ref, fa_ref, b_ref, dfa_ref, db_ref, carry_ref):
        @pl.when(pl.program_id(0) == 0)
        def _():
            carry_ref[...] = jnp.zeros_like(carry_ref)
            db_ref[...] = jnp.zeros_like(db_ref)

        dcv = dc_ref[...]
        dlf = _dot_hi(_tri(tb, False), dcv) + carry_ref[...]
        z = fa_ref[...] + b_ref[...]
        dz = dlf * _sigmoid(-z)
        dfa_ref[...] = dz.astype(BF16)
        db_ref[...] += jnp.sum(dz, axis=0, keepdims=True)
        carry_ref[...] = carry_ref[...] + jnp.sum(dcv, axis=0, keepdims=True)

    rev = pl.BlockSpec((tb, w), lambda i: (nb - 1 - i, 0))
    vec = pl.BlockSpec((1, w), lambda i: (0, 0))
    return pl.pallas_call(
        body, name=name, grid=(nb,), in_specs=[rev, rev, vec], out_specs=[rev, vec],
        out_shape=[jax.ShapeDtypeStruct((t, w), BF16), jax.ShapeDtypeStruct((1, w), F32)],
        scratch_shapes=[pltpu.VMEM((1, w), F32)], compiler_params=_params(("arbitrary",)),
    )(dc, fa, bias)


def _fox_probs(q_ref, k_ref, cc_ref, cr_ref, qi, tq, t):
    scale = HEAD_DIM ** -0.5
    s = lax.dot_general(q_ref[...].astype(BF16), k_ref[...].astype(BF16), _DN["nt"], preferred_element_type=F32)
    logits = s * scale + cc_ref[...] - cr_ref[...]
    qpos = qi * tq + lax.broadcasted_iota(jnp.int32, (tq, t), 0)
    kpos = lax.broadcasted_iota(jnp.int32, (tq, t), 1)
    logits = jnp.where(kpos <= qpos, logits, NEG_BIG)
    m = jnp.max(logits, axis=-1, keepdims=True)
    p = jnp.exp(logits - m)
    return p / jnp.sum(p, axis=-1, keepdims=True)


def fox_fwd(proj, c_col, c_row, name):
    t = proj.shape[0]
    tq = _pick(t, (256, 128))
    cb = _col_blocks()
    dh = HEAD_DIM

    def body(q_ref, k_ref, v_ref, cc_ref, cr_ref, o_ref):
        p = _fox_probs(q_ref, k_ref, cc_ref, cr_ref, pl.program_id(1), tq, t)
        o_ref[...] = jnp.dot(p.astype(BF16), v_ref[...].astype(BF16), preferred_element_type=F32).astype(BF16)

    return pl.pallas_call(
        body, name=name, grid=(HEADS, t // tq),
        in_specs=[pl.BlockSpec((tq, dh), lambda h, i: (i, cb["q_a"] + h)),
                  pl.BlockSpec((t, dh), lambda h, i: (0, cb["k_a"] + h)),
                  pl.BlockSpec((t, dh), lambda h, i: (0, cb["v_a"] + h)),
                  pl.BlockSpec((None, tq, 1), lambda h, i: (h, i, 0)),
                  pl.BlockSpec((None, 1, t), lambda h, i: (h, 0, 0))],
        out_specs=pl.BlockSpec((tq, dh), lambda h, i: (i, h)),
        out_shape=jax.ShapeDtypeStruct((t, WIDTH), BF16), compiler_params=_params(("parallel", "parallel")),
    )(proj, proj, proj, c_col, c_row)


def fox_bwd(proj, c_col, c_row, do, name):
    t = proj.shape[0]
    tq = _pick(t, (256, 128))
    cb = _col_blocks()
    dh = HEAD_DIM
    scale = HEAD_DIM ** -0.5

    def body(q_ref, k_ref, v_ref, cc_ref, cr_ref, do_ref, dq_ref, dk_ref, dv_ref, dcc_ref, dcr_ref):
        @pl.when(pl.program_id(1) == 0)
        def _():
            dk_ref[...] = jnp.zeros_like(dk_ref)
            dv_ref[...] = jnp.zeros_like(dv_ref)
            dcr_ref[...] = jnp.zeros_like(dcr_ref)

        p = _fox_probs(q_ref, k_ref, cc_ref, cr_ref, pl.program_id(1), tq, t)
        dov = do_ref[...].astype(BF16)
        kb = k_ref[...].astype(BF16)
        dv_ref[...] += lax.dot_general(p.astype(BF16), dov, _DN["tn"], preferred_element_type=F32)
        dp = lax.dot_general(dov, v_ref[...].astype(BF16), _DN["nt"], preferred_element_type=F32)
        ds = p * (dp - jnp.sum(p * dp, axis=-1, keepdims=True))
        dcc_ref[...] = jnp.sum(ds, axis=-1, keepdims=True)
        dcr_ref[...] -= jnp.sum(ds, axis=0, keepdims=True)
        dss = (ds * scale).astype(BF16)
        dq_ref[...] = jnp.dot(dss, kb, preferred_element_type=F32).astype(BF16)
        dk_ref[...] += lax.dot_general(dss, q_ref[...].astype(BF16), _DN["tn"], preferred_element_type=F32)

    return pl.pallas_call(
        body, name=name, grid=(HEADS, t // tq),
        in_specs=[pl.BlockSpec((tq, dh), lambda h, i: (i, cb["q_a"] + h)),
                  pl.BlockSpec((t, dh), lambda h, i: (0, cb["k_a"] + h)),
                  pl.BlockSpec((t, dh), lambda h, i: (0, cb["v_a"] + h)),
                  pl.BlockSpec((None, tq, 1), lambda h, i: (h, i, 0)),
                  pl.BlockSpec((None, 1, t), lambda h, i: (h, 0, 0)),
                  pl.BlockSpec((tq, dh), lambda h, i: (i, h))],
        out_specs=[pl.BlockSpec((tq, dh), lambda h, i: (i, h)),
                   pl.BlockSpec((t, dh), lambda h, i: (0, h)),
                   pl.BlockSpec((t, dh), lambda h, i: (0, h)),
                   pl.BlockSpec((None, tq, 1), lambda h, i: (h, i, 0)),
                   pl.BlockSpec((None, 1, t), lambda h, i: (h, 0, 0))],
        out_shape=[jax.ShapeDtypeStruct((t, WIDTH), BF16), jax.ShapeDtypeStruct((t, WIDTH), F32),
                   jax.ShapeDtypeStruct((t, WIDTH), F32), jax.ShapeDtypeStruct((HEADS, t, 1), F32),
                   jax.ShapeDtypeStruct((HEADS, 1, t), F32)],
        compiler_params=_params(("parallel", "arbitrary")),
    )(proj, proj, proj, c_col, c_row, do)


def _lower_bound(lg_ref):
    l0 = lg_ref[0:1, :]
    l1 = lg_ref[1:2, :]
    m = jnp.maximum(l0, l1)
    e0 = jnp.exp(l0 - m)
    e1 = jnp.exp(l1 - m)
    return e0 / (e0 + e1)


def _hgrn_inputs(qb_ref, fb_ref, lg_ref, q_s, k_s, cum_s):
    lb = _lower_bound(lg_ref)
    sig = _sigmoid(fb_ref[...])
    f = lb + (1.0 - lb) * sig
    q_s[...] = _silu(qb_ref[...])
    k_s[...] = 1.0 - f
    cum_s[...] = _dot_hi(_tri(CHUNK, True), jnp.log(f))
    return lb, sig, f


def _boundary(cum_s, a):
    if a == 0:
        return jnp.zeros((1, HEAD_DIM), F32)
    return cum_s[pl.ds(SUB * a - 1, 1), :]


def _hgrn_scores(q_s, k_s, cum_s):
    cum = cum_s[...]
    kk = k_s[...]
    lane = lax.broadcasted_iota(jnp.int32, (SUB, CHUNK), 1)
    row = lax.broadcasted_iota(jnp.int32, (SUB, 1), 0)
    blocks = []
    for a in range(CHUNK // SUB):
        rows = pl.ds(SUB * a, SUB)
        ca = _boundary(cum_s, a)
        cum_a = cum_s[rows, :]
        q_a = q_s[rows, :]
        qa = q_a * jnp.exp(cum_a - ca)
        ka = kk * jnp.exp(jnp.minimum(ca - cum, 0.0))
        blk = lax.dot_general(qa, ka, _DN["nt"], preferred_element_type=F32)
        blk = jnp.where(lane < SUB * a, blk, 0.0)
        for s in range(SUB):
            r = SUB * a + s
            e = jnp.exp(jnp.minimum(cum_a - cum_s[pl.ds(r, 1), :], 0.0))
            col = jnp.sum(q_a * k_s[pl.ds(r, 1), :] * e, axis=-1, keepdims=True)
            col = jnp.where(row >= s, col, 0.0)
            blk = jnp.where(lane == r, col, blk)
        blocks.append(blk)
    return jnp.concatenate(blocks, axis=0)


def hgrn_fwd(proj, lb_logits, name):
    t = proj.shape[0]
    n = t // CHUNK
    cb = _col_blocks()
    dh = HEAD_DIM

    def body(qb_ref, fb_ref, ib_ref, lg_ref, o_ref, st_ref, state, q_s, k_s, cum_s):
        @pl.when(pl.program_id(1) == 0)
        def _():
            state[...] = jnp.zeros_like(state)

        _hgrn_inputs(qb_ref, fb_ref, lg_ref, q_s, k_s, cum_s)
        st = state[...]
        st_ref[...] = st
        cum = cum_s[...]
        v = ib_ref[...]
        qe = q_s[...] * jnp.exp(cum)
        inter = lax.dot_general(qe, st, _DN["nt"], preferred_element_type=F32)
        a_mat = _hgrn_scores(q_s, k_s, cum_s)
        o_ref[...] = inter + jnp.dot(a_mat, v, preferred_element_type=F32)
        last = cum_s[pl.ds(CHUNK - 1, 1), :]
        kd = k_s[...] * jnp.exp(last - cum)
        state[...] = st * jnp.exp(last) + lax.dot_general(v, kd, _DN["tn"], preferred_element_type=F32)

    blk = lambda off: pl.BlockSpec((CHUNK, dh), lambda h, i: (i, off + h))
    return pl.pallas_call(
        body, name=name, grid=(HEADS, n),
        in_specs=[blk(cb["q_b"]), blk(cb["f_b"]), blk(cb["i_b"]), pl.BlockSpec((2, dh), lambda h, i: (0, h))],
        out_specs=[pl.BlockSpec((CHUNK, dh), lambda h, i: (i, h)),
                   pl.BlockSpec((None, None, dh, dh), lambda h, i: (h, i, 0, 0))],
        out_shape=[jax.ShapeDtypeStruct((t, WIDTH), F32), jax.ShapeDtypeStruct((HEADS, n, dh, dh), F32)],
        scratch_shapes=[pltpu.VMEM((dh, dh), F32)] + [pltpu.VMEM((CHUNK, dh), F32)] * 3,
        compiler_params=_params(("parallel", "arbitrary")),
    )(proj, proj, proj, lb_logits)


def hgrn_bwd(proj, lb_logits, states, do, name):
    t = proj.shape[0]
    n = t // CHUNK
    cb = _col_blocks()
    dh = HEAD_DIM
    nsub = CHUNK // SUB

    def body(qb_ref, fb_ref, ib_ref, lg_ref, st_ref, do_ref, dqb_ref, dfb_ref, dib_ref, dlb_ref,
             dstate, q_s, k_s, cum_s, da_s, dq_s, dk_s):
        @pl.when(pl.program_id(1) == 0)
        def _():
            dstate[...] = jnp.zeros_like(dstate)
            dlb_ref[...] = jnp.zeros_like(dlb_ref)

        lb, sig, f = _hgrn_inputs(qb_ref, fb_ref, lg_ref, q_s, k_s, cum_s)
        st = st_ref[...]
        dst = dstate[...]
        cum = cum_s[...]
        q = q_s[...]
        kk = k_s[...]
        v = ib_ref[...]
        dov = do_ref[...]
        e_cum = jnp.exp(cum)
        qe = q * e_cum
        last = cum_s[pl.ds(CHUNK - 1, 1), :]
        e_last = jnp.exp(last)
        e_tail = jnp.exp(last - cum)
        kd = kk * e_tail

        a_mat = _hgrn_scores(q_s, k_s, cum_s)
        tri = _tri(CHUNK, True)
        da_s[...] = lax.dot_general(dov, v, _DN["nt"], preferred_element_type=F32) * tri
        dv = (lax.dot_general(a_mat, dov, _DN["tn"], preferred_element_type=F32)
              + lax.dot_general(kd, dst, _DN["nt"], preferred_element_type=F32))
        dk_state = jnp.dot(v, dst, preferred_element_type=F32) * e_tail
        dq_inter = jnp.dot(dov, st, preferred_element_type=F32) * e_cum
        dstate[...] = dst * e_last + lax.dot_general(dov, qe, _DN["tn"], preferred_element_type=F32)

        lane = lax.broadcasted_iota(jnp.int32, (SUB, CHUNK), 1)
        row = lax.broadcasted_iota(jnp.int32, (SUB, 1), 0)
        dk_s[...] = jnp.zeros_like(dk_s)
        for a in range(nsub):
            rows = pl.ds(SUB * a, SUB)
            ca = _boundary(cum_s, a)
            cum_a = cum_s[rows, :]
            q_a = q_s[rows, :]
            ea = jnp.exp(cum_a - ca)
            eb = jnp.exp(jnp.minimum(ca - cum, 0.0))
            da_a = da_s[rows, :]
            da_off = jnp.where(lane < SUB * a, da_a, 0.0)
            dq_a = ea * jnp.dot(da_off, kk * eb, preferred_element_type=F32)
            dk_s[...] += eb * lax.dot_general(da_off, q_a * ea, _DN["tn"], preferred_element_type=F32)
            dk_rows = jnp.zeros((SUB, dh), F32)
            for s in range(SUB):
                r = SUB * a + s
                e = jnp.exp(jnp.minimum(cum_a - cum_s[pl.ds(r, 1), :], 0.0))
                dcol = jnp.sum(jnp.where(lane == r, da_a, 0.0), axis=-1, keepdims=True)
                dcol = jnp.where(row >= s, dcol, 0.0)
                w = dcol * e
                dq_a = dq_a + w * k_s[pl.ds(r, 1), :]
                dk_rows = jnp.where(row == s, jnp.sum(w * q_a, axis=0, keepdims=True), dk_rows)
            dq_s[rows, :] = dq_a
            dk_s[rows, :] += dk_rows

        dq = dq_inter + dq_s[...]
        dk = dk_s[...] + dk_state
        d_last = (jnp.sum(dst * st, axis=0, keepdims=True) * e_last
                  + jnp.sum(kk * dk_state, axis=0, keepdims=True))
        rowc = lax.broadcasted_iota(jnp.int32, (CHUNK, 1), 0)
        dcum = q * dq - kk * dk + jnp.where(rowc == CHUNK - 1, d_last, 0.0)
        dg = _dot_hi(_tri(CHUNK, False), dcum)
        df = dg / f - dk
        dqb_ref[...] = (dq * _silu_grad(qb_ref[...])).astype(BF16)
        dfb_ref[...] = (df * (1.0 - lb) * sig * (1.0 - sig)).astype(BF16)
        dib_ref[...] = dv.astype(BF16)
        dlb_ref[...] += jnp.sum(df * (1.0 - sig), axis=0, keepdims=True)

    blk = lambda off: pl.BlockSpec((CHUNK, dh), lambda h, i: (n - 1 - i, off + h))
    out_blk = pl.BlockSpec((CHUNK, dh), lambda h, i: (n - 1 - i, h))
    return pl.pallas_call(
        body, name=name, grid=(HEADS, n),
        in_specs=[blk(cb["q_b"]), blk(cb["f_b"]), blk(cb["i_b"]), pl.BlockSpec((2, dh), lambda h, i: (0, h)),
                  pl.BlockSpec((None, None, dh, dh), lambda h, i: (h, n - 1 - i, 0, 0)), out_blk],
        out_specs=[out_blk, out_blk, out_blk, pl.BlockSpec((1, dh), lambda h, i: (0, h))],
        out_shape=[jax.ShapeDtypeStruct((t, WIDTH), BF16)] * 3 + [jax.ShapeDtypeStruct((1, WIDTH), F32)],
        scratch_shapes=[pltpu.VMEM((dh, dh), F32)] + [pltpu.VMEM((CHUNK, dh), F32)] * 3
        + [pltpu.VMEM((CHUNK, CHUNK), F32)] + [pltpu.VMEM((CHUNK, dh), F32)] * 2,
        compiler_params=_params(("parallel", "arbitrary")),
    )(proj, proj, proj, lb_logits, states, do)


def lb_bwd(dlb, lb_logits, name):
    def body(dlb_ref, lg_ref, o_ref):
        p0 = _lower_bound(lg_ref)
        d0 = dlb_ref[...] * p0 * (1.0 - p0)
        o_ref[0:1, :] = d0
        o_ref[1:2, :] = -d0

    return pl.pallas_call(body, name=name, out_shape=jax.ShapeDtypeStruct(lb_logits.shape, F32))(dlb, lb_logits)


def gnorm_fwd(o_raw, proj, norm_g, name):
    t = o_raw.shape[0]
    tr = _row_tile(t)
    cb = _col_blocks()
    dh = HEAD_DIM

    def body(o_ref, gb_ref, g_ref, y_ref):
        x = o_ref[...]
        y_ref[...] = (x * _rstd(x) * g_ref[...] * _silu(gb_ref[...])).astype(BF16)

    return pl.pallas_call(
        body, name=name, grid=(t // tr, HEADS),
        in_specs=[pl.BlockSpec((tr, dh), lambda i, h: (i, h)), pl.BlockSpec((tr, dh), lambda i, h: (i, cb["g_b"] + h)),
                  pl.BlockSpec((1, dh), lambda i, h: (0, 0))],
        out_specs=pl.BlockSpec((tr, dh), lambda i, h: (i, h)),
        out_shape=jax.ShapeDtypeStruct((t, WIDTH), BF16), compiler_params=_params(("parallel", "parallel")),
    )(o_raw, proj, norm_g)


def gnorm_bwd(dy, o_raw, proj, norm_g, name):
    t = o_raw.shape[0]
    tr = _row_tile(t)
    cb = _col_blocks()
    dh = HEAD_DIM

    def body(dy_ref, o_ref, gb_ref, g_ref, do_ref, dgb_ref, dg_ref):
        @pl.when((pl.program_id(0) == 0) & (pl.program_id(1) == 0))
        def _():
            dg_ref[...] = jnp.zeros_like(dg_ref)

        x = o_ref[...]
        gb = gb_ref[...]
        dyv = dy_ref[...]
        g = g_ref[...]
        dx, dg = _rms_bwd(x, g, dyv * _silu(gb))
        do_ref[...] = dx
        dgb_ref[...] = (dyv * (x * _rstd(x) * g) * _silu_grad(gb)).astype(BF16)
        dg_ref[...] += dg

    hb = pl.BlockSpec((tr, dh), lambda i, h: (i, h))
    vec = pl.BlockSpec((1, dh), lambda i, h: (0, 0))
    return pl.pallas_call(
        body, name=name, grid=(t // tr, HEADS),
        in_specs=[hb, hb, pl.BlockSpec((tr, dh), lambda i, h: (i, cb["g_b"] + h)), vec],
        out_specs=[hb, hb, vec],
        out_shape=[jax.ShapeDtypeStruct((t, WIDTH), F32), jax.ShapeDtypeStruct((t, WIDTH), BF16),
                   jax.ShapeDtypeStruct((1, dh), F32)],
        compiler_params=_params(("arbitrary", "arbitrary")),
    )(dy, o_raw, proj, norm_g)


def merge_fwd(proj, y, name):
    _, t, d = y.shape
    tr = _row_tile(t)
    tc = _pick(d, (1024, 512, 256, 128))
    cb = _col_blocks()
    ga, gb = cb["gate_a"] * LANES // tc, cb["gate_b"] * LANES // tc

    def body(ga_ref, gb_ref, y_ref, o_ref):
        o_ref[...] = (_sigmoid(ga_ref[...]) * y_ref[0] + _sigmoid(gb_ref[...]) * y_ref[1]).astype(BF16)

    return pl.pallas_call(
        body, name=name, grid=(t // tr, d // tc),
        in_specs=[pl.BlockSpec((tr, tc), lambda i, j: (i, ga + j)), pl.BlockSpec((tr, tc), lambda i, j: (i, gb + j)),
                  pl.BlockSpec((2, tr, tc), lambda i, j: (0, i, j))],
        out_specs=pl.BlockSpec((tr, tc), lambda i, j: (i, j)),
        out_shape=jax.ShapeDtypeStruct((t, d), BF16), compiler_params=_params(("parallel", "parallel")),
    )(proj, proj, y)


def merge_bwd(dm, proj, y, name):
    _, t, d = y.shape
    tr = _row_tile(t)
    tc = _pick(d, (1024, 512, 256, 128))
    cb = _col_blocks()
    ga, gb = cb["gate_a"] * LANES // tc, cb["gate_b"] * LANES // tc

    def body(dm_ref, ga_ref, gb_ref, y_ref, dg_ref, dy_ref):
        dmv = dm_ref[...]
        for idx, g_ref in enumerate((ga_ref, gb_ref)):
            s = _sigmoid(g_ref[...])
            dg_ref[idx] = (dmv * y_ref[idx] * s * (1.0 - s)).astype(BF16)
            dy_ref[idx] = (dmv * s).astype(BF16)

    pair = pl.BlockSpec((2, tr, tc), lambda i, j: (0, i, j))
    return pl.pallas_call(
        body, name=name, grid=(t // tr, d // tc),
        in_specs=[pl.BlockSpec((tr, tc), lambda i, j: (i, j)), pl.BlockSpec((tr, tc), lambda i, j: (i, ga + j)),
                  pl.BlockSpec((tr, tc), lambda i, j: (i, gb + j)), pair],
        out_specs=[pair, pair],
        out_shape=[jax.ShapeDtypeStruct((2, t, d), BF16)] * 2, compiler_params=_params(("parallel", "parallel")),
    )(dm, proj, proj, y)


def ple_tail(h, a, b, g, target, name):
    t, d = h.shape
    tr = _row_tile(t)

    def body(h_ref, a_ref, b_ref, g_ref, t_ref, loss_ref, dh_ref, da_ref, db_ref, dg_ref):
        @pl.when(pl.program_id(0) == 0)
        def _():
            loss_ref[...] = jnp.zeros_like(loss_ref)
            dg_ref[...] = jnp.zeros_like(dg_ref)

        s = _sigmoid(a_ref[...])
        bv = b_ref[...]
        z = s * bv
        gv = g_ref[...]
        err = h_ref[...] + z * _rstd(z) * gv - t_ref[...]
        loss_ref[...] += 0.5 * jnp.sum(jnp.sum(err * err, axis=-1, keepdims=True), axis=0, keepdims=True) / d
        dh = err / d
        dh_ref[...] = dh
        dz, dg = _rms_bwd(z, gv, dh)
        da_ref[...] = (dz * bv * s * (1.0 - s)).astype(BF16)
        db_ref[...] = (dz * s).astype(BF16)
        dg_ref[...] += dg

    row = pl.BlockSpec((tr, d), lambda i: (i, 0))
    vec = pl.BlockSpec((1, d), lambda i: (0, 0))
    return pl.pallas_call(
        body, name=name, grid=(t // tr,), in_specs=[row, row, row, vec, row],
        out_specs=[pl.BlockSpec((1, 1), lambda i: (0, 0)), row, row, row, vec],
        out_shape=[jax.ShapeDtypeStruct((1, 1), F32), jax.ShapeDtypeStruct((t, d), F32),
                   jax.ShapeDtypeStruct((t, d), BF16), jax.ShapeDtypeStruct((t, d), BF16),
                   jax.ShapeDtypeStruct((1, d), F32)],
        compiler_params=_params(("arbitrary",)),
    )(h, a, b, g, target)


def _ffn_fwd(h, pre_g, post_g, get_w, idx, tag):
    u = norm_in(h, pre_g, tag + "_norm")
    gu = mm_nn_col(u, get_w("gu" + idx, h), F32, tag + "_gate_up")
    act = swiglu_act(gu, tag + "_act")
    y = mm_nn_2d(act, get_w("down" + idx, gu), F32, tag + "_down")
    out = resid_post(h, y, post_g, MACARON_SCALE, tag + "_out")
    return out, (h, u, gu, act, y)


def _behind(g, marks):
    for m in marks:
        if m is not None:
            g = g + m
    return g


def _ffn_bwd(dh, saved, pre_g, post_g, get_w, emit, idx, tag):
    h, u, gu, act, y = saved
    dy, d_post = post_bwd(dh, y, post_g, MACARON_SCALE, tag + "_post_bwd")
    marks = [emit("down" + idx, mm_tn_2d(act, dy, F32, tag + "_dw_down"))]
    dact = mm_nt_2d(dy, get_w("down" + idx), F32, tag + "_dact")
    dgu = swiglu_bwd(dact, gu, tag + "_act_bwd")
    marks.append(emit("gu" + idx, mm_tn_col(u, dgu, N_CHIPS, F32, tag + "_dw_gate_up")))
    du = mm_nt_col(dgu, get_w("gu" + idx), F32, tag + "_du")
    dh_in, d_pre = pre_bwd(dh, h, _behind(pre_g, marks), [du], tag + "_pre_bwd")
    return dh_in, d_pre, d_post


def _heads_col(a):
    t = a.shape[0]
    at = a[:, :HEADS].T
    return at.reshape(HEADS, t, 1), at.reshape(HEADS, 1, t)


def layer_step(x, p, target, gains, fox_bias, lb_logits, norm_g, get_w, emit):
    t = x.shape[0]
    h1, s1 = _ffn_fwd(x, gains["ffn1_pre"], gains["ffn1_post"], get_w, "1", "ffn1")

    u2 = norm_in(h1, gains["mix_pre"], "mix_norm")
    proj = mm_nt_2d(u2, get_w("in_main", h1), F32, "mix_in")
    fa = mm_nt_2d(u2, get_w("in_fa"), F32, "mix_in_fa")
    c = fox_prep(fa, fox_bias, "fox_prep")
    c_col, c_row = _heads_col(c)
    o_a = fox_fwd(proj, c_col, c_row, "fox_fwd")
    o_raw, states = hgrn_fwd(proj, lb_logits, "hgrn_fwd")
    o_b = gnorm_fwd(o_raw, proj, norm_g, "hgrn_norm")
    o_ab = jnp.stack([o_a, o_b])
    y_ab = _mm_branches(o_ab, get_w("proj", proj), "mix_proj")
    merged = merge_fwd(proj, y_ab, "mix_merge")
    mo = mm_nn_2d(merged, get_w("out"), F32, "mix_out")
    h2 = resid_post(h1, mo, gains["mix_post"], 1.0, "mix_resid")

    h3, s3 = _ffn_fwd(h2, gains["ffn2_pre"], gains["ffn2_post"], get_w, "2", "ffn2")

    u4 = norm_in(h3, gains["ple_pre"], "ple_norm")
    a4 = mm_nn_2d(u4, get_w("ple_gate"), F32, "ple_gate")
    b4 = mm_nn_col(p, get_w("ple_proj"), F32, "ple_proj")[0]
    loss, dh4, da4, db4, d_ple_post = ple_tail(h3, a4, b4, gains["ple_post"], target, "ple_tail")

    marks = [emit("ple_gate", mm_tn_2d(u4, da4, F32, "ple_dw_gate")),
             emit("ple_proj", mm_tn_col(p, db4[None], N_CHIPS, F32, "ple_dw_proj"))]
    du4 = mm_nt_2d(da4, get_w("ple_gate"), F32, "ple_du")
    dh3, d_ple_pre = pre_bwd(dh4, h3, _behind(gains["ple_pre"], marks), [du4], "ple_pre_bwd")

    dh2, d_f2_pre, d_f2_post = _ffn_bwd(dh3, s3, gains["ffn2_pre"], gains["ffn2_post"], get_w, emit, "2", "ffn2")

    dmo, d_mix_post = post_bwd(dh2, mo, gains["mix_post"], 1.0, "mix_post_bwd")
    marks = [emit("out", mm_tn_2d(merged, dmo, F32, "mix_dw_out"))]
    dmerged = mm_nt_2d(dmo, get_w("out"), F32, "mix_dmerged")
    dgate, dy_ab = merge_bwd(dmerged, proj, y_ab, "mix_merge_bwd")
    marks.append(emit("proj", _mm_branches_dw(o_ab, dy_ab, "mix_dw_proj")))
    do_ab = _mm_branches_bwd(dy_ab, get_w("proj"), "mix_do")
    do_raw, dg_b, d_norm_g = gnorm_bwd(do_ab[1], o_raw, proj, norm_g, "hgrn_norm_bwd")
    dq_b, df_b, di_b, dlb = hgrn_bwd(proj, lb_logits, states, do_raw, "hgrn_bwd")
    d_lb_logits = lb_bwd(dlb, lb_logits, "lb_bwd")
    dq_a, dk_a, dv_a, dc_col, dc_row = fox_bwd(proj, c_col, c_row, do_ab[0], "fox_bwd")
    dc = (dc_col.reshape(HEADS, t) + dc_row.reshape(HEADS, t)).T
    dc = jnp.pad(dc, ((0, 0), (0, LANES - HEADS)))
    dfa, d_fox_bias = fox_post_bwd(dc, fa, fox_bias, "fox_post_bwd")
    dproj = jnp.concatenate([dq_a, dk_a.astype(BF16), dv_a.astype(BF16), dq_b, df_b, di_b, dg_b,
                             dgate[0], dgate[1]], axis=1)
    marks.append(emit("in_main", mm_tn_2d(dproj, u2, F32, "mix_dw_in")))
    marks.append(emit("in_fa", mm_tn_2d(dfa, u2, F32, "mix_dw_in_fa")))
    du2a = mm_nn_2d(dproj, get_w("in_main"), F32, "mix_du")
    du2b = mm_nn_2d(dfa, get_w("in_fa"), F32, "mix_du_fa")
    dh1, d_mix_pre = pre_bwd(dh2, h1, _behind(gains["mix_pre"], marks), [du2a, du2b], "mix_pre_bwd")

    dx, d_f1_pre, d_f1_post = _ffn_bwd(dh1, s1, gains["ffn1_pre"], gains["ffn1_post"], get_w, emit, "1", "ffn1")

    small = dict(ffn1_pre=d_f1_pre, ffn1_post=d_f1_post, mix_pre=d_mix_pre, mix_post=d_mix_post,
                 ffn2_pre=d_f2_pre, ffn2_post=d_f2_post, ple_pre=d_ple_pre, ple_post=d_ple_post,
                 fox_bias=d_fox_bias, lb_logits=d_lb_logits, norm_g=d_norm_g)
    return loss, dx, small


def _mm_branches(o_ab, w_proj, name):
    g, t, kk = o_ab.shape
    _, jn, _, ns = w_proj.shape
    tm = _pick(t, (512, 256, 128))
    return _mm(o_ab, w_proj, mode="nn", grid=(t // tm, g * jn, 1),
               a_spec=pl.BlockSpec((None, tm, kk), lambda i, j, k: (j // jn, i, 0)),
               b_spec=pl.BlockSpec((None, None, kk, ns), lambda i, j, k: (j // jn, j % jn, 0, 0)),
               o_spec=pl.BlockSpec((None, tm, ns), lambda i, j, k: (j // jn, i, j % jn)),
               out_shape=jax.ShapeDtypeStruct((g, t, jn * ns), F32), acc_shape=(tm, ns), name=name)


def _mm_branches_bwd(dy_ab, w_proj, name):
    g, t, _ = dy_ab.shape
    _, jn, kk, ns = w_proj.shape
    tm = _pick(t, (512, 256, 128))
    return _mm(dy_ab, w_proj, mode="nt", grid=(t // tm, g, jn),
               a_spec=pl.BlockSpec((None, tm, ns), lambda i, j, k: (j, i, k)),
               b_spec=pl.BlockSpec((None, None, kk, ns), lambda i, j, k: (j, k, 0, 0)),
               o_spec=pl.BlockSpec((None, tm, kk), lambda i, j, k: (j, i, 0)),
               out_shape=jax.ShapeDtypeStruct((g, t, kk), F32), acc_shape=(tm, kk), name=name)


def _mm_branches_dw(o_ab, dy_ab, name):
    g, t, kk = o_ab.shape
    d = dy_ab.shape[2]
    jn = N_CHIPS
    ns = d // jn
    return _mm(o_ab, dy_ab, mode="tn", grid=(1, g * jn, 1),
               a_spec=pl.BlockSpec((None, t, kk), lambda i, j, k: (j // jn, 0, 0)),
               b_spec=pl.BlockSpec((None, t, ns), lambda i, j, k: (j // jn, 0, j % jn)),
               o_spec=pl.BlockSpec((None, None, kk, ns), lambda i, j, k: (j // jn, j % jn, 0, 0)),
               out_shape=jax.ShapeDtypeStruct((g, jn, kk, ns), F32), acc_shape=(kk, ns), name=name)


HBM_SPEC = pl.BlockSpec(memory_space=pltpu.HBM)
SEM_SPEC = pl.BlockSpec(memory_space=pltpu.SEMAPHORE)
ANY_SPEC = pl.BlockSpec(memory_space=pl.ANY)
EFFECT = pltpu.SideEffectType.DATAFLOW_SIDE_EFFECTING


def _in_hbm(a):
    return pltpu.with_memory_space_constraint(a, pltpu.HBM)


def _place():
    x, y, c = lax.axis_index("x"), lax.axis_index("y"), lax.axis_index("c")
    chips = [(1 - x, y), (x, 1 - y), (1 - x, 1 - y)]
    return x, y, c, chips


def _half(shape, which, axis):
    n = shape[-2 + axis] // 2
    cut = pl.ds(which * n, n)
    return (cut, slice(None)) if axis == 0 else (slice(None), cut)


def _half_shape(shape, axis):
    s = list(shape)
    s[len(s) - 2 + axis] //= 2
    return tuple(s)


def _remote(src, dst, send_sems, recv_sems, k, to):
    return pltpu.make_async_remote_copy(src_ref=src, dst_ref=dst, send_sem=send_sems.at[k], recv_sem=recv_sems.at[k],
                                        device_id=to, device_id_type=MESH)


def split_start(name, srcs, lands, counts, copies):
    ns, nl, nset = len(srcs), len(lands), len(counts)

    def body(*refs):
        src_refs, land_refs = refs[:ns], refs[ns:ns + nl]
        sems = refs[ns + nl:ns + nl + 2 * nset]
        for s, plan in enumerate(copies(src_refs, land_refs)):
            for k, (src, dst, to) in enumerate(plan):
                _remote(src, dst, sems[2 * s], sems[2 * s + 1], k, to).start()
        refs[-1][...] = jnp.zeros_like(refs[-1])

    out_shape = []
    for n in counts:
        out_shape += [pltpu.SemaphoreType.DMA((n,)), pltpu.SemaphoreType.DMA((n,))]
    out_shape += [pltpu.HBM(a.shape, a.dtype) for a in list(srcs) + list(lands)]
    out_shape.append(jax.ShapeDtypeStruct((8, LANES), F32))
    res = pl.pallas_call(
        body, name=name, out_shape=tuple(out_shape), in_specs=[HBM_SPEC] * (ns + nl),
        out_specs=tuple([SEM_SPEC] * (2 * nset) + [HBM_SPEC] * (ns + nl) + [pl.BlockSpec(memory_space=pltpu.VMEM)]),
        input_output_aliases={i: 2 * nset + i for i in range(ns + nl)},
        compiler_params=pltpu.CompilerParams(has_side_effects=EFFECT),
    )(*[_in_hbm(a) for a in list(srcs) + list(lands)])
    sems = [(res[2 * s], res[2 * s + 1]) for s in range(nset)]
    return sems, list(res[2 * nset:2 * nset + ns]), list(res[2 * nset + ns:-1]), res[-1][0:1, 0:1]


def split_wait(name, srcs, lands, sems, afters, copies):
    ns, nl, na = len(srcs), len(lands), len(afters)

    def body(*refs):
        src_refs, land_refs = refs[:ns], refs[ns:ns + nl]
        send_sems, recv_sems = refs[ns + nl:ns + nl + 2]
        for k, (src, dst, to) in enumerate(copies(src_refs, land_refs)):
            cp = _remote(src, dst, send_sems, recv_sems, k, to)
            cp.wait_send()
            cp.wait_recv()

    res = pl.pallas_call(
        body, name=name, out_shape=tuple(pltpu.HBM(a.shape, a.dtype) for a in list(srcs) + list(lands)),
        in_specs=[HBM_SPEC] * (ns + nl) + [SEM_SPEC, SEM_SPEC] + [ANY_SPEC] * na,
        out_specs=tuple([HBM_SPEC] * (ns + nl)), input_output_aliases={i: i for i in range(ns + nl)},
        compiler_params=pltpu.CompilerParams(has_side_effects=EFFECT),
    )(*srcs, *lands, sems[0], sems[1], *afters)
    return list(res[ns:])


def _gather_plan(blocks):
    def copies(src_refs, land_refs):
        x, y, c, chips = _place()
        j_me = 2 * x + y
        plan = []
        for si, li, g, axis in blocks:
            src, land = src_refs[si], land_refs[li].at[g]
            mine = _half(src.shape, c, axis)
            for px, py in chips:
                plan.append((src.at[mine], land.at[(j_me,) + mine], (px, py, c)))
            plan.append((src, land.at[j_me], (x, y, 1 - c)))
        return plan
    return copies


def _gather_arrivals(blocks):
    def copies(src_refs, land_refs):
        x, y, c, chips = _place()
        j_me = 2 * x + y
        plan = []
        for si, li, g, axis in blocks:
            src, land = src_refs[si], land_refs[li].at[g]
            mine = _half(src.shape, c, axis)
            for px, py in chips:
                plan.append((src.at[mine], land.at[(2 * px + py,) + mine], (px, py, c)))
            plan.append((src, land.at[j_me], (x, y, 1 - c)))
        return plan
    return copies


def gather_pass(name, lands, blocks):
    n = len(lands)

    def body(*refs):
        outs = refs[n:2 * n]
        send_sems, recv_sems = refs[2 * n:]
        x, y, c, chips = _place()
        sent = []
        for i, (li, g, axis) in enumerate(blocks):
            land = outs[li].at[g]
            mine = _half(land.shape[1:], c, axis)
            for k, (px, py) in enumerate(chips):
                part = land.at[(2 * px + py,) + mine]
                cp = _remote(part, part, send_sems, recv_sems, 3 * i + k, (x, y, 1 - c))
                cp.start()
                sent.append(cp)
        for i, (li, g, axis) in enumerate(blocks):
            land = outs[li].at[g]
            other = _half(land.shape[1:], 1 - c, axis)
            for k, (px, py) in enumerate(chips):
                part = land.at[(2 * px + py,) + other]
                _remote(part, part, send_sems, recv_sems, 3 * i + k, (x, y, 1 - c)).wait_recv()
        for cp in sent:
            cp.wait_send()

    m = 3 * len(blocks)
    return pl.pallas_call(
        body, name=name, in_specs=[ANY_SPEC] * n, out_specs=[ANY_SPEC] * n,
        out_shape=[jax.ShapeDtypeStruct(a.shape, a.dtype) for a in lands],
        input_output_aliases={i: i for i in range(n)},
        scratch_shapes=[pltpu.SemaphoreType.DMA((m,)), pltpu.SemaphoreType.DMA((m,))],
    )(*lands)


def pair_exchange(name, grads, axes):
    n = len(grads)

    def body(*refs):
        ins, outs = refs[:n], refs[n:2 * n]
        send_sems, recv_sems = refs[2 * n:]
        x, y, c, _ = _place()
        copies = []
        for i in range(n):
            theirs = (slice(None), slice(None)) + _half(ins[i].shape, 1 - c, axes[i])
            cp = _remote(ins[i].at[theirs], outs[i], send_sems, recv_sems, i, (x, y, 1 - c))
            cp.start()
            copies.append(cp)
        for cp in copies:
            cp.wait()

    out_shape = [jax.ShapeDtypeStruct(_half_shape(g.shape, a), g.dtype) for g, a in zip(grads, axes)]
    return pl.pallas_call(
        body, name=name, in_specs=[ANY_SPEC] * n, out_specs=[ANY_SPEC] * n, out_shape=out_shape,
        scratch_shapes=[pltpu.SemaphoreType.DMA((n,)), pltpu.SemaphoreType.DMA((n,))],
    )(*grads)


def _scatter_plan(n):
    def copies(src_refs, land_refs):
        x, y, c, chips = _place()
        return [(src_refs[i].at[:, 2 * px + py], land_refs[i].at[k], (px, py, c))
                for i in range(n) for k, (px, py) in enumerate(chips)]
    return copies


def pair_broadcast(name, bufs, axes):
    n = len(bufs)

    def body(*refs):
        outs = refs[n:2 * n]
        send_sems, recv_sems = refs[2 * n:]
        x, y, c, _ = _place()
        copies = []
        for i in range(n):
            mine = (slice(None),) + _half(outs[i].shape, c, axes[i])
            cp = _remote(outs[i].at[mine], outs[i].at[mine], send_sems, recv_sems, i, (x, y, 1 - c))
            cp.start()
            copies.append(cp)
        for i, cp in enumerate(copies):
            cp.wait_send()
            theirs = (slice(None),) + _half(outs[i].shape, 1 - c, axes[i])
            _remote(outs[i].at[theirs], outs[i].at[theirs], send_sems, recv_sems, i, (x, y, 1 - c)).wait_recv()

    return pl.pallas_call(
        body, name=name, in_specs=[ANY_SPEC] * n, out_specs=[ANY_SPEC] * n,
        out_shape=[jax.ShapeDtypeStruct(b.shape, b.dtype) for b in bufs],
        input_output_aliases={i: i for i in range(n)},
        scratch_shapes=[pltpu.SemaphoreType.DMA((n,)), pltpu.SemaphoreType.DMA((n,))],
    )(*bufs)


N_DEV = 8
SLAB_ROWS = 16


def allreduce_small(slab):
    def body(x_ref, o_ref, land, send_sems, recv_sems):
        x, y, c, _ = _place()
        me = 4 * x + 2 * y + c
        land[me] = x_ref[...]
        copies = []
        for d in range(1, N_DEV):
            to = (me + d) % N_DEV
            cp = pltpu.make_async_remote_copy(
                src_ref=x_ref, dst_ref=land.at[me], send_sem=send_sems.at[d - 1], recv_sem=recv_sems.at[me],
                device_id=(to // 4, (to // 2) % 2, to % 2), device_id_type=MESH)
            cp.start()
            copies.append(cp)
        for d in range(1, N_DEV):
            frm = (me + d) % N_DEV
            pltpu.make_async_remote_copy(
                src_ref=x_ref, dst_ref=land.at[frm], send_sem=send_sems.at[d - 1], recv_sem=recv_sems.at[frm],
                device_id=(frm // 4, (frm // 2) % 2, frm % 2), device_id_type=MESH).wait_recv()
        for cp in copies:
            cp.wait_send()
        acc = land[0]
        for s in range(1, N_DEV):
            acc = acc + land[s]
        o_ref[...] = acc

    vm = pl.BlockSpec(memory_space=pltpu.VMEM)
    return pl.pallas_call(
        body, name="allreduce_small", in_specs=[vm], out_specs=vm, out_shape=jax.ShapeDtypeStruct(slab.shape, F32),
        scratch_shapes=[pltpu.VMEM((N_DEV,) + slab.shape, F32), pltpu.SemaphoreType.DMA((N_DEV - 1,)),
                        pltpu.SemaphoreType.DMA((N_DEV,))],
    )(slab)


BLOCK_BYTES = 3 * 1024 * 1024


def _tiles_2d(r, c, budget=BLOCK_BYTES):
    if r % 8 == 0:
        tc = c if c % LANES else _pick(c, (2048, 1408, 1024, 512, 256, 128))
        tr = 8
        for cand in (512, 256, 128, 64, 32, 16, 8):
            if r % cand == 0 and cand * tc * 4 <= budget:
                tr = cand
                break
        return tr, tc
    tc = LANES
    for cand in (1024, 512, 256, 128):
        if c % cand == 0 and r * cand * 4 <= budget:
            tc = cand
            break
    return r, tc


def _grid_spec(grid, in_specs, out_specs):
    return pltpu.PrefetchScalarGridSpec(num_scalar_prefetch=1, grid=grid, in_specs=in_specs, out_specs=out_specs)


def _own(axis, nr, nc):
    if axis == 0:
        return lambda i, j, where: (where[1] * nr + i, j)
    return lambda i, j, where: (i, where[1] * nc + j)


def pair_add(where, grad, recv, axis, name):
    g, jn, hr, hc = recv.shape
    tr, tc = _tiles_2d(hr, hc)
    nr, nc = hr // tr, hc // tc
    own = _own(axis, nr, nc)

    def body(where_ref, a_ref, b_ref, o32_ref, o16_ref):
        s = a_ref[...] + b_ref[...]
        o32_ref[...] = s
        o16_ref[...] = s.astype(BF16)

    blk = pl.BlockSpec((None, None, tr, tc), lambda a, i, j, where: (a // jn, a % jn, i, j))
    mine = pl.BlockSpec((None, None, tr, tc), lambda a, i, j, where: (a // jn, a % jn) + own(i, j, where))
    return pl.pallas_call(
        body, name=name, grid_spec=_grid_spec((g * jn, nr, nc), [mine, blk], [blk, blk]),
        out_shape=[jax.ShapeDtypeStruct(recv.shape, F32), jax.ShapeDtypeStruct(recv.shape, BF16)],
        compiler_params=_params(("parallel", "parallel", "parallel")),
    )(where, grad, recv)


def chip_add(where, part, recv, axis, name):
    g, jn, hr, hc = part.shape
    tr, tc = _tiles_2d(hr, hc)
    nr, nc = hr // tr, hc // tc
    own = _own(axis, nr, nc)
    full = (g, 2 * hr, hc) if axis == 0 else (g, hr, 2 * hc)

    def body(where_ref, a_ref, b_ref, o_ref):
        s = a_ref[...]
        for k in range(3):
            s = s + b_ref[k].astype(F32)
        o_ref[...] = s

    return pl.pallas_call(
        body, name=name,
        grid_spec=_grid_spec((g, nr, nc),
                             [pl.BlockSpec((None, None, tr, tc), lambda a, i, j, where: (a, where[0], i, j)),
                              pl.BlockSpec((3, None, tr, tc), lambda a, i, j, where: (0, a, i, j))],
                             pl.BlockSpec((None, tr, tc), lambda a, i, j, where: (a,) + own(i, j, where))),
        out_shape=jax.ShapeDtypeStruct(full, F32), compiler_params=_params(("parallel", "parallel", "parallel")),
    )(where, part, recv)


def _adam_math(w, g, m, v):
    m2 = ADAM_B1 * m + (1.0 - ADAM_B1) * g
    v2 = ADAM_B2 * v + (1.0 - ADAM_B2) * (g * g)
    m_hat = m2 / (1.0 - ADAM_B1 ** ADAM_STEP)
    v_hat = v2 / (1.0 - ADAM_B2 ** ADAM_STEP)
    delta = -ADAM_LR * (m_hat / (jnp.sqrt(v_hat) + ADAM_EPS) + ADAM_WD * w)
    return delta, m2, v2


def adamw(grad, idx, w, m, v, name):
    _, r, cc = w.shape
    tr, tc = _tiles_2d(r, cc, BLOCK_BYTES // 2)

    def body(g_ref, w_ref, m_ref, v_ref, go_ref, d_ref, mo_ref, vo_ref):
        g = g_ref[...]
        delta, m2, v2 = _adam_math(w_ref[...], g, m_ref[...], v_ref[...])
        go_ref[...] = g
        d_ref[...] = delta
        mo_ref[...] = m2
        vo_ref[...] = v2

    blk = pl.BlockSpec((None, tr, tc), lambda i, j: (0, i, j))
    return pl.pallas_call(
        body, name=name, grid=(r // tr, cc // tc),
        in_specs=[pl.BlockSpec((None, tr, tc), lambda i, j: (idx, i, j)), blk, blk, blk], out_specs=[blk] * 4,
        out_shape=[jax.ShapeDtypeStruct(w.shape, F32)] * 4, compiler_params=_params(("parallel", "parallel")),
    )(grad, w, m, v)


def adamw_small(g, w, m, v):
    def body(g_ref, w_ref, m_ref, v_ref, d_ref, mo_ref, vo_ref):
        delta, m2, v2 = _adam_math(w_ref[...], g_ref[...], m_ref[...], v_ref[...])
        d_ref[...] = delta
        mo_ref[...] = m2
        vo_ref[...] = v2

    return pl.pallas_call(body, name="adamw_small", out_shape=[jax.ShapeDtypeStruct(w.shape, F32)] * 3)(g, w, m, v)


GAINS = ("ffn1_pre", "ffn1_post", "mix_pre", "mix_post", "ffn2_pre", "ffn2_post", "ple_pre", "ple_post")
WEIGHTS = ("ffn1_pre_g", "ffn1_post_g", "ffn1_w_gate", "ffn1_w_up", "ffn1_w_down", "mix_pre_g", "mix_post_g",
           "mix_w_in", "fox_f_bias", "hgrn_lb_logits", "hgrn_norm_g", "mix_w_proj_fox", "mix_w_proj_hgrn",
           "mix_w_out", "ffn2_pre_g", "ffn2_post_g", "ffn2_w_gate", "ffn2_w_up", "ffn2_w_down", "ple_pre_g",
           "ple_post_g", "ple_w_gate", "ple_w_proj")
GROUPS = dict(gu1=(("ffn1_w_gate", "ffn1_w_up"), 0), down1=(("ffn1_w_down",), 0), win=(("mix_w_in",), 1),
              proj=(("mix_w_proj_fox", "mix_w_proj_hgrn"), 0), out=(("mix_w_out",), 0),
              gu2=(("ffn2_w_gate", "ffn2_w_up"), 0), down2=(("ffn2_w_down",), 0), ple_gate=(("ple_w_gate",), 0),
              ple_proj=(("ple_w_proj",), 0))
TRANSPOSED = ("mix_w_in",)
ROW_BLOCKS = ("down1", "down2", "out", "ple_gate")
GATHER_SETS = (("gu1",), ("down1",), ("win",), ("proj", "out"), ("gu2", "down2", "ple_gate", "ple_proj"))
REDUCE_SETS = (("ple_gate", "ple_proj", "down2", "gu2"), ("out", "proj", "win"), ("down1", "gu1"))


def _pad_row(a, width):
    a = a.reshape(1, -1)
    return jnp.pad(a, ((0, 0), (0, width - a.shape[1])))


def _pack_small(vals):
    d = D_MODEL
    rows = [vals[n + "_g"].reshape(1, d) for n in GAINS]
    rows.append(_pad_row(vals["fox_f_bias"], d))
    lg = vals["hgrn_lb_logits"]
    rows += [_pad_row(lg[0], d), _pad_row(lg[1], d), _pad_row(vals["hgrn_norm_g"], d)]
    slab = jnp.concatenate(rows, axis=0)
    return jnp.pad(slab, ((0, SLAB_ROWS - slab.shape[0]), (0, 0)))


def _unpack_small(slab):
    out = {n + "_g": slab[i:i + 1] for i, n in enumerate(GAINS)}
    out["fox_f_bias"] = slab[8:9, :HEADS]
    out["hgrn_lb_logits"] = slab[9:11, :WIDTH]
    out["hgrn_norm_g"] = slab[11:12, :HEAD_DIM]
    return out


def _split_in(win_t):
    lo = 3 * WIDTH
    main = jnp.concatenate([win_t[:lo], win_t[lo + HEADS:]], axis=0)
    fa = jnp.pad(win_t[lo:lo + HEADS], ((0, LANES - HEADS), (0, 0)))
    return main, fa


def _join_in(main, fa):
    lo = 3 * WIDTH
    return jnp.concatenate([main[:lo], fa[:HEADS], main[lo:]], axis=0)


def _as_block(name, a):
    return jnp.swapaxes(a, 1, 2) if name in TRANSPOSED else a


def kernel(x, p, ffn1_pre_g, ffn1_post_g, ffn1_w_gate, ffn1_w_up, ffn1_w_down, mix_pre_g, mix_post_g, mix_w_in, fox_f_bias, hgrn_lb_logits, hgrn_norm_g, mix_w_proj_fox, mix_w_proj_hgrn, mix_w_out, ffn2_pre_g, ffn2_post_g, ffn2_w_gate, ffn2_w_up, ffn2_w_down, ple_pre_g, ple_post_g, ple_w_gate, ple_w_proj, loss_target, m_ffn1_pre_g, m_ffn1_post_g, m_ffn1_w_gate, m_ffn1_w_up, m_ffn1_w_down, m_mix_pre_g, m_mix_post_g, m_mix_w_in, m_fox_f_bias, m_hgrn_lb_logits, m_hgrn_norm_g, m_mix_w_proj_fox, m_mix_w_proj_hgrn, m_mix_w_out, m_ffn2_pre_g, m_ffn2_post_g, m_ffn2_w_gate, m_ffn2_w_up, m_ffn2_w_down, m_ple_pre_g, m_ple_post_g, m_ple_w_gate, m_ple_w_proj, v_ffn1_pre_g, v_ffn1_post_g, v_ffn1_w_gate, v_ffn1_w_up, v_ffn1_w_down, v_mix_pre_g, v_mix_post_g, v_mix_w_in, v_fox_f_bias, v_hgrn_lb_logits, v_hgrn_norm_g, v_mix_w_proj_fox, v_mix_w_proj_hgrn, v_mix_w_out, v_ffn2_pre_g, v_ffn2_post_g, v_ffn2_w_gate, v_ffn2_w_up, v_ffn2_w_down, v_ple_pre_g, v_ple_post_g, v_ple_w_gate, v_ple_w_proj):
    args = dict(locals())
    wts = {n: args[n] for n in WEIGHTS}
    mom = {n: args["m_" + n] for n in WEIGHTS}
    var = {n: args["v_" + n] for n in WEIGHTS}
    d = D_MODEL
    where = jnp.stack([2 * lax.axis_index("x") + lax.axis_index("y"), lax.axis_index("c")]).astype(jnp.int32)

    order = [g for s in GATHER_SETS for g in s]
    srcs, lands, plans = [], [], []
    for s in GATHER_SETS:
        blocks = []
        for g in s:
            names, axis = GROUPS[g]
            for pos, n in enumerate(names):
                blocks.append((len(srcs), order.index(g), pos, axis))
                srcs.append(_as_block(n, wts[n])[0].astype(BF16))
        plans.append(blocks)
    for g in order:
        names, _ = GROUPS[g]
        r, c = _as_block(names[0], wts[names[0]]).shape[1:]
        lands.append(lax.empty((len(names), N_CHIPS, r, c), BF16))
    sems, srcs, lands, _ = split_start(
        "gather_start", srcs, lands, [4 * len(b) for b in plans],
        lambda sr, lr: [_gather_plan(b)(sr, lr) for b in plans])
    full = {}

    def land_set(si, after):
        blocks = plans[si]
        s_idx = sorted({b[0] for b in blocks})
        l_idx = sorted({b[1] for b in blocks})
        local = [(s_idx.index(a), l_idx.index(b), g, ax) for a, b, g, ax in blocks]
        got = split_wait("gather_wait_%d" % si, [srcs[i] for i in s_idx], [lands[i] for i in l_idx], sems[si],
                         [] if after is None else [after], _gather_arrivals(local))
        got = gather_pass("gather_pass_%d" % si, got, [(b, g, ax) for _, b, g, ax in local])
        for i, arr in zip(l_idx, got):
            g = order[i]
            if g == "win":
                full["win"] = arr
                full["in_main"], full["in_fa"] = _split_in(arr.reshape(-1, d))
            else:
                full[g] = arr.reshape(-1, d) if g in ROW_BLOCKS else arr

    def get_w(key, after=None):
        g = "win" if key in ("in_main", "in_fa") else key
        if g not in full:
            land_set([g in s for s in GATHER_SETS].index(True), after)
        return full[key]

    grads, started = {}, {}
    rows4 = lambda a: a.reshape(1, N_CHIPS, a.shape[0] // N_CHIPS, a.shape[1])

    def emit(key, grad):
        if key in ("in_main", "in_fa"):
            grads[key] = grad
            if "in_main" not in grads or "in_fa" not in grads:
                return None
            key, grad = "win", rows4(_join_in(grads["in_main"], grads["in_fa"]))
        grads[key] = grad if grad.ndim == 4 else rows4(grad)
        for si, s in enumerate(REDUCE_SETS):
            if key in s and all(g in grads for g in s):
                axes = [GROUPS[g][1] for g in s]
                recv = pair_exchange("pair_exchange_%d" % si, [grads[g] for g in s], axes)
                sums = [pair_add(where, grads[g], r, a, "pair_add_" + g) for g, r, a in zip(s, recv, axes)]
                parts = [s16 for _, s16 in sums]
                zones = [lax.empty((3, q.shape[0]) + q.shape[2:], BF16) for q in parts]
                plan = _scatter_plan(len(s))
                sem, parts, zones, mark = split_start("scatter_start_%d" % si, parts, zones, [3 * len(s)],
                                                      lambda sr, lr: [plan(sr, lr)])
                started[si] = (sem[0], parts, zones, [s32 for s32, _ in sums], axes, plan)
                return mark
        return None

    gains = {n: wts[n + "_g"] for n in GAINS}
    loss, dx, small = layer_step(x[0], p[0, 0].astype(BF16), loss_target[0], gains, _pad_row(fox_f_bias, LANES),
                                 hgrn_lb_logits, hgrn_norm_g, get_w, emit)

    small_named = {n + "_g": small[n] for n in GAINS}
    small_named.update(fox_f_bias=small["fox_bias"][:, :HEADS], hgrn_lb_logits=small["lb_logits"],
                       hgrn_norm_g=small["norm_g"])
    g_small = allreduce_small(_pack_small(small_named))
    d_small, m_small, v_small = adamw_small(g_small, _pack_small(wts), _pack_small(mom), _pack_small(var))

    out_g, out_d, out_m, out_v = {}, {}, {}, {}
    after = d_small
    for si, s in enumerate(REDUCE_SETS):
        sem, parts, zones, sums32, axes, plan = started[si]
        zones = split_wait("scatter_wait_%d" % si, parts, zones, sem, [dx, after], plan)
        halves = [chip_add(where, s32, z, a, "chip_add_" + g) for g, s32, z, a in zip(s, sums32, zones, axes)]
        reduced = pair_broadcast("pair_broadcast_%d" % si, halves, axes)
        for g, red in zip(s, reduced):
            for idx, n in enumerate(GROUPS[g][0]):
                res = adamw(red, idx, _as_block(n, wts[n]), _as_block(n, mom[n]), _as_block(n, var[n]), "adamw_" + n)
                out_g[n], out_d[n], out_m[n], out_v[n] = [_as_block(n, r) for r in res]
                after = res[1]

    for dst, slab in ((out_g, g_small), (out_d, d_small), (out_m, m_small), (out_v, v_small)):
        dst.update(_unpack_small(slab))

    total = lax.psum(loss[0, 0], ("x", "y", "c"))
    return (total, dx[None], *[out_g[n] for n in WEIGHTS], *[out_d[n] for n in WEIGHTS],
            *[out_m[n] for n in WEIGHTS], *[out_v[n] for n in WEIGHTS])
```

```python
import functools

import jax
import jax.numpy as jnp
from jax import lax
from jax.experimental import pallas as pl
from jax.experimental.pallas import tpu as pltpu

F32 = jnp.float32
BF16 = jnp.bfloat16

D_MODEL = 2048
SEQ = 2048
D_FF = 5632
PLE_DIM = 256
HEADS = 8
HEAD_DIM = 128
WIDTH = HEADS * HEAD_DIM
CHUNK = 64
SUB = 16
NORM_EPS = 1e-6
MACARON_SCALE = 0.5
N_CHIPS = 4

ADAM_LR = 0.001
ADAM_B1 = 0.9
ADAM_B2 = 0.999
ADAM_EPS = 1e-08
ADAM_WD = 0.01
ADAM_STEP = 10

LANES = 128
VMEM_LIMIT = 56 * 1024 * 1024
NEG_BIG = -1e30
MESH = pl.DeviceIdType.MESH


def _pick(n, cands):
    for c in cands:
        if c <= n and n % c == 0:
            return c
    return n


def _params(sem, vmem=VMEM_LIMIT):
    return pltpu.CompilerParams(dimension_semantics=sem, vmem_limit_bytes=vmem)


def _sigmoid(x):
    return 1.0 / (1.0 + jnp.exp(-x))


def _silu(x):
    return x * _sigmoid(x)


def _silu_grad(x):
    s = _sigmoid(x)
    return s * (1.0 + x * (1.0 - s))


_DN = {"nn": (((1,), (0,)), ((), ())), "nt": (((1,), (1,)), ((), ())), "tn": (((0,), (0,)), ((), ()))}


def _mm(a, b, *, mode, grid, a_spec, b_spec, o_spec, out_shape, acc_shape, name, after=()):
    nk = grid[2]
    dn = _DN[mode]
    after = [m for m in after if m is not None]

    def body(a_ref, b_ref, *rest):
        o_ref, acc_ref = rest[len(after):]
        k = pl.program_id(2)

        @pl.when(k == 0)
        def _():
            acc_ref[...] = jnp.zeros_like(acc_ref)

        acc_ref[...] += lax.dot_general(a_ref[...].astype(BF16), b_ref[...].astype(BF16), dn,
                                        preferred_element_type=F32)

        @pl.when(k == nk - 1)
        def _():
            o_ref[...] = acc_ref[...].astype(o_ref.dtype)

    return pl.pallas_call(
        body, name=name, grid=grid, in_specs=[a_spec, b_spec] + [pl.BlockSpec(memory_space=pl.ANY)] * len(after),
        out_specs=o_spec, out_shape=out_shape, scratch_shapes=[pltpu.VMEM(acc_shape, F32)],
        compiler_params=_params(("parallel", "parallel", "arbitrary")),
    )(a, b, *after)


def mm_nn_2d(a, b, out_dtype, name, after=()):
    m, kk = a.shape
    n = b.shape[1]
    tm, tn, tk = _pick(m, (512, 256, 128)), _pick(n, (1024, 512, 256, 128)), _pick(kk, (2048, 1408, 1024, 512, 256, 128))
    return _mm(a, b, mode="nn", grid=(m // tm, n // tn, kk // tk),
               a_spec=pl.BlockSpec((tm, tk), lambda i, j, k: (i, k)),
               b_spec=pl.BlockSpec((tk, tn), lambda i, j, k: (k, j)),
               o_spec=pl.BlockSpec((tm, tn), lambda i, j, k: (i, j)),
               out_shape=jax.ShapeDtypeStruct((m, n), out_dtype), acc_shape=(tm, tn), name=name, after=after)


def mm_nt_2d(a, b, out_dtype, name, after=()):
    m, c = a.shape
    n = b.shape[0]
    tm, tn, tk = _pick(m, (512, 256, 128)), _pick(n, (1408, 1024, 512, 256, 128)), _pick(c, (2048, 1408, 1024, 512, 256, 128))
    return _mm(a, b, mode="nt", grid=(m // tm, n // tn, c // tk),
               a_spec=pl.BlockSpec((tm, tk), lambda i, j, k: (i, k)),
               b_spec=pl.BlockSpec((tn, tk), lambda i, j, k: (j, k)),
               o_spec=pl.BlockSpec((tm, tn), lambda i, j, k: (i, j)),
               out_shape=jax.ShapeDtypeStruct((m, n), out_dtype), acc_shape=(tm, tn), name=name, after=after)


def mm_tn_2d(a, b, out_dtype, name):
    c, m = a.shape
    n = b.shape[1]
    tm, tn, tk = _pick(m, (1408, 1024, 512, 256, 128)), _pick(n, (1024, 512, 256, 128)), _pick(c, (2048, 1024, 512, 256, 128))
    return _mm(a, b, mode="tn", grid=(m // tm, n // tn, c // tk),
               a_spec=pl.BlockSpec((tk, tm), lambda i, j, k: (k, i)),
               b_spec=pl.BlockSpec((tk, tn), lambda i, j, k: (k, j)),
               o_spec=pl.BlockSpec((tm, tn), lambda i, j, k: (i, j)),
               out_shape=jax.ShapeDtypeStruct((m, n), out_dtype), acc_shape=(tm, tn), name=name)


def mm_nn_col(a, w, out_dtype, name):
    m, kk = a.shape
    g, jn, _, ns = w.shape
    tm, tk = _pick(m, (512, 256, 128)), _pick(kk, (2048, 1024, 512, 256, 128))
    return _mm(a, w, mode="nn", grid=(m // tm, g * jn, kk // tk),
               a_spec=pl.BlockSpec((tm, tk), lambda i, j, k: (i, k)),
               b_spec=pl.BlockSpec((None, None, tk, ns), lambda i, j, k: (j // jn, j % jn, k, 0)),
               o_spec=pl.BlockSpec((None, tm, ns), lambda i, j, k: (j // jn, i, j % jn)),
               out_shape=jax.ShapeDtypeStruct((g, m, jn * ns), out_dtype), acc_shape=(tm, ns), name=name)


def mm_nt_col(a, w, out_dtype, name, after=()):
    g, m, _ = a.shape
    _, jn, kk, ns = w.shape
    tm, tn = _pick(m, (512, 256, 128)), _pick(kk, (1024, 512, 256, 128))
    return _mm(a, w, mode="nt", grid=(m // tm, kk // tn, g * jn),
               a_spec=pl.BlockSpec((None, tm, ns), lambda i, j, k: (k // jn, i, k % jn)),
               b_spec=pl.BlockSpec((None, None, tn, ns), lambda i, j, k: (k // jn, k % jn, j, 0)),
               o_spec=pl.BlockSpec((tm, tn), lambda i, j, k: (i, j)),
               out_shape=jax.ShapeDtypeStruct((m, kk), out_dtype), acc_shape=(tm, tn), name=name, after=after)


def mm_tn_col(a, b, jn, out_dtype, name):
    c, kk = a.shape
    g, _, n = b.shape
    ns = n // jn
    tm, tk = _pick(kk, (512, 256, 128)), _pick(c, (2048, 1024, 512, 256, 128))
    return _mm(a, b, mode="tn", grid=(kk // tm, g * jn, c // tk),
               a_spec=pl.BlockSpec((tk, tm), lambda i, j, k: (k, i)),
               b_spec=pl.BlockSpec((None, tk, ns), lambda i, j, k: (j // jn, k, j % jn)),
               o_spec=pl.BlockSpec((None, None, tm, ns), lambda i, j, k: (j // jn, j % jn, i, 0)),
               out_shape=jax.ShapeDtypeStruct((g, jn, kk, ns), out_dtype), acc_shape=(tm, ns), name=name)


def _rstd(x):
    return lax.rsqrt(jnp.mean(x * x, axis=-1, keepdims=True) + NORM_EPS)


def _rms_bwd(x, g, dy):
    r = _rstd(x)
    xn = x * r
    dyg = dy * g
    dx = r * (dyg - xn * jnp.mean(dyg * xn, axis=-1, keepdims=True))
    return dx, jnp.sum(dy * xn, axis=0, keepdims=True)


def _row_tile(t):
    return _pick(t, (256, 128, 64, 32, 16, 8))


def norm_in(h, g, name):
    t, d = h.shape
    tr = _row_tile(t)

    def body(h_ref, g_ref, u_ref):
        x = h_ref[...]
        u_ref[...] = (x * _rstd(x) * g_ref[...]).astype(BF16)

    return pl.pallas_call(
        body, name=name, grid=(t // tr,),
        in_specs=[pl.BlockSpec((tr, d), lambda i: (i, 0)), pl.BlockSpec((1, d), lambda i: (0, 0))],
        out_specs=pl.BlockSpec((tr, d), lambda i: (i, 0)),
        out_shape=jax.ShapeDtypeStruct((t, d), BF16), compiler_params=_params(("parallel",)),
    )(h, g)


def resid_post(h, y, g, scale, name):
    t, d = h.shape
    tr = _row_tile(t)

    def body(h_ref, y_ref, g_ref, o_ref):
        yv = y_ref[...]
        o_ref[...] = h_ref[...] + scale * (yv * _rstd(yv) * g_ref[...])

    row = pl.BlockSpec((tr, d), lambda i: (i, 0))
    return pl.pallas_call(
        body, name=name, grid=(t // tr,), in_specs=[row, row, pl.BlockSpec((1, d), lambda i: (0, 0))],
        out_specs=row, out_shape=jax.ShapeDtypeStruct((t, d), F32), compiler_params=_params(("parallel",)),
    )(h, y, g)


def post_bwd(dh, y, g, scale, name):
    t, d = dh.shape
    tr = _row_tile(t)

    def body(dh_ref, y_ref, g_ref, dy_ref, dg_ref):
        @pl.when(pl.program_id(0) == 0)
        def _():
            dg_ref[...] = jnp.zeros_like(dg_ref)

        dx, dg = _rms_bwd(y_ref[...], g_ref[...], scale * dh_ref[...])
        dy_ref[...] = dx.astype(BF16)
        dg_ref[...] += dg

    row = pl.BlockSpec((tr, d), lambda i: (i, 0))
    vec = pl.BlockSpec((1, d), lambda i: (0, 0))
    return pl.pallas_call(
        body, name=name, grid=(t // tr,), in_specs=[row, row, vec], out_specs=[row, vec],
        out_shape=[jax.ShapeDtypeStruct((t, d), BF16), jax.ShapeDtypeStruct((1, d), F32)],
        compiler_params=_params(("arbitrary",)),
    )(dh, y, g)


def pre_bwd(dh, h, g, dus, name, after=()):
    t, d = dh.shape
    tr = _row_tile(t)
    n_du = len(dus)
    after = [m for m in after if m is not None]

    def body(*refs):
        dh_ref, h_ref, g_ref = refs[:3]
        du_refs = refs[3:3 + n_du]
        o_ref, dg_ref = refs[3 + n_du + len(after):]

        @pl.when(pl.program_id(0) == 0)
        def _():
            dg_ref[...] = jnp.zeros_like(dg_ref)

        du = du_refs[0][...]
        for r in du_refs[1:]:
            du = du + r[...]
        dx, dg = _rms_bwd(h_ref[...], g_ref[...], du)
        o_ref[...] = dh_ref[...] + dx
        dg_ref[...] += dg

    row = pl.BlockSpec((tr, d), lambda i: (i, 0))
    vec = pl.BlockSpec((1, d), lambda i: (0, 0))
    return pl.pallas_call(
        body, name=name, grid=(t // tr,),
        in_specs=[row, row, vec] + [row] * n_du + [pl.BlockSpec(memory_space=pl.ANY)] * len(after),
        out_specs=[row, vec], out_shape=[jax.ShapeDtypeStruct((t, d), F32), jax.ShapeDtypeStruct((1, d), F32)],
        compiler_params=_params(("arbitrary",)),
    )(dh, h, g, *dus, *after)


def _ew_tiles(t, f):
    return _pick(t, (256, 128, 64, 32, 16, 8)), _pick(f, (1408, 1024, 512, 256, 128))


def swiglu_act(gu, name):
    _, t, f = gu.shape
    tr, tc = _ew_tiles(t, f)

    def body(gu_ref, o_ref):
        o_ref[...] = (_silu(gu_ref[0]) * gu_ref[1]).astype(BF16)

    return pl.pallas_call(
        body, name=name, grid=(t // tr, f // tc),
        in_specs=[pl.BlockSpec((2, tr, tc), lambda i, j: (0, i, j))],
        out_specs=pl.BlockSpec((tr, tc), lambda i, j: (i, j)),
        out_shape=jax.ShapeDtypeStruct((t, f), BF16), compiler_params=_params(("parallel", "parallel")),
    )(gu)


def swiglu_bwd(dact, gu, name):
    _, t, f = gu.shape
    tr, tc = _ew_tiles(t, f)

    def body(da_ref, gu_ref, o_ref):
        da = da_ref[...]
        gate = gu_ref[0]
        o_ref[0] = (da * gu_ref[1] * _silu_grad(gate)).astype(BF16)
        o_ref[1] = (da * _silu(gate)).astype(BF16)

    return pl.pallas_call(
        body, name=name, grid=(t // tr, f // tc),
        in_specs=[pl.BlockSpec((tr, tc), lambda i, j: (i, j)), pl.BlockSpec((2, tr, tc), lambda i, j: (0, i, j))],
        out_specs=pl.BlockSpec((2, tr, tc), lambda i, j: (0, i, j)),
        out_shape=jax.ShapeDtypeStruct((2, t, f), BF16), compiler_params=_params(("parallel", "parallel")),
    )(dact, gu)


def _col_blocks():
    w = WIDTH // LANES
    return dict(q_a=0, k_a=w, v_a=2 * w, q_b=3 * w, f_b=4 * w, i_b=5 * w, g_b=6 * w, gate_a=7 * w,
                gate_b=7 * w + D_MODEL // LANES)


def _tri(n, lower):
    r = lax.broadcasted_iota(jnp.int32, (n, n), 0)
    c = lax.broadcasted_iota(jnp.int32, (n, n), 1)
    return jnp.where((r >= c) if lower else (r <= c), 1.0, 0.0).astype(F32)


def _dot_hi(a, b):
    return jnp.dot(a, b, precision=lax.Precision.HIGHEST, preferred_element_type=F32)


def fox_prep(fa, bias, name):
    t, w = fa.shape
    tb = _pick(t, (256, 128, 64))

    def body(fa_ref, b_ref, c_ref, carry_ref):
        @pl.when(pl.program_id(0) == 0)
        def _():
            carry_ref[...] = jnp.zeros_like(carry_ref)

        z = fa_ref[...] + b_ref[...]
        lf = jnp.minimum(z, 0.0) - jnp.log(1.0 + jnp.exp(-jnp.abs(z)))
        c = _dot_hi(_tri(tb, True), lf) + carry_ref[...]
        c_ref[...] = c
        carry_ref[...] = carry_ref[...] + jnp.sum(lf, axis=0, keepdims=True)

    return pl.pallas_call(
        body, name=name, grid=(t // tb,),
        in_specs=[pl.BlockSpec((tb, w), lambda i: (i, 0)), pl.BlockSpec((1, w), lambda i: (0, 0))],
        out_specs=pl.BlockSpec((tb, w), lambda i: (i, 0)),
        out_shape=jax.ShapeDtypeStruct((t, w), F32), scratch_shapes=[pltpu.VMEM((1, w), F32)],
        compiler_params=_params(("arbitrary",)),
    )(fa, bias)


def fox_post_bwd(dc, fa, bias, name):
    t, w = fa.shape
    tb = _pick(t, (256, 128, 64))
    nb = t // tb

    def body(dc_ref, fa_ref, b_ref, dfa_ref, db_ref, carry_ref):
        @pl.when(pl.program_id(0) == 0)
        def _():
            carry_ref[...] = jnp.zeros_like(carry_ref)
            db_ref[...] = jnp.zeros_like(db_ref)

        dcv = dc_ref[...]
        dlf = _dot_hi(_tri(tb, False), dcv) + carry_ref[...]
        z = fa_ref[...] + b_ref[...]
        dz = dlf * _sigmoid(-z)
        dfa_ref[...] = dz.astype(BF16)
        db_ref[...] += jnp.sum(dz, axis=0, keepdims=True)
        carry_ref[...] = carry_ref[...] + jnp.sum(dcv, axis=0, keepdims=True)

    rev = pl.BlockSpec((tb, w), lambda i: (nb - 1 - i, 0))
    vec = pl.BlockSpec((1, w), lambda i: (0, 0))
    return pl.pallas_call(
        body, name=name, grid=(nb,), in_specs=[rev, rev, vec], out_specs=[rev, vec],
        out_shape=[jax.ShapeDtypeStruct((t, w), BF16), jax.ShapeDtypeStruct((1, w), F32)],
        scratch_shapes=[pltpu.VMEM((1, w), F32)], compiler_params=_params(("arbitrary",)),
    )(dc, fa, bias)


def _fox_probs(q_ref, k_ref, cc_ref, cr_ref, qi, tq, t):
    scale = HEAD_DIM ** -0.5
    s = lax.dot_general(q_ref[...].astype(BF16), k_ref[...].astype(BF16), _DN["nt"], preferred_element_type=F32)
    logits = s * scale + cc_ref[...] - cr_ref[...]
    qpos = qi * tq + lax.broadcasted_iota(jnp.int32, (tq, t), 0)
    kpos = lax.broadcasted_iota(jnp.int32, (tq, t), 1)
    logits = jnp.where(kpos <= qpos, logits, NEG_BIG)
    m = jnp.max(logits, axis=-1, keepdims=True)
    p = jnp.exp(logits - m)
    return p / jnp.sum(p, axis=-1, keepdims=True)


def fox_fwd(proj, c_col, c_row, name):
    t = proj.shape[0]
    tq = _pick(t, (256, 128))
    cb = _col_blocks()
    dh = HEAD_DIM

    def body(q_ref, k_ref, v_ref, cc_ref, cr_ref, o_ref):
        p = _fox_probs(q_ref, k_ref, cc_ref, cr_ref, pl.program_id(1), tq, t)
        o_ref[...] = jnp.dot(p.astype(BF16), v_ref[...].astype(BF16), preferred_element_type=F32).astype(BF16)

    return pl.pallas_call(
        body, name=name, grid=(HEADS, t // tq),
        in_specs=[pl.BlockSpec((tq, dh), lambda h, i: (i, cb["q_a"] + h)),
                  pl.BlockSpec((t, dh), lambda h, i: (0, cb["k_a"] + h)),
                  pl.BlockSpec((t, dh), lambda h, i: (0, cb["v_a"] + h)),
                  pl.BlockSpec((None, tq, 1), lambda h, i: (h, i, 0)),
                  pl.BlockSpec((None, 1, t), lambda h, i: (h, 0, 0))],
        out_specs=pl.BlockSpec((tq, dh), lambda h, i: (i, h)),
        out_shape=jax.ShapeDtypeStruct((t, WIDTH), BF16), compiler_params=_params(("parallel", "parallel")),
    )(proj, proj, proj, c_col, c_row)


def fox_bwd(proj, c_col, c_row, do, name):
    t = proj.shape[0]
    tq = _pick(t, (256, 128))
    cb = _col_blocks()
    dh = HEAD_DIM
    scale = HEAD_DIM ** -0.5

    def body(q_ref, k_ref, v_ref, cc_ref, cr_ref, do_ref, dq_ref, dk_ref, dv_ref, dcc_ref, dcr_ref):
        @pl.when(pl.program_id(1) == 0)
        def _():
            dk_ref[...] = jnp.zeros_like(dk_ref)
            dv_ref[...] = jnp.zeros_like(dv_ref)
            dcr_ref[...] = jnp.zeros_like(dcr_ref)

        p = _fox_probs(q_ref, k_ref, cc_ref, cr_ref, pl.program_id(1), tq, t)
        dov = do_ref[...].astype(BF16)
        kb = k_ref[...].astype(BF16)
        dv_ref[...] += lax.dot_general(p.astype(BF16), dov, _DN["tn"], preferred_element_type=F32)
        dp = lax.dot_general(dov, v_ref[...].astype(BF16), _DN["nt"], preferred_element_type=F32)
        ds = p * (dp - jnp.sum(p * dp, axis=-1, keepdims=True))
        dcc_ref[...] = jnp.sum(ds, axis=-1, keepdims=True)
        dcr_ref[...] -= jnp.sum(ds, axis=0, keepdims=True)
        dss = (ds * scale).astype(BF16)
        dq_ref[...] = jnp.dot(dss, kb, preferred_element_type=F32).astype(BF16)
        dk_ref[...] += lax.dot_general(dss, q_ref[...].astype(BF16), _DN["tn"], preferred_element_type=F32)

    return pl.pallas_call(
        body, name=name, grid=(HEADS, t // tq),
        in_specs=[pl.BlockSpec((tq, dh), lambda h, i: (i, cb["q_a"] + h)),
                  pl.BlockSpec((t, dh), lambda h, i: (0, cb["k_a"] + h)),
                  pl.BlockSpec((t, dh), lambda h, i: (0, cb["v_a"] + h)),
                  pl.BlockSpec((None, tq, 1), lambda h, i: (h, i, 0)),
                  pl.BlockSpec((None, 1, t), lambda h, i: (h, 0, 0)),
                  pl.BlockSpec((tq, dh), lambda h, i: (i, h))],
        out_specs=[pl.BlockSpec((tq, dh), lambda h, i: (i, h)),
                   pl.BlockSpec((t, dh), lambda h, i: (0, h)),
                   pl.BlockSpec((t, dh), lambda h, i: (0, h)),
                   pl.BlockSpec((None, tq, 1), lambda h, i: (h, i, 0)),
                   pl.BlockSpec((None, 1, t), lambda h, i: (h, 0, 0))],
        out_shape=[jax.ShapeDtypeStruct((t, WIDTH), BF16), jax.ShapeDtypeStruct((t, WIDTH), F32),
                   jax.ShapeDtypeStruct((t, WIDTH), F32), jax.ShapeDtypeStruct((HEADS, t, 1), F32),
                   jax.ShapeDtypeStruct((HEADS, 1, t), F32)],
        compiler_params=_params(("parallel", "arbitrary")),
    )(proj, proj, proj, c_col, c_row, do)


def _lower_bound(lg_ref):
    l0 = lg_ref[0:1, :]
    l1 = lg_ref[1:2, :]
    m = jnp.maximum(l0, l1)
    e0 = jnp.exp(l0 - m)
    e1 = jnp.exp(l1 - m)
    return e0 / (e0 + e1)


def _hgrn_inputs(qb_ref, fb_ref, lg_ref, q_s, k_s, cum_s):
    lb = _lower_bound(lg_ref)
    sig = _sigmoid(fb_ref[...])
    f = lb + (1.0 - lb) * sig
    q_s[...] = _silu(qb_ref[...])
    k_s[...] = 1.0 - f
    cum_s[...] = _dot_hi(_tri(CHUNK, True), jnp.log(f))
    return lb, sig, f


def _boundary(cum_s, a):
    if a == 0:
        return jnp.zeros((1, HEAD_DIM), F32)
    return cum_s[pl.ds(SUB * a - 1, 1), :]


def _hgrn_scores(q_s, k_s, cum_s):
    cum = cum_s[...]
    kk = k_s[...]
    lane = lax.broadcasted_iota(jnp.int32, (SUB, CHUNK), 1)
    row = lax.broadcasted_iota(jnp.int32, (SUB, 1), 0)
    blocks = []
    for a in range(CHUNK // SUB):
        rows = pl.ds(SUB * a, SUB)
        ca = _boundary(cum_s, a)
        cum_a = cum_s[rows, :]
        q_a = q_s[rows, :]
        qa = q_a * jnp.exp(cum_a - ca)
        ka = kk * jnp.exp(jnp.minimum(ca - cum, 0.0))
        blk = lax.dot_general(qa, ka, _DN["nt"], preferred_element_type=F32)
        blk = jnp.where(lane < SUB * a, blk, 0.0)
        for s in range(SUB):
            r = SUB * a + s
            e = jnp.exp(jnp.minimum(cum_a - cum_s[pl.ds(r, 1), :], 0.0))
            col = jnp.sum(q_a * k_s[pl.ds(r, 1), :] * e, axis=-1, keepdims=True)
            col = jnp.where(row >= s, col, 0.0)
            blk = jnp.where(lane == r, col, blk)
        blocks.append(blk)
    return jnp.concatenate(blocks, axis=0)


def hgrn_fwd(proj, lb_logits, name):
    t = proj.shape[0]
    n = t // CHUNK
    cb = _col_blocks()
    dh = HEAD_DIM

    def body(qb_ref, fb_ref, ib_ref, lg_ref, o_ref, st_ref, state, q_s, k_s, cum_s):
        @pl.when(pl.program_id(1) == 0)
        def _():
            state[...] = jnp.zeros_like(state)

        _hgrn_inputs(qb_ref, fb_ref, lg_ref, q_s, k_s, cum_s)
        st = state[...]
        st_ref[...] = st
        cum = cum_s[...]
        v = ib_ref[...]
        qe = q_s[...] * jnp.exp(cum)
        inter = lax.dot_general(qe, st, _DN["nt"], preferred_element_type=F32)
        a_mat = _hgrn_scores(q_s, k_s, cum_s)
        o_ref[...] = inter + jnp.dot(a_mat, v, preferred_element_type=F32)
        last = cum_s[pl.ds(CHUNK - 1, 1), :]
        kd = k_s[...] * jnp.exp(last - cum)
        state[...] = st * jnp.exp(last) + lax.dot_general(v, kd, _DN["tn"], preferred_element_type=F32)

    blk = lambda off: pl.BlockSpec((CHUNK, dh), lambda h, i: (i, off + h))
    return pl.pallas_call(
        body, name=name, grid=(HEADS, n),
        in_specs=[blk(cb["q_b"]), blk(cb["f_b"]), blk(cb["i_b"]), pl.BlockSpec((2, dh), lambda h, i: (0, h))],
        out_specs=[pl.BlockSpec((CHUNK, dh), lambda h, i: (i, h)),
                   pl.BlockSpec((None, None, dh, dh), lambda h, i: (h, i, 0, 0))],
        out_shape=[jax.ShapeDtypeStruct((t, WIDTH), F32), jax.ShapeDtypeStruct((HEADS, n, dh, dh), F32)],
        scratch_shapes=[pltpu.VMEM((dh, dh), F32)] + [pltpu.VMEM((CHUNK, dh), F32)] * 3,
        compiler_params=_params(("parallel", "arbitrary")),
    )(proj, proj, proj, lb_logits)


def hgrn_bwd(proj, lb_logits, states, do, name):
    t = proj.shape[0]
    n = t // CHUNK
    cb = _col_blocks()
    dh = HEAD_DIM
    nsub = CHUNK // SUB

    def body(qb_ref, fb_ref, ib_ref, lg_ref, st_ref, do_ref, dqb_ref, dfb_ref, dib_ref, dlb_ref,
             dstate, q_s, k_s, cum_s, da_s, dq_s, dk_s):
        @pl.when(pl.program_id(1) == 0)
        def _():
            dstate[...] = jnp.zeros_like(dstate)
            dlb_ref[...] = jnp.zeros_like(dlb_ref)

        lb, sig, f = _hgrn_inputs(qb_ref, fb_ref, lg_ref, q_s, k_s, cum_s)
        st = st_ref[...]
        dst = dstate[...]
        cum = cum_s[...]
        q = q_s[...]
        kk = k_s[...]
        v = ib_ref[...]
        dov = do_ref[...]
        e_cum = jnp.exp(cum)
        qe = q * e_cum
        last = cum_s[pl.ds(CHUNK - 1, 1), :]
        e_last = jnp.exp(last)
        e_tail = jnp.exp(last - cum)
        kd = kk * e_tail

        a_mat = _hgrn_scores(q_s, k_s, cum_s)
        tri = _tri(CHUNK, True)
        da_s[...] = lax.dot_general(dov, v, _DN["nt"], preferred_element_type=F32) * tri
        dv = (lax.dot_general(a_mat, dov, _DN["tn"], preferred_element_type=F32)
              + lax.dot_general(kd, dst, _DN["nt"], preferred_element_type=F32))
        dk_state = jnp.dot(v, dst, preferred_element_type=F32) * e_tail
        dq_inter = jnp.dot(dov, st, preferred_element_type=F32) * e_cum
        dstate[...] = dst * e_last + lax.dot_general(dov, qe, _DN["tn"], preferred_element_type=F32)

        lane = lax.broadcasted_iota(jnp.int32, (SUB, CHUNK), 1)
        row = lax.broadcasted_iota(jnp.int32, (SUB, 1), 0)
        dk_s[...] = jnp.zeros_like(dk_s)
        for a in range(nsub):
            rows = pl.ds(SUB * a, SUB)
            ca = _boundary(cum_s, a)
            cum_a = cum_s[rows, :]
            q_a = q_s[rows, :]
            ea = jnp.exp(cum_a - ca)
            eb = jnp.exp(jnp.minimum(ca - cum, 0.0))
            da_a = da_s[rows, :]
            da_off = jnp.where(lane < SUB * a, da_a, 0.0)
            dq_a = ea * jnp.dot(da_off, kk * eb, preferred_element_type=F32)
            dk_s[...] += eb * lax.dot_general(da_off, q_a * ea, _DN["tn"], preferred_element_type=F32)
            dk_rows = jnp.zeros((SUB, dh), F32)
            for s in range(SUB):
                r = SUB * a + s
                e = jnp.exp(jnp.minimum(cum_a - cum_s[pl.ds(r, 1), :], 0.0))
                dcol = jnp.sum(jnp.where(lane == r, da_a, 0.0), axis=-1, keepdims=True)
                dcol = jnp.where(row >= s, dcol, 0.0)
                w = dcol * e
                dq_a = dq_a + w * k_s[pl.ds(r, 1), :]
                dk_rows = jnp.where(row == s, jnp.sum(w * q_a, axis=0, keepdims=True), dk_rows)
            dq_s[rows, :] = dq_a
            dk_s[rows, :] += dk_rows

        dq = dq_inter + dq_s[...]
        dk = dk_s[...] + dk_state
        d_last = (jnp.sum(dst * st, axis=0, keepdims=True) * e_last
                  + jnp.sum(kk * dk_state, axis=0, keepdims=True))
        rowc = lax.broadcasted_iota(jnp.int32, (CHUNK, 1), 0)
        dcum = q * dq - kk * dk + jnp.where(rowc == CHUNK - 1, d_last, 0.0)
        dg = _dot_hi(_tri(CHUNK, False), dcum)
        df = dg / f - dk
        dqb_ref[...] = (dq * _silu_grad(qb_ref[...])).astype(BF16)
        dfb_ref[...] = (df * (1.0 - lb) * sig * (1.0 - sig)).astype(BF16)
        dib_ref[...] = dv.astype(BF16)
        dlb_ref[...] += jnp.sum(df * (1.0 - sig), axis=0, keepdims=True)

    blk = lambda off: pl.BlockSpec((CHUNK, dh), lambda h, i: (n - 1 - i, off + h))
    out_blk = pl.BlockSpec((CHUNK, dh), lambda h, i: (n - 1 - i, h))
    return pl.pallas_call(
        body, name=name, grid=(HEADS, n),
        in_specs=[blk(cb["q_b"]), blk(cb["f_b"]), blk(cb["i_b"]), pl.BlockSpec((2, dh), lambda h, i: (0, h)),
                  pl.BlockSpec((None, None, dh, dh), lambda h, i: (h, n - 1 - i, 0, 0)), out_blk],
        out_specs=[out_blk, out_blk, out_blk, pl.BlockSpec((1, dh), lambda h, i: (0, h))],
        out_shape=[jax.ShapeDtypeStruct((t, WIDTH), BF16)] * 3 + [jax.ShapeDtypeStruct((1, WIDTH), F32)],
        scratch_shapes=[pltpu.VMEM((dh, dh), F32)] + [pltpu.VMEM((CHUNK, dh), F32)] * 3
        + [pltpu.VMEM((CHUNK, CHUNK), F32)] + [pltpu.VMEM((CHUNK, dh), F32)] * 2,
        compiler_params=_params(("parallel", "arbitrary")),
    )(proj, proj, proj, lb_logits, states, do)


def lb_bwd(dlb, lb_logits, name):
    def body(dlb_ref, lg_ref, o_ref):
        p0 = _lower_bound(lg_ref)
        d0 = dlb_ref[...] * p0 * (1.0 - p0)
        o_ref[0:1, :] = d0
        o_ref[1:2, :] = -d0

    return pl.pallas_call(body, name=name, out_shape=jax.ShapeDtypeStruct(lb_logits.shape, F32))(dlb, lb_logits)


def gnorm_fwd(o_raw, proj, norm_g, name):
    t = o_raw.shape[0]
    tr = _row_tile(t)
    cb = _col_blocks()
    dh = HEAD_DIM

    def body(o_ref, gb_ref, g_ref, y_ref):
        x = o_ref[...]
        y_ref[...] = (x * _rstd(x) * g_ref[...] * _silu(gb_ref[...])).astype(BF16)

    return pl.pallas_call(
        body, name=name, grid=(t // tr, HEADS),
        in_specs=[pl.BlockSpec((tr, dh), lambda i, h: (i, h)), pl.BlockSpec((tr, dh), lambda i, h: (i, cb["g_b"] + h)),
                  pl.BlockSpec((1, dh), lambda i, h: (0, 0))],
        out_specs=pl.BlockSpec((tr, dh), lambda i, h: (i, h)),
        out_shape=jax.ShapeDtypeStruct((t, WIDTH), BF16), compiler_params=_params(("parallel", "parallel")),
    )(o_raw, proj, norm_g)


def gnorm_bwd(dy, o_raw, proj, norm_g, name):
    t = o_raw.shape[0]
    tr = _row_tile(t)
    cb = _col_blocks()
    dh = HEAD_DIM

    def body(dy_ref, o_ref, gb_ref, g_ref, do_ref, dgb_ref, dg_ref):
        @pl.when((pl.program_id(0) == 0) & (pl.program_id(1) == 0))
        def _():
            dg_ref[...] = jnp.zeros_like(dg_ref)

        x = o_ref[...]
        gb = gb_ref[...]
        dyv = dy_ref[...]
        g = g_ref[...]
        dx, dg = _rms_bwd(x, g, dyv * _silu(gb))
        do_ref[...] = dx
        dgb_ref[...] = (dyv * (x * _rstd(x) * g) * _silu_grad(gb)).astype(BF16)
        dg_ref[...] += dg

    hb = pl.BlockSpec((tr, dh), lambda i, h: (i, h))
    vec = pl.BlockSpec((1, dh), lambda i, h: (0, 0))
    return pl.pallas_call(
        body, name=name, grid=(t // tr, HEADS),
        in_specs=[hb, hb, pl.BlockSpec((tr, dh), lambda i, h: (i, cb["g_b"] + h)), vec],
        out_specs=[hb, hb, vec],
        out_shape=[jax.ShapeDtypeStruct((t, WIDTH), F32), jax.ShapeDtypeStruct((t, WIDTH), BF16),
                   jax.ShapeDtypeStruct((1, dh), F32)],
        compiler_params=_params(("arbitrary", "arbitrary")),
    )(dy, o_raw, proj, norm_g)


def merge_fwd(proj, y, name):
    _, t, d = y.shape
    tr = _row_tile(t)
    tc = _pick(d, (1024, 512, 256, 128))
    cb = _col_blocks()
    ga, gb = cb["gate_a"] * LANES // tc, cb["gate_b"] * LANES // tc

    def body(ga_ref, gb_ref, y_ref, o_ref):
        o_ref[...] = (_sigmoid(ga_ref[...]) * y_ref[0] + _sigmoid(gb_ref[...]) * y_ref[1]).astype(BF16)

    return pl.pallas_call(
        body, name=name, grid=(t // tr, d // tc),
        in_specs=[pl.BlockSpec((tr, tc), lambda i, j: (i, ga + j)), pl.BlockSpec((tr, tc), lambda i, j: (i, gb + j)),
                  pl.BlockSpec((2, tr, tc), lambda i, j: (0, i, j))],
        out_specs=pl.BlockSpec((tr, tc), lambda i, j: (i, j)),
        out_shape=jax.ShapeDtypeStruct((t, d), BF16), compiler_params=_params(("parallel", "parallel")),
    )(proj, proj, y)


def merge_bwd(dm, proj, y, name):
    _, t, d = y.shape
    tr = _row_tile(t)
    tc = _pick(d, (1024, 512, 256, 128))
    cb = _col_blocks()
    ga, gb = cb["gate_a"] * LANES // tc, cb["gate_b"] * LANES // tc

    def body(dm_ref, ga_ref, gb_ref, y_ref, dg_ref, dy_ref):
        dmv = dm_ref[...]
        for idx, g_ref in enumerate((ga_ref, gb_ref)):
            s = _sigmoid(g_ref[...])
            dg_ref[idx] = (dmv * y_ref[idx] * s * (1.0 - s)).astype(BF16)
            dy_ref[idx] = (dmv * s).astype(BF16)

    pair = pl.BlockSpec((2, tr, tc), lambda i, j: (0, i, j))
    return pl.pallas_call(
        body, name=name, grid=(t // tr, d // tc),
        in_specs=[pl.BlockSpec((tr, tc), lambda i, j: (i, j)), pl.BlockSpec((tr, tc), lambda i, j: (i, ga + j)),
                  pl.BlockSpec((tr, tc), lambda i, j: (i, gb + j)), pair],
        out_specs=[pair, pair],
        out_shape=[jax.ShapeDtypeStruct((2, t, d), BF16)] * 2, compiler_params=_params(("parallel", "parallel")),
    )(dm, proj, proj, y)


def ple_tail(h, a, b, g, target, name):
    t, d = h.shape
    tr = _row_tile(t)

    def body(h_ref, a_ref, b_ref, g_ref, t_ref, loss_ref, dh_ref, da_ref, db_ref, dg_ref):
        @pl.when(pl.program_id(0) == 0)
        def _():
            loss_ref[...] = jnp.zeros_like(loss_ref)
            dg_ref[...] = jnp.zeros_like(dg_ref)

        s = _sigmoid(a_ref[...])
        bv = b_ref[...]
        z = s * bv
        gv = g_ref[...]
        err = h_ref[...] + z * _rstd(z) * gv - t_ref[...]
        loss_ref[...] += 0.5 * jnp.sum(jnp.sum(err * err, axis=-1, keepdims=True), axis=0, keepdims=True) / d
        dh = err / d
        dh_ref[...] = dh
        dz, dg = _rms_bwd(z, gv, dh)
        da_ref[...] = (dz * bv * s * (1.0 - s)).astype(BF16)
        db_ref[...] = (dz * s).astype(BF16)
        dg_ref[...] += dg

    row = pl.BlockSpec((tr, d), lambda i: (i, 0))
    vec = pl.BlockSpec((1, d), lambda i: (0, 0))
    return pl.pallas_call(
        body, name=name, grid=(t // tr,), in_specs=[row, row, row, vec, row],
        out_specs=[pl.BlockSpec((1, 1), lambda i: (0, 0)), row, row, row, vec],
        out_shape=[jax.ShapeDtypeStruct((1, 1), F32), jax.ShapeDtypeStruct((t, d), F32),
                   jax.ShapeDtypeStruct((t, d), BF16), jax.ShapeDtypeStruct((t, d), BF16),
                   jax.ShapeDtypeStruct((1, d), F32)],
        compiler_params=_params(("arbitrary",)),
    )(h, a, b, g, target)


def _ffn_fwd(h, pre_g, post_g, get_w, idx, tag):
    u = norm_in(h, pre_g, tag + "_norm")
    gu = mm_nn_col(u, get_w("gu" + idx, h), F32, tag + "_gate_up")
    act = swiglu_act(gu, tag + "_act")
    y = mm_nn_2d(act, get_w("down" + idx, gu), F32, tag + "_down")
    out = resid_post(h, y, post_g, MACARON_SCALE, tag + "_out")
    return out, (h, u, gu, act, y)


def _ffn_bwd(dh, saved, pre_g, post_g, get_w, emit, advance, idx, tag):
    h, u, gu, act, y = saved
    dy, d_post = post_bwd(dh, y, post_g, MACARON_SCALE, tag + "_post_bwd")
    m1 = emit("down" + idx, mm_tn_2d(act, dy, F32, tag + "_dw_down"))
    dact = mm_nt_2d(dy, get_w("down" + idx), F32, tag + "_dact", after=[m1])
    m2 = advance(dact)
    dgu = swiglu_bwd(dact, gu, tag + "_act_bwd")
    m3 = emit("gu" + idx, mm_tn_col(u, dgu, N_CHIPS, F32, tag + "_dw_gate_up"))
    du = mm_nt_col(dgu, get_w("gu" + idx), F32, tag + "_du", after=[m2, m3])
    m4 = advance(du)
    dh_in, d_pre = pre_bwd(dh, h, pre_g, [du], tag + "_pre_bwd", after=[m4])
    return dh_in, d_pre, d_post


def _heads_col(a):
    t = a.shape[0]
    at = a[:, :HEADS].T
    return at.reshape(HEADS, t, 1), at.reshape(HEADS, 1, t)


def layer_step(x, p, target, gains, fox_bias, lb_logits, norm_g, get_w, emit, advance):
    t = x.shape[0]
    h1, s1 = _ffn_fwd(x, gains["ffn1_pre"], gains["ffn1_post"], get_w, "1", "ffn1")

    u2 = norm_in(h1, gains["mix_pre"], "mix_norm")
    proj = mm_nt_2d(u2, get_w("in_main", h1), F32, "mix_in")
    fa = mm_nt_2d(u2, get_w("in_fa"), F32, "mix_in_fa")
    c = fox_prep(fa, fox_bias, "fox_prep")
    c_col, c_row = _heads_col(c)
    o_a = fox_fwd(proj, c_col, c_row, "fox_fwd")
    o_raw, states = hgrn_fwd(proj, lb_logits, "hgrn_fwd")
    o_b = gnorm_fwd(o_raw, proj, norm_g, "hgrn_norm")
    o_ab = jnp.stack([o_a, o_b])
    y_ab = _mm_branches(o_ab, get_w("proj", proj), "mix_proj")
    merged = merge_fwd(proj, y_ab, "mix_merge")
    mo = mm_nn_2d(merged, get_w("out"), F32, "mix_out")
    h2 = resid_post(h1, mo, gains["mix_post"], 1.0, "mix_resid")

    h3, s3 = _ffn_fwd(h2, gains["ffn2_pre"], gains["ffn2_post"], get_w, "2", "ffn2")

    u4 = norm_in(h3, gains["ple_pre"], "ple_norm")
    a4 = mm_nn_2d(u4, get_w("ple_gate"), F32, "ple_gate")
    b4 = mm_nn_col(p, get_w("ple_proj"), F32, "ple_proj")[0]
    loss, dh4, da4, db4, d_ple_post = ple_tail(h3, a4, b4, gains["ple_post"], target, "ple_tail")

    marks = [emit("ple_gate", mm_tn_2d(u4, da4, F32, "ple_dw_gate")),
             emit("ple_proj", mm_tn_col(p, db4[None], N_CHIPS, F32, "ple_dw_proj"))]
    du4 = mm_nt_2d(da4, get_w("ple_gate"), F32, "ple_du", after=marks)
    dh3, d_ple_pre = pre_bwd(dh4, h3, gains["ple_pre"], [du4], "ple_pre_bwd", after=[advance(du4)])

    dh2, d_f2_pre, d_f2_post = _ffn_bwd(dh3, s3, gains["ffn2_pre"], gains["ffn2_post"], get_w, emit, advance,
                                        "2", "ffn2")

    dmo, d_mix_post = post_bwd(dh2, mo, gains["mix_post"], 1.0, "mix_post_bwd")
    marks = [emit("out", mm_tn_2d(merged, dmo, F32, "mix_dw_out"))]
    dmerged = mm_nt_2d(dmo, get_w("out"), F32, "mix_dmerged", after=marks)
    dgate, dy_ab = merge_bwd(dmerged, proj, y_ab, "mix_merge_bwd")
    marks = [advance(dmerged), emit("proj", _mm_branches_dw(o_ab, dy_ab, "mix_dw_proj"))]
    do_ab = _mm_branches_bwd(dy_ab, get_w("proj"), "mix_do")
    do_raw, dg_b, d_norm_g = gnorm_bwd(do_ab[1], o_raw, proj, norm_g, "hgrn_norm_bwd")
    dq_b, df_b, di_b, dlb = hgrn_bwd(proj, lb_logits, states, do_raw, "hgrn_bwd")
    d_lb_logits = lb_bwd(dlb, lb_logits, "lb_bwd")
    dq_a, dk_a, dv_a, dc_col, dc_row = fox_bwd(proj, c_col, c_row, do_ab[0], "fox_bwd")
    dc = (dc_col.reshape(HEADS, t) + dc_row.reshape(HEADS, t)).T
    dc = jnp.pad(dc, ((0, 0), (0, LANES - HEADS)))
    dfa, d_fox_bias = fox_post_bwd(dc, fa, fox_bias, "fox_post_bwd")
    dproj = jnp.concatenate([dq_a, dk_a.astype(BF16), dv_a.astype(BF16), dq_b, df_b, di_b, dg_b,
                             dgate[0], dgate[1]], axis=1)
    marks.append(emit("in_main", mm_tn_2d(dproj, u2, F32, "mix_dw_in")))
    marks.append(emit("in_fa", mm_tn_2d(dfa, u2, F32, "mix_dw_in_fa")))
    du2a = mm_nn_2d(dproj, get_w("in_main"), F32, "mix_du", after=marks)
    du2b = mm_nn_2d(dfa, get_w("in_fa"), F32, "mix_du_fa")
    dh1, d_mix_pre = pre_bwd(dh2, h1, gains["mix_pre"], [du2a, du2b], "mix_pre_bwd", after=[advance(du2a)])

    dx, d_f1_pre, d_f1_post = _ffn_bwd(dh1, s1, gains["ffn1_pre"], gains["ffn1_post"], get_w, emit, advance,
                                       "1", "ffn1")

    small = dict(ffn1_pre=d_f1_pre, ffn1_post=d_f1_post, mix_pre=d_mix_pre, mix_post=d_mix_post,
                 ffn2_pre=d_f2_pre, ffn2_post=d_f2_post, ple_pre=d_ple_pre, ple_post=d_ple_post,
                 fox_bias=d_fox_bias, lb_logits=d_lb_logits, norm_g=d_norm_g)
    return loss, dx, small


def _mm_branches(o_ab, w_proj, name):
    g, t, kk = o_ab.shape
    _, jn, _, ns = w_proj.shape
    tm = _pick(t, (512, 256, 128))
    return _mm(o_ab, w_proj, mode="nn", grid=(t // tm, g * jn, 1),
               a_spec=pl.BlockSpec((None, tm, kk), lambda i, j, k: (j // jn, i, 0)),
               b_spec=pl.BlockSpec((None, None, kk, ns), lambda i, j, k: (j // jn, j % jn, 0, 0)),
               o_spec=pl.BlockSpec((None, tm, ns), lambda i, j, k: (j // jn, i, j % jn)),
               out_shape=jax.ShapeDtypeStruct((g, t, jn * ns), F32), acc_shape=(tm, ns), name=name)


def _mm_branches_bwd(dy_ab, w_proj, name):
    g, t, _ = dy_ab.shape
    _, jn, kk, ns = w_proj.shape
    tm = _pick(t, (512, 256, 128))
    return _mm(dy_ab, w_proj, mode="nt", grid=(t // tm, g, jn),
               a_spec=pl.BlockSpec((None, tm, ns), lambda i, j, k: (j, i, k)),
               b_spec=pl.BlockSpec((None, None, kk, ns), lambda i, j, k: (j, k, 0, 0)),
               o_spec=pl.BlockSpec((None, tm, kk), lambda i, j, k: (j, i, 0)),
               out_shape=jax.ShapeDtypeStruct((g, t, kk), F32), acc_shape=(tm, kk), name=name)


def _mm_branches_dw(o_ab, dy_ab, name):
    g, t, kk = o_ab.shape
    d = dy_ab.shape[2]
    jn = N_CHIPS
    ns = d // jn
    return _mm(o_ab, dy_ab, mode="tn", grid=(1, g * jn, 1),
               a_spec=pl.BlockSpec((None, t, kk), lambda i, j, k: (j // jn, 0, 0)),
               b_spec=pl.BlockSpec((None, t, ns), lambda i, j, k: (j // jn, 0, j % jn)),
               o_spec=pl.BlockSpec((None, None, kk, ns), lambda i, j, k: (j // jn, j % jn, 0, 0)),
               out_shape=jax.ShapeDtypeStruct((g, jn, kk, ns), F32), acc_shape=(kk, ns), name=name)


HBM_SPEC = pl.BlockSpec(memory_space=pltpu.HBM)
SEM_SPEC = pl.BlockSpec(memory_space=pltpu.SEMAPHORE)
ANY_SPEC = pl.BlockSpec(memory_space=pl.ANY)
EFFECT = pltpu.SideEffectType.DATAFLOW_SIDE_EFFECTING


def _in_hbm(a):
    return pltpu.with_memory_space_constraint(a, pltpu.HBM)


def _place():
    x, y, c = lax.axis_index("x"), lax.axis_index("y"), lax.axis_index("c")
    chips = [(1 - x, y), (x, 1 - y), (1 - x, 1 - y)]
    return x, y, c, chips


def _half(shape, which, axis):
    n = shape[-2 + axis] // 2
    cut = pl.ds(which * n, n)
    return (cut, slice(None)) if axis == 0 else (slice(None), cut)


def _half_shape(shape, axis):
    s = list(shape)
    s[len(s) - 2 + axis] //= 2
    return tuple(s)


def _remote(src, dst, send_sems, recv_sems, k, to):
    return pltpu.make_async_remote_copy(src_ref=src, dst_ref=dst, send_sem=send_sems.at[k], recv_sem=recv_sems.at[k],
                                        device_id=to, device_id_type=MESH)


def split_start(name, srcs, lands, counts, copies):
    ns, nl, nset = len(srcs), len(lands), len(counts)

    def body(*refs):
        src_refs, land_refs = refs[:ns], refs[ns:ns + nl]
        sems = refs[ns + nl:ns + nl + 2 * nset]
        for s, plan in enumerate(copies(src_refs, land_refs)):
            for k, (src, dst, to) in enumerate(plan):
                _remote(src, dst, sems[2 * s], sems[2 * s + 1], k, to).start()
        refs[-1][...] = jnp.zeros_like(refs[-1])

    out_shape = []
    for n in counts:
        out_shape += [pltpu.SemaphoreType.DMA((n,)), pltpu.SemaphoreType.DMA((n,))]
    out_shape += [pltpu.HBM(a.shape, a.dtype) for a in list(srcs) + list(lands)]
    out_shape.append(jax.ShapeDtypeStruct((8, LANES), F32))
    res = pl.pallas_call(
        body, name=name, out_shape=tuple(out_shape), in_specs=[HBM_SPEC] * (ns + nl),
        out_specs=tuple([SEM_SPEC] * (2 * nset) + [HBM_SPEC] * (ns + nl) + [pl.BlockSpec(memory_space=pltpu.VMEM)]),
        input_output_aliases={i: 2 * nset + i for i in range(ns + nl)},
        compiler_params=pltpu.CompilerParams(has_side_effects=EFFECT),
    )(*[_in_hbm(a) for a in list(srcs) + list(lands)])
    sems = [(res[2 * s], res[2 * s + 1]) for s in range(nset)]
    return sems, list(res[2 * nset:2 * nset + ns]), list(res[2 * nset + ns:-1]), res[-1]


def split_wait(name, srcs, lands, sems, afters, copies):
    afters = [a for a in afters if a is not None]
    ns, nl, na = len(srcs), len(lands), len(afters)

    def body(*refs):
        src_refs, land_refs = refs[:ns], refs[ns:ns + nl]
        send_sems, recv_sems = refs[ns + nl:ns + nl + 2]
        for k, (src, dst, to) in enumerate(copies(src_refs, land_refs)):
            cp = _remote(src, dst, send_sems, recv_sems, k, to)
            cp.wait_send()
            cp.wait_recv()

    res = pl.pallas_call(
        body, name=name, out_shape=tuple(pltpu.HBM(a.shape, a.dtype) for a in list(srcs) + list(lands)),
        in_specs=[HBM_SPEC] * (ns + nl) + [SEM_SPEC, SEM_SPEC] + [ANY_SPEC] * na,
        out_specs=tuple([HBM_SPEC] * (ns + nl)), input_output_aliases={i: i for i in range(ns + nl)},
        compiler_params=pltpu.CompilerParams(has_side_effects=EFFECT),
    )(*srcs, *lands, sems[0], sems[1], *afters)
    return list(res[:ns]), list(res[ns:])


def _gather_plan(blocks):
    def copies(src_refs, land_refs):
        x, y, c, chips = _place()
        j_me = 2 * x + y
        plan = []
        for si, li, g, axis in blocks:
            src, land = src_refs[si], land_refs[li].at[g]
            mine = _half(src.shape, c, axis)
            for px, py in chips:
                plan.append((src.at[mine], land.at[(j_me,) + mine], (px, py, c)))
            plan.append((src, land.at[j_me], (x, y, 1 - c)))
        return plan
    return copies


def _gather_arrivals(blocks):
    def copies(src_refs, land_refs):
        x, y, c, chips = _place()
        j_me = 2 * x + y
        plan = []
        for si, li, g, axis in blocks:
            src, land = src_refs[si], land_refs[li].at[g]
            mine = _half(src.shape, c, axis)
            for px, py in chips:
                plan.append((src.at[mine], land.at[(2 * px + py,) + mine], (px, py, c)))
            plan.append((src, land.at[j_me], (x, y, 1 - c)))
        return plan
    return copies


def gather_pass(name, lands, blocks):
    n = len(lands)

    def body(*refs):
        outs = refs[n:2 * n]
        send_sems, recv_sems = refs[2 * n:]
        x, y, c, chips = _place()
        sent = []
        for i, (li, g, axis) in enumerate(blocks):
            land = outs[li].at[g]
            mine = _half(land.shape[1:], c, axis)
            for k, (px, py) in enumerate(chips):
                part = land.at[(2 * px + py,) + mine]
                cp = _remote(part, part, send_sems, recv_sems, 3 * i + k, (x, y, 1 - c))
                cp.start()
                sent.append(cp)
        for i, (li, g, axis) in enumerate(blocks):
            land = outs[li].at[g]
            other = _half(land.shape[1:], 1 - c, axis)
            for k, (px, py) in enumerate(chips):
                part = land.at[(2 * px + py,) + other]
                _remote(part, part, send_sems, recv_sems, 3 * i + k, (x, y, 1 - c)).wait_recv()
        for cp in sent:
            cp.wait_send()

    m = 3 * len(blocks)
    return pl.pallas_call(
        body, name=name, in_specs=[ANY_SPEC] * n, out_specs=[ANY_SPEC] * n,
        out_shape=[jax.ShapeDtypeStruct(a.shape, a.dtype) for a in lands],
        input_output_aliases={i: i for i in range(n)},
        scratch_shapes=[pltpu.SemaphoreType.DMA((m,)), pltpu.SemaphoreType.DMA((m,))],
    )(*lands)


def _pair_plan(axes):
    def copies(src_refs, land_refs):
        x, y, c, _ = _place()
        return [(src_refs[i].at[(slice(None), slice(None)) + _half(src_refs[i].shape, 1 - c, a)], land_refs[i],
                 (x, y, 1 - c)) for i, a in enumerate(axes)]
    return copies


def _scatter_plan(n):
    def copies(src_refs, land_refs):
        x, y, c, chips = _place()
        return [(src_refs[i].at[:, 2 * px + py], land_refs[i].at[k], (px, py, c))
                for i in range(n) for k, (px, py) in enumerate(chips)]
    return copies


def pair_broadcast(name, bufs, axes):
    n = len(bufs)

    def body(*refs):
        outs = refs[n:2 * n]
        send_sems, recv_sems = refs[2 * n:]
        x, y, c, _ = _place()
        copies = []
        for i in range(n):
            mine = (slice(None),) + _half(outs[i].shape, c, axes[i])
            cp = _remote(outs[i].at[mine], outs[i].at[mine], send_sems, recv_sems, i, (x, y, 1 - c))
            cp.start()
            copies.append(cp)
        for i, cp in enumerate(copies):
            cp.wait_send()
            theirs = (slice(None),) + _half(outs[i].shape, 1 - c, axes[i])
            _remote(outs[i].at[theirs], outs[i].at[theirs], send_sems, recv_sems, i, (x, y, 1 - c)).wait_recv()

    return pl.pallas_call(
        body, name=name, in_specs=[ANY_SPEC] * n, out_specs=[ANY_SPEC] * n,
        out_shape=[jax.ShapeDtypeStruct(b.shape, b.dtype) for b in bufs],
        input_output_aliases={i: i for i in range(n)},
        scratch_shapes=[pltpu.SemaphoreType.DMA((n,)), pltpu.SemaphoreType.DMA((n,))],
    )(*bufs)


N_DEV = 8
SLAB_ROWS = 16


def allreduce_small(slab, after):
    def body(x_ref, after_ref, o_ref, land, send_sems, recv_sems):
        x, y, c, _ = _place()
        me = 4 * x + 2 * y + c
        land[me] = x_ref[...]
        copies = []
        for d in range(1, N_DEV):
            to = (me + d) % N_DEV
            cp = pltpu.make_async_remote_copy(
                src_ref=x_ref, dst_ref=land.at[me], send_sem=send_sems.at[d - 1], recv_sem=recv_sems.at[me],
                device_id=(to // 4, (to // 2) % 2, to % 2), device_id_type=MESH)
            cp.start()
            copies.append(cp)
        for d in range(1, N_DEV):
            frm = (me + d) % N_DEV
            pltpu.make_async_remote_copy(
                src_ref=x_ref, dst_ref=land.at[frm], send_sem=send_sems.at[d - 1], recv_sem=recv_sems.at[frm],
                device_id=(frm // 4, (frm // 2) % 2, frm % 2), device_id_type=MESH).wait_recv()
        for cp in copies:
            cp.wait_send()
        acc = land[0]
        for s in range(1, N_DEV):
            acc = acc + land[s]
        o_ref[...] = acc

    vm = pl.BlockSpec(memory_space=pltpu.VMEM)
    return pl.pallas_call(
        body, name="allreduce_small", in_specs=[vm, ANY_SPEC], out_specs=vm,
        out_shape=jax.ShapeDtypeStruct(slab.shape, F32),
        scratch_shapes=[pltpu.VMEM((N_DEV,) + slab.shape, F32), pltpu.SemaphoreType.DMA((N_DEV - 1,)),
                        pltpu.SemaphoreType.DMA((N_DEV,))],
    )(slab, after)


BLOCK_BYTES = 3 * 1024 * 1024


def _tiles_2d(r, c, budget=BLOCK_BYTES):
    if r % 8 == 0:
        tc = c if c % LANES else _pick(c, (2048, 1408, 1024, 512, 256, 128))
        tr = 8
        for cand in (512, 256, 128, 64, 32, 16, 8):
            if r % cand == 0 and cand * tc * 4 <= budget:
                tr = cand
                break
        return tr, tc
    tc = LANES
    for cand in (1024, 512, 256, 128):
        if c % cand == 0 and r * cand * 4 <= budget:
            tc = cand
            break
    return r, tc


def _grid_spec(grid, in_specs, out_specs):
    return pltpu.PrefetchScalarGridSpec(num_scalar_prefetch=1, grid=grid, in_specs=in_specs, out_specs=out_specs)


def _own(axis, nr, nc):
    if axis == 0:
        return lambda i, j, where: (where[1] * nr + i, j)
    return lambda i, j, where: (i, where[1] * nc + j)


def pair_add(where, grad, recv, axis, name):
    g, jn, hr, hc = recv.shape
    tr, tc = _tiles_2d(hr, hc)
    nr, nc = hr // tr, hc // tc
    own = _own(axis, nr, nc)
    others = jn - 1

    def body(where_ref, a_ref, b_ref, o_ref):
        o_ref[...] = (a_ref[...] + b_ref[...]).astype(BF16)

    def block(a, where):
        return a // others, (where[0] + 1 + a % others) % jn

    blk = pl.BlockSpec((None, None, tr, tc), lambda a, i, j, where: block(a, where) + (i, j))
    mine = pl.BlockSpec((None, None, tr, tc), lambda a, i, j, where: block(a, where) + own(i, j, where))
    return pl.pallas_call(
        body, name=name, grid_spec=_grid_spec((g * others, nr, nc), [mine, blk], blk),
        out_shape=jax.ShapeDtypeStruct(recv.shape, BF16),
        compiler_params=_params(("parallel", "parallel", "parallel")),
    )(where, grad, recv)


def chip_add(where, grad, pair, recv, axis, name):
    g, jn, hr, hc = pair.shape
    tr, tc = _tiles_2d(hr, hc)
    nr, nc = hr // tr, hc // tc
    own = _own(axis, nr, nc)
    full = (g, 2 * hr, hc) if axis == 0 else (g, hr, 2 * hc)

    def body(where_ref, a_ref, p_ref, b_ref, o_ref):
        s = a_ref[...] + p_ref[...]
        for k in range(3):
            s = s + b_ref[k].astype(F32)
        o_ref[...] = s

    return pl.pallas_call(
        body, name=name,
        grid_spec=_grid_spec((g, nr, nc),
                             [pl.BlockSpec((None, None, tr, tc), lambda a, i, j, where: (a, where[0]) + own(i, j, where)),
                              pl.BlockSpec((None, None, tr, tc), lambda a, i, j, where: (a, where[0], i, j)),
                              pl.BlockSpec((3, None, tr, tc), lambda a, i, j, where: (0, a, i, j))],
                             pl.BlockSpec((None, tr, tc), lambda a, i, j, where: (a,) + own(i, j, where))),
        out_shape=jax.ShapeDtypeStruct(full, F32), compiler_params=_params(("parallel", "parallel", "parallel")),
    )(where, grad, pair, recv)


def _adam_math(w, g, m, v):
    m2 = ADAM_B1 * m + (1.0 - ADAM_B1) * g
    v2 = ADAM_B2 * v + (1.0 - ADAM_B2) * (g * g)
    m_hat = m2 / (1.0 - ADAM_B1 ** ADAM_STEP)
    v_hat = v2 / (1.0 - ADAM_B2 ** ADAM_STEP)
    delta = -ADAM_LR * (m_hat / (jnp.sqrt(v_hat) + ADAM_EPS) + ADAM_WD * w)
    return delta, m2, v2


def adamw(grad, idx, w, m, v, name):
    _, r, cc = w.shape
    tr, tc = _tiles_2d(r, cc, BLOCK_BYTES // 2)

    def body(g_ref, w_ref, m_ref, v_ref, go_ref, d_ref, mo_ref, vo_ref):
        g = g_ref[...]
        delta, m2, v2 = _adam_math(w_ref[...], g, m_ref[...], v_ref[...])
        go_ref[...] = g
        d_ref[...] = delta
        mo_ref[...] = m2
        vo_ref[...] = v2

    blk = pl.BlockSpec((None, tr, tc), lambda i, j: (0, i, j))
    return pl.pallas_call(
        body, name=name, grid=(r // tr, cc // tc),
        in_specs=[pl.BlockSpec((None, tr, tc), lambda i, j: (idx, i, j)), blk, blk, blk], out_specs=[blk] * 4,
        out_shape=[jax.ShapeDtypeStruct(w.shape, F32)] * 4, compiler_params=_params(("parallel", "parallel")),
    )(grad, w, m, v)


def adamw_small(g, w, m, v):
    def body(g_ref, w_ref, m_ref, v_ref, d_ref, mo_ref, vo_ref):
        delta, m2, v2 = _adam_math(w_ref[...], g_ref[...], m_ref[...], v_ref[...])
        d_ref[...] = delta
        mo_ref[...] = m2
        vo_ref[...] = v2

    return pl.pallas_call(body, name="adamw_small", out_shape=[jax.ShapeDtypeStruct(w.shape, F32)] * 3)(g, w, m, v)


GAINS = ("ffn1_pre", "ffn1_post", "mix_pre", "mix_post", "ffn2_pre", "ffn2_post", "ple_pre", "ple_post")
WEIGHTS = ("ffn1_pre_g", "ffn1_post_g", "ffn1_w_gate", "ffn1_w_up", "ffn1_w_down", "mix_pre_g", "mix_post_g",
           "mix_w_in", "fox_f_bias", "hgrn_lb_logits", "hgrn_norm_g", "mix_w_proj_fox", "mix_w_proj_hgrn",
           "mix_w_out", "ffn2_pre_g", "ffn2_post_g", "ffn2_w_gate", "ffn2_w_up", "ffn2_w_down", "ple_pre_g",
           "ple_post_g", "ple_w_gate", "ple_w_proj")
GROUPS = dict(gu1=(("ffn1_w_gate", "ffn1_w_up"), 0), down1=(("ffn1_w_down",), 0), win=(("mix_w_in",), 1),
              proj=(("mix_w_proj_fox", "mix_w_proj_hgrn"), 0), out=(("mix_w_out",), 0),
              gu2=(("ffn2_w_gate", "ffn2_w_up"), 0), down2=(("ffn2_w_down",), 0), ple_gate=(("ple_w_gate",), 0),
              ple_proj=(("ple_w_proj",), 0))
TRANSPOSED = ("mix_w_in",)
ROW_BLOCKS = ("down1", "down2", "out", "ple_gate")
GATHER_SETS = (("gu1",), ("down1",), ("win",), ("proj", "out"), ("gu2", "down2", "ple_gate", "ple_proj"))
REDUCE_SETS = (("ple_gate", "ple_proj", "down2", "gu2"), ("out", "proj", "win"), ("down1",), ("gu1",))


def _pad_row(a, width):
    a = a.reshape(1, -1)
    return jnp.pad(a, ((0, 0), (0, width - a.shape[1])))


def _pack_small(vals):
    d = D_MODEL
    rows = [vals[n + "_g"].reshape(1, d) for n in GAINS]
    rows.append(_pad_row(vals["fox_f_bias"], d))
    lg = vals["hgrn_lb_logits"]
    rows += [_pad_row(lg[0], d), _pad_row(lg[1], d), _pad_row(vals["hgrn_norm_g"], d)]
    slab = jnp.concatenate(rows, axis=0)
    return jnp.pad(slab, ((0, SLAB_ROWS - slab.shape[0]), (0, 0)))


def _unpack_small(slab):
    out = {n + "_g": slab[i:i + 1] for i, n in enumerate(GAINS)}
    out["fox_f_bias"] = slab[8:9, :HEADS]
    out["hgrn_lb_logits"] = slab[9:11, :WIDTH]
    out["hgrn_norm_g"] = slab[11:12, :HEAD_DIM]
    return out


def _split_in(win_t):
    lo = 3 * WIDTH
    main = jnp.concatenate([win_t[:lo], win_t[lo + HEADS:]], axis=0)
    fa = jnp.pad(win_t[lo:lo + HEADS], ((0, LANES - HEADS), (0, 0)))
    return main, fa


def _join_in(main, fa):
    lo = 3 * WIDTH
    return jnp.concatenate([main[:lo], fa[:HEADS], main[lo:]], axis=0)


def _as_block(name, a):
    return jnp.swapaxes(a, 1, 2) if name in TRANSPOSED else a


def kernel(x, p, ffn1_pre_g, ffn1_post_g, ffn1_w_gate, ffn1_w_up, ffn1_w_down, mix_pre_g, mix_post_g, mix_w_in, fox_f_bias, hgrn_lb_logits, hgrn_norm_g, mix_w_proj_fox, mix_w_proj_hgrn, mix_w_out, ffn2_pre_g, ffn2_post_g, ffn2_w_gate, ffn2_w_up, ffn2_w_down, ple_pre_g, ple_post_g, ple_w_gate, ple_w_proj, loss_target, m_ffn1_pre_g, m_ffn1_post_g, m_ffn1_w_gate, m_ffn1_w_up, m_ffn1_w_down, m_mix_pre_g, m_mix_post_g, m_mix_w_in, m_fox_f_bias, m_hgrn_lb_logits, m_hgrn_norm_g, m_mix_w_proj_fox, m_mix_w_proj_hgrn, m_mix_w_out, m_ffn2_pre_g, m_ffn2_post_g, m_ffn2_w_gate, m_ffn2_w_up, m_ffn2_w_down, m_ple_pre_g, m_ple_post_g, m_ple_w_gate, m_ple_w_proj, v_ffn1_pre_g, v_ffn1_post_g, v_ffn1_w_gate, v_ffn1_w_up, v_ffn1_w_down, v_mix_pre_g, v_mix_post_g, v_mix_w_in, v_fox_f_bias, v_hgrn_lb_logits, v_hgrn_norm_g, v_mix_w_proj_fox, v_mix_w_proj_hgrn, v_mix_w_out, v_ffn2_pre_g, v_ffn2_post_g, v_ffn2_w_gate, v_ffn2_w_up, v_ffn2_w_down, v_ple_pre_g, v_ple_post_g, v_ple_w_gate, v_ple_w_proj):
    args = dict(locals())
    wts = {n: args[n] for n in WEIGHTS}
    mom = {n: args["m_" + n] for n in WEIGHTS}
    var = {n: args["v_" + n] for n in WEIGHTS}
    d = D_MODEL
    where = jnp.stack([2 * lax.axis_index("x") + lax.axis_index("y"), lax.axis_index("c")]).astype(jnp.int32)

    order = [g for s in GATHER_SETS for g in s]
    srcs, lands, plans = [], [], []
    for s in GATHER_SETS:
        blocks = []
        for g in s:
            names, axis = GROUPS[g]
            for pos, n in enumerate(names):
                blocks.append((len(srcs), order.index(g), pos, axis))
                srcs.append(_as_block(n, wts[n])[0].astype(BF16))
        plans.append(blocks)
    for g in order:
        names, _ = GROUPS[g]
        r, c = _as_block(names[0], wts[names[0]]).shape[1:]
        lands.append(lax.empty((len(names), N_CHIPS, r, c), BF16))
    sems, srcs, lands, _ = split_start(
        "gather_start", srcs, lands, [4 * len(b) for b in plans],
        lambda sr, lr: [_gather_plan(b)(sr, lr) for b in plans])
    full = {}

    def land_set(si, after):
        blocks = plans[si]
        s_idx = sorted({b[0] for b in blocks})
        l_idx = sorted({b[1] for b in blocks})
        local = [(s_idx.index(a), l_idx.index(b), g, ax) for a, b, g, ax in blocks]
        _, got = split_wait("gather_wait_%d" % si, [srcs[i] for i in s_idx], [lands[i] for i in l_idx], sems[si],
                            [after], _gather_arrivals(local))
        got = gather_pass("gather_pass_%d" % si, got, [(b, g, ax) for _, b, g, ax in local])
        for i, arr in zip(l_idx, got):
            g = order[i]
            if g == "win":
                full["win"] = arr
                full["in_main"], full["in_fa"] = _split_in(arr.reshape(-1, d))
            else:
                full[g] = arr.reshape(-1, d) if g in ROW_BLOCKS else arr

    def get_w(key, after=None):
        g = "win" if key in ("in_main", "in_fa") else key
        if g not in full:
            land_set([g in s for s in GATHER_SETS].index(True), after)
        return full[key]

    grads, pairing, started = {}, [], {}
    rows4 = lambda a: a.reshape(1, N_CHIPS, a.shape[0] // N_CHIPS, a.shape[1])

    def emit(key, grad):
        if key in ("in_main", "in_fa"):
            grads[key] = grad
            if "in_main" not in grads or "in_fa" not in grads:
                return None
            key, grad = "win", rows4(_join_in(grads["in_main"], grads["in_fa"]))
        grads[key] = grad if grad.ndim == 4 else rows4(grad)
        for si, s in enumerate(REDUCE_SETS):
            if key in s and all(g in grads for g in s):
                axes = [GROUPS[g][1] for g in s]
                own = [grads[g] for g in s]
                zones = [lax.empty(_half_shape(a.shape, ax), F32) for a, ax in zip(own, axes)]
                plan = _pair_plan(axes)
                sem, own, zones, mark = split_start("pair_start_%d" % si, own, zones, [len(s)],
                                                    lambda sr, lr: [plan(sr, lr)])
                pairing.append((si, sem[0], own, zones, axes, plan))
                return mark
        return None

    def advance(value):
        mark = None
        while pairing:
            si, sem, own, zones, axes, plan = pairing.pop(0)
            s = REDUCE_SETS[si]
            own, recv = split_wait("pair_wait_%d" % si, own, zones, sem, [value], plan)
            parts = [pair_add(where, a, r, ax, "pair_add_" + g) for g, a, r, ax in zip(s, own, recv, axes)]
            zones = [lax.empty((3, q.shape[0]) + q.shape[2:], BF16) for q in parts]
            plan = _scatter_plan(len(s))
            sem, parts, zones, mark = split_start("scatter_start_%d" % si, parts, zones, [3 * len(s)],
                                                  lambda sr, lr: [plan(sr, lr)])
            started[si] = (sem[0], parts, zones, own, recv, axes, plan)
        return mark

    gains = {n: wts[n + "_g"] for n in GAINS}
    loss, dx, small = layer_step(x[0], p[0, 0].astype(BF16), loss_target[0], gains, _pad_row(fox_f_bias, LANES),
                                 hgrn_lb_logits, hgrn_norm_g, get_w, emit, advance)

    out_g, out_d, out_m, out_v = {}, {}, {}, {}
    after = None
    for si, s in enumerate(REDUCE_SETS):
        sem, parts, zones, own, recv, axes, plan = started[si]
        _, zones = split_wait("scatter_wait_%d" % si, parts, zones, sem, [dx, after], plan)
        halves = [chip_add(where, a, r, z, ax, "chip_add_" + g) for g, a, r, z, ax in zip(s, own, recv, zones, axes)]
        reduced = pair_broadcast("pair_broadcast_%d" % si, halves, axes)
        for g, red in zip(s, reduced):
            for idx, n in enumerate(GROUPS[g][0]):
                res = adamw(red, idx, _as_block(n, wts[n]), _as_block(n, mom[n]), _as_block(n, var[n]), "adamw_" + n)
                out_g[n], out_d[n], out_m[n], out_v[n] = [_as_block(n, r) for r in res]
                after = res[1]

    small_named = {n + "_g": small[n] for n in GAINS}
    small_named.update(fox_f_bias=small["fox_bias"][:, :HEADS], hgrn_lb_logits=small["lb_logits"],
                       hgrn_norm_g=small["norm_g"])
    g_small = allreduce_small(_pack_small(small_named), after)
    d_small, m_small, v_small = adamw_small(g_small, _pack_small(wts), _pack_small(mom), _pack_small(var))

    for dst, slab in ((out_g, g_small), (out_d, d_small), (out_m, m_small), (out_v, v_small)):
        dst.update(_unpack_small(slab))

    total = lax.psum(loss[0, 0], ("x", "y", "c"))
    return (total, dx[None], *[out_g[n] for n in WEIGHTS], *[out_d[n] for n in WEIGHTS],
            *[out_m[n] for n in WEIGHTS], *[out_v[n] for n in WEIGHTS])
```

```python
import functools

import jax
import jax.numpy as jnp
from jax import lax
from jax.experimental import pallas as pl
from jax.experimental.pallas import tpu as pltpu

F32 = jnp.float32
BF16 = jnp.bfloat16

D_MODEL = 2048
SEQ = 2048
D_FF = 5632
PLE_DIM = 256
HEADS = 8
HEAD_DIM = 128
WIDTH = HEADS * HEAD_DIM
CHUNK = 64
SUB = 16
HGRN_HEADS_PER_STEP = 2
NORM_EPS = 1e-6
MACARON_SCALE = 0.5
N_CHIPS = 4

ADAM_LR = 0.001
ADAM_B1 = 0.9
ADAM_B2 = 0.999
ADAM_EPS = 1e-08
ADAM_WD = 0.01
ADAM_STEP = 10

LANES = 128
VMEM_LIMIT = 56 * 1024 * 1024
NEG_BIG = -1e30
MESH = pl.DeviceIdType.MESH


def _pick(n, cands):
    for c in cands:
        if c <= n and n % c == 0:
            return c
    return n


def _params(sem, vmem=VMEM_LIMIT):
    return pltpu.CompilerParams(dimension_semantics=sem, vmem_limit_bytes=vmem)


def _sigmoid(x):
    return 1.0 / (1.0 + jnp.exp(-x))


def _silu(x):
    return x * _sigmoid(x)


def _silu_grad(x):
    s = _sigmoid(x)
    return s * (1.0 + x * (1.0 - s))


_DN = {"nn": (((1,), (0,)), ((), ())), "nt": (((1,), (1,)), ((), ())), "tn": (((0,), (0,)), ((), ()))}


def _mm(a, b, *, mode, grid, a_spec, b_spec, o_spec, out_shape, acc_shape, name, after=()):
    nk = grid[2]
    dn = _DN[mode]
    after = [m for m in after if m is not None]

    def body(a_ref, b_ref, *rest):
        o_ref, acc_ref = rest[len(after):]
        k = pl.program_id(2)

        @pl.when(k == 0)
        def _():
            acc_ref[...] = jnp.zeros_like(acc_ref)

        acc_ref[...] += lax.dot_general(a_ref[...].astype(BF16), b_ref[...].astype(BF16), dn,
                                        preferred_element_type=F32)

        @pl.when(k == nk - 1)
        def _():
            o_ref[...] = acc_ref[...].astype(o_ref.dtype)

    return pl.pallas_call(
        body, name=name, grid=grid, in_specs=[a_spec, b_spec] + [pl.BlockSpec(memory_space=pl.ANY)] * len(after),
        out_specs=o_spec, out_shape=out_shape, scratch_shapes=[pltpu.VMEM(acc_shape, F32)],
        compiler_params=_params(("parallel", "parallel", "arbitrary")),
    )(a, b, *after)


def mm_nn_2d(a, b, out_dtype, name, after=()):
    m, kk = a.shape
    n = b.shape[1]
    tm, tn, tk = _pick(m, (512, 256, 128)), _pick(n, (1024, 512, 256, 128)), _pick(kk, (2048, 1408, 1024, 512, 256, 128))
    return _mm(a, b, mode="nn", grid=(m // tm, n // tn, kk // tk),
               a_spec=pl.BlockSpec((tm, tk), lambda i, j, k: (i, k)),
               b_spec=pl.BlockSpec((tk, tn), lambda i, j, k: (k, j)),
               o_spec=pl.BlockSpec((tm, tn), lambda i, j, k: (i, j)),
               out_shape=jax.ShapeDtypeStruct((m, n), out_dtype), acc_shape=(tm, tn), name=name, after=after)


def mm_nt_2d(a, b, out_dtype, name, after=()):
    m, c = a.shape
    n = b.shape[0]
    tm, tn, tk = _pick(m, (512, 256, 128)), _pick(n, (1408, 1024, 512, 256, 128)), _pick(c, (2048, 1408, 1024, 512, 256, 128))
    return _mm(a, b, mode="nt", grid=(m // tm, n // tn, c // tk),
               a_spec=pl.BlockSpec((tm, tk), lambda i, j, k: (i, k)),
               b_spec=pl.BlockSpec((tn, tk), lambda i, j, k: (j, k)),
               o_spec=pl.BlockSpec((tm, tn), lambda i, j, k: (i, j)),
               out_shape=jax.ShapeDtypeStruct((m, n), out_dtype), acc_shape=(tm, tn), name=name, after=after)


def mm_tn_2d(a, b, out_dtype, name):
    c, m = a.shape
    n = b.shape[1]
    tm, tn, tk = _pick(m, (1408, 1024, 512, 256, 128)), _pick(n, (1024, 512, 256, 128)), _pick(c, (2048, 1024, 512, 256, 128))
    return _mm(a, b, mode="tn", grid=(m // tm, n // tn, c // tk),
               a_spec=pl.BlockSpec((tk, tm), lambda i, j, k: (k, i)),
               b_spec=pl.BlockSpec((tk, tn), lambda i, j, k: (k, j)),
               o_spec=pl.BlockSpec((tm, tn), lambda i, j, k: (i, j)),
               out_shape=jax.ShapeDtypeStruct((m, n), out_dtype), acc_shape=(tm, tn), name=name)


def mm_nn_col(a, w, out_dtype, name):
    m, kk = a.shape
    g, jn, _, ns = w.shape
    tm, tk = _pick(m, (512, 256, 128)), _pick(kk, (2048, 1024, 512, 256, 128))
    return _mm(a, w, mode="nn", grid=(m // tm, g * jn, kk // tk),
               a_spec=pl.BlockSpec((tm, tk), lambda i, j, k: (i, k)),
               b_spec=pl.BlockSpec((None, None, tk, ns), lambda i, j, k: (j // jn, j % jn, k, 0)),
               o_spec=pl.BlockSpec((None, tm, ns), lambda i, j, k: (j // jn, i, j % jn)),
               out_shape=jax.ShapeDtypeStruct((g, m, jn * ns), out_dtype), acc_shape=(tm, ns), name=name)


def mm_nt_col(a, w, out_dtype, name, after=()):
    g, m, _ = a.shape
    _, jn, kk, ns = w.shape
    tm, tn = _pick(m, (512, 256, 128)), _pick(kk, (1024, 512, 256, 128))
    return _mm(a, w, mode="nt", grid=(m // tm, kk // tn, g * jn),
               a_spec=pl.BlockSpec((None, tm, ns), lambda i, j, k: (k // jn, i, k % jn)),
               b_spec=pl.BlockSpec((None, None, tn, ns), lambda i, j, k: (k // jn, k % jn, j, 0)),
               o_spec=pl.BlockSpec((tm, tn), lambda i, j, k: (i, j)),
               out_shape=jax.ShapeDtypeStruct((m, kk), out_dtype), acc_shape=(tm, tn), name=name, after=after)


def mm_tn_col(a, b, jn, out_dtype, name):
    c, kk = a.shape
    g, _, n = b.shape
    ns = n // jn
    tm, tk = _pick(kk, (512, 256, 128)), _pick(c, (2048, 1024, 512, 256, 128))
    return _mm(a, b, mode="tn", grid=(kk // tm, g * jn, c // tk),
               a_spec=pl.BlockSpec((tk, tm), lambda i, j, k: (k, i)),
               b_spec=pl.BlockSpec((None, tk, ns), lambda i, j, k: (j // jn, k, j % jn)),
               o_spec=pl.BlockSpec((None, None, tm, ns), lambda i, j, k: (j // jn, j % jn, i, 0)),
               out_shape=jax.ShapeDtypeStruct((g, jn, kk, ns), out_dtype), acc_shape=(tm, ns), name=name)


def _rstd(x):
    return lax.rsqrt(jnp.mean(x * x, axis=-1, keepdims=True) + NORM_EPS)


def _rms_bwd(x, g, dy):
    r = _rstd(x)
    xn = x * r
    dyg = dy * g
    dx = r * (dyg - xn * jnp.mean(dyg * xn, axis=-1, keepdims=True))
    return dx, jnp.sum(dy * xn, axis=0, keepdims=True)


def _row_tile(t):
    return _pick(t, (256, 128, 64, 32, 16, 8))


def norm_in(h, g, name):
    t, d = h.shape
    tr = _row_tile(t)

    def body(h_ref, g_ref, u_ref):
        x = h_ref[...]
        u_ref[...] = (x * _rstd(x) * g_ref[...]).astype(BF16)

    return pl.pallas_call(
        body, name=name, grid=(t // tr,),
        in_specs=[pl.BlockSpec((tr, d), lambda i: (i, 0)), pl.BlockSpec((1, d), lambda i: (0, 0))],
        out_specs=pl.BlockSpec((tr, d), lambda i: (i, 0)),
        out_shape=jax.ShapeDtypeStruct((t, d), BF16), compiler_params=_params(("parallel",)),
    )(h, g)


def resid_post(h, y, g, scale, name):
    t, d = h.shape
    tr = _row_tile(t)

    def body(h_ref, y_ref, g_ref, o_ref):
        yv = y_ref[...]
        o_ref[...] = h_ref[...] + scale * (yv * _rstd(yv) * g_ref[...])

    row = pl.BlockSpec((tr, d), lambda i: (i, 0))
    return pl.pallas_call(
        body, name=name, grid=(t // tr,), in_specs=[row, row, pl.BlockSpec((1, d), lambda i: (0, 0))],
        out_specs=row, out_shape=jax.ShapeDtypeStruct((t, d), F32), compiler_params=_params(("parallel",)),
    )(h, y, g)


def post_bwd(dh, y, g, scale, name):
    t, d = dh.shape
    tr = _row_tile(t)

    def body(dh_ref, y_ref, g_ref, dy_ref, dg_ref):
        @pl.when(pl.program_id(0) == 0)
        def _():
            dg_ref[...] = jnp.zeros_like(dg_ref)

        dx, dg = _rms_bwd(y_ref[...], g_ref[...], scale * dh_ref[...])
        dy_ref[...] = dx.astype(BF16)
        dg_ref[...] += dg

    row = pl.BlockSpec((tr, d), lambda i: (i, 0))
    vec = pl.BlockSpec((1, d), lambda i: (0, 0))
    return pl.pallas_call(
        body, name=name, grid=(t // tr,), in_specs=[row, row, vec], out_specs=[row, vec],
        out_shape=[jax.ShapeDtypeStruct((t, d), BF16), jax.ShapeDtypeStruct((1, d), F32)],
        compiler_params=_params(("arbitrary",)),
    )(dh, y, g)


def pre_bwd(dh, h, g, dus, name, after=()):
    t, d = dh.shape
    tr = _row_tile(t)
    n_du = len(dus)
    after = [m for m in after if m is not None]

    def body(*refs):
        dh_ref, h_ref, g_ref = refs[:3]
        du_refs = refs[3:3 + n_du]
        o_ref, dg_ref = refs[3 + n_du + len(after):]

        @pl.when(pl.program_id(0) == 0)
        def _():
            dg_ref[...] = jnp.zeros_like(dg_ref)

        du = du_refs[0][...]
        for r in du_refs[1:]:
            du = du + r[...]
        dx, dg = _rms_bwd(h_ref[...], g_ref[...], du)
        o_ref[...] = dh_ref[...] + dx
        dg_ref[...] += dg

    row = pl.BlockSpec((tr, d), lambda i: (i, 0))
    vec = pl.BlockSpec((1, d), lambda i: (0, 0))
    return pl.pallas_call(
        body, name=name, grid=(t // tr,),
        in_specs=[row, row, vec] + [row] * n_du + [pl.BlockSpec(memory_space=pl.ANY)] * len(after),
        out_specs=[row, vec], out_shape=[jax.ShapeDtypeStruct((t, d), F32), jax.ShapeDtypeStruct((1, d), F32)],
        compiler_params=_params(("arbitrary",)),
    )(dh, h, g, *dus, *after)


def _ew_tiles(t, f):
    return _pick(t, (256, 128, 64, 32, 16, 8)), _pick(f, (1408, 1024, 512, 256, 128))


def swiglu_act(gu, name):
    _, t, f = gu.shape
    tr, tc = _ew_tiles(t, f)

    def body(gu_ref, o_ref):
        o_ref[...] = (_silu(gu_ref[0]) * gu_ref[1]).astype(BF16)

    return pl.pallas_call(
        body, name=name, grid=(t // tr, f // tc),
        in_specs=[pl.BlockSpec((2, tr, tc), lambda i, j: (0, i, j))],
        out_specs=pl.BlockSpec((tr, tc), lambda i, j: (i, j)),
        out_shape=jax.ShapeDtypeStruct((t, f), BF16), compiler_params=_params(("parallel", "parallel")),
    )(gu)


def swiglu_bwd(dact, gu, name):
    _, t, f = gu.shape
    tr, tc = _ew_tiles(t, f)

    def body(da_ref, gu_ref, o_ref):
        da = da_ref[...]
        gate = gu_ref[0]
        o_ref[0] = (da * gu_ref[1] * _silu_grad(gate)).astype(BF16)
        o_ref[1] = (da * _silu(gate)).astype(BF16)

    return pl.pallas_call(
        body, name=name, grid=(t // tr, f // tc),
        in_specs=[pl.BlockSpec((tr, tc), lambda i, j: (i, j)), pl.BlockSpec((2, tr, tc), lambda i, j: (0, i, j))],
        out_specs=pl.BlockSpec((2, tr, tc), lambda i, j: (0, i, j)),
        out_shape=jax.ShapeDtypeStruct((2, t, f), BF16), compiler_params=_params(("parallel", "parallel")),
    )(dact, gu)


def _col_blocks():
    w = WIDTH // LANES
    return dict(q_a=0, k_a=w, v_a=2 * w, q_b=3 * w, f_b=4 * w, i_b=5 * w, g_b=6 * w, gate_a=7 * w,
                gate_b=7 * w + D_MODEL // LANES)


def _tri(n, lower):
    r = lax.broadcasted_iota(jnp.int32, (n, n), 0)
    c = lax.broadcasted_iota(jnp.int32, (n, n), 1)
    return jnp.where((r >= c) if lower else (r <= c), 1.0, 0.0).astype(F32)


def _dot_hi(a, b):
    return jnp.dot(a, b, precision=lax.Precision.HIGHEST, preferred_element_type=F32)


def fox_prep(fa, bias, name):
    t, w = fa.shape
    tb = _pick(t, (256, 128, 64))

    def body(fa_ref, b_ref, c_ref, carry_ref):
        @pl.when(pl.program_id(0) == 0)
        def _():
            carry_ref[...] = jnp.zeros_like(carry_ref)

        z = fa_ref[...] + b_ref[...]
        lf = jnp.minimum(z, 0.0) - jnp.log(1.0 + jnp.exp(-jnp.abs(z)))
        c = _dot_hi(_tri(tb, True), lf) + carry_ref[...]
        c_ref[...] = c
        carry_ref[...] = carry_ref[...] + jnp.sum(lf, axis=0, keepdims=True)

    return pl.pallas_call(
        body, name=name, grid=(t // tb,),
        in_specs=[pl.BlockSpec((tb, w), lambda i: (i, 0)), pl.BlockSpec((1, w), lambda i: (0, 0))],
        out_specs=pl.BlockSpec((tb, w), lambda i: (i, 0)),
        out_shape=jax.ShapeDtypeStruct((t, w), F32), scratch_shapes=[pltpu.VMEM((1, w), F32)],
        compiler_params=_params(("arbitrary",)),
    )(fa, bias)


def fox_post_bwd(dc, fa, bias, name):
    t, w = fa.shape
    tb = _pick(t, (256, 128, 64))
    nb = t // tb

    def body(dc_ref, fa_ref, b_ref, dfa_ref, db_ref, carry_ref):
        @pl.when(pl.program_id(0) == 0)
        def _():
            carry_ref[...] = jnp.zeros_like(carry_ref)
            db_ref[...] = jnp.zeros_like(db_ref)

        dcv = dc_ref[...]
        dlf = _dot_hi(_tri(tb, False), dcv) + carry_ref[...]
        z = fa_ref[...] + b_ref[...]
        dz = dlf * _sigmoid(-z)
        dfa_ref[...] = dz.astype(BF16)
        db_ref[...] += jnp.sum(dz, axis=0, keepdims=True)
        carry_ref[...] = carry_ref[...] + jnp.sum(dcv, axis=0, keepdims=True)

    rev = pl.BlockSpec((tb, w), lambda i: (nb - 1 - i, 0))
    vec = pl.BlockSpec((1, w), lambda i: (0, 0))
    return pl.pallas_call(
        body, name=name, grid=(nb,), in_specs=[rev, rev, vec], out_specs=[rev, vec],
        out_shape=[jax.ShapeDtypeStruct((t, w), BF16), jax.ShapeDtypeStruct((1, w), F32)],
        scratch_shapes=[pltpu.VMEM((1, w), F32)], compiler_params=_params(("arbitrary",)),
    )(dc, fa, bias)


def _fox_probs(q_ref, k_ref, cc_ref, cr_ref, qi, tq, t):
    scale = HEAD_DIM ** -0.5
    s = lax.dot_general(q_ref[...].astype(BF16), k_ref[...].astype(BF16), _DN["nt"], preferred_element_type=F32)
    logits = s * scale + cc_ref[...] - cr_ref[...]
    qpos = qi * tq + lax.broadcasted_iota(jnp.int32, (tq, t), 0)
    kpos = lax.broadcasted_iota(jnp.int32, (tq, t), 1)
    logits = jnp.where(kpos <= qpos, logits, NEG_BIG)
    m = jnp.max(logits, axis=-1, keepdims=True)
    p = jnp.exp(logits - m)
    return p / jnp.sum(p, axis=-1, keepdims=True)


def fox_fwd(proj, c_col, c_row, name):
    t = proj.shape[0]
    tq = _pick(t, (256, 128))
    cb = _col_blocks()
    dh = HEAD_DIM

    def body(q_ref, k_ref, v_ref, cc_ref, cr_ref, o_ref):
        p = _fox_probs(q_ref, k_ref, cc_ref, cr_ref, pl.program_id(1), tq, t)
        o_ref[...] = jnp.dot(p.astype(BF16), v_ref[...].astype(BF16), preferred_element_type=F32).astype(BF16)

    return pl.pallas_call(
        body, name=name, grid=(HEADS, t // tq),
        in_specs=[pl.BlockSpec((tq, dh), lambda h, i: (i, cb["q_a"] + h)),
                  pl.BlockSpec((t, dh), lambda h, i: (0, cb["k_a"] + h)),
                  pl.BlockSpec((t, dh), lambda h, i: (0, cb["v_a"] + h)),
                  pl.BlockSpec((None, tq, 1), lambda h, i: (h, i, 0)),
                  pl.BlockSpec((None, 1, t), lambda h, i: (h, 0, 0))],
        out_specs=pl.BlockSpec((tq, dh), lambda h, i: (i, h)),
        out_shape=jax.ShapeDtypeStruct((t, WIDTH), BF16), compiler_params=_params(("parallel", "parallel")),
    )(proj, proj, proj, c_col, c_row)


def fox_bwd(proj, c_col, c_row, do, name):
    t = proj.shape[0]
    tq = _pick(t, (256, 128))
    cb = _col_blocks()
    dh = HEAD_DIM
    scale = HEAD_DIM ** -0.5

    def body(q_ref, k_ref, v_ref, cc_ref, cr_ref, do_ref, dq_ref, dk_ref, dv_ref, dcc_ref, dcr_ref):
        @pl.when(pl.program_id(1) == 0)
        def _():
            dk_ref[...] = jnp.zeros_like(dk_ref)
            dv_ref[...] = jnp.zeros_like(dv_ref)
            dcr_ref[...] = jnp.zeros_like(dcr_ref)

        p = _fox_probs(q_ref, k_ref, cc_ref, cr_ref, pl.program_id(1), tq, t)
        dov = do_ref[...].astype(BF16)
        kb = k_ref[...].astype(BF16)
        dv_ref[...] += lax.dot_general(p.astype(BF16), dov, _DN["tn"], preferred_element_type=F32)
        dp = lax.dot_general(dov, v_ref[...].astype(BF16), _DN["nt"], preferred_element_type=F32)
        ds = p * (dp - jnp.sum(p * dp, axis=-1, keepdims=True))
        dcc_ref[...] = jnp.sum(ds, axis=-1, keepdims=True)
        dcr_ref[...] -= jnp.sum(ds, axis=0, keepdims=True)
        dss = (ds * scale).astype(BF16)
        dq_ref[...] = jnp.dot(dss, kb, preferred_element_type=F32).astype(BF16)
        dk_ref[...] += lax.dot_general(dss, q_ref[...].astype(BF16), _DN["tn"], preferred_element_type=F32)

    return pl.pallas_call(
        body, name=name, grid=(HEADS, t // tq),
        in_specs=[pl.BlockSpec((tq, dh), lambda h, i: (i, cb["q_a"] + h)),
                  pl.BlockSpec((t, dh), lambda h, i: (0, cb["k_a"] + h)),
                  pl.BlockSpec((t, dh), lambda h, i: (0, cb["v_a"] + h)),
                  pl.BlockSpec((None, tq, 1), lambda h, i: (h, i, 0)),
                  pl.BlockSpec((None, 1, t), lambda h, i: (h, 0, 0)),
                  pl.BlockSpec((tq, dh), lambda h, i: (i, h))],
        out_specs=[pl.BlockSpec((tq, dh), lambda h, i: (i, h)),
                   pl.BlockSpec((t, dh), lambda h, i: (0, h)),
                   pl.BlockSpec((t, dh), lambda h, i: (0, h)),
                   pl.BlockSpec((None, tq, 1), lambda h, i: (h, i, 0)),
                   pl.BlockSpec((None, 1, t), lambda h, i: (h, 0, 0))],
        out_shape=[jax.ShapeDtypeStruct((t, WIDTH), BF16), jax.ShapeDtypeStruct((t, WIDTH), F32),
                   jax.ShapeDtypeStruct((t, WIDTH), F32), jax.ShapeDtypeStruct((HEADS, t, 1), F32),
                   jax.ShapeDtypeStruct((HEADS, 1, t), F32)],
        compiler_params=_params(("parallel", "arbitrary")),
    )(proj, proj, proj, c_col, c_row, do)


def _lower_bound(lg_ref):
    l0 = lg_ref[0:1, :]
    l1 = lg_ref[1:2, :]
    m = jnp.maximum(l0, l1)
    e0 = jnp.exp(l0 - m)
    e1 = jnp.exp(l1 - m)
    return e0 / (e0 + e1)


def _hgrn_inputs(qb_ref, fb_ref, lg_ref, q_s, k_s, cum_s):
    lb = _lower_bound(lg_ref)
    sig = _sigmoid(fb_ref[...])
    f = lb + (1.0 - lb) * sig
    q_s[...] = _silu(qb_ref[...])
    k_s[...] = 1.0 - f
    cum_s[...] = _dot_hi(_tri(CHUNK, True), jnp.log(f))
    return lb, sig, f


def _boundary(cum_s, a):
    if a == 0:
        return jnp.zeros((1, HEAD_DIM), F32)
    return cum_s[pl.ds(SUB * a - 1, 1), :]


def _hgrn_scores(q_s, k_s, cum_s):
    cum = cum_s[...]
    kk = k_s[...]
    lane = lax.broadcasted_iota(jnp.int32, (SUB, CHUNK), 1)
    row = lax.broadcasted_iota(jnp.int32, (SUB, 1), 0)
    blocks = []
    for a in range(CHUNK // SUB):
        rows = pl.ds(SUB * a, SUB)
        ca = _boundary(cum_s, a)
        cum_a = cum_s[rows, :]
        q_a = q_s[rows, :]
        qa = q_a * jnp.exp(cum_a - ca)
        ka = kk * jnp.exp(jnp.minimum(ca - cum, 0.0))
        blk = lax.dot_general(qa, ka, _DN["nt"], preferred_element_type=F32)
        blk = jnp.where(lane < SUB * a, blk, 0.0)
        for s in range(SUB):
            r = SUB * a + s
            e = jnp.exp(jnp.minimum(cum_a - cum_s[pl.ds(r, 1), :], 0.0))
            col = jnp.sum(q_a * k_s[pl.ds(r, 1), :] * e, axis=-1, keepdims=True)
            col = jnp.where(row >= s, col, 0.0)
            blk = jnp.where(lane == r, col, blk)
        blocks.append(blk)
    return jnp.concatenate(blocks, axis=0)


def hgrn_fwd(proj, lb_logits, name):
    t = proj.shape[0]
    n = t // CHUNK
    cb = _col_blocks()
    dh = HEAD_DIM

    hb = HGRN_HEADS_PER_STEP
    w = hb * dh

    def one_head(qb_ref, fb_ref, ib_ref, lg_ref, o_ref, st_ref, state, q_s, k_s, cum_s):
        _hgrn_inputs(qb_ref, fb_ref, lg_ref, q_s, k_s, cum_s)
        st = state[...]
        st_ref[...] = st
        cum = cum_s[...]
        v = ib_ref[...]
        qe = q_s[...] * jnp.exp(cum)
        inter = lax.dot_general(qe, st, _DN["nt"], preferred_element_type=F32)
        a_mat = _hgrn_scores(q_s, k_s, cum_s)
        o_ref[...] = inter + jnp.dot(a_mat, v, preferred_element_type=F32)
        last = cum_s[pl.ds(CHUNK - 1, 1), :]
        kd = k_s[...] * jnp.exp(last - cum)
        state[...] = st * jnp.exp(last) + lax.dot_general(v, kd, _DN["tn"], preferred_element_type=F32)

    def body(qb_ref, fb_ref, ib_ref, lg_ref, o_ref, st_ref, *scratch):
        @pl.when(pl.program_id(1) == 0)
        def _():
            for j in range(hb):
                scratch[4 * j][...] = jnp.zeros((dh, dh), F32)

        for j in range(hb):
            cols = (slice(None), pl.ds(j * dh, dh))
            one_head(qb_ref.at[cols], fb_ref.at[cols], ib_ref.at[cols], lg_ref.at[cols], o_ref.at[cols],
                     st_ref.at[j], *scratch[4 * j:4 * j + 4])

    blk = lambda off: pl.BlockSpec((CHUNK, w), lambda h, i: (i, off // hb + h))
    return pl.pallas_call(
        body, name=name, grid=(HEADS // hb, n),
        in_specs=[blk(cb["q_b"]), blk(cb["f_b"]), blk(cb["i_b"]), pl.BlockSpec((2, w), lambda h, i: (0, h))],
        out_specs=[pl.BlockSpec((CHUNK, w), lambda h, i: (i, h)),
                   pl.BlockSpec((hb, None, dh, dh), lambda h, i: (h, i, 0, 0))],
        out_shape=[jax.ShapeDtypeStruct((t, WIDTH), F32), jax.ShapeDtypeStruct((HEADS, n, dh, dh), F32)],
        scratch_shapes=([pltpu.VMEM((dh, dh), F32)] + [pltpu.VMEM((CHUNK, dh), F32)] * 3) * hb,
        compiler_params=_params(("parallel", "arbitrary")),
    )(proj, proj, proj, lb_logits)


def hgrn_bwd(proj, lb_logits, states, do, name):
    t = proj.shape[0]
    n = t // CHUNK
    cb = _col_blocks()
    dh = HEAD_DIM
    nsub = CHUNK // SUB

    hb = HGRN_HEADS_PER_STEP
    w = hb * dh

    def one_head(qb_ref, fb_ref, ib_ref, lg_ref, st_ref, do_ref, dqb_ref, dfb_ref, dib_ref, dlb_ref,
                 dstate, q_s, k_s, cum_s, da_s, dq_s, dk_s):
        lb, sig, f = _hgrn_inputs(qb_ref, fb_ref, lg_ref, q_s, k_s, cum_s)
        st = st_ref[...]
        dst = dstate[...]
        cum = cum_s[...]
        q = q_s[...]
        kk = k_s[...]
        v = ib_ref[...]
        dov = do_ref[...]
        e_cum = jnp.exp(cum)
        qe = q * e_cum
        last = cum_s[pl.ds(CHUNK - 1, 1), :]
        e_last = jnp.exp(last)
        e_tail = jnp.exp(last - cum)
        kd = kk * e_tail

        a_mat = _hgrn_scores(q_s, k_s, cum_s)
        tri = _tri(CHUNK, True)
        da_s[...] = lax.dot_general(dov, v, _DN["nt"], preferred_element_type=F32) * tri
        dv = (lax.dot_general(a_mat, dov, _DN["tn"], preferred_element_type=F32)
              + lax.dot_general(kd, dst, _DN["nt"], preferred_element_type=F32))
        dk_state = jnp.dot(v, dst, preferred_element_type=F32) * e_tail
        dq_inter = jnp.dot(dov, st, preferred_element_type=F32) * e_cum
        dstate[...] = dst * e_last + lax.dot_general(dov, qe, _DN["tn"], preferred_element_type=F32)

        lane = lax.broadcasted_iota(jnp.int32, (SUB, CHUNK), 1)
        row = lax.broadcasted_iota(jnp.int32, (SUB, 1), 0)
        dk_s[...] = jnp.zeros_like(dk_s)
        for a in range(nsub):
            rows = pl.ds(SUB * a, SUB)
            ca = _boundary(cum_s, a)
            cum_a = cum_s[rows, :]
            q_a = q_s[rows, :]
            ea = jnp.exp(cum_a - ca)
            eb = jnp.exp(jnp.minimum(ca - cum, 0.0))
            da_a = da_s[rows, :]
            da_off = jnp.where(lane < SUB * a, da_a, 0.0)
            dq_a = ea * jnp.dot(da_off, kk * eb, preferred_element_type=F32)
            dk_s[...] += eb * lax.dot_general(da_off, q_a * ea, _DN["tn"], preferred_element_type=F32)
            dk_rows = jnp.zeros((SUB, dh), F32)
            for s in range(SUB):
                r = SUB * a + s
                e = jnp.exp(jnp.minimum(cum_a - cum_s[pl.ds(r, 1), :], 0.0))
                dcol = jnp.sum(jnp.where(lane == r, da_a, 0.0), axis=-1, keepdims=True)
                dcol = jnp.where(row >= s, dcol, 0.0)
                w = dcol * e
                dq_a = dq_a + w * k_s[pl.ds(r, 1), :]
                dk_rows = jnp.where(row == s, jnp.sum(w * q_a, axis=0, keepdims=True), dk_rows)
            dq_s[rows, :] = dq_a
            dk_s[rows, :] += dk_rows

        dq = dq_inter + dq_s[...]
        dk = dk_s[...] + dk_state
        d_last = (jnp.sum(dst * st, axis=0, keepdims=True) * e_last
                  + jnp.sum(kk * dk_state, axis=0, keepdims=True))
        rowc = lax.broadcasted_iota(jnp.int32, (CHUNK, 1), 0)
        dcum = q * dq - kk * dk + jnp.where(rowc == CHUNK - 1, d_last, 0.0)
        dg = _dot_hi(_tri(CHUNK, False), dcum)
        df = dg / f - dk
        dqb_ref[...] = (dq * _silu_grad(qb_ref[...])).astype(BF16)
        dfb_ref[...] = (df * (1.0 - lb) * sig * (1.0 - sig)).astype(BF16)
        dib_ref[...] = dv.astype(BF16)
        dlb_ref[...] += jnp.sum(df * (1.0 - sig), axis=0, keepdims=True)

    def body(qb_ref, fb_ref, ib_ref, lg_ref, st_ref, do_ref, dqb_ref, dfb_ref, dib_ref, dlb_ref, *scratch):
        @pl.when(pl.program_id(1) == 0)
        def _():
            for j in range(hb):
                scratch[7 * j][...] = jnp.zeros((dh, dh), F32)
            dlb_ref[...] = jnp.zeros_like(dlb_ref)

        for j in range(hb):
            cols = (slice(None), pl.ds(j * dh, dh))
            one_head(qb_ref.at[cols], fb_ref.at[cols], ib_ref.at[cols], lg_ref.at[cols], st_ref.at[j], do_ref.at[cols],
                     dqb_ref.at[cols], dfb_ref.at[cols], dib_ref.at[cols], dlb_ref.at[cols],
                     *scratch[7 * j:7 * j + 7])

    blk = lambda off: pl.BlockSpec((CHUNK, w), lambda h, i: (n - 1 - i, off // hb + h))
    out_blk = pl.BlockSpec((CHUNK, w), lambda h, i: (n - 1 - i, h))
    return pl.pallas_call(
        body, name=name, grid=(HEADS // hb, n),
        in_specs=[blk(cb["q_b"]), blk(cb["f_b"]), blk(cb["i_b"]), pl.BlockSpec((2, w), lambda h, i: (0, h)),
                  pl.BlockSpec((hb, None, dh, dh), lambda h, i: (h, n - 1 - i, 0, 0)), out_blk],
        out_specs=[out_blk, out_blk, out_blk, pl.BlockSpec((1, w), lambda h, i: (0, h))],
        out_shape=[jax.ShapeDtypeStruct((t, WIDTH), BF16)] * 3 + [jax.ShapeDtypeStruct((1, WIDTH), F32)],
        scratch_shapes=([pltpu.VMEM((dh, dh), F32)] + [pltpu.VMEM((CHUNK, dh), F32)] * 3
                        + [pltpu.VMEM((CHUNK, CHUNK), F32)] + [pltpu.VMEM((CHUNK, dh), F32)] * 2) * hb,
        compiler_params=_params(("parallel", "arbitrary")),
    )(proj, proj, proj, lb_logits, states, do)


def lb_bwd(dlb, lb_logits, name):
    def body(dlb_ref, lg_ref, o_ref):
        p0 = _lower_bound(lg_ref)
        d0 = dlb_ref[...] * p0 * (1.0 - p0)
        o_ref[0:1, :] = d0
        o_ref[1:2, :] = -d0

    return pl.pallas_call(body, name=name, out_shape=jax.ShapeDtypeStruct(lb_logits.shape, F32))(dlb, lb_logits)


def gnorm_fwd(o_raw, proj, norm_g, name):
    t = o_raw.shape[0]
    tr = _row_tile(t)
    cb = _col_blocks()
    dh = HEAD_DIM

    def body(o_ref, gb_ref, g_ref, y_ref):
        x = o_ref[...]
        y_ref[...] = (x * _rstd(x) * g_ref[...] * _silu(gb_ref[...])).astype(BF16)

    return pl.pallas_call(
        body, name=name, grid=(t // tr, HEADS),
        in_specs=[pl.BlockSpec((tr, dh), lambda i, h: (i, h)), pl.BlockSpec((tr, dh), lambda i, h: (i, cb["g_b"] + h)),
                  pl.BlockSpec((1, dh), lambda i, h: (0, 0))],
        out_specs=pl.BlockSpec((tr, dh), lambda i, h: (i, h)),
        out_shape=jax.ShapeDtypeStruct((t, WIDTH), BF16), compiler_params=_params(("parallel", "parallel")),
    )(o_raw, proj, norm_g)


def gnorm_bwd(dy, o_raw, proj, norm_g, name):
    t = o_raw.shape[0]
    tr = _row_tile(t)
    cb = _col_blocks()
    dh = HEAD_DIM

    def body(dy_ref, o_ref, gb_ref, g_ref, do_ref, dgb_ref, dg_ref):
        @pl.when((pl.program_id(0) == 0) & (pl.program_id(1) == 0))
        def _():
            dg_ref[...] = jnp.zeros_like(dg_ref)

        x = o_ref[...]
        gb = gb_ref[...]
        dyv = dy_ref[...]
        g = g_ref[...]
        dx, dg = _rms_bwd(x, g, dyv * _silu(gb))
        do_ref[...] = dx
        dgb_ref[...] = (dyv * (x * _rstd(x) * g) * _silu_grad(gb)).astype(BF16)
        dg_ref[...] += dg

    hb = pl.BlockSpec((tr, dh), lambda i, h: (i, h))
    vec = pl.BlockSpec((1, dh), lambda i, h: (0, 0))
    return pl.pallas_call(
        body, name=name, grid=(t // tr, HEADS),
        in_specs=[hb, hb, pl.BlockSpec((tr, dh), lambda i, h: (i, cb["g_b"] + h)), vec],
        out_specs=[hb, hb, vec],
        out_shape=[jax.ShapeDtypeStruct((t, WIDTH), F32), jax.ShapeDtypeStruct((t, WIDTH), BF16),
                   jax.ShapeDtypeStruct((1, dh), F32)],
        compiler_params=_params(("arbitrary", "arbitrary")),
    )(dy, o_raw, proj, norm_g)


def merge_fwd(proj, y, name):
    _, t, d = y.shape
    tr = _row_tile(t)
    tc = _pick(d, (1024, 512, 256, 128))
    cb = _col_blocks()
    ga, gb = cb["gate_a"] * LANES // tc, cb["gate_b"] * LANES // tc

    def body(ga_ref, gb_ref, y_ref, o_ref):
        o_ref[...] = (_sigmoid(ga_ref[...]) * y_ref[0] + _sigmoid(gb_ref[...]) * y_ref[1]).astype(BF16)

    return pl.pallas_call(
        body, name=name, grid=(t // tr, d // tc),
        in_specs=[pl.BlockSpec((tr, tc), lambda i, j: (i, ga + j)), pl.BlockSpec((tr, tc), lambda i, j: (i, gb + j)),
                  pl.BlockSpec((2, tr, tc), lambda i, j: (0, i, j))],
        out_specs=pl.BlockSpec((tr, tc), lambda i, j: (i, j)),
        out_shape=jax.ShapeDtypeStruct((t, d), BF16), compiler_params=_params(("parallel", "parallel")),
    )(proj, proj, y)


def merge_bwd(dm, proj, y, name):
    _, t, d = y.shape
    tr = _row_tile(t)
    tc = _pick(d, (1024, 512, 256, 128))
    cb = _col_blocks()
    ga, gb = cb["gate_a"] * LANES // tc, cb["gate_b"] * LANES // tc

    def body(dm_ref, ga_ref, gb_ref, y_ref, dg_ref, dy_ref):
        dmv = dm_ref[...]
        for idx, g_ref in enumerate((ga_ref, gb_ref)):
            s = _sigmoid(g_ref[...])
            dg_ref[idx] = (dmv * y_ref[idx] * s * (1.0 - s)).astype(BF16)
            dy_ref[idx] = (dmv * s).astype(BF16)

    pair = pl.BlockSpec((2, tr, tc), lambda i, j: (0, i, j))
    return pl.pallas_call(
        body, name=name, grid=(t // tr, d // tc),
        in_specs=[pl.BlockSpec((tr, tc), lambda i, j: (i, j)), pl.BlockSpec((tr, tc), lambda i, j: (i, ga + j)),
                  pl.BlockSpec((tr, tc), lambda i, j: (i, gb + j)), pair],
        out_specs=[pair, pair],
        out_shape=[jax.ShapeDtypeStruct((2, t, d), BF16)] * 2, compiler_params=_params(("parallel", "parallel")),
    )(dm, proj, proj, y)


def ple_tail(h, a, b, g, target, name):
    t, d = h.shape
    tr = _row_tile(t)

    def body(h_ref, a_ref, b_ref, g_ref, t_ref, loss_ref, dh_ref, da_ref, db_ref, dg_ref):
        @pl.when(pl.program_id(0) == 0)
        def _():
            loss_ref[...] = jnp.zeros_like(loss_ref)
            dg_ref[...] = jnp.zeros_like(dg_ref)

        s = _sigmoid(a_ref[...])
        bv = b_ref[...]
        z = s * bv
        gv = g_ref[...]
        err = h_ref[...] + z * _rstd(z) * gv - t_ref[...]
        loss_ref[...] += 0.5 * jnp.sum(jnp.sum(err * err, axis=-1, keepdims=True), axis=0, keepdims=True) / d
        dh = err / d
        dh_ref[...] = dh
        dz, dg = _rms_bwd(z, gv, dh)
        da_ref[...] = (dz * bv * s * (1.0 - s)).astype(BF16)
        db_ref[...] = (dz * s).astype(BF16)
        dg_ref[...] += dg

    row = pl.BlockSpec((tr, d), lambda i: (i, 0))
    vec = pl.BlockSpec((1, d), lambda i: (0, 0))
    return pl.pallas_call(
        body, name=name, grid=(t // tr,), in_specs=[row, row, row, vec, row],
        out_specs=[pl.BlockSpec((1, 1), lambda i: (0, 0)), row, row, row, vec],
        out_shape=[jax.ShapeDtypeStruct((1, 1), F32), jax.ShapeDtypeStruct((t, d), F32),
                   jax.ShapeDtypeStruct((t, d), BF16), jax.ShapeDtypeStruct((t, d), BF16),
                   jax.ShapeDtypeStruct((1, d), F32)],
        compiler_params=_params(("arbitrary",)),
    )(h, a, b, g, target)


def _ffn_fwd(h, pre_g, post_g, get_w, idx, tag):
    u = norm_in(h, pre_g, tag + "_norm")
    gu = mm_nn_col(u, get_w("gu" + idx, h), F32, tag + "_gate_up")
    act = swiglu_act(gu, tag + "_act")
    y = mm_nn_2d(act, get_w("down" + idx, gu), F32, tag + "_down")
    out = resid_post(h, y, post_g, MACARON_SCALE, tag + "_out")
    return out, (h, u, gu, act, y)


def _ffn_bwd(dh, saved, pre_g, post_g, get_w, emit, advance, idx, tag):
    h, u, gu, act, y = saved
    dy, d_post = post_bwd(dh, y, post_g, MACARON_SCALE, tag + "_post_bwd")
    m1 = emit("down" + idx, mm_tn_2d(act, dy, F32, tag + "_dw_down"))
    dact = mm_nt_2d(dy, get_w("down" + idx), F32, tag + "_dact", after=[m1])
    m2 = advance(dact)
    dgu = swiglu_bwd(dact, gu, tag + "_act_bwd")
    m3 = emit("gu" + idx, mm_tn_col(u, dgu, N_CHIPS, F32, tag + "_dw_gate_up"))
    du = mm_nt_col(dgu, get_w("gu" + idx), F32, tag + "_du", after=[m2, m3])
    m4 = advance(du)
    dh_in, d_pre = pre_bwd(dh, h, pre_g, [du], tag + "_pre_bwd", after=[m4])
    return dh_in, d_pre, d_post


def _heads_col(a):
    t = a.shape[0]
    at = a[:, :HEADS].T
    return at.reshape(HEADS, t, 1), at.reshape(HEADS, 1, t)


def layer_step(x, p, target, gains, fox_bias, lb_logits, norm_g, get_w, emit, advance):
    t = x.shape[0]
    h1, s1 = _ffn_fwd(x, gains["ffn1_pre"], gains["ffn1_post"], get_w, "1", "ffn1")

    u2 = norm_in(h1, gains["mix_pre"], "mix_norm")
    proj = mm_nt_2d(u2, get_w("in_main", h1), F32, "mix_in")
    fa = mm_nt_2d(u2, get_w("in_fa"), F32, "mix_in_fa")
    c = fox_prep(fa, fox_bias, "fox_prep")
    c_col, c_row = _heads_col(c)
    o_a = fox_fwd(proj, c_col, c_row, "fox_fwd")
    o_raw, states = hgrn_fwd(proj, lb_logits, "hgrn_fwd")
    o_b = gnorm_fwd(o_raw, proj, norm_g, "hgrn_norm")
    o_ab = jnp.stack([o_a, o_b])
    y_ab = _mm_branches(o_ab, get_w("proj", proj), "mix_proj")
    merged = merge_fwd(proj, y_ab, "mix_merge")
    mo = mm_nn_2d(merged, get_w("out"), F32, "mix_out")
    h2 = resid_post(h1, mo, gains["mix_post"], 1.0, "mix_resid")

    h3, s3 = _ffn_fwd(h2, gains["ffn2_pre"], gains["ffn2_post"], get_w, "2", "ffn2")

    u4 = norm_in(h3, gains["ple_pre"], "ple_norm")
    a4 = mm_nn_2d(u4, get_w("ple_gate"), F32, "ple_gate")
    b4 = mm_nn_col(p, get_w("ple_proj"), F32, "ple_proj")[0]
    loss, dh4, da4, db4, d_ple_post = ple_tail(h3, a4, b4, gains["ple_post"], target, "ple_tail")

    marks = [emit("ple_gate", mm_tn_2d(u4, da4, F32, "ple_dw_gate")),
             emit("ple_proj", mm_tn_col(p, db4[None], N_CHIPS, F32, "ple_dw_proj"))]
    du4 = mm_nt_2d(da4, get_w("ple_gate"), F32, "ple_du", after=marks)
    dh3, d_ple_pre = pre_bwd(dh4, h3, gains["ple_pre"], [du4], "ple_pre_bwd", after=[advance(du4)])

    dh2, d_f2_pre, d_f2_post = _ffn_bwd(dh3, s3, gains["ffn2_pre"], gains["ffn2_post"], get_w, emit, advance,
                                        "2", "ffn2")

    dmo, d_mix_post = post_bwd(dh2, mo, gains["mix_post"], 1.0, "mix_post_bwd")
    marks = [emit("out", mm_tn_2d(merged, dmo, F32, "mix_dw_out"))]
    dmerged = mm_nt_2d(dmo, get_w("out"), F32, "mix_dmerged", after=marks)
    dgate, dy_ab = merge_bwd(dmerged, proj, y_ab, "mix_merge_bwd")
    marks = [advance(dmerged), emit("proj", _mm_branches_dw(o_ab, dy_ab, "mix_dw_proj"))]
    do_ab = _mm_branches_bwd(dy_ab, get_w("proj"), "mix_do")
    do_raw, dg_b, d_norm_g = gnorm_bwd(do_ab[1], o_raw, proj, norm_g, "hgrn_norm_bwd")
    dq_b, df_b, di_b, dlb = hgrn_bwd(proj, lb_logits, states, do_raw, "hgrn_bwd")
    d_lb_logits = lb_bwd(dlb, lb_logits, "lb_bwd")
    dq_a, dk_a, dv_a, dc_col, dc_row = fox_bwd(proj, c_col, c_row, do_ab[0], "fox_bwd")
    dc = (dc_col.reshape(HEADS, t) + dc_row.reshape(HEADS, t)).T
    dc = jnp.pad(dc, ((0, 0), (0, LANES - HEADS)))
    dfa, d_fox_bias = fox_post_bwd(dc, fa, fox_bias, "fox_post_bwd")
    dproj = jnp.concatenate([dq_a, dk_a.astype(BF16), dv_a.astype(BF16), dq_b, df_b, di_b, dg_b,
                             dgate[0], dgate[1]], axis=1)
    marks.append(emit("in_main", mm_tn_2d(dproj, u2, F32, "mix_dw_in")))
    marks.append(emit("in_fa", mm_tn_2d(dfa, u2, F32, "mix_dw_in_fa")))
    du2a = mm_nn_2d(dproj, get_w("in_main"), F32, "mix_du", after=marks)
    du2b = mm_nn_2d(dfa, get_w("in_fa"), F32, "mix_du_fa")
    dh1, d_mix_pre = pre_bwd(dh2, h1, gains["mix_pre"], [du2a, du2b], "mix_pre_bwd", after=[advance(du2a)])

    dx, d_f1_pre, d_f1_post = _ffn_bwd(dh1, s1, gains["ffn1_pre"], gains["ffn1_post"], get_w, emit, advance,
                                       "1", "ffn1")

    small = dict(ffn1_pre=d_f1_pre, ffn1_post=d_f1_post, mix_pre=d_mix_pre, mix_post=d_mix_post,
                 ffn2_pre=d_f2_pre, ffn2_post=d_f2_post, ple_pre=d_ple_pre, ple_post=d_ple_post,
                 fox_bias=d_fox_bias, lb_logits=d_lb_logits, norm_g=d_norm_g)
    return loss, dx, small


def _mm_branches(o_ab, w_proj, name):
    g, t, kk = o_ab.shape
    _, jn, _, ns = w_proj.shape
    tm = _pick(t, (512, 256, 128))
    return _mm(o_ab, w_proj, mode="nn", grid=(t // tm, g * jn, 1),
               a_spec=pl.BlockSpec((None, tm, kk), lambda i, j, k: (j // jn, i, 0)),
               b_spec=pl.BlockSpec((None, None, kk, ns), lambda i, j, k: (j // jn, j % jn, 0, 0)),
               o_spec=pl.BlockSpec((None, tm, ns), lambda i, j, k: (j // jn, i, j % jn)),
               out_shape=jax.ShapeDtypeStruct((g, t, jn * ns), F32), acc_shape=(tm, ns), name=name)


def _mm_branches_bwd(dy_ab, w_proj, name):
    g, t, _ = dy_ab.shape
    _, jn, kk, ns = w_proj.shape
    tm = _pick(t, (512, 256, 128))
    return _mm(dy_ab, w_proj, mode="nt", grid=(t // tm, g, jn),
               a_spec=pl.BlockSpec((None, tm, ns), lambda i, j, k: (j, i, k)),
               b_spec=pl.BlockSpec((None, None, kk, ns), lambda i, j, k: (j, k, 0, 0)),
               o_spec=pl.BlockSpec((None, tm, kk), lambda i, j, k: (j, i, 0)),
               out_shape=jax.ShapeDtypeStruct((g, t, kk), F32), acc_shape=(tm, kk), name=name)


def _mm_branches_dw(o_ab, dy_ab, name):
    g, t, kk = o_ab.shape
    d = dy_ab.shape[2]
    jn = N_CHIPS
    ns = d // jn
    return _mm(o_ab, dy_ab, mode="tn", grid=(1, g * jn, 1),
               a_spec=pl.BlockSpec((None, t, kk), lambda i, j, k: (j // jn, 0, 0)),
               b_spec=pl.BlockSpec((None, t, ns), lambda i, j, k: (j // jn, 0, j % jn)),
               o_spec=pl.BlockSpec((None, None, kk, ns), lambda i, j, k: (j // jn, j % jn, 0, 0)),
               out_shape=jax.ShapeDtypeStruct((g, jn, kk, ns), F32), acc_shape=(kk, ns), name=name)


HBM_SPEC = pl.BlockSpec(memory_space=pltpu.HBM)
SEM_SPEC = pl.BlockSpec(memory_space=pltpu.SEMAPHORE)
ANY_SPEC = pl.BlockSpec(memory_space=pl.ANY)
EFFECT = pltpu.SideEffectType.DATAFLOW_SIDE_EFFECTING


def _in_hbm(a):
    return pltpu.with_memory_space_constraint(a, pltpu.HBM)


def _place():
    x, y, c = lax.axis_index("x"), lax.axis_index("y"), lax.axis_index("c")
    chips = [(1 - x, y), (x, 1 - y), (1 - x, 1 - y)]
    return x, y, c, chips


def _half(shape, which, axis):
    n = shape[-2 + axis] // 2
    cut = pl.ds(which * n, n)
    return (cut, slice(None)) if axis == 0 else (slice(None), cut)


def _half_shape(shape, axis):
    s = list(shape)
    s[len(s) - 2 + axis] //= 2
    return tuple(s)


def _remote(src, dst, send_sems, recv_sems, k, to):
    return pltpu.make_async_remote_copy(src_ref=src, dst_ref=dst, send_sem=send_sems.at[k], recv_sem=recv_sems.at[k],
                                        device_id=to, device_id_type=MESH)


def split_start(name, srcs, lands, counts, copies):
    ns, nl, nset = len(srcs), len(lands), len(counts)

    def body(*refs):
        src_refs, land_refs = refs[:ns], refs[ns:ns + nl]
        sems = refs[ns + nl:ns + nl + 2 * nset]
        for s, plan in enumerate(copies(src_refs, land_refs)):
            for k, (src, dst, to) in enumerate(plan):
                _remote(src, dst, sems[2 * s], sems[2 * s + 1], k, to).start()
        refs[-1][...] = jnp.zeros_like(refs[-1])

    out_shape = []
    for n in counts:
        out_shape += [pltpu.SemaphoreType.DMA((n,)), pltpu.SemaphoreType.DMA((n,))]
    out_shape += [pltpu.HBM(a.shape, a.dtype) for a in list(srcs) + list(lands)]
    out_shape.append(jax.ShapeDtypeStruct((8, LANES), F32))
    res = pl.pallas_call(
        body, name=name, out_shape=tuple(out_shape), in_specs=[HBM_SPEC] * (ns + nl),
        out_specs=tuple([SEM_SPEC] * (2 * nset) + [HBM_SPEC] * (ns + nl) + [pl.BlockSpec(memory_space=pltpu.VMEM)]),
        input_output_aliases={i: 2 * nset + i for i in range(ns + nl)},
        compiler_params=pltpu.CompilerParams(has_side_effects=EFFECT),
    )(*[_in_hbm(a) for a in list(srcs) + list(lands)])
    sems = [(res[2 * s], res[2 * s + 1]) for s in range(nset)]
    return sems, list(res[2 * nset:2 * nset + ns]), list(res[2 * nset + ns:-1]), res[-1]


def split_wait(name, srcs, lands, sems, afters, copies):
    afters = [a for a in afters if a is not None]
    ns, nl, na = len(srcs), len(lands), len(afters)

    def body(*refs):
        src_refs, land_refs = refs[:ns], refs[ns:ns + nl]
        send_sems, recv_sems = refs[ns + nl:ns + nl + 2]
        for k, (src, dst, to) in enumerate(copies(src_refs, land_refs)):
            cp = _remote(src, dst, send_sems, recv_sems, k, to)
            cp.wait_send()
            cp.wait_recv()

    res = pl.pallas_call(
        body, name=name, out_shape=tuple(pltpu.HBM(a.shape, a.dtype) for a in list(srcs) + list(lands)),
        in_specs=[HBM_SPEC] * (ns + nl) + [SEM_SPEC, SEM_SPEC] + [ANY_SPEC] * na,
        out_specs=tuple([HBM_SPEC] * (ns + nl)), input_output_aliases={i: i for i in range(ns + nl)},
        compiler_params=pltpu.CompilerParams(has_side_effects=EFFECT),
    )(*srcs, *lands, sems[0], sems[1], *afters)
    return list(res[:ns]), list(res[ns:])


def _gather_plan(blocks):
    def copies(src_refs, land_refs):
        x, y, c, chips = _place()
        j_me = 2 * x + y
        plan = []
        for si, li, g, axis in blocks:
            src, land = src_refs[si], land_refs[li].at[g]
            mine = _half(src.shape, c, axis)
            for px, py in chips:
                plan.append((src.at[mine], land.at[(j_me,) + mine], (px, py, c)))
            plan.append((src, land.at[j_me], (x, y, 1 - c)))
        return plan
    return copies


def _gather_arrivals(blocks):
    def copies(src_refs, land_refs):
        x, y, c, chips = _place()
        j_me = 2 * x + y
        plan = []
        for si, li, g, axis in blocks:
            src, land = src_refs[si], land_refs[li].at[g]
            mine = _half(src.shape, c, axis)
            for px, py in chips:
                plan.append((src.at[mine], land.at[(2 * px + py,) + mine], (px, py, c)))
            plan.append((src, land.at[j_me], (x, y, 1 - c)))
        return plan
    return copies


def gather_pass(name, lands, blocks):
    n = len(lands)

    def body(*refs):
        outs = refs[n:2 * n]
        send_sems, recv_sems = refs[2 * n:]
        x, y, c, chips = _place()
        sent = []
        for i, (li, g, axis) in enumerate(blocks):
            land = outs[li].at[g]
            mine = _half(land.shape[1:], c, axis)
            for k, (px, py) in enumerate(chips):
                part = land.at[(2 * px + py,) + mine]
                cp = _remote(part, part, send_sems, recv_sems, 3 * i + k, (x, y, 1 - c))
                cp.start()
                sent.append(cp)
        for i, (li, g, axis) in enumerate(blocks):
            land = outs[li].at[g]
            other = _half(land.shape[1:], 1 - c, axis)
            for k, (px, py) in enumerate(chips):
                part = land.at[(2 * px + py,) + other]
                _remote(part, part, send_sems, recv_sems, 3 * i + k, (x, y, 1 - c)).wait_recv()
        for cp in sent:
            cp.wait_send()

    m = 3 * len(blocks)
    return pl.pallas_call(
        body, name=name, in_specs=[ANY_SPEC] * n, out_specs=[ANY_SPEC] * n,
        out_shape=[jax.ShapeDtypeStruct(a.shape, a.dtype) for a in lands],
        input_output_aliases={i: i for i in range(n)},
        scratch_shapes=[pltpu.SemaphoreType.DMA((m,)), pltpu.SemaphoreType.DMA((m,))],
    )(*lands)


def _pair_plan(axes):
    def copies(src_refs, land_refs):
        x, y, c, _ = _place()
        return [(src_refs[i].at[(slice(None), slice(None)) + _half(src_refs[i].shape, 1 - c, a)], land_refs[i],
                 (x, y, 1 - c)) for i, a in enumerate(axes)]
    return copies


def _scatter_plan(n):
    def copies(src_refs, land_refs):
        x, y, c, chips = _place()
        return [(src_refs[i].at[:, 2 * px + py], land_refs[i].at[k], (px, py, c))
                for i in range(n) for k, (px, py) in enumerate(chips)]
    return copies


def _broadcast_plan(axes, arrivals):
    def copies(src_refs, land_refs):
        x, y, c, _ = _place()
        plan = []
        for i, a in enumerate(axes):
            part = src_refs[i].at[(slice(None),) + _half(src_refs[i].shape, (1 - c) if arrivals else c, a)]
            plan.append((part, part, (x, y, 1 - c)))
        return plan
    return copies


N_DEV = 8
SLAB_ROWS = 16


def allreduce_small(slab, after):
    def body(x_ref, after_ref, o_ref, land, send_sems, recv_sems):
        x, y, c, _ = _place()
        me = 4 * x + 2 * y + c
        land[me] = x_ref[...]
        copies = []
        for d in range(1, N_DEV):
            to = (me + d) % N_DEV
            cp = pltpu.make_async_remote_copy(
                src_ref=x_ref, dst_ref=land.at[me], send_sem=send_sems.at[d - 1], recv_sem=recv_sems.at[me],
                device_id=(to // 4, (to // 2) % 2, to % 2), device_id_type=MESH)
            cp.start()
            copies.append(cp)
        for d in range(1, N_DEV):
            frm = (me + d) % N_DEV
            pltpu.make_async_remote_copy(
                src_ref=x_ref, dst_ref=land.at[frm], send_sem=send_sems.at[d - 1], recv_sem=recv_sems.at[frm],
                device_id=(frm // 4, (frm // 2) % 2, frm % 2), device_id_type=MESH).wait_recv()
        for cp in copies:
            cp.wait_send()
        acc = land[0]
        for s in range(1, N_DEV):
            acc = acc + land[s]
        o_ref[...] = acc

    vm = pl.BlockSpec(memory_space=pltpu.VMEM)
    return pl.pallas_call(
        body, name="allreduce_small", in_specs=[vm, ANY_SPEC], out_specs=vm,
        out_shape=jax.ShapeDtypeStruct(slab.shape, F32),
        scratch_shapes=[pltpu.VMEM((N_DEV,) + slab.shape, F32), pltpu.SemaphoreType.DMA((N_DEV - 1,)),
                        pltpu.SemaphoreType.DMA((N_DEV,))],
    )(slab, after)


BLOCK_BYTES = 3 * 1024 * 1024


def _tiles_2d(r, c, budget=BLOCK_BYTES):
    if r % 8 == 0:
        tc = c if c % LANES else _pick(c, (2048, 1408, 1024, 512, 256, 128))
        tr = 8
        for cand in (512, 256, 128, 64, 32, 16, 8):
            if r % cand == 0 and cand * tc * 4 <= budget:
                tr = cand
                break
        if tr >= 64 or c % LANES or r * LANES * 4 > budget:
            return tr, tc
    tc = LANES
    for cand in (1024, 512, 256, 128):
        if c % cand == 0 and r * cand * 4 <= budget:
            tc = cand
            break
    return r, tc


def _grid_spec(grid, in_specs, out_specs):
    return pltpu.PrefetchScalarGridSpec(num_scalar_prefetch=1, grid=grid, in_specs=in_specs, out_specs=out_specs)


def _own(axis, nr, nc):
    if axis == 0:
        return lambda i, j, where: (where[1] * nr + i, j)
    return lambda i, j, where: (i, where[1] * nc + j)


def pair_add(where, grad, recv, axis, name):
    g, jn, hr, hc = recv.shape
    tr, tc = _tiles_2d(hr, hc)
    nr, nc = hr // tr, hc // tc
    own = _own(axis, nr, nc)
    others = jn - 1

    def body(where_ref, a_ref, b_ref, o_ref):
        o_ref[...] = (a_ref[...] + b_ref[...]).astype(BF16)

    def block(a, where):
        return a // others, (where[0] + 1 + a % others) % jn

    blk = pl.BlockSpec((None, None, tr, tc), lambda a, i, j, where: block(a, where) + (i, j))
    mine = pl.BlockSpec((None, None, tr, tc), lambda a, i, j, where: block(a, where) + own(i, j, where))
    return pl.pallas_call(
        body, name=name, grid_spec=_grid_spec((g * others, nr, nc), [mine, blk], blk),
        out_shape=jax.ShapeDtypeStruct(recv.shape, BF16),
        compiler_params=_params(("parallel", "parallel", "parallel")),
    )(where, grad, recv)


def chip_add(where, grad, pair, recv, axis, name):
    g, jn, hr, hc = pair.shape
    tr, tc = _tiles_2d(hr, hc)
    nr, nc = hr // tr, hc // tc
    own = _own(axis, nr, nc)
    full = (g, 2 * hr, hc) if axis == 0 else (g, hr, 2 * hc)

    def body(where_ref, a_ref, p_ref, b_ref, o_ref):
        s = a_ref[...] + p_ref[...]
        for k in range(3):
            s = s + b_ref[k].astype(F32)
        o_ref[...] = s

    return pl.pallas_call(
        body, name=name,
        grid_spec=_grid_spec((g, nr, nc),
                             [pl.BlockSpec((None, None, tr, tc), lambda a, i, j, where: (a, where[0]) + own(i, j, where)),
                              pl.BlockSpec((None, None, tr, tc), lambda a, i, j, where: (a, where[0], i, j)),
                              pl.BlockSpec((3, None, tr, tc), lambda a, i, j, where: (0, a, i, j))],
                             pl.BlockSpec((None, tr, tc), lambda a, i, j, where: (a,) + own(i, j, where))),
        out_shape=jax.ShapeDtypeStruct(full, F32), compiler_params=_params(("parallel", "parallel", "parallel")),
    )(where, grad, pair, recv)


def _adam_math(w, g, m, v):
    m2 = ADAM_B1 * m + (1.0 - ADAM_B1) * g
    v2 = ADAM_B2 * v + (1.0 - ADAM_B2) * (g * g)
    m_hat = m2 / (1.0 - ADAM_B1 ** ADAM_STEP)
    v_hat = v2 / (1.0 - ADAM_B2 ** ADAM_STEP)
    delta = -ADAM_LR * (m_hat / (jnp.sqrt(v_hat) + ADAM_EPS) + ADAM_WD * w)
    return delta, m2, v2


def adamw(grad, idx, w, m, v, name):
    _, r, cc = w.shape
    rg = grad.shape[1]
    tr, tc = _tiles_2d(r, cc, BLOCK_BYTES // 2)
    assert rg == r or tr == r
    gr = tr if rg == r else rg

    def body(g_ref, w_ref, m_ref, v_ref, go_ref, d_ref, mo_ref, vo_ref):
        g = g_ref[pl.ds(0, tr), :]
        delta, m2, v2 = _adam_math(w_ref[...], g, m_ref[...], v_ref[...])
        go_ref[...] = g
        d_ref[...] = delta
        mo_ref[...] = m2
        vo_ref[...] = v2

    blk = pl.BlockSpec((None, tr, tc), lambda i, j: (0, i, j))
    return pl.pallas_call(
        body, name=name, grid=(r // tr, cc // tc),
        in_specs=[pl.BlockSpec((None, gr, tc), lambda i, j: (idx, i, j)), blk, blk, blk], out_specs=[blk] * 4,
        out_shape=[jax.ShapeDtypeStruct(w.shape, F32)] * 4, compiler_params=_params(("parallel", "parallel")),
    )(grad, w, m, v)


def adamw_small(g, w, m, v):
    def body(g_ref, w_ref, m_ref, v_ref, d_ref, mo_ref, vo_ref):
        delta, m2, v2 = _adam_math(w_ref[...], g_ref[...], m_ref[...], v_ref[...])
        d_ref[...] = delta
        mo_ref[...] = m2
        vo_ref[...] = v2

    return pl.pallas_call(body, name="adamw_small", out_shape=[jax.ShapeDtypeStruct(w.shape, F32)] * 3)(g, w, m, v)


GAINS = ("ffn1_pre", "ffn1_post", "mix_pre", "mix_post", "ffn2_pre", "ffn2_post", "ple_pre", "ple_post")
WEIGHTS = ("ffn1_pre_g", "ffn1_post_g", "ffn1_w_gate", "ffn1_w_up", "ffn1_w_down", "mix_pre_g", "mix_post_g",
           "mix_w_in", "fox_f_bias", "hgrn_lb_logits", "hgrn_norm_g", "mix_w_proj_fox", "mix_w_proj_hgrn",
           "mix_w_out", "ffn2_pre_g", "ffn2_post_g", "ffn2_w_gate", "ffn2_w_up", "ffn2_w_down", "ple_pre_g",
           "ple_post_g", "ple_w_gate", "ple_w_proj")
GROUPS = dict(gu1=(("ffn1_w_gate", "ffn1_w_up"), 0), down1=(("ffn1_w_down",), 0), win=(("mix_w_in",), 1),
              proj=(("mix_w_proj_fox", "mix_w_proj_hgrn"), 0), out=(("mix_w_out",), 0),
              gu2=(("ffn2_w_gate", "ffn2_w_up"), 0), down2=(("ffn2_w_down",), 0), ple_gate=(("ple_w_gate",), 0),
              ple_proj=(("ple_w_proj",), 0))
TRANSPOSED = ("mix_w_in",)
ROW_BLOCKS = ("down1", "down2", "out", "ple_gate")
GATHER_SETS = (("gu1",), ("down1",), ("win",), ("proj", "out"), ("gu2", "down2", "ple_gate", "ple_proj"))
REDUCE_SETS = (("ple_gate", "ple_proj", "down2", "gu2"), ("out", "proj", "win"), ("down1",), ("gu1",))


def _pad_row(a, width):
    a = a.reshape(1, -1)
    return jnp.pad(a, ((0, 0), (0, width - a.shape[1])))


def _pack_small(vals):
    d = D_MODEL
    rows = [vals[n + "_g"].reshape(1, d) for n in GAINS]
    rows.append(_pad_row(vals["fox_f_bias"], d))
    lg = vals["hgrn_lb_logits"]
    rows += [_pad_row(lg[0], d), _pad_row(lg[1], d), _pad_row(vals["hgrn_norm_g"], d)]
    slab = jnp.concatenate(rows, axis=0)
    return jnp.pad(slab, ((0, SLAB_ROWS - slab.shape[0]), (0, 0)))


def _unpack_small(slab):
    out = {n + "_g": slab[i:i + 1] for i, n in enumerate(GAINS)}
    out["fox_f_bias"] = slab[8:9, :HEADS]
    out["hgrn_lb_logits"] = slab[9:11, :WIDTH]
    out["hgrn_norm_g"] = slab[11:12, :HEAD_DIM]
    return out


def _split_in(win_t):
    lo = 3 * WIDTH
    main = jnp.concatenate([win_t[:lo], win_t[lo + HEADS:]], axis=0)
    fa = jnp.pad(win_t[lo:lo + HEADS], ((0, LANES - HEADS), (0, 0)))
    return main, fa


def _join_in(main, fa):
    lo = 3 * WIDTH
    return jnp.concatenate([main[:lo], fa[:HEADS], main[lo:]], axis=0)


def _as_block(name, a):
    return jnp.swapaxes(a, 1, 2) if name in TRANSPOSED else a


def _send_block(name, a):
    return _as_block(name, a)[0].astype(BF16)


def kernel(x, p, ffn1_pre_g, ffn1_post_g, ffn1_w_gate, ffn1_w_up, ffn1_w_down, mix_pre_g, mix_post_g, mix_w_in, fox_f_bias, hgrn_lb_logits, hgrn_norm_g, mix_w_proj_fox, mix_w_proj_hgrn, mix_w_out, ffn2_pre_g, ffn2_post_g, ffn2_w_gate, ffn2_w_up, ffn2_w_down, ple_pre_g, ple_post_g, ple_w_gate, ple_w_proj, loss_target, m_ffn1_pre_g, m_ffn1_post_g, m_ffn1_w_gate, m_ffn1_w_up, m_ffn1_w_down, m_mix_pre_g, m_mix_post_g, m_mix_w_in, m_fox_f_bias, m_hgrn_lb_logits, m_hgrn_norm_g, m_mix_w_proj_fox, m_mix_w_proj_hgrn, m_mix_w_out, m_ffn2_pre_g, m_ffn2_post_g, m_ffn2_w_gate, m_ffn2_w_up, m_ffn2_w_down, m_ple_pre_g, m_ple_post_g, m_ple_w_gate, m_ple_w_proj, v_ffn1_pre_g, v_ffn1_post_g, v_ffn1_w_gate, v_ffn1_w_up, v_ffn1_w_down, v_mix_pre_g, v_mix_post_g, v_mix_w_in, v_fox_f_bias, v_hgrn_lb_logits, v_hgrn_norm_g, v_mix_w_proj_fox, v_mix_w_proj_hgrn, v_mix_w_out, v_ffn2_pre_g, v_ffn2_post_g, v_ffn2_w_gate, v_ffn2_w_up, v_ffn2_w_down, v_ple_pre_g, v_ple_post_g, v_ple_w_gate, v_ple_w_proj):
    args = dict(locals())
    wts = {n: args[n] for n in WEIGHTS}
    mom = {n: args["m_" + n] for n in WEIGHTS}
    var = {n: args["v_" + n] for n in WEIGHTS}
    d = D_MODEL
    where = jnp.stack([2 * lax.axis_index("x") + lax.axis_index("y"), lax.axis_index("c")]).astype(jnp.int32)

    order = [g for s in GATHER_SETS for g in s]
    srcs, lands, plans = [], [], []
    for s in GATHER_SETS:
        blocks = []
        for g in s:
            names, axis = GROUPS[g]
            for pos, n in enumerate(names):
                blocks.append((len(srcs), order.index(g), pos, axis))
                srcs.append(_send_block(n, wts[n]))
            lands.append(lax.empty((len(names), N_CHIPS) + srcs[-1].shape, BF16))
        plans.append(blocks)
    sems, srcs, lands, _ = split_start(
        "gather_start", srcs, lands, [4 * len(b) for b in plans],
        lambda sr, lr: [_gather_plan(b)(sr, lr) for b in plans])
    full = {}

    def land_set(si, after):
        blocks = plans[si]
        s_idx = sorted({b[0] for b in blocks})
        l_idx = sorted({b[1] for b in blocks})
        local = [(s_idx.index(a), l_idx.index(b), g, ax) for a, b, g, ax in blocks]
        _, got = split_wait("gather_wait_%d" % si, [srcs[i] for i in s_idx], [lands[i] for i in l_idx], sems[si],
                            [after], _gather_arrivals(local))
        got = gather_pass("gather_pass_%d" % si, got, [(b, g, ax) for _, b, g, ax in local])
        for i, arr in zip(l_idx, got):
            g = order[i]
            if g == "win":
                full["win"] = arr
                full["in_main"], full["in_fa"] = _split_in(arr.reshape(-1, d))
            else:
                full[g] = arr.reshape(-1, d) if g in ROW_BLOCKS else arr

    def get_w(key, after=None):
        g = "win" if key in ("in_main", "in_fa") else key
        if g not in full:
            land_set([g in s for s in GATHER_SETS].index(True), after)
        return full[key]

    grads, pairing, started = {}, [], {}
    rows4 = lambda a: a.reshape(1, N_CHIPS, a.shape[0] // N_CHIPS, a.shape[1])

    def emit(key, grad):
        if key in ("in_main", "in_fa"):
            grads[key] = grad
            if "in_main" not in grads or "in_fa" not in grads:
                return None
            key, grad = "win", rows4(_join_in(grads["in_main"], grads["in_fa"]))
        grads[key] = grad if grad.ndim == 4 else rows4(grad)
        for si, s in enumerate(REDUCE_SETS):
            if key in s and all(g in grads for g in s):
                axes = [GROUPS[g][1] for g in s]
                own = [grads[g] for g in s]
                zones = [lax.empty(_half_shape(a.shape, ax), F32) for a, ax in zip(own, axes)]
                plan = _pair_plan(axes)
                sem, own, zones, mark = split_start("pair_start_%d" % si, own, zones, [len(s)],
                                                    lambda sr, lr: [plan(sr, lr)])
                pairing.append((si, sem[0], own, zones, axes, plan))
                return mark
        return None

    def advance(value):
        mark = None
        while pairing:
            si, sem, own, zones, axes, plan = pairing.pop(0)
            s = REDUCE_SETS[si]
            own, recv = split_wait("pair_wait_%d" % si, own, zones, sem, [value], plan)
            parts = [pair_add(where, a, r, ax, "pair_add_" + g) for g, a, r, ax in zip(s, own, recv, axes)]
            zones = [lax.empty((3, q.shape[0]) + q.shape[2:], BF16) for q in parts]
            plan = _scatter_plan(len(s))
            sem, parts, zones, mark = split_start("scatter_start_%d" % si, parts, zones, [3 * len(s)],
                                                  lambda sr, lr: [plan(sr, lr)])
            started[si] = (sem[0], parts, zones, own, recv, axes, plan)
        return mark

    gains = {n: wts[n + "_g"] for n in GAINS}
    loss, dx, small = layer_step(x[0], p[0, 0].astype(BF16), loss_target[0], gains, _pad_row(fox_f_bias, LANES),
                                 hgrn_lb_logits, hgrn_norm_g, get_w, emit, advance)

    out_g, out_d, out_m, out_v = {}, {}, {}, {}
    after, crossing = None, None

    def finish(si, sem, halves, axes, mark):
        reduced, _ = split_wait("broadcast_wait_%d" % si, halves, [], sem, [mark], _broadcast_plan(axes, True))
        last = None
        for g, red in zip(REDUCE_SETS[si], reduced):
            for idx, n in enumerate(GROUPS[g][0]):
                res = adamw(red, idx, _as_block(n, wts[n]), _as_block(n, mom[n]), _as_block(n, var[n]), "adamw_" + n)
                out_g[n], out_d[n], out_m[n], out_v[n] = [_as_block(n, r) for r in res]
                last = res[1]
        return last

    for si, s in enumerate(REDUCE_SETS):
        sem, parts, zones, own, recv, axes, plan = started[si]
        _, zones = split_wait("scatter_wait_%d" % si, parts, zones, sem, [dx, after], plan)
        halves = [chip_add(where, a, r, z, ax, "chip_add_" + g) for g, a, r, z, ax in zip(s, own, recv, zones, axes)]
        plan = _broadcast_plan(axes, False)
        sem, halves, _, mark = split_start("broadcast_start_%d" % si, halves, [], [len(s)],
                                           lambda sr, lr: [plan(sr, lr)])
        if crossing is not None:
            after = finish(*crossing, mark)
        crossing = (si, sem[0], halves, axes)
    after = finish(*crossing, None)

    small_named = {n + "_g": small[n] for n in GAINS}
    small_named.update(fox_f_bias=small["fox_bias"][:, :HEADS], hgrn_lb_logits=small["lb_logits"],
                       hgrn_norm_g=small["norm_g"])
    g_small = allreduce_small(_pack_small(small_named), after)
    d_small, m_small, v_small = adamw_small(g_small, _pack_small(wts), _pack_small(mom), _pack_small(var))

    for dst, slab in ((out_g, g_small), (out_d, d_small), (out_m, m_small), (out_v, v_small)):
        dst.update(_unpack_small(slab))

    total = lax.psum(loss[0, 0], ("x", "y", "c"))
    return (total, dx[None], *[out_g[n] for n in WEIGHTS], *[out_d[n] for n in WEIGHTS],
            *[out_m[n] for n in WEIGHTS], *[out_v[n] for n in WEIGHTS])
```

```python
import functools

import jax
import jax.numpy as jnp
from jax import lax
from jax.experimental import pallas as pl
from jax.experimental.pallas import tpu as pltpu

F32 = jnp.float32
BF16 = jnp.bfloat16

D_MODEL = 2048
SEQ = 2048
D_FF = 5632
PLE_DIM = 256
HEADS = 8
HEAD_DIM = 128
WIDTH = HEADS * HEAD_DIM
CHUNK = 64
SUB = 16
HGRN_HEADS_PER_STEP = 2
NORM_EPS = 1e-6
MACARON_SCALE = 0.5
N_CHIPS = 4

ADAM_LR = 0.001
ADAM_B1 = 0.9
ADAM_B2 = 0.999
ADAM_EPS = 1e-08
ADAM_WD = 0.01
ADAM_STEP = 10

LANES = 128
VMEM_LIMIT = 56 * 1024 * 1024
NEG_BIG = -1e30
MESH = pl.DeviceIdType.MESH


def _pick(n, cands):
    for c in cands:
        if c <= n and n % c == 0:
            return c
    return n


def _params(sem, vmem=VMEM_LIMIT):
    return pltpu.CompilerParams(dimension_semantics=sem, vmem_limit_bytes=vmem)


def _sigmoid(x):
    return 1.0 / (1.0 + jnp.exp(-x))


def _silu(x):
    return x * _sigmoid(x)


def _silu_grad(x):
    s = _sigmoid(x)
    return s * (1.0 + x * (1.0 - s))


_DN = {"nn": (((1,), (0,)), ((), ())), "nt": (((1,), (1,)), ((), ())), "tn": (((0,), (0,)), ((), ()))}


def _mm(a, b, *, mode, grid, a_spec, b_spec, o_spec, out_shape, acc_shape, name, after=(), init=None, into=None):
    nk = grid[2]
    dn = _DN[mode]
    after = [m for m in after if m is not None]
    extra = ([init] if init is not None else []) + ([into] if into is not None else []) + after
    n_extra = len(extra)

    def body(a_ref, b_ref, *rest):
        o_ref, acc_ref = rest[n_extra:]
        k = pl.program_id(2)

        @pl.when(k == 0)
        def _():
            acc_ref[...] = jnp.zeros_like(acc_ref) if init is None else rest[0][...].astype(F32)

        acc_ref[...] += lax.dot_general(a_ref[...].astype(BF16), b_ref[...].astype(BF16), dn,
                                        preferred_element_type=F32)

        @pl.when(k == nk - 1)
        def _():
            o_ref[...] = acc_ref[...].astype(o_ref.dtype)

    anywhere = pl.BlockSpec(memory_space=pl.ANY)
    return pl.pallas_call(
        body, name=name, grid=grid,
        in_specs=[a_spec, b_spec] + ([o_spec] if init is not None else []) + [anywhere] * (n_extra - (init is not None)),
        out_specs=o_spec, out_shape=out_shape, scratch_shapes=[pltpu.VMEM(acc_shape, F32)],
        input_output_aliases={} if into is None else {2 + (init is not None): 0},
        compiler_params=_params(("parallel", "parallel", "arbitrary")),
    )(a, b, *extra)


def mm_nn_2d(a, b, out_dtype, name, after=()):
    m, kk = a.shape
    n = b.shape[1]
    tm, tn = _pick(m, (512, 256, 128)), _pick(n, (1024, 512, 256, 128))
    tk = _pick(kk, (5632, 2816, 2048, 1408, 1024, 512, 256, 128))
    return _mm(a, b, mode="nn", grid=(m // tm, n // tn, kk // tk),
               a_spec=pl.BlockSpec((tm, tk), lambda i, j, k: (i, k)),
               b_spec=pl.BlockSpec((tk, tn), lambda i, j, k: (k, j)),
               o_spec=pl.BlockSpec((tm, tn), lambda i, j, k: (i, j)),
               out_shape=jax.ShapeDtypeStruct((m, n), out_dtype), acc_shape=(tm, tn), name=name, after=after)


def mm_nt_2d(a, b, out_dtype, name, after=()):
    m, c = a.shape
    n = b.shape[0]
    tm, tn, tk = _pick(m, (512, 256, 128)), _pick(n, (1408, 1024, 512, 256, 128)), _pick(c, (2048, 1408, 1024, 512, 256, 128))
    return _mm(a, b, mode="nt", grid=(m // tm, n // tn, c // tk),
               a_spec=pl.BlockSpec((tm, tk), lambda i, j, k: (i, k)),
               b_spec=pl.BlockSpec((tn, tk), lambda i, j, k: (j, k)),
               o_spec=pl.BlockSpec((tm, tn), lambda i, j, k: (i, j)),
               out_shape=jax.ShapeDtypeStruct((m, n), out_dtype), acc_shape=(tm, tn), name=name, after=after)


def mm_tn_2d(a, b, out_dtype, name):
    c, m = a.shape
    n = b.shape[1]
    tm, tn, tk = _pick(m, (1408, 1024, 512, 256, 128)), _pick(n, (1024, 512, 256, 128)), _pick(c, (2048, 1024, 512, 256, 128))
    return _mm(a, b, mode="tn", grid=(m // tm, n // tn, c // tk),
               a_spec=pl.BlockSpec((tk, tm), lambda i, j, k: (k, i)),
               b_spec=pl.BlockSpec((tk, tn), lambda i, j, k: (k, j)),
               o_spec=pl.BlockSpec((tm, tn), lambda i, j, k: (i, j)),
               out_shape=jax.ShapeDtypeStruct((m, n), out_dtype), acc_shape=(tm, tn), name=name)


def mm_nn_col(a, w, out_dtype, name, slot0=0, total=None, into=None):
    m, kk = a.shape
    g, jn, _, ns = w.shape
    total = g if total is None else total
    tm, tk = _pick(m, (512, 256, 128)), _pick(kk, (2048, 1024, 512, 256, 128))
    return _mm(a, w, mode="nn", grid=(m // tm, g * jn, kk // tk),
               a_spec=pl.BlockSpec((tm, tk), lambda i, j, k: (i, k)),
               b_spec=pl.BlockSpec((None, None, tk, ns), lambda i, j, k: (j // jn, j % jn, k, 0)),
               o_spec=pl.BlockSpec((None, tm, ns), lambda i, j, k: (slot0 + j // jn, i, j % jn)),
               out_shape=jax.ShapeDtypeStruct((total, m, jn * ns), out_dtype), acc_shape=(tm, ns), name=name,
               into=into)


def mm_nt_col(a, w, out_dtype, name, after=(), slot0=0, init=None):
    _, m, _ = a.shape
    g, jn, kk, ns = w.shape
    tm, tn = _pick(m, (512, 256, 128)), _pick(kk, (1024, 512, 256, 128))
    return _mm(a, w, mode="nt", grid=(m // tm, kk // tn, g * jn),
               a_spec=pl.BlockSpec((None, tm, ns), lambda i, j, k: (slot0 + k // jn, i, k % jn)),
               b_spec=pl.BlockSpec((None, None, tn, ns), lambda i, j, k: (k // jn, k % jn, j, 0)),
               o_spec=pl.BlockSpec((tm, tn), lambda i, j, k: (i, j)),
               out_shape=jax.ShapeDtypeStruct((m, kk), out_dtype), acc_shape=(tm, tn), name=name, after=after,
               init=init)


def mm_tn_col(a, b, jn, out_dtype, name):
    c, kk = a.shape
    g, _, n = b.shape
    ns = n // jn
    tm, tk = _pick(kk, (512, 256, 128)), _pick(c, (2048, 1024, 512, 256, 128))
    return _mm(a, b, mode="tn", grid=(kk // tm, g * jn, c // tk),
               a_spec=pl.BlockSpec((tk, tm), lambda i, j, k: (k, i)),
               b_spec=pl.BlockSpec((None, tk, ns), lambda i, j, k: (j // jn, k, j % jn)),
               o_spec=pl.BlockSpec((None, None, tm, ns), lambda i, j, k: (j // jn, j % jn, i, 0)),
               out_shape=jax.ShapeDtypeStruct((g, jn, kk, ns), out_dtype), acc_shape=(tm, ns), name=name)


def _rstd(x):
    return lax.rsqrt(jnp.mean(x * x, axis=-1, keepdims=True) + NORM_EPS)


def _rms_bwd(x, g, dy):
    r = _rstd(x)
    xn = x * r
    dyg = dy * g
    dx = r * (dyg - xn * jnp.mean(dyg * xn, axis=-1, keepdims=True))
    return dx, jnp.sum(dy * xn, axis=0, keepdims=True)


def _row_tile(t):
    return _pick(t, (256, 128, 64, 32, 16, 8))


def norm_in(h, g, name):
    t, d = h.shape
    tr = _row_tile(t)

    def body(h_ref, g_ref, u_ref):
        x = h_ref[...]
        u_ref[...] = (x * _rstd(x) * g_ref[...]).astype(BF16)

    return pl.pallas_call(
        body, name=name, grid=(t // tr,),
        in_specs=[pl.BlockSpec((tr, d), lambda i: (i, 0)), pl.BlockSpec((1, d), lambda i: (0, 0))],
        out_specs=pl.BlockSpec((tr, d), lambda i: (i, 0)),
        out_shape=jax.ShapeDtypeStruct((t, d), BF16), compiler_params=_params(("parallel",)),
    )(h, g)


def resid_post(h, y, g, scale, name):
    t, d = h.shape
    tr = _row_tile(t)

    def body(h_ref, y_ref, g_ref, o_ref):
        yv = y_ref[...]
        o_ref[...] = h_ref[...] + scale * (yv * _rstd(yv) * g_ref[...])

    row = pl.BlockSpec((tr, d), lambda i: (i, 0))
    return pl.pallas_call(
        body, name=name, grid=(t // tr,), in_specs=[row, row, pl.BlockSpec((1, d), lambda i: (0, 0))],
        out_specs=row, out_shape=jax.ShapeDtypeStruct((t, d), F32), compiler_params=_params(("parallel",)),
    )(h, y, g)


def post_bwd(dh, y, g, scale, name):
    t, d = dh.shape
    tr = _row_tile(t)

    def body(dh_ref, y_ref, g_ref, dy_ref, dg_ref):
        @pl.when(pl.program_id(0) == 0)
        def _():
            dg_ref[...] = jnp.zeros_like(dg_ref)

        dx, dg = _rms_bwd(y_ref[...], g_ref[...], scale * dh_ref[...])
        dy_ref[...] = dx.astype(BF16)
        dg_ref[...] += dg

    row = pl.BlockSpec((tr, d), lambda i: (i, 0))
    vec = pl.BlockSpec((1, d), lambda i: (0, 0))
    return pl.pallas_call(
        body, name=name, grid=(t // tr,), in_specs=[row, row, vec], out_specs=[row, vec],
        out_shape=[jax.ShapeDtypeStruct((t, d), BF16), jax.ShapeDtypeStruct((1, d), F32)],
        compiler_params=_params(("arbitrary",)),
    )(dh, y, g)


def pre_bwd(dh, h, g, dus, name, after=()):
    t, d = dh.shape
    tr = _row_tile(t)
    n_du = len(dus)
    after = [m for m in after if m is not None]

    def body(*refs):
        dh_ref, h_ref, g_ref = refs[:3]
        du_refs = refs[3:3 + n_du]
        o_ref, dg_ref = refs[3 + n_du + len(after):]

        @pl.when(pl.program_id(0) == 0)
        def _():
            dg_ref[...] = jnp.zeros_like(dg_ref)

        du = du_refs[0][...]
        for r in du_refs[1:]:
            du = du + r[...]
        dx, dg = _rms_bwd(h_ref[...], g_ref[...], du)
        o_ref[...] = dh_ref[...] + dx
        dg_ref[...] += dg

    row = pl.BlockSpec((tr, d), lambda i: (i, 0))
    vec = pl.BlockSpec((1, d), lambda i: (0, 0))
    return pl.pallas_call(
        body, name=name, grid=(t // tr,),
        in_specs=[row, row, vec] + [row] * n_du + [pl.BlockSpec(memory_space=pl.ANY)] * len(after),
        out_specs=[row, vec], out_shape=[jax.ShapeDtypeStruct((t, d), F32), jax.ShapeDtypeStruct((1, d), F32)],
        compiler_params=_params(("arbitrary",)),
    )(dh, h, g, *dus, *after)


def _ew_tiles(t, f):
    return _pick(t, (256, 128, 64, 32, 16, 8)), _pick(f, (1408, 1024, 512, 256, 128))


def swiglu_act(gu, name):
    _, t, f = gu.shape
    tr, tc = _ew_tiles(t, f)

    def body(gu_ref, o_ref):
        o_ref[...] = (_silu(gu_ref[0]) * gu_ref[1]).astype(BF16)

    return pl.pallas_call(
        body, name=name, grid=(t // tr, f // tc),
        in_specs=[pl.BlockSpec((2, tr, tc), lambda i, j: (0, i, j))],
        out_specs=pl.BlockSpec((tr, tc), lambda i, j: (i, j)),
        out_shape=jax.ShapeDtypeStruct((t, f), BF16), compiler_params=_params(("parallel", "parallel")),
    )(gu)


def swiglu_bwd(dact, gu, name):
    _, t, f = gu.shape
    tr, tc = _ew_tiles(t, f)

    def body(da_ref, gu_ref, o_ref):
        da = da_ref[...]
        gate = gu_ref[0]
        o_ref[0] = (da * gu_ref[1] * _silu_grad(gate)).astype(BF16)
        o_ref[1] = (da * _silu(gate)).astype(BF16)

    return pl.pallas_call(
        body, name=name, grid=(t // tr, f // tc),
        in_specs=[pl.BlockSpec((tr, tc), lambda i, j: (i, j)), pl.BlockSpec((2, tr, tc), lambda i, j: (0, i, j))],
        out_specs=pl.BlockSpec((2, tr, tc), lambda i, j: (0, i, j)),
        out_shape=jax.ShapeDtypeStruct((2, t, f), BF16), compiler_params=_params(("parallel", "parallel")),
    )(dact, gu)


def _col_blocks():
    w = WIDTH // LANES
    return dict(q_a=0, k_a=w, v_a=2 * w, q_b=3 * w, f_b=4 * w, i_b=5 * w, g_b=6 * w, gate_a=7 * w,
                gate_b=7 * w + D_MODEL // LANES)


def _tri(n, lower):
    r = lax.broadcasted_iota(jnp.int32, (n, n), 0)
    c = lax.broadcasted_iota(jnp.int32, (n, n), 1)
    return jnp.where((r >= c) if lower else (r <= c), 1.0, 0.0).astype(F32)


def _dot_hi(a, b):
    return jnp.dot(a, b, precision=lax.Precision.HIGHEST, preferred_element_type=F32)


def fox_prep(fa, bias, name):
    t, w = fa.shape
    tb = _pick(t, (256, 128, 64))

    def body(fa_ref, b_ref, c_ref, carry_ref):
        @pl.when(pl.program_id(0) == 0)
        def _():
            carry_ref[...] = jnp.zeros_like(carry_ref)

        z = fa_ref[...] + b_ref[...]
        lf = jnp.minimum(z, 0.0) - jnp.log(1.0 + jnp.exp(-jnp.abs(z)))
        c = _dot_hi(_tri(tb, True), lf) + carry_ref[...]
        c_ref[...] = c
        carry_ref[...] = carry_ref[...] + jnp.sum(lf, axis=0, keepdims=True)

    return pl.pallas_call(
        body, name=name, grid=(t // tb,),
        in_specs=[pl.BlockSpec((tb, w), lambda i: (i, 0)), pl.BlockSpec((1, w), lambda i: (0, 0))],
        out_specs=pl.BlockSpec((tb, w), lambda i: (i, 0)),
        out_shape=jax.ShapeDtypeStruct((t, w), F32), scratch_shapes=[pltpu.VMEM((1, w), F32)],
        compiler_params=_params(("arbitrary",)),
    )(fa, bias)


def fox_post_bwd(dc, fa, bias, name):
    t, w = fa.shape
    tb = _pick(t, (256, 128, 64))
    nb = t // tb

    def body(dc_ref, fa_ref, b_ref, dfa_ref, db_ref, carry_ref):
        @pl.when(pl.program_id(0) == 0)
        def _():
            carry_ref[...] = jnp.zeros_like(carry_ref)
            db_ref[...] = jnp.zeros_like(db_ref)

        dcv = dc_ref[...]
        dlf = _dot_hi(_tri(tb, False), dcv) + carry_ref[...]
        z = fa_ref[...] + b_ref[...]
        dz = dlf * _sigmoid(-z)
        dfa_ref[...] = dz.astype(BF16)
        db_ref[...] += jnp.sum(dz, axis=0, keepdims=True)
        carry_ref[...] = carry_ref[...] + jnp.sum(dcv, axis=0, keepdims=True)

    rev = pl.BlockSpec((tb, w), lambda i: (nb - 1 - i, 0))
    vec = pl.BlockSpec((1, w), lambda i: (0, 0))
    return pl.pallas_call(
        body, name=name, grid=(nb,), in_specs=[rev, rev, vec], out_specs=[rev, vec],
        out_shape=[jax.ShapeDtypeStruct((t, w), BF16), jax.ShapeDtypeStruct((1, w), F32)],
        scratch_shapes=[pltpu.VMEM((1, w), F32)], compiler_params=_params(("arbitrary",)),
    )(dc, fa, bias)


def _fox_probs(q_ref, k_ref, cc_ref, cr_ref, qi, tq, t):
    scale = HEAD_DIM ** -0.5
    s = lax.dot_general(q_ref[...].astype(BF16), k_ref[...].astype(BF16), _DN["nt"], preferred_element_type=F32)
    logits = s * scale + cc_ref[...] - cr_ref[...]
    qpos = qi * tq + lax.broadcasted_iota(jnp.int32, (tq, t), 0)
    kpos = lax.broadcasted_iota(jnp.int32, (tq, t), 1)
    logits = jnp.where(kpos <= qpos, logits, NEG_BIG)
    m = jnp.max(logits, axis=-1, keepdims=True)
    p = jnp.exp(logits - m)
    return p / jnp.sum(p, axis=-1, keepdims=True)


FOX_SEGMENTS = 4


def _fox_segments(t):
    tq = _pick(t, (256, 128))
    nseg = min(FOX_SEGMENTS, t // tq)
    return tq, nseg, t // tq // nseg


def _fox_specs(t, q0, kt, tq):
    cb = _col_blocks()
    dh = HEAD_DIM
    return [pl.BlockSpec((tq, dh), lambda h, i: (q0 + i, cb["q_a"] + h)),
            pl.BlockSpec((kt, dh), lambda h, i: (0, cb["k_a"] + h)),
            pl.BlockSpec((kt, dh), lambda h, i: (0, cb["v_a"] + h)),
            pl.BlockSpec((None, tq, 1), lambda h, i: (h, q0 + i, 0)),
            pl.BlockSpec((None, 1, kt), lambda h, i: (h, 0, 0))]


def fox_fwd(proj, c_col, c_row, name):
    t = proj.shape[0]
    tq, nseg, nq = _fox_segments(t)
    dh = HEAD_DIM

    def segment(out, r):
        q0, kt = r * nq, (r + 1) * nq * tq

        def body(q_ref, k_ref, v_ref, cc_ref, cr_ref, prev_ref, o_ref):
            p = _fox_probs(q_ref, k_ref, cc_ref, cr_ref, q0 + pl.program_id(1), tq, kt)
            o_ref[...] = jnp.dot(p.astype(BF16), v_ref[...].astype(BF16), preferred_element_type=F32).astype(BF16)

        return pl.pallas_call(
            body, name="%s_%d" % (name, r), grid=(HEADS, nq),
            in_specs=_fox_specs(t, q0, kt, tq) + [pl.BlockSpec(memory_space=pl.ANY)],
            out_specs=pl.BlockSpec((tq, dh), lambda h, i: (q0 + i, h)),
            out_shape=jax.ShapeDtypeStruct((t, WIDTH), BF16), input_output_aliases={5: 0},
            compiler_params=_params(("parallel", "parallel")),
        )(proj, proj, proj, c_col, c_row, out)

    out = lax.empty((t, WIDTH), BF16)
    for r in range(nseg):
        out = segment(out, r)
    return out


def fox_bwd(proj, c_col, c_row, do, name):
    t = proj.shape[0]
    tq, nseg, nq = _fox_segments(t)
    dh = HEAD_DIM
    scale = HEAD_DIM ** -0.5

    def segment(acc, r):
        q0, kt = r * nq, (r + 1) * nq * tq

        def body(q_ref, k_ref, v_ref, cc_ref, cr_ref, do_ref, dqp_ref, dkp_ref, dvp_ref, dccp_ref, dcrp_ref,
                 dq_ref, dk_ref, dv_ref, dcc_ref, dcr_ref):
            @pl.when(pl.program_id(1) == 0)
            def _():
                dk_ref[...] = dkp_ref[...]
                dv_ref[...] = dvp_ref[...]
                dcr_ref[...] = dcrp_ref[...]

            p = _fox_probs(q_ref, k_ref, cc_ref, cr_ref, q0 + pl.program_id(1), tq, kt)
            dov = do_ref[...].astype(BF16)
            kb = k_ref[...].astype(BF16)
            dv_ref[...] += lax.dot_general(p.astype(BF16), dov, _DN["tn"], preferred_element_type=F32)
            dp = lax.dot_general(dov, v_ref[...].astype(BF16), _DN["nt"], preferred_element_type=F32)
            ds = p * (dp - jnp.sum(p * dp, axis=-1, keepdims=True))
            dcc_ref[...] = jnp.sum(ds, axis=-1, keepdims=True)
            dcr_ref[...] -= jnp.sum(ds, axis=0, keepdims=True)
            dss = (ds * scale).astype(BF16)
            dq_ref[...] = jnp.dot(dss, kb, preferred_element_type=F32).astype(BF16)
            dk_ref[...] += lax.dot_general(dss, q_ref[...].astype(BF16), _DN["tn"], preferred_element_type=F32)

        rows = pl.BlockSpec((tq, dh), lambda h, i: (q0 + i, h))
        keys = pl.BlockSpec((kt, dh), lambda h, i: (0, h))
        col = pl.BlockSpec((None, tq, 1), lambda h, i: (h, q0 + i, 0))
        row = pl.BlockSpec((None, 1, kt), lambda h, i: (h, 0, 0))
        anywhere = pl.BlockSpec(memory_space=pl.ANY)
        return pl.pallas_call(
            body, name="%s_%d" % (name, r), grid=(HEADS, nq),
            in_specs=_fox_specs(t, q0, kt, tq) + [rows, anywhere, keys, keys, anywhere, row],
            out_specs=[rows, keys, keys, col, row],
            out_shape=[jax.ShapeDtypeStruct(a.shape, a.dtype) for a in acc],
            input_output_aliases={6 + k: k for k in range(5)},
            compiler_params=_params(("parallel", "arbitrary")),
        )(proj, proj, proj, c_col, c_row, do, *acc)

    acc = [lax.empty((t, WIDTH), BF16), jnp.zeros((t, WIDTH), F32), jnp.zeros((t, WIDTH), F32),
           lax.empty((HEADS, t, 1), F32), jnp.zeros((HEADS, 1, t), F32)]
    for r in range(nseg):
        acc = segment(acc, r)
    return acc


def _lower_bound(lg_ref):
    l0 = lg_ref[0:1, :]
    l1 = lg_ref[1:2, :]
    m = jnp.maximum(l0, l1)
    e0 = jnp.exp(l0 - m)
    e1 = jnp.exp(l1 - m)
    return e0 / (e0 + e1)


def _hgrn_inputs(qb_ref, fb_ref, lg_ref, q_s, k_s, cum_s):
    lb = _lower_bound(lg_ref)
    sig = _sigmoid(fb_ref[...])
    f = lb + (1.0 - lb) * sig
    q_s[...] = _silu(qb_ref[...])
    k_s[...] = 1.0 - f
    cum_s[...] = _dot_hi(_tri(CHUNK, True), jnp.log(f))
    return lb, sig, f


def _boundary(cum_s, a):
    if a == 0:
        return jnp.zeros((1, HEAD_DIM), F32)
    return cum_s[pl.ds(SUB * a - 1, 1), :]


def _hgrn_scores(q_s, k_s, cum_s):
    cum = cum_s[...]
    kk = k_s[...]
    lane = lax.broadcasted_iota(jnp.int32, (SUB, CHUNK), 1)
    row = lax.broadcasted_iota(jnp.int32, (SUB, 1), 0)
    blocks = []
    for a in range(CHUNK // SUB):
        rows = pl.ds(SUB * a, SUB)
        ca = _boundary(cum_s, a)
        cum_a = cum_s[rows, :]
        q_a = q_s[rows, :]
        qa = q_a * jnp.exp(cum_a - ca)
        ka = kk * jnp.exp(jnp.minimum(ca - cum, 0.0))
        blk = lax.dot_general(qa, ka, _DN["nt"], preferred_element_type=F32)
        blk = jnp.where(lane < SUB * a, blk, 0.0)
        for s in range(SUB):
            r = SUB * a + s
            e = jnp.exp(jnp.minimum(cum_a - cum_s[pl.ds(r, 1), :], 0.0))
            col = jnp.sum(q_a * k_s[pl.ds(r, 1), :] * e, axis=-1, keepdims=True)
            col = jnp.where(row >= s, col, 0.0)
            blk = jnp.where(lane == r, col, blk)
        blocks.append(blk)
    return jnp.concatenate(blocks, axis=0)


def hgrn_fwd(proj, lb_logits, name):
    t = proj.shape[0]
    n = t // CHUNK
    cb = _col_blocks()
    dh = HEAD_DIM

    hb = HGRN_HEADS_PER_STEP
    w = hb * dh

    def one_head(qb_ref, fb_ref, ib_ref, lg_ref, o_ref, st_ref, state, q_s, k_s, cum_s):
        _hgrn_inputs(qb_ref, fb_ref, lg_ref, q_s, k_s, cum_s)
        st = state[...]
        st_ref[...] = st
        cum = cum_s[...]
        v = ib_ref[...]
        qe = q_s[...] * jnp.exp(cum)
        inter = lax.dot_general(qe, st, _DN["nt"], preferred_element_type=F32)
        a_mat = _hgrn_scores(q_s, k_s, cum_s)
        o_ref[...] = inter + jnp.dot(a_mat, v, preferred_element_type=F32)
        last = cum_s[pl.ds(CHUNK - 1, 1), :]
        kd = k_s[...] * jnp.exp(last - cum)
        state[...] = st * jnp.exp(last) + lax.dot_general(v, kd, _DN["tn"], preferred_element_type=F32)

    def body(qb_ref, fb_ref, ib_ref, lg_ref, o_ref, st_ref, *scratch):
        @pl.when(pl.program_id(1) == 0)
        def _():
            for j in range(hb):
                scratch[4 * j][...] = jnp.zeros((dh, dh), F32)

        for j in range(hb):
            cols = (slice(None), pl.ds(j * dh, dh))
            one_head(qb_ref.at[cols], fb_ref.at[cols], ib_ref.at[cols], lg_ref.at[cols], o_ref.at[cols],
                     st_ref.at[j], *scratch[4 * j:4 * j + 4])

    blk = lambda off: pl.BlockSpec((CHUNK, w), lambda h, i: (i, off // hb + h))
    return pl.pallas_call(
        body, name=name, grid=(HEADS // hb, n),
        in_specs=[blk(cb["q_b"]), blk(cb["f_b"]), blk(cb["i_b"]), pl.BlockSpec((2, w), lambda h, i: (0, h))],
        out_specs=[pl.BlockSpec((CHUNK, w), lambda h, i: (i, h)),
                   pl.BlockSpec((hb, None, dh, dh), lambda h, i: (h, i, 0, 0))],
        out_shape=[jax.ShapeDtypeStruct((t, WIDTH), F32), jax.ShapeDtypeStruct((HEADS, n, dh, dh), F32)],
        scratch_shapes=([pltpu.VMEM((dh, dh), F32)] + [pltpu.VMEM((CHUNK, dh), F32)] * 3) * hb,
        compiler_params=_params(("parallel", "arbitrary")),
    )(proj, proj, proj, lb_logits)


def hgrn_bwd(proj, lb_logits, states, do, name):
    t = proj.shape[0]
    n = t // CHUNK
    cb = _col_blocks()
    dh = HEAD_DIM
    nsub = CHUNK // SUB

    hb = HGRN_HEADS_PER_STEP
    w = hb * dh

    def one_head(qb_ref, fb_ref, ib_ref, lg_ref, st_ref, do_ref, dqb_ref, dfb_ref, dib_ref, dlb_ref,
                 dstate, q_s, k_s, cum_s, da_s, dq_s, dk_s):
        lb, sig, f = _hgrn_inputs(qb_ref, fb_ref, lg_ref, q_s, k_s, cum_s)
        st = st_ref[...]
        dst = dstate[...]
        cum = cum_s[...]
        q = q_s[...]
        kk = k_s[...]
        v = ib_ref[...]
        dov = do_ref[...]
        e_cum = jnp.exp(cum)
        qe = q * e_cum
        last = cum_s[pl.ds(CHUNK - 1, 1), :]
        e_last = jnp.exp(last)
        e_tail = jnp.exp(last - cum)
        kd = kk * e_tail

        a_mat = _hgrn_scores(q_s, k_s, cum_s)
        tri = _tri(CHUNK, True)
        da_s[...] = lax.dot_general(dov, v, _DN["nt"], preferred_element_type=F32) * tri
        dv = (lax.dot_general(a_mat, dov, _DN["tn"], preferred_element_type=F32)
              + lax.dot_general(kd, dst, _DN["nt"], preferred_element_type=F32))
        dk_state = jnp.dot(v, dst, preferred_element_type=F32) * e_tail
        dq_inter = jnp.dot(dov, st, preferred_element_type=F32) * e_cum
        dstate[...] = dst * e_last + lax.dot_general(dov, qe, _DN["tn"], preferred_element_type=F32)

        lane = lax.broadcasted_iota(jnp.int32, (SUB, CHUNK), 1)
        row = lax.broadcasted_iota(jnp.int32, (SUB, 1), 0)
        dk_s[...] = jnp.zeros_like(dk_s)
        for a in range(nsub):
            rows = pl.ds(SUB * a, SUB)
            ca = _boundary(cum_s, a)
            cum_a = cum_s[rows, :]
            q_a = q_s[rows, :]
            ea = jnp.exp(cum_a - ca)
            eb = jnp.exp(jnp.minimum(ca - cum, 0.0))
            da_a = da_s[rows, :]
            da_off = jnp.where(lane < SUB * a, da_a, 0.0)
            dq_a = ea * jnp.dot(da_off, kk * eb, preferred_element_type=F32)
            dk_s[...] += eb * lax.dot_general(da_off, q_a * ea, _DN["tn"], preferred_element_type=F32)
            dk_rows = jnp.zeros((SUB, dh), F32)
            for s in range(SUB):
                r = SUB * a + s
                e = jnp.exp(jnp.minimum(cum_a - cum_s[pl.ds(r, 1), :], 0.0))
                dcol = jnp.sum(jnp.where(lane == r, da_a, 0.0), axis=-1, keepdims=True)
                dcol = jnp.where(row >= s, dcol, 0.0)
                w = dcol * e
                dq_a = dq_a + w * k_s[pl.ds(r, 1), :]
                dk_rows = jnp.where(row == s, jnp.sum(w * q_a, axis=0, keepdims=True), dk_rows)
            dq_s[rows, :] = dq_a
            dk_s[rows, :] += dk_rows

        dq = dq_inter + dq_s[...]
        dk = dk_s[...] + dk_state
        d_last = (jnp.sum(dst * st, axis=0, keepdims=True) * e_last
                  + jnp.sum(kk * dk_state, axis=0, keepdims=True))
        rowc = lax.broadcasted_iota(jnp.int32, (CHUNK, 1), 0)
        dcum = q * dq - kk * dk + jnp.where(rowc == CHUNK - 1, d_last, 0.0)
        dg = _dot_hi(_tri(CHUNK, False), dcum)
        df = dg / f - dk
        dqb_ref[...] = (dq * _silu_grad(qb_ref[...])).astype(BF16)
        dfb_ref[...] = (df * (1.0 - lb) * sig * (1.0 - sig)).astype(BF16)
        dib_ref[...] = dv.astype(BF16)
        dlb_ref[...] += jnp.sum(df * (1.0 - sig), axis=0, keepdims=True)

    def body(qb_ref, fb_ref, ib_ref, lg_ref, st_ref, do_ref, dqb_ref, dfb_ref, dib_ref, dlb_ref, *scratch):
        @pl.when(pl.program_id(1) == 0)
        def _():
            for j in range(hb):
                scratch[7 * j][...] = jnp.zeros((dh, dh), F32)
            dlb_ref[...] = jnp.zeros_like(dlb_ref)

        for j in range(hb):
            cols = (slice(None), pl.ds(j * dh, dh))
            one_head(qb_ref.at[cols], fb_ref.at[cols], ib_ref.at[cols], lg_ref.at[cols], st_ref.at[j], do_ref.at[cols],
                     dqb_ref.at[cols], dfb_ref.at[cols], dib_ref.at[cols], dlb_ref.at[cols],
                     *scratch[7 * j:7 * j + 7])

    blk = lambda off: pl.BlockSpec((CHUNK, w), lambda h, i: (n - 1 - i, off // hb + h))
    out_blk = pl.BlockSpec((CHUNK, w), lambda h, i: (n - 1 - i, h))
    return pl.pallas_call(
        body, name=name, grid=(HEADS // hb, n),
        in_specs=[blk(cb["q_b"]), blk(cb["f_b"]), blk(cb["i_b"]), pl.BlockSpec((2, w), lambda h, i: (0, h)),
                  pl.BlockSpec((hb, None, dh, dh), lambda h, i: (h, n - 1 - i, 0, 0)), out_blk],
        out_specs=[out_blk, out_blk, out_blk, pl.BlockSpec((1, w), lambda h, i: (0, h))],
        out_shape=[jax.ShapeDtypeStruct((t, WIDTH), BF16)] * 3 + [jax.ShapeDtypeStruct((1, WIDTH), F32)],
        scratch_shapes=([pltpu.VMEM((dh, dh), F32)] + [pltpu.VMEM((CHUNK, dh), F32)] * 3
                        + [pltpu.VMEM((CHUNK, CHUNK), F32)] + [pltpu.VMEM((CHUNK, dh), F32)] * 2) * hb,
        compiler_params=_params(("parallel", "arbitrary")),
    )(proj, proj, proj, lb_logits, states, do)


def lb_bwd(dlb, lb_logits, name):
    def body(dlb_ref, lg_ref, o_ref):
        p0 = _lower_bound(lg_ref)
        d0 = dlb_ref[...] * p0 * (1.0 - p0)
        o_ref[0:1, :] = d0
        o_ref[1:2, :] = -d0

    return pl.pallas_call(body, name=name, out_shape=jax.ShapeDtypeStruct(lb_logits.shape, F32))(dlb, lb_logits)


def gnorm_fwd(o_raw, proj, norm_g, name):
    t = o_raw.shape[0]
    tr = _row_tile(t)
    cb = _col_blocks()
    dh = HEAD_DIM

    def body(o_ref, gb_ref, g_ref, y_ref):
        x = o_ref[...]
        y_ref[...] = (x * _rstd(x) * g_ref[...] * _silu(gb_ref[...])).astype(BF16)

    return pl.pallas_call(
        body, name=name, grid=(t // tr, HEADS),
        in_specs=[pl.BlockSpec((tr, dh), lambda i, h: (i, h)), pl.BlockSpec((tr, dh), lambda i, h: (i, cb["g_b"] + h)),
                  pl.BlockSpec((1, dh), lambda i, h: (0, 0))],
        out_specs=pl.BlockSpec((tr, dh), lambda i, h: (i, h)),
        out_shape=jax.ShapeDtypeStruct((t, WIDTH), BF16), compiler_params=_params(("parallel", "parallel")),
    )(o_raw, proj, norm_g)


def gnorm_bwd(dy, o_raw, proj, norm_g, name):
    t = o_raw.shape[0]
    tr = _row_tile(t)
    cb = _col_blocks()
    dh = HEAD_DIM

    def body(dy_ref, o_ref, gb_ref, g_ref, do_ref, dgb_ref, dg_ref):
        @pl.when((pl.program_id(0) == 0) & (pl.program_id(1) == 0))
        def _():
            dg_ref[...] = jnp.zeros_like(dg_ref)

        x = o_ref[...]
        gb = gb_ref[...]
        dyv = dy_ref[...]
        g = g_ref[...]
        dx, dg = _rms_bwd(x, g, dyv * _silu(gb))
        do_ref[...] = dx
        dgb_ref[...] = (dyv * (x * _rstd(x) * g) * _silu_grad(gb)).astype(BF16)
        dg_ref[...] += dg

    hb = pl.BlockSpec((tr, dh), lambda i, h: (i, h))
    vec = pl.BlockSpec((1, dh), lambda i, h: (0, 0))
    return pl.pallas_call(
        body, name=name, grid=(t // tr, HEADS),
        in_specs=[hb, hb, pl.BlockSpec((tr, dh), lambda i, h: (i, cb["g_b"] + h)), vec],
        out_specs=[hb, hb, vec],
        out_shape=[jax.ShapeDtypeStruct((t, WIDTH), F32), jax.ShapeDtypeStruct((t, WIDTH), BF16),
                   jax.ShapeDtypeStruct((1, dh), F32)],
        compiler_params=_params(("arbitrary", "arbitrary")),
    )(dy, o_raw, proj, norm_g)


def merge_fwd(proj, y, name):
    _, t, d = y.shape
    tr = _row_tile(t)
    tc = _pick(d, (1024, 512, 256, 128))
    cb = _col_blocks()
    ga, gb = cb["gate_a"] * LANES // tc, cb["gate_b"] * LANES // tc

    def body(ga_ref, gb_ref, y_ref, o_ref):
        o_ref[...] = (_sigmoid(ga_ref[...]) * y_ref[0] + _sigmoid(gb_ref[...]) * y_ref[1]).astype(BF16)

    return pl.pallas_call(
        body, name=name, grid=(t // tr, d // tc),
        in_specs=[pl.BlockSpec((tr, tc), lambda i, j: (i, ga + j)), pl.BlockSpec((tr, tc), lambda i, j: (i, gb + j)),
                  pl.BlockSpec((2, tr, tc), lambda i, j: (0, i, j))],
        out_specs=pl.BlockSpec((tr, tc), lambda i, j: (i, j)),
        out_shape=jax.ShapeDtypeStruct((t, d), BF16), compiler_params=_params(("parallel", "parallel")),
    )(proj, proj, y)


def merge_bwd(dm, proj, y, name):
    _, t, d = y.shape
    tr = _row_tile(t)
    tc = _pick(d, (1024, 512, 256, 128))
    cb = _col_blocks()
    ga, gb = cb["gate_a"] * LANES // tc, cb["gate_b"] * LANES // tc

    def body(dm_ref, ga_ref, gb_ref, y_ref, dg_ref, dy_ref):
        dmv = dm_ref[...]
        for idx, g_ref in enumerate((ga_ref, gb_ref)):
            s = _sigmoid(g_ref[...])
            dg_ref[idx] = (dmv * y_ref[idx] * s * (1.0 - s)).astype(BF16)
            dy_ref[idx] = (dmv * s).astype(BF16)

    pair = pl.BlockSpec((2, tr, tc), lambda i, j: (0, i, j))
    return pl.pallas_call(
        body, name=name, grid=(t // tr, d // tc),
        in_specs=[pl.BlockSpec((tr, tc), lambda i, j: (i, j)), pl.BlockSpec((tr, tc), lambda i, j: (i, ga + j)),
                  pl.BlockSpec((tr, tc), lambda i, j: (i, gb + j)), pair],
        out_specs=[pair, pair],
        out_shape=[jax.ShapeDtypeStruct((2, t, d), BF16)] * 2, compiler_params=_params(("parallel", "parallel")),
    )(dm, proj, proj, y)


def ple_tail(h, a, b, g, target, name):
    t, d = h.shape
    tr = _row_tile(t)

    def body(h_ref, a_ref, b_ref, g_ref, t_ref, loss_ref, dh_ref, da_ref, db_ref, dg_ref):
        @pl.when(pl.program_id(0) == 0)
        def _():
            loss_ref[...] = jnp.zeros_like(loss_ref)
            dg_ref[...] = jnp.zeros_like(dg_ref)

        s = _sigmoid(a_ref[...])
        bv = b_ref[...]
        z = s * bv
        gv = g_ref[...]
        err = h_ref[...] + z * _rstd(z) * gv - t_ref[...]
        loss_ref[...] += 0.5 * jnp.sum(jnp.sum(err * err, axis=-1, keepdims=True), axis=0, keepdims=True) / d
        dh = err / d
        dh_ref[...] = dh
        dz, dg = _rms_bwd(z, gv, dh)
        da_ref[...] = (dz * bv * s * (1.0 - s)).astype(BF16)
        db_ref[...] = (dz * s).astype(BF16)
        dg_ref[...] += dg

    row = pl.BlockSpec((tr, d), lambda i: (i, 0))
    vec = pl.BlockSpec((1, d), lambda i: (0, 0))
    return pl.pallas_call(
        body, name=name, grid=(t // tr,), in_specs=[row, row, row, vec, row],
        out_specs=[pl.BlockSpec((1, 1), lambda i: (0, 0)), row, row, row, vec],
        out_shape=[jax.ShapeDtypeStruct((1, 1), F32), jax.ShapeDtypeStruct((t, d), F32),
                   jax.ShapeDtypeStruct((t, d), BF16), jax.ShapeDtypeStruct((t, d), BF16),
                   jax.ShapeDtypeStruct((1, d), F32)],
        compiler_params=_params(("arbitrary",)),
    )(h, a, b, g, target)


def _ffn_fwd(h, pre_g, post_g, get_w, idx, tag):
    u = norm_in(h, pre_g, tag + "_norm")
    gu = None
    for i, key in enumerate(GATE_UP_KEYS["gu" + idx]):
        w = get_w(key, h if gu is None else gu)
        gu = mm_nn_col(u, w, F32, "%s_gate_up_%d" % (tag, i), slot0=i, total=2, into=gu)
    act = swiglu_act(gu, tag + "_act")
    y = mm_nn_2d(act, get_w("down" + idx, gu), F32, tag + "_down")
    out = resid_post(h, y, post_g, MACARON_SCALE, tag + "_out")
    return out, (h, u, gu, act, y)


def _ffn_bwd(dh, saved, pre_g, post_g, get_w, emit, advance, idx, tag):
    h, u, gu, act, y = saved
    dy, d_post = post_bwd(dh, y, post_g, MACARON_SCALE, tag + "_post_bwd")
    m1 = emit("down" + idx, mm_tn_2d(act, dy, F32, tag + "_dw_down"))
    dact = mm_nt_2d(dy, get_w("down" + idx), F32, tag + "_dact", after=[m1])
    m2 = advance(dact)
    dgu = swiglu_bwd(dact, gu, tag + "_act_bwd")
    m3 = emit("gu" + idx, mm_tn_col(u, dgu, N_CHIPS, F32, tag + "_dw_gate_up"))
    du = None
    for i, key in enumerate(GATE_UP_KEYS["gu" + idx]):
        du = mm_nt_col(dgu, get_w(key), F32, "%s_du_%d" % (tag, i), after=[m2, m3] if du is None else (),
                       slot0=i, init=du)
    m4 = advance(du)
    dh_in, d_pre = pre_bwd(dh, h, pre_g, [du], tag + "_pre_bwd", after=[m4])
    return dh_in, d_pre, d_post


def _heads_col(a):
    t = a.shape[0]
    at = a[:, :HEADS].T
    return at.reshape(HEADS, t, 1), at.reshape(HEADS, 1, t)


def layer_step(x, p, target, gains, fox_bias, lb_logits, norm_g, get_w, emit, advance):
    t = x.shape[0]
    h1, s1 = _ffn_fwd(x, gains["ffn1_pre"], gains["ffn1_post"], get_w, "1", "ffn1")

    u2 = norm_in(h1, gains["mix_pre"], "mix_norm")
    proj = mm_nt_2d(u2, get_w("in_main", h1), F32, "mix_in")
    fa = mm_nt_2d(u2, get_w("in_fa"), F32, "mix_in_fa")
    c = fox_prep(fa, fox_bias, "fox_prep")
    c_col, c_row = _heads_col(c)
    o_a = fox_fwd(proj, c_col, c_row, "fox_fwd")
    o_raw, states = hgrn_fwd(proj, lb_logits, "hgrn_fwd")
    o_b = gnorm_fwd(o_raw, proj, norm_g, "hgrn_norm")
    o_ab = jnp.stack([o_a, o_b])
    y_ab = _mm_branches(o_ab, get_w("proj", proj), "mix_proj")
    merged = merge_fwd(proj, y_ab, "mix_merge")
    mo = mm_nn_2d(merged, get_w("out"), F32, "mix_out")
    h2 = resid_post(h1, mo, gains["mix_post"], 1.0, "mix_resid")

    h3, s3 = _ffn_fwd(h2, gains["ffn2_pre"], gains["ffn2_post"], get_w, "2", "ffn2")

    u4 = norm_in(h3, gains["ple_pre"], "ple_norm")
    a4 = mm_nn_2d(u4, get_w("ple_gate"), F32, "ple_gate")
    b4 = mm_nn_col(p, get_w("ple_proj"), F32, "ple_proj")[0]
    loss, dh4, da4, db4, d_ple_post = ple_tail(h3, a4, b4, gains["ple_post"], target, "ple_tail")

    marks = [emit("ple_gate", mm_tn_2d(u4, da4, F32, "ple_dw_gate")),
             emit("ple_proj", mm_tn_col(p, db4[None], N_CHIPS, F32, "ple_dw_proj"))]
    du4 = mm_nt_2d(da4, get_w("ple_gate"), F32, "ple_du", after=marks)
    dh3, d_ple_pre = pre_bwd(dh4, h3, gains["ple_pre"], [du4], "ple_pre_bwd", after=[advance(du4)])

    dh2, d_f2_pre, d_f2_post = _ffn_bwd(dh3, s3, gains["ffn2_pre"], gains["ffn2_post"], get_w, emit, advance,
                                        "2", "ffn2")

    dmo, d_mix_post = post_bwd(dh2, mo, gains["mix_post"], 1.0, "mix_post_bwd")
    marks = [emit("out", mm_tn_2d(merged, dmo, F32, "mix_dw_out"))]
    dmerged = mm_nt_2d(dmo, get_w("out"), F32, "mix_dmerged", after=marks)
    dgate, dy_ab = merge_bwd(dmerged, proj, y_ab, "mix_merge_bwd")
    marks = [advance(dmerged), emit("proj", _mm_branches_dw(o_ab, dy_ab, "mix_dw_proj"))]
    do_ab = _mm_branches_bwd(dy_ab, get_w("proj"), "mix_do")
    do_raw, dg_b, d_norm_g = gnorm_bwd(do_ab[1], o_raw, proj, norm_g, "hgrn_norm_bwd")
    dq_b, df_b, di_b, dlb = hgrn_bwd(proj, lb_logits, states, do_raw, "hgrn_bwd")
    d_lb_logits = lb_bwd(dlb, lb_logits, "lb_bwd")
    dq_a, dk_a, dv_a, dc_col, dc_row = fox_bwd(proj, c_col, c_row, do_ab[0], "fox_bwd")
    dc = (dc_col.reshape(HEADS, t) + dc_row.reshape(HEADS, t)).T
    dc = jnp.pad(dc, ((0, 0), (0, LANES - HEADS)))
    dfa, d_fox_bias = fox_post_bwd(dc, fa, fox_bias, "fox_post_bwd")
    dproj = jnp.concatenate([dq_a, dk_a.astype(BF16), dv_a.astype(BF16), dq_b, df_b, di_b, dg_b,
                             dgate[0], dgate[1]], axis=1)
    marks.append(emit("in_main", mm_tn_2d(dproj, u2, F32, "mix_dw_in")))
    marks.append(emit("in_fa", mm_tn_2d(dfa, u2, F32, "mix_dw_in_fa")))
    du2a = mm_nn_2d(dproj, get_w("in_main"), F32, "mix_du", after=marks)
    du2b = mm_nn_2d(dfa, get_w("in_fa"), F32, "mix_du_fa")
    dh1, d_mix_pre = pre_bwd(dh2, h1, gains["mix_pre"], [du2a, du2b], "mix_pre_bwd", after=[advance(du2a)])

    dx, d_f1_pre, d_f1_post = _ffn_bwd(dh1, s1, gains["ffn1_pre"], gains["ffn1_post"], get_w, emit, advance,
                                       "1", "ffn1")

    small = dict(ffn1_pre=d_f1_pre, ffn1_post=d_f1_post, mix_pre=d_mix_pre, mix_post=d_mix_post,
                 ffn2_pre=d_f2_pre, ffn2_post=d_f2_post, ple_pre=d_ple_pre, ple_post=d_ple_post,
                 fox_bias=d_fox_bias, lb_logits=d_lb_logits, norm_g=d_norm_g)
    return loss, dx, small


def _mm_branches(o_ab, w_proj, name):
    g, t, kk = o_ab.shape
    _, jn, _, ns = w_proj.shape
    tm = _pick(t, (512, 256, 128))
    return _mm(o_ab, w_proj, mode="nn", grid=(t // tm, g * jn, 1),
               a_spec=pl.BlockSpec((None, tm, kk), lambda i, j, k: (j // jn, i, 0)),
               b_spec=pl.BlockSpec((None, None, kk, ns), lambda i, j, k: (j // jn, j % jn, 0, 0)),
               o_spec=pl.BlockSpec((None, tm, ns), lambda i, j, k: (j // jn, i, j % jn)),
               out_shape=jax.ShapeDtypeStruct((g, t, jn * ns), F32), acc_shape=(tm, ns), name=name)


def _mm_branches_bwd(dy_ab, w_proj, name):
    g, t, _ = dy_ab.shape
    _, jn, kk, ns = w_proj.shape
    tm = _pick(t, (512, 256, 128))
    return _mm(dy_ab, w_proj, mode="nt", grid=(t // tm, g, jn),
               a_spec=pl.BlockSpec((None, tm, ns), lambda i, j, k: (j, i, k)),
               b_spec=pl.BlockSpec((None, None, kk, ns), lambda i, j, k: (j, k, 0, 0)),
               o_spec=pl.BlockSpec((None, tm, kk), lambda i, j, k: (j, i, 0)),
               out_shape=jax.ShapeDtypeStruct((g, t, kk), F32), acc_shape=(tm, kk), name=name)


def _mm_branches_dw(o_ab, dy_ab, name):
    g, t, kk = o_ab.shape
    d = dy_ab.shape[2]
    jn = N_CHIPS
    ns = d // jn
    return _mm(o_ab, dy_ab, mode="tn", grid=(1, g * jn, 1),
               a_spec=pl.BlockSpec((None, t, kk), lambda i, j, k: (j // jn, 0, 0)),
               b_spec=pl.BlockSpec((None, t, ns), lambda i, j, k: (j // jn, 0, j % jn)),
               o_spec=pl.BlockSpec((None, None, kk, ns), lambda i, j, k: (j // jn, j % jn, 0, 0)),
               out_shape=jax.ShapeDtypeStruct((g, jn, kk, ns), F32), acc_shape=(kk, ns), name=name)


HBM_SPEC = pl.BlockSpec(memory_space=pltpu.HBM)
SEM_SPEC = pl.BlockSpec(memory_space=pltpu.SEMAPHORE)
ANY_SPEC = pl.BlockSpec(memory_space=pl.ANY)
EFFECT = pltpu.SideEffectType.DATAFLOW_SIDE_EFFECTING


def _in_hbm(a):
    return pltpu.with_memory_space_constraint(a, pltpu.HBM)


def _place():
    x, y, c = lax.axis_index("x"), lax.axis_index("y"), lax.axis_index("c")
    chips = [(1 - x, y), (x, 1 - y), (1 - x, 1 - y)]
    return x, y, c, chips


def _half(shape, which, axis):
    n = shape[-2 + axis] // 2
    cut = pl.ds(which * n, n)
    return (cut, slice(None)) if axis == 0 else (slice(None), cut)


def _half_shape(shape, axis):
    s = list(shape)
    s[len(s) - 2 + axis] //= 2
    return tuple(s)


def _remote(src, dst, send_sems, recv_sems, k, to):
    return pltpu.make_async_remote_copy(src_ref=src, dst_ref=dst, send_sem=send_sems.at[k], recv_sem=recv_sems.at[k],
                                        device_id=to, device_id_type=MESH)


def split_start(name, srcs, lands, counts, copies):
    ns, nl, nset = len(srcs), len(lands), len(counts)

    def body(*refs):
        src_refs, land_refs = refs[:ns], refs[ns:ns + nl]
        sems = refs[ns + nl:ns + nl + 2 * nset]
        for s, plan in enumerate(copies(src_refs, land_refs)):
            for k, (src, dst, to) in enumerate(plan):
                _remote(src, dst, sems[2 * s], sems[2 * s + 1], k, to).start()
        refs[-1][...] = jnp.zeros_like(refs[-1])

    out_shape = []
    for n in counts:
        out_shape += [pltpu.SemaphoreType.DMA((n,)), pltpu.SemaphoreType.DMA((n,))]
    out_shape += [pltpu.HBM(a.shape, a.dtype) for a in list(srcs) + list(lands)]
    out_shape.append(jax.ShapeDtypeStruct((8, LANES), F32))
    res = pl.pallas_call(
        body, name=name, out_shape=tuple(out_shape), in_specs=[HBM_SPEC] * (ns + nl),
        out_specs=tuple([SEM_SPEC] * (2 * nset) + [HBM_SPEC] * (ns + nl) + [pl.BlockSpec(memory_space=pltpu.VMEM)]),
        input_output_aliases={i: 2 * nset + i for i in range(ns + nl)},
        compiler_params=pltpu.CompilerParams(has_side_effects=EFFECT),
    )(*[_in_hbm(a) for a in list(srcs) + list(lands)])
    sems = [(res[2 * s], res[2 * s + 1]) for s in range(nset)]
    return sems, list(res[2 * nset:2 * nset + ns]), list(res[2 * nset + ns:-1]), res[-1]


def split_wait(name, srcs, lands, sems, afters, copies):
    afters = [a for a in afters if a is not None]
    ns, nl, na = len(srcs), len(lands), len(afters)

    def body(*refs):
        src_refs, land_refs = refs[:ns], refs[ns:ns + nl]
        send_sems, recv_sems = refs[ns + nl:ns + nl + 2]
        for k, (src, dst, to) in enumerate(copies(src_refs, land_refs)):
            cp = _remote(src, dst, send_sems, recv_sems, k, to)
            cp.wait_send()
            cp.wait_recv()

    res = pl.pallas_call(
        body, name=name, out_shape=tuple(pltpu.HBM(a.shape, a.dtype) for a in list(srcs) + list(lands)),
        in_specs=[HBM_SPEC] * (ns + nl) + [SEM_SPEC, SEM_SPEC] + [ANY_SPEC] * na,
        out_specs=tuple([HBM_SPEC] * (ns + nl)), input_output_aliases={i: i for i in range(ns + nl)},
        compiler_params=pltpu.CompilerParams(has_side_effects=EFFECT),
    )(*srcs, *lands, sems[0], sems[1], *afters)
    return list(res[:ns]), list(res[ns:])


def _gather_plan(blocks):
    def copies(src_refs, land_refs):
        x, y, c, chips = _place()
        j_me = 2 * x + y
        plan = []
        for si, li, g, axis in blocks:
            src, land = src_refs[si], land_refs[li].at[g]
            mine = _half(src.shape, c, axis)
            for px, py in chips:
                plan.append((src.at[mine], land.at[(j_me,) + mine], (px, py, c)))
            plan.append((src, land.at[j_me], (x, y, 1 - c)))
        return plan
    return copies


def _gather_arrivals(blocks):
    def copies(src_refs, land_refs):
        x, y, c, chips = _place()
        j_me = 2 * x + y
        plan = []
        for si, li, g, axis in blocks:
            src, land = src_refs[si], land_refs[li].at[g]
            mine = _half(src.shape, c, axis)
            for px, py in chips:
                plan.append((src.at[mine], land.at[(2 * px + py,) + mine], (px, py, c)))
            plan.append((src, land.at[j_me], (x, y, 1 - c)))
        return plan
    return copies


def gather_pass(name, lands, blocks):
    n = len(lands)

    def body(*refs):
        outs = refs[n:2 * n]
        send_sems, recv_sems = refs[2 * n:]
        x, y, c, chips = _place()
        sent = []
        for i, (li, g, axis) in enumerate(blocks):
            land = outs[li].at[g]
            mine = _half(land.shape[1:], c, axis)
            for k, (px, py) in enumerate(chips):
                part = land.at[(2 * px + py,) + mine]
                cp = _remote(part, part, send_sems, recv_sems, 3 * i + k, (x, y, 1 - c))
                cp.start()
                sent.append(cp)
        for i, (li, g, axis) in enumerate(blocks):
            land = outs[li].at[g]
            other = _half(land.shape[1:], 1 - c, axis)
            for k, (px, py) in enumerate(chips):
                part = land.at[(2 * px + py,) + other]
                _remote(part, part, send_sems, recv_sems, 3 * i + k, (x, y, 1 - c)).wait_recv()
        for cp in sent:
            cp.wait_send()

    m = 3 * len(blocks)
    return pl.pallas_call(
        body, name=name, in_specs=[ANY_SPEC] * n, out_specs=[ANY_SPEC] * n,
        out_shape=[jax.ShapeDtypeStruct(a.shape, a.dtype) for a in lands],
        input_output_aliases={i: i for i in range(n)},
        scratch_shapes=[pltpu.SemaphoreType.DMA((m,)), pltpu.SemaphoreType.DMA((m,))],
    )(*lands)


def _pair_plan(axes):
    def copies(src_refs, land_refs):
        x, y, c, _ = _place()
        return [(src_refs[i].at[(slice(None), slice(None)) + _half(src_refs[i].shape, 1 - c, a)], land_refs[i],
                 (x, y, 1 - c)) for i, a in enumerate(axes)]
    return copies


def _scatter_plan(n):
    def copies(src_refs, land_refs):
        x, y, c, chips = _place()
        return [(src_refs[i].at[:, 2 * px + py], land_refs[i].at[k], (px, py, c))
                for i in range(n) for k, (px, py) in enumerate(chips)]
    return copies


def _broadcast_plan(axes, arrivals):
    def copies(src_refs, land_refs):
        x, y, c, _ = _place()
        plan = []
        for i, a in enumerate(axes):
            part = src_refs[i].at[(slice(None),) + _half(src_refs[i].shape, (1 - c) if arrivals else c, a)]
            plan.append((part, part, (x, y, 1 - c)))
        return plan
    return copies


N_DEV = 8
SLAB_ROWS = 16


def allreduce_small(slab, after):
    def body(x_ref, after_ref, o_ref, land, send_sems, recv_sems):
        x, y, c, _ = _place()
        me = 4 * x + 2 * y + c
        land[me] = x_ref[...]
        copies = []
        for d in range(1, N_DEV):
            to = (me + d) % N_DEV
            cp = pltpu.make_async_remote_copy(
                src_ref=x_ref, dst_ref=land.at[me], send_sem=send_sems.at[d - 1], recv_sem=recv_sems.at[me],
                device_id=(to // 4, (to // 2) % 2, to % 2), device_id_type=MESH)
            cp.start()
            copies.append(cp)
        for d in range(1, N_DEV):
            frm = (me + d) % N_DEV
            pltpu.make_async_remote_copy(
                src_ref=x_ref, dst_ref=land.at[frm], send_sem=send_sems.at[d - 1], recv_sem=recv_sems.at[frm],
                device_id=(frm // 4, (frm // 2) % 2, frm % 2), device_id_type=MESH).wait_recv()
        for cp in copies:
            cp.wait_send()
        acc = land[0]
        for s in range(1, N_DEV):
            acc = acc + land[s]
        o_ref[...] = acc

    vm = pl.BlockSpec(memory_space=pltpu.VMEM)
    return pl.pallas_call(
        body, name="allreduce_small", in_specs=[vm, ANY_SPEC], out_specs=vm,
        out_shape=jax.ShapeDtypeStruct(slab.shape, F32),
        scratch_shapes=[pltpu.VMEM((N_DEV,) + slab.shape, F32), pltpu.SemaphoreType.DMA((N_DEV - 1,)),
                        pltpu.SemaphoreType.DMA((N_DEV,))],
    )(slab, after)


BLOCK_BYTES = 3 * 1024 * 1024


def _tiles_2d(r, c, budget=BLOCK_BYTES):
    if r % 8 == 0:
        tc = c if c % LANES else _pick(c, (2048, 1408, 1024, 512, 256, 128))
        tr = 8
        for cand in (512, 256, 128, 64, 32, 16, 8):
            if r % cand == 0 and cand * tc * 4 <= budget:
                tr = cand
                break
        if tr >= 64 or c % LANES or r * LANES * 4 > budget:
            return tr, tc
    tc = LANES
    for cand in (1024, 512, 256, 128):
        if c % cand == 0 and r * cand * 4 <= budget:
            tc = cand
            break
    return r, tc


def _grid_spec(grid, in_specs, out_specs):
    return pltpu.PrefetchScalarGridSpec(num_scalar_prefetch=1, grid=grid, in_specs=in_specs, out_specs=out_specs)


def _own(axis, nr, nc):
    if axis == 0:
        return lambda i, j, where: (where[1] * nr + i, j)
    return lambda i, j, where: (i, where[1] * nc + j)


def pair_add(where, grad, recv, axis, name):
    g, jn, hr, hc = recv.shape
    tr, tc = _tiles_2d(hr, hc)
    nr, nc = hr // tr, hc // tc
    own = _own(axis, nr, nc)
    others = jn - 1

    def body(where_ref, a_ref, b_ref, o_ref):
        o_ref[...] = (a_ref[...] + b_ref[...]).astype(BF16)

    def block(a, where):
        return a // others, (where[0] + 1 + a % others) % jn

    blk = pl.BlockSpec((None, None, tr, tc), lambda a, i, j, where: block(a, where) + (i, j))
    mine = pl.BlockSpec((None, None, tr, tc), lambda a, i, j, where: block(a, where) + own(i, j, where))
    return pl.pallas_call(
        body, name=name, grid_spec=_grid_spec((g * others, nr, nc), [mine, blk], blk),
        out_shape=jax.ShapeDtypeStruct(recv.shape, BF16),
        compiler_params=_params(("parallel", "parallel", "parallel")),
    )(where, grad, recv)


def chip_add(where, grad, pair, recv, axis, name):
    g, jn, hr, hc = pair.shape
    tr, tc = _tiles_2d(hr, hc)
    nr, nc = hr // tr, hc // tc
    own = _own(axis, nr, nc)
    full = (g, 2 * hr, hc) if axis == 0 else (g, hr, 2 * hc)

    def body(where_ref, a_ref, p_ref, b_ref, o_ref):
        s = a_ref[...] + p_ref[...]
        for k in range(3):
            s = s + b_ref[k].astype(F32)
        o_ref[...] = s

    return pl.pallas_call(
        body, name=name,
        grid_spec=_grid_spec((g, nr, nc),
                             [pl.BlockSpec((None, None, tr, tc), lambda a, i, j, where: (a, where[0]) + own(i, j, where)),
                              pl.BlockSpec((None, None, tr, tc), lambda a, i, j, where: (a, where[0], i, j)),
                              pl.BlockSpec((3, None, tr, tc), lambda a, i, j, where: (0, a, i, j))],
                             pl.BlockSpec((None, tr, tc), lambda a, i, j, where: (a,) + own(i, j, where))),
        out_shape=jax.ShapeDtypeStruct(full, F32), compiler_params=_params(("parallel", "parallel", "parallel")),
    )(where, grad, pair, recv)


def _adam_math(w, g, m, v):
    m2 = ADAM_B1 * m + (1.0 - ADAM_B1) * g
    v2 = ADAM_B2 * v + (1.0 - ADAM_B2) * (g * g)
    m_hat = m2 / (1.0 - ADAM_B1 ** ADAM_STEP)
    v_hat = v2 / (1.0 - ADAM_B2 ** ADAM_STEP)
    delta = -ADAM_LR * (m_hat / (jnp.sqrt(v_hat) + ADAM_EPS) + ADAM_WD * w)
    return delta, m2, v2


def adamw(grad, idx, w, m, v, name):
    _, r, cc = w.shape
    rg = grad.shape[1]
    tr, tc = _tiles_2d(r, cc, BLOCK_BYTES // 2)
    assert rg == r or tr == r
    gr = tr if rg == r else rg

    def body(g_ref, w_ref, m_ref, v_ref, go_ref, d_ref, mo_ref, vo_ref):
        g = g_ref[pl.ds(0, tr), :]
        delta, m2, v2 = _adam_math(w_ref[...], g, m_ref[...], v_ref[...])
        go_ref[...] = g
        d_ref[...] = delta
        mo_ref[...] = m2
        vo_ref[...] = v2

    blk = pl.BlockSpec((None, tr, tc), lambda i, j: (0, i, j))
    return pl.pallas_call(
        body, name=name, grid=(r // tr, cc // tc),
        in_specs=[pl.BlockSpec((None, gr, tc), lambda i, j: (idx, i, j)), blk, blk, blk], out_specs=[blk] * 4,
        out_shape=[jax.ShapeDtypeStruct(w.shape, F32)] * 4, compiler_params=_params(("parallel", "parallel")),
    )(grad, w, m, v)


def adamw_small(g, w, m, v):
    def body(g_ref, w_ref, m_ref, v_ref, d_ref, mo_ref, vo_ref):
        delta, m2, v2 = _adam_math(w_ref[...], g_ref[...], m_ref[...], v_ref[...])
        d_ref[...] = delta
        mo_ref[...] = m2
        vo_ref[...] = v2

    return pl.pallas_call(body, name="adamw_small", out_shape=[jax.ShapeDtypeStruct(w.shape, F32)] * 3)(g, w, m, v)


GAINS = ("ffn1_pre", "ffn1_post", "mix_pre", "mix_post", "ffn2_pre", "ffn2_post", "ple_pre", "ple_post")
WEIGHTS = ("ffn1_pre_g", "ffn1_post_g", "ffn1_w_gate", "ffn1_w_up", "ffn1_w_down", "mix_pre_g", "mix_post_g",
           "mix_w_in", "fox_f_bias", "hgrn_lb_logits", "hgrn_norm_g", "mix_w_proj_fox", "mix_w_proj_hgrn",
           "mix_w_out", "ffn2_pre_g", "ffn2_post_g", "ffn2_w_gate", "ffn2_w_up", "ffn2_w_down", "ple_pre_g",
           "ple_post_g", "ple_w_gate", "ple_w_proj")
GROUPS = dict(gu1=(("ffn1_w_gate", "ffn1_w_up"), 0), down1=(("ffn1_w_down",), 0), win=(("mix_w_in",), 1),
              proj=(("mix_w_proj_fox", "mix_w_proj_hgrn"), 0), out=(("mix_w_out",), 0),
              gu2=(("ffn2_w_gate", "ffn2_w_up"), 0), down2=(("ffn2_w_down",), 0), ple_gate=(("ple_w_gate",), 0),
              ple_proj=(("ple_w_proj",), 0))
TRANSPOSED = ("mix_w_in",)
ROW_BLOCKS = ("down1", "down2", "out", "ple_gate")
GATHER_SETS = (("gate1",), ("up1",), ("down1",), ("win",), ("proj", "out"), ("gu2", "down2", "ple_gate", "ple_proj"))
GATHER_GROUPS = dict(GROUPS, gate1=(("ffn1_w_gate",), 0), up1=(("ffn1_w_up",), 0))
GATE_UP_KEYS = dict(gu1=("gate1", "up1"), gu2=("gu2",))
REDUCE_SETS = (("ple_gate", "ple_proj", "down2", "gu2"), ("out", "proj", "win"), ("down1",), ("gu1",))


def _pad_row(a, width):
    a = a.reshape(1, -1)
    return jnp.pad(a, ((0, 0), (0, width - a.shape[1])))


def _pack_small(vals):
    d = D_MODEL
    rows = [vals[n + "_g"].reshape(1, d) for n in GAINS]
    rows.append(_pad_row(vals["fox_f_bias"], d))
    lg = vals["hgrn_lb_logits"]
    rows += [_pad_row(lg[0], d), _pad_row(lg[1], d), _pad_row(vals["hgrn_norm_g"], d)]
    slab = jnp.concatenate(rows, axis=0)
    return jnp.pad(slab, ((0, SLAB_ROWS - slab.shape[0]), (0, 0)))


def _unpack_small(slab):
    out = {n + "_g": slab[i:i + 1] for i, n in enumerate(GAINS)}
    out["fox_f_bias"] = slab[8:9, :HEADS]
    out["hgrn_lb_logits"] = slab[9:11, :WIDTH]
    out["hgrn_norm_g"] = slab[11:12, :HEAD_DIM]
    return out


def _split_in(win_t):
    lo = 3 * WIDTH
    main = jnp.concatenate([win_t[:lo], win_t[lo + HEADS:]], axis=0)
    fa = jnp.pad(win_t[lo:lo + HEADS], ((0, LANES - HEADS), (0, 0)))
    return main, fa


def _join_in(main, fa):
    lo = 3 * WIDTH
    return jnp.concatenate([main[:lo], fa[:HEADS], main[lo:]], axis=0)


def _as_block(name, a):
    return jnp.swapaxes(a, 1, 2) if name in TRANSPOSED else a


def _send_block(name, a):
    return _as_block(name, a)[0].astype(BF16)


def kernel(x, p, ffn1_pre_g, ffn1_post_g, ffn1_w_gate, ffn1_w_up, ffn1_w_down, mix_pre_g, mix_post_g, mix_w_in, fox_f_bias, hgrn_lb_logits, hgrn_norm_g, mix_w_proj_fox, mix_w_proj_hgrn, mix_w_out, ffn2_pre_g, ffn2_post_g, ffn2_w_gate, ffn2_w_up, ffn2_w_down, ple_pre_g, ple_post_g, ple_w_gate, ple_w_proj, loss_target, m_ffn1_pre_g, m_ffn1_post_g, m_ffn1_w_gate, m_ffn1_w_up, m_ffn1_w_down, m_mix_pre_g, m_mix_post_g, m_mix_w_in, m_fox_f_bias, m_hgrn_lb_logits, m_hgrn_norm_g, m_mix_w_proj_fox, m_mix_w_proj_hgrn, m_mix_w_out, m_ffn2_pre_g, m_ffn2_post_g, m_ffn2_w_gate, m_ffn2_w_up, m_ffn2_w_down, m_ple_pre_g, m_ple_post_g, m_ple_w_gate, m_ple_w_proj, v_ffn1_pre_g, v_ffn1_post_g, v_ffn1_w_gate, v_ffn1_w_up, v_ffn1_w_down, v_mix_pre_g, v_mix_post_g, v_mix_w_in, v_fox_f_bias, v_hgrn_lb_logits, v_hgrn_norm_g, v_mix_w_proj_fox, v_mix_w_proj_hgrn, v_mix_w_out, v_ffn2_pre_g, v_ffn2_post_g, v_ffn2_w_gate, v_ffn2_w_up, v_ffn2_w_down, v_ple_pre_g, v_ple_post_g, v_ple_w_gate, v_ple_w_proj):
    args = dict(locals())
    wts = {n: args[n] for n in WEIGHTS}
    mom = {n: args["m_" + n] for n in WEIGHTS}
    var = {n: args["v_" + n] for n in WEIGHTS}
    d = D_MODEL
    where = jnp.stack([2 * lax.axis_index("x") + lax.axis_index("y"), lax.axis_index("c")]).astype(jnp.int32)

    order = [g for s in GATHER_SETS for g in s]
    srcs, lands, plans = [], [], []
    for s in GATHER_SETS:
        blocks = []
        for g in s:
            names, axis = GATHER_GROUPS[g]
            for pos, n in enumerate(names):
                blocks.append((len(srcs), order.index(g), pos, axis))
                srcs.append(_send_block(n, wts[n]))
            lands.append(lax.empty((len(names), N_CHIPS) + srcs[-1].shape, BF16))
        plans.append(blocks)
    sems, srcs, lands, _ = split_start(
        "gather_start", srcs, lands, [4 * len(b) for b in plans],
        lambda sr, lr: [_gather_plan(b)(sr, lr) for b in plans])
    full = {}

    def land_set(si, after):
        blocks = plans[si]
        s_idx = sorted({b[0] for b in blocks})
        l_idx = sorted({b[1] for b in blocks})
        local = [(s_idx.index(a), l_idx.index(b), g, ax) for a, b, g, ax in blocks]
        _, got = split_wait("gather_wait_%d" % si, [srcs[i] for i in s_idx], [lands[i] for i in l_idx], sems[si],
                            [after], _gather_arrivals(local))
        got = gather_pass("gather_pass_%d" % si, got, [(b, g, ax) for _, b, g, ax in local])
        for i, arr in zip(l_idx, got):
            g = order[i]
            if g == "win":
                full["win"] = arr
                full["in_main"], full["in_fa"] = _split_in(arr.reshape(-1, d))
            else:
                full[g] = arr.reshape(-1, d) if g in ROW_BLOCKS else arr

    def get_w(key, after=None):
        g = "win" if key in ("in_main", "in_fa") else key
        if g not in full:
            land_set([g in s for s in GATHER_SETS].index(True), after)
        return full[key]

    grads, pairing, started = {}, [], {}
    rows4 = lambda a: a.reshape(1, N_CHIPS, a.shape[0] // N_CHIPS, a.shape[1])

    def emit(key, grad):
        if key in ("in_main", "in_fa"):
            grads[key] = grad
            if "in_main" not in grads or "in_fa" not in grads:
                return None
            key, grad = "win", rows4(_join_in(grads["in_main"], grads["in_fa"]))
        grads[key] = grad if grad.ndim == 4 else rows4(grad)
        for si, s in enumerate(REDUCE_SETS):
            if key in s and all(g in grads for g in s):
                axes = [GROUPS[g][1] for g in s]
                own = [grads[g] for g in s]
                zones = [lax.empty(_half_shape(a.shape, ax), F32) for a, ax in zip(own, axes)]
                plan = _pair_plan(axes)
                sem, own, zones, mark = split_start("pair_start_%d" % si, own, zones, [len(s)],
                                                    lambda sr, lr: [plan(sr, lr)])
                pairing.append((si, sem[0], own, zones, axes, plan))
                return mark
        return None

    def advance(value):
        mark = None
        while pairing:
            si, sem, own, zones, axes, plan = pairing.pop(0)
            s = REDUCE_SETS[si]
            own, recv = split_wait("pair_wait_%d" % si, own, zones, sem, [value], plan)
            parts = [pair_add(where, a, r, ax, "pair_add_" + g) for g, a, r, ax in zip(s, own, recv, axes)]
            zones = [lax.empty((3, q.shape[0]) + q.shape[2:], BF16) for q in parts]
            plan = _scatter_plan(len(s))
            sem, parts, zones, mark = split_start("scatter_start_%d" % si, parts, zones, [3 * len(s)],
                                                  lambda sr, lr: [plan(sr, lr)])
            started[si] = (sem[0], parts, zones, own, recv, axes, plan)
        return mark

    gains = {n: wts[n + "_g"] for n in GAINS}
    loss, dx, small = layer_step(x[0], p[0, 0].astype(BF16), loss_target[0], gains, _pad_row(fox_f_bias, LANES),
                                 hgrn_lb_logits, hgrn_norm_g, get_w, emit, advance)

    out_g, out_d, out_m, out_v = {}, {}, {}, {}
    after, crossing = None, None

    def finish(si, sem, halves, axes, mark):
        reduced, _ = split_wait("broadcast_wait_%d" % si, halves, [], sem, [mark], _broadcast_plan(axes, True))
        last = None
        for g, red in zip(REDUCE_SETS[si], reduced):
            for idx, n in enumerate(GROUPS[g][0]):
                res = adamw(red, idx, _as_block(n, wts[n]), _as_block(n, mom[n]), _as_block(n, var[n]), "adamw_" + n)
                out_g[n], out_d[n], out_m[n], out_v[n] = [_as_block(n, r) for r in res]
                last = res[1]
        return last

    for si, s in enumerate(REDUCE_SETS):
        sem, parts, zones, own, recv, axes, plan = started[si]
        _, zones = split_wait("scatter_wait_%d" % si, parts, zones, sem, [dx, after], plan)
        halves = [chip_add(where, a, r, z, ax, "chip_add_" + g) for g, a, r, z, ax in zip(s, own, recv, zones, axes)]
        plan = _broadcast_plan(axes, False)
        sem, halves, _, mark = split_start("broadcast_start_%d" % si, halves, [], [len(s)],
                                           lambda sr, lr: [plan(sr, lr)])
        if crossing is not None:
            after = finish(*crossing, mark)
        crossing = (si, sem[0], halves, axes)
    after = finish(*crossing, None)

    small_named = {n + "_g": small[n] for n in GAINS}
    small_named.update(fox_f_bias=small["fox_bias"][:, :HEADS], hgrn_lb_logits=small["lb_logits"],
                       hgrn_norm_g=small["norm_g"])
    g_small = allreduce_small(_pack_small(small_named), after)
    d_small, m_small, v_small = adamw_small(g_small, _pack_small(wts), _pack_small(mom), _pack_small(var))

    for dst, slab in ((out_g, g_small), (out_d, d_small), (out_m, m_small), (out_v, v_small)):
        dst.update(_unpack_small(slab))

    total = lax.psum(loss[0, 0], ("x", "y", "c"))
    return (total, dx[None], *[out_g[n] for n in WEIGHTS], *[out_d[n] for n in WEIGHTS],
            *[out_m[n] for n in WEIGHTS], *[out_v[n] for n in WEIGHTS])
```

```python
import functools

import jax
import jax.numpy as jnp
from jax import lax
from jax.experimental import pallas as pl
from jax.experimental.pallas import tpu as pltpu

F32 = jnp.float32
BF16 = jnp.bfloat16

D_MODEL = 2048
SEQ = 2048
D_FF = 5632
PLE_DIM = 256
HEADS = 8
HEAD_DIM = 128
WIDTH = HEADS * HEAD_DIM
CHUNK = 64
SUB = 16
HGRN_HEADS_PER_STEP = 2
NORM_EPS = 1e-6
MACARON_SCALE = 0.5
N_CHIPS = 4

ADAM_LR = 0.001
ADAM_B1 = 0.9
ADAM_B2 = 0.999
ADAM_EPS = 1e-08
ADAM_WD = 0.01
ADAM_STEP = 10

LANES = 128
VMEM_LIMIT = 56 * 1024 * 1024
NEG_BIG = -1e30
MESH = pl.DeviceIdType.MESH


def _pick(n, cands):
    for c in cands:
        if c <= n and n % c == 0:
            return c
    return n


def _params(sem, vmem=VMEM_LIMIT):
    return pltpu.CompilerParams(dimension_semantics=sem, vmem_limit_bytes=vmem)


def _sigmoid(x):
    return 1.0 / (1.0 + jnp.exp(-x))


def _silu(x):
    return x * _sigmoid(x)


def _silu_grad(x):
    s = _sigmoid(x)
    return s * (1.0 + x * (1.0 - s))


_DN = {"nn": (((1,), (0,)), ((), ())), "nt": (((1,), (1,)), ((), ())), "tn": (((0,), (0,)), ((), ()))}


def _mm(a, b, *, mode, grid, a_spec, b_spec, o_spec, out_shape, acc_shape, name, after=(), init=None, into=None):
    nk = grid[2]
    dn = _DN[mode]
    after = [m for m in after if m is not None]
    extra = ([init] if init is not None else []) + ([into] if into is not None else []) + after
    n_extra = len(extra)

    def body(a_ref, b_ref, *rest):
        o_ref, acc_ref = rest[n_extra:]
        k = pl.program_id(2)

        @pl.when(k == 0)
        def _():
            acc_ref[...] = jnp.zeros_like(acc_ref) if init is None else rest[0][...].astype(F32)

        acc_ref[...] += lax.dot_general(a_ref[...].astype(BF16), b_ref[...].astype(BF16), dn,
                                        preferred_element_type=F32)

        @pl.when(k == nk - 1)
        def _():
            o_ref[...] = acc_ref[...].astype(o_ref.dtype)

    anywhere = pl.BlockSpec(memory_space=pl.ANY)
    return pl.pallas_call(
        body, name=name, grid=grid,
        in_specs=[a_spec, b_spec] + ([o_spec] if init is not None else []) + [anywhere] * (n_extra - (init is not None)),
        out_specs=o_spec, out_shape=out_shape, scratch_shapes=[pltpu.VMEM(acc_shape, F32)],
        input_output_aliases={} if into is None else {2 + (init is not None): 0},
        compiler_params=_params(("parallel", "parallel", "arbitrary")),
    )(a, b, *extra)


def mm_nn_2d(a, b, out_dtype, name, after=()):
    m, kk = a.shape
    n = b.shape[1]
    tm, tn = _pick(m, (512, 256, 128)), _pick(n, (1024, 512, 256, 128))
    tk = _pick(kk, (5632, 2816, 2048, 1408, 1024, 512, 256, 128))
    return _mm(a, b, mode="nn", grid=(m // tm, n // tn, kk // tk),
               a_spec=pl.BlockSpec((tm, tk), lambda i, j, k: (i, k)),
               b_spec=pl.BlockSpec((tk, tn), lambda i, j, k: (k, j)),
               o_spec=pl.BlockSpec((tm, tn), lambda i, j, k: (i, j)),
               out_shape=jax.ShapeDtypeStruct((m, n), out_dtype), acc_shape=(tm, tn), name=name, after=after)


def mm_nt_2d(a, b, out_dtype, name, after=()):
    m, c = a.shape
    n = b.shape[0]
    tm, tn, tk = _pick(m, (512, 256, 128)), _pick(n, (1408, 1024, 512, 256, 128)), _pick(c, (2048, 1408, 1024, 512, 256, 128))
    return _mm(a, b, mode="nt", grid=(m // tm, n // tn, c // tk),
               a_spec=pl.BlockSpec((tm, tk), lambda i, j, k: (i, k)),
               b_spec=pl.BlockSpec((tn, tk), lambda i, j, k: (j, k)),
               o_spec=pl.BlockSpec((tm, tn), lambda i, j, k: (i, j)),
               out_shape=jax.ShapeDtypeStruct((m, n), out_dtype), acc_shape=(tm, tn), name=name, after=after)


def mm_tn_2d(a, b, out_dtype, name):
    c, m = a.shape
    n = b.shape[1]
    tm, tn, tk = _pick(m, (1408, 1024, 512, 256, 128)), _pick(n, (1024, 512, 256, 128)), _pick(c, (2048, 1024, 512, 256, 128))
    return _mm(a, b, mode="tn", grid=(m // tm, n // tn, c // tk),
               a_spec=pl.BlockSpec((tk, tm), lambda i, j, k: (k, i)),
               b_spec=pl.BlockSpec((tk, tn), lambda i, j, k: (k, j)),
               o_spec=pl.BlockSpec((tm, tn), lambda i, j, k: (i, j)),
               out_shape=jax.ShapeDtypeStruct((m, n), out_dtype), acc_shape=(tm, tn), name=name)


def mm_nn_col(a, w, out_dtype, name, slot0=0, total=None, into=None):
    m, kk = a.shape
    g, jn, _, ns = w.shape
    total = g if total is None else total
    tm, tk = _pick(m, (512, 256, 128)), _pick(kk, (2048, 1024, 512, 256, 128))
    return _mm(a, w, mode="nn", grid=(m // tm, g * jn, kk // tk),
               a_spec=pl.BlockSpec((tm, tk), lambda i, j, k: (i, k)),
               b_spec=pl.BlockSpec((None, None, tk, ns), lambda i, j, k: (j // jn, j % jn, k, 0)),
               o_spec=pl.BlockSpec((None, tm, ns), lambda i, j, k: (slot0 + j // jn, i, j % jn)),
               out_shape=jax.ShapeDtypeStruct((total, m, jn * ns), out_dtype), acc_shape=(tm, ns), name=name,
               into=into)


def mm_nt_col(a, w, out_dtype, name, after=(), slot0=0, init=None):
    _, m, _ = a.shape
    g, jn, kk, ns = w.shape
    tm, tn = _pick(m, (512, 256, 128)), _pick(kk, (1024, 512, 256, 128))
    return _mm(a, w, mode="nt", grid=(m // tm, kk // tn, g * jn),
               a_spec=pl.BlockSpec((None, tm, ns), lambda i, j, k: (slot0 + k // jn, i, k % jn)),
               b_spec=pl.BlockSpec((None, None, tn, ns), lambda i, j, k: (k // jn, k % jn, j, 0)),
               o_spec=pl.BlockSpec((tm, tn), lambda i, j, k: (i, j)),
               out_shape=jax.ShapeDtypeStruct((m, kk), out_dtype), acc_shape=(tm, tn), name=name, after=after,
               init=init)


def mm_tn_col(a, b, jn, out_dtype, name):
    c, kk = a.shape
    g, _, n = b.shape
    ns = n // jn
    tm, tk = _pick(kk, (512, 256, 128)), _pick(c, (2048, 1024, 512, 256, 128))
    return _mm(a, b, mode="tn", grid=(kk // tm, g * jn, c // tk),
               a_spec=pl.BlockSpec((tk, tm), lambda i, j, k: (k, i)),
               b_spec=pl.BlockSpec((None, tk, ns), lambda i, j, k: (j // jn, k, j % jn)),
               o_spec=pl.BlockSpec((None, None, tm, ns), lambda i, j, k: (j // jn, j % jn, i, 0)),
               out_shape=jax.ShapeDtypeStruct((g, jn, kk, ns), out_dtype), acc_shape=(tm, ns), name=name)


def _rstd(x):
    return lax.rsqrt(jnp.mean(x * x, axis=-1, keepdims=True) + NORM_EPS)


def _rms_bwd(x, g, dy):
    r = _rstd(x)
    xn = x * r
    dyg = dy * g
    dx = r * (dyg - xn * jnp.mean(dyg * xn, axis=-1, keepdims=True))
    return dx, jnp.sum(dy * xn, axis=0, keepdims=True)


def _row_tile(t):
    return _pick(t, (256, 128, 64, 32, 16, 8))


def norm_in(h, g, name):
    t, d = h.shape
    tr = _row_tile(t)

    def body(h_ref, g_ref, u_ref):
        x = h_ref[...]
        u_ref[...] = (x * _rstd(x) * g_ref[...]).astype(BF16)

    return pl.pallas_call(
        body, name=name, grid=(t // tr,),
        in_specs=[pl.BlockSpec((tr, d), lambda i: (i, 0)), pl.BlockSpec((1, d), lambda i: (0, 0))],
        out_specs=pl.BlockSpec((tr, d), lambda i: (i, 0)),
        out_shape=jax.ShapeDtypeStruct((t, d), BF16), compiler_params=_params(("parallel",)),
    )(h, g)


def resid_post(h, y, g, scale, name):
    t, d = h.shape
    tr = _row_tile(t)

    def body(h_ref, y_ref, g_ref, o_ref):
        yv = y_ref[...]
        o_ref[...] = h_ref[...] + scale * (yv * _rstd(yv) * g_ref[...])

    row = pl.BlockSpec((tr, d), lambda i: (i, 0))
    return pl.pallas_call(
        body, name=name, grid=(t // tr,), in_specs=[row, row, pl.BlockSpec((1, d), lambda i: (0, 0))],
        out_specs=row, out_shape=jax.ShapeDtypeStruct((t, d), F32), compiler_params=_params(("parallel",)),
    )(h, y, g)


def post_bwd(dh, y, g, scale, name):
    t, d = dh.shape
    tr = _row_tile(t)

    def body(dh_ref, y_ref, g_ref, dy_ref, dg_ref):
        @pl.when(pl.program_id(0) == 0)
        def _():
            dg_ref[...] = jnp.zeros_like(dg_ref)

        dx, dg = _rms_bwd(y_ref[...], g_ref[...], scale * dh_ref[...])
        dy_ref[...] = dx.astype(BF16)
        dg_ref[...] += dg

    row = pl.BlockSpec((tr, d), lambda i: (i, 0))
    vec = pl.BlockSpec((1, d), lambda i: (0, 0))
    return pl.pallas_call(
        body, name=name, grid=(t // tr,), in_specs=[row, row, vec], out_specs=[row, vec],
        out_shape=[jax.ShapeDtypeStruct((t, d), BF16), jax.ShapeDtypeStruct((1, d), F32)],
        compiler_params=_params(("arbitrary",)),
    )(dh, y, g)


def pre_bwd(dh, h, g, dus, name, after=()):
    t, d = dh.shape
    tr = _row_tile(t)
    n_du = len(dus)
    after = [m for m in after if m is not None]

    def body(*refs):
        dh_ref, h_ref, g_ref = refs[:3]
        du_refs = refs[3:3 + n_du]
        o_ref, dg_ref = refs[3 + n_du + len(after):]

        @pl.when(pl.program_id(0) == 0)
        def _():
            dg_ref[...] = jnp.zeros_like(dg_ref)

        du = du_refs[0][...]
        for r in du_refs[1:]:
            du = du + r[...]
        dx, dg = _rms_bwd(h_ref[...], g_ref[...], du)
        o_ref[...] = dh_ref[...] + dx
        dg_ref[...] += dg

    row = pl.BlockSpec((tr, d), lambda i: (i, 0))
    vec = pl.BlockSpec((1, d), lambda i: (0, 0))
    return pl.pallas_call(
        body, name=name, grid=(t // tr,),
        in_specs=[row, row, vec] + [row] * n_du + [pl.BlockSpec(memory_space=pl.ANY)] * len(after),
        out_specs=[row, vec], out_shape=[jax.ShapeDtypeStruct((t, d), F32), jax.ShapeDtypeStruct((1, d), F32)],
        compiler_params=_params(("arbitrary",)),
    )(dh, h, g, *dus, *after)


def _ew_tiles(t, f):
    return _pick(t, (256, 128, 64, 32, 16, 8)), _pick(f, (1408, 1024, 512, 256, 128))


def swiglu_act(gu, name):
    _, t, f = gu.shape
    tr, tc = _ew_tiles(t, f)

    def body(gu_ref, o_ref):
        o_ref[...] = (_silu(gu_ref[0]) * gu_ref[1]).astype(BF16)

    return pl.pallas_call(
        body, name=name, grid=(t // tr, f // tc),
        in_specs=[pl.BlockSpec((2, tr, tc), lambda i, j: (0, i, j))],
        out_specs=pl.BlockSpec((tr, tc), lambda i, j: (i, j)),
        out_shape=jax.ShapeDtypeStruct((t, f), BF16), compiler_params=_params(("parallel", "parallel")),
    )(gu)


def swiglu_bwd(dact, gu, name):
    _, t, f = gu.shape
    tr, tc = _ew_tiles(t, f)

    def body(da_ref, gu_ref, o_ref):
        da = da_ref[...]
        gate = gu_ref[0]
        o_ref[0] = (da * gu_ref[1] * _silu_grad(gate)).astype(BF16)
        o_ref[1] = (da * _silu(gate)).astype(BF16)

    return pl.pallas_call(
        body, name=name, grid=(t // tr, f // tc),
        in_specs=[pl.BlockSpec((tr, tc), lambda i, j: (i, j)), pl.BlockSpec((2, tr, tc), lambda i, j: (0, i, j))],
        out_specs=pl.BlockSpec((2, tr, tc), lambda i, j: (0, i, j)),
        out_shape=jax.ShapeDtypeStruct((2, t, f), BF16), compiler_params=_params(("parallel", "parallel")),
    )(dact, gu)


def _col_blocks():
    w = WIDTH // LANES
    return dict(q_a=0, k_a=w, v_a=2 * w, q_b=3 * w, f_b=4 * w, i_b=5 * w, g_b=6 * w, gate_a=7 * w,
                gate_b=7 * w + D_MODEL // LANES)


def _tri(n, lower):
    r = lax.broadcasted_iota(jnp.int32, (n, n), 0)
    c = lax.broadcasted_iota(jnp.int32, (n, n), 1)
    return jnp.where((r >= c) if lower else (r <= c), 1.0, 0.0).astype(F32)


def _dot_hi(a, b):
    return jnp.dot(a, b, precision=lax.Precision.HIGHEST, preferred_element_type=F32)


def fox_prep(fa, bias, name):
    t, w = fa.shape
    tb = _pick(t, (256, 128, 64))

    def body(fa_ref, b_ref, c_ref, carry_ref):
        @pl.when(pl.program_id(0) == 0)
        def _():
            carry_ref[...] = jnp.zeros_like(carry_ref)

        z = fa_ref[...] + b_ref[...]
        lf = jnp.minimum(z, 0.0) - jnp.log(1.0 + jnp.exp(-jnp.abs(z)))
        c = _dot_hi(_tri(tb, True), lf) + carry_ref[...]
        c_ref[...] = c
        carry_ref[...] = carry_ref[...] + jnp.sum(lf, axis=0, keepdims=True)

    return pl.pallas_call(
        body, name=name, grid=(t // tb,),
        in_specs=[pl.BlockSpec((tb, w), lambda i: (i, 0)), pl.BlockSpec((1, w), lambda i: (0, 0))],
        out_specs=pl.BlockSpec((tb, w), lambda i: (i, 0)),
        out_shape=jax.ShapeDtypeStruct((t, w), F32), scratch_shapes=[pltpu.VMEM((1, w), F32)],
        compiler_params=_params(("arbitrary",)),
    )(fa, bias)


def fox_post_bwd(dc, fa, bias, name):
    t, w = fa.shape
    tb = _pick(t, (256, 128, 64))
    nb = t // tb

    def body(dc_ref, fa_ref, b_ref, dfa_ref, db_ref, carry_ref):
        @pl.when(pl.program_id(0) == 0)
        def _():
            carry_ref[...] = jnp.zeros_like(carry_ref)
            db_ref[...] = jnp.zeros_like(db_ref)

        dcv = dc_ref[...]
        dlf = _dot_hi(_tri(tb, False), dcv) + carry_ref[...]
        z = fa_ref[...] + b_ref[...]
        dz = dlf * _sigmoid(-z)
        dfa_ref[...] = dz.astype(BF16)
        db_ref[...] += jnp.sum(dz, axis=0, keepdims=True)
        carry_ref[...] = carry_ref[...] + jnp.sum(dcv, axis=0, keepdims=True)

    rev = pl.BlockSpec((tb, w), lambda i: (nb - 1 - i, 0))
    vec = pl.BlockSpec((1, w), lambda i: (0, 0))
    return pl.pallas_call(
        body, name=name, grid=(nb,), in_specs=[rev, rev, vec], out_specs=[rev, vec],
        out_shape=[jax.ShapeDtypeStruct((t, w), BF16), jax.ShapeDtypeStruct((1, w), F32)],
        scratch_shapes=[pltpu.VMEM((1, w), F32)], compiler_params=_params(("arbitrary",)),
    )(dc, fa, bias)


def _fox_probs(q_ref, k_ref, cc_ref, cr_ref, qi, tq, t):
    scale = HEAD_DIM ** -0.5
    s = lax.dot_general(q_ref[...].astype(BF16), k_ref[...].astype(BF16), _DN["nt"], preferred_element_type=F32)
    logits = s * scale + cc_ref[...] - cr_ref[...]
    qpos = qi * tq + lax.broadcasted_iota(jnp.int32, (tq, t), 0)
    kpos = lax.broadcasted_iota(jnp.int32, (tq, t), 1)
    logits = jnp.where(kpos <= qpos, logits, NEG_BIG)
    m = jnp.max(logits, axis=-1, keepdims=True)
    p = jnp.exp(logits - m)
    return p / jnp.sum(p, axis=-1, keepdims=True)


FOX_SEGMENTS = 4


def _fox_segments(t):
    tq = _pick(t, (256, 128))
    nseg = min(FOX_SEGMENTS, t // tq)
    return tq, nseg, t // tq // nseg


def _fox_specs(t, q0, kt, tq):
    cb = _col_blocks()
    dh = HEAD_DIM
    return [pl.BlockSpec((tq, dh), lambda h, i: (q0 + i, cb["q_a"] + h)),
            pl.BlockSpec((kt, dh), lambda h, i: (0, cb["k_a"] + h)),
            pl.BlockSpec((kt, dh), lambda h, i: (0, cb["v_a"] + h)),
            pl.BlockSpec((None, tq, 1), lambda h, i: (h, q0 + i, 0)),
            pl.BlockSpec((None, 1, kt), lambda h, i: (h, 0, 0))]


def fox_fwd(proj, c_col, c_row, name):
    t = proj.shape[0]
    tq, nseg, nq = _fox_segments(t)
    dh = HEAD_DIM

    def segment(out, r):
        q0, kt = r * nq, (r + 1) * nq * tq

        def body(q_ref, k_ref, v_ref, cc_ref, cr_ref, prev_ref, o_ref):
            p = _fox_probs(q_ref, k_ref, cc_ref, cr_ref, q0 + pl.program_id(1), tq, kt)
            o_ref[...] = jnp.dot(p.astype(BF16), v_ref[...].astype(BF16), preferred_element_type=F32).astype(BF16)

        return pl.pallas_call(
            body, name="%s_%d" % (name, r), grid=(HEADS, nq),
            in_specs=_fox_specs(t, q0, kt, tq) + [pl.BlockSpec(memory_space=pl.ANY)],
            out_specs=pl.BlockSpec((tq, dh), lambda h, i: (q0 + i, h)),
            out_shape=jax.ShapeDtypeStruct((t, WIDTH), BF16), input_output_aliases={5: 0},
            compiler_params=_params(("parallel", "parallel")),
        )(proj, proj, proj, c_col, c_row, out)

    out = lax.empty((t, WIDTH), BF16)
    for r in range(nseg):
        out = segment(out, r)
    return out


def fox_bwd(proj, c_col, c_row, do, name):
    t = proj.shape[0]
    tq, nseg, nq = _fox_segments(t)
    dh = HEAD_DIM
    scale = HEAD_DIM ** -0.5

    def segment(acc, r):
        q0, kt = r * nq, (r + 1) * nq * tq

        def body(q_ref, k_ref, v_ref, cc_ref, cr_ref, do_ref, dqp_ref, dkp_ref, dvp_ref, dccp_ref, dcrp_ref,
                 dq_ref, dk_ref, dv_ref, dcc_ref, dcr_ref):
            @pl.when(pl.program_id(1) == 0)
            def _():
                dk_ref[...] = dkp_ref[...]
                dv_ref[...] = dvp_ref[...]
                dcr_ref[...] = dcrp_ref[...]

            p = _fox_probs(q_ref, k_ref, cc_ref, cr_ref, q0 + pl.program_id(1), tq, kt)
            dov = do_ref[...].astype(BF16)
            kb = k_ref[...].astype(BF16)
            dv_ref[...] += lax.dot_general(p.astype(BF16), dov, _DN["tn"], preferred_element_type=F32)
            dp = lax.dot_general(dov, v_ref[...].astype(BF16), _DN["nt"], preferred_element_type=F32)
            ds = p * (dp - jnp.sum(p * dp, axis=-1, keepdims=True))
            dcc_ref[...] = jnp.sum(ds, axis=-1, keepdims=True)
            dcr_ref[...] -= jnp.sum(ds, axis=0, keepdims=True)
            dss = (ds * scale).astype(BF16)
            dq_ref[...] = jnp.dot(dss, kb, preferred_element_type=F32).astype(BF16)
            dk_ref[...] += lax.dot_general(dss, q_ref[...].astype(BF16), _DN["tn"], preferred_element_type=F32)

        rows = pl.BlockSpec((tq, dh), lambda h, i: (q0 + i, h))
        keys = pl.BlockSpec((kt, dh), lambda h, i: (0, h))
        col = pl.BlockSpec((None, tq, 1), lambda h, i: (h, q0 + i, 0))
        row = pl.BlockSpec((None, 1, kt), lambda h, i: (h, 0, 0))
        anywhere = pl.BlockSpec(memory_space=pl.ANY)
        return pl.pallas_call(
            body, name="%s_%d" % (name, r), grid=(HEADS, nq),
            in_specs=_fox_specs(t, q0, kt, tq) + [rows, anywhere, keys, keys, anywhere, row],
            out_specs=[rows, keys, keys, col, row],
            out_shape=[jax.ShapeDtypeStruct(a.shape, a.dtype) for a in acc],
            input_output_aliases={6 + k: k for k in range(5)},
            compiler_params=_params(("parallel", "arbitrary")),
        )(proj, proj, proj, c_col, c_row, do, *acc)

    acc = [lax.empty((t, WIDTH), BF16), jnp.zeros((t, WIDTH), F32), jnp.zeros((t, WIDTH), F32),
           lax.empty((HEADS, t, 1), F32), jnp.zeros((HEADS, 1, t), F32)]
    for r in range(nseg):
        acc = segment(acc, r)
    return acc


def _lower_bound(lg_ref):
    l0 = lg_ref[0:1, :]
    l1 = lg_ref[1:2, :]
    m = jnp.maximum(l0, l1)
    e0 = jnp.exp(l0 - m)
    e1 = jnp.exp(l1 - m)
    return e0 / (e0 + e1)


def _hgrn_inputs(qb_ref, fb_ref, lg_ref, q_s, k_s, cum_s):
    lb = _lower_bound(lg_ref)
    sig = _sigmoid(fb_ref[...])
    f = lb + (1.0 - lb) * sig
    q_s[...] = _silu(qb_ref[...])
    k_s[...] = 1.0 - f
    cum_s[...] = _dot_hi(_tri(CHUNK, True), jnp.log(f))
    return lb, sig, f


def _boundary(cum_s, a):
    if a == 0:
        return jnp.zeros((1, HEAD_DIM), F32)
    return cum_s[pl.ds(SUB * a - 1, 1), :]


def _hgrn_scores(q_s, k_s, cum_s):
    cum = cum_s[...]
    kk = k_s[...]
    lane = lax.broadcasted_iota(jnp.int32, (SUB, CHUNK), 1)
    row = lax.broadcasted_iota(jnp.int32, (SUB, 1), 0)
    blocks = []
    for a in range(CHUNK // SUB):
        rows = pl.ds(SUB * a, SUB)
        ca = _boundary(cum_s, a)
        cum_a = cum_s[rows, :]
        q_a = q_s[rows, :]
        qa = q_a * jnp.exp(cum_a - ca)
        ka = kk * jnp.exp(jnp.minimum(ca - cum, 0.0))
        blk = lax.dot_general(qa, ka, _DN["nt"], preferred_element_type=F32)
        blk = jnp.where(lane < SUB * a, blk, 0.0)
        for s in range(SUB):
            r = SUB * a + s
            e = jnp.exp(jnp.minimum(cum_a - cum_s[pl.ds(r, 1), :], 0.0))
            col = jnp.sum(q_a * k_s[pl.ds(r, 1), :] * e, axis=-1, keepdims=True)
            col = jnp.where(row >= s, col, 0.0)
            blk = jnp.where(lane == r, col, blk)
        blocks.append(blk)
    return jnp.concatenate(blocks, axis=0)


def hgrn_fwd(proj, lb_logits, name, after=()):
    after = [m for m in after if m is not None]
    t = proj.shape[0]
    n = t // CHUNK
    cb = _col_blocks()
    dh = HEAD_DIM

    hb = HGRN_HEADS_PER_STEP
    w = hb * dh

    def one_head(qb_ref, fb_ref, ib_ref, lg_ref, o_ref, st_ref, state, q_s, k_s, cum_s):
        _hgrn_inputs(qb_ref, fb_ref, lg_ref, q_s, k_s, cum_s)
        st = state[...]
        st_ref[...] = st
        cum = cum_s[...]
        v = ib_ref[...]
        qe = q_s[...] * jnp.exp(cum)
        inter = lax.dot_general(qe, st, _DN["nt"], preferred_element_type=F32)
        a_mat = _hgrn_scores(q_s, k_s, cum_s)
        o_ref[...] = inter + jnp.dot(a_mat, v, preferred_element_type=F32)
        last = cum_s[pl.ds(CHUNK - 1, 1), :]
        kd = k_s[...] * jnp.exp(last - cum)
        state[...] = st * jnp.exp(last) + lax.dot_general(v, kd, _DN["tn"], preferred_element_type=F32)

    def body(qb_ref, fb_ref, ib_ref, lg_ref, *rest):
        o_ref, st_ref = rest[len(after):len(after) + 2]
        scratch = rest[len(after) + 2:]

        @pl.when(pl.program_id(1) == 0)
        def _():
            for j in range(hb):
                scratch[4 * j][...] = jnp.zeros((dh, dh), F32)

        for j in range(hb):
            cols = (slice(None), pl.ds(j * dh, dh))
            one_head(qb_ref.at[cols], fb_ref.at[cols], ib_ref.at[cols], lg_ref.at[cols], o_ref.at[cols],
                     st_ref.at[j], *scratch[4 * j:4 * j + 4])

    blk = lambda off: pl.BlockSpec((CHUNK, w), lambda h, i: (i, off // hb + h))
    return pl.pallas_call(
        body, name=name, grid=(HEADS // hb, n),
        in_specs=[blk(cb["q_b"]), blk(cb["f_b"]), blk(cb["i_b"]), pl.BlockSpec((2, w), lambda h, i: (0, h))]
        + [pl.BlockSpec(memory_space=pl.ANY)] * len(after),
        out_specs=[pl.BlockSpec((CHUNK, w), lambda h, i: (i, h)),
                   pl.BlockSpec((hb, None, dh, dh), lambda h, i: (h, i, 0, 0))],
        out_shape=[jax.ShapeDtypeStruct((t, WIDTH), F32), jax.ShapeDtypeStruct((HEADS, n, dh, dh), F32)],
        scratch_shapes=([pltpu.VMEM((dh, dh), F32)] + [pltpu.VMEM((CHUNK, dh), F32)] * 3) * hb,
        compiler_params=_params(("parallel", "arbitrary")),
    )(proj, proj, proj, lb_logits, *after)


def hgrn_bwd(proj, lb_logits, states, do, name):
    t = proj.shape[0]
    n = t // CHUNK
    cb = _col_blocks()
    dh = HEAD_DIM
    nsub = CHUNK // SUB

    hb = HGRN_HEADS_PER_STEP
    w = hb * dh

    def one_head(qb_ref, fb_ref, ib_ref, lg_ref, st_ref, do_ref, dqb_ref, dfb_ref, dib_ref, dlb_ref,
                 dstate, q_s, k_s, cum_s, da_s, dq_s, dk_s):
        lb, sig, f = _hgrn_inputs(qb_ref, fb_ref, lg_ref, q_s, k_s, cum_s)
        st = st_ref[...]
        dst = dstate[...]
        cum = cum_s[...]
        q = q_s[...]
        kk = k_s[...]
        v = ib_ref[...]
        dov = do_ref[...]
        e_cum = jnp.exp(cum)
        qe = q * e_cum
        last = cum_s[pl.ds(CHUNK - 1, 1), :]
        e_last = jnp.exp(last)
        e_tail = jnp.exp(last - cum)
        kd = kk * e_tail

        a_mat = _hgrn_scores(q_s, k_s, cum_s)
        tri = _tri(CHUNK, True)
        da_s[...] = lax.dot_general(dov, v, _DN["nt"], preferred_element_type=F32) * tri
        dv = (lax.dot_general(a_mat, dov, _DN["tn"], preferred_element_type=F32)
              + lax.dot_general(kd, dst, _DN["nt"], preferred_element_type=F32))
        dk_state = jnp.dot(v, dst, preferred_element_type=F32) * e_tail
        dq_inter = jnp.dot(dov, st, preferred_element_type=F32) * e_cum
        dstate[...] = dst * e_last + lax.dot_general(dov, qe, _DN["tn"], preferred_element_type=F32)

        lane = lax.broadcasted_iota(jnp.int32, (SUB, CHUNK), 1)
        row = lax.broadcasted_iota(jnp.int32, (SUB, 1), 0)
        dk_s[...] = jnp.zeros_like(dk_s)
        for a in range(nsub):
            rows = pl.ds(SUB * a, SUB)
            ca = _boundary(cum_s, a)
            cum_a = cum_s[rows, :]
            q_a = q_s[rows, :]
            ea = jnp.exp(cum_a - ca)
            eb = jnp.exp(jnp.minimum(ca - cum, 0.0))
            da_a = da_s[rows, :]
            da_off = jnp.where(lane < SUB * a, da_a, 0.0)
            dq_a = ea * jnp.dot(da_off, kk * eb, preferred_element_type=F32)
            dk_s[...] += eb * lax.dot_general(da_off, q_a * ea, _DN["tn"], preferred_element_type=F32)
            dk_rows = jnp.zeros((SUB, dh), F32)
            for s in range(SUB):
                r = SUB * a + s
                e = jnp.exp(jnp.minimum(cum_a - cum_s[pl.ds(r, 1), :], 0.0))
                dcol = jnp.sum(jnp.where(lane == r, da_a, 0.0), axis=-1, keepdims=True)
                dcol = jnp.where(row >= s, dcol, 0.0)
                w = dcol * e
                dq_a = dq_a + w * k_s[pl.ds(r, 1), :]
                dk_rows = jnp.where(row == s, jnp.sum(w * q_a, axis=0, keepdims=True), dk_rows)
            dq_s[rows, :] = dq_a
            dk_s[rows, :] += dk_rows

        dq = dq_inter + dq_s[...]
        dk = dk_s[...] + dk_state
        d_last = (jnp.sum(dst * st, axis=0, keepdims=True) * e_last
                  + jnp.sum(kk * dk_state, axis=0, keepdims=True))
        rowc = lax.broadcasted_iota(jnp.int32, (CHUNK, 1), 0)
        dcum = q * dq - kk * dk + jnp.where(rowc == CHUNK - 1, d_last, 0.0)
        dg = _dot_hi(_tri(CHUNK, False), dcum)
        df = dg / f - dk
        dqb_ref[...] = (dq * _silu_grad(qb_ref[...])).astype(BF16)
        dfb_ref[...] = (df * (1.0 - lb) * sig * (1.0 - sig)).astype(BF16)
        dib_ref[...] = dv.astype(BF16)
        dlb_ref[...] += jnp.sum(df * (1.0 - sig), axis=0, keepdims=True)

    def body(qb_ref, fb_ref, ib_ref, lg_ref, st_ref, do_ref, dqb_ref, dfb_ref, dib_ref, dlb_ref, *scratch):
        @pl.when(pl.program_id(1) == 0)
        def _():
            for j in range(hb):
                scratch[7 * j][...] = jnp.zeros((dh, dh), F32)
            dlb_ref[...] = jnp.zeros_like(dlb_ref)

        for j in range(hb):
            cols = (slice(None), pl.ds(j * dh, dh))
            one_head(qb_ref.at[cols], fb_ref.at[cols], ib_ref.at[cols], lg_ref.at[cols], st_ref.at[j], do_ref.at[cols],
                     dqb_ref.at[cols], dfb_ref.at[cols], dib_ref.at[cols], dlb_ref.at[cols],
                     *scratch[7 * j:7 * j + 7])

    blk = lambda off: pl.BlockSpec((CHUNK, w), lambda h, i: (n - 1 - i, off // hb + h))
    out_blk = pl.BlockSpec((CHUNK, w), lambda h, i: (n - 1 - i, h))
    return pl.pallas_call(
        body, name=name, grid=(HEADS // hb, n),
        in_specs=[blk(cb["q_b"]), blk(cb["f_b"]), blk(cb["i_b"]), pl.BlockSpec((2, w), lambda h, i: (0, h)),
                  pl.BlockSpec((hb, None, dh, dh), lambda h, i: (h, n - 1 - i, 0, 0)), out_blk],
        out_specs=[out_blk, out_blk, out_blk, pl.BlockSpec((1, w), lambda h, i: (0, h))],
        out_shape=[jax.ShapeDtypeStruct((t, WIDTH), BF16)] * 3 + [jax.ShapeDtypeStruct((1, WIDTH), F32)],
        scratch_shapes=([pltpu.VMEM((dh, dh), F32)] + [pltpu.VMEM((CHUNK, dh), F32)] * 3
                        + [pltpu.VMEM((CHUNK, CHUNK), F32)] + [pltpu.VMEM((CHUNK, dh), F32)] * 2) * hb,
        compiler_params=_params(("parallel", "arbitrary")),
    )(proj, proj, proj, lb_logits, states, do)


def lb_bwd(dlb, lb_logits, name):
    def body(dlb_ref, lg_ref, o_ref):
        p0 = _lower_bound(lg_ref)
        d0 = dlb_ref[...] * p0 * (1.0 - p0)
        o_ref[0:1, :] = d0
        o_ref[1:2, :] = -d0

    return pl.pallas_call(body, name=name, out_shape=jax.ShapeDtypeStruct(lb_logits.shape, F32))(dlb, lb_logits)


def gnorm_fwd(o_raw, proj, norm_g, name, after=()):
    after = [m for m in after if m is not None]
    t = o_raw.shape[0]
    tr = _row_tile(t)
    cb = _col_blocks()
    dh = HEAD_DIM

    def body(o_ref, gb_ref, g_ref, *rest):
        x = o_ref[...]
        rest[-1][...] = (x * _rstd(x) * g_ref[...] * _silu(gb_ref[...])).astype(BF16)

    return pl.pallas_call(
        body, name=name, grid=(t // tr, HEADS),
        in_specs=[pl.BlockSpec((tr, dh), lambda i, h: (i, h)), pl.BlockSpec((tr, dh), lambda i, h: (i, cb["g_b"] + h)),
                  pl.BlockSpec((1, dh), lambda i, h: (0, 0))] + [pl.BlockSpec(memory_space=pl.ANY)] * len(after),
        out_specs=pl.BlockSpec((tr, dh), lambda i, h: (i, h)),
        out_shape=jax.ShapeDtypeStruct((t, WIDTH), BF16), compiler_params=_params(("parallel", "parallel")),
    )(o_raw, proj, norm_g, *after)


def gnorm_bwd(dy, o_raw, proj, norm_g, name):
    t = o_raw.shape[0]
    tr = _row_tile(t)
    cb = _col_blocks()
    dh = HEAD_DIM

    def body(dy_ref, o_ref, gb_ref, g_ref, do_ref, dgb_ref, dg_ref):
        @pl.when((pl.program_id(0) == 0) & (pl.program_id(1) == 0))
        def _():
            dg_ref[...] = jnp.zeros_like(dg_ref)

        x = o_ref[...]
        gb = gb_ref[...]
        dyv = dy_ref[...]
        g = g_ref[...]
        dx, dg = _rms_bwd(x, g, dyv * _silu(gb))
        do_ref[...] = dx
        dgb_ref[...] = (dyv * (x * _rstd(x) * g) * _silu_grad(gb)).astype(BF16)
        dg_ref[...] += dg

    hb = pl.BlockSpec((tr, dh), lambda i, h: (i, h))
    vec = pl.BlockSpec((1, dh), lambda i, h: (0, 0))
    return pl.pallas_call(
        body, name=name, grid=(t // tr, HEADS),
        in_specs=[hb, hb, pl.BlockSpec((tr, dh), lambda i, h: (i, cb["g_b"] + h)), vec],
        out_specs=[hb, hb, vec],
        out_shape=[jax.ShapeDtypeStruct((t, WIDTH), F32), jax.ShapeDtypeStruct((t, WIDTH), BF16),
                   jax.ShapeDtypeStruct((1, dh), F32)],
        compiler_params=_params(("arbitrary", "arbitrary")),
    )(dy, o_raw, proj, norm_g)


def merge_fwd(proj, y, name):
    _, t, d = y.shape
    tr = _row_tile(t)
    tc = _pick(d, (1024, 512, 256, 128))
    cb = _col_blocks()
    ga, gb = cb["gate_a"] * LANES // tc, cb["gate_b"] * LANES // tc

    def body(ga_ref, gb_ref, y_ref, o_ref):
        o_ref[...] = (_sigmoid(ga_ref[...]) * y_ref[0] + _sigmoid(gb_ref[...]) * y_ref[1]).astype(BF16)

    return pl.pallas_call(
        body, name=name, grid=(t // tr, d // tc),
        in_specs=[pl.BlockSpec((tr, tc), lambda i, j: (i, ga + j)), pl.BlockSpec((tr, tc), lambda i, j: (i, gb + j)),
                  pl.BlockSpec((2, tr, tc), lambda i, j: (0, i, j))],
        out_specs=pl.BlockSpec((tr, tc), lambda i, j: (i, j)),
        out_shape=jax.ShapeDtypeStruct((t, d), BF16), compiler_params=_params(("parallel", "parallel")),
    )(proj, proj, y)


def merge_bwd(dm, proj, y, name):
    _, t, d = y.shape
    tr = _row_tile(t)
    tc = _pick(d, (1024, 512, 256, 128))
    cb = _col_blocks()
    ga, gb = cb["gate_a"] * LANES // tc, cb["gate_b"] * LANES // tc

    def body(dm_ref, ga_ref, gb_ref, y_ref, dg_ref, dy_ref):
        dmv = dm_ref[...]
        for idx, g_ref in enumerate((ga_ref, gb_ref)):
            s = _sigmoid(g_ref[...])
            dg_ref[idx] = (dmv * y_ref[idx] * s * (1.0 - s)).astype(BF16)
            dy_ref[idx] = (dmv * s).astype(BF16)

    pair = pl.BlockSpec((2, tr, tc), lambda i, j: (0, i, j))
    return pl.pallas_call(
        body, name=name, grid=(t // tr, d // tc),
        in_specs=[pl.BlockSpec((tr, tc), lambda i, j: (i, j)), pl.BlockSpec((tr, tc), lambda i, j: (i, ga + j)),
                  pl.BlockSpec((tr, tc), lambda i, j: (i, gb + j)), pair],
        out_specs=[pair, pair],
        out_shape=[jax.ShapeDtypeStruct((2, t, d), BF16)] * 2, compiler_params=_params(("parallel", "parallel")),
    )(dm, proj, proj, y)


def ple_tail(h, a, b, g, target, name):
    t, d = h.shape
    tr = _row_tile(t)

    def body(h_ref, a_ref, b_ref, g_ref, t_ref, loss_ref, dh_ref, da_ref, db_ref, dg_ref):
        @pl.when(pl.program_id(0) == 0)
        def _():
            loss_ref[...] = jnp.zeros_like(loss_ref)
            dg_ref[...] = jnp.zeros_like(dg_ref)

        s = _sigmoid(a_ref[...])
        bv = b_ref[...]
        z = s * bv
        gv = g_ref[...]
        err = h_ref[...] + z * _rstd(z) * gv - t_ref[...]
        loss_ref[...] += 0.5 * jnp.sum(jnp.sum(err * err, axis=-1, keepdims=True), axis=0, keepdims=True) / d
        dh = err / d
        dh_ref[...] = dh
        dz, dg = _rms_bwd(z, gv, dh)
        da_ref[...] = (dz * bv * s * (1.0 - s)).astype(BF16)
        db_ref[...] = (dz * s).astype(BF16)
        dg_ref[...] += dg

    row = pl.BlockSpec((tr, d), lambda i: (i, 0))
    vec = pl.BlockSpec((1, d), lambda i: (0, 0))
    return pl.pallas_call(
        body, name=name, grid=(t // tr,), in_specs=[row, row, row, vec, row],
        out_specs=[pl.BlockSpec((1, 1), lambda i: (0, 0)), row, row, row, vec],
        out_shape=[jax.ShapeDtypeStruct((1, 1), F32), jax.ShapeDtypeStruct((t, d), F32),
                   jax.ShapeDtypeStruct((t, d), BF16), jax.ShapeDtypeStruct((t, d), BF16),
                   jax.ShapeDtypeStruct((1, d), F32)],
        compiler_params=_params(("arbitrary",)),
    )(h, a, b, g, target)


def _ffn_fwd(h, pre_g, post_g, get_w, idx, tag):
    u = norm_in(h, pre_g, tag + "_norm")
    gu = None
    for i, key in enumerate(GATE_UP_KEYS["gu" + idx]):
        w = get_w(key, h if gu is None else gu)
        gu = mm_nn_col(u, w, F32, "%s_gate_up_%d" % (tag, i), slot0=i, total=2, into=gu)
    act = swiglu_act(gu, tag + "_act")
    y = mm_nn_2d(act, get_w("down" + idx, gu), F32, tag + "_down")
    out = resid_post(h, y, post_g, MACARON_SCALE, tag + "_out")
    return out, (h, u, gu, act, y)


def _ffn_bwd(dh, saved, pre_g, post_g, get_w, emit, advance, idx, tag):
    h, u, gu, act, y = saved
    dy, d_post = post_bwd(dh, y, post_g, MACARON_SCALE, tag + "_post_bwd")
    m1 = emit("down" + idx, mm_tn_2d(act, dy, F32, tag + "_dw_down"))
    dact = mm_nt_2d(dy, get_w("down" + idx), F32, tag + "_dact", after=[m1])
    m2 = advance(dact)
    dgu = swiglu_bwd(dact, gu, tag + "_act_bwd")
    m3 = emit("gu" + idx, mm_tn_col(u, dgu, N_CHIPS, F32, tag + "_dw_gate_up"))
    du = None
    for i, key in enumerate(GATE_UP_KEYS["gu" + idx]):
        du = mm_nt_col(dgu, get_w(key), F32, "%s_du_%d" % (tag, i), after=[m2, m3] if du is None else (),
                       slot0=i, init=du)
    m4 = advance(du)
    dh_in, d_pre = pre_bwd(dh, h, pre_g, [du], tag + "_pre_bwd", after=[m4])
    return dh_in, d_pre, d_post


def _heads_col(a):
    t = a.shape[0]
    at = a[:, :HEADS].T
    return at.reshape(HEADS, t, 1), at.reshape(HEADS, 1, t)


def layer_step(x, p, target, gains, fox_bias, lb_logits, norm_g, get_w, emit, advance, early):
    t = x.shape[0]
    h1, s1 = _ffn_fwd(x, gains["ffn1_pre"], gains["ffn1_post"], get_w, "1", "ffn1")

    u2 = norm_in(h1, gains["mix_pre"], "mix_norm")
    proj = mm_nt_2d(u2, get_w("in_main", h1), F32, "mix_in")
    fa = mm_nt_2d(u2, get_w("in_fa"), F32, "mix_in_fa")
    mark = early("proj", proj)
    c = fox_prep(fa, fox_bias, "fox_prep")
    c_col, c_row = _heads_col(c)
    o_a = fox_fwd(proj, c_col, c_row, "fox_fwd")
    o_raw, states = hgrn_fwd(proj, lb_logits, "hgrn_fwd", after=[mark])
    mark = early("gu2", o_raw)
    o_b = gnorm_fwd(o_raw, proj, norm_g, "hgrn_norm", after=[mark])
    o_ab = jnp.stack([o_a, o_b])
    y_ab = _mm_branches(o_ab, get_w("proj", proj), "mix_proj")
    merged = merge_fwd(proj, y_ab, "mix_merge")
    mo = mm_nn_2d(merged, get_w("out"), F32, "mix_out")
    h2 = resid_post(h1, mo, gains["mix_post"], 1.0, "mix_resid")

    h3, s3 = _ffn_fwd(h2, gains["ffn2_pre"], gains["ffn2_post"], get_w, "2", "ffn2")

    u4 = norm_in(h3, gains["ple_pre"], "ple_norm")
    a4 = mm_nn_2d(u4, get_w("ple_gate"), F32, "ple_gate")
    b4 = mm_nn_col(p, get_w("ple_proj"), F32, "ple_proj")[0]
    loss, dh4, da4, db4, d_ple_post = ple_tail(h3, a4, b4, gains["ple_post"], target, "ple_tail")

    marks = [emit("ple_gate", mm_tn_2d(u4, da4, F32, "ple_dw_gate")),
             emit("ple_proj", mm_tn_col(p, db4[None], N_CHIPS, F32, "ple_dw_proj"))]
    du4 = mm_nt_2d(da4, get_w("ple_gate"), F32, "ple_du", after=marks)
    dh3, d_ple_pre = pre_bwd(dh4, h3, gains["ple_pre"], [du4], "ple_pre_bwd", after=[advance(du4)])

    dh2, d_f2_pre, d_f2_post = _ffn_bwd(dh3, s3, gains["ffn2_pre"], gains["ffn2_post"], get_w, emit, advance,
                                        "2", "ffn2")

    dmo, d_mix_post = post_bwd(dh2, mo, gains["mix_post"], 1.0, "mix_post_bwd")
    marks = [emit("out", mm_tn_2d(merged, dmo, F32, "mix_dw_out"))]
    dmerged = mm_nt_2d(dmo, get_w("out"), F32, "mix_dmerged", after=marks)
    dgate, dy_ab = merge_bwd(dmerged, proj, y_ab, "mix_merge_bwd")
    marks = [advance(dmerged), emit("proj", _mm_branches_dw(o_ab, dy_ab, "mix_dw_proj"))]
    do_ab = _mm_branches_bwd(dy_ab, get_w("proj"), "mix_do")
    do_raw, dg_b, d_norm_g = gnorm_bwd(do_ab[1], o_raw, proj, norm_g, "hgrn_norm_bwd")
    dq_b, df_b, di_b, dlb = hgrn_bwd(proj, lb_logits, states, do_raw, "hgrn_bwd")
    d_lb_logits = lb_bwd(dlb, lb_logits, "lb_bwd")
    dq_a, dk_a, dv_a, dc_col, dc_row = fox_bwd(proj, c_col, c_row, do_ab[0], "fox_bwd")
    dc = (dc_col.reshape(HEADS, t) + dc_row.reshape(HEADS, t)).T
    dc = jnp.pad(dc, ((0, 0), (0, LANES - HEADS)))
    dfa, d_fox_bias = fox_post_bwd(dc, fa, fox_bias, "fox_post_bwd")
    dproj = jnp.concatenate([dq_a, dk_a.astype(BF16), dv_a.astype(BF16), dq_b, df_b, di_b, dg_b,
                             dgate[0], dgate[1]], axis=1)
    marks.append(emit("in_main", mm_tn_2d(dproj, u2, F32, "mix_dw_in")))
    marks.append(emit("in_fa", mm_tn_2d(dfa, u2, F32, "mix_dw_in_fa")))
    du2a = mm_nn_2d(dproj, get_w("in_main"), F32, "mix_du", after=marks)
    du2b = mm_nn_2d(dfa, get_w("in_fa"), F32, "mix_du_fa")
    dh1, d_mix_pre = pre_bwd(dh2, h1, gains["mix_pre"], [du2a, du2b], "mix_pre_bwd", after=[advance(du2a)])

    dx, d_f1_pre, d_f1_post = _ffn_bwd(dh1, s1, gains["ffn1_pre"], gains["ffn1_post"], get_w, emit, advance,
                                       "1", "ffn1")

    small = dict(ffn1_pre=d_f1_pre, ffn1_post=d_f1_post, mix_pre=d_mix_pre, mix_post=d_mix_post,
                 ffn2_pre=d_f2_pre, ffn2_post=d_f2_post, ple_pre=d_ple_pre, ple_post=d_ple_post,
                 fox_bias=d_fox_bias, lb_logits=d_lb_logits, norm_g=d_norm_g)
    return loss, dx, small


def _mm_branches(o_ab, w_proj, name):
    g, t, kk = o_ab.shape
    _, jn, _, ns = w_proj.shape
    tm = _pick(t, (512, 256, 128))
    return _mm(o_ab, w_proj, mode="nn", grid=(t // tm, g * jn, 1),
               a_spec=pl.BlockSpec((None, tm, kk), lambda i, j, k: (j // jn, i, 0)),
               b_spec=pl.BlockSpec((None, None, kk, ns), lambda i, j, k: (j // jn, j % jn, 0, 0)),
               o_spec=pl.BlockSpec((None, tm, ns), lambda i, j, k: (j // jn, i, j % jn)),
               out_shape=jax.ShapeDtypeStruct((g, t, jn * ns), F32), acc_shape=(tm, ns), name=name)


def _mm_branches_bwd(dy_ab, w_proj, name):
    g, t, _ = dy_ab.shape
    _, jn, kk, ns = w_proj.shape
    tm = _pick(t, (512, 256, 128))
    return _mm(dy_ab, w_proj, mode="nt", grid=(t // tm, g, jn),
               a_spec=pl.BlockSpec((None, tm, ns), lambda i, j, k: (j, i, k)),
               b_spec=pl.BlockSpec((None, None, kk, ns), lambda i, j, k: (j, k, 0, 0)),
               o_spec=pl.BlockSpec((None, tm, kk), lambda i, j, k: (j, i, 0)),
               out_shape=jax.ShapeDtypeStruct((g, t, kk), F32), acc_shape=(tm, kk), name=name)


def _mm_branches_dw(o_ab, dy_ab, name):
    g, t, kk = o_ab.shape
    d = dy_ab.shape[2]
    jn = N_CHIPS
    ns = d // jn
    return _mm(o_ab, dy_ab, mode="tn", grid=(1, g * jn, 1),
               a_spec=pl.BlockSpec((None, t, kk), lambda i, j, k: (j // jn, 0, 0)),
               b_spec=pl.BlockSpec((None, t, ns), lambda i, j, k: (j // jn, 0, j % jn)),
               o_spec=pl.BlockSpec((None, None, kk, ns), lambda i, j, k: (j // jn, j % jn, 0, 0)),
               out_shape=jax.ShapeDtypeStruct((g, jn, kk, ns), F32), acc_shape=(kk, ns), name=name)


HBM_SPEC = pl.BlockSpec(memory_space=pltpu.HBM)
SEM_SPEC = pl.BlockSpec(memory_space=pltpu.SEMAPHORE)
ANY_SPEC = pl.BlockSpec(memory_space=pl.ANY)
EFFECT = pltpu.SideEffectType.DATAFLOW_SIDE_EFFECTING


def _in_hbm(a):
    return pltpu.with_memory_space_constraint(a, pltpu.HBM)


def _place():
    x, y, c = lax.axis_index("x"), lax.axis_index("y"), lax.axis_index("c")
    chips = [(1 - x, y), (x, 1 - y), (1 - x, 1 - y)]
    return x, y, c, chips


def _half(shape, which, axis):
    n = shape[-2 + axis] // 2
    cut = pl.ds(which * n, n)
    return (cut, slice(None)) if axis == 0 else (slice(None), cut)


def _half_shape(shape, axis):
    s = list(shape)
    s[len(s) - 2 + axis] //= 2
    return tuple(s)


def _remote(src, dst, send_sems, recv_sems, k, to):
    return pltpu.make_async_remote_copy(src_ref=src, dst_ref=dst, send_sem=send_sems.at[k], recv_sem=recv_sems.at[k],
                                        device_id=to, device_id_type=MESH)


def split_start(name, srcs, lands, counts, copies):
    ns, nl, nset = len(srcs), len(lands), len(counts)

    def body(*refs):
        src_refs, land_refs = refs[:ns], refs[ns:ns + nl]
        sems = refs[ns + nl:ns + nl + 2 * nset]
        for s, plan in enumerate(copies(src_refs, land_refs)):
            for k, (src, dst, to) in enumerate(plan):
                _remote(src, dst, sems[2 * s], sems[2 * s + 1], k, to).start()
        refs[-1][...] = jnp.zeros_like(refs[-1])

    out_shape = []
    for n in counts:
        out_shape += [pltpu.SemaphoreType.DMA((n,)), pltpu.SemaphoreType.DMA((n,))]
    out_shape += [pltpu.HBM(a.shape, a.dtype) for a in list(srcs) + list(lands)]
    out_shape.append(jax.ShapeDtypeStruct((8, LANES), F32))
    res = pl.pallas_call(
        body, name=name, out_shape=tuple(out_shape), in_specs=[HBM_SPEC] * (ns + nl),
        out_specs=tuple([SEM_SPEC] * (2 * nset) + [HBM_SPEC] * (ns + nl) + [pl.BlockSpec(memory_space=pltpu.VMEM)]),
        input_output_aliases={i: 2 * nset + i for i in range(ns + nl)},
        compiler_params=pltpu.CompilerParams(has_side_effects=EFFECT),
    )(*[_in_hbm(a) for a in list(srcs) + list(lands)])
    sems = [(res[2 * s], res[2 * s + 1]) for s in range(nset)]
    return sems, list(res[2 * nset:2 * nset + ns]), list(res[2 * nset + ns:-1]), res[-1]


def split_wait(name, srcs, lands, sems, afters, copies):
    afters = [a for a in afters if a is not None]
    ns, nl, na = len(srcs), len(lands), len(afters)

    def body(*refs):
        src_refs, land_refs = refs[:ns], refs[ns:ns + nl]
        send_sems, recv_sems = refs[ns + nl:ns + nl + 2]
        for k, (src, dst, to) in enumerate(copies(src_refs, land_refs)):
            cp = _remote(src, dst, send_sems, recv_sems, k, to)
            cp.wait_send()
            cp.wait_recv()

    res = pl.pallas_call(
        body, name=name, out_shape=tuple(pltpu.HBM(a.shape, a.dtype) for a in list(srcs) + list(lands)),
        in_specs=[HBM_SPEC] * (ns + nl) + [SEM_SPEC, SEM_SPEC] + [ANY_SPEC] * na,
        out_specs=tuple([HBM_SPEC] * (ns + nl)), input_output_aliases={i: i for i in range(ns + nl)},
        compiler_params=pltpu.CompilerParams(has_side_effects=EFFECT),
    )(*srcs, *lands, sems[0], sems[1], *afters)
    return list(res[:ns]), list(res[ns:])


def _gather_plan(blocks):
    def copies(src_refs, land_refs):
        x, y, c, chips = _place()
        j_me = 2 * x + y
        plan = []
        for si, li, g, axis in blocks:
            src, land = src_refs[si], land_refs[li].at[g]
            mine = _half(src.shape, c, axis)
            for px, py in chips:
                plan.append((src.at[mine], land.at[(j_me,) + mine], (px, py, c)))
            plan.append((src, land.at[j_me], (x, y, 1 - c)))
        return plan
    return copies


def _gather_arrivals(blocks):
    def copies(src_refs, land_refs):
        x, y, c, chips = _place()
        j_me = 2 * x + y
        plan = []
        for si, li, g, axis in blocks:
            src, land = src_refs[si], land_refs[li].at[g]
            mine = _half(src.shape, c, axis)
            for px, py in chips:
                plan.append((src.at[mine], land.at[(2 * px + py,) + mine], (px, py, c)))
            plan.append((src, land.at[j_me], (x, y, 1 - c)))
        return plan
    return copies


def _pass_plan(blocks, arrivals):
    def copies(src_refs, land_refs):
        x, y, c, chips = _place()
        plan = []
        for li, g, axis in blocks:
            land = src_refs[li].at[g]
            half = _half(land.shape[1:], (1 - c) if arrivals else c, axis)
            for px, py in chips:
                part = land.at[(2 * px + py,) + half]
                plan.append((part, part, (x, y, 1 - c)))
        return plan
    return copies


def gather_pass(name, lands, blocks):
    n = len(lands)

    def body(*refs):
        outs = refs[n:2 * n]
        send_sems, recv_sems = refs[2 * n:]
        x, y, c, chips = _place()
        sent = []
        for i, (li, g, axis) in enumerate(blocks):
            land = outs[li].at[g]
            mine = _half(land.shape[1:], c, axis)
            for k, (px, py) in enumerate(chips):
                part = land.at[(2 * px + py,) + mine]
                cp = _remote(part, part, send_sems, recv_sems, 3 * i + k, (x, y, 1 - c))
                cp.start()
                sent.append(cp)
        for i, (li, g, axis) in enumerate(blocks):
            land = outs[li].at[g]
            other = _half(land.shape[1:], 1 - c, axis)
            for k, (px, py) in enumerate(chips):
                part = land.at[(2 * px + py,) + other]
                _remote(part, part, send_sems, recv_sems, 3 * i + k, (x, y, 1 - c)).wait_recv()
        for cp in sent:
            cp.wait_send()

    m = 3 * len(blocks)
    return pl.pallas_call(
        body, name=name, in_specs=[ANY_SPEC] * n, out_specs=[ANY_SPEC] * n,
        out_shape=[jax.ShapeDtypeStruct(a.shape, a.dtype) for a in lands],
        input_output_aliases={i: i for i in range(n)},
        scratch_shapes=[pltpu.SemaphoreType.DMA((m,)), pltpu.SemaphoreType.DMA((m,))],
    )(*lands)


def _pair_plan(axes):
    def copies(src_refs, land_refs):
        x, y, c, _ = _place()
        return [(src_refs[i].at[(slice(None), slice(None)) + _half(src_refs[i].shape, 1 - c, a)], land_refs[i],
                 (x, y, 1 - c)) for i, a in enumerate(axes)]
    return copies


def _scatter_plan(n):
    def copies(src_refs, land_refs):
        x, y, c, chips = _place()
        return [(src_refs[i].at[:, 2 * px + py], land_refs[i].at[k], (px, py, c))
                for i in range(n) for k, (px, py) in enumerate(chips)]
    return copies


def _broadcast_plan(axes, arrivals):
    def copies(src_refs, land_refs):
        x, y, c, _ = _place()
        plan = []
        for i, a in enumerate(axes):
            part = src_refs[i].at[(slice(None),) + _half(src_refs[i].shape, (1 - c) if arrivals else c, a)]
            plan.append((part, part, (x, y, 1 - c)))
        return plan
    return copies


N_DEV = 8
SLAB_ROWS = 16


def allreduce_small(slab, after):
    def body(x_ref, after_ref, o_ref, land, send_sems, recv_sems):
        x, y, c, _ = _place()
        me = 4 * x + 2 * y + c
        land[me] = x_ref[...]
        copies = []
        for d in range(1, N_DEV):
            to = (me + d) % N_DEV
            cp = pltpu.make_async_remote_copy(
                src_ref=x_ref, dst_ref=land.at[me], send_sem=send_sems.at[d - 1], recv_sem=recv_sems.at[me],
                device_id=(to // 4, (to // 2) % 2, to % 2), device_id_type=MESH)
            cp.start()
            copies.append(cp)
        for d in range(1, N_DEV):
            frm = (me + d) % N_DEV
            pltpu.make_async_remote_copy(
                src_ref=x_ref, dst_ref=land.at[frm], send_sem=send_sems.at[d - 1], recv_sem=recv_sems.at[frm],
                device_id=(frm // 4, (frm // 2) % 2, frm % 2), device_id_type=MESH).wait_recv()
        for cp in copies:
            cp.wait_send()
        acc = land[0]
        for s in range(1, N_DEV):
            acc = acc + land[s]
        o_ref[...] = acc

    vm = pl.BlockSpec(memory_space=pltpu.VMEM)
    return pl.pallas_call(
        body, name="allreduce_small", in_specs=[vm, ANY_SPEC], out_specs=vm,
        out_shape=jax.ShapeDtypeStruct(slab.shape, F32),
        scratch_shapes=[pltpu.VMEM((N_DEV,) + slab.shape, F32), pltpu.SemaphoreType.DMA((N_DEV - 1,)),
                        pltpu.SemaphoreType.DMA((N_DEV,))],
    )(slab, after)


BLOCK_BYTES = 3 * 1024 * 1024


def _tiles_2d(r, c, budget=BLOCK_BYTES):
    if r % 8 == 0:
        tc = c if c % LANES else _pick(c, (2048, 1408, 1024, 512, 256, 128))
        tr = 8
        for cand in (512, 256, 128, 64, 32, 16, 8):
            if r % cand == 0 and cand * tc * 4 <= budget:
                tr = cand
                break
        if tr >= 64 or c % LANES or r * LANES * 4 > budget:
            return tr, tc
    tc = LANES
    for cand in (1024, 512, 256, 128):
        if c % cand == 0 and r * cand * 4 <= budget:
            tc = cand
            break
    return r, tc


def _grid_spec(grid, in_specs, out_specs):
    return pltpu.PrefetchScalarGridSpec(num_scalar_prefetch=1, grid=grid, in_specs=in_specs, out_specs=out_specs)


def _own(axis, nr, nc):
    if axis == 0:
        return lambda i, j, where: (where[1] * nr + i, j)
    return lambda i, j, where: (i, where[1] * nc + j)


def pair_add(where, grad, recv, axis, name):
    g, jn, hr, hc = recv.shape
    tr, tc = _tiles_2d(hr, hc)
    nr, nc = hr // tr, hc // tc
    own = _own(axis, nr, nc)
    others = jn - 1

    def body(where_ref, a_ref, b_ref, o_ref):
        o_ref[...] = (a_ref[...] + b_ref[...]).astype(BF16)

    def block(a, where):
        return a // others, (where[0] + 1 + a % others) % jn

    blk = pl.BlockSpec((None, None, tr, tc), lambda a, i, j, where: block(a, where) + (i, j))
    mine = pl.BlockSpec((None, None, tr, tc), lambda a, i, j, where: block(a, where) + own(i, j, where))
    return pl.pallas_call(
        body, name=name, grid_spec=_grid_spec((g * others, nr, nc), [mine, blk], blk),
        out_shape=jax.ShapeDtypeStruct(recv.shape, BF16),
        compiler_params=_params(("parallel", "parallel", "parallel")),
    )(where, grad, recv)


def chip_add(where, grad, pair, recv, axis, name):
    g, jn, hr, hc = pair.shape
    tr, tc = _tiles_2d(hr, hc)
    nr, nc = hr // tr, hc // tc
    own = _own(axis, nr, nc)
    full = (g, 2 * hr, hc) if axis == 0 else (g, hr, 2 * hc)

    def body(where_ref, a_ref, p_ref, b_ref, o_ref):
        s = a_ref[...] + p_ref[...]
        for k in range(3):
            s = s + b_ref[k].astype(F32)
        o_ref[...] = s

    return pl.pallas_call(
        body, name=name,
        grid_spec=_grid_spec((g, nr, nc),
                             [pl.BlockSpec((None, None, tr, tc), lambda a, i, j, where: (a, where[0]) + own(i, j, where)),
                              pl.BlockSpec((None, None, tr, tc), lambda a, i, j, where: (a, where[0], i, j)),
                              pl.BlockSpec((3, None, tr, tc), lambda a, i, j, where: (0, a, i, j))],
                             pl.BlockSpec((None, tr, tc), lambda a, i, j, where: (a,) + own(i, j, where))),
        out_shape=jax.ShapeDtypeStruct(full, F32), compiler_params=_params(("parallel", "parallel", "parallel")),
    )(where, grad, pair, recv)


def _adam_math(w, g, m, v):
    m2 = ADAM_B1 * m + (1.0 - ADAM_B1) * g
    v2 = ADAM_B2 * v + (1.0 - ADAM_B2) * (g * g)
    m_hat = m2 / (1.0 - ADAM_B1 ** ADAM_STEP)
    v_hat = v2 / (1.0 - ADAM_B2 ** ADAM_STEP)
    delta = -ADAM_LR * (m_hat / (jnp.sqrt(v_hat) + ADAM_EPS) + ADAM_WD * w)
    return delta, m2, v2


def adamw(grad, idx, w, m, v, name):
    _, r, cc = w.shape
    rg = grad.shape[1]
    tr, tc = _tiles_2d(r, cc, BLOCK_BYTES // 2)
    assert rg == r or tr == r
    gr = tr if rg == r else rg

    def body(g_ref, w_ref, m_ref, v_ref, go_ref, d_ref, mo_ref, vo_ref):
        g = g_ref[pl.ds(0, tr), :]
        delta, m2, v2 = _adam_math(w_ref[...], g, m_ref[...], v_ref[...])
        go_ref[...] = g
        d_ref[...] = delta
        mo_ref[...] = m2
        vo_ref[...] = v2

    blk = pl.BlockSpec((None, tr, tc), lambda i, j: (0, i, j))
    return pl.pallas_call(
        body, name=name, grid=(r // tr, cc // tc),
        in_specs=[pl.BlockSpec((None, gr, tc), lambda i, j: (idx, i, j)), blk, blk, blk], out_specs=[blk] * 4,
        out_shape=[jax.ShapeDtypeStruct(w.shape, F32)] * 4, compiler_params=_params(("parallel", "parallel")),
    )(grad, w, m, v)


def adamw_small(g, w, m, v):
    def body(g_ref, w_ref, m_ref, v_ref, d_ref, mo_ref, vo_ref):
        delta, m2, v2 = _adam_math(w_ref[...], g_ref[...], m_ref[...], v_ref[...])
        d_ref[...] = delta
        mo_ref[...] = m2
        vo_ref[...] = v2

    return pl.pallas_call(body, name="adamw_small", out_shape=[jax.ShapeDtypeStruct(w.shape, F32)] * 3)(g, w, m, v)


GAINS = ("ffn1_pre", "ffn1_post", "mix_pre", "mix_post", "ffn2_pre", "ffn2_post", "ple_pre", "ple_post")
WEIGHTS = ("ffn1_pre_g", "ffn1_post_g", "ffn1_w_gate", "ffn1_w_up", "ffn1_w_down", "mix_pre_g", "mix_post_g",
           "mix_w_in", "fox_f_bias", "hgrn_lb_logits", "hgrn_norm_g", "mix_w_proj_fox", "mix_w_proj_hgrn",
           "mix_w_out", "ffn2_pre_g", "ffn2_post_g", "ffn2_w_gate", "ffn2_w_up", "ffn2_w_down", "ple_pre_g",
           "ple_post_g", "ple_w_gate", "ple_w_proj")
GROUPS = dict(gu1=(("ffn1_w_gate", "ffn1_w_up"), 0), down1=(("ffn1_w_down",), 0), win=(("mix_w_in",), 1),
              proj=(("mix_w_proj_fox", "mix_w_proj_hgrn"), 0), out=(("mix_w_out",), 0),
              gu2=(("ffn2_w_gate", "ffn2_w_up"), 0), down2=(("ffn2_w_down",), 0), ple_gate=(("ple_w_gate",), 0),
              ple_proj=(("ple_w_proj",), 0))
TRANSPOSED = ("mix_w_in",)
ROW_BLOCKS = ("down1", "down2", "out", "ple_gate")
GATHER_SETS = (("gate1",), ("up1",), ("down1",), ("win",), ("proj", "out"), ("gu2", "down2", "ple_gate", "ple_proj"))
GATHER_GROUPS = dict(GROUPS, gate1=(("ffn1_w_gate",), 0), up1=(("ffn1_w_up",), 0))
GATE_UP_KEYS = dict(gu1=("gate1", "up1"), gu2=("gu2",))
REDUCE_SETS = (("ple_gate", "ple_proj", "down2", "gu2"), ("out", "proj", "win"), ("down1",), ("gu1",))


def _pad_row(a, width):
    a = a.reshape(1, -1)
    return jnp.pad(a, ((0, 0), (0, width - a.shape[1])))


def _pack_small(vals):
    d = D_MODEL
    rows = [vals[n + "_g"].reshape(1, d) for n in GAINS]
    rows.append(_pad_row(vals["fox_f_bias"], d))
    lg = vals["hgrn_lb_logits"]
    rows += [_pad_row(lg[0], d), _pad_row(lg[1], d), _pad_row(vals["hgrn_norm_g"], d)]
    slab = jnp.concatenate(rows, axis=0)
    return jnp.pad(slab, ((0, SLAB_ROWS - slab.shape[0]), (0, 0)))


def _unpack_small(slab):
    out = {n + "_g": slab[i:i + 1] for i, n in enumerate(GAINS)}
    out["fox_f_bias"] = slab[8:9, :HEADS]
    out["hgrn_lb_logits"] = slab[9:11, :WIDTH]
    out["hgrn_norm_g"] = slab[11:12, :HEAD_DIM]
    return out


def _split_in(win_t):
    lo = 3 * WIDTH
    main = jnp.concatenate([win_t[:lo], win_t[lo + HEADS:]], axis=0)
    fa = jnp.pad(win_t[lo:lo + HEADS], ((0, LANES - HEADS), (0, 0)))
    return main, fa


def _join_in(main, fa):
    lo = 3 * WIDTH
    return jnp.concatenate([main[:lo], fa[:HEADS], main[lo:]], axis=0)


def _as_block(name, a):
    return jnp.swapaxes(a, 1, 2) if name in TRANSPOSED else a


def _send_block(name, a, mark):
    blk = _as_block(name, a)[0]
    if mark is not None:
        blk = blk + mark[0, 0]
    return blk.astype(BF16)


def kernel(x, p, ffn1_pre_g, ffn1_post_g, ffn1_w_gate, ffn1_w_up, ffn1_w_down, mix_pre_g, mix_post_g, mix_w_in, fox_f_bias, hgrn_lb_logits, hgrn_norm_g, mix_w_proj_fox, mix_w_proj_hgrn, mix_w_out, ffn2_pre_g, ffn2_post_g, ffn2_w_gate, ffn2_w_up, ffn2_w_down, ple_pre_g, ple_post_g, ple_w_gate, ple_w_proj, loss_target, m_ffn1_pre_g, m_ffn1_post_g, m_ffn1_w_gate, m_ffn1_w_up, m_ffn1_w_down, m_mix_pre_g, m_mix_post_g, m_mix_w_in, m_fox_f_bias, m_hgrn_lb_logits, m_hgrn_norm_g, m_mix_w_proj_fox, m_mix_w_proj_hgrn, m_mix_w_out, m_ffn2_pre_g, m_ffn2_post_g, m_ffn2_w_gate, m_ffn2_w_up, m_ffn2_w_down, m_ple_pre_g, m_ple_post_g, m_ple_w_gate, m_ple_w_proj, v_ffn1_pre_g, v_ffn1_post_g, v_ffn1_w_gate, v_ffn1_w_up, v_ffn1_w_down, v_mix_pre_g, v_mix_post_g, v_mix_w_in, v_fox_f_bias, v_hgrn_lb_logits, v_hgrn_norm_g, v_mix_w_proj_fox, v_mix_w_proj_hgrn, v_mix_w_out, v_ffn2_pre_g, v_ffn2_post_g, v_ffn2_w_gate, v_ffn2_w_up, v_ffn2_w_down, v_ple_pre_g, v_ple_post_g, v_ple_w_gate, v_ple_w_proj):
    args = dict(locals())
    wts = {n: args[n] for n in WEIGHTS}
    mom = {n: args["m_" + n] for n in WEIGHTS}
    var = {n: args["v_" + n] for n in WEIGHTS}
    d = D_MODEL
    where = jnp.stack([2 * lax.axis_index("x") + lax.axis_index("y"), lax.axis_index("c")]).astype(jnp.int32)

    def start_sets(name, which, mark):
        srcs, lands, plans = [], [], []
        for si in which:
            blocks = []
            for g in GATHER_SETS[si]:
                names, axis = GATHER_GROUPS[g]
                for pos, n in enumerate(names):
                    blocks.append((len(srcs), len(lands), pos, axis))
                    srcs.append(_send_block(n, wts[n], mark))
                lands.append(lax.empty((len(names), N_CHIPS) + srcs[-1].shape, BF16))
            plans.append(blocks)
        sems, srcs, lands, mark = split_start(name, srcs, lands, [4 * len(b) for b in plans],
                                              lambda sr, lr: [_gather_plan(b)(sr, lr) for b in plans])
        out = {}
        for k, si in enumerate(which):
            s_idx = sorted({b[0] for b in plans[k]})
            l_idx = sorted({b[1] for b in plans[k]})
            local = [(s_idx.index(a), l_idx.index(b), pos, ax) for a, b, pos, ax in plans[k]]
            out[si] = (sems[k], [srcs[i] for i in s_idx], [lands[i] for i in l_idx], local)
        return out, mark

    flying, mark = start_sets("gather_start_0", [0], None)
    rest, _ = start_sets("gather_start_1", list(range(1, len(GATHER_SETS))), mark)
    flying.update(rest)
    full, passing = {}, {}

    def set_of(key):
        g = "win" if key in ("in_main", "in_fa") else key
        return g, [g in s for s in GATHER_SETS].index(True)

    def arrive(si, after):
        sem, srcs, lands, local = flying.pop(si)
        _, got = split_wait("gather_wait_%d" % si, srcs, lands, sem, [after], _gather_arrivals(local))
        return got, [(b, pos, ax) for _, b, pos, ax in local]

    def early(key, after):
        g, si = set_of(key)
        if g in full or si not in flying:
            return None
        got, blocks = arrive(si, after)
        sem, got, _, mark = split_start("pass_start_%d" % si, got, [], [3 * len(blocks)],
                                        lambda sr, lr: [_pass_plan(blocks, False)(sr, lr)])
        passing[si] = (sem[0], got, blocks)
        return mark

    def land_set(si, after):
        if si in passing:
            sem, got, blocks = passing.pop(si)
            got, _ = split_wait("pass_wait_%d" % si, got, [], sem, [after], _pass_plan(blocks, True))
        else:
            got, blocks = arrive(si, after)
            got = gather_pass("gather_pass_%d" % si, got, blocks)
        for g, arr in zip(GATHER_SETS[si], got):
            if g == "win":
                full["win"] = arr
                full["in_main"], full["in_fa"] = _split_in(arr.reshape(-1, d))
            else:
                full[g] = arr.reshape(-1, d) if g in ROW_BLOCKS else arr

    def get_w(key, after=None):
        g, si = set_of(key)
        if g not in full:
            land_set(si, after)
        return full[key]

    grads, pairing, started = {}, [], {}
    rows4 = lambda a: a.reshape(1, N_CHIPS, a.shape[0] // N_CHIPS, a.shape[1])

    def emit(key, grad):
        if key in ("in_main", "in_fa"):
            grads[key] = grad
            if "in_main" not in grads or "in_fa" not in grads:
                return None
            key, grad = "win", rows4(_join_in(grads["in_main"], grads["in_fa"]))
        grads[key] = grad if grad.ndim == 4 else rows4(grad)
        for si, s in enumerate(REDUCE_SETS):
            if key in s and all(g in grads for g in s):
                axes = [GROUPS[g][1] for g in s]
                own = [grads[g] for g in s]
                zones = [lax.empty(_half_shape(a.shape, ax), F32) for a, ax in zip(own, axes)]
                plan = _pair_plan(axes)
                sem, own, zones, mark = split_start("pair_start_%d" % si, own, zones, [len(s)],
                                                    lambda sr, lr: [plan(sr, lr)])
                pairing.append((si, sem[0], own, zones, axes, plan))
                return mark
        return None

    def advance(value):
        mark = None
        while pairing:
            si, sem, own, zones, axes, plan = pairing.pop(0)
            s = REDUCE_SETS[si]
            own, recv = split_wait("pair_wait_%d" % si, own, zones, sem, [value], plan)
            parts = [pair_add(where, a, r, ax, "pair_add_" + g) for g, a, r, ax in zip(s, own, recv, axes)]
            zones = [lax.empty((3, q.shape[0]) + q.shape[2:], BF16) for q in parts]
            plan = _scatter_plan(len(s))
            sem, parts, zones, mark = split_start("scatter_start_%d" % si, parts, zones, [3 * len(s)],
                                                  lambda sr, lr: [plan(sr, lr)])
            started[si] = (sem[0], parts, zones, own, recv, axes, plan)
        return mark

    gains = {n: wts[n + "_g"] for n in GAINS}
    loss, dx, small = layer_step(x[0], p[0, 0].astype(BF16), loss_target[0], gains, _pad_row(fox_f_bias, LANES),
                                 hgrn_lb_logits, hgrn_norm_g, get_w, emit, advance, early)

    out_g, out_d, out_m, out_v = {}, {}, {}, {}
    after, crossing = None, None

    def finish(si, sem, halves, axes, mark):
        reduced, _ = split_wait("broadcast_wait_%d" % si, halves, [], sem, [mark], _broadcast_plan(axes, True))
        last = None
        for g, red in zip(REDUCE_SETS[si], reduced):
            for idx, n in enumerate(GROUPS[g][0]):
                res = adamw(red, idx, _as_block(n, wts[n]), _as_block(n, mom[n]), _as_block(n, var[n]), "adamw_" + n)
                out_g[n], out_d[n], out_m[n], out_v[n] = [_as_block(n, r) for r in res]
                last = res[1]
        return last

    for si, s in enumerate(REDUCE_SETS):
        sem, parts, zones, own, recv, axes, plan = started[si]
        _, zones = split_wait("scatter_wait_%d" % si, parts, zones, sem, [dx, after], plan)
        halves = [chip_add(where, a, r, z, ax, "chip_add_" + g) for g, a, r, z, ax in zip(s, own, recv, zones, axes)]
        plan = _broadcast_plan(axes, False)
        sem, halves, _, mark = split_start("broadcast_start_%d" % si, halves, [], [len(s)],
                                           lambda sr, lr: [plan(sr, lr)])
        if crossing is not None:
            after = finish(*crossing, mark)
        crossing = (si, sem[0], halves, axes)
    after = finish(*crossing, None)

    small_named = {n + "_g": small[n] for n in GAINS}
    small_named.update(fox_f_bias=small["fox_bias"][:, :HEADS], hgrn_lb_logits=small["lb_logits"],
                       hgrn_norm_g=small["norm_g"])
    g_small = allreduce_small(_pack_small(small_named), after)
    d_small, m_small, v_small = adamw_small(g_small, _pack_small(wts), _pack_small(mom), _pack_small(var))

    for dst, slab in ((out_g, g_small), (out_d, d_small), (out_m, m_small), (out_v, v_small)):
        dst.update(_unpack_small(slab))

    total = lax.psum(loss[0, 0], ("x", "y", "c"))
    return (total, dx[None], *[out_g[n] for n in WEIGHTS], *[out_d[n] for n in WEIGHTS],
            *[out_m[n] for n in WEIGHTS], *[out_v[n] for n in WEIGHTS])
```

```python
import functools

import jax
import jax.numpy as jnp
from jax import lax
from jax.experimental import pallas as pl
from jax.experimental.pallas import tpu as pltpu

F32 = jnp.float32
BF16 = jnp.bfloat16

D_MODEL = 2048
SEQ = 2048
D_FF = 5632
PLE_DIM = 256
HEADS = 8
HEAD_DIM = 128
WIDTH = HEADS * HEAD_DIM
CHUNK = 64
SUB = 16
HGRN_HEADS_PER_STEP = 2
NORM_EPS = 1e-6
MACARON_SCALE = 0.5
N_CHIPS = 4

ADAM_LR = 0.001
ADAM_B1 = 0.9
ADAM_B2 = 0.999
ADAM_EPS = 1e-08
ADAM_WD = 0.01
ADAM_STEP = 10

LANES = 128
VMEM_LIMIT = 56 * 1024 * 1024
NEG_BIG = -1e30
MESH = pl.DeviceIdType.MESH


def _pick(n, cands):
    for c in cands:
        if c <= n and n % c == 0:
            return c
    return n


def _params(sem, vmem=VMEM_LIMIT):
    return pltpu.CompilerParams(dimension_semantics=sem, vmem_limit_bytes=vmem)


def _sigmoid(x):
    return 1.0 / (1.0 + jnp.exp(-x))


def _silu(x):
    return x * _sigmoid(x)


def _silu_grad(x):
    s = _sigmoid(x)
    return s * (1.0 + x * (1.0 - s))


_DN = {"nn": (((1,), (0,)), ((), ())), "nt": (((1,), (1,)), ((), ())), "tn": (((0,), (0,)), ((), ()))}


def _mm(a, b, *, mode, grid, a_spec, b_spec, o_spec, out_shape, acc_shape, name, after=(), init=None, into=None):
    nk = grid[2]
    dn = _DN[mode]
    after = [m for m in after if m is not None]
    extra = ([init] if init is not None else []) + ([into] if into is not None else []) + after
    n_extra = len(extra)

    def body(a_ref, b_ref, *rest):
        o_ref, acc_ref = rest[n_extra:]
        k = pl.program_id(2)

        @pl.when(k == 0)
        def _():
            acc_ref[...] = jnp.zeros_like(acc_ref) if init is None else rest[0][...].astype(F32)

        acc_ref[...] += lax.dot_general(a_ref[...].astype(BF16), b_ref[...].astype(BF16), dn,
                                        preferred_element_type=F32)

        @pl.when(k == nk - 1)
        def _():
            o_ref[...] = acc_ref[...].astype(o_ref.dtype)

    anywhere = pl.BlockSpec(memory_space=pl.ANY)
    return pl.pallas_call(
        body, name=name, grid=grid,
        in_specs=[a_spec, b_spec] + ([o_spec] if init is not None else []) + [anywhere] * (n_extra - (init is not None)),
        out_specs=o_spec, out_shape=out_shape, scratch_shapes=[pltpu.VMEM(acc_shape, F32)],
        input_output_aliases={} if into is None else {2 + (init is not None): 0},
        compiler_params=_params(("parallel", "parallel", "arbitrary")),
    )(a, b, *extra)


def mm_nn_2d(a, b, out_dtype, name, after=()):
    m, kk = a.shape
    n = b.shape[1]
    tm, tn = _pick(m, (512, 256, 128)), _pick(n, (1024, 512, 256, 128))
    tk = _pick(kk, (5632, 2816, 2048, 1408, 1024, 512, 256, 128))
    return _mm(a, b, mode="nn", grid=(m // tm, n // tn, kk // tk),
               a_spec=pl.BlockSpec((tm, tk), lambda i, j, k: (i, k)),
               b_spec=pl.BlockSpec((tk, tn), lambda i, j, k: (k, j)),
               o_spec=pl.BlockSpec((tm, tn), lambda i, j, k: (i, j)),
               out_shape=jax.ShapeDtypeStruct((m, n), out_dtype), acc_shape=(tm, tn), name=name, after=after)


def mm_nt_2d(a, b, out_dtype, name, after=()):
    m, c = a.shape
    n = b.shape[0]
    tm, tn, tk = _pick(m, (512, 256, 128)), _pick(n, (1408, 1024, 512, 256, 128)), _pick(c, (2048, 1408, 1024, 512, 256, 128))
    return _mm(a, b, mode="nt", grid=(m // tm, n // tn, c // tk),
               a_spec=pl.BlockSpec((tm, tk), lambda i, j, k: (i, k)),
               b_spec=pl.BlockSpec((tn, tk), lambda i, j, k: (j, k)),
               o_spec=pl.BlockSpec((tm, tn), lambda i, j, k: (i, j)),
               out_shape=jax.ShapeDtypeStruct((m, n), out_dtype), acc_shape=(tm, tn), name=name, after=after)


def mm_tn_2d(a, b, out_dtype, name):
    c, m = a.shape
    n = b.shape[1]
    tm, tn, tk = _pick(m, (1408, 1024, 512, 256, 128)), _pick(n, (1024, 512, 256, 128)), _pick(c, (2048, 1024, 512, 256, 128))
    return _mm(a, b, mode="tn", grid=(m // tm, n // tn, c // tk),
               a_spec=pl.BlockSpec((tk, tm), lambda i, j, k: (k, i)),
               b_spec=pl.BlockSpec((tk, tn), lambda i, j, k: (k, j)),
               o_spec=pl.BlockSpec((tm, tn), lambda i, j, k: (i, j)),
               out_shape=jax.ShapeDtypeStruct((m, n), out_dtype), acc_shape=(tm, tn), name=name)


def mm_nn_col(a, w, out_dtype, name, slot0=0, total=None, into=None):
    m, kk = a.shape
    g, jn, _, ns = w.shape
    total = g if total is None else total
    tm, tk = _pick(m, (512, 256, 128)), _pick(kk, (2048, 1024, 512, 256, 128))
    return _mm(a, w, mode="nn", grid=(m // tm, g * jn, kk // tk),
               a_spec=pl.BlockSpec((tm, tk), lambda i, j, k: (i, k)),
               b_spec=pl.BlockSpec((None, None, tk, ns), lambda i, j, k: (j // jn, j % jn, k, 0)),
               o_spec=pl.BlockSpec((None, tm, ns), lambda i, j, k: (slot0 + j // jn, i, j % jn)),
               out_shape=jax.ShapeDtypeStruct((total, m, jn * ns), out_dtype), acc_shape=(tm, ns), name=name,
               into=into)


def mm_nt_col(a, w, out_dtype, name, after=(), slot0=0, init=None):
    _, m, _ = a.shape
    g, jn, kk, ns = w.shape
    tm, tn = _pick(m, (512, 256, 128)), _pick(kk, (1024, 512, 256, 128))
    return _mm(a, w, mode="nt", grid=(m // tm, kk // tn, g * jn),
               a_spec=pl.BlockSpec((None, tm, ns), lambda i, j, k: (slot0 + k // jn, i, k % jn)),
               b_spec=pl.BlockSpec((None, None, tn, ns), lambda i, j, k: (k // jn, k % jn, j, 0)),
               o_spec=pl.BlockSpec((tm, tn), lambda i, j, k: (i, j)),
               out_shape=jax.ShapeDtypeStruct((m, kk), out_dtype), acc_shape=(tm, tn), name=name, after=after,
               init=init)


def mm_tn_col(a, b, jn, out_dtype, name):
    c, kk = a.shape
    g, _, n = b.shape
    ns = n // jn
    tm, tk = _pick(kk, (512, 256, 128)), _pick(c, (2048, 1024, 512, 256, 128))
    return _mm(a, b, mode="tn", grid=(kk // tm, g * jn, c // tk),
               a_spec=pl.BlockSpec((tk, tm), lambda i, j, k: (k, i)),
               b_spec=pl.BlockSpec((None, tk, ns), lambda i, j, k: (j // jn, k, j % jn)),
               o_spec=pl.BlockSpec((None, None, tm, ns), lambda i, j, k: (j // jn, j % jn, i, 0)),
               out_shape=jax.ShapeDtypeStruct((g, jn, kk, ns), out_dtype), acc_shape=(tm, ns), name=name)


def _rstd(x):
    return lax.rsqrt(jnp.mean(x * x, axis=-1, keepdims=True) + NORM_EPS)


def _rms_bwd(x, g, dy):
    r = _rstd(x)
    xn = x * r
    dyg = dy * g
    dx = r * (dyg - xn * jnp.mean(dyg * xn, axis=-1, keepdims=True))
    return dx, jnp.sum(dy * xn, axis=0, keepdims=True)


def _row_tile(t):
    return _pick(t, (256, 128, 64, 32, 16, 8))


def norm_in(h, g, name):
    t, d = h.shape
    tr = _row_tile(t)

    def body(h_ref, g_ref, u_ref):
        x = h_ref[...]
        u_ref[...] = (x * _rstd(x) * g_ref[...]).astype(BF16)

    return pl.pallas_call(
        body, name=name, grid=(t // tr,),
        in_specs=[pl.BlockSpec((tr, d), lambda i: (i, 0)), pl.BlockSpec((1, d), lambda i: (0, 0))],
        out_specs=pl.BlockSpec((tr, d), lambda i: (i, 0)),
        out_shape=jax.ShapeDtypeStruct((t, d), BF16), compiler_params=_params(("parallel",)),
    )(h, g)


def resid_post(h, y, g, scale, name):
    t, d = h.shape
    tr = _row_tile(t)

    def body(h_ref, y_ref, g_ref, o_ref):
        yv = y_ref[...]
        o_ref[...] = h_ref[...] + scale * (yv * _rstd(yv) * g_ref[...])

    row = pl.BlockSpec((tr, d), lambda i: (i, 0))
    return pl.pallas_call(
        body, name=name, grid=(t // tr,), in_specs=[row, row, pl.BlockSpec((1, d), lambda i: (0, 0))],
        out_specs=row, out_shape=jax.ShapeDtypeStruct((t, d), F32), compiler_params=_params(("parallel",)),
    )(h, y, g)


def post_bwd(dh, y, g, scale, name):
    t, d = dh.shape
    tr = _row_tile(t)

    def body(dh_ref, y_ref, g_ref, dy_ref, dg_ref):
        @pl.when(pl.program_id(0) == 0)
        def _():
            dg_ref[...] = jnp.zeros_like(dg_ref)

        dx, dg = _rms_bwd(y_ref[...], g_ref[...], scale * dh_ref[...])
        dy_ref[...] = dx.astype(BF16)
        dg_ref[...] += dg

    row = pl.BlockSpec((tr, d), lambda i: (i, 0))
    vec = pl.BlockSpec((1, d), lambda i: (0, 0))
    return pl.pallas_call(
        body, name=name, grid=(t // tr,), in_specs=[row, row, vec], out_specs=[row, vec],
        out_shape=[jax.ShapeDtypeStruct((t, d), BF16), jax.ShapeDtypeStruct((1, d), F32)],
        compiler_params=_params(("arbitrary",)),
    )(dh, y, g)


def pre_bwd(dh, h, g, dus, name, after=()):
    t, d = dh.shape
    tr = _row_tile(t)
    n_du = len(dus)
    after = [m for m in after if m is not None]

    def body(*refs):
        dh_ref, h_ref, g_ref = refs[:3]
        du_refs = refs[3:3 + n_du]
        o_ref, dg_ref = refs[3 + n_du + len(after):]

        @pl.when(pl.program_id(0) == 0)
        def _():
            dg_ref[...] = jnp.zeros_like(dg_ref)

        du = du_refs[0][...]
        for r in du_refs[1:]:
            du = du + r[...]
        dx, dg = _rms_bwd(h_ref[...], g_ref[...], du)
        o_ref[...] = dh_ref[...] + dx
        dg_ref[...] += dg

    row = pl.BlockSpec((tr, d), lambda i: (i, 0))
    vec = pl.BlockSpec((1, d), lambda i: (0, 0))
    return pl.pallas_call(
        body, name=name, grid=(t // tr,),
        in_specs=[row, row, vec] + [row] * n_du + [pl.BlockSpec(memory_space=pl.ANY)] * len(after),
        out_specs=[row, vec], out_shape=[jax.ShapeDtypeStruct((t, d), F32), jax.ShapeDtypeStruct((1, d), F32)],
        compiler_params=_params(("arbitrary",)),
    )(dh, h, g, *dus, *after)


def _ew_tiles(t, f):
    return _pick(t, (256, 128, 64, 32, 16, 8)), _pick(f, (1408, 1024, 512, 256, 128))


def swiglu_act(gu, name):
    _, t, f = gu.shape
    tr, tc = _ew_tiles(t, f)

    def body(gu_ref, o_ref):
        o_ref[...] = (_silu(gu_ref[0]) * gu_ref[1]).astype(BF16)

    return pl.pallas_call(
        body, name=name, grid=(t // tr, f // tc),
        in_specs=[pl.BlockSpec((2, tr, tc), lambda i, j: (0, i, j))],
        out_specs=pl.BlockSpec((tr, tc), lambda i, j: (i, j)),
        out_shape=jax.ShapeDtypeStruct((t, f), BF16), compiler_params=_params(("parallel", "parallel")),
    )(gu)


def swiglu_bwd(dact, gu, name):
    _, t, f = gu.shape
    tr, tc = _ew_tiles(t, f)

    def body(da_ref, gu_ref, o_ref):
        da = da_ref[...]
        gate = gu_ref[0]
        o_ref[0] = (da * gu_ref[1] * _silu_grad(gate)).astype(BF16)
        o_ref[1] = (da * _silu(gate)).astype(BF16)

    return pl.pallas_call(
        body, name=name, grid=(t // tr, f // tc),
        in_specs=[pl.BlockSpec((tr, tc), lambda i, j: (i, j)), pl.BlockSpec((2, tr, tc), lambda i, j: (0, i, j))],
        out_specs=pl.BlockSpec((2, tr, tc), lambda i, j: (0, i, j)),
        out_shape=jax.ShapeDtypeStruct((2, t, f), BF16), compiler_params=_params(("parallel", "parallel")),
    )(dact, gu)


def _col_blocks():
    w = WIDTH // LANES
    return dict(q_a=0, k_a=w, v_a=2 * w, q_b=3 * w, f_b=4 * w, i_b=5 * w, g_b=6 * w, gate_a=7 * w,
                gate_b=7 * w + D_MODEL // LANES)


def _tri(n, lower):
    r = lax.broadcasted_iota(jnp.int32, (n, n), 0)
    c = lax.broadcasted_iota(jnp.int32, (n, n), 1)
    return jnp.where((r >= c) if lower else (r <= c), 1.0, 0.0).astype(F32)


def _dot_hi(a, b):
    return jnp.dot(a, b, precision=lax.Precision.HIGHEST, preferred_element_type=F32)


def fox_prep(fa, bias, name):
    t, w = fa.shape
    tb = _pick(t, (256, 128, 64))

    def body(fa_ref, b_ref, c_ref, carry_ref):
        @pl.when(pl.program_id(0) == 0)
        def _():
            carry_ref[...] = jnp.zeros_like(carry_ref)

        z = fa_ref[...] + b_ref[...]
        lf = jnp.minimum(z, 0.0) - jnp.log(1.0 + jnp.exp(-jnp.abs(z)))
        c = _dot_hi(_tri(tb, True), lf) + carry_ref[...]
        c_ref[...] = c
        carry_ref[...] = carry_ref[...] + jnp.sum(lf, axis=0, keepdims=True)

    return pl.pallas_call(
        body, name=name, grid=(t // tb,),
        in_specs=[pl.BlockSpec((tb, w), lambda i: (i, 0)), pl.BlockSpec((1, w), lambda i: (0, 0))],
        out_specs=pl.BlockSpec((tb, w), lambda i: (i, 0)),
        out_shape=jax.ShapeDtypeStruct((t, w), F32), scratch_shapes=[pltpu.VMEM((1, w), F32)],
        compiler_params=_params(("arbitrary",)),
    )(fa, bias)


def fox_post_bwd(dc, fa, bias, name):
    t, w = fa.shape
    tb = _pick(t, (256, 128, 64))
    nb = t // tb

    def body(dc_ref, fa_ref, b_ref, dfa_ref, db_ref, carry_ref):
        @pl.when(pl.program_id(0) == 0)
        def _():
            carry_ref[...] = jnp.zeros_like(carry_ref)
            db_ref[...] = jnp.zeros_like(db_ref)

        dcv = dc_ref[...]
        dlf = _dot_hi(_tri(tb, False), dcv) + carry_ref[...]
        z = fa_ref[...] + b_ref[...]
        dz = dlf * _sigmoid(-z)
        dfa_ref[...] = dz.astype(BF16)
        db_ref[...] += jnp.sum(dz, axis=0, keepdims=True)
        carry_ref[...] = carry_ref[...] + jnp.sum(dcv, axis=0, keepdims=True)

    rev = pl.BlockSpec((tb, w), lambda i: (nb - 1 - i, 0))
    vec = pl.BlockSpec((1, w), lambda i: (0, 0))
    return pl.pallas_call(
        body, name=name, grid=(nb,), in_specs=[rev, rev, vec], out_specs=[rev, vec],
        out_shape=[jax.ShapeDtypeStruct((t, w), BF16), jax.ShapeDtypeStruct((1, w), F32)],
        scratch_shapes=[pltpu.VMEM((1, w), F32)], compiler_params=_params(("arbitrary",)),
    )(dc, fa, bias)


def _fox_probs(q_ref, k_ref, cc_ref, cr_ref, qi, tq, t):
    scale = HEAD_DIM ** -0.5
    s = lax.dot_general(q_ref[...].astype(BF16), k_ref[...].astype(BF16), _DN["nt"], preferred_element_type=F32)
    logits = s * scale + cc_ref[...] - cr_ref[...]
    qpos = qi * tq + lax.broadcasted_iota(jnp.int32, (tq, t), 0)
    kpos = lax.broadcasted_iota(jnp.int32, (tq, t), 1)
    logits = jnp.where(kpos <= qpos, logits, NEG_BIG)
    m = jnp.max(logits, axis=-1, keepdims=True)
    p = jnp.exp(logits - m)
    return p / jnp.sum(p, axis=-1, keepdims=True)


FOX_SEGMENTS = 4


def _fox_segments(t):
    tq = _pick(t, (256, 128))
    nseg = min(FOX_SEGMENTS, t // tq)
    return tq, nseg, t // tq // nseg


def _fox_specs(t, q0, kt, tq):
    cb = _col_blocks()
    dh = HEAD_DIM
    return [pl.BlockSpec((tq, dh), lambda h, i: (q0 + i, cb["q_a"] + h)),
            pl.BlockSpec((kt, dh), lambda h, i: (0, cb["k_a"] + h)),
            pl.BlockSpec((kt, dh), lambda h, i: (0, cb["v_a"] + h)),
            pl.BlockSpec((None, tq, 1), lambda h, i: (h, q0 + i, 0)),
            pl.BlockSpec((None, 1, kt), lambda h, i: (h, 0, 0))]


def fox_fwd(proj, c_col, c_row, name):
    t = proj.shape[0]
    tq, nseg, nq = _fox_segments(t)
    dh = HEAD_DIM

    def segment(out, r):
        q0, kt = r * nq, (r + 1) * nq * tq

        def body(q_ref, k_ref, v_ref, cc_ref, cr_ref, prev_ref, o_ref):
            p = _fox_probs(q_ref, k_ref, cc_ref, cr_ref, q0 + pl.program_id(1), tq, kt)
            o_ref[...] = jnp.dot(p.astype(BF16), v_ref[...].astype(BF16), preferred_element_type=F32).astype(BF16)

        return pl.pallas_call(
            body, name="%s_%d" % (name, r), grid=(HEADS, nq),
            in_specs=_fox_specs(t, q0, kt, tq) + [pl.BlockSpec(memory_space=pl.ANY)],
            out_specs=pl.BlockSpec((tq, dh), lambda h, i: (q0 + i, h)),
            out_shape=jax.ShapeDtypeStruct((t, WIDTH), BF16), input_output_aliases={5: 0},
            compiler_params=_params(("parallel", "parallel")),
        )(proj, proj, proj, c_col, c_row, out)

    out = lax.empty((t, WIDTH), BF16)
    for r in range(nseg):
        out = segment(out, r)
    return out


def fox_bwd(proj, c_col, c_row, do, name):
    t = proj.shape[0]
    tq, nseg, nq = _fox_segments(t)
    dh = HEAD_DIM
    scale = HEAD_DIM ** -0.5

    def segment(acc, r):
        q0, kt = r * nq, (r + 1) * nq * tq

        def body(q_ref, k_ref, v_ref, cc_ref, cr_ref, do_ref, dqp_ref, dkp_ref, dvp_ref, dccp_ref, dcrp_ref,
                 dq_ref, dk_ref, dv_ref, dcc_ref, dcr_ref):
            @pl.when(pl.program_id(1) == 0)
            def _():
                dk_ref[...] = dkp_ref[...]
                dv_ref[...] = dvp_ref[...]
                dcr_ref[...] = dcrp_ref[...]

            p = _fox_probs(q_ref, k_ref, cc_ref, cr_ref, q0 + pl.program_id(1), tq, kt)
            dov = do_ref[...].astype(BF16)
            kb = k_ref[...].astype(BF16)
            dv_ref[...] += lax.dot_general(p.astype(BF16), dov, _DN["tn"], preferred_element_type=F32)
            dp = lax.dot_general(dov, v_ref[...].astype(BF16), _DN["nt"], preferred_element_type=F32)
            ds = p * (dp - jnp.sum(p * dp, axis=-1, keepdims=True))
            dcc_ref[...] = jnp.sum(ds, axis=-1, keepdims=True)
            dcr_ref[...] -= jnp.sum(ds, axis=0, keepdims=True)
            dss = (ds * scale).astype(BF16)
            dq_ref[...] = jnp.dot(dss, kb, preferred_element_type=F32).astype(BF16)
            dk_ref[...] += lax.dot_general(dss, q_ref[...].astype(BF16), _DN["tn"], preferred_element_type=F32)

        rows = pl.BlockSpec((tq, dh), lambda h, i: (q0 + i, h))
        keys = pl.BlockSpec((kt, dh), lambda h, i: (0, h))
        col = pl.BlockSpec((None, tq, 1), lambda h, i: (h, q0 + i, 0))
        row = pl.BlockSpec((None, 1, kt), lambda h, i: (h, 0, 0))
        anywhere = pl.BlockSpec(memory_space=pl.ANY)
        return pl.pallas_call(
            body, name="%s_%d" % (name, r), grid=(HEADS, nq),
            in_specs=_fox_specs(t, q0, kt, tq) + [rows, anywhere, keys, keys, anywhere, row],
            out_specs=[rows, keys, keys, col, row],
            out_shape=[jax.ShapeDtypeStruct(a.shape, a.dtype) for a in acc],
            input_output_aliases={6 + k: k for k in range(5)},
            compiler_params=_params(("parallel", "arbitrary")),
        )(proj, proj, proj, c_col, c_row, do, *acc)

    acc = [lax.empty((t, WIDTH), BF16), jnp.zeros((t, WIDTH), F32), jnp.zeros((t, WIDTH), F32),
           lax.empty((HEADS, t, 1), F32), jnp.zeros((HEADS, 1, t), F32)]
    for r in range(nseg):
        acc = segment(acc, r)
    return acc


def _lower_bound(lg_ref):
    l0 = lg_ref[0:1, :]
    l1 = lg_ref[1:2, :]
    m = jnp.maximum(l0, l1)
    e0 = jnp.exp(l0 - m)
    e1 = jnp.exp(l1 - m)
    return e0 / (e0 + e1)


def _hgrn_inputs(qb_ref, fb_ref, lg_ref, q_s, k_s, cum_s):
    lb = _lower_bound(lg_ref)
    sig = _sigmoid(fb_ref[...])
    f = lb + (1.0 - lb) * sig
    q_s[...] = _silu(qb_ref[...])
    k_s[...] = 1.0 - f
    cum_s[...] = _dot_hi(_tri(CHUNK, True), jnp.log(f))
    return lb, sig, f


def _boundary(cum_s, a):
    if a == 0:
        return jnp.zeros((1, HEAD_DIM), F32)
    return cum_s[pl.ds(SUB * a - 1, 1), :]


def _hgrn_scores(q_s, k_s, cum_s):
    cum = cum_s[...]
    kk = k_s[...]
    lane = lax.broadcasted_iota(jnp.int32, (SUB, CHUNK), 1)
    row = lax.broadcasted_iota(jnp.int32, (SUB, 1), 0)
    blocks = []
    for a in range(CHUNK // SUB):
        rows = pl.ds(SUB * a, SUB)
        ca = _boundary(cum_s, a)
        cum_a = cum_s[rows, :]
        q_a = q_s[rows, :]
        qa = q_a * jnp.exp(cum_a - ca)
        ka = kk * jnp.exp(jnp.minimum(ca - cum, 0.0))
        blk = lax.dot_general(qa, ka, _DN["nt"], preferred_element_type=F32)
        blk = jnp.where(lane < SUB * a, blk, 0.0)
        for s in range(SUB):
            r = SUB * a + s
            e = jnp.exp(jnp.minimum(cum_a - cum_s[pl.ds(r, 1), :], 0.0))
            col = jnp.sum(q_a * k_s[pl.ds(r, 1), :] * e, axis=-1, keepdims=True)
            col = jnp.where(row >= s, col, 0.0)
            blk = jnp.where(lane == r, col, blk)
        blocks.append(blk)
    return jnp.concatenate(blocks, axis=0)


def hgrn_fwd(proj, lb_logits, name, after=()):
    after = [m for m in after if m is not None]
    t = proj.shape[0]
    n = t // CHUNK
    cb = _col_blocks()
    dh = HEAD_DIM

    hb = HGRN_HEADS_PER_STEP
    w = hb * dh

    def one_head(qb_ref, fb_ref, ib_ref, lg_ref, o_ref, st_ref, state, q_s, k_s, cum_s):
        _hgrn_inputs(qb_ref, fb_ref, lg_ref, q_s, k_s, cum_s)
        st = state[...]
        st_ref[...] = st
        cum = cum_s[...]
        v = ib_ref[...]
        qe = q_s[...] * jnp.exp(cum)
        inter = lax.dot_general(qe, st, _DN["nt"], preferred_element_type=F32)
        a_mat = _hgrn_scores(q_s, k_s, cum_s)
        o_ref[...] = inter + jnp.dot(a_mat, v, preferred_element_type=F32)
        last = cum_s[pl.ds(CHUNK - 1, 1), :]
        kd = k_s[...] * jnp.exp(last - cum)
        state[...] = st * jnp.exp(last) + lax.dot_general(v, kd, _DN["tn"], preferred_element_type=F32)

    def body(qb_ref, fb_ref, ib_ref, lg_ref, *rest):
        o_ref, st_ref = rest[len(after):len(after) + 2]
        scratch = rest[len(after) + 2:]

        @pl.when(pl.program_id(1) == 0)
        def _():
            for j in range(hb):
                scratch[4 * j][...] = jnp.zeros((dh, dh), F32)

        for j in range(hb):
            cols = (slice(None), pl.ds(j * dh, dh))
            one_head(qb_ref.at[cols], fb_ref.at[cols], ib_ref.at[cols], lg_ref.at[cols], o_ref.at[cols],
                     st_ref.at[j], *scratch[4 * j:4 * j + 4])

    blk = lambda off: pl.BlockSpec((CHUNK, w), lambda h, i: (i, off // hb + h))
    return pl.pallas_call(
        body, name=name, grid=(HEADS // hb, n),
        in_specs=[blk(cb["q_b"]), blk(cb["f_b"]), blk(cb["i_b"]), pl.BlockSpec((2, w), lambda h, i: (0, h))]
        + [pl.BlockSpec(memory_space=pl.ANY)] * len(after),
        out_specs=[pl.BlockSpec((CHUNK, w), lambda h, i: (i, h)),
                   pl.BlockSpec((hb, None, dh, dh), lambda h, i: (h, i, 0, 0))],
        out_shape=[jax.ShapeDtypeStruct((t, WIDTH), F32), jax.ShapeDtypeStruct((HEADS, n, dh, dh), F32)],
        scratch_shapes=([pltpu.VMEM((dh, dh), F32)] + [pltpu.VMEM((CHUNK, dh), F32)] * 3) * hb,
        compiler_params=_params(("parallel", "arbitrary")),
    )(proj, proj, proj, lb_logits, *after)


def hgrn_bwd(proj, lb_logits, states, do, name):
    t = proj.shape[0]
    n = t // CHUNK
    cb = _col_blocks()
    dh = HEAD_DIM
    nsub = CHUNK // SUB

    hb = HGRN_HEADS_PER_STEP
    w = hb * dh

    def one_head(qb_ref, fb_ref, ib_ref, lg_ref, st_ref, do_ref, dqb_ref, dfb_ref, dib_ref, dlb_ref,
                 dstate, q_s, k_s, cum_s, da_s, dq_s, dk_s):
        lb, sig, f = _hgrn_inputs(qb_ref, fb_ref, lg_ref, q_s, k_s, cum_s)
        st = st_ref[...]
        dst = dstate[...]
        cum = cum_s[...]
        q = q_s[...]
        kk = k_s[...]
        v = ib_ref[...]
        dov = do_ref[...]
        e_cum = jnp.exp(cum)
        qe = q * e_cum
        last = cum_s[pl.ds(CHUNK - 1, 1), :]
        e_last = jnp.exp(last)
        e_tail = jnp.exp(last - cum)
        kd = kk * e_tail

        a_mat = _hgrn_scores(q_s, k_s, cum_s)
        tri = _tri(CHUNK, True)
        da_s[...] = lax.dot_general(dov, v, _DN["nt"], preferred_element_type=F32) * tri
        dv = (lax.dot_general(a_mat, dov, _DN["tn"], preferred_element_type=F32)
              + lax.dot_general(kd, dst, _DN["nt"], preferred_element_type=F32))
        dk_state = jnp.dot(v, dst, preferred_element_type=F32) * e_tail
        dq_inter = jnp.dot(dov, st, preferred_element_type=F32) * e_cum
        dstate[...] = dst * e_last + lax.dot_general(dov, qe, _DN["tn"], preferred_element_type=F32)

        lane = lax.broadcasted_iota(jnp.int32, (SUB, CHUNK), 1)
        row = lax.broadcasted_iota(jnp.int32, (SUB, 1), 0)
        dk_s[...] = jnp.zeros_like(dk_s)
        for a in range(nsub):
            rows = pl.ds(SUB * a, SUB)
            ca = _boundary(cum_s, a)
            cum_a = cum_s[rows, :]
            q_a = q_s[rows, :]
            ea = jnp.exp(cum_a - ca)
            eb = jnp.exp(jnp.minimum(ca - cum, 0.0))
            da_a = da_s[rows, :]
            da_off = jnp.where(lane < SUB * a, da_a, 0.0)
            dq_a = ea * jnp.dot(da_off, kk * eb, preferred_element_type=F32)
            dk_s[...] += eb * lax.dot_general(da_off, q_a * ea, _DN["tn"], preferred_element_type=F32)
            dk_rows = jnp.zeros((SUB, dh), F32)
            for s in range(SUB):
                r = SUB * a + s
                e = jnp.exp(jnp.minimum(cum_a - cum_s[pl.ds(r, 1), :], 0.0))
                dcol = jnp.sum(jnp.where(lane == r, da_a, 0.0), axis=-1, keepdims=True)
                dcol = jnp.where(row >= s, dcol, 0.0)
                w = dcol * e
                dq_a = dq_a + w * k_s[pl.ds(r, 1), :]
                dk_rows = jnp.where(row == s, jnp.sum(w * q_a, axis=0, keepdims=True), dk_rows)
            dq_s[rows, :] = dq_a
            dk_s[rows, :] += dk_rows

        dq = dq_inter + dq_s[...]
        dk = dk_s[...] + dk_state
        d_last = (jnp.sum(dst * st, axis=0, keepdims=True) * e_last
                  + jnp.sum(kk * dk_state, axis=0, keepdims=True))
        rowc = lax.broadcasted_iota(jnp.int32, (CHUNK, 1), 0)
        dcum = q * dq - kk * dk + jnp.where(rowc == CHUNK - 1, d_last, 0.0)
        dg = _dot_hi(_tri(CHUNK, False), dcum)
        df = dg / f - dk
        dqb_ref[...] = (dq * _silu_grad(qb_ref[...])).astype(BF16)
        dfb_ref[...] = (df * (1.0 - lb) * sig * (1.0 - sig)).astype(BF16)
        dib_ref[...] = dv.astype(BF16)
        dlb_ref[...] += jnp.sum(df * (1.0 - sig), axis=0, keepdims=True)

    def body(qb_ref, fb_ref, ib_ref, lg_ref, st_ref, do_ref, dqb_ref, dfb_ref, dib_ref, dlb_ref, *scratch):
        @pl.when(pl.program_id(1) == 0)
        def _():
            for j in range(hb):
                scratch[7 * j][...] = jnp.zeros((dh, dh), F32)
            dlb_ref[...] = jnp.zeros_like(dlb_ref)

        for j in range(hb):
            cols = (slice(None), pl.ds(j * dh, dh))
            one_head(qb_ref.at[cols], fb_ref.at[cols], ib_ref.at[cols], lg_ref.at[cols], st_ref.at[j], do_ref.at[cols],
                     dqb_ref.at[cols], dfb_ref.at[cols], dib_ref.at[cols], dlb_ref.at[cols],
                     *scratch[7 * j:7 * j + 7])

    blk = lambda off: pl.BlockSpec((CHUNK, w), lambda h, i: (n - 1 - i, off // hb + h))
    out_blk = pl.BlockSpec((CHUNK, w), lambda h, i: (n - 1 - i, h))
    return pl.pallas_call(
        body, name=name, grid=(HEADS // hb, n),
        in_specs=[blk(cb["q_b"]), blk(cb["f_b"]), blk(cb["i_b"]), pl.BlockSpec((2, w), lambda h, i: (0, h)),
                  pl.BlockSpec((hb, None, dh, dh), lambda h, i: (h, n - 1 - i, 0, 0)), out_blk],
        out_specs=[out_blk, out_blk, out_blk, pl.BlockSpec((1, w), lambda h, i: (0, h))],
        out_shape=[jax.ShapeDtypeStruct((t, WIDTH), BF16)] * 3 + [jax.ShapeDtypeStruct((1, WIDTH), F32)],
        scratch_shapes=([pltpu.VMEM((dh, dh), F32)] + [pltpu.VMEM((CHUNK, dh), F32)] * 3
                        + [pltpu.VMEM((CHUNK, CHUNK), F32)] + [pltpu.VMEM((CHUNK, dh), F32)] * 2) * hb,
        compiler_params=_params(("parallel", "arbitrary")),
    )(proj, proj, proj, lb_logits, states, do)


def lb_bwd(dlb, lb_logits, name):
    def body(dlb_ref, lg_ref, o_ref):
        p0 = _lower_bound(lg_ref)
        d0 = dlb_ref[...] * p0 * (1.0 - p0)
        o_ref[0:1, :] = d0
        o_ref[1:2, :] = -d0

    return pl.pallas_call(body, name=name, out_shape=jax.ShapeDtypeStruct(lb_logits.shape, F32))(dlb, lb_logits)


def gnorm_fwd(o_raw, proj, norm_g, name, after=()):
    after = [m for m in after if m is not None]
    t = o_raw.shape[0]
    tr = _row_tile(t)
    cb = _col_blocks()
    dh = HEAD_DIM

    def body(o_ref, gb_ref, g_ref, *rest):
        x = o_ref[...]
        rest[-1][...] = (x * _rstd(x) * g_ref[...] * _silu(gb_ref[...])).astype(BF16)

    return pl.pallas_call(
        body, name=name, grid=(t // tr, HEADS),
        in_specs=[pl.BlockSpec((tr, dh), lambda i, h: (i, h)), pl.BlockSpec((tr, dh), lambda i, h: (i, cb["g_b"] + h)),
                  pl.BlockSpec((1, dh), lambda i, h: (0, 0))] + [pl.BlockSpec(memory_space=pl.ANY)] * len(after),
        out_specs=pl.BlockSpec((tr, dh), lambda i, h: (i, h)),
        out_shape=jax.ShapeDtypeStruct((t, WIDTH), BF16), compiler_params=_params(("parallel", "parallel")),
    )(o_raw, proj, norm_g, *after)


def gnorm_bwd(dy, o_raw, proj, norm_g, name):
    t = o_raw.shape[0]
    tr = _row_tile(t)
    cb = _col_blocks()
    dh = HEAD_DIM

    def body(dy_ref, o_ref, gb_ref, g_ref, do_ref, dgb_ref, dg_ref):
        @pl.when((pl.program_id(0) == 0) & (pl.program_id(1) == 0))
        def _():
            dg_ref[...] = jnp.zeros_like(dg_ref)

        x = o_ref[...]
        gb = gb_ref[...]
        dyv = dy_ref[...]
        g = g_ref[...]
        dx, dg = _rms_bwd(x, g, dyv * _silu(gb))
        do_ref[...] = dx
        dgb_ref[...] = (dyv * (x * _rstd(x) * g) * _silu_grad(gb)).astype(BF16)
        dg_ref[...] += dg

    hb = pl.BlockSpec((tr, dh), lambda i, h: (i, h))
    vec = pl.BlockSpec((1, dh), lambda i, h: (0, 0))
    return pl.pallas_call(
        body, name=name, grid=(t // tr, HEADS),
        in_specs=[hb, hb, pl.BlockSpec((tr, dh), lambda i, h: (i, cb["g_b"] + h)), vec],
        out_specs=[hb, hb, vec],
        out_shape=[jax.ShapeDtypeStruct((t, WIDTH), F32), jax.ShapeDtypeStruct((t, WIDTH), BF16),
                   jax.ShapeDtypeStruct((1, dh), F32)],
        compiler_params=_params(("arbitrary", "arbitrary")),
    )(dy, o_raw, proj, norm_g)


def merge_fwd(proj, y, name):
    _, t, d = y.shape
    tr = _row_tile(t)
    tc = _pick(d, (1024, 512, 256, 128))
    cb = _col_blocks()
    ga, gb = cb["gate_a"] * LANES // tc, cb["gate_b"] * LANES // tc

    def body(ga_ref, gb_ref, y_ref, o_ref):
        o_ref[...] = (_sigmoid(ga_ref[...]) * y_ref[0] + _sigmoid(gb_ref[...]) * y_ref[1]).astype(BF16)

    return pl.pallas_call(
        body, name=name, grid=(t // tr, d // tc),
        in_specs=[pl.BlockSpec((tr, tc), lambda i, j: (i, ga + j)), pl.BlockSpec((tr, tc), lambda i, j: (i, gb + j)),
                  pl.BlockSpec((2, tr, tc), lambda i, j: (0, i, j))],
        out_specs=pl.BlockSpec((tr, tc), lambda i, j: (i, j)),
        out_shape=jax.ShapeDtypeStruct((t, d), BF16), compiler_params=_params(("parallel", "parallel")),
    )(proj, proj, y)


def merge_bwd(dm, proj, y, name):
    _, t, d = y.shape
    tr = _row_tile(t)
    tc = _pick(d, (1024, 512, 256, 128))
    cb = _col_blocks()
    ga, gb = cb["gate_a"] * LANES // tc, cb["gate_b"] * LANES // tc

    def body(dm_ref, ga_ref, gb_ref, y_ref, dg_ref, dy_ref):
        dmv = dm_ref[...]
        for idx, g_ref in enumerate((ga_ref, gb_ref)):
            s = _sigmoid(g_ref[...])
            dg_ref[idx] = (dmv * y_ref[idx] * s * (1.0 - s)).astype(BF16)
            dy_ref[idx] = (dmv * s).astype(BF16)

    pair = pl.BlockSpec((2, tr, tc), lambda i, j: (0, i, j))
    return pl.pallas_call(
        body, name=name, grid=(t // tr, d // tc),
        in_specs=[pl.BlockSpec((tr, tc), lambda i, j: (i, j)), pl.BlockSpec((tr, tc), lambda i, j: (i, ga + j)),
                  pl.BlockSpec((tr, tc), lambda i, j: (i, gb + j)), pair],
        out_specs=[pair, pair],
        out_shape=[jax.ShapeDtypeStruct((2, t, d), BF16)] * 2, compiler_params=_params(("parallel", "parallel")),
    )(dm, proj, proj, y)


def ple_tail(h, a, b, g, target, name):
    t, d = h.shape
    tr = _row_tile(t)

    def body(h_ref, a_ref, b_ref, g_ref, t_ref, loss_ref, dh_ref, da_ref, db_ref, dg_ref):
        @pl.when(pl.program_id(0) == 0)
        def _():
            loss_ref[...] = jnp.zeros_like(loss_ref)
            dg_ref[...] = jnp.zeros_like(dg_ref)

        s = _sigmoid(a_ref[...])
        bv = b_ref[...]
        z = s * bv
        gv = g_ref[...]
        err = h_ref[...] + z * _rstd(z) * gv - t_ref[...]
        loss_ref[...] += 0.5 * jnp.sum(jnp.sum(err * err, axis=-1, keepdims=True), axis=0, keepdims=True) / d
        dh = err / d
        dh_ref[...] = dh
        dz, dg = _rms_bwd(z, gv, dh)
        da_ref[...] = (dz * bv * s * (1.0 - s)).astype(BF16)
        db_ref[...] = (dz * s).astype(BF16)
        dg_ref[...] += dg

    row = pl.BlockSpec((tr, d), lambda i: (i, 0))
    vec = pl.BlockSpec((1, d), lambda i: (0, 0))
    return pl.pallas_call(
        body, name=name, grid=(t // tr,), in_specs=[row, row, row, vec, row],
        out_specs=[pl.BlockSpec((1, 1), lambda i: (0, 0)), row, row, row, vec],
        out_shape=[jax.ShapeDtypeStruct((1, 1), F32), jax.ShapeDtypeStruct((t, d), F32),
                   jax.ShapeDtypeStruct((t, d), BF16), jax.ShapeDtypeStruct((t, d), BF16),
                   jax.ShapeDtypeStruct((1, d), F32)],
        compiler_params=_params(("arbitrary",)),
    )(h, a, b, g, target)


def _ffn_fwd(h, pre_g, post_g, get_w, idx, tag):
    u = norm_in(h, pre_g, tag + "_norm")
    gu = None
    for i, key in enumerate(GATE_UP_KEYS["gu" + idx]):
        w = get_w(key, h if gu is None else gu)
        gu = mm_nn_col(u, w, F32, "%s_gate_up_%d" % (tag, i), slot0=i, total=2, into=gu)
    act = swiglu_act(gu, tag + "_act")
    y = mm_nn_2d(act, get_w("down" + idx, gu), F32, tag + "_down")
    out = resid_post(h, y, post_g, MACARON_SCALE, tag + "_out")
    return out, (h, u, gu, act, y)


def _ffn_bwd(dh, saved, pre_g, post_g, get_w, emit, advance, idx, tag):
    h, u, gu, act, y = saved
    dy, d_post = post_bwd(dh, y, post_g, MACARON_SCALE, tag + "_post_bwd")
    m1 = emit("down" + idx, mm_tn_2d(act, dy, F32, tag + "_dw_down"))
    dact = mm_nt_2d(dy, get_w("down" + idx), F32, tag + "_dact", after=[m1])
    m2 = advance(dact)
    dgu = swiglu_bwd(dact, gu, tag + "_act_bwd")
    m3 = emit("gu" + idx, mm_tn_col(u, dgu, N_CHIPS, F32, tag + "_dw_gate_up"))
    du = None
    for i, key in enumerate(GATE_UP_KEYS["gu" + idx]):
        du = mm_nt_col(dgu, get_w(key), F32, "%s_du_%d" % (tag, i), after=[m2, m3] if du is None else (),
                       slot0=i, init=du)
    m4 = advance(du)
    dh_in, d_pre = pre_bwd(dh, h, pre_g, [du], tag + "_pre_bwd", after=[m4])
    return dh_in, d_pre, d_post


def _heads_col(a):
    t = a.shape[0]
    at = a[:, :HEADS].T
    return at.reshape(HEADS, t, 1), at.reshape(HEADS, 1, t)


def layer_step(x, p, target, gains, fox_bias, lb_logits, norm_g, get_w, emit, advance, early):
    t = x.shape[0]
    h1, s1 = _ffn_fwd(x, gains["ffn1_pre"], gains["ffn1_post"], get_w, "1", "ffn1")

    u2 = norm_in(h1, gains["mix_pre"], "mix_norm")
    proj = mm_nt_2d(u2, get_w("in_main", h1), F32, "mix_in")
    fa = mm_nt_2d(u2, get_w("in_fa"), F32, "mix_in_fa")
    mark = early("proj", proj)
    c = fox_prep(fa, fox_bias, "fox_prep")
    c_col, c_row = _heads_col(c)
    o_a = fox_fwd(proj, c_col, c_row, "fox_fwd")
    o_raw, states = hgrn_fwd(proj, lb_logits, "hgrn_fwd", after=[mark])
    mark = early("gu2", o_raw)
    o_b = gnorm_fwd(o_raw, proj, norm_g, "hgrn_norm", after=[mark])
    o_ab = jnp.stack([o_a, o_b])
    y_ab = _mm_branches(o_ab, get_w("proj", proj), "mix_proj")
    merged = merge_fwd(proj, y_ab, "mix_merge")
    mo = mm_nn_2d(merged, get_w("out"), F32, "mix_out")
    h2 = resid_post(h1, mo, gains["mix_post"], 1.0, "mix_resid")

    h3, s3 = _ffn_fwd(h2, gains["ffn2_pre"], gains["ffn2_post"], get_w, "2", "ffn2")

    u4 = norm_in(h3, gains["ple_pre"], "ple_norm")
    a4 = mm_nn_2d(u4, get_w("ple_gate"), F32, "ple_gate")
    b4 = mm_nn_col(p, get_w("ple_proj"), F32, "ple_proj")[0]
    loss, dh4, da4, db4, d_ple_post = ple_tail(h3, a4, b4, gains["ple_post"], target, "ple_tail")

    marks = [emit("ple_gate", mm_tn_2d(u4, da4, F32, "ple_dw_gate")),
             emit("ple_proj", mm_tn_col(p, db4[None], N_CHIPS, F32, "ple_dw_proj"))]
    du4 = mm_nt_2d(da4, get_w("ple_gate"), F32, "ple_du", after=marks)
    dh3, d_ple_pre = pre_bwd(dh4, h3, gains["ple_pre"], [du4], "ple_pre_bwd", after=[advance(du4)])

    dh2, d_f2_pre, d_f2_post = _ffn_bwd(dh3, s3, gains["ffn2_pre"], gains["ffn2_post"], get_w, emit, advance,
                                        "2", "ffn2")

    dmo, d_mix_post = post_bwd(dh2, mo, gains["mix_post"], 1.0, "mix_post_bwd")
    marks = [emit("out", mm_tn_2d(merged, dmo, F32, "mix_dw_out"))]
    dmerged = mm_nt_2d(dmo, get_w("out"), F32, "mix_dmerged", after=marks)
    dgate, dy_ab = merge_bwd(dmerged, proj, y_ab, "mix_merge_bwd")
    marks = [advance(dmerged), emit("proj", _mm_branches_dw(o_ab, dy_ab, "mix_dw_proj"))]
    do_ab = _mm_branches_bwd(dy_ab, get_w("proj"), "mix_do")
    do_raw, dg_b, d_norm_g = gnorm_bwd(do_ab[1], o_raw, proj, norm_g, "hgrn_norm_bwd")
    dq_b, df_b, di_b, dlb = hgrn_bwd(proj, lb_logits, states, do_raw, "hgrn_bwd")
    d_lb_logits = lb_bwd(dlb, lb_logits, "lb_bwd")
    dq_a, dk_a, dv_a, dc_col, dc_row = fox_bwd(proj, c_col, c_row, do_ab[0], "fox_bwd")
    dc = (dc_col.reshape(HEADS, t) + dc_row.reshape(HEADS, t)).T
    dc = jnp.pad(dc, ((0, 0), (0, LANES - HEADS)))
    dfa, d_fox_bias = fox_post_bwd(dc, fa, fox_bias, "fox_post_bwd")
    dproj = jnp.concatenate([dq_a, dk_a.astype(BF16), dv_a.astype(BF16), dq_b, df_b, di_b, dg_b,
                             dgate[0], dgate[1]], axis=1)
    marks.append(emit("in_main", mm_tn_2d(dproj, u2, F32, "mix_dw_in")))
    marks.append(emit("in_fa", mm_tn_2d(dfa, u2, F32, "mix_dw_in_fa")))
    du2a = mm_nn_2d(dproj, get_w("in_main"), F32, "mix_du", after=marks)
    du2b = mm_nn_2d(dfa, get_w("in_fa"), F32, "mix_du_fa")
    dh1, d_mix_pre = pre_bwd(dh2, h1, gains["mix_pre"], [du2a, du2b], "mix_pre_bwd", after=[advance(du2a)])

    dx, d_f1_pre, d_f1_post = _ffn_bwd(dh1, s1, gains["ffn1_pre"], gains["ffn1_post"], get_w, emit, advance,
                                       "1", "ffn1")

    small = dict(ffn1_pre=d_f1_pre, ffn1_post=d_f1_post, mix_pre=d_mix_pre, mix_post=d_mix_post,
                 ffn2_pre=d_f2_pre, ffn2_post=d_f2_post, ple_pre=d_ple_pre, ple_post=d_ple_post,
                 fox_bias=d_fox_bias, lb_logits=d_lb_logits, norm_g=d_norm_g)
    return loss, dx, small


def _mm_branches(o_ab, w_proj, name):
    g, t, kk = o_ab.shape
    _, jn, _, ns = w_proj.shape
    tm = _pick(t, (512, 256, 128))
    return _mm(o_ab, w_proj, mode="nn", grid=(t // tm, g * jn, 1),
               a_spec=pl.BlockSpec((None, tm, kk), lambda i, j, k: (j // jn, i, 0)),
               b_spec=pl.BlockSpec((None, None, kk, ns), lambda i, j, k: (j // jn, j % jn, 0, 0)),
               o_spec=pl.BlockSpec((None, tm, ns), lambda i, j, k: (j // jn, i, j % jn)),
               out_shape=jax.ShapeDtypeStruct((g, t, jn * ns), F32), acc_shape=(tm, ns), name=name)


def _mm_branches_bwd(dy_ab, w_proj, name):
    g, t, _ = dy_ab.shape
    _, jn, kk, ns = w_proj.shape
    tm = _pick(t, (512, 256, 128))
    return _mm(dy_ab, w_proj, mode="nt", grid=(t // tm, g, jn),
               a_spec=pl.BlockSpec((None, tm, ns), lambda i, j, k: (j, i, k)),
               b_spec=pl.BlockSpec((None, None, kk, ns), lambda i, j, k: (j, k, 0, 0)),
               o_spec=pl.BlockSpec((None, tm, kk), lambda i, j, k: (j, i, 0)),
               out_shape=jax.ShapeDtypeStruct((g, t, kk), F32), acc_shape=(tm, kk), name=name)


def _mm_branches_dw(o_ab, dy_ab, name):
    g, t, kk = o_ab.shape
    d = dy_ab.shape[2]
    jn = N_CHIPS
    ns = d // jn
    return _mm(o_ab, dy_ab, mode="tn", grid=(1, g * jn, 1),
               a_spec=pl.BlockSpec((None, t, kk), lambda i, j, k: (j // jn, 0, 0)),
               b_spec=pl.BlockSpec((None, t, ns), lambda i, j, k: (j // jn, 0, j % jn)),
               o_spec=pl.BlockSpec((None, None, kk, ns), lambda i, j, k: (j // jn, j % jn, 0, 0)),
               out_shape=jax.ShapeDtypeStruct((g, jn, kk, ns), F32), acc_shape=(kk, ns), name=name)


HBM_SPEC = pl.BlockSpec(memory_space=pltpu.HBM)
SEM_SPEC = pl.BlockSpec(memory_space=pltpu.SEMAPHORE)
ANY_SPEC = pl.BlockSpec(memory_space=pl.ANY)
EFFECT = pltpu.SideEffectType.DATAFLOW_SIDE_EFFECTING


def _in_hbm(a):
    return pltpu.with_memory_space_constraint(a, pltpu.HBM)


def _place():
    x, y, c = lax.axis_index("x"), lax.axis_index("y"), lax.axis_index("c")
    chips = [(1 - x, y), (x, 1 - y), (1 - x, 1 - y)]
    return x, y, c, chips


def _half(shape, which, axis):
    n = shape[-2 + axis] // 2
    cut = pl.ds(which * n, n)
    return (cut, slice(None)) if axis == 0 else (slice(None), cut)


def _half_shape(shape, axis):
    s = list(shape)
    s[len(s) - 2 + axis] //= 2
    return tuple(s)


def _remote(src, dst, send_sems, recv_sems, k, to):
    return pltpu.make_async_remote_copy(src_ref=src, dst_ref=dst, send_sem=send_sems.at[k], recv_sem=recv_sems.at[k],
                                        device_id=to, device_id_type=MESH)


def split_start(name, srcs, lands, counts, copies):
    ns, nl, nset = len(srcs), len(lands), len(counts)

    def body(*refs):
        src_refs, land_refs = refs[:ns], refs[ns:ns + nl]
        sems = refs[ns + nl:ns + nl + 2 * nset]
        for s, plan in enumerate(copies(src_refs, land_refs)):
            for k, (src, dst, to) in enumerate(plan):
                _remote(src, dst, sems[2 * s], sems[2 * s + 1], k, to).start()
        refs[-1][...] = jnp.zeros_like(refs[-1])

    out_shape = []
    for n in counts:
        out_shape += [pltpu.SemaphoreType.DMA((n,)), pltpu.SemaphoreType.DMA((n,))]
    out_shape += [pltpu.HBM(a.shape, a.dtype) for a in list(srcs) + list(lands)]
    out_shape.append(jax.ShapeDtypeStruct((8, LANES), F32))
    res = pl.pallas_call(
        body, name=name, out_shape=tuple(out_shape), in_specs=[HBM_SPEC] * (ns + nl),
        out_specs=tuple([SEM_SPEC] * (2 * nset) + [HBM_SPEC] * (ns + nl) + [pl.BlockSpec(memory_space=pltpu.VMEM)]),
        input_output_aliases={i: 2 * nset + i for i in range(ns + nl)},
        compiler_params=pltpu.CompilerParams(has_side_effects=EFFECT),
    )(*[_in_hbm(a) for a in list(srcs) + list(lands)])
    sems = [(res[2 * s], res[2 * s + 1]) for s in range(nset)]
    return sems, list(res[2 * nset:2 * nset + ns]), list(res[2 * nset + ns:-1]), res[-1]


def split_wait(name, srcs, lands, sems, afters, copies):
    afters = [a for a in afters if a is not None]
    ns, nl, na = len(srcs), len(lands), len(afters)

    def body(*refs):
        src_refs, land_refs = refs[:ns], refs[ns:ns + nl]
        send_sems, recv_sems = refs[ns + nl:ns + nl + 2]
        for k, (src, dst, to) in enumerate(copies(src_refs, land_refs)):
            cp = _remote(src, dst, send_sems, recv_sems, k, to)
            cp.wait_send()
            cp.wait_recv()

    res = pl.pallas_call(
        body, name=name, out_shape=tuple(pltpu.HBM(a.shape, a.dtype) for a in list(srcs) + list(lands)),
        in_specs=[HBM_SPEC] * (ns + nl) + [SEM_SPEC, SEM_SPEC] + [ANY_SPEC] * na,
        out_specs=tuple([HBM_SPEC] * (ns + nl)), input_output_aliases={i: i for i in range(ns + nl)},
        compiler_params=pltpu.CompilerParams(has_side_effects=EFFECT),
    )(*srcs, *lands, sems[0], sems[1], *afters)
    return list(res[:ns]), list(res[ns:])


def _gather_plan(blocks):
    def copies(src_refs, land_refs):
        x, y, c, chips = _place()
        j_me = 2 * x + y
        plan = []
        for si, li, g, axis in blocks:
            src, land = src_refs[si], land_refs[li].at[g]
            mine = _half(src.shape, c, axis)
            for px, py in chips:
                plan.append((src.at[mine], land.at[(j_me,) + mine], (px, py, c)))
            plan.append((src, land.at[j_me], (x, y, 1 - c)))
        return plan
    return copies


def _gather_arrivals(blocks):
    def copies(src_refs, land_refs):
        x, y, c, chips = _place()
        j_me = 2 * x + y
        plan = []
        for si, li, g, axis in blocks:
            src, land = src_refs[si], land_refs[li].at[g]
            mine = _half(src.shape, c, axis)
            for px, py in chips:
                plan.append((src.at[mine], land.at[(2 * px + py,) + mine], (px, py, c)))
            plan.append((src, land.at[j_me], (x, y, 1 - c)))
        return plan
    return copies


def _pass_plan(blocks, arrivals):
    def copies(src_refs, land_refs):
        x, y, c, chips = _place()
        plan = []
        for li, g, axis in blocks:
            land = src_refs[li].at[g]
            half = _half(land.shape[1:], (1 - c) if arrivals else c, axis)
            for px, py in chips:
                part = land.at[(2 * px + py,) + half]
                plan.append((part, part, (x, y, 1 - c)))
        return plan
    return copies


def gather_pass(name, lands, blocks):
    n = len(lands)

    def body(*refs):
        outs = refs[n:2 * n]
        send_sems, recv_sems = refs[2 * n:]
        x, y, c, chips = _place()
        sent = []
        for i, (li, g, axis) in enumerate(blocks):
            land = outs[li].at[g]
            mine = _half(land.shape[1:], c, axis)
            for k, (px, py) in enumerate(chips):
                part = land.at[(2 * px + py,) + mine]
                cp = _remote(part, part, send_sems, recv_sems, 3 * i + k, (x, y, 1 - c))
                cp.start()
                sent.append(cp)
        for i, (li, g, axis) in enumerate(blocks):
            land = outs[li].at[g]
            other = _half(land.shape[1:], 1 - c, axis)
            for k, (px, py) in enumerate(chips):
                part = land.at[(2 * px + py,) + other]
                _remote(part, part, send_sems, recv_sems, 3 * i + k, (x, y, 1 - c)).wait_recv()
        for cp in sent:
            cp.wait_send()

    m = 3 * len(blocks)
    return pl.pallas_call(
        body, name=name, in_specs=[ANY_SPEC] * n, out_specs=[ANY_SPEC] * n,
        out_shape=[jax.ShapeDtypeStruct(a.shape, a.dtype) for a in lands],
        input_output_aliases={i: i for i in range(n)},
        scratch_shapes=[pltpu.SemaphoreType.DMA((m,)), pltpu.SemaphoreType.DMA((m,))],
    )(*lands)


def _pair_plan(axes):
    def copies(src_refs, land_refs):
        x, y, c, _ = _place()
        return [(src_refs[i].at[(slice(None), slice(None)) + _half(src_refs[i].shape, 1 - c, a)], land_refs[i],
                 (x, y, 1 - c)) for i, a in enumerate(axes)]
    return copies


def _scatter_plan(n):
    def copies(src_refs, land_refs):
        x, y, c, chips = _place()
        return [(src_refs[i].at[:, 2 * px + py], land_refs[i].at[k], (px, py, c))
                for i in range(n) for k, (px, py) in enumerate(chips)]
    return copies


def _broadcast_plan(axes, arrivals):
    def copies(src_refs, land_refs):
        x, y, c, _ = _place()
        plan = []
        for i, a in enumerate(axes):
            part = src_refs[i].at[(slice(None),) + _half(src_refs[i].shape, (1 - c) if arrivals else c, a)]
            plan.append((part, part, (x, y, 1 - c)))
        return plan
    return copies


N_DEV = 8
SLAB_ROWS = 16


def allreduce_small(slab, after):
    def body(x_ref, after_ref, o_ref, land, send_sems, recv_sems):
        x, y, c, _ = _place()
        me = 4 * x + 2 * y + c
        land[me] = x_ref[...]
        copies = []
        for d in range(1, N_DEV):
            to = (me + d) % N_DEV
            cp = pltpu.make_async_remote_copy(
                src_ref=x_ref, dst_ref=land.at[me], send_sem=send_sems.at[d - 1], recv_sem=recv_sems.at[me],
                device_id=(to // 4, (to // 2) % 2, to % 2), device_id_type=MESH)
            cp.start()
            copies.append(cp)
        for d in range(1, N_DEV):
            frm = (me + d) % N_DEV
            pltpu.make_async_remote_copy(
                src_ref=x_ref, dst_ref=land.at[frm], send_sem=send_sems.at[d - 1], recv_sem=recv_sems.at[frm],
                device_id=(frm // 4, (frm // 2) % 2, frm % 2), device_id_type=MESH).wait_recv()
        for cp in copies:
            cp.wait_send()
        acc = land[0]
        for s in range(1, N_DEV):
            acc = acc + land[s]
        o_ref[...] = acc

    vm = pl.BlockSpec(memory_space=pltpu.VMEM)
    return pl.pallas_call(
        body, name="allreduce_small", in_specs=[vm, ANY_SPEC], out_specs=vm,
        out_shape=jax.ShapeDtypeStruct(slab.shape, F32),
        scratch_shapes=[pltpu.VMEM((N_DEV,) + slab.shape, F32), pltpu.SemaphoreType.DMA((N_DEV - 1,)),
                        pltpu.SemaphoreType.DMA((N_DEV,))],
    )(slab, after)


BLOCK_BYTES = 3 * 1024 * 1024


def _tiles_2d(r, c, budget=BLOCK_BYTES):
    if r % 8 == 0:
        tc = c if c % LANES else _pick(c, (2048, 1408, 1024, 512, 256, 128))
        tr = 8
        for cand in (512, 256, 128, 64, 32, 16, 8):
            if r % cand == 0 and cand * tc * 4 <= budget:
                tr = cand
                break
        if tr >= 64 or c % LANES or r * LANES * 4 > budget:
            return tr, tc
    tc = LANES
    for cand in (1024, 512, 256, 128):
        if c % cand == 0 and r * cand * 4 <= budget:
            tc = cand
            break
    return r, tc


def _grid_spec(grid, in_specs, out_specs):
    return pltpu.PrefetchScalarGridSpec(num_scalar_prefetch=1, grid=grid, in_specs=in_specs, out_specs=out_specs)


def _own(axis, nr, nc):
    if axis == 0:
        return lambda i, j, where: (where[1] * nr + i, j)
    return lambda i, j, where: (i, where[1] * nc + j)


def pair_add(where, grad, recv, axis, name):
    g, jn, hr, hc = recv.shape
    tr, tc = _tiles_2d(hr, hc)
    nr, nc = hr // tr, hc // tc
    own = _own(axis, nr, nc)
    others = jn - 1

    def body(where_ref, a_ref, b_ref, o_ref):
        o_ref[...] = (a_ref[...] + b_ref[...]).astype(BF16)

    def block(a, where):
        return a // others, (where[0] + 1 + a % others) % jn

    blk = pl.BlockSpec((None, None, tr, tc), lambda a, i, j, where: block(a, where) + (i, j))
    mine = pl.BlockSpec((None, None, tr, tc), lambda a, i, j, where: block(a, where) + own(i, j, where))
    return pl.pallas_call(
        body, name=name, grid_spec=_grid_spec((g * others, nr, nc), [mine, blk], blk),
        out_shape=jax.ShapeDtypeStruct(recv.shape, BF16),
        compiler_params=_params(("parallel", "parallel", "parallel")),
    )(where, grad, recv)


def chip_add(where, grad, pair, recv, axis, name):
    g, jn, hr, hc = pair.shape
    tr, tc = _tiles_2d(hr, hc)
    nr, nc = hr // tr, hc // tc
    own = _own(axis, nr, nc)
    full = (g, 2 * hr, hc) if axis == 0 else (g, hr, 2 * hc)

    def body(where_ref, a_ref, p_ref, b_ref, o_ref):
        s = a_ref[...] + p_ref[...]
        for k in range(3):
            s = s + b_ref[k].astype(F32)
        o_ref[...] = s

    return pl.pallas_call(
        body, name=name,
        grid_spec=_grid_spec((g, nr, nc),
                             [pl.BlockSpec((None, None, tr, tc), lambda a, i, j, where: (a, where[0]) + own(i, j, where)),
                              pl.BlockSpec((None, None, tr, tc), lambda a, i, j, where: (a, where[0], i, j)),
                              pl.BlockSpec((3, None, tr, tc), lambda a, i, j, where: (0, a, i, j))],
                             pl.BlockSpec((None, tr, tc), lambda a, i, j, where: (a,) + own(i, j, where))),
        out_shape=jax.ShapeDtypeStruct(full, F32), compiler_params=_params(("parallel", "parallel", "parallel")),
    )(where, grad, pair, recv)


def _adam_math(w, g, m, v):
    m2 = ADAM_B1 * m + (1.0 - ADAM_B1) * g
    v2 = ADAM_B2 * v + (1.0 - ADAM_B2) * (g * g)
    m_hat = m2 / (1.0 - ADAM_B1 ** ADAM_STEP)
    v_hat = v2 / (1.0 - ADAM_B2 ** ADAM_STEP)
    delta = -ADAM_LR * (m_hat / (jnp.sqrt(v_hat) + ADAM_EPS) + ADAM_WD * w)
    return delta, m2, v2


def adamw(grad, idx, w, m, v, name):
    _, r, cc = w.shape
    rg = grad.shape[1]
    tr, tc = _tiles_2d(r, cc, BLOCK_BYTES // 2)
    assert rg == r or tr == r
    gr = tr if rg == r else rg

    def body(g_ref, w_ref, m_ref, v_ref, go_ref, d_ref, mo_ref, vo_ref):
        g = g_ref[pl.ds(0, tr), :]
        delta, m2, v2 = _adam_math(w_ref[...], g, m_ref[...], v_ref[...])
        go_ref[...] = g
        d_ref[...] = delta
        mo_ref[...] = m2
        vo_ref[...] = v2

    blk = pl.BlockSpec((None, tr, tc), lambda i, j: (0, i, j))
    return pl.pallas_call(
        body, name=name, grid=(r // tr, cc // tc),
        in_specs=[pl.BlockSpec((None, gr, tc), lambda i, j: (idx, i, j)), blk, blk, blk], out_specs=[blk] * 4,
        out_shape=[jax.ShapeDtypeStruct(w.shape, F32)] * 4, compiler_params=_params(("parallel", "parallel")),
    )(grad, w, m, v)


def adamw_small(g, w, m, v):
    def body(g_ref, w_ref, m_ref, v_ref, d_ref, mo_ref, vo_ref):
        delta, m2, v2 = _adam_math(w_ref[...], g_ref[...], m_ref[...], v_ref[...])
        d_ref[...] = delta
        mo_ref[...] = m2
        vo_ref[...] = v2

    return pl.pallas_call(body, name="adamw_small", out_shape=[jax.ShapeDtypeStruct(w.shape, F32)] * 3)(g, w, m, v)


GAINS = ("ffn1_pre", "ffn1_post", "mix_pre", "mix_post", "ffn2_pre", "ffn2_post", "ple_pre", "ple_post")
WEIGHTS = ("ffn1_pre_g", "ffn1_post_g", "ffn1_w_gate", "ffn1_w_up", "ffn1_w_down", "mix_pre_g", "mix_post_g",
           "mix_w_in", "fox_f_bias", "hgrn_lb_logits", "hgrn_norm_g", "mix_w_proj_fox", "mix_w_proj_hgrn",
           "mix_w_out", "ffn2_pre_g", "ffn2_post_g", "ffn2_w_gate", "ffn2_w_up", "ffn2_w_down", "ple_pre_g",
           "ple_post_g", "ple_w_gate", "ple_w_proj")
GROUPS = dict(gu1=(("ffn1_w_gate", "ffn1_w_up"), 0), down1=(("ffn1_w_down",), 0), win=(("mix_w_in",), 1),
              proj=(("mix_w_proj_fox", "mix_w_proj_hgrn"), 0), out=(("mix_w_out",), 0),
              gu2=(("ffn2_w_gate", "ffn2_w_up"), 0), down2=(("ffn2_w_down",), 0), ple_gate=(("ple_w_gate",), 0),
              ple_proj=(("ple_w_proj",), 0))
TRANSPOSED = ("mix_w_in",)
ROW_BLOCKS = ("down1", "down2", "out", "ple_gate")
GATHER_SETS = (("gate1",), ("up1",), ("down1",), ("win",), ("proj", "out"), ("gu2", "down2", "ple_gate", "ple_proj"))
GATHER_GROUPS = dict(GROUPS, gate1=(("ffn1_w_gate",), 0), up1=(("ffn1_w_up",), 0))
GATE_UP_KEYS = dict(gu1=("gate1", "up1"), gu2=("gu2",))
REDUCE_SETS = (("ple_gate", "ple_proj", "down2", "gu2"), ("out", "proj", "win"), ("down1",), ("gu1",))


def _pad_row(a, width):
    a = a.reshape(1, -1)
    return jnp.pad(a, ((0, 0), (0, width - a.shape[1])))


def _pack_small(vals):
    d = D_MODEL
    rows = [vals[n + "_g"].reshape(1, d) for n in GAINS]
    rows.append(_pad_row(vals["fox_f_bias"], d))
    lg = vals["hgrn_lb_logits"]
    rows += [_pad_row(lg[0], d), _pad_row(lg[1], d), _pad_row(vals["hgrn_norm_g"], d)]
    slab = jnp.concatenate(rows, axis=0)
    return jnp.pad(slab, ((0, SLAB_ROWS - slab.shape[0]), (0, 0)))


def _unpack_small(slab):
    out = {n + "_g": slab[i:i + 1] for i, n in enumerate(GAINS)}
    out["fox_f_bias"] = slab[8:9, :HEADS]
    out["hgrn_lb_logits"] = slab[9:11, :WIDTH]
    out["hgrn_norm_g"] = slab[11:12, :HEAD_DIM]
    return out


def _split_in(win_t):
    lo = 3 * WIDTH
    main = jnp.concatenate([win_t[:lo], win_t[lo + HEADS:]], axis=0)
    fa = jnp.pad(win_t[lo:lo + HEADS], ((0, LANES - HEADS), (0, 0)))
    return main, fa


def _join_in(main, fa):
    lo = 3 * WIDTH
    return jnp.concatenate([main[:lo], fa[:HEADS], main[lo:]], axis=0)


def _as_block(name, a):
    return jnp.swapaxes(a, 1, 2) if name in TRANSPOSED else a


def _send_block(name, a, mark):
    blk = _as_block(name, a)[0]
    if mark is not None:
        blk = blk + mark[0, 0]
    return blk.astype(BF16)


def kernel(x, p, ffn1_pre_g, ffn1_post_g, ffn1_w_gate, ffn1_w_up, ffn1_w_down, mix_pre_g, mix_post_g, mix_w_in, fox_f_bias, hgrn_lb_logits, hgrn_norm_g, mix_w_proj_fox, mix_w_proj_hgrn, mix_w_out, ffn2_pre_g, ffn2_post_g, ffn2_w_gate, ffn2_w_up, ffn2_w_down, ple_pre_g, ple_post_g, ple_w_gate, ple_w_proj, loss_target, m_ffn1_pre_g, m_ffn1_post_g, m_ffn1_w_gate, m_ffn1_w_up, m_ffn1_w_down, m_mix_pre_g, m_mix_post_g, m_mix_w_in, m_fox_f_bias, m_hgrn_lb_logits, m_hgrn_norm_g, m_mix_w_proj_fox, m_mix_w_proj_hgrn, m_mix_w_out, m_ffn2_pre_g, m_ffn2_post_g, m_ffn2_w_gate, m_ffn2_w_up, m_ffn2_w_down, m_ple_pre_g, m_ple_post_g, m_ple_w_gate, m_ple_w_proj, v_ffn1_pre_g, v_ffn1_post_g, v_ffn1_w_gate, v_ffn1_w_up, v_ffn1_w_down, v_mix_pre_g, v_mix_post_g, v_mix_w_in, v_fox_f_bias, v_hgrn_lb_logits, v_hgrn_norm_g, v_mix_w_proj_fox, v_mix_w_proj_hgrn, v_mix_w_out, v_ffn2_pre_g, v_ffn2_post_g, v_ffn2_w_gate, v_ffn2_w_up, v_ffn2_w_down, v_ple_pre_g, v_ple_post_g, v_ple_w_gate, v_ple_w_proj):
    args = dict(locals())
    wts = {n: args[n] for n in WEIGHTS}
    mom = {n: args["m_" + n] for n in WEIGHTS}
    var = {n: args["v_" + n] for n in WEIGHTS}
    d = D_MODEL
    where = jnp.stack([2 * lax.axis_index("x") + lax.axis_index("y"), lax.axis_index("c")]).astype(jnp.int32)

    def start_sets(name, which, mark):
        srcs, lands, plans = [], [], []
        for si in which:
            blocks = []
            for g in GATHER_SETS[si]:
                names, axis = GATHER_GROUPS[g]
                for pos, n in enumerate(names):
                    blocks.append((len(srcs), len(lands), pos, axis))
                    srcs.append(_send_block(n, wts[n], mark))
                lands.append(lax.empty((len(names), N_CHIPS) + srcs[-1].shape, BF16))
            plans.append(blocks)
        sems, srcs, lands, mark = split_start(name, srcs, lands, [4 * len(b) for b in plans],
                                              lambda sr, lr: [_gather_plan(b)(sr, lr) for b in plans])
        out = {}
        for k, si in enumerate(which):
            s_idx = sorted({b[0] for b in plans[k]})
            l_idx = sorted({b[1] for b in plans[k]})
            local = [(s_idx.index(a), l_idx.index(b), pos, ax) for a, b, pos, ax in plans[k]]
            out[si] = (sems[k], [srcs[i] for i in s_idx], [lands[i] for i in l_idx], local)
        return out, mark

    flying, mark = start_sets("gather_start_0", [0], None)
    rest, all_started = start_sets("gather_start_1", list(range(1, len(GATHER_SETS))), mark)
    flying.update(rest)
    full, passing = {}, {}

    def set_of(key):
        g = "win" if key in ("in_main", "in_fa") else key
        return g, [g in s for s in GATHER_SETS].index(True)

    def arrive(si, after):
        sem, srcs, lands, local = flying.pop(si)
        _, got = split_wait("gather_wait_%d" % si, srcs, lands, sem, [after, all_started], _gather_arrivals(local))
        return got, [(b, pos, ax) for _, b, pos, ax in local]

    def early(key, after):
        g, si = set_of(key)
        if g in full or si not in flying:
            return None
        got, blocks = arrive(si, after)
        sem, got, _, mark = split_start("pass_start_%d" % si, got, [], [3 * len(blocks)],
                                        lambda sr, lr: [_pass_plan(blocks, False)(sr, lr)])
        passing[si] = (sem[0], got, blocks)
        return mark

    def land_set(si, after):
        if si in passing:
            sem, got, blocks = passing.pop(si)
            got, _ = split_wait("pass_wait_%d" % si, got, [], sem, [after], _pass_plan(blocks, True))
        else:
            got, blocks = arrive(si, after)
            got = gather_pass("gather_pass_%d" % si, got, blocks)
        for g, arr in zip(GATHER_SETS[si], got):
            if g == "win":
                full["win"] = arr
                full["in_main"], full["in_fa"] = _split_in(arr.reshape(-1, d))
            else:
                full[g] = arr.reshape(-1, d) if g in ROW_BLOCKS else arr

    def get_w(key, after=None):
        g, si = set_of(key)
        if g not in full:
            land_set(si, after)
        return full[key]

    grads, pairing, started = {}, [], {}
    rows4 = lambda a: a.reshape(1, N_CHIPS, a.shape[0] // N_CHIPS, a.shape[1])

    def emit(key, grad):
        if key in ("in_main", "in_fa"):
            grads[key] = grad
            if "in_main" not in grads or "in_fa" not in grads:
                return None
            key, grad = "win", rows4(_join_in(grads["in_main"], grads["in_fa"]))
        grads[key] = grad if grad.ndim == 4 else rows4(grad)
        for si, s in enumerate(REDUCE_SETS):
            if key in s and all(g in grads for g in s):
                axes = [GROUPS[g][1] for g in s]
                own = [grads[g] for g in s]
                zones = [lax.empty(_half_shape(a.shape, ax), F32) for a, ax in zip(own, axes)]
                plan = _pair_plan(axes)
                sem, own, zones, mark = split_start("pair_start_%d" % si, own, zones, [len(s)],
                                                    lambda sr, lr: [plan(sr, lr)])
                pairing.append((si, sem[0], own, zones, axes, plan))
                return mark
        return None

    def advance(value):
        mark = None
        while pairing:
            si, sem, own, zones, axes, plan = pairing.pop(0)
            s = REDUCE_SETS[si]
            own, recv = split_wait("pair_wait_%d" % si, own, zones, sem, [value], plan)
            parts = [pair_add(where, a, r, ax, "pair_add_" + g) for g, a, r, ax in zip(s, own, recv, axes)]
            zones = [lax.empty((3, q.shape[0]) + q.shape[2:], BF16) for q in parts]
            plan = _scatter_plan(len(s))
            sem, parts, zones, mark = split_start("scatter_start_%d" % si, parts, zones, [3 * len(s)],
                                                  lambda sr, lr: [plan(sr, lr)])
            started[si] = (sem[0], parts, zones, own, recv, axes, plan)
        return mark

    gains = {n: wts[n + "_g"] for n in GAINS}
    loss, dx, small = layer_step(x[0], p[0, 0].astype(BF16), loss_target[0], gains, _pad_row(fox_f_bias, LANES),
                                 hgrn_lb_logits, hgrn_norm_g, get_w, emit, advance, early)

    out_g, out_d, out_m, out_v = {}, {}, {}, {}
    after, crossing = None, None

    def finish(si, sem, halves, axes, mark):
        reduced, _ = split_wait("broadcast_wait_%d" % si, halves, [], sem, [mark], _broadcast_plan(axes, True))
        last = None
        for g, red in zip(REDUCE_SETS[si], reduced):
            for idx, n in enumerate(GROUPS[g][0]):
                res = adamw(red, idx, _as_block(n, wts[n]), _as_block(n, mom[n]), _as_block(n, var[n]), "adamw_" + n)
                out_g[n], out_d[n], out_m[n], out_v[n] = [_as_block(n, r) for r in res]
                last = res[1]
        return last

    for si, s in enumerate(REDUCE_SETS):
        sem, parts, zones, own, recv, axes, plan = started[si]
        _, zones = split_wait("scatter_wait_%d" % si, parts, zones, sem, [dx, after], plan)
        halves = [chip_add(where, a, r, z, ax, "chip_add_" + g) for g, a, r, z, ax in zip(s, own, recv, zones, axes)]
        plan = _broadcast_plan(axes, False)
        sem, halves, _, mark = split_start("broadcast_start_%d" % si, halves, [], [len(s)],
                                           lambda sr, lr: [plan(sr, lr)])
        if crossing is not None:
            after = finish(*crossing, mark)
        crossing = (si, sem[0], halves, axes)
    after = finish(*crossing, None)

    small_named = {n + "_g": small[n] for n in GAINS}
    small_named.update(fox_f_bias=small["fox_bias"][:, :HEADS], hgrn_lb_logits=small["lb_logits"],
                       hgrn_norm_g=small["norm_g"])
    g_small = allreduce_small(_pack_small(small_named), after)
    d_small, m_small, v_small = adamw_small(g_small, _pack_small(wts), _pack_small(mom), _pack_small(var))

    for dst, slab in ((out_g, g_small), (out_d, d_small), (out_m, m_small), (out_v, v_small)):
        dst.update(_unpack_small(slab))

    total = lax.psum(loss[0, 0], ("x", "y", "c"))
    return (total, dx[None], *[out_g[n] for n in WEIGHTS], *[out_d[n] for n in WEIGHTS],
            *[out_m[n] for n in WEIGHTS], *[out_v[n] for n in WEIGHTS])
```

```python
import functools

import jax
import jax.numpy as jnp
from jax import lax
from jax.experimental import pallas as pl
from jax.experimental.pallas import tpu as pltpu

F32 = jnp.float32
BF16 = jnp.bfloat16

D_MODEL = 2048
SEQ = 2048
D_FF = 5632
PLE_DIM = 256
HEADS = 8
HEAD_DIM = 128
WIDTH = HEADS * HEAD_DIM
CHUNK = 64
SUB = 16
HGRN_HEADS_PER_STEP = 2
NORM_EPS = 1e-6
MACARON_SCALE = 0.5
N_CHIPS = 4

ADAM_LR = 0.001
ADAM_B1 = 0.9
ADAM_B2 = 0.999
ADAM_EPS = 1e-08
ADAM_WD = 0.01
ADAM_STEP = 10

LANES = 128
VMEM_LIMIT = 56 * 1024 * 1024
NEG_BIG = -1e30
MESH = pl.DeviceIdType.MESH


def _pick(n, cands):
    for c in cands:
        if c <= n and n % c == 0:
            return c
    return n


def _params(sem, vmem=VMEM_LIMIT):
    return pltpu.CompilerParams(dimension_semantics=sem, vmem_limit_bytes=vmem)


def _sigmoid(x):
    return 1.0 / (1.0 + jnp.exp(-x))


def _silu(x):
    return x * _sigmoid(x)


def _silu_grad(x):
    s = _sigmoid(x)
    return s * (1.0 + x * (1.0 - s))


_DN = {"nn": (((1,), (0,)), ((), ())), "nt": (((1,), (1,)), ((), ())), "tn": (((0,), (0,)), ((), ()))}


def _mm(a, b, *, mode, grid, a_spec, b_spec, o_spec, out_shape, acc_shape, name, after=(), init=None, into=None):
    nk = grid[2]
    dn = _DN[mode]
    after = [m for m in after if m is not None]
    extra = ([init] if init is not None else []) + ([into] if into is not None else []) + after
    n_extra = len(extra)

    def body(a_ref, b_ref, *rest):
        o_ref, acc_ref = rest[n_extra:]
        k = pl.program_id(2)

        @pl.when(k == 0)
        def _():
            acc_ref[...] = jnp.zeros_like(acc_ref) if init is None else rest[0][...].astype(F32)

        acc_ref[...] += lax.dot_general(a_ref[...].astype(BF16), b_ref[...].astype(BF16), dn,
                                        preferred_element_type=F32)

        @pl.when(k == nk - 1)
        def _():
            o_ref[...] = acc_ref[...].astype(o_ref.dtype)

    anywhere = pl.BlockSpec(memory_space=pl.ANY)
    return pl.pallas_call(
        body, name=name, grid=grid,
        in_specs=[a_spec, b_spec] + ([o_spec] if init is not None else []) + [anywhere] * (n_extra - (init is not None)),
        out_specs=o_spec, out_shape=out_shape, scratch_shapes=[pltpu.VMEM(acc_shape, F32)],
        input_output_aliases={} if into is None else {2 + (init is not None): 0},
        compiler_params=_params(("parallel", "parallel", "arbitrary")),
    )(a, b, *extra)


def mm_nn_2d(a, b, out_dtype, name, after=()):
    m, kk = a.shape
    n = b.shape[1]
    tm, tn = _pick(m, (512, 256, 128)), _pick(n, (1024, 512, 256, 128))
    tk = _pick(kk, (5632, 2816, 2048, 1408, 1024, 512, 256, 128))
    return _mm(a, b, mode="nn", grid=(m // tm, n // tn, kk // tk),
               a_spec=pl.BlockSpec((tm, tk), lambda i, j, k: (i, k)),
               b_spec=pl.BlockSpec((tk, tn), lambda i, j, k: (k, j)),
               o_spec=pl.BlockSpec((tm, tn), lambda i, j, k: (i, j)),
               out_shape=jax.ShapeDtypeStruct((m, n), out_dtype), acc_shape=(tm, tn), name=name, after=after)


def mm_nt_2d(a, b, out_dtype, name, after=()):
    m, c = a.shape
    n = b.shape[0]
    tm, tn, tk = _pick(m, (512, 256, 128)), _pick(n, (1408, 1024, 512, 256, 128)), _pick(c, (2048, 1408, 1024, 512, 256, 128))
    return _mm(a, b, mode="nt", grid=(m // tm, n // tn, c // tk),
               a_spec=pl.BlockSpec((tm, tk), lambda i, j, k: (i, k)),
               b_spec=pl.BlockSpec((tn, tk), lambda i, j, k: (j, k)),
               o_spec=pl.BlockSpec((tm, tn), lambda i, j, k: (i, j)),
               out_shape=jax.ShapeDtypeStruct((m, n), out_dtype), acc_shape=(tm, tn), name=name, after=after)


def mm_tn_2d(a, b, out_dtype, name):
    c, m = a.shape
    n = b.shape[1]
    tm, tn, tk = _pick(m, (1408, 1024, 512, 256, 128)), _pick(n, (1024, 512, 256, 128)), _pick(c, (2048, 1024, 512, 256, 128))
    return _mm(a, b, mode="tn", grid=(m // tm, n // tn, c // tk),
               a_spec=pl.BlockSpec((tk, tm), lambda i, j, k: (k, i)),
               b_spec=pl.BlockSpec((tk, tn), lambda i, j, k: (k, j)),
               o_spec=pl.BlockSpec((tm, tn), lambda i, j, k: (i, j)),
               out_shape=jax.ShapeDtypeStruct((m, n), out_dtype), acc_shape=(tm, tn), name=name)


def mm_nn_col(a, w, out_dtype, name, slot0=0, total=None, into=None):
    m, kk = a.shape
    g, jn, _, ns = w.shape
    total = g if total is None else total
    tm, tk = _pick(m, (512, 256, 128)), _pick(kk, (2048, 1024, 512, 256, 128))
    return _mm(a, w, mode="nn", grid=(m // tm, g * jn, kk // tk),
               a_spec=pl.BlockSpec((tm, tk), lambda i, j, k: (i, k)),
               b_spec=pl.BlockSpec((None, None, tk, ns), lambda i, j, k: (j // jn, j % jn, k, 0)),
               o_spec=pl.BlockSpec((None, tm, ns), lambda i, j, k: (slot0 + j // jn, i, j % jn)),
               out_shape=jax.ShapeDtypeStruct((total, m, jn * ns), out_dtype), acc_shape=(tm, ns), name=name,
               into=into)


def mm_nt_col(a, w, out_dtype, name, after=(), slot0=0, init=None):
    _, m, _ = a.shape
    g, jn, kk, ns = w.shape
    kc = 2 if jn % 2 == 0 else 1
    per = jn // kc
    nk = g * per
    tm, tn = _pick(m, (512, 256, 128)), _pick(kk, (1024, 512, 256, 128))
    extra = ([init] if init is not None else []) + [v for v in after if v is not None]

    def body(a_ref, w_ref, *rest):
        o_ref, acc_ref = rest[len(extra):]
        k = pl.program_id(2)
        part = None
        for c in range(kc):
            prod = lax.dot_general(a_ref[:, c * ns:(c + 1) * ns].astype(BF16), w_ref[c].astype(BF16), _DN["nt"],
                                   preferred_element_type=F32)
            part = prod if part is None else part + prod

        @pl.when(k == 0)
        def _():
            acc_ref[...] = part if init is None else rest[0][...].astype(F32) + part

        @pl.when(k > 0)
        def _():
            acc_ref[...] += part

        @pl.when(k == nk - 1)
        def _():
            o_ref[...] = acc_ref[...].astype(o_ref.dtype)

    o_spec = pl.BlockSpec((tm, tn), lambda i, j, k: (i, j))
    anywhere = pl.BlockSpec(memory_space=pl.ANY)
    return pl.pallas_call(
        body, name=name, grid=(m // tm, kk // tn, nk),
        in_specs=[pl.BlockSpec((None, tm, kc * ns), lambda i, j, k: (slot0 + k // per, i, k % per)),
                  pl.BlockSpec((None, kc, tn, ns), lambda i, j, k: (k // per, k % per, j, 0))]
        + ([o_spec] if init is not None else []) + [anywhere] * (len(extra) - (init is not None)),
        out_specs=o_spec, out_shape=jax.ShapeDtypeStruct((m, kk), out_dtype),
        scratch_shapes=[pltpu.VMEM((tm, tn), F32)],
        compiler_params=_params(("parallel", "parallel", "arbitrary")),
    )(a, w, *extra)


def mm_tn_col(a, b, jn, out_dtype, name):
    c, kk = a.shape
    g, _, n = b.shape
    ns = n // jn
    tm, tk = _pick(kk, (512, 256, 128)), _pick(c, (2048, 1024, 512, 256, 128))
    return _mm(a, b, mode="tn", grid=(kk // tm, g * jn, c // tk),
               a_spec=pl.BlockSpec((tk, tm), lambda i, j, k: (k, i)),
               b_spec=pl.BlockSpec((None, tk, ns), lambda i, j, k: (j // jn, k, j % jn)),
               o_spec=pl.BlockSpec((None, None, tm, ns), lambda i, j, k: (j // jn, j % jn, i, 0)),
               out_shape=jax.ShapeDtypeStruct((g, jn, kk, ns), out_dtype), acc_shape=(tm, ns), name=name)


def _rstd(x):
    return lax.rsqrt(jnp.mean(x * x, axis=-1, keepdims=True) + NORM_EPS)


def _rms_bwd(x, g, dy):
    r = _rstd(x)
    xn = x * r
    dyg = dy * g
    dx = r * (dyg - xn * jnp.mean(dyg * xn, axis=-1, keepdims=True))
    return dx, jnp.sum(dy * xn, axis=0, keepdims=True)


def _row_tile(t):
    return _pick(t, (256, 128, 64, 32, 16, 8))


def norm_in(h, g, name):
    t, d = h.shape
    tr = _row_tile(t)

    def body(h_ref, g_ref, u_ref):
        x = h_ref[...]
        u_ref[...] = (x * _rstd(x) * g_ref[...]).astype(BF16)

    return pl.pallas_call(
        body, name=name, grid=(t // tr,),
        in_specs=[pl.BlockSpec((tr, d), lambda i: (i, 0)), pl.BlockSpec((1, d), lambda i: (0, 0))],
        out_specs=pl.BlockSpec((tr, d), lambda i: (i, 0)),
        out_shape=jax.ShapeDtypeStruct((t, d), BF16), compiler_params=_params(("parallel",)),
    )(h, g)


def resid_post(h, y, g, scale, name):
    t, d = h.shape
    tr = _row_tile(t)

    def body(h_ref, y_ref, g_ref, o_ref):
        yv = y_ref[...]
        o_ref[...] = h_ref[...] + scale * (yv * _rstd(yv) * g_ref[...])

    row = pl.BlockSpec((tr, d), lambda i: (i, 0))
    return pl.pallas_call(
        body, name=name, grid=(t // tr,), in_specs=[row, row, pl.BlockSpec((1, d), lambda i: (0, 0))],
        out_specs=row, out_shape=jax.ShapeDtypeStruct((t, d), F32), compiler_params=_params(("parallel",)),
    )(h, y, g)


def post_bwd(dh, y, g, scale, name):
    t, d = dh.shape
    tr = _row_tile(t)

    def body(dh_ref, y_ref, g_ref, dy_ref, dg_ref):
        @pl.when(pl.program_id(0) == 0)
        def _():
            dg_ref[...] = jnp.zeros_like(dg_ref)

        dx, dg = _rms_bwd(y_ref[...], g_ref[...], scale * dh_ref[...])
        dy_ref[...] = dx.astype(BF16)
        dg_ref[...] += dg

    row = pl.BlockSpec((tr, d), lambda i: (i, 0))
    vec = pl.BlockSpec((1, d), lambda i: (0, 0))
    return pl.pallas_call(
        body, name=name, grid=(t // tr,), in_specs=[row, row, vec], out_specs=[row, vec],
        out_shape=[jax.ShapeDtypeStruct((t, d), BF16), jax.ShapeDtypeStruct((1, d), F32)],
        compiler_params=_params(("arbitrary",)),
    )(dh, y, g)


def pre_bwd(dh, h, g, dus, name, after=()):
    t, d = dh.shape
    tr = _row_tile(t)
    n_du = len(dus)
    after = [m for m in after if m is not None]

    def body(*refs):
        dh_ref, h_ref, g_ref = refs[:3]
        du_refs = refs[3:3 + n_du]
        o_ref, dg_ref = refs[3 + n_du + len(after):]

        @pl.when(pl.program_id(0) == 0)
        def _():
            dg_ref[...] = jnp.zeros_like(dg_ref)

        du = du_refs[0][...]
        for r in du_refs[1:]:
            du = du + r[...]
        dx, dg = _rms_bwd(h_ref[...], g_ref[...], du)
        o_ref[...] = dh_ref[...] + dx
        dg_ref[...] += dg

    row = pl.BlockSpec((tr, d), lambda i: (i, 0))
    vec = pl.BlockSpec((1, d), lambda i: (0, 0))
    return pl.pallas_call(
        body, name=name, grid=(t // tr,),
        in_specs=[row, row, vec] + [row] * n_du + [pl.BlockSpec(memory_space=pl.ANY)] * len(after),
        out_specs=[row, vec], out_shape=[jax.ShapeDtypeStruct((t, d), F32), jax.ShapeDtypeStruct((1, d), F32)],
        compiler_params=_params(("arbitrary",)),
    )(dh, h, g, *dus, *after)


def _ew_tiles(t, f):
    return _pick(t, (256, 128, 64, 32, 16, 8)), _pick(f, (1408, 1024, 512, 256, 128))


def swiglu_act(gu, name):
    _, t, f = gu.shape
    tr, tc = _ew_tiles(t, f)

    def body(gu_ref, o_ref):
        o_ref[...] = (_silu(gu_ref[0]) * gu_ref[1]).astype(BF16)

    return pl.pallas_call(
        body, name=name, grid=(t // tr, f // tc),
        in_specs=[pl.BlockSpec((2, tr, tc), lambda i, j: (0, i, j))],
        out_specs=pl.BlockSpec((tr, tc), lambda i, j: (i, j)),
        out_shape=jax.ShapeDtypeStruct((t, f), BF16), compiler_params=_params(("parallel", "parallel")),
    )(gu)


def swiglu_bwd(dact, gu, name):
    _, t, f = gu.shape
    tr, tc = _ew_tiles(t, f)

    def body(da_ref, gu_ref, o_ref):
        da = da_ref[...]
        gate = gu_ref[0]
        o_ref[0] = (da * gu_ref[1] * _silu_grad(gate)).astype(BF16)
        o_ref[1] = (da * _silu(gate)).astype(BF16)

    return pl.pallas_call(
        body, name=name, grid=(t // tr, f // tc),
        in_specs=[pl.BlockSpec((tr, tc), lambda i, j: (i, j)), pl.BlockSpec((2, tr, tc), lambda i, j: (0, i, j))],
        out_specs=pl.BlockSpec((2, tr, tc), lambda i, j: (0, i, j)),
        out_shape=jax.ShapeDtypeStruct((2, t, f), BF16), compiler_params=_params(("parallel", "parallel")),
    )(dact, gu)


def _col_blocks():
    w = WIDTH // LANES
    return dict(q_a=0, k_a=w, v_a=2 * w, q_b=3 * w, f_b=4 * w, i_b=5 * w, g_b=6 * w, gate_a=7 * w,
                gate_b=7 * w + D_MODEL // LANES)


def _tri(n, lower):
    r = lax.broadcasted_iota(jnp.int32, (n, n), 0)
    c = lax.broadcasted_iota(jnp.int32, (n, n), 1)
    return jnp.where((r >= c) if lower else (r <= c), 1.0, 0.0).astype(F32)


def _dot_hi(a, b):
    return jnp.dot(a, b, precision=lax.Precision.HIGHEST, preferred_element_type=F32)


def fox_prep(fa, bias, name):
    t, w = fa.shape
    tb = _pick(t, (256, 128, 64))

    def body(fa_ref, b_ref, c_ref, carry_ref):
        @pl.when(pl.program_id(0) == 0)
        def _():
            carry_ref[...] = jnp.zeros_like(carry_ref)

        z = fa_ref[...] + b_ref[...]
        lf = jnp.minimum(z, 0.0) - jnp.log(1.0 + jnp.exp(-jnp.abs(z)))
        c = _dot_hi(_tri(tb, True), lf) + carry_ref[...]
        c_ref[...] = c
        carry_ref[...] = carry_ref[...] + jnp.sum(lf, axis=0, keepdims=True)

    return pl.pallas_call(
        body, name=name, grid=(t // tb,),
        in_specs=[pl.BlockSpec((tb, w), lambda i: (i, 0)), pl.BlockSpec((1, w), lambda i: (0, 0))],
        out_specs=pl.BlockSpec((tb, w), lambda i: (i, 0)),
        out_shape=jax.ShapeDtypeStruct((t, w), F32), scratch_shapes=[pltpu.VMEM((1, w), F32)],
        compiler_params=_params(("arbitrary",)),
    )(fa, bias)


def fox_post_bwd(dc, fa, bias, name):
    t, w = fa.shape
    tb = _pick(t, (256, 128, 64))
    nb = t // tb

    def body(dc_ref, fa_ref, b_ref, dfa_ref, db_ref, carry_ref):
        @pl.when(pl.program_id(0) == 0)
        def _():
            carry_ref[...] = jnp.zeros_like(carry_ref)
            db_ref[...] = jnp.zeros_like(db_ref)

        dcv = dc_ref[...]
        dlf = _dot_hi(_tri(tb, False), dcv) + carry_ref[...]
        z = fa_ref[...] + b_ref[...]
        dz = dlf * _sigmoid(-z)
        dfa_ref[...] = dz.astype(BF16)
        db_ref[...] += jnp.sum(dz, axis=0, keepdims=True)
        carry_ref[...] = carry_ref[...] + jnp.sum(dcv, axis=0, keepdims=True)

    rev = pl.BlockSpec((tb, w), lambda i: (nb - 1 - i, 0))
    vec = pl.BlockSpec((1, w), lambda i: (0, 0))
    return pl.pallas_call(
        body, name=name, grid=(nb,), in_specs=[rev, rev, vec], out_specs=[rev, vec],
        out_shape=[jax.ShapeDtypeStruct((t, w), BF16), jax.ShapeDtypeStruct((1, w), F32)],
        scratch_shapes=[pltpu.VMEM((1, w), F32)], compiler_params=_params(("arbitrary",)),
    )(dc, fa, bias)


def _fox_probs(q_ref, k_ref, cc_ref, cr_ref, qi, tq, t):
    scale = HEAD_DIM ** -0.5
    s = lax.dot_general(q_ref[...].astype(BF16), k_ref[...].astype(BF16), _DN["nt"], preferred_element_type=F32)
    logits = s * scale + cc_ref[...] - cr_ref[...]
    qpos = qi * tq + lax.broadcasted_iota(jnp.int32, (tq, t), 0)
    kpos = lax.broadcasted_iota(jnp.int32, (tq, t), 1)
    logits = jnp.where(kpos <= qpos, logits, NEG_BIG)
    m = jnp.max(logits, axis=-1, keepdims=True)
    p = jnp.exp(logits - m)
    return p / jnp.sum(p, axis=-1, keepdims=True)


FOX_SEGMENTS = 4


def _fox_segments(t):
    tq = _pick(t, (256, 128))
    nseg = min(FOX_SEGMENTS, t // tq)
    return tq, nseg, t // tq // nseg


def _fox_specs(t, q0, kt, tq):
    cb = _col_blocks()
    dh = HEAD_DIM
    return [pl.BlockSpec((tq, dh), lambda h, i: (q0 + i, cb["q_a"] + h)),
            pl.BlockSpec((kt, dh), lambda h, i: (0, cb["k_a"] + h)),
            pl.BlockSpec((kt, dh), lambda h, i: (0, cb["v_a"] + h)),
            pl.BlockSpec((None, tq, 1), lambda h, i: (h, q0 + i, 0)),
            pl.BlockSpec((None, 1, kt), lambda h, i: (h, 0, 0))]


def fox_fwd(proj, c_col, c_row, name):
    t = proj.shape[0]
    tq, nseg, nq = _fox_segments(t)
    dh = HEAD_DIM

    def segment(out, r):
        q0, kt = r * nq, (r + 1) * nq * tq

        def body(q_ref, k_ref, v_ref, cc_ref, cr_ref, prev_ref, o_ref):
            p = _fox_probs(q_ref, k_ref, cc_ref, cr_ref, q0 + pl.program_id(1), tq, kt)
            o_ref[...] = jnp.dot(p.astype(BF16), v_ref[...].astype(BF16), preferred_element_type=F32).astype(BF16)

        return pl.pallas_call(
            body, name="%s_%d" % (name, r), grid=(HEADS, nq),
            in_specs=_fox_specs(t, q0, kt, tq) + [pl.BlockSpec(memory_space=pl.ANY)],
            out_specs=pl.BlockSpec((tq, dh), lambda h, i: (q0 + i, h)),
            out_shape=jax.ShapeDtypeStruct((t, WIDTH), BF16), input_output_aliases={5: 0},
            compiler_params=_params(("parallel", "parallel")),
        )(proj, proj, proj, c_col, c_row, out)

    out = lax.empty((t, WIDTH), BF16)
    for r in range(nseg):
        out = segment(out, r)
    return out


def fox_bwd(proj, c_col, c_row, do, name):
    t = proj.shape[0]
    tq, nseg, nq = _fox_segments(t)
    dh = HEAD_DIM
    scale = HEAD_DIM ** -0.5

    def segment(acc, r):
        q0, kt = r * nq, (r + 1) * nq * tq

        def body(q_ref, k_ref, v_ref, cc_ref, cr_ref, do_ref, dqp_ref, dkp_ref, dvp_ref, dccp_ref, dcrp_ref,
                 dq_ref, dk_ref, dv_ref, dcc_ref, dcr_ref):
            @pl.when(pl.program_id(1) == 0)
            def _():
                dk_ref[...] = dkp_ref[...]
                dv_ref[...] = dvp_ref[...]
                dcr_ref[...] = dcrp_ref[...]

            p = _fox_probs(q_ref, k_ref, cc_ref, cr_ref, q0 + pl.program_id(1), tq, kt)
            dov = do_ref[...].astype(BF16)
            kb = k_ref[...].astype(BF16)
            dv_ref[...] += lax.dot_general(p.astype(BF16), dov, _DN["tn"], preferred_element_type=F32)
            dp = lax.dot_general(dov, v_ref[...].astype(BF16), _DN["nt"], preferred_element_type=F32)
            ds = p * (dp - jnp.sum(p * dp, axis=-1, keepdims=True))
            dcc_ref[...] = jnp.sum(ds, axis=-1, keepdims=True)
            dcr_ref[...] -= jnp.sum(ds, axis=0, keepdims=True)
            dss = (ds * scale).astype(BF16)
            dq_ref[...] = jnp.dot(dss, kb, preferred_element_type=F32).astype(BF16)
            dk_ref[...] += lax.dot_general(dss, q_ref[...].astype(BF16), _DN["tn"], preferred_element_type=F32)

        rows = pl.BlockSpec((tq, dh), lambda h, i: (q0 + i, h))
        keys = pl.BlockSpec((kt, dh), lambda h, i: (0, h))
        col = pl.BlockSpec((None, tq, 1), lambda h, i: (h, q0 + i, 0))
        row = pl.BlockSpec((None, 1, kt), lambda h, i: (h, 0, 0))
        anywhere = pl.BlockSpec(memory_space=pl.ANY)
        return pl.pallas_call(
            body, name="%s_%d" % (name, r), grid=(HEADS, nq),
            in_specs=_fox_specs(t, q0, kt, tq) + [rows, anywhere, keys, keys, anywhere, row],
            out_specs=[rows, keys, keys, col, row],
            out_shape=[jax.ShapeDtypeStruct(a.shape, a.dtype) for a in acc],
            input_output_aliases={6 + k: k for k in range(5)},
            compiler_params=_params(("parallel", "arbitrary")),
        )(proj, proj, proj, c_col, c_row, do, *acc)

    acc = [lax.empty((t, WIDTH), BF16), jnp.zeros((t, WIDTH), F32), jnp.zeros((t, WIDTH), F32),
           lax.empty((HEADS, t, 1), F32), jnp.zeros((HEADS, 1, t), F32)]
    for r in range(nseg):
        acc = segment(acc, r)
    return acc


def _lower_bound(lg_ref):
    l0 = lg_ref[0:1, :]
    l1 = lg_ref[1:2, :]
    m = jnp.maximum(l0, l1)
    e0 = jnp.exp(l0 - m)
    e1 = jnp.exp(l1 - m)
    return e0 / (e0 + e1)


def _hgrn_inputs(qb_ref, fb_ref, lg_ref, q_s, k_s, cum_s):
    lb = _lower_bound(lg_ref)
    sig = _sigmoid(fb_ref[...])
    f = lb + (1.0 - lb) * sig
    q_s[...] = _silu(qb_ref[...])
    k_s[...] = 1.0 - f
    cum_s[...] = _dot_hi(_tri(CHUNK, True), jnp.log(f))
    return lb, sig, f


def _boundary(cum_s, a):
    if a == 0:
        return jnp.zeros((1, HEAD_DIM), F32)
    return cum_s[pl.ds(SUB * a - 1, 1), :]


def _hgrn_scores(q_s, k_s, cum_s):
    cum = cum_s[...]
    kk = k_s[...]
    lane = lax.broadcasted_iota(jnp.int32, (SUB, CHUNK), 1)
    row = lax.broadcasted_iota(jnp.int32, (SUB, 1), 0)
    blocks = []
    for a in range(CHUNK // SUB):
        rows = pl.ds(SUB * a, SUB)
        ca = _boundary(cum_s, a)
        cum_a = cum_s[rows, :]
        q_a = q_s[rows, :]
        qa = q_a * jnp.exp(cum_a - ca)
        ka = kk * jnp.exp(jnp.minimum(ca - cum, 0.0))
        blk = lax.dot_general(qa, ka, _DN["nt"], preferred_element_type=F32)
        blk = jnp.where(lane < SUB * a, blk, 0.0)
        for s in range(SUB):
            r = SUB * a + s
            e = jnp.exp(jnp.minimum(cum_a - cum_s[pl.ds(r, 1), :], 0.0))
            col = jnp.sum(q_a * k_s[pl.ds(r, 1), :] * e, axis=-1, keepdims=True)
            col = jnp.where(row >= s, col, 0.0)
            blk = jnp.where(lane == r, col, blk)
        blocks.append(blk)
    return jnp.concatenate(blocks, axis=0)


def hgrn_fwd(proj, lb_logits, name, after=()):
    after = [m for m in after if m is not None]
    t = proj.shape[0]
    n = t // CHUNK
    cb = _col_blocks()
    dh = HEAD_DIM

    hb = HGRN_HEADS_PER_STEP
    w = hb * dh

    def one_head(qb_ref, fb_ref, ib_ref, lg_ref, o_ref, st_ref, a_ref, state, q_s, k_s, cum_s):
        _hgrn_inputs(qb_ref, fb_ref, lg_ref, q_s, k_s, cum_s)
        st = state[...]
        st_ref[...] = st
        cum = cum_s[...]
        v = ib_ref[...]
        qe = q_s[...] * jnp.exp(cum)
        inter = lax.dot_general(qe, st, _DN["nt"], preferred_element_type=F32)
        a_mat = _hgrn_scores(q_s, k_s, cum_s)
        a_ref[...] = a_mat
        o_ref[...] = inter + jnp.dot(a_mat, v, preferred_element_type=F32)
        last = cum_s[pl.ds(CHUNK - 1, 1), :]
        kd = k_s[...] * jnp.exp(last - cum)
        state[...] = st * jnp.exp(last) + lax.dot_general(v, kd, _DN["tn"], preferred_element_type=F32)

    def body(qb_ref, fb_ref, ib_ref, lg_ref, *rest):
        o_ref, st_ref, a_ref = rest[len(after):len(after) + 3]
        scratch = rest[len(after) + 3:]

        @pl.when(pl.program_id(1) == 0)
        def _():
            for j in range(hb):
                scratch[4 * j][...] = jnp.zeros((dh, dh), F32)

        for j in range(hb):
            cols = (slice(None), pl.ds(j * dh, dh))
            one_head(qb_ref.at[cols], fb_ref.at[cols], ib_ref.at[cols], lg_ref.at[cols], o_ref.at[cols],
                     st_ref.at[j], a_ref.at[j], *scratch[4 * j:4 * j + 4])

    blk = lambda off: pl.BlockSpec((CHUNK, w), lambda h, i: (i, off // hb + h))
    return pl.pallas_call(
        body, name=name, grid=(HEADS // hb, n),
        in_specs=[blk(cb["q_b"]), blk(cb["f_b"]), blk(cb["i_b"]), pl.BlockSpec((2, w), lambda h, i: (0, h))]
        + [pl.BlockSpec(memory_space=pl.ANY)] * len(after),
        out_specs=[pl.BlockSpec((CHUNK, w), lambda h, i: (i, h)),
                   pl.BlockSpec((hb, None, dh, dh), lambda h, i: (h, i, 0, 0)),
                   pl.BlockSpec((hb, None, CHUNK, CHUNK), lambda h, i: (h, i, 0, 0))],
        out_shape=[jax.ShapeDtypeStruct((t, WIDTH), F32), jax.ShapeDtypeStruct((HEADS, n, dh, dh), F32),
                   jax.ShapeDtypeStruct((HEADS, n, CHUNK, CHUNK), F32)],
        scratch_shapes=([pltpu.VMEM((dh, dh), F32)] + [pltpu.VMEM((CHUNK, dh), F32)] * 3) * hb,
        compiler_params=_params(("parallel", "arbitrary")),
    )(proj, proj, proj, lb_logits, *after)


def hgrn_bwd(proj, lb_logits, states, scores, do, name):
    t = proj.shape[0]
    n = t // CHUNK
    cb = _col_blocks()
    dh = HEAD_DIM
    nsub = CHUNK // SUB

    hb = HGRN_HEADS_PER_STEP
    w = hb * dh

    def one_head(qb_ref, fb_ref, ib_ref, lg_ref, st_ref, a_ref, do_ref, dqb_ref, dfb_ref, dib_ref, dlb_ref,
                 dstate, q_s, k_s, cum_s, da_s, dq_s, dk_s):
        lb, sig, f = _hgrn_inputs(qb_ref, fb_ref, lg_ref, q_s, k_s, cum_s)
        st = st_ref[...]
        dst = dstate[...]
        cum = cum_s[...]
        q = q_s[...]
        kk = k_s[...]
        v = ib_ref[...]
        dov = do_ref[...]
        e_cum = jnp.exp(cum)
        qe = q * e_cum
        last = cum_s[pl.ds(CHUNK - 1, 1), :]
        e_last = jnp.exp(last)
        e_tail = jnp.exp(last - cum)
        kd = kk * e_tail

        a_mat = a_ref[...]
        tri = _tri(CHUNK, True)
        da_s[...] = lax.dot_general(dov, v, _DN["nt"], preferred_element_type=F32) * tri
        dv = (lax.dot_general(a_mat, dov, _DN["tn"], preferred_element_type=F32)
              + lax.dot_general(kd, dst, _DN["nt"], preferred_element_type=F32))
        dk_state = jnp.dot(v, dst, preferred_element_type=F32) * e_tail
        dq_inter = jnp.dot(dov, st, preferred_element_type=F32) * e_cum
        dstate[...] = dst * e_last + lax.dot_general(dov, qe, _DN["tn"], preferred_element_type=F32)

        lane = lax.broadcasted_iota(jnp.int32, (SUB, CHUNK), 1)
        row = lax.broadcasted_iota(jnp.int32, (SUB, 1), 0)
        dk_s[...] = jnp.zeros_like(dk_s)
        for a in range(nsub):
            rows = pl.ds(SUB * a, SUB)
            ca = _boundary(cum_s, a)
            cum_a = cum_s[rows, :]
            q_a = q_s[rows, :]
            ea = jnp.exp(cum_a - ca)
            eb = jnp.exp(jnp.minimum(ca - cum, 0.0))
            da_a = da_s[rows, :]
            da_off = jnp.where(lane < SUB * a, da_a, 0.0)
            dq_a = ea * jnp.dot(da_off, kk * eb, preferred_element_type=F32)
            dk_s[...] += eb * lax.dot_general(da_off, q_a * ea, _DN["tn"], preferred_element_type=F32)
            dk_rows = jnp.zeros((SUB, dh), F32)
            for s in range(SUB):
                r = SUB * a + s
                e = jnp.exp(jnp.minimum(cum_a - cum_s[pl.ds(r, 1), :], 0.0))
                dcol = jnp.sum(jnp.where(lane == r, da_a, 0.0), axis=-1, keepdims=True)
                dcol = jnp.where(row >= s, dcol, 0.0)
                w = dcol * e
                dq_a = dq_a + w * k_s[pl.ds(r, 1), :]
                dk_rows = jnp.where(row == s, jnp.sum(w * q_a, axis=0, keepdims=True), dk_rows)
            dq_s[rows, :] = dq_a
            dk_s[rows, :] += dk_rows

        dq = dq_inter + dq_s[...]
        dk = dk_s[...] + dk_state
        d_last = (jnp.sum(dst * st, axis=0, keepdims=True) * e_last
                  + jnp.sum(kk * dk_state, axis=0, keepdims=True))
        rowc = lax.broadcasted_iota(jnp.int32, (CHUNK, 1), 0)
        dcum = q * dq - kk * dk + jnp.where(rowc == CHUNK - 1, d_last, 0.0)
        dg = _dot_hi(_tri(CHUNK, False), dcum)
        df = dg / f - dk
        dqb_ref[...] = (dq * _silu_grad(qb_ref[...])).astype(BF16)
        dfb_ref[...] = (df * (1.0 - lb) * sig * (1.0 - sig)).astype(BF16)
        dib_ref[...] = dv.astype(BF16)
        dlb_ref[...] += jnp.sum(df * (1.0 - sig), axis=0, keepdims=True)

    def body(qb_ref, fb_ref, ib_ref, lg_ref, st_ref, a_ref, do_ref, dqb_ref, dfb_ref, dib_ref, dlb_ref, *scratch):
        @pl.when(pl.program_id(1) == 0)
        def _():
            for j in range(hb):
                scratch[7 * j][...] = jnp.zeros((dh, dh), F32)
            dlb_ref[...] = jnp.zeros_like(dlb_ref)

        for j in range(hb):
            cols = (slice(None), pl.ds(j * dh, dh))
            one_head(qb_ref.at[cols], fb_ref.at[cols], ib_ref.at[cols], lg_ref.at[cols], st_ref.at[j], a_ref.at[j],
                     do_ref.at[cols], dqb_ref.at[cols], dfb_ref.at[cols], dib_ref.at[cols], dlb_ref.at[cols],
                     *scratch[7 * j:7 * j + 7])

    blk = lambda off: pl.BlockSpec((CHUNK, w), lambda h, i: (n - 1 - i, off // hb + h))
    out_blk = pl.BlockSpec((CHUNK, w), lambda h, i: (n - 1 - i, h))
    return pl.pallas_call(
        body, name=name, grid=(HEADS // hb, n),
        in_specs=[blk(cb["q_b"]), blk(cb["f_b"]), blk(cb["i_b"]), pl.BlockSpec((2, w), lambda h, i: (0, h)),
                  pl.BlockSpec((hb, None, dh, dh), lambda h, i: (h, n - 1 - i, 0, 0)),
                  pl.BlockSpec((hb, None, CHUNK, CHUNK), lambda h, i: (h, n - 1 - i, 0, 0)), out_blk],
        out_specs=[out_blk, out_blk, out_blk, pl.BlockSpec((1, w), lambda h, i: (0, h))],
        out_shape=[jax.ShapeDtypeStruct((t, WIDTH), BF16)] * 3 + [jax.ShapeDtypeStruct((1, WIDTH), F32)],
        scratch_shapes=([pltpu.VMEM((dh, dh), F32)] + [pltpu.VMEM((CHUNK, dh), F32)] * 3
                        + [pltpu.VMEM((CHUNK, CHUNK), F32)] + [pltpu.VMEM((CHUNK, dh), F32)] * 2) * hb,
        compiler_params=_params(("parallel", "arbitrary")),
    )(proj, proj, proj, lb_logits, states, scores, do)


def lb_bwd(dlb, lb_logits, name):
    def body(dlb_ref, lg_ref, o_ref):
        p0 = _lower_bound(lg_ref)
        d0 = dlb_ref[...] * p0 * (1.0 - p0)
        o_ref[0:1, :] = d0
        o_ref[1:2, :] = -d0

    return pl.pallas_call(body, name=name, out_shape=jax.ShapeDtypeStruct(lb_logits.shape, F32))(dlb, lb_logits)


def gnorm_fwd(o_raw, proj, norm_g, name, after=()):
    after = [m for m in after if m is not None]
    t = o_raw.shape[0]
    tr = _row_tile(t)
    cb = _col_blocks()
    dh = HEAD_DIM

    def body(o_ref, gb_ref, g_ref, *rest):
        x = o_ref[...]
        rest[-1][...] = (x * _rstd(x) * g_ref[...] * _silu(gb_ref[...])).astype(BF16)

    return pl.pallas_call(
        body, name=name, grid=(t // tr, HEADS),
        in_specs=[pl.BlockSpec((tr, dh), lambda i, h: (i, h)), pl.BlockSpec((tr, dh), lambda i, h: (i, cb["g_b"] + h)),
                  pl.BlockSpec((1, dh), lambda i, h: (0, 0))] + [pl.BlockSpec(memory_space=pl.ANY)] * len(after),
        out_specs=pl.BlockSpec((tr, dh), lambda i, h: (i, h)),
        out_shape=jax.ShapeDtypeStruct((t, WIDTH), BF16), compiler_params=_params(("parallel", "parallel")),
    )(o_raw, proj, norm_g, *after)


def gnorm_bwd(dy, o_raw, proj, norm_g, name):
    t = o_raw.shape[0]
    tr = _row_tile(t)
    cb = _col_blocks()
    dh = HEAD_DIM

    def body(dy_ref, o_ref, gb_ref, g_ref, do_ref, dgb_ref, dg_ref):
        @pl.when((pl.program_id(0) == 0) & (pl.program_id(1) == 0))
        def _():
            dg_ref[...] = jnp.zeros_like(dg_ref)

        x = o_ref[...]
        gb = gb_ref[...]
        dyv = dy_ref[...]
        g = g_ref[...]
        dx, dg = _rms_bwd(x, g, dyv * _silu(gb))
        do_ref[...] = dx
        dgb_ref[...] = (dyv * (x * _rstd(x) * g) * _silu_grad(gb)).astype(BF16)
        dg_ref[...] += dg

    hb = pl.BlockSpec((tr, dh), lambda i, h: (i, h))
    vec = pl.BlockSpec((1, dh), lambda i, h: (0, 0))
    return pl.pallas_call(
        body, name=name, grid=(t // tr, HEADS),
        in_specs=[hb, hb, pl.BlockSpec((tr, dh), lambda i, h: (i, cb["g_b"] + h)), vec],
        out_specs=[hb, hb, vec],
        out_shape=[jax.ShapeDtypeStruct((t, WIDTH), F32), jax.ShapeDtypeStruct((t, WIDTH), BF16),
                   jax.ShapeDtypeStruct((1, dh), F32)],
        compiler_params=_params(("arbitrary", "arbitrary")),
    )(dy, o_raw, proj, norm_g)


def merge_fwd(proj, y, name):
    _, t, d = y.shape
    tr = _row_tile(t)
    tc = _pick(d, (1024, 512, 256, 128))
    cb = _col_blocks()
    ga, gb = cb["gate_a"] * LANES // tc, cb["gate_b"] * LANES // tc

    def body(ga_ref, gb_ref, y_ref, o_ref):
        o_ref[...] = (_sigmoid(ga_ref[...]) * y_ref[0] + _sigmoid(gb_ref[...]) * y_ref[1]).astype(BF16)

    return pl.pallas_call(
        body, name=name, grid=(t // tr, d // tc),
        in_specs=[pl.BlockSpec((tr, tc), lambda i, j: (i, ga + j)), pl.BlockSpec((tr, tc), lambda i, j: (i, gb + j)),
                  pl.BlockSpec((2, tr, tc), lambda i, j: (0, i, j))],
        out_specs=pl.BlockSpec((tr, tc), lambda i, j: (i, j)),
        out_shape=jax.ShapeDtypeStruct((t, d), BF16), compiler_params=_params(("parallel", "parallel")),
    )(proj, proj, y)


def merge_bwd(dm, proj, y, name):
    _, t, d = y.shape
    tr = _row_tile(t)
    tc = _pick(d, (1024, 512, 256, 128))
    cb = _col_blocks()
    ga, gb = cb["gate_a"] * LANES // tc, cb["gate_b"] * LANES // tc

    def body(dm_ref, ga_ref, gb_ref, y_ref, dg_ref, dy_ref):
        dmv = dm_ref[...]
        for idx, g_ref in enumerate((ga_ref, gb_ref)):
            s = _sigmoid(g_ref[...])
            dg_ref[idx] = (dmv * y_ref[idx] * s * (1.0 - s)).astype(BF16)
            dy_ref[idx] = (dmv * s).astype(BF16)

    pair = pl.BlockSpec((2, tr, tc), lambda i, j: (0, i, j))
    return pl.pallas_call(
        body, name=name, grid=(t // tr, d // tc),
        in_specs=[pl.BlockSpec((tr, tc), lambda i, j: (i, j)), pl.BlockSpec((tr, tc), lambda i, j: (i, ga + j)),
                  pl.BlockSpec((tr, tc), lambda i, j: (i, gb + j)), pair],
        out_specs=[pair, pair],
        out_shape=[jax.ShapeDtypeStruct((2, t, d), BF16)] * 2, compiler_params=_params(("parallel", "parallel")),
    )(dm, proj, proj, y)


def ple_tail(h, a, b, g, target, name):
    t, d = h.shape
    tr = _row_tile(t)

    def body(h_ref, a_ref, b_ref, g_ref, t_ref, loss_ref, dh_ref, da_ref, db_ref, dg_ref):
        @pl.when(pl.program_id(0) == 0)
        def _():
            loss_ref[...] = jnp.zeros_like(loss_ref)
            dg_ref[...] = jnp.zeros_like(dg_ref)

        s = _sigmoid(a_ref[...])
        bv = b_ref[...]
        z = s * bv
        gv = g_ref[...]
        err = h_ref[...] + z * _rstd(z) * gv - t_ref[...]
        loss_ref[...] += 0.5 * jnp.sum(jnp.sum(err * err, axis=-1, keepdims=True), axis=0, keepdims=True) / d
        dh = err / d
        dh_ref[...] = dh
        dz, dg = _rms_bwd(z, gv, dh)
        da_ref[...] = (dz * bv * s * (1.0 - s)).astype(BF16)
        db_ref[...] = (dz * s).astype(BF16)
        dg_ref[...] += dg

    row = pl.BlockSpec((tr, d), lambda i: (i, 0))
    vec = pl.BlockSpec((1, d), lambda i: (0, 0))
    return pl.pallas_call(
        body, name=name, grid=(t // tr,), in_specs=[row, row, row, vec, row],
        out_specs=[pl.BlockSpec((1, 1), lambda i: (0, 0)), row, row, row, vec],
        out_shape=[jax.ShapeDtypeStruct((1, 1), F32), jax.ShapeDtypeStruct((t, d), F32),
                   jax.ShapeDtypeStruct((t, d), BF16), jax.ShapeDtypeStruct((t, d), BF16),
                   jax.ShapeDtypeStruct((1, d), F32)],
        compiler_params=_params(("arbitrary",)),
    )(h, a, b, g, target)


def _ffn_fwd(h, pre_g, post_g, get_w, idx, tag):
    u = norm_in(h, pre_g, tag + "_norm")
    gu = None
    for i, key in enumerate(GATE_UP_KEYS["gu" + idx]):
        w = get_w(key, h if gu is None else gu)
        gu = mm_nn_col(u, w, F32, "%s_gate_up_%d" % (tag, i), slot0=i, total=2, into=gu)
    act = swiglu_act(gu, tag + "_act")
    y = mm_nn_2d(act, get_w("down" + idx, gu), F32, tag + "_down")
    out = resid_post(h, y, post_g, MACARON_SCALE, tag + "_out")
    return out, (h, u, gu, act, y)


def _ffn_bwd(dh, saved, pre_g, post_g, get_w, emit, advance, idx, tag):
    h, u, gu, act, y = saved
    dy, d_post = post_bwd(dh, y, post_g, MACARON_SCALE, tag + "_post_bwd")
    m1 = emit("down" + idx, mm_tn_2d(act, dy, F32, tag + "_dw_down"))
    dact = mm_nt_2d(dy, get_w("down" + idx), F32, tag + "_dact", after=[m1])
    m2 = advance(dact)
    dgu = swiglu_bwd(dact, gu, tag + "_act_bwd")
    m3 = emit("gu" + idx, mm_tn_col(u, dgu, N_CHIPS, F32, tag + "_dw_gate_up"))
    du = None
    for i, key in enumerate(GATE_UP_KEYS["gu" + idx]):
        du = mm_nt_col(dgu, get_w(key), F32, "%s_du_%d" % (tag, i), after=[m2, m3] if du is None else (),
                       slot0=i, init=du)
    m4 = advance(du)
    dh_in, d_pre = pre_bwd(dh, h, pre_g, [du], tag + "_pre_bwd", after=[m4])
    return dh_in, d_pre, d_post


def _heads_col(a):
    t = a.shape[0]
    at = a[:, :HEADS].T
    return at.reshape(HEADS, t, 1), at.reshape(HEADS, 1, t)


def layer_step(x, p, target, gains, fox_bias, lb_logits, norm_g, get_w, emit, advance, early):
    t = x.shape[0]
    h1, s1 = _ffn_fwd(x, gains["ffn1_pre"], gains["ffn1_post"], get_w, "1", "ffn1")

    u2 = norm_in(h1, gains["mix_pre"], "mix_norm")
    proj = mm_nt_2d(u2, get_w("in_main", h1), F32, "mix_in")
    fa = mm_nt_2d(u2, get_w("in_fa"), F32, "mix_in_fa")
    mark = early("proj", proj)
    c = fox_prep(fa, fox_bias, "fox_prep")
    c_col, c_row = _heads_col(c)
    o_a = fox_fwd(proj, c_col, c_row, "fox_fwd")
    o_raw, states, scores = hgrn_fwd(proj, lb_logits, "hgrn_fwd", after=[mark])
    mark = early("gu2", o_raw)
    o_b = gnorm_fwd(o_raw, proj, norm_g, "hgrn_norm", after=[mark])
    o_ab = jnp.stack([o_a, o_b])
    y_ab = _mm_branches(o_ab, get_w("proj", proj), "mix_proj")
    merged = merge_fwd(proj, y_ab, "mix_merge")
    mo = mm_nn_2d(merged, get_w("out"), F32, "mix_out")
    h2 = resid_post(h1, mo, gains["mix_post"], 1.0, "mix_resid")

    h3, s3 = _ffn_fwd(h2, gains["ffn2_pre"], gains["ffn2_post"], get_w, "2", "ffn2")

    u4 = norm_in(h3, gains["ple_pre"], "ple_norm")
    a4 = mm_nn_2d(u4, get_w("ple_gate"), F32, "ple_gate")
    b4 = mm_nn_col(p, get_w("ple_proj"), F32, "ple_proj")[0]
    loss, dh4, da4, db4, d_ple_post = ple_tail(h3, a4, b4, gains["ple_post"], target, "ple_tail")

    marks = [emit("ple_gate", mm_tn_2d(u4, da4, F32, "ple_dw_gate")),
             emit("ple_proj", mm_tn_col(p, db4[None], N_CHIPS, F32, "ple_dw_proj"))]
    du4 = mm_nt_2d(da4, get_w("ple_gate"), F32, "ple_du", after=marks)
    dh3, d_ple_pre = pre_bwd(dh4, h3, gains["ple_pre"], [du4], "ple_pre_bwd", after=[advance(du4)])

    dh2, d_f2_pre, d_f2_post = _ffn_bwd(dh3, s3, gains["ffn2_pre"], gains["ffn2_post"], get_w, emit, advance,
                                        "2", "ffn2")

    dmo, d_mix_post = post_bwd(dh2, mo, gains["mix_post"], 1.0, "mix_post_bwd")
    marks = [emit("out", mm_tn_2d(merged, dmo, F32, "mix_dw_out"))]
    dmerged = mm_nt_2d(dmo, get_w("out"), F32, "mix_dmerged", after=marks)
    dgate, dy_ab = merge_bwd(dmerged, proj, y_ab, "mix_merge_bwd")
    marks = [advance(dmerged), emit("proj", _mm_branches_dw(o_ab, dy_ab, "mix_dw_proj"))]
    do_ab = _mm_branches_bwd(dy_ab, get_w("proj"), "mix_do")
    do_raw, dg_b, d_norm_g = gnorm_bwd(do_ab[1], o_raw, proj, norm_g, "hgrn_norm_bwd")
    dq_b, df_b, di_b, dlb = hgrn_bwd(proj, lb_logits, states, scores, do_raw, "hgrn_bwd")
    d_lb_logits = lb_bwd(dlb, lb_logits, "lb_bwd")
    dq_a, dk_a, dv_a, dc_col, dc_row = fox_bwd(proj, c_col, c_row, do_ab[0], "fox_bwd")
    dc = (dc_col.reshape(HEADS, t) + dc_row.reshape(HEADS, t)).T
    dc = jnp.pad(dc, ((0, 0), (0, LANES - HEADS)))
    dfa, d_fox_bias = fox_post_bwd(dc, fa, fox_bias, "fox_post_bwd")
    dproj = jnp.concatenate([dq_a, dk_a.astype(BF16), dv_a.astype(BF16), dq_b, df_b, di_b, dg_b,
                             dgate[0], dgate[1]], axis=1)
    marks.append(emit("in_main", mm_tn_2d(dproj, u2, F32, "mix_dw_in")))
    marks.append(emit("in_fa", mm_tn_2d(dfa, u2, F32, "mix_dw_in_fa")))
    du2a = mm_nn_2d(dproj, get_w("in_main"), F32, "mix_du", after=marks)
    du2b = mm_nn_2d(dfa, get_w("in_fa"), F32, "mix_du_fa")
    dh1, d_mix_pre = pre_bwd(dh2, h1, gains["mix_pre"], [du2a, du2b], "mix_pre_bwd", after=[advance(du2a)])

    dx, d_f1_pre, d_f1_post = _ffn_bwd(dh1, s1, gains["ffn1_pre"], gains["ffn1_post"], get_w, emit, advance,
                                       "1", "ffn1")

    small = dict(ffn1_pre=d_f1_pre, ffn1_post=d_f1_post, mix_pre=d_mix_pre, mix_post=d_mix_post,
                 ffn2_pre=d_f2_pre, ffn2_post=d_f2_post, ple_pre=d_ple_pre, ple_post=d_ple_post,
                 fox_bias=d_fox_bias, lb_logits=d_lb_logits, norm_g=d_norm_g)
    return loss, dx, small


def _mm_branches(o_ab, w_proj, name):
    g, t, kk = o_ab.shape
    _, jn, _, ns = w_proj.shape
    tm = _pick(t, (512, 256, 128))
    return _mm(o_ab, w_proj, mode="nn", grid=(t // tm, g * jn, 1),
               a_spec=pl.BlockSpec((None, tm, kk), lambda i, j, k: (j // jn, i, 0)),
               b_spec=pl.BlockSpec((None, None, kk, ns), lambda i, j, k: (j // jn, j % jn, 0, 0)),
               o_spec=pl.BlockSpec((None, tm, ns), lambda i, j, k: (j // jn, i, j % jn)),
               out_shape=jax.ShapeDtypeStruct((g, t, jn * ns), F32), acc_shape=(tm, ns), name=name)


def _mm_branches_bwd(dy_ab, w_proj, name):
    g, t, _ = dy_ab.shape
    _, jn, kk, ns = w_proj.shape
    tm = _pick(t, (512, 256, 128))
    return _mm(dy_ab, w_proj, mode="nt", grid=(t // tm, g, jn),
               a_spec=pl.BlockSpec((None, tm, ns), lambda i, j, k: (j, i, k)),
               b_spec=pl.BlockSpec((None, None, kk, ns), lambda i, j, k: (j, k, 0, 0)),
               o_spec=pl.BlockSpec((None, tm, kk), lambda i, j, k: (j, i, 0)),
               out_shape=jax.ShapeDtypeStruct((g, t, kk), F32), acc_shape=(tm, kk), name=name)


def _mm_branches_dw(o_ab, dy_ab, name):
    g, t, kk = o_ab.shape
    d = dy_ab.shape[2]
    jn = N_CHIPS
    ns = d // jn
    return _mm(o_ab, dy_ab, mode="tn", grid=(1, g * jn, 1),
               a_spec=pl.BlockSpec((None, t, kk), lambda i, j, k: (j // jn, 0, 0)),
               b_spec=pl.BlockSpec((None, t, ns), lambda i, j, k: (j // jn, 0, j % jn)),
               o_spec=pl.BlockSpec((None, None, kk, ns), lambda i, j, k: (j // jn, j % jn, 0, 0)),
               out_shape=jax.ShapeDtypeStruct((g, jn, kk, ns), F32), acc_shape=(kk, ns), name=name)


HBM_SPEC = pl.BlockSpec(memory_space=pltpu.HBM)
SEM_SPEC = pl.BlockSpec(memory_space=pltpu.SEMAPHORE)
ANY_SPEC = pl.BlockSpec(memory_space=pl.ANY)
EFFECT = pltpu.SideEffectType.DATAFLOW_SIDE_EFFECTING


def _in_hbm(a):
    return pltpu.with_memory_space_constraint(a, pltpu.HBM)


def _place():
    x, y, c = lax.axis_index("x"), lax.axis_index("y"), lax.axis_index("c")
    chips = [(1 - x, y), (x, 1 - y), (1 - x, 1 - y)]
    return x, y, c, chips


def _half(shape, which, axis):
    n = shape[-2 + axis] // 2
    cut = pl.ds(which * n, n)
    return (cut, slice(None)) if axis == 0 else (slice(None), cut)


def _half_shape(shape, axis):
    s = list(shape)
    s[len(s) - 2 + axis] //= 2
    return tuple(s)


def _remote(src, dst, send_sems, recv_sems, k, to):
    return pltpu.make_async_remote_copy(src_ref=src, dst_ref=dst, send_sem=send_sems.at[k], recv_sem=recv_sems.at[k],
                                        device_id=to, device_id_type=MESH)


def split_start(name, srcs, lands, counts, copies):
    ns, nl, nset = len(srcs), len(lands), len(counts)

    def body(*refs):
        src_refs, land_refs = refs[:ns], refs[ns:ns + nl]
        sems = refs[ns + nl:ns + nl + 2 * nset]
        for s, plan in enumerate(copies(src_refs, land_refs)):
            for k, (src, dst, to) in enumerate(plan):
                _remote(src, dst, sems[2 * s], sems[2 * s + 1], k, to).start()
        refs[-1][...] = jnp.zeros_like(refs[-1])

    out_shape = []
    for n in counts:
        out_shape += [pltpu.SemaphoreType.DMA((n,)), pltpu.SemaphoreType.DMA((n,))]
    out_shape += [pltpu.HBM(a.shape, a.dtype) for a in list(srcs) + list(lands)]
    out_shape.append(jax.ShapeDtypeStruct((8, LANES), F32))
    res = pl.pallas_call(
        body, name=name, out_shape=tuple(out_shape), in_specs=[HBM_SPEC] * (ns + nl),
        out_specs=tuple([SEM_SPEC] * (2 * nset) + [HBM_SPEC] * (ns + nl) + [pl.BlockSpec(memory_space=pltpu.VMEM)]),
        input_output_aliases={i: 2 * nset + i for i in range(ns + nl)},
        compiler_params=pltpu.CompilerParams(has_side_effects=EFFECT),
    )(*[_in_hbm(a) for a in list(srcs) + list(lands)])
    sems = [(res[2 * s], res[2 * s + 1]) for s in range(nset)]
    return sems, list(res[2 * nset:2 * nset + ns]), list(res[2 * nset + ns:-1]), res[-1]


def split_wait(name, srcs, lands, sems, afters, copies):
    afters = [a for a in afters if a is not None]
    ns, nl, na = len(srcs), len(lands), len(afters)

    def body(*refs):
        src_refs, land_refs = refs[:ns], refs[ns:ns + nl]
        send_sems, recv_sems = refs[ns + nl:ns + nl + 2]
        for k, (src, dst, to) in enumerate(copies(src_refs, land_refs)):
            cp = _remote(src, dst, send_sems, recv_sems, k, to)
            cp.wait_send()
            cp.wait_recv()

    res = pl.pallas_call(
        body, name=name, out_shape=tuple(pltpu.HBM(a.shape, a.dtype) for a in list(srcs) + list(lands)),
        in_specs=[HBM_SPEC] * (ns + nl) + [SEM_SPEC, SEM_SPEC] + [ANY_SPEC] * na,
        out_specs=tuple([HBM_SPEC] * (ns + nl)), input_output_aliases={i: i for i in range(ns + nl)},
        compiler_params=pltpu.CompilerParams(has_side_effects=EFFECT),
    )(*srcs, *lands, sems[0], sems[1], *afters)
    return list(res[:ns]), list(res[ns:])


def _gather_plan(blocks):
    def copies(src_refs, land_refs):
        x, y, c, chips = _place()
        j_me = 2 * x + y
        plan = []
        for si, li, g, axis in blocks:
            src, land = src_refs[si], land_refs[li].at[g]
            mine = _half(src.shape, c, axis)
            for px, py in chips:
                plan.append((src.at[mine], land.at[(j_me,) + mine], (px, py, c)))
            plan.append((src, land.at[j_me], (x, y, 1 - c)))
        return plan
    return copies


def _gather_arrivals(blocks):
    def copies(src_refs, land_refs):
        x, y, c, chips = _place()
        j_me = 2 * x + y
        plan = []
        for si, li, g, axis in blocks:
            src, land = src_refs[si], land_refs[li].at[g]
            mine = _half(src.shape, c, axis)
            for px, py in chips:
                plan.append((src.at[mine], land.at[(2 * px + py,) + mine], (px, py, c)))
            plan.append((src, land.at[j_me], (x, y, 1 - c)))
        return plan
    return copies


def _pass_plan(blocks, arrivals):
    def copies(src_refs, land_refs):
        x, y, c, chips = _place()
        plan = []
        for li, g, axis in blocks:
            land = src_refs[li].at[g]
            half = _half(land.shape[1:], (1 - c) if arrivals else c, axis)
            for px, py in chips:
                part = land.at[(2 * px + py,) + half]
                plan.append((part, part, (x, y, 1 - c)))
        return plan
    return copies


def gather_pass(name, lands, blocks):
    n = len(lands)

    def body(*refs):
        outs = refs[n:2 * n]
        send_sems, recv_sems = refs[2 * n:]
        x, y, c, chips = _place()
        sent = []
        for i, (li, g, axis) in enumerate(blocks):
            land = outs[li].at[g]
            mine = _half(land.shape[1:], c, axis)
            for k, (px, py) in enumerate(chips):
                part = land.at[(2 * px + py,) + mine]
                cp = _remote(part, part, send_sems, recv_sems, 3 * i + k, (x, y, 1 - c))
                cp.start()
                sent.append(cp)
        for i, (li, g, axis) in enumerate(blocks):
            land = outs[li].at[g]
            other = _half(land.shape[1:], 1 - c, axis)
            for k, (px, py) in enumerate(chips):
                part = land.at[(2 * px + py,) + other]
                _remote(part, part, send_sems, recv_sems, 3 * i + k, (x, y, 1 - c)).wait_recv()
        for cp in sent:
            cp.wait_send()

    m = 3 * len(blocks)
    return pl.pallas_call(
        body, name=name, in_specs=[ANY_SPEC] * n, out_specs=[ANY_SPEC] * n,
        out_shape=[jax.ShapeDtypeStruct(a.shape, a.dtype) for a in lands],
        input_output_aliases={i: i for i in range(n)},
        scratch_shapes=[pltpu.SemaphoreType.DMA((m,)), pltpu.SemaphoreType.DMA((m,))],
    )(*lands)


def _pair_plan(axes):
    def copies(src_refs, land_refs):
        x, y, c, _ = _place()
        return [(src_refs[i].at[(slice(None), slice(None)) + _half(src_refs[i].shape, 1 - c, a)], land_refs[i],
                 (x, y, 1 - c)) for i, a in enumerate(axes)]
    return copies


def _scatter_plan(n):
    def copies(src_refs, land_refs):
        x, y, c, chips = _place()
        return [(src_refs[i].at[:, 2 * px + py], land_refs[i].at[k], (px, py, c))
                for i in range(n) for k, (px, py) in enumerate(chips)]
    return copies


def _broadcast_plan(axes, arrivals):
    def copies(src_refs, land_refs):
        x, y, c, _ = _place()
        plan = []
        for i, a in enumerate(axes):
            part = src_refs[i].at[(slice(None),) + _half(src_refs[i].shape, (1 - c) if arrivals else c, a)]
            plan.append((part, part, (x, y, 1 - c)))
        return plan
    return copies


N_DEV = 8
SLAB_ROWS = 16


def allreduce_small(slab, after):
    def body(x_ref, after_ref, o_ref, land, send_sems, recv_sems):
        x, y, c, _ = _place()
        me = 4 * x + 2 * y + c
        land[me] = x_ref[...]
        copies = []
        for d in range(1, N_DEV):
            to = (me + d) % N_DEV
            cp = pltpu.make_async_remote_copy(
                src_ref=x_ref, dst_ref=land.at[me], send_sem=send_sems.at[d - 1], recv_sem=recv_sems.at[me],
                device_id=(to // 4, (to // 2) % 2, to % 2), device_id_type=MESH)
            cp.start()
            copies.append(cp)
        for d in range(1, N_DEV):
            frm = (me + d) % N_DEV
            pltpu.make_async_remote_copy(
                src_ref=x_ref, dst_ref=land.at[frm], send_sem=send_sems.at[d - 1], recv_sem=recv_sems.at[frm],
                device_id=(frm // 4, (frm // 2) % 2, frm % 2), device_id_type=MESH).wait_recv()
        for cp in copies:
            cp.wait_send()
        acc = land[0]
        for s in range(1, N_DEV):
            acc = acc + land[s]
        o_ref[...] = acc

    vm = pl.BlockSpec(memory_space=pltpu.VMEM)
    return pl.pallas_call(
        body, name="allreduce_small", in_specs=[vm, ANY_SPEC], out_specs=vm,
        out_shape=jax.ShapeDtypeStruct(slab.shape, F32),
        scratch_shapes=[pltpu.VMEM((N_DEV,) + slab.shape, F32), pltpu.SemaphoreType.DMA((N_DEV - 1,)),
                        pltpu.SemaphoreType.DMA((N_DEV,))],
    )(slab, after)


BLOCK_BYTES = 3 * 1024 * 1024


def _tiles_2d(r, c, budget=BLOCK_BYTES):
    if r % 8 == 0:
        tc = c if c % LANES else _pick(c, (2048, 1408, 1024, 512, 256, 128))
        tr = 8
        for cand in (512, 256, 128, 64, 32, 16, 8):
            if r % cand == 0 and cand * tc * 4 <= budget:
                tr = cand
                break
        if tr >= 64 or c % LANES or r * LANES * 4 > budget:
            return tr, tc
    tc = LANES
    for cand in (1024, 512, 256, 128):
        if c % cand == 0 and r * cand * 4 <= budget:
            tc = cand
            break
    return r, tc


def _grid_spec(grid, in_specs, out_specs):
    return pltpu.PrefetchScalarGridSpec(num_scalar_prefetch=1, grid=grid, in_specs=in_specs, out_specs=out_specs)


def _own(axis, nr, nc):
    if axis == 0:
        return lambda i, j, where: (where[1] * nr + i, j)
    return lambda i, j, where: (i, where[1] * nc + j)


def pair_add(where, grad, recv, axis, name):
    g, jn, hr, hc = recv.shape
    tr, tc = _tiles_2d(hr, hc)
    nr, nc = hr // tr, hc // tc
    own = _own(axis, nr, nc)
    others = jn - 1

    def body(where_ref, a_ref, b_ref, o_ref):
        o_ref[...] = (a_ref[...] + b_ref[...]).astype(BF16)

    def block(a, where):
        return a // others, (where[0] + 1 + a % others) % jn

    blk = pl.BlockSpec((None, None, tr, tc), lambda a, i, j, where: block(a, where) + (i, j))
    mine = pl.BlockSpec((None, None, tr, tc), lambda a, i, j, where: block(a, where) + own(i, j, where))
    return pl.pallas_call(
        body, name=name, grid_spec=_grid_spec((g * others, nr, nc), [mine, blk], blk),
        out_shape=jax.ShapeDtypeStruct(recv.shape, BF16),
        compiler_params=_params(("parallel", "parallel", "parallel")),
    )(where, grad, recv)


def chip_add(where, grad, pair, recv, axis, name):
    g, jn, hr, hc = pair.shape
    tr, tc = _tiles_2d(hr, hc)
    nr, nc = hr // tr, hc // tc
    own = _own(axis, nr, nc)
    full = (g, 2 * hr, hc) if axis == 0 else (g, hr, 2 * hc)

    def body(where_ref, a_ref, p_ref, b_ref, o_ref):
        s = a_ref[...] + p_ref[...]
        for k in range(3):
            s = s + b_ref[k].astype(F32)
        o_ref[...] = s

    return pl.pallas_call(
        body, name=name,
        grid_spec=_grid_spec((g, nr, nc),
                             [pl.BlockSpec((None, None, tr, tc), lambda a, i, j, where: (a, where[0]) + own(i, j, where)),
                              pl.BlockSpec((None, None, tr, tc), lambda a, i, j, where: (a, where[0], i, j)),
                              pl.BlockSpec((3, None, tr, tc), lambda a, i, j, where: (0, a, i, j))],
                             pl.BlockSpec((None, tr, tc), lambda a, i, j, where: (a,) + own(i, j, where))),
        out_shape=jax.ShapeDtypeStruct(full, F32), compiler_params=_params(("parallel", "parallel", "parallel")),
    )(where, grad, pair, recv)


def _adam_math(w, g, m, v):
    m2 = ADAM_B1 * m + (1.0 - ADAM_B1) * g
    v2 = ADAM_B2 * v + (1.0 - ADAM_B2) * (g * g)
    m_hat = m2 / (1.0 - ADAM_B1 ** ADAM_STEP)
    v_hat = v2 / (1.0 - ADAM_B2 ** ADAM_STEP)
    delta = -ADAM_LR * (m_hat / (jnp.sqrt(v_hat) + ADAM_EPS) + ADAM_WD * w)
    return delta, m2, v2


def adamw(grad, idx, w, m, v, name):
    _, r, cc = w.shape
    rg = grad.shape[1]
    tr, tc = _tiles_2d(r, cc, BLOCK_BYTES // 2)
    assert rg == r or tr == r
    gr = tr if rg == r else rg

    def body(g_ref, w_ref, m_ref, v_ref, go_ref, d_ref, mo_ref, vo_ref):
        g = g_ref[pl.ds(0, tr), :]
        delta, m2, v2 = _adam_math(w_ref[...], g, m_ref[...], v_ref[...])
        go_ref[...] = g
        d_ref[...] = delta
        mo_ref[...] = m2
        vo_ref[...] = v2

    blk = pl.BlockSpec((None, tr, tc), lambda i, j: (0, i, j))
    return pl.pallas_call(
        body, name=name, grid=(r // tr, cc // tc),
        in_specs=[pl.BlockSpec((None, gr, tc), lambda i, j: (idx, i, j)), blk, blk, blk], out_specs=[blk] * 4,
        out_shape=[jax.ShapeDtypeStruct(w.shape, F32)] * 4, compiler_params=_params(("parallel", "parallel")),
    )(grad, w, m, v)


def adamw_small(g, w, m, v):
    def body(g_ref, w_ref, m_ref, v_ref, d_ref, mo_ref, vo_ref):
        delta, m2, v2 = _adam_math(w_ref[...], g_ref[...], m_ref[...], v_ref[...])
        d_ref[...] = delta
        mo_ref[...] = m2
        vo_ref[...] = v2

    return pl.pallas_call(body, name="adamw_small", out_shape=[jax.ShapeDtypeStruct(w.shape, F32)] * 3)(g, w, m, v)


GAINS = ("ffn1_pre", "ffn1_post", "mix_pre", "mix_post", "ffn2_pre", "ffn2_post", "ple_pre", "ple_post")
WEIGHTS = ("ffn1_pre_g", "ffn1_post_g", "ffn1_w_gate", "ffn1_w_up", "ffn1_w_down", "mix_pre_g", "mix_post_g",
           "mix_w_in", "fox_f_bias", "hgrn_lb_logits", "hgrn_norm_g", "mix_w_proj_fox", "mix_w_proj_hgrn",
           "mix_w_out", "ffn2_pre_g", "ffn2_post_g", "ffn2_w_gate", "ffn2_w_up", "ffn2_w_down", "ple_pre_g",
           "ple_post_g", "ple_w_gate", "ple_w_proj")
GROUPS = dict(gu1=(("ffn1_w_gate", "ffn1_w_up"), 0), down1=(("ffn1_w_down",), 0), win=(("mix_w_in",), 1),
              proj=(("mix_w_proj_fox", "mix_w_proj_hgrn"), 0), out=(("mix_w_out",), 0),
              gu2=(("ffn2_w_gate", "ffn2_w_up"), 0), down2=(("ffn2_w_down",), 0), ple_gate=(("ple_w_gate",), 0),
              ple_proj=(("ple_w_proj",), 0))
TRANSPOSED = ("mix_w_in",)
ROW_BLOCKS = ("down1", "down2", "out", "ple_gate")
GATHER_SETS = (("gate1",), ("up1",), ("down1",), ("win",), ("proj", "out"), ("gu2", "down2", "ple_gate", "ple_proj"))
GATHER_GROUPS = dict(GROUPS, gate1=(("ffn1_w_gate",), 0), up1=(("ffn1_w_up",), 0))
GATE_UP_KEYS = dict(gu1=("gate1", "up1"), gu2=("gu2",))
REDUCE_SETS = (("ple_gate", "ple_proj", "down2", "gu2"), ("out", "proj", "win"), ("down1",), ("gu1",))


def _pad_row(a, width):
    a = a.reshape(1, -1)
    return jnp.pad(a, ((0, 0), (0, width - a.shape[1])))


def _pack_small(vals):
    d = D_MODEL
    rows = [vals[n + "_g"].reshape(1, d) for n in GAINS]
    rows.append(_pad_row(vals["fox_f_bias"], d))
    lg = vals["hgrn_lb_logits"]
    rows += [_pad_row(lg[0], d), _pad_row(lg[1], d), _pad_row(vals["hgrn_norm_g"], d)]
    slab = jnp.concatenate(rows, axis=0)
    return jnp.pad(slab, ((0, SLAB_ROWS - slab.shape[0]), (0, 0)))


def _unpack_small(slab):
    out = {n + "_g": slab[i:i + 1] for i, n in enumerate(GAINS)}
    out["fox_f_bias"] = slab[8:9, :HEADS]
    out["hgrn_lb_logits"] = slab[9:11, :WIDTH]
    out["hgrn_norm_g"] = slab[11:12, :HEAD_DIM]
    return out


def _split_in(win_t):
    lo = 3 * WIDTH
    main = jnp.concatenate([win_t[:lo], win_t[lo + HEADS:]], axis=0)
    fa = jnp.pad(win_t[lo:lo + HEADS], ((0, LANES - HEADS), (0, 0)))
    return main, fa


def _join_in(main, fa):
    lo = 3 * WIDTH
    return jnp.concatenate([main[:lo], fa[:HEADS], main[lo:]], axis=0)


def _as_block(name, a):
    return jnp.swapaxes(a, 1, 2) if name in TRANSPOSED else a


def _send_block(name, a, mark):
    blk = _as_block(name, a)[0]
    if mark is not None:
        blk = blk + mark[0, 0]
    return blk.astype(BF16)


def kernel(x, p, ffn1_pre_g, ffn1_post_g, ffn1_w_gate, ffn1_w_up, ffn1_w_down, mix_pre_g, mix_post_g, mix_w_in, fox_f_bias, hgrn_lb_logits, hgrn_norm_g, mix_w_proj_fox, mix_w_proj_hgrn, mix_w_out, ffn2_pre_g, ffn2_post_g, ffn2_w_gate, ffn2_w_up, ffn2_w_down, ple_pre_g, ple_post_g, ple_w_gate, ple_w_proj, loss_target, m_ffn1_pre_g, m_ffn1_post_g, m_ffn1_w_gate, m_ffn1_w_up, m_ffn1_w_down, m_mix_pre_g, m_mix_post_g, m_mix_w_in, m_fox_f_bias, m_hgrn_lb_logits, m_hgrn_norm_g, m_mix_w_proj_fox, m_mix_w_proj_hgrn, m_mix_w_out, m_ffn2_pre_g, m_ffn2_post_g, m_ffn2_w_gate, m_ffn2_w_up, m_ffn2_w_down, m_ple_pre_g, m_ple_post_g, m_ple_w_gate, m_ple_w_proj, v_ffn1_pre_g, v_ffn1_post_g, v_ffn1_w_gate, v_ffn1_w_up, v_ffn1_w_down, v_mix_pre_g, v_mix_post_g, v_mix_w_in, v_fox_f_bias, v_hgrn_lb_logits, v_hgrn_norm_g, v_mix_w_proj_fox, v_mix_w_proj_hgrn, v_mix_w_out, v_ffn2_pre_g, v_ffn2_post_g, v_ffn2_w_gate, v_ffn2_w_up, v_ffn2_w_down, v_ple_pre_g, v_ple_post_g, v_ple_w_gate, v_ple_w_proj):
    args = dict(locals())
    wts = {n: args[n] for n in WEIGHTS}
    mom = {n: args["m_" + n] for n in WEIGHTS}
    var = {n: args["v_" + n] for n in WEIGHTS}
    d = D_MODEL
    where = jnp.stack([2 * lax.axis_index("x") + lax.axis_index("y"), lax.axis_index("c")]).astype(jnp.int32)

    def start_sets(name, which, mark):
        srcs, lands, plans = [], [], []
        for si in which:
            blocks = []
            for g in GATHER_SETS[si]:
                names, axis = GATHER_GROUPS[g]
                for pos, n in enumerate(names):
                    blocks.append((len(srcs), len(lands), pos, axis))
                    srcs.append(_send_block(n, wts[n], mark))
                lands.append(lax.empty((len(names), N_CHIPS) + srcs[-1].shape, BF16))
            plans.append(blocks)
        sems, srcs, lands, mark = split_start(name, srcs, lands, [4 * len(b) for b in plans],
                                              lambda sr, lr: [_gather_plan(b)(sr, lr) for b in plans])
        out = {}
        for k, si in enumerate(which):
            s_idx = sorted({b[0] for b in plans[k]})
            l_idx = sorted({b[1] for b in plans[k]})
            local = [(s_idx.index(a), l_idx.index(b), pos, ax) for a, b, pos, ax in plans[k]]
            out[si] = (sems[k], [srcs[i] for i in s_idx], [lands[i] for i in l_idx], local)
        return out, mark

    flying, mark = start_sets("gather_start_0", [0], None)
    rest, all_started = start_sets("gather_start_1", list(range(1, len(GATHER_SETS))), mark)
    flying.update(rest)
    full, passing = {}, {}

    def set_of(key):
        g = "win" if key in ("in_main", "in_fa") else key
        return g, [g in s for s in GATHER_SETS].index(True)

    def arrive(si, after):
        sem, srcs, lands, local = flying.pop(si)
        _, got = split_wait("gather_wait_%d" % si, srcs, lands, sem, [after, all_started], _gather_arrivals(local))
        return got, [(b, pos, ax) for _, b, pos, ax in local]

    def early(key, after):
        g, si = set_of(key)
        if g in full or si not in flying:
            return None
        got, blocks = arrive(si, after)
        sem, got, _, mark = split_start("pass_start_%d" % si, got, [], [3 * len(blocks)],
                                        lambda sr, lr: [_pass_plan(blocks, False)(sr, lr)])
        passing[si] = (sem[0], got, blocks)
        return mark

    def land_set(si, after):
        if si in passing:
            sem, got, blocks = passing.pop(si)
            got, _ = split_wait("pass_wait_%d" % si, got, [], sem, [after], _pass_plan(blocks, True))
        else:
            got, blocks = arrive(si, after)
            got = gather_pass("gather_pass_%d" % si, got, blocks)
        for g, arr in zip(GATHER_SETS[si], got):
            if g == "win":
                full["win"] = arr
                full["in_main"], full["in_fa"] = _split_in(arr.reshape(-1, d))
            else:
                full[g] = arr.reshape(-1, d) if g in ROW_BLOCKS else arr

    def get_w(key, after=None):
        g, si = set_of(key)
        if g not in full:
            land_set(si, after)
        return full[key]

    grads, pairing, started = {}, [], {}
    rows4 = lambda a: a.reshape(1, N_CHIPS, a.shape[0] // N_CHIPS, a.shape[1])

    def emit(key, grad):
        if key in ("in_main", "in_fa"):
            grads[key] = grad
            if "in_main" not in grads or "in_fa" not in grads:
                return None
            key, grad = "win", rows4(_join_in(grads["in_main"], grads["in_fa"]))
        grads[key] = grad if grad.ndim == 4 else rows4(grad)
        for si, s in enumerate(REDUCE_SETS):
            if key in s and all(g in grads for g in s):
                axes = [GROUPS[g][1] for g in s]
                own = [grads[g] for g in s]
                zones = [lax.empty(_half_shape(a.shape, ax), F32) for a, ax in zip(own, axes)]
                plan = _pair_plan(axes)
                sem, own, zones, mark = split_start("pair_start_%d" % si, own, zones, [len(s)],
                                                    lambda sr, lr: [plan(sr, lr)])
                pairing.append((si, sem[0], own, zones, axes, plan))
                return mark
        return None

    def advance(value):
        mark = None
        while pairing:
            si, sem, own, zones, axes, plan = pairing.pop(0)
            s = REDUCE_SETS[si]
            own, recv = split_wait("pair_wait_%d" % si, own, zones, sem, [value], plan)
            parts = [pair_add(where, a, r, ax, "pair_add_" + g) for g, a, r, ax in zip(s, own, recv, axes)]
            zones = [lax.empty((3, q.shape[0]) + q.shape[2:], BF16) for q in parts]
            plan = _scatter_plan(len(s))
            sem, parts, zones, mark = split_start("scatter_start_%d" % si, parts, zones, [3 * len(s)],
                                                  lambda sr, lr: [plan(sr, lr)])
            started[si] = (sem[0], parts, zones, own, recv, axes, plan)
        return mark

    gains = {n: wts[n + "_g"] for n in GAINS}
    loss, dx, small = layer_step(x[0], p[0, 0].astype(BF16), loss_target[0], gains, _pad_row(fox_f_bias, LANES),
                                 hgrn_lb_logits, hgrn_norm_g, get_w, emit, advance, early)

    out_g, out_d, out_m, out_v = {}, {}, {}, {}
    after, crossing = None, None

    def finish(si, sem, halves, axes, mark):
        reduced, _ = split_wait("broadcast_wait_%d" % si, halves, [], sem, [mark], _broadcast_plan(axes, True))
        last = None
        for g, red in zip(REDUCE_SETS[si], reduced):
            for idx, n in enumerate(GROUPS[g][0]):
                res = adamw(red, idx, _as_block(n, wts[n]), _as_block(n, mom[n]), _as_block(n, var[n]), "adamw_" + n)
                out_g[n], out_d[n], out_m[n], out_v[n] = [_as_block(n, r) for r in res]
                last = res[1]
        return last

    for si, s in enumerate(REDUCE_SETS):
        sem, parts, zones, own, recv, axes, plan = started[si]
        _, zones = split_wait("scatter_wait_%d" % si, parts, zones, sem, [dx, after], plan)
        halves = [chip_add(where, a, r, z, ax, "chip_add_" + g) for g, a, r, z, ax in zip(s, own, recv, zones, axes)]
        plan = _broadcast_plan(axes, False)
        sem, halves, _, mark = split_start("broadcast_start_%d" % si, halves, [], [len(s)],
                                           lambda sr, lr: [plan(sr, lr)])
        if crossing is not None:
            after = finish(*crossing, mark)
        crossing = (si, sem[0], halves, axes)
    after = finish(*crossing, None)

    small_named = {n + "_g": small[n] for n in GAINS}
    small_named.update(fox_f_bias=small["fox_bias"][:, :HEADS], hgrn_lb_logits=small["lb_logits"],
                       hgrn_norm_g=small["norm_g"])
    g_small = allreduce_small(_pack_small(small_named), after)
    d_small, m_small, v_small = adamw_small(g_small, _pack_small(wts), _pack_small(mom), _pack_small(var))

    for dst, slab in ((out_g, g_small), (out_d, d_small), (out_m, m_small), (out_v, v_small)):
        dst.update(_unpack_small(slab))

    total = lax.psum(loss[0, 0], ("x", "y", "c"))
    return (total, dx[None], *[out_g[n] for n in WEIGHTS], *[out_d[n] for n in WEIGHTS],
            *[out_m[n] for n in WEIGHTS], *[out_v[n] for n in WEIGHTS])
```

```python
import functools

import jax
import jax.numpy as jnp
from jax import lax
from jax.experimental import pallas as pl
from jax.experimental.pallas import tpu as pltpu

F32 = jnp.float32
BF16 = jnp.bfloat16

D_MODEL = 2048
SEQ = 2048
D_FF = 5632
PLE_DIM = 256
HEADS = 8
HEAD_DIM = 128
WIDTH = HEADS * HEAD_DIM
CHUNK = 64
SUB = 16
HGRN_HEADS_PER_STEP = 2
NORM_EPS = 1e-6
MACARON_SCALE = 0.5
N_CHIPS = 4

ADAM_LR = 0.001
ADAM_B1 = 0.9
ADAM_B2 = 0.999
ADAM_EPS = 1e-08
ADAM_WD = 0.01
ADAM_STEP = 10

LANES = 128
VMEM_LIMIT = 56 * 1024 * 1024
NEG_BIG = -1e30
MESH = pl.DeviceIdType.MESH


def _pick(n, cands):
    for c in cands:
        if c <= n and n % c == 0:
            return c
    return n


def _params(sem, vmem=VMEM_LIMIT):
    return pltpu.CompilerParams(dimension_semantics=sem, vmem_limit_bytes=vmem)


def _sigmoid(x):
    return 1.0 / (1.0 + jnp.exp(-x))


def _silu(x):
    return x * _sigmoid(x)


def _silu_grad(x):
    s = _sigmoid(x)
    return s * (1.0 + x * (1.0 - s))


_DN = {"nn": (((1,), (0,)), ((), ())), "nt": (((1,), (1,)), ((), ())), "tn": (((0,), (0,)), ((), ()))}


def _mm(a, b, *, mode, grid, a_spec, b_spec, o_spec, out_shape, acc_shape, name, after=(), init=None, into=None):
    nk = grid[2]
    dn = _DN[mode]
    after = [m for m in after if m is not None]
    extra = ([init] if init is not None else []) + ([into] if into is not None else []) + after
    n_extra = len(extra)

    def body(a_ref, b_ref, *rest):
        o_ref, acc_ref = rest[n_extra:]
        k = pl.program_id(2)

        @pl.when(k == 0)
        def _():
            acc_ref[...] = jnp.zeros_like(acc_ref) if init is None else rest[0][...].astype(F32)

        acc_ref[...] += lax.dot_general(a_ref[...].astype(BF16), b_ref[...].astype(BF16), dn,
                                        preferred_element_type=F32)

        @pl.when(k == nk - 1)
        def _():
            o_ref[...] = acc_ref[...].astype(o_ref.dtype)

    anywhere = pl.BlockSpec(memory_space=pl.ANY)
    return pl.pallas_call(
        body, name=name, grid=grid,
        in_specs=[a_spec, b_spec] + ([o_spec] if init is not None else []) + [anywhere] * (n_extra - (init is not None)),
        out_specs=o_spec, out_shape=out_shape, scratch_shapes=[pltpu.VMEM(acc_shape, F32)],
        input_output_aliases={} if into is None else {2 + (init is not None): 0},
        compiler_params=_params(("parallel", "parallel", "arbitrary")),
    )(a, b, *extra)


def mm_nn_2d(a, b, out_dtype, name, after=()):
    m, kk = a.shape
    n = b.shape[1]
    tm, tn = _pick(m, (512, 256, 128)), _pick(n, (1024, 512, 256, 128))
    tk = _pick(kk, (5632, 2816, 2048, 1408, 1024, 512, 256, 128))
    return _mm(a, b, mode="nn", grid=(m // tm, n // tn, kk // tk),
               a_spec=pl.BlockSpec((tm, tk), lambda i, j, k: (i, k)),
               b_spec=pl.BlockSpec((tk, tn), lambda i, j, k: (k, j)),
               o_spec=pl.BlockSpec((tm, tn), lambda i, j, k: (i, j)),
               out_shape=jax.ShapeDtypeStruct((m, n), out_dtype), acc_shape=(tm, tn), name=name, after=after)


def mm_nt_2d(a, b, out_dtype, name, after=()):
    m, c = a.shape
    n = b.shape[0]
    tm, tn, tk = _pick(m, (512, 256, 128)), _pick(n, (1408, 1024, 512, 256, 128)), _pick(c, (2048, 1408, 1024, 512, 256, 128))
    return _mm(a, b, mode="nt", grid=(m // tm, n // tn, c // tk),
               a_spec=pl.BlockSpec((tm, tk), lambda i, j, k: (i, k)),
               b_spec=pl.BlockSpec((tn, tk), lambda i, j, k: (j, k)),
               o_spec=pl.BlockSpec((tm, tn), lambda i, j, k: (i, j)),
               out_shape=jax.ShapeDtypeStruct((m, n), out_dtype), acc_shape=(tm, tn), name=name, after=after)


def mm_tn_2d(a, b, out_dtype, name):
    c, m = a.shape
    n = b.shape[1]
    tm, tn, tk = _pick(m, (1408, 1024, 512, 256, 128)), _pick(n, (1024, 512, 256, 128)), _pick(c, (2048, 1024, 512, 256, 128))
    return _mm(a, b, mode="tn", grid=(m // tm, n // tn, c // tk),
               a_spec=pl.BlockSpec((tk, tm), lambda i, j, k: (k, i)),
               b_spec=pl.BlockSpec((tk, tn), lambda i, j, k: (k, j)),
               o_spec=pl.BlockSpec((tm, tn), lambda i, j, k: (i, j)),
               out_shape=jax.ShapeDtypeStruct((m, n), out_dtype), acc_shape=(tm, tn), name=name)


def mm_nn_col(a, w, out_dtype, name, slot0=0, total=None, into=None):
    m, kk = a.shape
    g, jn, _, ns = w.shape
    total = g if total is None else total
    tm, tk = _pick(m, (512, 256, 128)), _pick(kk, (2048, 1024, 512, 256, 128))
    return _mm(a, w, mode="nn", grid=(m // tm, g * jn, kk // tk),
               a_spec=pl.BlockSpec((tm, tk), lambda i, j, k: (i, k)),
               b_spec=pl.BlockSpec((None, None, tk, ns), lambda i, j, k: (j // jn, j % jn, k, 0)),
               o_spec=pl.BlockSpec((None, tm, ns), lambda i, j, k: (slot0 + j // jn, i, j % jn)),
               out_shape=jax.ShapeDtypeStruct((total, m, jn * ns), out_dtype), acc_shape=(tm, ns), name=name,
               into=into)


def mm_nt_col(a, w, out_dtype, name, after=(), slot0=0, init=None):
    _, m, _ = a.shape
    g, jn, kk, ns = w.shape
    kc = 2 if jn % 2 == 0 else 1
    per = jn // kc
    nk = g * per
    tm, tn = _pick(m, (512, 256, 128)), _pick(kk, (1024, 512, 256, 128))
    extra = ([init] if init is not None else []) + [v for v in after if v is not None]

    def body(a_ref, w_ref, *rest):
        o_ref, acc_ref = rest[len(extra):]
        k = pl.program_id(2)
        part = None
        for c in range(kc):
            prod = lax.dot_general(a_ref[:, c * ns:(c + 1) * ns].astype(BF16), w_ref[c].astype(BF16), _DN["nt"],
                                   preferred_element_type=F32)
            part = prod if part is None else part + prod

        @pl.when(k == 0)
        def _():
            acc_ref[...] = part if init is None else rest[0][...].astype(F32) + part

        @pl.when(k > 0)
        def _():
            acc_ref[...] += part

        @pl.when(k == nk - 1)
        def _():
            o_ref[...] = acc_ref[...].astype(o_ref.dtype)

    o_spec = pl.BlockSpec((tm, tn), lambda i, j, k: (i, j))
    anywhere = pl.BlockSpec(memory_space=pl.ANY)
    return pl.pallas_call(
        body, name=name, grid=(m // tm, kk // tn, nk),
        in_specs=[pl.BlockSpec((None, tm, kc * ns), lambda i, j, k: (slot0 + k // per, i, k % per)),
                  pl.BlockSpec((None, kc, tn, ns), lambda i, j, k: (k // per, k % per, j, 0))]
        + ([o_spec] if init is not None else []) + [anywhere] * (len(extra) - (init is not None)),
        out_specs=o_spec, out_shape=jax.ShapeDtypeStruct((m, kk), out_dtype),
        scratch_shapes=[pltpu.VMEM((tm, tn), F32)],
        compiler_params=_params(("parallel", "parallel", "arbitrary")),
    )(a, w, *extra)


def mm_tn_col(a, b, jn, out_dtype, name):
    c, kk = a.shape
    g, _, n = b.shape
    ns = n // jn
    tm, tk = _pick(kk, (512, 256, 128)), _pick(c, (2048, 1024, 512, 256, 128))
    return _mm(a, b, mode="tn", grid=(kk // tm, g * jn, c // tk),
               a_spec=pl.BlockSpec((tk, tm), lambda i, j, k: (k, i)),
               b_spec=pl.BlockSpec((None, tk, ns), lambda i, j, k: (j // jn, k, j % jn)),
               o_spec=pl.BlockSpec((None, None, tm, ns), lambda i, j, k: (j // jn, j % jn, i, 0)),
               out_shape=jax.ShapeDtypeStruct((g, jn, kk, ns), out_dtype), acc_shape=(tm, ns), name=name)


def _rstd(x):
    return lax.rsqrt(jnp.mean(x * x, axis=-1, keepdims=True) + NORM_EPS)


def _rms_bwd(x, g, dy):
    r = _rstd(x)
    xn = x * r
    dyg = dy * g
    dx = r * (dyg - xn * jnp.mean(dyg * xn, axis=-1, keepdims=True))
    return dx, jnp.sum(dy * xn, axis=0, keepdims=True)


def _row_tile(t):
    return _pick(t, (256, 128, 64, 32, 16, 8))


def norm_in(h, g, name):
    t, d = h.shape
    tr = _row_tile(t)

    def body(h_ref, g_ref, u_ref):
        x = h_ref[...]
        u_ref[...] = (x * _rstd(x) * g_ref[...]).astype(BF16)

    return pl.pallas_call(
        body, name=name, grid=(t // tr,),
        in_specs=[pl.BlockSpec((tr, d), lambda i: (i, 0)), pl.BlockSpec((1, d), lambda i: (0, 0))],
        out_specs=pl.BlockSpec((tr, d), lambda i: (i, 0)),
        out_shape=jax.ShapeDtypeStruct((t, d), BF16), compiler_params=_params(("parallel",)),
    )(h, g)


def resid_norm(h, y, g, scale, g_next, name):
    t, d = h.shape
    tr = _row_tile(t)

    def body(h_ref, y_ref, g_ref, gn_ref, o_ref, u_ref):
        yv = y_ref[...]
        out = h_ref[...] + scale * (yv * _rstd(yv) * g_ref[...])
        o_ref[...] = out
        u_ref[...] = (out * _rstd(out) * gn_ref[...]).astype(BF16)

    row = pl.BlockSpec((tr, d), lambda i: (i, 0))
    vec = pl.BlockSpec((1, d), lambda i: (0, 0))
    return pl.pallas_call(
        body, name=name, grid=(t // tr,), in_specs=[row, row, vec, vec], out_specs=[row, row],
        out_shape=[jax.ShapeDtypeStruct((t, d), F32), jax.ShapeDtypeStruct((t, d), BF16)],
        compiler_params=_params(("parallel",)),
    )(h, y, g, g_next)


def post_bwd(dh, y, g, scale, name):
    t, d = dh.shape
    tr = _row_tile(t)

    def body(dh_ref, y_ref, g_ref, dy_ref, dg_ref):
        @pl.when(pl.program_id(0) == 0)
        def _():
            dg_ref[...] = jnp.zeros_like(dg_ref)

        dx, dg = _rms_bwd(y_ref[...], g_ref[...], scale * dh_ref[...])
        dy_ref[...] = dx.astype(BF16)
        dg_ref[...] += dg

    row = pl.BlockSpec((tr, d), lambda i: (i, 0))
    vec = pl.BlockSpec((1, d), lambda i: (0, 0))
    return pl.pallas_call(
        body, name=name, grid=(t // tr,), in_specs=[row, row, vec], out_specs=[row, vec],
        out_shape=[jax.ShapeDtypeStruct((t, d), BF16), jax.ShapeDtypeStruct((1, d), F32)],
        compiler_params=_params(("arbitrary",)),
    )(dh, y, g)


def pre_bwd(dh, h, g, dus, name, after=()):
    t, d = dh.shape
    tr = _row_tile(t)
    n_du = len(dus)
    after = [m for m in after if m is not None]

    def body(*refs):
        dh_ref, h_ref, g_ref = refs[:3]
        du_refs = refs[3:3 + n_du]
        o_ref, dg_ref = refs[3 + n_du + len(after):]

        @pl.when(pl.program_id(0) == 0)
        def _():
            dg_ref[...] = jnp.zeros_like(dg_ref)

        du = du_refs[0][...]
        for r in du_refs[1:]:
            du = du + r[...]
        dx, dg = _rms_bwd(h_ref[...], g_ref[...], du)
        o_ref[...] = dh_ref[...] + dx
        dg_ref[...] += dg

    row = pl.BlockSpec((tr, d), lambda i: (i, 0))
    vec = pl.BlockSpec((1, d), lambda i: (0, 0))
    return pl.pallas_call(
        body, name=name, grid=(t // tr,),
        in_specs=[row, row, vec] + [row] * n_du + [pl.BlockSpec(memory_space=pl.ANY)] * len(after),
        out_specs=[row, vec], out_shape=[jax.ShapeDtypeStruct((t, d), F32), jax.ShapeDtypeStruct((1, d), F32)],
        compiler_params=_params(("arbitrary",)),
    )(dh, h, g, *dus, *after)


def _ew_tiles(t, f):
    return _pick(t, (256, 128, 64, 32, 16, 8)), _pick(f, (1408, 1024, 512, 256, 128))


def swiglu_act(gu, name):
    _, t, f = gu.shape
    tr, tc = _ew_tiles(t, f)

    def body(gu_ref, o_ref):
        o_ref[...] = (_silu(gu_ref[0]) * gu_ref[1]).astype(BF16)

    return pl.pallas_call(
        body, name=name, grid=(t // tr, f // tc),
        in_specs=[pl.BlockSpec((2, tr, tc), lambda i, j: (0, i, j))],
        out_specs=pl.BlockSpec((tr, tc), lambda i, j: (i, j)),
        out_shape=jax.ShapeDtypeStruct((t, f), BF16), compiler_params=_params(("parallel", "parallel")),
    )(gu)


def swiglu_bwd(dact, gu, name):
    _, t, f = gu.shape
    tr, tc = _ew_tiles(t, f)

    def body(da_ref, gu_ref, o_ref):
        da = da_ref[...]
        gate = gu_ref[0]
        o_ref[0] = (da * gu_ref[1] * _silu_grad(gate)).astype(BF16)
        o_ref[1] = (da * _silu(gate)).astype(BF16)

    return pl.pallas_call(
        body, name=name, grid=(t // tr, f // tc),
        in_specs=[pl.BlockSpec((tr, tc), lambda i, j: (i, j)), pl.BlockSpec((2, tr, tc), lambda i, j: (0, i, j))],
        out_specs=pl.BlockSpec((2, tr, tc), lambda i, j: (0, i, j)),
        out_shape=jax.ShapeDtypeStruct((2, t, f), BF16), compiler_params=_params(("parallel", "parallel")),
    )(dact, gu)


def _col_blocks():
    w = WIDTH // LANES
    return dict(q_a=0, k_a=w, v_a=2 * w, q_b=3 * w, f_b=4 * w, i_b=5 * w, g_b=6 * w, gate_a=7 * w,
                gate_b=7 * w + D_MODEL // LANES)


def _tri(n, lower):
    r = lax.broadcasted_iota(jnp.int32, (n, n), 0)
    c = lax.broadcasted_iota(jnp.int32, (n, n), 1)
    return jnp.where((r >= c) if lower else (r <= c), 1.0, 0.0).astype(F32)


def _dot_hi(a, b):
    return jnp.dot(a, b, precision=lax.Precision.HIGHEST, preferred_element_type=F32)


def fox_prep(fa, bias, name):
    t, w = fa.shape
    tb = _pick(t, (256, 128, 64))

    def body(fa_ref, b_ref, c_ref, carry_ref):
        @pl.when(pl.program_id(0) == 0)
        def _():
            carry_ref[...] = jnp.zeros_like(carry_ref)

        z = fa_ref[...] + b_ref[...]
        lf = jnp.minimum(z, 0.0) - jnp.log(1.0 + jnp.exp(-jnp.abs(z)))
        c = _dot_hi(_tri(tb, True), lf) + carry_ref[...]
        c_ref[...] = c
        carry_ref[...] = carry_ref[...] + jnp.sum(lf, axis=0, keepdims=True)

    return pl.pallas_call(
        body, name=name, grid=(t // tb,),
        in_specs=[pl.BlockSpec((tb, w), lambda i: (i, 0)), pl.BlockSpec((1, w), lambda i: (0, 0))],
        out_specs=pl.BlockSpec((tb, w), lambda i: (i, 0)),
        out_shape=jax.ShapeDtypeStruct((t, w), F32), scratch_shapes=[pltpu.VMEM((1, w), F32)],
        compiler_params=_params(("arbitrary",)),
    )(fa, bias)


def fox_post_bwd(dc, fa, bias, name):
    t, w = fa.shape
    tb = _pick(t, (256, 128, 64))
    nb = t // tb

    def body(dc_ref, fa_ref, b_ref, dfa_ref, db_ref, carry_ref):
        @pl.when(pl.program_id(0) == 0)
        def _():
            carry_ref[...] = jnp.zeros_like(carry_ref)
            db_ref[...] = jnp.zeros_like(db_ref)

        dcv = dc_ref[...]
        dlf = _dot_hi(_tri(tb, False), dcv) + carry_ref[...]
        z = fa_ref[...] + b_ref[...]
        dz = dlf * _sigmoid(-z)
        dfa_ref[...] = dz.astype(BF16)
        db_ref[...] += jnp.sum(dz, axis=0, keepdims=True)
        carry_ref[...] = carry_ref[...] + jnp.sum(dcv, axis=0, keepdims=True)

    rev = pl.BlockSpec((tb, w), lambda i: (nb - 1 - i, 0))
    vec = pl.BlockSpec((1, w), lambda i: (0, 0))
    return pl.pallas_call(
        body, name=name, grid=(nb,), in_specs=[rev, rev, vec], out_specs=[rev, vec],
        out_shape=[jax.ShapeDtypeStruct((t, w), BF16), jax.ShapeDtypeStruct((1, w), F32)],
        scratch_shapes=[pltpu.VMEM((1, w), F32)], compiler_params=_params(("arbitrary",)),
    )(dc, fa, bias)


def _fox_probs(q_ref, k_ref, cc_ref, cr_ref, qi, tq, t):
    scale = HEAD_DIM ** -0.5
    s = lax.dot_general(q_ref[...].astype(BF16), k_ref[...].astype(BF16), _DN["nt"], preferred_element_type=F32)
    logits = s * scale + cc_ref[...] - cr_ref[...]
    qpos = qi * tq + lax.broadcasted_iota(jnp.int32, (tq, t), 0)
    kpos = lax.broadcasted_iota(jnp.int32, (tq, t), 1)
    logits = jnp.where(kpos <= qpos, logits, NEG_BIG)
    m = jnp.max(logits, axis=-1, keepdims=True)
    p = jnp.exp(logits - m)
    return p / jnp.sum(p, axis=-1, keepdims=True)


FOX_SEGMENTS = 4


def _fox_segments(t):
    tq = _pick(t, (256, 128))
    nseg = min(FOX_SEGMENTS, t // tq)
    return tq, nseg, t // tq // nseg


def _fox_specs(t, q0, kt, tq):
    cb = _col_blocks()
    dh = HEAD_DIM
    return [pl.BlockSpec((tq, dh), lambda h, i: (q0 + i, cb["q_a"] + h)),
            pl.BlockSpec((kt, dh), lambda h, i: (0, cb["k_a"] + h)),
            pl.BlockSpec((kt, dh), lambda h, i: (0, cb["v_a"] + h)),
            pl.BlockSpec((None, tq, 1), lambda h, i: (h, q0 + i, 0)),
            pl.BlockSpec((None, 1, kt), lambda h, i: (h, 0, 0))]


def fox_fwd(proj, c_col, c_row, name):
    t = proj.shape[0]
    tq, nseg, nq = _fox_segments(t)
    dh = HEAD_DIM

    def segment(out, r):
        q0, kt = r * nq, (r + 1) * nq * tq

        def body(q_ref, k_ref, v_ref, cc_ref, cr_ref, prev_ref, o_ref):
            p = _fox_probs(q_ref, k_ref, cc_ref, cr_ref, q0 + pl.program_id(1), tq, kt)
            o_ref[...] = jnp.dot(p.astype(BF16), v_ref[...].astype(BF16), preferred_element_type=F32).astype(BF16)

        return pl.pallas_call(
            body, name="%s_%d" % (name, r), grid=(HEADS, nq),
            in_specs=_fox_specs(t, q0, kt, tq) + [pl.BlockSpec(memory_space=pl.ANY)],
            out_specs=pl.BlockSpec((tq, dh), lambda h, i: (q0 + i, h)),
            out_shape=jax.ShapeDtypeStruct((t, WIDTH), BF16), input_output_aliases={5: 0},
            compiler_params=_params(("parallel", "parallel")),
        )(proj, proj, proj, c_col, c_row, out)

    out = lax.empty((t, WIDTH), BF16)
    for r in range(nseg):
        out = segment(out, r)
    return out


def fox_bwd(proj, c_col, c_row, do, name):
    t = proj.shape[0]
    tq, nseg, nq = _fox_segments(t)
    dh = HEAD_DIM
    scale = HEAD_DIM ** -0.5

    def segment(acc, r):
        q0, kt = r * nq, (r + 1) * nq * tq

        def body(q_ref, k_ref, v_ref, cc_ref, cr_ref, do_ref, dqp_ref, dkp_ref, dvp_ref, dccp_ref, dcrp_ref,
                 dq_ref, dk_ref, dv_ref, dcc_ref, dcr_ref):
            @pl.when(pl.program_id(1) == 0)
            def _():
                dk_ref[...] = dkp_ref[...]
                dv_ref[...] = dvp_ref[...]
                dcr_ref[...] = dcrp_ref[...]

            p = _fox_probs(q_ref, k_ref, cc_ref, cr_ref, q0 + pl.program_id(1), tq, kt)
            dov = do_ref[...].astype(BF16)
            kb = k_ref[...].astype(BF16)
            dv_ref[...] += lax.dot_general(p.astype(BF16), dov, _DN["tn"], preferred_element_type=F32)
            dp = lax.dot_general(dov, v_ref[...].astype(BF16), _DN["nt"], preferred_element_type=F32)
            ds = p * (dp - jnp.sum(p * dp, axis=-1, keepdims=True))
            dcc_ref[...] = jnp.sum(ds, axis=-1, keepdims=True)
            dcr_ref[...] -= jnp.sum(ds, axis=0, keepdims=True)
            dss = (ds * scale).astype(BF16)
            dq_ref[...] = jnp.dot(dss, kb, preferred_element_type=F32).astype(BF16)
            dk_ref[...] += lax.dot_general(dss, q_ref[...].astype(BF16), _DN["tn"], preferred_element_type=F32)

        rows = pl.BlockSpec((tq, dh), lambda h, i: (q0 + i, h))
        keys = pl.BlockSpec((kt, dh), lambda h, i: (0, h))
        col = pl.BlockSpec((None, tq, 1), lambda h, i: (h, q0 + i, 0))
        row = pl.BlockSpec((None, 1, kt), lambda h, i: (h, 0, 0))
        anywhere = pl.BlockSpec(memory_space=pl.ANY)
        return pl.pallas_call(
            body, name="%s_%d" % (name, r), grid=(HEADS, nq),
            in_specs=_fox_specs(t, q0, kt, tq) + [rows, anywhere, keys, keys, anywhere, row],
            out_specs=[rows, keys, keys, col, row],
            out_shape=[jax.ShapeDtypeStruct(a.shape, a.dtype) for a in acc],
            input_output_aliases={6 + k: k for k in range(5)},
            compiler_params=_params(("parallel", "arbitrary")),
        )(proj, proj, proj, c_col, c_row, do, *acc)

    acc = [lax.empty((t, WIDTH), BF16), jnp.zeros((t, WIDTH), F32), jnp.zeros((t, WIDTH), F32),
           lax.empty((HEADS, t, 1), F32), jnp.zeros((HEADS, 1, t), F32)]
    for r in range(nseg):
        acc = segment(acc, r)
    return acc


def _lower_bound(lg_ref):
    l0 = lg_ref[0:1, :]
    l1 = lg_ref[1:2, :]
    m = jnp.maximum(l0, l1)
    e0 = jnp.exp(l0 - m)
    e1 = jnp.exp(l1 - m)
    return e0 / (e0 + e1)


def _hgrn_inputs(qb_ref, fb_ref, lg_ref, q_s, k_s, cum_s):
    lb = _lower_bound(lg_ref)
    sig = _sigmoid(fb_ref[...])
    f = lb + (1.0 - lb) * sig
    q_s[...] = _silu(qb_ref[...])
    k_s[...] = 1.0 - f
    cum_s[...] = _dot_hi(_tri(CHUNK, True), jnp.log(f))
    return lb, sig, f


def _boundary(cum_s, a):
    if a == 0:
        return jnp.zeros((1, HEAD_DIM), F32)
    return cum_s[pl.ds(SUB * a - 1, 1), :]


def _hgrn_scores(q_s, k_s, cum_s):
    cum = cum_s[...]
    kk = k_s[...]
    lane = lax.broadcasted_iota(jnp.int32, (SUB, CHUNK), 1)
    row = lax.broadcasted_iota(jnp.int32, (SUB, 1), 0)
    blocks = []
    for a in range(CHUNK // SUB):
        rows = pl.ds(SUB * a, SUB)
        ca = _boundary(cum_s, a)
        cum_a = cum_s[rows, :]
        q_a = q_s[rows, :]
        qa = q_a * jnp.exp(cum_a - ca)
        ka = kk * jnp.exp(jnp.minimum(ca - cum, 0.0))
        blk = lax.dot_general(qa, ka, _DN["nt"], preferred_element_type=F32)
        blk = jnp.where(lane < SUB * a, blk, 0.0)
        for s in range(SUB):
            r = SUB * a + s
            e = jnp.exp(jnp.minimum(cum_a - cum_s[pl.ds(r, 1), :], 0.0))
            col = jnp.sum(q_a * k_s[pl.ds(r, 1), :] * e, axis=-1, keepdims=True)
            col = jnp.where(row >= s, col, 0.0)
            blk = jnp.where(lane == r, col, blk)
        blocks.append(blk)
    return jnp.concatenate(blocks, axis=0)


def hgrn_fwd(proj, lb_logits, name, after=()):
    after = [m for m in after if m is not None]
    t = proj.shape[0]
    n = t // CHUNK
    cb = _col_blocks()
    dh = HEAD_DIM

    hb = HGRN_HEADS_PER_STEP
    w = hb * dh

    def one_head(qb_ref, fb_ref, ib_ref, lg_ref, o_ref, st_ref, a_ref, state, q_s, k_s, cum_s):
        _hgrn_inputs(qb_ref, fb_ref, lg_ref, q_s, k_s, cum_s)
        st = state[...]
        st_ref[...] = st
        cum = cum_s[...]
        v = ib_ref[...]
        qe = q_s[...] * jnp.exp(cum)
        inter = lax.dot_general(qe, st, _DN["nt"], preferred_element_type=F32)
        a_mat = _hgrn_scores(q_s, k_s, cum_s)
        a_ref[...] = a_mat
        o_ref[...] = inter + jnp.dot(a_mat, v, preferred_element_type=F32)
        last = cum_s[pl.ds(CHUNK - 1, 1), :]
        kd = k_s[...] * jnp.exp(last - cum)
        state[...] = st * jnp.exp(last) + lax.dot_general(v, kd, _DN["tn"], preferred_element_type=F32)

    def body(qb_ref, fb_ref, ib_ref, lg_ref, *rest):
        o_ref, st_ref, a_ref = rest[len(after):len(after) + 3]
        scratch = rest[len(after) + 3:]

        @pl.when(pl.program_id(1) == 0)
        def _():
            for j in range(hb):
                scratch[4 * j][...] = jnp.zeros((dh, dh), F32)

        for j in range(hb):
            cols = (slice(None), pl.ds(j * dh, dh))
            one_head(qb_ref.at[cols], fb_ref.at[cols], ib_ref.at[cols], lg_ref.at[cols], o_ref.at[cols],
                     st_ref.at[j], a_ref.at[j], *scratch[4 * j:4 * j + 4])

    blk = lambda off: pl.BlockSpec((CHUNK, w), lambda h, i: (i, off // hb + h))
    return pl.pallas_call(
        body, name=name, grid=(HEADS // hb, n),
        in_specs=[blk(cb["q_b"]), blk(cb["f_b"]), blk(cb["i_b"]), pl.BlockSpec((2, w), lambda h, i: (0, h))]
        + [pl.BlockSpec(memory_space=pl.ANY)] * len(after),
        out_specs=[pl.BlockSpec((CHUNK, w), lambda h, i: (i, h)),
                   pl.BlockSpec((hb, None, dh, dh), lambda h, i: (h, i, 0, 0)),
                   pl.BlockSpec((hb, None, CHUNK, CHUNK), lambda h, i: (h, i, 0, 0))],
        out_shape=[jax.ShapeDtypeStruct((t, WIDTH), F32), jax.ShapeDtypeStruct((HEADS, n, dh, dh), F32),
                   jax.ShapeDtypeStruct((HEADS, n, CHUNK, CHUNK), F32)],
        scratch_shapes=([pltpu.VMEM((dh, dh), F32)] + [pltpu.VMEM((CHUNK, dh), F32)] * 3) * hb,
        compiler_params=_params(("parallel", "arbitrary")),
    )(proj, proj, proj, lb_logits, *after)


def hgrn_bwd(proj, lb_logits, states, scores, do, name):
    t = proj.shape[0]
    n = t // CHUNK
    cb = _col_blocks()
    dh = HEAD_DIM
    nsub = CHUNK // SUB

    hb = HGRN_HEADS_PER_STEP
    w = hb * dh

    def one_head(qb_ref, fb_ref, ib_ref, lg_ref, st_ref, a_ref, do_ref, dqb_ref, dfb_ref, dib_ref, dlb_ref,
                 dstate, q_s, k_s, cum_s, da_s, dq_s, dk_s):
        lb, sig, f = _hgrn_inputs(qb_ref, fb_ref, lg_ref, q_s, k_s, cum_s)
        st = st_ref[...]
        dst = dstate[...]
        cum = cum_s[...]
        q = q_s[...]
        kk = k_s[...]
        v = ib_ref[...]
        dov = do_ref[...]
        e_cum = jnp.exp(cum)
        qe = q * e_cum
        last = cum_s[pl.ds(CHUNK - 1, 1), :]
        e_last = jnp.exp(last)
        e_tail = jnp.exp(last - cum)
        kd = kk * e_tail

        a_mat = a_ref[...]
        tri = _tri(CHUNK, True)
        da_s[...] = lax.dot_general(dov, v, _DN["nt"], preferred_element_type=F32) * tri
        dv = (lax.dot_general(a_mat, dov, _DN["tn"], preferred_element_type=F32)
              + lax.dot_general(kd, dst, _DN["nt"], preferred_element_type=F32))
        dk_state = jnp.dot(v, dst, preferred_element_type=F32) * e_tail
        dq_inter = jnp.dot(dov, st, preferred_element_type=F32) * e_cum
        dstate[...] = dst * e_last + lax.dot_general(dov, qe, _DN["tn"], preferred_element_type=F32)

        lane = lax.broadcasted_iota(jnp.int32, (SUB, CHUNK), 1)
        row = lax.broadcasted_iota(jnp.int32, (SUB, 1), 0)
        dk_s[...] = jnp.zeros_like(dk_s)
        for a in range(nsub):
            rows = pl.ds(SUB * a, SUB)
            ca = _boundary(cum_s, a)
            cum_a = cum_s[rows, :]
            q_a = q_s[rows, :]
            ea = jnp.exp(cum_a - ca)
            eb = jnp.exp(jnp.minimum(ca - cum, 0.0))
            da_a = da_s[rows, :]
            da_off = jnp.where(lane < SUB * a, da_a, 0.0)
            dq_a = ea * jnp.dot(da_off, kk * eb, preferred_element_type=F32)
            dk_s[...] += eb * lax.dot_general(da_off, q_a * ea, _DN["tn"], preferred_element_type=F32)
            dk_rows = jnp.zeros((SUB, dh), F32)
            for s in range(SUB):
                r = SUB * a + s
                e = jnp.exp(jnp.minimum(cum_a - cum_s[pl.ds(r, 1), :], 0.0))
                dcol = jnp.sum(jnp.where(lane == r, da_a, 0.0), axis=-1, keepdims=True)
                dcol = jnp.where(row >= s, dcol, 0.0)
                w = dcol * e
                dq_a = dq_a + w * k_s[pl.ds(r, 1), :]
                dk_rows = jnp.where(row == s, jnp.sum(w * q_a, axis=0, keepdims=True), dk_rows)
            dq_s[rows, :] = dq_a
            dk_s[rows, :] += dk_rows

        dq = dq_inter + dq_s[...]
        dk = dk_s[...] + dk_state
        d_last = (jnp.sum(dst * st, axis=0, keepdims=True) * e_last
                  + jnp.sum(kk * dk_state, axis=0, keepdims=True))
        rowc = lax.broadcasted_iota(jnp.int32, (CHUNK, 1), 0)
        dcum = q * dq - kk * dk + jnp.where(rowc == CHUNK - 1, d_last, 0.0)
        dg = _dot_hi(_tri(CHUNK, False), dcum)
        df = dg / f - dk
        dqb_ref[...] = (dq * _silu_grad(qb_ref[...])).astype(BF16)
        dfb_ref[...] = (df * (1.0 - lb) * sig * (1.0 - sig)).astype(BF16)
        dib_ref[...] = dv.astype(BF16)
        dlb_ref[...] += jnp.sum(df * (1.0 - sig), axis=0, keepdims=True)

    def body(qb_ref, fb_ref, ib_ref, lg_ref, st_ref, a_ref, do_ref, dqb_ref, dfb_ref, dib_ref, dlb_ref, *scratch):
        @pl.when(pl.program_id(1) == 0)
        def _():
            for j in range(hb):
                scratch[7 * j][...] = jnp.zeros((dh, dh), F32)
            dlb_ref[...] = jnp.zeros_like(dlb_ref)

        for j in range(hb):
            cols = (slice(None), pl.ds(j * dh, dh))
            one_head(qb_ref.at[cols], fb_ref.at[cols], ib_ref.at[cols], lg_ref.at[cols], st_ref.at[j], a_ref.at[j],
                     do_ref.at[cols], dqb_ref.at[cols], dfb_ref.at[cols], dib_ref.at[cols], dlb_ref.at[cols],
                     *scratch[7 * j:7 * j + 7])

    blk = lambda off: pl.BlockSpec((CHUNK, w), lambda h, i: (n - 1 - i, off // hb + h))
    out_blk = pl.BlockSpec((CHUNK, w), lambda h, i: (n - 1 - i, h))
    return pl.pallas_call(
        body, name=name, grid=(HEADS // hb, n),
        in_specs=[blk(cb["q_b"]), blk(cb["f_b"]), blk(cb["i_b"]), pl.BlockSpec((2, w), lambda h, i: (0, h)),
                  pl.BlockSpec((hb, None, dh, dh), lambda h, i: (h, n - 1 - i, 0, 0)),
                  pl.BlockSpec((hb, None, CHUNK, CHUNK), lambda h, i: (h, n - 1 - i, 0, 0)), out_blk],
        out_specs=[out_blk, out_blk, out_blk, pl.BlockSpec((1, w), lambda h, i: (0, h))],
        out_shape=[jax.ShapeDtypeStruct((t, WIDTH), BF16)] * 3 + [jax.ShapeDtypeStruct((1, WIDTH), F32)],
        scratch_shapes=([pltpu.VMEM((dh, dh), F32)] + [pltpu.VMEM((CHUNK, dh), F32)] * 3
                        + [pltpu.VMEM((CHUNK, CHUNK), F32)] + [pltpu.VMEM((CHUNK, dh), F32)] * 2) * hb,
        compiler_params=_params(("parallel", "arbitrary")),
    )(proj, proj, proj, lb_logits, states, scores, do)


def lb_bwd(dlb, lb_logits, name):
    def body(dlb_ref, lg_ref, o_ref):
        p0 = _lower_bound(lg_ref)
        d0 = dlb_ref[...] * p0 * (1.0 - p0)
        o_ref[0:1, :] = d0
        o_ref[1:2, :] = -d0

    return pl.pallas_call(body, name=name, out_shape=jax.ShapeDtypeStruct(lb_logits.shape, F32))(dlb, lb_logits)


def gnorm_fwd(o_raw, proj, norm_g, name, after=()):
    after = [m for m in after if m is not None]
    t = o_raw.shape[0]
    tr = _row_tile(t)
    cb = _col_blocks()
    dh = HEAD_DIM

    def body(o_ref, gb_ref, g_ref, *rest):
        x = o_ref[...]
        rest[-1][...] = (x * _rstd(x) * g_ref[...] * _silu(gb_ref[...])).astype(BF16)

    return pl.pallas_call(
        body, name=name, grid=(t // tr, HEADS),
        in_specs=[pl.BlockSpec((tr, dh), lambda i, h: (i, h)), pl.BlockSpec((tr, dh), lambda i, h: (i, cb["g_b"] + h)),
                  pl.BlockSpec((1, dh), lambda i, h: (0, 0))] + [pl.BlockSpec(memory_space=pl.ANY)] * len(after),
        out_specs=pl.BlockSpec((tr, dh), lambda i, h: (i, h)),
        out_shape=jax.ShapeDtypeStruct((t, WIDTH), BF16), compiler_params=_params(("parallel", "parallel")),
    )(o_raw, proj, norm_g, *after)


def gnorm_bwd(dy, o_raw, proj, norm_g, name):
    t = o_raw.shape[0]
    tr = _row_tile(t)
    cb = _col_blocks()
    dh = HEAD_DIM

    def body(dy_ref, o_ref, gb_ref, g_ref, do_ref, dgb_ref, dg_ref):
        @pl.when((pl.program_id(0) == 0) & (pl.program_id(1) == 0))
        def _():
            dg_ref[...] = jnp.zeros_like(dg_ref)

        x = o_ref[...]
        gb = gb_ref[...]
        dyv = dy_ref[...]
        g = g_ref[...]
        dx, dg = _rms_bwd(x, g, dyv * _silu(gb))
        do_ref[...] = dx
        dgb_ref[...] = (dyv * (x * _rstd(x) * g) * _silu_grad(gb)).astype(BF16)
        dg_ref[...] += dg

    hb = pl.BlockSpec((tr, dh), lambda i, h: (i, h))
    vec = pl.BlockSpec((1, dh), lambda i, h: (0, 0))
    return pl.pallas_call(
        body, name=name, grid=(t // tr, HEADS),
        in_specs=[hb, hb, pl.BlockSpec((tr, dh), lambda i, h: (i, cb["g_b"] + h)), vec],
        out_specs=[hb, hb, vec],
        out_shape=[jax.ShapeDtypeStruct((t, WIDTH), F32), jax.ShapeDtypeStruct((t, WIDTH), BF16),
                   jax.ShapeDtypeStruct((1, dh), F32)],
        compiler_params=_params(("arbitrary", "arbitrary")),
    )(dy, o_raw, proj, norm_g)


def merge_fwd(proj, y, name):
    _, t, d = y.shape
    tr = _row_tile(t)
    tc = _pick(d, (1024, 512, 256, 128))
    cb = _col_blocks()
    ga, gb = cb["gate_a"] * LANES // tc, cb["gate_b"] * LANES // tc

    def body(ga_ref, gb_ref, y_ref, o_ref):
        o_ref[...] = (_sigmoid(ga_ref[...]) * y_ref[0] + _sigmoid(gb_ref[...]) * y_ref[1]).astype(BF16)

    return pl.pallas_call(
        body, name=name, grid=(t // tr, d // tc),
        in_specs=[pl.BlockSpec((tr, tc), lambda i, j: (i, ga + j)), pl.BlockSpec((tr, tc), lambda i, j: (i, gb + j)),
                  pl.BlockSpec((2, tr, tc), lambda i, j: (0, i, j))],
        out_specs=pl.BlockSpec((tr, tc), lambda i, j: (i, j)),
        out_shape=jax.ShapeDtypeStruct((t, d), BF16), compiler_params=_params(("parallel", "parallel")),
    )(proj, proj, y)


def merge_bwd(dm, proj, y, name):
    _, t, d = y.shape
    tr = _row_tile(t)
    tc = _pick(d, (1024, 512, 256, 128))
    cb = _col_blocks()
    ga, gb = cb["gate_a"] * LANES // tc, cb["gate_b"] * LANES // tc

    def body(dm_ref, ga_ref, gb_ref, y_ref, dg_ref, dy_ref):
        dmv = dm_ref[...]
        for idx, g_ref in enumerate((ga_ref, gb_ref)):
            s = _sigmoid(g_ref[...])
            dg_ref[idx] = (dmv * y_ref[idx] * s * (1.0 - s)).astype(BF16)
            dy_ref[idx] = (dmv * s).astype(BF16)

    pair = pl.BlockSpec((2, tr, tc), lambda i, j: (0, i, j))
    return pl.pallas_call(
        body, name=name, grid=(t // tr, d // tc),
        in_specs=[pl.BlockSpec((tr, tc), lambda i, j: (i, j)), pl.BlockSpec((tr, tc), lambda i, j: (i, ga + j)),
                  pl.BlockSpec((tr, tc), lambda i, j: (i, gb + j)), pair],
        out_specs=[pair, pair],
        out_shape=[jax.ShapeDtypeStruct((2, t, d), BF16)] * 2, compiler_params=_params(("parallel", "parallel")),
    )(dm, proj, proj, y)


def ple_tail(h, a, b, g, target, name):
    t, d = h.shape
    tr = _row_tile(t)

    def body(h_ref, a_ref, b_ref, g_ref, t_ref, loss_ref, dh_ref, da_ref, db_ref, dg_ref):
        @pl.when(pl.program_id(0) == 0)
        def _():
            loss_ref[...] = jnp.zeros_like(loss_ref)
            dg_ref[...] = jnp.zeros_like(dg_ref)

        s = _sigmoid(a_ref[...])
        bv = b_ref[...]
        z = s * bv
        gv = g_ref[...]
        err = h_ref[...] + z * _rstd(z) * gv - t_ref[...]
        loss_ref[...] += 0.5 * jnp.sum(jnp.sum(err * err, axis=-1, keepdims=True), axis=0, keepdims=True) / d
        dh = err / d
        dh_ref[...] = dh
        dz, dg = _rms_bwd(z, gv, dh)
        da_ref[...] = (dz * bv * s * (1.0 - s)).astype(BF16)
        db_ref[...] = (dz * s).astype(BF16)
        dg_ref[...] += dg

    row = pl.BlockSpec((tr, d), lambda i: (i, 0))
    vec = pl.BlockSpec((1, d), lambda i: (0, 0))
    return pl.pallas_call(
        body, name=name, grid=(t // tr,), in_specs=[row, row, row, vec, row],
        out_specs=[pl.BlockSpec((1, 1), lambda i: (0, 0)), row, row, row, vec],
        out_shape=[jax.ShapeDtypeStruct((1, 1), F32), jax.ShapeDtypeStruct((t, d), F32),
                   jax.ShapeDtypeStruct((t, d), BF16), jax.ShapeDtypeStruct((t, d), BF16),
                   jax.ShapeDtypeStruct((1, d), F32)],
        compiler_params=_params(("arbitrary",)),
    )(h, a, b, g, target)


def _ffn_fwd(h, u, post_g, next_g, get_w, idx, tag):
    gu = None
    for i, key in enumerate(GATE_UP_KEYS["gu" + idx]):
        w = get_w(key, h if gu is None else gu)
        gu = mm_nn_col(u, w, F32, "%s_gate_up_%d" % (tag, i), slot0=i, total=2, into=gu)
    act = swiglu_act(gu, tag + "_act")
    y = mm_nn_2d(act, get_w("down" + idx, gu), F32, tag + "_down")
    out, u_next = resid_norm(h, y, post_g, MACARON_SCALE, next_g, tag + "_out")
    return out, u_next, (h, u, gu, act, y)


def _ffn_bwd(dh, saved, pre_g, post_g, get_w, emit, advance, idx, tag):
    h, u, gu, act, y = saved
    dy, d_post = post_bwd(dh, y, post_g, MACARON_SCALE, tag + "_post_bwd")
    m1 = emit("down" + idx, mm_tn_2d(act, dy, F32, tag + "_dw_down"))
    dact = mm_nt_2d(dy, get_w("down" + idx), F32, tag + "_dact", after=[m1])
    m2 = advance(dact)
    dgu = swiglu_bwd(dact, gu, tag + "_act_bwd")
    m3 = emit("gu" + idx, mm_tn_col(u, dgu, N_CHIPS, F32, tag + "_dw_gate_up"))
    du = None
    for i, key in enumerate(GATE_UP_KEYS["gu" + idx]):
        du = mm_nt_col(dgu, get_w(key), F32, "%s_du_%d" % (tag, i), after=[m2, m3] if du is None else (),
                       slot0=i, init=du)
    m4 = advance(du)
    dh_in, d_pre = pre_bwd(dh, h, pre_g, [du], tag + "_pre_bwd", after=[m4])
    return dh_in, d_pre, d_post


def _heads_col(a):
    t = a.shape[0]
    at = a[:, :HEADS].T
    return at.reshape(HEADS, t, 1), at.reshape(HEADS, 1, t)


def layer_step(x, p, target, gains, fox_bias, lb_logits, norm_g, get_w, emit, advance, early):
    t = x.shape[0]
    u1 = norm_in(x, gains["ffn1_pre"], "ffn1_norm")
    h1, u2, s1 = _ffn_fwd(x, u1, gains["ffn1_post"], gains["mix_pre"], get_w, "1", "ffn1")

    proj = mm_nt_2d(u2, get_w("in_main", h1), F32, "mix_in")
    fa = mm_nt_2d(u2, get_w("in_fa"), F32, "mix_in_fa")
    mark = early("proj", proj)
    c = fox_prep(fa, fox_bias, "fox_prep")
    c_col, c_row = _heads_col(c)
    o_a = fox_fwd(proj, c_col, c_row, "fox_fwd")
    o_raw, states, scores = hgrn_fwd(proj, lb_logits, "hgrn_fwd", after=[mark])
    mark = early("gu2", o_raw)
    o_b = gnorm_fwd(o_raw, proj, norm_g, "hgrn_norm", after=[mark])
    o_ab = jnp.stack([o_a, o_b])
    y_ab = _mm_branches(o_ab, get_w("proj", proj), "mix_proj")
    merged = merge_fwd(proj, y_ab, "mix_merge")
    mo = mm_nn_2d(merged, get_w("out"), F32, "mix_out")
    h2, u3 = resid_norm(h1, mo, gains["mix_post"], 1.0, gains["ffn2_pre"], "mix_resid")

    h3, u4, s3 = _ffn_fwd(h2, u3, gains["ffn2_post"], gains["ple_pre"], get_w, "2", "ffn2")

    a4 = mm_nn_2d(u4, get_w("ple_gate"), F32, "ple_gate")
    b4 = mm_nn_col(p, get_w("ple_proj"), F32, "ple_proj")[0]
    loss, dh4, da4, db4, d_ple_post = ple_tail(h3, a4, b4, gains["ple_post"], target, "ple_tail")

    marks = [emit("ple_gate", mm_tn_2d(u4, da4, F32, "ple_dw_gate")),
             emit("ple_proj", mm_tn_col(p, db4[None], N_CHIPS, F32, "ple_dw_proj"))]
    du4 = mm_nt_2d(da4, get_w("ple_gate"), F32, "ple_du", after=marks)
    dh3, d_ple_pre = pre_bwd(dh4, h3, gains["ple_pre"], [du4], "ple_pre_bwd", after=[advance(du4)])

    dh2, d_f2_pre, d_f2_post = _ffn_bwd(dh3, s3, gains["ffn2_pre"], gains["ffn2_post"], get_w, emit, advance,
                                        "2", "ffn2")

    dmo, d_mix_post = post_bwd(dh2, mo, gains["mix_post"], 1.0, "mix_post_bwd")
    marks = [emit("out", mm_tn_2d(merged, dmo, F32, "mix_dw_out"))]
    dmerged = mm_nt_2d(dmo, get_w("out"), F32, "mix_dmerged", after=marks)
    dgate, dy_ab = merge_bwd(dmerged, proj, y_ab, "mix_merge_bwd")
    marks = [advance(dmerged), emit("proj", _mm_branches_dw(o_ab, dy_ab, "mix_dw_proj"))]
    do_ab = _mm_branches_bwd(dy_ab, get_w("proj"), "mix_do")
    do_raw, dg_b, d_norm_g = gnorm_bwd(do_ab[1], o_raw, proj, norm_g, "hgrn_norm_bwd")
    dq_b, df_b, di_b, dlb = hgrn_bwd(proj, lb_logits, states, scores, do_raw, "hgrn_bwd")
    d_lb_logits = lb_bwd(dlb, lb_logits, "lb_bwd")
    dq_a, dk_a, dv_a, dc_col, dc_row = fox_bwd(proj, c_col, c_row, do_ab[0], "fox_bwd")
    dc = (dc_col.reshape(HEADS, t) + dc_row.reshape(HEADS, t)).T
    dc = jnp.pad(dc, ((0, 0), (0, LANES - HEADS)))
    dfa, d_fox_bias = fox_post_bwd(dc, fa, fox_bias, "fox_post_bwd")
    dproj = jnp.concatenate([dq_a, dk_a.astype(BF16), dv_a.astype(BF16), dq_b, df_b, di_b, dg_b,
                             dgate[0], dgate[1]], axis=1)
    marks.append(emit("in_main", mm_tn_2d(dproj, u2, F32, "mix_dw_in")))
    marks.append(emit("in_fa", mm_tn_2d(dfa, u2, F32, "mix_dw_in_fa")))
    du2a = mm_nn_2d(dproj, get_w("in_main"), F32, "mix_du", after=marks)
    du2b = mm_nn_2d(dfa, get_w("in_fa"), F32, "mix_du_fa")
    dh1, d_mix_pre = pre_bwd(dh2, h1, gains["mix_pre"], [du2a, du2b], "mix_pre_bwd", after=[advance(du2a)])

    dx, d_f1_pre, d_f1_post = _ffn_bwd(dh1, s1, gains["ffn1_pre"], gains["ffn1_post"], get_w, emit, advance,
                                       "1", "ffn1")

    small = dict(ffn1_pre=d_f1_pre, ffn1_post=d_f1_post, mix_pre=d_mix_pre, mix_post=d_mix_post,
                 ffn2_pre=d_f2_pre, ffn2_post=d_f2_post, ple_pre=d_ple_pre, ple_post=d_ple_post,
                 fox_bias=d_fox_bias, lb_logits=d_lb_logits, norm_g=d_norm_g)
    return loss, dx, small


def _mm_branches(o_ab, w_proj, name):
    g, t, kk = o_ab.shape
    _, jn, _, ns = w_proj.shape
    tm = _pick(t, (512, 256, 128))
    return _mm(o_ab, w_proj, mode="nn", grid=(t // tm, g * jn, 1),
               a_spec=pl.BlockSpec((None, tm, kk), lambda i, j, k: (j // jn, i, 0)),
               b_spec=pl.BlockSpec((None, None, kk, ns), lambda i, j, k: (j // jn, j % jn, 0, 0)),
               o_spec=pl.BlockSpec((None, tm, ns), lambda i, j, k: (j // jn, i, j % jn)),
               out_shape=jax.ShapeDtypeStruct((g, t, jn * ns), F32), acc_shape=(tm, ns), name=name)


def _mm_branches_bwd(dy_ab, w_proj, name):
    g, t, _ = dy_ab.shape
    _, jn, kk, ns = w_proj.shape
    tm = _pick(t, (512, 256, 128))
    return _mm(dy_ab, w_proj, mode="nt", grid=(t // tm, g, jn),
               a_spec=pl.BlockSpec((None, tm, ns), lambda i, j, k: (j, i, k)),
               b_spec=pl.BlockSpec((None, None, kk, ns), lambda i, j, k: (j, k, 0, 0)),
               o_spec=pl.BlockSpec((None, tm, kk), lambda i, j, k: (j, i, 0)),
               out_shape=jax.ShapeDtypeStruct((g, t, kk), F32), acc_shape=(tm, kk), name=name)


def _mm_branches_dw(o_ab, dy_ab, name):
    g, t, kk = o_ab.shape
    d = dy_ab.shape[2]
    jn = N_CHIPS
    ns = d // jn
    return _mm(o_ab, dy_ab, mode="tn", grid=(1, g * jn, 1),
               a_spec=pl.BlockSpec((None, t, kk), lambda i, j, k: (j // jn, 0, 0)),
               b_spec=pl.BlockSpec((None, t, ns), lambda i, j, k: (j // jn, 0, j % jn)),
               o_spec=pl.BlockSpec((None, None, kk, ns), lambda i, j, k: (j // jn, j % jn, 0, 0)),
               out_shape=jax.ShapeDtypeStruct((g, jn, kk, ns), F32), acc_shape=(kk, ns), name=name)


HBM_SPEC = pl.BlockSpec(memory_space=pltpu.HBM)
SEM_SPEC = pl.BlockSpec(memory_space=pltpu.SEMAPHORE)
ANY_SPEC = pl.BlockSpec(memory_space=pl.ANY)
EFFECT = pltpu.SideEffectType.DATAFLOW_SIDE_EFFECTING


def _in_hbm(a):
    return pltpu.with_memory_space_constraint(a, pltpu.HBM)


def _place():
    x, y, c = lax.axis_index("x"), lax.axis_index("y"), lax.axis_index("c")
    chips = [(1 - x, y), (x, 1 - y), (1 - x, 1 - y)]
    return x, y, c, chips


def _half(shape, which, axis):
    n = shape[-2 + axis] // 2
    cut = pl.ds(which * n, n)
    return (cut, slice(None)) if axis == 0 else (slice(None), cut)


def _half_shape(shape, axis):
    s = list(shape)
    s[len(s) - 2 + axis] //= 2
    return tuple(s)


def _remote(src, dst, send_sems, recv_sems, k, to):
    return pltpu.make_async_remote_copy(src_ref=src, dst_ref=dst, send_sem=send_sems.at[k], recv_sem=recv_sems.at[k],
                                        device_id=to, device_id_type=MESH)


def split_start(name, srcs, lands, counts, copies):
    ns, nl, nset = len(srcs), len(lands), len(counts)

    def body(*refs):
        src_refs, land_refs = refs[:ns], refs[ns:ns + nl]
        sems = refs[ns + nl:ns + nl + 2 * nset]
        for s, plan in enumerate(copies(src_refs, land_refs)):
            for k, (src, dst, to) in enumerate(plan):
                _remote(src, dst, sems[2 * s], sems[2 * s + 1], k, to).start()
        refs[-1][...] = jnp.zeros_like(refs[-1])

    out_shape = []
    for n in counts:
        out_shape += [pltpu.SemaphoreType.DMA((n,)), pltpu.SemaphoreType.DMA((n,))]
    out_shape += [pltpu.HBM(a.shape, a.dtype) for a in list(srcs) + list(lands)]
    out_shape.append(jax.ShapeDtypeStruct((8, LANES), F32))
    res = pl.pallas_call(
        body, name=name, out_shape=tuple(out_shape), in_specs=[HBM_SPEC] * (ns + nl),
        out_specs=tuple([SEM_SPEC] * (2 * nset) + [HBM_SPEC] * (ns + nl) + [pl.BlockSpec(memory_space=pltpu.VMEM)]),
        input_output_aliases={i: 2 * nset + i for i in range(ns + nl)},
        compiler_params=pltpu.CompilerParams(has_side_effects=EFFECT),
    )(*[_in_hbm(a) for a in list(srcs) + list(lands)])
    sems = [(res[2 * s], res[2 * s + 1]) for s in range(nset)]
    return sems, list(res[2 * nset:2 * nset + ns]), list(res[2 * nset + ns:-1]), res[-1]


def split_wait(name, srcs, lands, sems, afters, copies):
    afters = [a for a in afters if a is not None]
    ns, nl, na = len(srcs), len(lands), len(afters)

    def body(*refs):
        src_refs, land_refs = refs[:ns], refs[ns:ns + nl]
        send_sems, recv_sems = refs[ns + nl:ns + nl + 2]
        for k, (src, dst, to) in enumerate(copies(src_refs, land_refs)):
            cp = _remote(src, dst, send_sems, recv_sems, k, to)
            cp.wait_send()
            cp.wait_recv()

    res = pl.pallas_call(
        body, name=name, out_shape=tuple(pltpu.HBM(a.shape, a.dtype) for a in list(srcs) + list(lands)),
        in_specs=[HBM_SPEC] * (ns + nl) + [SEM_SPEC, SEM_SPEC] + [ANY_SPEC] * na,
        out_specs=tuple([HBM_SPEC] * (ns + nl)), input_output_aliases={i: i for i in range(ns + nl)},
        compiler_params=pltpu.CompilerParams(has_side_effects=EFFECT),
    )(*srcs, *lands, sems[0], sems[1], *afters)
    return list(res[:ns]), list(res[ns:])


def _gather_plan(blocks):
    def copies(src_refs, land_refs):
        x, y, c, chips = _place()
        j_me = 2 * x + y
        plan = []
        for si, li, g, axis in blocks:
            src, land = src_refs[si], land_refs[li].at[g]
            mine = _half(src.shape, c, axis)
            for px, py in chips:
                plan.append((src.at[mine], land.at[(j_me,) + mine], (px, py, c)))
            plan.append((src, land.at[j_me], (x, y, 1 - c)))
        return plan
    return copies


def _gather_arrivals(blocks):
    def copies(src_refs, land_refs):
        x, y, c, chips = _place()
        j_me = 2 * x + y
        plan = []
        for si, li, g, axis in blocks:
            src, land = src_refs[si], land_refs[li].at[g]
            mine = _half(src.shape, c, axis)
            for px, py in chips:
                plan.append((src.at[mine], land.at[(2 * px + py,) + mine], (px, py, c)))
            plan.append((src, land.at[j_me], (x, y, 1 - c)))
        return plan
    return copies


def _pass_plan(blocks, arrivals):
    def copies(src_refs, land_refs):
        x, y, c, chips = _place()
        plan = []
        for li, g, axis in blocks:
            land = src_refs[li].at[g]
            half = _half(land.shape[1:], (1 - c) if arrivals else c, axis)
            for px, py in chips:
                part = land.at[(2 * px + py,) + half]
                plan.append((part, part, (x, y, 1 - c)))
        return plan
    return copies


def gather_pass(name, lands, blocks):
    n = len(lands)

    def body(*refs):
        outs = refs[n:2 * n]
        send_sems, recv_sems = refs[2 * n:]
        x, y, c, chips = _place()
        sent = []
        for i, (li, g, axis) in enumerate(blocks):
            land = outs[li].at[g]
            mine = _half(land.shape[1:], c, axis)
            for k, (px, py) in enumerate(chips):
                part = land.at[(2 * px + py,) + mine]
                cp = _remote(part, part, send_sems, recv_sems, 3 * i + k, (x, y, 1 - c))
                cp.start()
                sent.append(cp)
        for i, (li, g, axis) in enumerate(blocks):
            land = outs[li].at[g]
            other = _half(land.shape[1:], 1 - c, axis)
            for k, (px, py) in enumerate(chips):
                part = land.at[(2 * px + py,) + other]
                _remote(part, part, send_sems, recv_sems, 3 * i + k, (x, y, 1 - c)).wait_recv()
        for cp in sent:
            cp.wait_send()

    m = 3 * len(blocks)
    return pl.pallas_call(
        body, name=name, in_specs=[ANY_SPEC] * n, out_specs=[ANY_SPEC] * n,
        out_shape=[jax.ShapeDtypeStruct(a.shape, a.dtype) for a in lands],
        input_output_aliases={i: i for i in range(n)},
        scratch_shapes=[pltpu.SemaphoreType.DMA((m,)), pltpu.SemaphoreType.DMA((m,))],
    )(*lands)


def _pair_plan(axes):
    def copies(src_refs, land_refs):
        x, y, c, _ = _place()
        return [(src_refs[i].at[(slice(None), slice(None)) + _half(src_refs[i].shape, 1 - c, a)], land_refs[i],
                 (x, y, 1 - c)) for i, a in enumerate(axes)]
    return copies


def _scatter_plan(n):
    def copies(src_refs, land_refs):
        x, y, c, chips = _place()
        return [(src_refs[i].at[:, 2 * px + py], land_refs[i].at[k], (px, py, c))
                for i in range(n) for k, (px, py) in enumerate(chips)]
    return copies


def _broadcast_plan(axes, arrivals):
    def copies(src_refs, land_refs):
        x, y, c, _ = _place()
        plan = []
        for i, a in enumerate(axes):
            part = src_refs[i].at[(slice(None),) + _half(src_refs[i].shape, (1 - c) if arrivals else c, a)]
            plan.append((part, part, (x, y, 1 - c)))
        return plan
    return copies


N_DEV = 8
SLAB_ROWS = 16


def allreduce_small(slab, after):
    def body(x_ref, after_ref, o_ref, land, send_sems, recv_sems):
        x, y, c, _ = _place()
        me = 4 * x + 2 * y + c
        land[me] = x_ref[...]
        copies = []
        for d in range(1, N_DEV):
            to = (me + d) % N_DEV
            cp = pltpu.make_async_remote_copy(
                src_ref=x_ref, dst_ref=land.at[me], send_sem=send_sems.at[d - 1], recv_sem=recv_sems.at[me],
                device_id=(to // 4, (to // 2) % 2, to % 2), device_id_type=MESH)
            cp.start()
            copies.append(cp)
        for d in range(1, N_DEV):
            frm = (me + d) % N_DEV
            pltpu.make_async_remote_copy(
                src_ref=x_ref, dst_ref=land.at[frm], send_sem=send_sems.at[d - 1], recv_sem=recv_sems.at[frm],
                device_id=(frm // 4, (frm // 2) % 2, frm % 2), device_id_type=MESH).wait_recv()
        for cp in copies:
            cp.wait_send()
        acc = land[0]
        for s in range(1, N_DEV):
            acc = acc + land[s]
        o_ref[...] = acc

    vm = pl.BlockSpec(memory_space=pltpu.VMEM)
    return pl.pallas_call(
        body, name="allreduce_small", in_specs=[vm, ANY_SPEC], out_specs=vm,
        out_shape=jax.ShapeDtypeStruct(slab.shape, F32),
        scratch_shapes=[pltpu.VMEM((N_DEV,) + slab.shape, F32), pltpu.SemaphoreType.DMA((N_DEV - 1,)),
                        pltpu.SemaphoreType.DMA((N_DEV,))],
    )(slab, after)


BLOCK_BYTES = 3 * 1024 * 1024


def _tiles_2d(r, c, budget=BLOCK_BYTES):
    if r % 8 == 0:
        tc = c if c % LANES else _pick(c, (2048, 1408, 1024, 512, 256, 128))
        tr = 8
        for cand in (512, 256, 128, 64, 32, 16, 8):
            if r % cand == 0 and cand * tc * 4 <= budget:
                tr = cand
                break
        if tr >= 64 or c % LANES or r * LANES * 4 > budget:
            return tr, tc
    tc = LANES
    for cand in (1024, 512, 256, 128):
        if c % cand == 0 and r * cand * 4 <= budget:
            tc = cand
            break
    return r, tc


def _grid_spec(grid, in_specs, out_specs):
    return pltpu.PrefetchScalarGridSpec(num_scalar_prefetch=1, grid=grid, in_specs=in_specs, out_specs=out_specs)


def _own(axis, nr, nc):
    if axis == 0:
        return lambda i, j, where: (where[1] * nr + i, j)
    return lambda i, j, where: (i, where[1] * nc + j)


def pair_add(where, grad, recv, axis, name):
    g, jn, hr, hc = recv.shape
    tr, tc = _tiles_2d(hr, hc)
    nr, nc = hr // tr, hc // tc
    own = _own(axis, nr, nc)
    others = jn - 1

    def body(where_ref, a_ref, b_ref, o_ref):
        o_ref[...] = (a_ref[...] + b_ref[...]).astype(BF16)

    def block(a, where):
        return a // others, (where[0] + 1 + a % others) % jn

    blk = pl.BlockSpec((None, None, tr, tc), lambda a, i, j, where: block(a, where) + (i, j))
    mine = pl.BlockSpec((None, None, tr, tc), lambda a, i, j, where: block(a, where) + own(i, j, where))
    return pl.pallas_call(
        body, name=name, grid_spec=_grid_spec((g * others, nr, nc), [mine, blk], blk),
        out_shape=jax.ShapeDtypeStruct(recv.shape, BF16),
        compiler_params=_params(("parallel", "parallel", "parallel")),
    )(where, grad, recv)


def chip_add(where, grad, pair, recv, axis, name):
    g, jn, hr, hc = pair.shape
    tr, tc = _tiles_2d(hr, hc)
    nr, nc = hr // tr, hc // tc
    own = _own(axis, nr, nc)
    full = (g, 2 * hr, hc) if axis == 0 else (g, hr, 2 * hc)

    def body(where_ref, a_ref, p_ref, b_ref, o_ref):
        s = a_ref[...] + p_ref[...]
        for k in range(3):
            s = s + b_ref[k].astype(F32)
        o_ref[...] = s

    return pl.pallas_call(
        body, name=name,
        grid_spec=_grid_spec((g, nr, nc),
                             [pl.BlockSpec((None, None, tr, tc), lambda a, i, j, where: (a, where[0]) + own(i, j, where)),
                              pl.BlockSpec((None, None, tr, tc), lambda a, i, j, where: (a, where[0], i, j)),
                              pl.BlockSpec((3, None, tr, tc), lambda a, i, j, where: (0, a, i, j))],
                             pl.BlockSpec((None, tr, tc), lambda a, i, j, where: (a,) + own(i, j, where))),
        out_shape=jax.ShapeDtypeStruct(full, F32), compiler_params=_params(("parallel", "parallel", "parallel")),
    )(where, grad, pair, recv)


def _adam_math(w, g, m, v):
    m2 = ADAM_B1 * m + (1.0 - ADAM_B1) * g
    v2 = ADAM_B2 * v + (1.0 - ADAM_B2) * (g * g)
    m_hat = m2 / (1.0 - ADAM_B1 ** ADAM_STEP)
    v_hat = v2 / (1.0 - ADAM_B2 ** ADAM_STEP)
    delta = -ADAM_LR * (m_hat / (jnp.sqrt(v_hat) + ADAM_EPS) + ADAM_WD * w)
    return delta, m2, v2


def adamw(grad, idx, w, m, v, name):
    _, r, cc = w.shape
    rg = grad.shape[1]
    tr, tc = _tiles_2d(r, cc, BLOCK_BYTES // 2)
    assert rg == r or tr == r
    gr = tr if rg == r else rg

    def body(g_ref, w_ref, m_ref, v_ref, go_ref, d_ref, mo_ref, vo_ref):
        g = g_ref[pl.ds(0, tr), :]
        delta, m2, v2 = _adam_math(w_ref[...], g, m_ref[...], v_ref[...])
        go_ref[...] = g
        d_ref[...] = delta
        mo_ref[...] = m2
        vo_ref[...] = v2

    blk = pl.BlockSpec((None, tr, tc), lambda i, j: (0, i, j))
    return pl.pallas_call(
        body, name=name, grid=(r // tr, cc // tc),
        in_specs=[pl.BlockSpec((None, gr, tc), lambda i, j: (idx, i, j)), blk, blk, blk], out_specs=[blk] * 4,
        out_shape=[jax.ShapeDtypeStruct(w.shape, F32)] * 4, compiler_params=_params(("parallel", "parallel")),
    )(grad, w, m, v)


def adamw_small(g, w, m, v):
    def body(g_ref, w_ref, m_ref, v_ref, d_ref, mo_ref, vo_ref):
        delta, m2, v2 = _adam_math(w_ref[...], g_ref[...], m_ref[...], v_ref[...])
        d_ref[...] = delta
        mo_ref[...] = m2
        vo_ref[...] = v2

    return pl.pallas_call(body, name="adamw_small", out_shape=[jax.ShapeDtypeStruct(w.shape, F32)] * 3)(g, w, m, v)


GAINS = ("ffn1_pre", "ffn1_post", "mix_pre", "mix_post", "ffn2_pre", "ffn2_post", "ple_pre", "ple_post")
WEIGHTS = ("ffn1_pre_g", "ffn1_post_g", "ffn1_w_gate", "ffn1_w_up", "ffn1_w_down", "mix_pre_g", "mix_post_g",
           "mix_w_in", "fox_f_bias", "hgrn_lb_logits", "hgrn_norm_g", "mix_w_proj_fox", "mix_w_proj_hgrn",
           "mix_w_out", "ffn2_pre_g", "ffn2_post_g", "ffn2_w_gate", "ffn2_w_up", "ffn2_w_down", "ple_pre_g",
           "ple_post_g", "ple_w_gate", "ple_w_proj")
GROUPS = dict(gu1=(("ffn1_w_gate", "ffn1_w_up"), 0), down1=(("ffn1_w_down",), 0), win=(("mix_w_in",), 1),
              proj=(("mix_w_proj_fox", "mix_w_proj_hgrn"), 0), out=(("mix_w_out",), 0),
              gu2=(("ffn2_w_gate", "ffn2_w_up"), 0), down2=(("ffn2_w_down",), 0), ple_gate=(("ple_w_gate",), 0),
              ple_proj=(("ple_w_proj",), 0))
TRANSPOSED = ("mix_w_in",)
ROW_BLOCKS = ("down1", "down2", "out", "ple_gate")
GATHER_SETS = (("gate1",), ("up1",), ("down1",), ("win",), ("proj", "out"), ("gu2", "down2", "ple_gate", "ple_proj"))
GATHER_GROUPS = dict(GROUPS, gate1=(("ffn1_w_gate",), 0), up1=(("ffn1_w_up",), 0))
GATE_UP_KEYS = dict(gu1=("gate1", "up1"), gu2=("gu2",))
REDUCE_SETS = (("ple_gate", "ple_proj", "down2", "gu2"), ("out", "proj", "win"), ("down1",), ("gu1",))


def _pad_row(a, width):
    a = a.reshape(1, -1)
    return jnp.pad(a, ((0, 0), (0, width - a.shape[1])))


def _pack_small(vals):
    d = D_MODEL
    rows = [vals[n + "_g"].reshape(1, d) for n in GAINS]
    rows.append(_pad_row(vals["fox_f_bias"], d))
    lg = vals["hgrn_lb_logits"]
    rows += [_pad_row(lg[0], d), _pad_row(lg[1], d), _pad_row(vals["hgrn_norm_g"], d)]
    slab = jnp.concatenate(rows, axis=0)
    return jnp.pad(slab, ((0, SLAB_ROWS - slab.shape[0]), (0, 0)))


def _unpack_small(slab):
    out = {n + "_g": slab[i:i + 1] for i, n in enumerate(GAINS)}
    out["fox_f_bias"] = slab[8:9, :HEADS]
    out["hgrn_lb_logits"] = slab[9:11, :WIDTH]
    out["hgrn_norm_g"] = slab[11:12, :HEAD_DIM]
    return out


def _split_in(win_t):
    lo = 3 * WIDTH
    main = jnp.concatenate([win_t[:lo], win_t[lo + HEADS:]], axis=0)
    fa = jnp.pad(win_t[lo:lo + HEADS], ((0, LANES - HEADS), (0, 0)))
    return main, fa


def _join_in(main, fa):
    lo = 3 * WIDTH
    return jnp.concatenate([main[:lo], fa[:HEADS], main[lo:]], axis=0)


def _as_block(name, a):
    return jnp.swapaxes(a, 1, 2) if name in TRANSPOSED else a


def _send_block(name, a, mark):
    blk = _as_block(name, a)[0]
    if mark is not None:
        blk = blk + mark[0, 0]
    return blk.astype(BF16)


def kernel(x, p, ffn1_pre_g, ffn1_post_g, ffn1_w_gate, ffn1_w_up, ffn1_w_down, mix_pre_g, mix_post_g, mix_w_in, fox_f_bias, hgrn_lb_logits, hgrn_norm_g, mix_w_proj_fox, mix_w_proj_hgrn, mix_w_out, ffn2_pre_g, ffn2_post_g, ffn2_w_gate, ffn2_w_up, ffn2_w_down, ple_pre_g, ple_post_g, ple_w_gate, ple_w_proj, loss_target, m_ffn1_pre_g, m_ffn1_post_g, m_ffn1_w_gate, m_ffn1_w_up, m_ffn1_w_down, m_mix_pre_g, m_mix_post_g, m_mix_w_in, m_fox_f_bias, m_hgrn_lb_logits, m_hgrn_norm_g, m_mix_w_proj_fox, m_mix_w_proj_hgrn, m_mix_w_out, m_ffn2_pre_g, m_ffn2_post_g, m_ffn2_w_gate, m_ffn2_w_up, m_ffn2_w_down, m_ple_pre_g, m_ple_post_g, m_ple_w_gate, m_ple_w_proj, v_ffn1_pre_g, v_ffn1_post_g, v_ffn1_w_gate, v_ffn1_w_up, v_ffn1_w_down, v_mix_pre_g, v_mix_post_g, v_mix_w_in, v_fox_f_bias, v_hgrn_lb_logits, v_hgrn_norm_g, v_mix_w_proj_fox, v_mix_w_proj_hgrn, v_mix_w_out, v_ffn2_pre_g, v_ffn2_post_g, v_ffn2_w_gate, v_ffn2_w_up, v_ffn2_w_down, v_ple_pre_g, v_ple_post_g, v_ple_w_gate, v_ple_w_proj):
    args = dict(locals())
    wts = {n: args[n] for n in WEIGHTS}
    mom = {n: args["m_" + n] for n in WEIGHTS}
    var = {n: args["v_" + n] for n in WEIGHTS}
    d = D_MODEL
    where = jnp.stack([2 * lax.axis_index("x") + lax.axis_index("y"), lax.axis_index("c")]).astype(jnp.int32)

    def start_sets(name, which, mark):
        srcs, lands, plans = [], [], []
        for si in which:
            blocks = []
            for g in GATHER_SETS[si]:
                names, axis = GATHER_GROUPS[g]
                for pos, n in enumerate(names):
                    blocks.append((len(srcs), len(lands), pos, axis))
                    srcs.append(_send_block(n, wts[n], mark))
                lands.append(lax.empty((len(names), N_CHIPS) + srcs[-1].shape, BF16))
            plans.append(blocks)
        sems, srcs, lands, mark = split_start(name, srcs, lands, [4 * len(b) for b in plans],
                                              lambda sr, lr: [_gather_plan(b)(sr, lr) for b in plans])
        out = {}
        for k, si in enumerate(which):
            s_idx = sorted({b[0] for b in plans[k]})
            l_idx = sorted({b[1] for b in plans[k]})
            local = [(s_idx.index(a), l_idx.index(b), pos, ax) for a, b, pos, ax in plans[k]]
            out[si] = (sems[k], [srcs[i] for i in s_idx], [lands[i] for i in l_idx], local)
        return out, mark

    flying, mark = start_sets("gather_start_0", [0], None)
    rest, all_started = start_sets("gather_start_1", list(range(1, len(GATHER_SETS))), mark)
    flying.update(rest)
    full, passing = {}, {}

    def set_of(key):
        g = "win" if key in ("in_main", "in_fa") else key
        return g, [g in s for s in GATHER_SETS].index(True)

    def arrive(si, after):
        sem, srcs, lands, local = flying.pop(si)
        _, got = split_wait("gather_wait_%d" % si, srcs, lands, sem, [after, all_started], _gather_arrivals(local))
        return got, [(b, pos, ax) for _, b, pos, ax in local]

    def early(key, after):
        g, si = set_of(key)
        if g in full or si not in flying:
            return None
        got, blocks = arrive(si, after)
        sem, got, _, mark = split_start("pass_start_%d" % si, got, [], [3 * len(blocks)],
                                        lambda sr, lr: [_pass_plan(blocks, False)(sr, lr)])
        passing[si] = (sem[0], got, blocks)
        return mark

    def land_set(si, after):
        if si in passing:
            sem, got, blocks = passing.pop(si)
            got, _ = split_wait("pass_wait_%d" % si, got, [], sem, [after], _pass_plan(blocks, True))
        else:
            got, blocks = arrive(si, after)
            got = gather_pass("gather_pass_%d" % si, got, blocks)
        for g, arr in zip(GATHER_SETS[si], got):
            if g == "win":
                full["win"] = arr
                full["in_main"], full["in_fa"] = _split_in(arr.reshape(-1, d))
            else:
                full[g] = arr.reshape(-1, d) if g in ROW_BLOCKS else arr

    def get_w(key, after=None):
        g, si = set_of(key)
        if g not in full:
            land_set(si, after)
        return full[key]

    grads, pairing, started = {}, [], {}
    rows4 = lambda a: a.reshape(1, N_CHIPS, a.shape[0] // N_CHIPS, a.shape[1])

    def emit(key, grad):
        if key in ("in_main", "in_fa"):
            grads[key] = grad
            if "in_main" not in grads or "in_fa" not in grads:
                return None
            key, grad = "win", rows4(_join_in(grads["in_main"], grads["in_fa"]))
        grads[key] = grad if grad.ndim == 4 else rows4(grad)
        for si, s in enumerate(REDUCE_SETS):
            if key in s and all(g in grads for g in s):
                axes = [GROUPS[g][1] for g in s]
                own = [grads[g] for g in s]
                zones = [lax.empty(_half_shape(a.shape, ax), F32) for a, ax in zip(own, axes)]
                plan = _pair_plan(axes)
                sem, own, zones, mark = split_start("pair_start_%d" % si, own, zones, [len(s)],
                                                    lambda sr, lr: [plan(sr, lr)])
                pairing.append((si, sem[0], own, zones, axes, plan))
                return mark
        return None

    def advance(value):
        mark = None
        while pairing:
            si, sem, own, zones, axes, plan = pairing.pop(0)
            s = REDUCE_SETS[si]
            own, recv = split_wait("pair_wait_%d" % si, own, zones, sem, [value], plan)
            parts = [pair_add(where, a, r, ax, "pair_add_" + g) for g, a, r, ax in zip(s, own, recv, axes)]
            zones = [lax.empty((3, q.shape[0]) + q.shape[2:], BF16) for q in parts]
            plan = _scatter_plan(len(s))
            sem, parts, zones, mark = split_start("scatter_start_%d" % si, parts, zones, [3 * len(s)],
                                                  lambda sr, lr: [plan(sr, lr)])
            started[si] = (sem[0], parts, zones, own, recv, axes, plan)
        return mark

    gains = {n: wts[n + "_g"] for n in GAINS}
    loss, dx, small = layer_step(x[0], p[0, 0].astype(BF16), loss_target[0], gains, _pad_row(fox_f_bias, LANES),
                                 hgrn_lb_logits, hgrn_norm_g, get_w, emit, advance, early)

    out_g, out_d, out_m, out_v = {}, {}, {}, {}
    after, crossing = None, []

    def finish(si, sem, halves, axes, mark):
        reduced, _ = split_wait("broadcast_wait_%d" % si, halves, [], sem, [mark], _broadcast_plan(axes, True))
        last = None
        for g, red in zip(REDUCE_SETS[si], reduced):
            for idx, n in enumerate(GROUPS[g][0]):
                res = adamw(red, idx, _as_block(n, wts[n]), _as_block(n, mom[n]), _as_block(n, var[n]), "adamw_" + n)
                out_g[n], out_d[n], out_m[n], out_v[n] = [_as_block(n, r) for r in res]
                last = res[1]
        return last

    for si, s in enumerate(REDUCE_SETS):
        sem, parts, zones, own, recv, axes, plan = started[si]
        _, zones = split_wait("scatter_wait_%d" % si, parts, zones, sem, [dx, after], plan)
        halves = [chip_add(where, a, r, z, ax, "chip_add_" + g) for g, a, r, z, ax in zip(s, own, recv, zones, axes)]
        plan = _broadcast_plan(axes, False)
        sem, halves, _, mark = split_start("broadcast_start_%d" % si, halves, [], [len(s)],
                                           lambda sr, lr: [plan(sr, lr)])
        crossing.append((si, sem[0], halves, axes))
        if len(crossing) > 2:
            after = finish(*crossing.pop(0), mark)
    while crossing:
        after = finish(*crossing.pop(0), after)

    small_named = {n + "_g": small[n] for n in GAINS}
    small_named.update(fox_f_bias=small["fox_bias"][:, :HEADS], hgrn_lb_logits=small["lb_logits"],
                       hgrn_norm_g=small["norm_g"])
    g_small = allreduce_small(_pack_small(small_named), after)
    d_small, m_small, v_small = adamw_small(g_small, _pack_small(wts), _pack_small(mom), _pack_small(var))

    for dst, slab in ((out_g, g_small), (out_d, d_small), (out_m, m_small), (out_v, v_small)):
        dst.update(_unpack_small(slab))

    total = lax.psum(loss[0, 0], ("x", "y", "c"))
    return (total, dx[None], *[out_g[n] for n in WEIGHTS], *[out_d[n] for n in WEIGHTS],
            *[out_m[n] for n in WEIGHTS], *[out_v[n] for n in WEIGHTS])
```

```python
import functools

import jax
import jax.numpy as jnp
from jax import lax
from jax.experimental import pallas as pl
from jax.experimental.pallas import tpu as pltpu

F32 = jnp.float32
BF16 = jnp.bfloat16

D_MODEL = 2048
SEQ = 2048
D_FF = 5632
PLE_DIM = 256
HEADS = 8
HEAD_DIM = 128
WIDTH = HEADS * HEAD_DIM
CHUNK = 64
SUB = 16
HGRN_HEADS_PER_STEP = 2
NORM_EPS = 1e-6
MACARON_SCALE = 0.5
N_CHIPS = 4

ADAM_LR = 0.001
ADAM_B1 = 0.9
ADAM_B2 = 0.999
ADAM_EPS = 1e-08
ADAM_WD = 0.01
ADAM_STEP = 10

LANES = 128
VMEM_LIMIT = 56 * 1024 * 1024
NEG_BIG = -1e30
MESH = pl.DeviceIdType.MESH


def _pick(n, cands):
    for c in cands:
        if c <= n and n % c == 0:
            return c
    return n


def _params(sem, vmem=VMEM_LIMIT):
    return pltpu.CompilerParams(dimension_semantics=sem, vmem_limit_bytes=vmem)


def _sigmoid(x):
    return 1.0 / (1.0 + jnp.exp(-x))


def _silu(x):
    return x * _sigmoid(x)


def _silu_grad(x):
    s = _sigmoid(x)
    return s * (1.0 + x * (1.0 - s))


_DN = {"nn": (((1,), (0,)), ((), ())), "nt": (((1,), (1,)), ((), ())), "tn": (((0,), (0,)), ((), ()))}


def _mm(a, b, *, mode, grid, a_spec, b_spec, o_spec, out_shape, acc_shape, name, after=(), init=None, into=None):
    nk = grid[2]
    dn = _DN[mode]
    after = [m for m in after if m is not None]
    extra = ([init] if init is not None else []) + ([into] if into is not None else []) + after
    n_extra = len(extra)

    def body(a_ref, b_ref, *rest):
        o_ref, acc_ref = rest[n_extra:]
        k = pl.program_id(2)

        @pl.when(k == 0)
        def _():
            acc_ref[...] = jnp.zeros_like(acc_ref) if init is None else rest[0][...].astype(F32)

        acc_ref[...] += lax.dot_general(a_ref[...].astype(BF16), b_ref[...].astype(BF16), dn,
                                        preferred_element_type=F32)

        @pl.when(k == nk - 1)
        def _():
            o_ref[...] = acc_ref[...].astype(o_ref.dtype)

    anywhere = pl.BlockSpec(memory_space=pl.ANY)
    return pl.pallas_call(
        body, name=name, grid=grid,
        in_specs=[a_spec, b_spec] + ([o_spec] if init is not None else []) + [anywhere] * (n_extra - (init is not None)),
        out_specs=o_spec, out_shape=out_shape, scratch_shapes=[pltpu.VMEM(acc_shape, F32)],
        input_output_aliases={} if into is None else {2 + (init is not None): 0},
        compiler_params=_params(("parallel", "parallel", "arbitrary")),
    )(a, b, *extra)


def mm_nn_2d(a, b, out_dtype, name, after=()):
    m, kk = a.shape
    n = b.shape[1]
    tm, tn = _pick(m, (512, 256, 128)), _pick(n, (1024, 512, 256, 128))
    tk = _pick(kk, (5632, 2816, 2048, 1408, 1024, 512, 256, 128))
    return _mm(a, b, mode="nn", grid=(m // tm, n // tn, kk // tk),
               a_spec=pl.BlockSpec((tm, tk), lambda i, j, k: (i, k)),
               b_spec=pl.BlockSpec((tk, tn), lambda i, j, k: (k, j)),
               o_spec=pl.BlockSpec((tm, tn), lambda i, j, k: (i, j)),
               out_shape=jax.ShapeDtypeStruct((m, n), out_dtype), acc_shape=(tm, tn), name=name, after=after)


def mm_nt_2d(a, b, out_dtype, name, after=()):
    m, c = a.shape
    n = b.shape[0]
    tm, tn, tk = _pick(m, (512, 256, 128)), _pick(n, (1408, 1024, 512, 256, 128)), _pick(c, (2048, 1408, 1024, 512, 256, 128))
    return _mm(a, b, mode="nt", grid=(m // tm, n // tn, c // tk),
               a_spec=pl.BlockSpec((tm, tk), lambda i, j, k: (i, k)),
               b_spec=pl.BlockSpec((tn, tk), lambda i, j, k: (j, k)),
               o_spec=pl.BlockSpec((tm, tn), lambda i, j, k: (i, j)),
               out_shape=jax.ShapeDtypeStruct((m, n), out_dtype), acc_shape=(tm, tn), name=name, after=after)


def mm_tn_2d(a, b, out_dtype, name):
    c, m = a.shape
    n = b.shape[1]
    tm, tn, tk = _pick(m, (1408, 1024, 512, 256, 128)), _pick(n, (1024, 512, 256, 128)), _pick(c, (2048, 1024, 512, 256, 128))
    return _mm(a, b, mode="tn", grid=(m // tm, n // tn, c // tk),
               a_spec=pl.BlockSpec((tk, tm), lambda i, j, k: (k, i)),
               b_spec=pl.BlockSpec((tk, tn), lambda i, j, k: (k, j)),
               o_spec=pl.BlockSpec((tm, tn), lambda i, j, k: (i, j)),
               out_shape=jax.ShapeDtypeStruct((m, n), out_dtype), acc_shape=(tm, tn), name=name)


def mm_nn_col(a, w, out_dtype, name, slot0=0, total=None, into=None):
    m, kk = a.shape
    g, jn, _, ns = w.shape
    total = g if total is None else total
    tm, tk = _pick(m, (512, 256, 128)), _pick(kk, (2048, 1024, 512, 256, 128))
    return _mm(a, w, mode="nn", grid=(m // tm, g * jn, kk // tk),
               a_spec=pl.BlockSpec((tm, tk), lambda i, j, k: (i, k)),
               b_spec=pl.BlockSpec((None, None, tk, ns), lambda i, j, k: (j // jn, j % jn, k, 0)),
               o_spec=pl.BlockSpec((None, tm, ns), lambda i, j, k: (slot0 + j // jn, i, j % jn)),
               out_shape=jax.ShapeDtypeStruct((total, m, jn * ns), out_dtype), acc_shape=(tm, ns), name=name,
               into=into)


def mm_nt_col(a, w, out_dtype, name, after=(), slot0=0, init=None):
    _, m, _ = a.shape
    g, jn, kk, ns = w.shape
    kc = 2 if jn % 2 == 0 else 1
    per = jn // kc
    nk = g * per
    tm, tn = _pick(m, (512, 256, 128)), _pick(kk, (1024, 512, 256, 128))
    extra = ([init] if init is not None else []) + [v for v in after if v is not None]

    def body(a_ref, w_ref, *rest):
        o_ref, acc_ref = rest[len(extra):]
        k = pl.program_id(2)
        part = None
        for c in range(kc):
            prod = lax.dot_general(a_ref[:, c * ns:(c + 1) * ns].astype(BF16), w_ref[c].astype(BF16), _DN["nt"],
                                   preferred_element_type=F32)
            part = prod if part is None else part + prod

        @pl.when(k == 0)
        def _():
            acc_ref[...] = part if init is None else rest[0][...].astype(F32) + part

        @pl.when(k > 0)
        def _():
            acc_ref[...] += part

        @pl.when(k == nk - 1)
        def _():
            o_ref[...] = acc_ref[...].astype(o_ref.dtype)

    o_spec = pl.BlockSpec((tm, tn), lambda i, j, k: (i, j))
    anywhere = pl.BlockSpec(memory_space=pl.ANY)
    return pl.pallas_call(
        body, name=name, grid=(m // tm, kk // tn, nk),
        in_specs=[pl.BlockSpec((None, tm, kc * ns), lambda i, j, k: (slot0 + k // per, i, k % per)),
                  pl.BlockSpec((None, kc, tn, ns), lambda i, j, k: (k // per, k % per, j, 0))]
        + ([o_spec] if init is not None else []) + [anywhere] * (len(extra) - (init is not None)),
        out_specs=o_spec, out_shape=jax.ShapeDtypeStruct((m, kk), out_dtype),
        scratch_shapes=[pltpu.VMEM((tm, tn), F32)],
        compiler_params=_params(("parallel", "parallel", "arbitrary")),
    )(a, w, *extra)


def mm_tn_col(a, b, jn, out_dtype, name):
    c, kk = a.shape
    g, _, n = b.shape
    ns = n // jn
    tm, tk = _pick(kk, (512, 256, 128)), _pick(c, (2048, 1024, 512, 256, 128))
    return _mm(a, b, mode="tn", grid=(kk // tm, g * jn, c // tk),
               a_spec=pl.BlockSpec((tk, tm), lambda i, j, k: (k, i)),
               b_spec=pl.BlockSpec((None, tk, ns), lambda i, j, k: (j // jn, k, j % jn)),
               o_spec=pl.BlockSpec((None, None, tm, ns), lambda i, j, k: (j // jn, j % jn, i, 0)),
               out_shape=jax.ShapeDtypeStruct((g, jn, kk, ns), out_dtype), acc_shape=(tm, ns), name=name)


def _rstd(x):
    return lax.rsqrt(jnp.mean(x * x, axis=-1, keepdims=True) + NORM_EPS)


def _rms_bwd(x, g, dy):
    r = _rstd(x)
    xn = x * r
    dyg = dy * g
    dx = r * (dyg - xn * jnp.mean(dyg * xn, axis=-1, keepdims=True))
    return dx, jnp.sum(dy * xn, axis=0, keepdims=True)


def _row_tile(t):
    return _pick(t, (256, 128, 64, 32, 16, 8))


def norm_in(h, g, name):
    t, d = h.shape
    tr = _row_tile(t)

    def body(h_ref, g_ref, u_ref):
        x = h_ref[...]
        u_ref[...] = (x * _rstd(x) * g_ref[...]).astype(BF16)

    return pl.pallas_call(
        body, name=name, grid=(t // tr,),
        in_specs=[pl.BlockSpec((tr, d), lambda i: (i, 0)), pl.BlockSpec((1, d), lambda i: (0, 0))],
        out_specs=pl.BlockSpec((tr, d), lambda i: (i, 0)),
        out_shape=jax.ShapeDtypeStruct((t, d), BF16), compiler_params=_params(("parallel",)),
    )(h, g)


def resid_norm(h, y, g, scale, g_next, name):
    t, d = h.shape
    tr = _row_tile(t)

    def body(h_ref, y_ref, g_ref, gn_ref, o_ref, u_ref):
        yv = y_ref[...]
        out = h_ref[...] + scale * (yv * _rstd(yv) * g_ref[...])
        o_ref[...] = out
        u_ref[...] = (out * _rstd(out) * gn_ref[...]).astype(BF16)

    row = pl.BlockSpec((tr, d), lambda i: (i, 0))
    vec = pl.BlockSpec((1, d), lambda i: (0, 0))
    return pl.pallas_call(
        body, name=name, grid=(t // tr,), in_specs=[row, row, vec, vec], out_specs=[row, row],
        out_shape=[jax.ShapeDtypeStruct((t, d), F32), jax.ShapeDtypeStruct((t, d), BF16)],
        compiler_params=_params(("parallel",)),
    )(h, y, g, g_next)


def post_bwd(dh, y, g, scale, name):
    t, d = dh.shape
    tr = _row_tile(t)

    def body(dh_ref, y_ref, g_ref, dy_ref, dg_ref):
        @pl.when(pl.program_id(0) == 0)
        def _():
            dg_ref[...] = jnp.zeros_like(dg_ref)

        dx, dg = _rms_bwd(y_ref[...], g_ref[...], scale * dh_ref[...])
        dy_ref[...] = dx.astype(BF16)
        dg_ref[...] += dg

    row = pl.BlockSpec((tr, d), lambda i: (i, 0))
    vec = pl.BlockSpec((1, d), lambda i: (0, 0))
    return pl.pallas_call(
        body, name=name, grid=(t // tr,), in_specs=[row, row, vec], out_specs=[row, vec],
        out_shape=[jax.ShapeDtypeStruct((t, d), BF16), jax.ShapeDtypeStruct((1, d), F32)],
        compiler_params=_params(("arbitrary",)),
    )(dh, y, g)


def pre_bwd(dh, h, g, dus, name, after=()):
    t, d = dh.shape
    tr = _row_tile(t)
    n_du = len(dus)
    after = [m for m in after if m is not None]

    def body(*refs):
        dh_ref, h_ref, g_ref = refs[:3]
        du_refs = refs[3:3 + n_du]
        o_ref, dg_ref = refs[3 + n_du + len(after):]

        @pl.when(pl.program_id(0) == 0)
        def _():
            dg_ref[...] = jnp.zeros_like(dg_ref)

        du = du_refs[0][...]
        for r in du_refs[1:]:
            du = du + r[...]
        dx, dg = _rms_bwd(h_ref[...], g_ref[...], du)
        o_ref[...] = dh_ref[...] + dx
        dg_ref[...] += dg

    row = pl.BlockSpec((tr, d), lambda i: (i, 0))
    vec = pl.BlockSpec((1, d), lambda i: (0, 0))
    return pl.pallas_call(
        body, name=name, grid=(t // tr,),
        in_specs=[row, row, vec] + [row] * n_du + [pl.BlockSpec(memory_space=pl.ANY)] * len(after),
        out_specs=[row, vec], out_shape=[jax.ShapeDtypeStruct((t, d), F32), jax.ShapeDtypeStruct((1, d), F32)],
        compiler_params=_params(("arbitrary",)),
    )(dh, h, g, *dus, *after)


def _ew_tiles(t, f):
    return _pick(t, (256, 128, 64, 32, 16, 8)), _pick(f, (1408, 1024, 512, 256, 128))


def swiglu_act(gu, name):
    _, t, f = gu.shape
    tr, tc = _ew_tiles(t, f)

    def body(gu_ref, o_ref):
        o_ref[...] = (_silu(gu_ref[0]) * gu_ref[1]).astype(BF16)

    return pl.pallas_call(
        body, name=name, grid=(t // tr, f // tc),
        in_specs=[pl.BlockSpec((2, tr, tc), lambda i, j: (0, i, j))],
        out_specs=pl.BlockSpec((tr, tc), lambda i, j: (i, j)),
        out_shape=jax.ShapeDtypeStruct((t, f), BF16), compiler_params=_params(("parallel", "parallel")),
    )(gu)


def swiglu_bwd(dact, gu, name):
    _, t, f = gu.shape
    tr, tc = _ew_tiles(t, f)

    def body(da_ref, gu_ref, o_ref):
        da = da_ref[...]
        gate = gu_ref[0]
        o_ref[0] = (da * gu_ref[1] * _silu_grad(gate)).astype(BF16)
        o_ref[1] = (da * _silu(gate)).astype(BF16)

    return pl.pallas_call(
        body, name=name, grid=(t // tr, f // tc),
        in_specs=[pl.BlockSpec((tr, tc), lambda i, j: (i, j)), pl.BlockSpec((2, tr, tc), lambda i, j: (0, i, j))],
        out_specs=pl.BlockSpec((2, tr, tc), lambda i, j: (0, i, j)),
        out_shape=jax.ShapeDtypeStruct((2, t, f), BF16), compiler_params=_params(("parallel", "parallel")),
    )(dact, gu)


def _col_blocks():
    w = WIDTH // LANES
    return dict(q_a=0, k_a=w, v_a=2 * w, q_b=3 * w, f_b=4 * w, i_b=5 * w, g_b=6 * w, gate_a=7 * w,
                gate_b=7 * w + D_MODEL // LANES)


def _tri(n, lower):
    r = lax.broadcasted_iota(jnp.int32, (n, n), 0)
    c = lax.broadcasted_iota(jnp.int32, (n, n), 1)
    return jnp.where((r >= c) if lower else (r <= c), 1.0, 0.0).astype(F32)


def _dot_hi(a, b):
    return jnp.dot(a, b, precision=lax.Precision.HIGHEST, preferred_element_type=F32)


def fox_prep(fa, bias, name):
    t, w = fa.shape
    tb = _pick(t, (256, 128, 64))

    def body(fa_ref, b_ref, c_ref, carry_ref):
        @pl.when(pl.program_id(0) == 0)
        def _():
            carry_ref[...] = jnp.zeros_like(carry_ref)

        z = fa_ref[...] + b_ref[...]
        lf = jnp.minimum(z, 0.0) - jnp.log(1.0 + jnp.exp(-jnp.abs(z)))
        c = _dot_hi(_tri(tb, True), lf) + carry_ref[...]
        c_ref[...] = c
        carry_ref[...] = carry_ref[...] + jnp.sum(lf, axis=0, keepdims=True)

    return pl.pallas_call(
        body, name=name, grid=(t // tb,),
        in_specs=[pl.BlockSpec((tb, w), lambda i: (i, 0)), pl.BlockSpec((1, w), lambda i: (0, 0))],
        out_specs=pl.BlockSpec((tb, w), lambda i: (i, 0)),
        out_shape=jax.ShapeDtypeStruct((t, w), F32), scratch_shapes=[pltpu.VMEM((1, w), F32)],
        compiler_params=_params(("arbitrary",)),
    )(fa, bias)


def fox_post_bwd(dc, fa, bias, name):
    t, w = fa.shape
    tb = _pick(t, (256, 128, 64))
    nb = t // tb

    def body(dc_ref, fa_ref, b_ref, dfa_ref, db_ref, carry_ref):
        @pl.when(pl.program_id(0) == 0)
        def _():
            carry_ref[...] = jnp.zeros_like(carry_ref)
            db_ref[...] = jnp.zeros_like(db_ref)

        dcv = dc_ref[...]
        dlf = _dot_hi(_tri(tb, False), dcv) + carry_ref[...]
        z = fa_ref[...] + b_ref[...]
        dz = dlf * _sigmoid(-z)
        dfa_ref[...] = dz.astype(BF16)
        db_ref[...] += jnp.sum(dz, axis=0, keepdims=True)
        carry_ref[...] = carry_ref[...] + jnp.sum(dcv, axis=0, keepdims=True)

    rev = pl.BlockSpec((tb, w), lambda i: (nb - 1 - i, 0))
    vec = pl.BlockSpec((1, w), lambda i: (0, 0))
    return pl.pallas_call(
        body, name=name, grid=(nb,), in_specs=[rev, rev, vec], out_specs=[rev, vec],
        out_shape=[jax.ShapeDtypeStruct((t, w), BF16), jax.ShapeDtypeStruct((1, w), F32)],
        scratch_shapes=[pltpu.VMEM((1, w), F32)], compiler_params=_params(("arbitrary",)),
    )(dc, fa, bias)


def _fox_probs(q_ref, k_ref, cc_ref, cr_ref, qi, tq, t):
    scale = HEAD_DIM ** -0.5
    s = lax.dot_general(q_ref[...].astype(BF16), k_ref[...].astype(BF16), _DN["nt"], preferred_element_type=F32)
    logits = s * scale + cc_ref[...] - cr_ref[...]
    qpos = qi * tq + lax.broadcasted_iota(jnp.int32, (tq, t), 0)
    kpos = lax.broadcasted_iota(jnp.int32, (tq, t), 1)
    logits = jnp.where(kpos <= qpos, logits, NEG_BIG)
    m = jnp.max(logits, axis=-1, keepdims=True)
    p = jnp.exp(logits - m)
    return p / jnp.sum(p, axis=-1, keepdims=True)


FOX_SEGMENTS = 4


def _fox_segments(t):
    tq = _pick(t, (256, 128))
    nseg = min(FOX_SEGMENTS, t // tq)
    return tq, nseg, t // tq // nseg


def _fox_specs(t, q0, kt, tq):
    cb = _col_blocks()
    dh = HEAD_DIM
    return [pl.BlockSpec((tq, dh), lambda h, i: (q0 + i, cb["q_a"] + h)),
            pl.BlockSpec((kt, dh), lambda h, i: (0, cb["k_a"] + h)),
            pl.BlockSpec((kt, dh), lambda h, i: (0, cb["v_a"] + h)),
            pl.BlockSpec((None, tq, 1), lambda h, i: (h, q0 + i, 0)),
            pl.BlockSpec((None, 1, kt), lambda h, i: (h, 0, 0))]


def fox_fwd(proj, c_col, c_row, name):
    t = proj.shape[0]
    tq, nseg, nq = _fox_segments(t)
    dh = HEAD_DIM

    def segment(out, r):
        q0, kt = r * nq, (r + 1) * nq * tq

        def body(q_ref, k_ref, v_ref, cc_ref, cr_ref, prev_ref, o_ref):
            p = _fox_probs(q_ref, k_ref, cc_ref, cr_ref, q0 + pl.program_id(1), tq, kt)
            o_ref[...] = jnp.dot(p.astype(BF16), v_ref[...].astype(BF16), preferred_element_type=F32).astype(BF16)

        return pl.pallas_call(
            body, name="%s_%d" % (name, r), grid=(HEADS, nq),
            in_specs=_fox_specs(t, q0, kt, tq) + [pl.BlockSpec(memory_space=pl.ANY)],
            out_specs=pl.BlockSpec((tq, dh), lambda h, i: (q0 + i, h)),
            out_shape=jax.ShapeDtypeStruct((t, WIDTH), BF16), input_output_aliases={5: 0},
            compiler_params=_params(("parallel", "parallel")),
        )(proj, proj, proj, c_col, c_row, out)

    out = lax.empty((t, WIDTH), BF16)
    for r in range(nseg):
        out = segment(out, r)
    return out


def fox_bwd(proj, c_col, c_row, do_ab, name):
    t = proj.shape[0]
    tq, nseg, nq = _fox_segments(t)
    dh = HEAD_DIM
    scale = HEAD_DIM ** -0.5

    def segment(acc, r):
        q0, kt = r * nq, (r + 1) * nq * tq

        def body(q_ref, k_ref, v_ref, cc_ref, cr_ref, do_ref, dqp_ref, dkp_ref, dvp_ref, dccp_ref, dcrp_ref,
                 dq_ref, dk_ref, dv_ref, dcc_ref, dcr_ref):
            @pl.when(pl.program_id(1) == 0)
            def _():
                dk_ref[...] = dkp_ref[...]
                dv_ref[...] = dvp_ref[...]
                dcr_ref[...] = dcrp_ref[...]

            p = _fox_probs(q_ref, k_ref, cc_ref, cr_ref, q0 + pl.program_id(1), tq, kt)
            dov = do_ref[...].astype(BF16)
            kb = k_ref[...].astype(BF16)
            dv_ref[...] += lax.dot_general(p.astype(BF16), dov, _DN["tn"], preferred_element_type=F32)
            dp = lax.dot_general(dov, v_ref[...].astype(BF16), _DN["nt"], preferred_element_type=F32)
            ds = p * (dp - jnp.sum(p * dp, axis=-1, keepdims=True))
            dcc_ref[...] = jnp.sum(ds, axis=-1, keepdims=True)
            dcr_ref[...] -= jnp.sum(ds, axis=0, keepdims=True)
            dss = (ds * scale).astype(BF16)
            dq_ref[...] = jnp.dot(dss, kb, preferred_element_type=F32).astype(BF16)
            dk_ref[...] += lax.dot_general(dss, q_ref[...].astype(BF16), _DN["tn"], preferred_element_type=F32)

        rows = pl.BlockSpec((tq, dh), lambda h, i: (q0 + i, h))
        keys = pl.BlockSpec((kt, dh), lambda h, i: (0, h))
        col = pl.BlockSpec((None, tq, 1), lambda h, i: (h, q0 + i, 0))
        row = pl.BlockSpec((None, 1, kt), lambda h, i: (h, 0, 0))
        anywhere = pl.BlockSpec(memory_space=pl.ANY)
        return pl.pallas_call(
            body, name="%s_%d" % (name, r), grid=(HEADS, nq),
            in_specs=_fox_specs(t, q0, kt, tq)
            + [pl.BlockSpec((None, tq, dh), lambda h, i: (0, q0 + i, h)), anywhere, keys, keys, anywhere, row],
            out_specs=[rows, keys, keys, col, row],
            out_shape=[jax.ShapeDtypeStruct(a.shape, a.dtype) for a in acc],
            input_output_aliases={6 + k: k for k in range(5)},
            compiler_params=_params(("parallel", "arbitrary")),
        )(proj, proj, proj, c_col, c_row, do_ab, *acc)

    acc = [lax.empty((t, WIDTH), BF16), jnp.zeros((t, WIDTH), F32), jnp.zeros((t, WIDTH), F32),
           lax.empty((HEADS, t, 1), F32), jnp.zeros((HEADS, 1, t), F32)]
    for r in range(nseg):
        acc = segment(acc, r)
    return acc


def _lower_bound(lg_ref):
    l0 = lg_ref[0:1, :]
    l1 = lg_ref[1:2, :]
    m = jnp.maximum(l0, l1)
    e0 = jnp.exp(l0 - m)
    e1 = jnp.exp(l1 - m)
    return e0 / (e0 + e1)


def _hgrn_inputs(qb_ref, fb_ref, lg_ref, q_s, k_s, cum_s):
    lb = _lower_bound(lg_ref)
    sig = _sigmoid(fb_ref[...])
    f = lb + (1.0 - lb) * sig
    q_s[...] = _silu(qb_ref[...])
    k_s[...] = 1.0 - f
    cum_s[...] = _dot_hi(_tri(CHUNK, True), jnp.log(f))
    return lb, sig, f


def _boundary(cum_s, a):
    if a == 0:
        return jnp.zeros((1, HEAD_DIM), F32)
    return cum_s[pl.ds(SUB * a - 1, 1), :]


def _hgrn_scores(q_s, k_s, cum_s):
    cum = cum_s[...]
    kk = k_s[...]
    lane = lax.broadcasted_iota(jnp.int32, (SUB, CHUNK), 1)
    row = lax.broadcasted_iota(jnp.int32, (SUB, 1), 0)
    blocks = []
    for a in range(CHUNK // SUB):
        rows = pl.ds(SUB * a, SUB)
        ca = _boundary(cum_s, a)
        cum_a = cum_s[rows, :]
        q_a = q_s[rows, :]
        qa = q_a * jnp.exp(cum_a - ca)
        ka = kk * jnp.exp(jnp.minimum(ca - cum, 0.0))
        blk = lax.dot_general(qa, ka, _DN["nt"], preferred_element_type=F32)
        blk = jnp.where(lane < SUB * a, blk, 0.0)
        for s in range(SUB):
            r = SUB * a + s
            e = jnp.exp(jnp.minimum(cum_a - cum_s[pl.ds(r, 1), :], 0.0))
            col = jnp.sum(q_a * k_s[pl.ds(r, 1), :] * e, axis=-1, keepdims=True)
            col = jnp.where(row >= s, col, 0.0)
            blk = jnp.where(lane == r, col, blk)
        blocks.append(blk)
    return jnp.concatenate(blocks, axis=0)


def hgrn_fwd(proj, lb_logits, name, after=()):
    after = [m for m in after if m is not None]
    t = proj.shape[0]
    n = t // CHUNK
    cb = _col_blocks()
    dh = HEAD_DIM

    hb = HGRN_HEADS_PER_STEP
    w = hb * dh

    def one_head(qb_ref, fb_ref, ib_ref, lg_ref, o_ref, st_ref, a_ref, state, q_s, k_s, cum_s):
        _hgrn_inputs(qb_ref, fb_ref, lg_ref, q_s, k_s, cum_s)
        st = state[...]
        st_ref[...] = st
        cum = cum_s[...]
        v = ib_ref[...]
        qe = q_s[...] * jnp.exp(cum)
        inter = lax.dot_general(qe, st, _DN["nt"], preferred_element_type=F32)
        a_mat = _hgrn_scores(q_s, k_s, cum_s)
        a_ref[...] = a_mat
        o_ref[...] = inter + jnp.dot(a_mat, v, preferred_element_type=F32)
        last = cum_s[pl.ds(CHUNK - 1, 1), :]
        kd = k_s[...] * jnp.exp(last - cum)
        state[...] = st * jnp.exp(last) + lax.dot_general(v, kd, _DN["tn"], preferred_element_type=F32)

    def body(qb_ref, fb_ref, ib_ref, lg_ref, *rest):
        o_ref, st_ref, a_ref = rest[len(after):len(after) + 3]
        scratch = rest[len(after) + 3:]

        @pl.when(pl.program_id(1) == 0)
        def _():
            for j in range(hb):
                scratch[4 * j][...] = jnp.zeros((dh, dh), F32)

        for j in range(hb):
            cols = (slice(None), pl.ds(j * dh, dh))
            one_head(qb_ref.at[cols], fb_ref.at[cols], ib_ref.at[cols], lg_ref.at[cols], o_ref.at[cols],
                     st_ref.at[j], a_ref.at[j], *scratch[4 * j:4 * j + 4])

    blk = lambda off: pl.BlockSpec((CHUNK, w), lambda h, i: (i, off // hb + h))
    return pl.pallas_call(
        body, name=name, grid=(HEADS // hb, n),
        in_specs=[blk(cb["q_b"]), blk(cb["f_b"]), blk(cb["i_b"]), pl.BlockSpec((2, w), lambda h, i: (0, h))]
        + [pl.BlockSpec(memory_space=pl.ANY)] * len(after),
        out_specs=[pl.BlockSpec((CHUNK, w), lambda h, i: (i, h)),
                   pl.BlockSpec((hb, None, dh, dh), lambda h, i: (h, i, 0, 0)),
                   pl.BlockSpec((hb, None, CHUNK, CHUNK), lambda h, i: (h, i, 0, 0))],
        out_shape=[jax.ShapeDtypeStruct((t, WIDTH), F32), jax.ShapeDtypeStruct((HEADS, n, dh, dh), F32),
                   jax.ShapeDtypeStruct((HEADS, n, CHUNK, CHUNK), F32)],
        scratch_shapes=([pltpu.VMEM((dh, dh), F32)] + [pltpu.VMEM((CHUNK, dh), F32)] * 3) * hb,
        compiler_params=_params(("parallel", "arbitrary")),
    )(proj, proj, proj, lb_logits, *after)


def hgrn_bwd(proj, lb_logits, states, scores, do, name):
    t = proj.shape[0]
    n = t // CHUNK
    cb = _col_blocks()
    dh = HEAD_DIM
    nsub = CHUNK // SUB

    hb = HGRN_HEADS_PER_STEP
    w = hb * dh

    def one_head(qb_ref, fb_ref, ib_ref, lg_ref, st_ref, a_ref, do_ref, dqb_ref, dfb_ref, dib_ref, dlb_ref,
                 dstate, q_s, k_s, cum_s, da_s, dq_s, dk_s):
        lb, sig, f = _hgrn_inputs(qb_ref, fb_ref, lg_ref, q_s, k_s, cum_s)
        st = st_ref[...]
        dst = dstate[...]
        cum = cum_s[...]
        q = q_s[...]
        kk = k_s[...]
        v = ib_ref[...]
        dov = do_ref[...]
        e_cum = jnp.exp(cum)
        qe = q * e_cum
        last = cum_s[pl.ds(CHUNK - 1, 1), :]
        e_last = jnp.exp(last)
        e_tail = jnp.exp(last - cum)
        kd = kk * e_tail

        a_mat = a_ref[...]
        tri = _tri(CHUNK, True)
        da_s[...] = lax.dot_general(dov, v, _DN["nt"], preferred_element_type=F32) * tri
        dv = (lax.dot_general(a_mat, dov, _DN["tn"], preferred_element_type=F32)
              + lax.dot_general(kd, dst, _DN["nt"], preferred_element_type=F32))
        dk_state = jnp.dot(v, dst, preferred_element_type=F32) * e_tail
        dq_inter = jnp.dot(dov, st, preferred_element_type=F32) * e_cum
        dstate[...] = dst * e_last + lax.dot_general(dov, qe, _DN["tn"], preferred_element_type=F32)

        lane = lax.broadcasted_iota(jnp.int32, (SUB, CHUNK), 1)
        row = lax.broadcasted_iota(jnp.int32, (SUB, 1), 0)
        dk_s[...] = jnp.zeros_like(dk_s)
        for a in range(nsub):
            rows = pl.ds(SUB * a, SUB)
            ca = _boundary(cum_s, a)
            cum_a = cum_s[rows, :]
            q_a = q_s[rows, :]
            ea = jnp.exp(cum_a - ca)
            eb = jnp.exp(jnp.minimum(ca - cum, 0.0))
            da_a = da_s[rows, :]
            da_off = jnp.where(lane < SUB * a, da_a, 0.0)
            dq_a = ea * jnp.dot(da_off, kk * eb, preferred_element_type=F32)
            dk_s[...] += eb * lax.dot_general(da_off, q_a * ea, _DN["tn"], preferred_element_type=F32)
            dk_rows = jnp.zeros((SUB, dh), F32)
            for s in range(SUB):
                r = SUB * a + s
                e = jnp.exp(jnp.minimum(cum_a - cum_s[pl.ds(r, 1), :], 0.0))
                dcol = jnp.sum(jnp.where(lane == r, da_a, 0.0), axis=-1, keepdims=True)
                dcol = jnp.where(row >= s, dcol, 0.0)
                w = dcol * e
                dq_a = dq_a + w * k_s[pl.ds(r, 1), :]
                dk_rows = jnp.where(row == s, jnp.sum(w * q_a, axis=0, keepdims=True), dk_rows)
            dq_s[rows, :] = dq_a
            dk_s[rows, :] += dk_rows

        dq = dq_inter + dq_s[...]
        dk = dk_s[...] + dk_state
        d_last = (jnp.sum(dst * st, axis=0, keepdims=True) * e_last
                  + jnp.sum(kk * dk_state, axis=0, keepdims=True))
        rowc = lax.broadcasted_iota(jnp.int32, (CHUNK, 1), 0)
        dcum = q * dq - kk * dk + jnp.where(rowc == CHUNK - 1, d_last, 0.0)
        dg = _dot_hi(_tri(CHUNK, False), dcum)
        df = dg / f - dk
        dqb_ref[...] = (dq * _silu_grad(qb_ref[...])).astype(BF16)
        dfb_ref[...] = (df * (1.0 - lb) * sig * (1.0 - sig)).astype(BF16)
        dib_ref[...] = dv.astype(BF16)
        dlb_ref[...] += jnp.sum(df * (1.0 - sig), axis=0, keepdims=True)

    def body(qb_ref, fb_ref, ib_ref, lg_ref, st_ref, a_ref, do_ref, dqb_ref, dfb_ref, dib_ref, dlb_ref, *scratch):
        @pl.when(pl.program_id(1) == 0)
        def _():
            for j in range(hb):
                scratch[7 * j][...] = jnp.zeros((dh, dh), F32)
            dlb_ref[...] = jnp.zeros_like(dlb_ref)

        for j in range(hb):
            cols = (slice(None), pl.ds(j * dh, dh))
            one_head(qb_ref.at[cols], fb_ref.at[cols], ib_ref.at[cols], lg_ref.at[cols], st_ref.at[j], a_ref.at[j],
                     do_ref.at[cols], dqb_ref.at[cols], dfb_ref.at[cols], dib_ref.at[cols], dlb_ref.at[cols],
                     *scratch[7 * j:7 * j + 7])

    blk = lambda off: pl.BlockSpec((CHUNK, w), lambda h, i: (n - 1 - i, off // hb + h))
    out_blk = pl.BlockSpec((CHUNK, w), lambda h, i: (n - 1 - i, h))
    return pl.pallas_call(
        body, name=name, grid=(HEADS // hb, n),
        in_specs=[blk(cb["q_b"]), blk(cb["f_b"]), blk(cb["i_b"]), pl.BlockSpec((2, w), lambda h, i: (0, h)),
                  pl.BlockSpec((hb, None, dh, dh), lambda h, i: (h, n - 1 - i, 0, 0)),
                  pl.BlockSpec((hb, None, CHUNK, CHUNK), lambda h, i: (h, n - 1 - i, 0, 0)), out_blk],
        out_specs=[out_blk, out_blk, out_blk, pl.BlockSpec((1, w), lambda h, i: (0, h))],
        out_shape=[jax.ShapeDtypeStruct((t, WIDTH), BF16)] * 3 + [jax.ShapeDtypeStruct((1, WIDTH), F32)],
        scratch_shapes=([pltpu.VMEM((dh, dh), F32)] + [pltpu.VMEM((CHUNK, dh), F32)] * 3
                        + [pltpu.VMEM((CHUNK, CHUNK), F32)] + [pltpu.VMEM((CHUNK, dh), F32)] * 2) * hb,
        compiler_params=_params(("parallel", "arbitrary")),
    )(proj, proj, proj, lb_logits, states, scores, do)


def lb_bwd(dlb, lb_logits, name):
    def body(dlb_ref, lg_ref, o_ref):
        p0 = _lower_bound(lg_ref)
        d0 = dlb_ref[...] * p0 * (1.0 - p0)
        o_ref[0:1, :] = d0
        o_ref[1:2, :] = -d0

    return pl.pallas_call(body, name=name, out_shape=jax.ShapeDtypeStruct(lb_logits.shape, F32))(dlb, lb_logits)


def gnorm_fwd(o_raw, proj, norm_g, name, after=()):
    after = [m for m in after if m is not None]
    t = o_raw.shape[0]
    tr = _row_tile(t)
    cb = _col_blocks()
    dh = HEAD_DIM

    gcol = cb["g_b"] * LANES // WIDTH

    def body(o_ref, gb_ref, g_ref, *rest):
        for h in range(HEADS):
            cols = (slice(None), pl.ds(h * dh, dh))
            x = o_ref[cols]
            rest[-1][cols] = (x * _rstd(x) * g_ref[...] * _silu(gb_ref[cols])).astype(BF16)

    return pl.pallas_call(
        body, name=name, grid=(t // tr,),
        in_specs=[pl.BlockSpec((tr, WIDTH), lambda i: (i, 0)), pl.BlockSpec((tr, WIDTH), lambda i: (i, gcol)),
                  pl.BlockSpec((1, dh), lambda i: (0, 0))] + [pl.BlockSpec(memory_space=pl.ANY)] * len(after),
        out_specs=pl.BlockSpec((tr, WIDTH), lambda i: (i, 0)),
        out_shape=jax.ShapeDtypeStruct((t, WIDTH), BF16), compiler_params=_params(("parallel",)),
    )(o_raw, proj, norm_g, *after)


def gnorm_bwd(dy_ab, o_raw, proj, norm_g, name):
    t = o_raw.shape[0]
    tr = _row_tile(t)
    cb = _col_blocks()
    dh = HEAD_DIM
    gcol = cb["g_b"] * LANES // WIDTH

    def body(dy_ref, o_ref, gb_ref, g_ref, do_ref, dgb_ref, dg_ref):
        @pl.when(pl.program_id(0) == 0)
        def _():
            dg_ref[...] = jnp.zeros_like(dg_ref)

        g = g_ref[...]
        acc = jnp.zeros_like(g)
        for h in range(HEADS):
            cols = (slice(None), pl.ds(h * dh, dh))
            x = o_ref[cols]
            gb = gb_ref[cols]
            dyv = dy_ref[cols]
            dx, dg = _rms_bwd(x, g, dyv * _silu(gb))
            do_ref[cols] = dx
            dgb_ref[cols] = (dyv * (x * _rstd(x) * g) * _silu_grad(gb)).astype(BF16)
            acc = acc + dg
        dg_ref[...] += acc

    rows = pl.BlockSpec((tr, WIDTH), lambda i: (i, 0))
    vec = pl.BlockSpec((1, dh), lambda i: (0, 0))
    return pl.pallas_call(
        body, name=name, grid=(t // tr,),
        in_specs=[pl.BlockSpec((None, tr, WIDTH), lambda i: (1, i, 0)), rows,
                  pl.BlockSpec((tr, WIDTH), lambda i: (i, gcol)), vec],
        out_specs=[rows, rows, vec],
        out_shape=[jax.ShapeDtypeStruct((t, WIDTH), F32), jax.ShapeDtypeStruct((t, WIDTH), BF16),
                   jax.ShapeDtypeStruct((1, dh), F32)],
        compiler_params=_params(("arbitrary",)),
    )(dy_ab, o_raw, proj, norm_g)


def merge_fwd(proj, y, name):
    _, t, d = y.shape
    tr = _row_tile(t)
    tc = _pick(d, (1024, 512, 256, 128))
    cb = _col_blocks()
    ga, gb = cb["gate_a"] * LANES // tc, cb["gate_b"] * LANES // tc

    def body(ga_ref, gb_ref, y_ref, o_ref):
        o_ref[...] = (_sigmoid(ga_ref[...]) * y_ref[0] + _sigmoid(gb_ref[...]) * y_ref[1]).astype(BF16)

    return pl.pallas_call(
        body, name=name, grid=(t // tr, d // tc),
        in_specs=[pl.BlockSpec((tr, tc), lambda i, j: (i, ga + j)), pl.BlockSpec((tr, tc), lambda i, j: (i, gb + j)),
                  pl.BlockSpec((2, tr, tc), lambda i, j: (0, i, j))],
        out_specs=pl.BlockSpec((tr, tc), lambda i, j: (i, j)),
        out_shape=jax.ShapeDtypeStruct((t, d), BF16), compiler_params=_params(("parallel", "parallel")),
    )(proj, proj, y)


def merge_bwd(dm, proj, y, name):
    _, t, d = y.shape
    tr = _row_tile(t)
    tc = _pick(d, (1024, 512, 256, 128))
    cb = _col_blocks()
    ga, gb = cb["gate_a"] * LANES // tc, cb["gate_b"] * LANES // tc

    def body(dm_ref, ga_ref, gb_ref, y_ref, dg_ref, dy_ref):
        dmv = dm_ref[...]
        for idx, g_ref in enumerate((ga_ref, gb_ref)):
            s = _sigmoid(g_ref[...])
            dg_ref[idx] = (dmv * y_ref[idx] * s * (1.0 - s)).astype(BF16)
            dy_ref[idx] = (dmv * s).astype(BF16)

    pair = pl.BlockSpec((2, tr, tc), lambda i, j: (0, i, j))
    return pl.pallas_call(
        body, name=name, grid=(t // tr, d // tc),
        in_specs=[pl.BlockSpec((tr, tc), lambda i, j: (i, j)), pl.BlockSpec((tr, tc), lambda i, j: (i, ga + j)),
                  pl.BlockSpec((tr, tc), lambda i, j: (i, gb + j)), pair],
        out_specs=[pair, pair],
        out_shape=[jax.ShapeDtypeStruct((2, t, d), BF16)] * 2, compiler_params=_params(("parallel", "parallel")),
    )(dm, proj, proj, y)


def ple_tail(h, a, b, g, target, name):
    t, d = h.shape
    tr = _row_tile(t)

    def body(h_ref, a_ref, b_ref, g_ref, t_ref, loss_ref, dh_ref, da_ref, db_ref, dg_ref):
        @pl.when(pl.program_id(0) == 0)
        def _():
            loss_ref[...] = jnp.zeros_like(loss_ref)
            dg_ref[...] = jnp.zeros_like(dg_ref)

        s = _sigmoid(a_ref[...])
        bv = b_ref[...]
        z = s * bv
        gv = g_ref[...]
        err = h_ref[...] + z * _rstd(z) * gv - t_ref[...]
        loss_ref[...] += 0.5 * jnp.sum(jnp.sum(err * err, axis=-1, keepdims=True), axis=0, keepdims=True) / d
        dh = err / d
        dh_ref[...] = dh
        dz, dg = _rms_bwd(z, gv, dh)
        da_ref[...] = (dz * bv * s * (1.0 - s)).astype(BF16)
        db_ref[...] = (dz * s).astype(BF16)
        dg_ref[...] += dg

    row = pl.BlockSpec((tr, d), lambda i: (i, 0))
    vec = pl.BlockSpec((1, d), lambda i: (0, 0))
    return pl.pallas_call(
        body, name=name, grid=(t // tr,), in_specs=[row, row, row, vec, row],
        out_specs=[pl.BlockSpec((1, 1), lambda i: (0, 0)), row, row, row, vec],
        out_shape=[jax.ShapeDtypeStruct((1, 1), F32), jax.ShapeDtypeStruct((t, d), F32),
                   jax.ShapeDtypeStruct((t, d), BF16), jax.ShapeDtypeStruct((t, d), BF16),
                   jax.ShapeDtypeStruct((1, d), F32)],
        compiler_params=_params(("arbitrary",)),
    )(h, a, b, g, target)


def _ffn_fwd(h, u, post_g, next_g, get_w, idx, tag):
    gu = None
    for i, key in enumerate(GATE_UP_KEYS["gu" + idx]):
        w = get_w(key, h if gu is None else gu)
        gu = mm_nn_col(u, w, F32, "%s_gate_up_%d" % (tag, i), slot0=i, total=2, into=gu)
    act = swiglu_act(gu, tag + "_act")
    y = mm_nn_2d(act, get_w("down" + idx, gu), F32, tag + "_down")
    out, u_next = resid_norm(h, y, post_g, MACARON_SCALE, next_g, tag + "_out")
    return out, u_next, (h, u, gu, act, y)


def _ffn_bwd(dh, saved, pre_g, post_g, get_w, emit, advance, idx, tag):
    h, u, gu, act, y = saved
    dy, d_post = post_bwd(dh, y, post_g, MACARON_SCALE, tag + "_post_bwd")
    m1 = emit("down" + idx, mm_tn_2d(act, dy, F32, tag + "_dw_down"))
    dact = mm_nt_2d(dy, get_w("down" + idx), F32, tag + "_dact", after=[m1])
    m2 = advance(dact)
    dgu = swiglu_bwd(dact, gu, tag + "_act_bwd")
    m3 = emit("gu" + idx, mm_tn_col(u, dgu, N_CHIPS, F32, tag + "_dw_gate_up"))
    du = None
    for i, key in enumerate(GATE_UP_KEYS["gu" + idx]):
        du = mm_nt_col(dgu, get_w(key), F32, "%s_du_%d" % (tag, i), after=[m2, m3] if du is None else (),
                       slot0=i, init=du)
    m4 = advance(du)
    dh_in, d_pre = pre_bwd(dh, h, pre_g, [du], tag + "_pre_bwd", after=[m4])
    return dh_in, d_pre, d_post


def _heads_col(a):
    t = a.shape[0]
    at = a[:, :HEADS].T
    return at.reshape(HEADS, t, 1), at.reshape(HEADS, 1, t)


def layer_step(x, p, target, gains, fox_bias, lb_logits, norm_g, get_w, emit, advance, early):
    t = x.shape[0]
    u1 = norm_in(x, gains["ffn1_pre"], "ffn1_norm")
    h1, u2, s1 = _ffn_fwd(x, u1, gains["ffn1_post"], gains["mix_pre"], get_w, "1", "ffn1")

    proj = mm_nt_2d(u2, get_w("in_main", h1), F32, "mix_in")
    fa = mm_nt_2d(u2, get_w("in_fa"), F32, "mix_in_fa")
    mark = early("proj", proj)
    c = fox_prep(fa, fox_bias, "fox_prep")
    c_col, c_row = _heads_col(c)
    o_a = fox_fwd(proj, c_col, c_row, "fox_fwd")
    o_raw, states, scores = hgrn_fwd(proj, lb_logits, "hgrn_fwd", after=[mark])
    mark = early("gu2", o_raw)
    o_b = gnorm_fwd(o_raw, proj, norm_g, "hgrn_norm", after=[mark])
    o_ab = jnp.stack([o_a, o_b])
    y_ab = _mm_branches(o_ab, get_w("proj", proj), "mix_proj")
    merged = merge_fwd(proj, y_ab, "mix_merge")
    mo = mm_nn_2d(merged, get_w("out"), F32, "mix_out")
    h2, u3 = resid_norm(h1, mo, gains["mix_post"], 1.0, gains["ffn2_pre"], "mix_resid")

    h3, u4, s3 = _ffn_fwd(h2, u3, gains["ffn2_post"], gains["ple_pre"], get_w, "2", "ffn2")

    a4 = mm_nn_2d(u4, get_w("ple_gate"), F32, "ple_gate")
    b4 = mm_nn_col(p, get_w("ple_proj"), F32, "ple_proj")[0]
    loss, dh4, da4, db4, d_ple_post = ple_tail(h3, a4, b4, gains["ple_post"], target, "ple_tail")

    marks = [emit("ple_gate", mm_tn_2d(u4, da4, F32, "ple_dw_gate")),
             emit("ple_proj", mm_tn_col(p, db4[None], N_CHIPS, F32, "ple_dw_proj"))]
    du4 = mm_nt_2d(da4, get_w("ple_gate"), F32, "ple_du", after=marks)
    dh3, d_ple_pre = pre_bwd(dh4, h3, gains["ple_pre"], [du4], "ple_pre_bwd", after=[advance(du4)])

    dh2, d_f2_pre, d_f2_post = _ffn_bwd(dh3, s3, gains["ffn2_pre"], gains["ffn2_post"], get_w, emit, advance,
                                        "2", "ffn2")

    dmo, d_mix_post = post_bwd(dh2, mo, gains["mix_post"], 1.0, "mix_post_bwd")
    marks = [emit("out", mm_tn_2d(merged, dmo, F32, "mix_dw_out"))]
    dmerged = mm_nt_2d(dmo, get_w("out"), F32, "mix_dmerged", after=marks)
    dgate, dy_ab = merge_bwd(dmerged, proj, y_ab, "mix_merge_bwd")
    marks = [advance(dmerged), emit("proj", _mm_branches_dw(o_ab, dy_ab, "mix_dw_proj"))]
    do_ab = _mm_branches_bwd(dy_ab, get_w("proj"), "mix_do")
    do_raw, dg_b, d_norm_g = gnorm_bwd(do_ab, o_raw, proj, norm_g, "hgrn_norm_bwd")
    dq_b, df_b, di_b, dlb = hgrn_bwd(proj, lb_logits, states, scores, do_raw, "hgrn_bwd")
    d_lb_logits = lb_bwd(dlb, lb_logits, "lb_bwd")
    dq_a, dk_a, dv_a, dc_col, dc_row = fox_bwd(proj, c_col, c_row, do_ab, "fox_bwd")
    dc = (dc_col.reshape(HEADS, t) + dc_row.reshape(HEADS, t)).T
    dc = jnp.pad(dc, ((0, 0), (0, LANES - HEADS)))
    dfa, d_fox_bias = fox_post_bwd(dc, fa, fox_bias, "fox_post_bwd")
    dproj = jnp.concatenate([dq_a, dk_a.astype(BF16), dv_a.astype(BF16), dq_b, df_b, di_b, dg_b,
                             dgate[0], dgate[1]], axis=1)
    marks.append(emit("in_main", mm_tn_2d(dproj, u2, F32, "mix_dw_in")))
    marks.append(emit("in_fa", mm_tn_2d(dfa, u2, F32, "mix_dw_in_fa")))
    du2a = mm_nn_2d(dproj, get_w("in_main"), F32, "mix_du", after=marks)
    du2b = mm_nn_2d(dfa, get_w("in_fa"), F32, "mix_du_fa")
    dh1, d_mix_pre = pre_bwd(dh2, h1, gains["mix_pre"], [du2a, du2b], "mix_pre_bwd", after=[advance(du2a)])

    dx, d_f1_pre, d_f1_post = _ffn_bwd(dh1, s1, gains["ffn1_pre"], gains["ffn1_post"], get_w, emit, advance,
                                       "1", "ffn1")

    small = dict(ffn1_pre=d_f1_pre, ffn1_post=d_f1_post, mix_pre=d_mix_pre, mix_post=d_mix_post,
                 ffn2_pre=d_f2_pre, ffn2_post=d_f2_post, ple_pre=d_ple_pre, ple_post=d_ple_post,
                 fox_bias=d_fox_bias, lb_logits=d_lb_logits, norm_g=d_norm_g)
    return loss, dx, small


def _mm_branches(o_ab, w_proj, name):
    g, t, kk = o_ab.shape
    _, jn, _, ns = w_proj.shape
    tm = _pick(t, (512, 256, 128))
    return _mm(o_ab, w_proj, mode="nn", grid=(t // tm, g * jn, 1),
               a_spec=pl.BlockSpec((None, tm, kk), lambda i, j, k: (j // jn, i, 0)),
               b_spec=pl.BlockSpec((None, None, kk, ns), lambda i, j, k: (j // jn, j % jn, 0, 0)),
               o_spec=pl.BlockSpec((None, tm, ns), lambda i, j, k: (j // jn, i, j % jn)),
               out_shape=jax.ShapeDtypeStruct((g, t, jn * ns), F32), acc_shape=(tm, ns), name=name)


def _mm_branches_bwd(dy_ab, w_proj, name):
    g, t, _ = dy_ab.shape
    _, jn, kk, ns = w_proj.shape
    tm = _pick(t, (512, 256, 128))
    return _mm(dy_ab, w_proj, mode="nt", grid=(t // tm, g, jn),
               a_spec=pl.BlockSpec((None, tm, ns), lambda i, j, k: (j, i, k)),
               b_spec=pl.BlockSpec((None, None, kk, ns), lambda i, j, k: (j, k, 0, 0)),
               o_spec=pl.BlockSpec((None, tm, kk), lambda i, j, k: (j, i, 0)),
               out_shape=jax.ShapeDtypeStruct((g, t, kk), F32), acc_shape=(tm, kk), name=name)


def _mm_branches_dw(o_ab, dy_ab, name):
    g, t, kk = o_ab.shape
    d = dy_ab.shape[2]
    jn = N_CHIPS
    ns = d // jn
    return _mm(o_ab, dy_ab, mode="tn", grid=(1, g * jn, 1),
               a_spec=pl.BlockSpec((None, t, kk), lambda i, j, k: (j // jn, 0, 0)),
               b_spec=pl.BlockSpec((None, t, ns), lambda i, j, k: (j // jn, 0, j % jn)),
               o_spec=pl.BlockSpec((None, None, kk, ns), lambda i, j, k: (j // jn, j % jn, 0, 0)),
               out_shape=jax.ShapeDtypeStruct((g, jn, kk, ns), F32), acc_shape=(kk, ns), name=name)


HBM_SPEC = pl.BlockSpec(memory_space=pltpu.HBM)
SEM_SPEC = pl.BlockSpec(memory_space=pltpu.SEMAPHORE)
ANY_SPEC = pl.BlockSpec(memory_space=pl.ANY)
EFFECT = pltpu.SideEffectType.DATAFLOW_SIDE_EFFECTING


def _in_hbm(a):
    return pltpu.with_memory_space_constraint(a, pltpu.HBM)


def _place():
    x, y, c = lax.axis_index("x"), lax.axis_index("y"), lax.axis_index("c")
    chips = [(1 - x, y), (x, 1 - y), (1 - x, 1 - y)]
    return x, y, c, chips


def _half(shape, which, axis):
    n = shape[-2 + axis] // 2
    cut = pl.ds(which * n, n)
    return (cut, slice(None)) if axis == 0 else (slice(None), cut)


def _half_shape(shape, axis):
    s = list(shape)
    s[len(s) - 2 + axis] //= 2
    return tuple(s)


def _remote(src, dst, send_sems, recv_sems, k, to):
    return pltpu.make_async_remote_copy(src_ref=src, dst_ref=dst, send_sem=send_sems.at[k], recv_sem=recv_sems.at[k],
                                        device_id=to, device_id_type=MESH)


def split_start(name, srcs, lands, counts, copies):
    ns, nl, nset = len(srcs), len(lands), len(counts)

    def body(*refs):
        src_refs, land_refs = refs[:ns], refs[ns:ns + nl]
        sems = refs[ns + nl:ns + nl + 2 * nset]
        for s, plan in enumerate(copies(src_refs, land_refs)):
            for k, (src, dst, to) in enumerate(plan):
                _remote(src, dst, sems[2 * s], sems[2 * s + 1], k, to).start()
        refs[-1][...] = jnp.zeros_like(refs[-1])

    out_shape = []
    for n in counts:
        out_shape += [pltpu.SemaphoreType.DMA((n,)), pltpu.SemaphoreType.DMA((n,))]
    out_shape += [pltpu.HBM(a.shape, a.dtype) for a in list(srcs) + list(lands)]
    out_shape.append(jax.ShapeDtypeStruct((8, LANES), F32))
    res = pl.pallas_call(
        body, name=name, out_shape=tuple(out_shape), in_specs=[HBM_SPEC] * (ns + nl),
        out_specs=tuple([SEM_SPEC] * (2 * nset) + [HBM_SPEC] * (ns + nl) + [pl.BlockSpec(memory_space=pltpu.VMEM)]),
        input_output_aliases={i: 2 * nset + i for i in range(ns + nl)},
        compiler_params=pltpu.CompilerParams(has_side_effects=EFFECT),
    )(*[_in_hbm(a) for a in list(srcs) + list(lands)])
    sems = [(res[2 * s], res[2 * s + 1]) for s in range(nset)]
    return sems, list(res[2 * nset:2 * nset + ns]), list(res[2 * nset + ns:-1]), res[-1]


def split_wait(name, srcs, lands, sems, afters, copies):
    afters = [a for a in afters if a is not None]
    ns, nl, na = len(srcs), len(lands), len(afters)

    def body(*refs):
        src_refs, land_refs = refs[:ns], refs[ns:ns + nl]
        send_sems, recv_sems = refs[ns + nl:ns + nl + 2]
        for k, (src, dst, to) in enumerate(copies(src_refs, land_refs)):
            cp = _remote(src, dst, send_sems, recv_sems, k, to)
            cp.wait_send()
            cp.wait_recv()

    res = pl.pallas_call(
        body, name=name, out_shape=tuple(pltpu.HBM(a.shape, a.dtype) for a in list(srcs) + list(lands)),
        in_specs=[HBM_SPEC] * (ns + nl) + [SEM_SPEC, SEM_SPEC] + [ANY_SPEC] * na,
        out_specs=tuple([HBM_SPEC] * (ns + nl)), input_output_aliases={i: i for i in range(ns + nl)},
        compiler_params=pltpu.CompilerParams(has_side_effects=EFFECT),
    )(*srcs, *lands, sems[0], sems[1], *afters)
    return list(res[:ns]), list(res[ns:])


def _gather_plan(blocks):
    def copies(src_refs, land_refs):
        x, y, c, chips = _place()
        j_me = 2 * x + y
        plan = []
        for si, li, g, axis in blocks:
            src, land = src_refs[si], land_refs[li].at[g]
            mine = _half(src.shape, c, axis)
            for px, py in chips:
                plan.append((src.at[mine], land.at[(j_me,) + mine], (px, py, c)))
            plan.append((src, land.at[j_me], (x, y, 1 - c)))
        return plan
    return copies


def _gather_arrivals(blocks):
    def copies(src_refs, land_refs):
        x, y, c, chips = _place()
        j_me = 2 * x + y
        plan = []
        for si, li, g, axis in blocks:
            src, land = src_refs[si], land_refs[li].at[g]
            mine = _half(src.shape, c, axis)
            for px, py in chips:
                plan.append((src.at[mine], land.at[(2 * px + py,) + mine], (px, py, c)))
            plan.append((src, land.at[j_me], (x, y, 1 - c)))
        return plan
    return copies


def _pass_plan(blocks, arrivals):
    def copies(src_refs, land_refs):
        x, y, c, chips = _place()
        plan = []
        for li, g, axis in blocks:
            land = src_refs[li].at[g]
            half = _half(land.shape[1:], (1 - c) if arrivals else c, axis)
            for px, py in chips:
                part = land.at[(2 * px + py,) + half]
                plan.append((part, part, (x, y, 1 - c)))
        return plan
    return copies


def gather_pass(name, lands, blocks):
    n = len(lands)

    def body(*refs):
        outs = refs[n:2 * n]
        send_sems, recv_sems = refs[2 * n:]
        x, y, c, chips = _place()
        sent = []
        for i, (li, g, axis) in enumerate(blocks):
            land = outs[li].at[g]
            mine = _half(land.shape[1:], c, axis)
            for k, (px, py) in enumerate(chips):
                part = land.at[(2 * px + py,) + mine]
                cp = _remote(part, part, send_sems, recv_sems, 3 * i + k, (x, y, 1 - c))
                cp.start()
                sent.append(cp)
        for i, (li, g, axis) in enumerate(blocks):
            land = outs[li].at[g]
            other = _half(land.shape[1:], 1 - c, axis)
            for k, (px, py) in enumerate(chips):
                part = land.at[(2 * px + py,) + other]
                _remote(part, part, send_sems, recv_sems, 3 * i + k, (x, y, 1 - c)).wait_recv()
        for cp in sent:
            cp.wait_send()

    m = 3 * len(blocks)
    return pl.pallas_call(
        body, name=name, in_specs=[ANY_SPEC] * n, out_specs=[ANY_SPEC] * n,
        out_shape=[jax.ShapeDtypeStruct(a.shape, a.dtype) for a in lands],
        input_output_aliases={i: i for i in range(n)},
        scratch_shapes=[pltpu.SemaphoreType.DMA((m,)), pltpu.SemaphoreType.DMA((m,))],
    )(*lands)


def _pair_plan(axes):
    def copies(src_refs, land_refs):
        x, y, c, _ = _place()
        return [(src_refs[i].at[(slice(None), slice(None)) + _half(src_refs[i].shape, 1 - c, a)], land_refs[i],
                 (x, y, 1 - c)) for i, a in enumerate(axes)]
    return copies


def _scatter_plan(n):
    def copies(src_refs, land_refs):
        x, y, c, chips = _place()
        return [(src_refs[i].at[:, 2 * px + py], land_refs[i].at[k], (px, py, c))
                for i in range(n) for k, (px, py) in enumerate(chips)]
    return copies


def _broadcast_plan(axes, arrivals):
    def copies(src_refs, land_refs):
        x, y, c, _ = _place()
        plan = []
        for i, a in enumerate(axes):
            part = src_refs[i].at[(slice(None),) + _half(src_refs[i].shape, (1 - c) if arrivals else c, a)]
            plan.append((part, part, (x, y, 1 - c)))
        return plan
    return copies


N_DEV = 8
SLAB_ROWS = 16


def allreduce_small(slab, after):
    def body(x_ref, after_ref, o_ref, land, send_sems, recv_sems):
        x, y, c, _ = _place()
        me = 4 * x + 2 * y + c
        land[me] = x_ref[...]
        copies = []
        for d in range(1, N_DEV):
            to = (me + d) % N_DEV
            cp = pltpu.make_async_remote_copy(
                src_ref=x_ref, dst_ref=land.at[me], send_sem=send_sems.at[d - 1], recv_sem=recv_sems.at[me],
                device_id=(to // 4, (to // 2) % 2, to % 2), device_id_type=MESH)
            cp.start()
            copies.append(cp)
        for d in range(1, N_DEV):
            frm = (me + d) % N_DEV
            pltpu.make_async_remote_copy(
                src_ref=x_ref, dst_ref=land.at[frm], send_sem=send_sems.at[d - 1], recv_sem=recv_sems.at[frm],
                device_id=(frm // 4, (frm // 2) % 2, frm % 2), device_id_type=MESH).wait_recv()
        for cp in copies:
            cp.wait_send()
        acc = land[0]
        for s in range(1, N_DEV):
            acc = acc + land[s]
        o_ref[...] = acc

    vm = pl.BlockSpec(memory_space=pltpu.VMEM)
    return pl.pallas_call(
        body, name="allreduce_small", in_specs=[vm, ANY_SPEC], out_specs=vm,
        out_shape=jax.ShapeDtypeStruct(slab.shape, F32),
        scratch_shapes=[pltpu.VMEM((N_DEV,) + slab.shape, F32), pltpu.SemaphoreType.DMA((N_DEV - 1,)),
                        pltpu.SemaphoreType.DMA((N_DEV,))],
    )(slab, after)


BLOCK_BYTES = 3 * 1024 * 1024


def _tiles_2d(r, c, budget=BLOCK_BYTES):
    if r % 8 == 0:
        tc = c if c % LANES else _pick(c, (2048, 1408, 1024, 512, 256, 128))
        tr = 8
        for cand in (512, 256, 128, 64, 32, 16, 8):
            if r % cand == 0 and cand * tc * 4 <= budget:
                tr = cand
                break
        if tr >= 64 or c % LANES or r * LANES * 4 > budget:
            return tr, tc
    tc = LANES
    for cand in (1024, 512, 256, 128):
        if c % cand == 0 and r * cand * 4 <= budget:
            tc = cand
            break
    return r, tc


def _grid_spec(grid, in_specs, out_specs):
    return pltpu.PrefetchScalarGridSpec(num_scalar_prefetch=1, grid=grid, in_specs=in_specs, out_specs=out_specs)


def _own(axis, nr, nc):
    if axis == 0:
        return lambda i, j, where: (where[1] * nr + i, j)
    return lambda i, j, where: (i, where[1] * nc + j)


def pair_add(where, grad, recv, axis, name):
    g, jn, hr, hc = recv.shape
    tr, tc = _tiles_2d(hr, hc)
    nr, nc = hr // tr, hc // tc
    own = _own(axis, nr, nc)
    others = jn - 1

    def body(where_ref, a_ref, b_ref, o_ref):
        o_ref[...] = (a_ref[...] + b_ref[...]).astype(BF16)

    def block(a, where):
        return a // others, (where[0] + 1 + a % others) % jn

    blk = pl.BlockSpec((None, None, tr, tc), lambda a, i, j, where: block(a, where) + (i, j))
    mine = pl.BlockSpec((None, None, tr, tc), lambda a, i, j, where: block(a, where) + own(i, j, where))
    return pl.pallas_call(
        body, name=name, grid_spec=_grid_spec((g * others, nr, nc), [mine, blk], blk),
        out_shape=jax.ShapeDtypeStruct(recv.shape, BF16),
        compiler_params=_params(("parallel", "parallel", "parallel")),
    )(where, grad, recv)


def chip_add(where, grad, pair, recv, axis, name):
    g, jn, hr, hc = pair.shape
    tr, tc = _tiles_2d(hr, hc)
    nr, nc = hr // tr, hc // tc
    own = _own(axis, nr, nc)
    full = (g, 2 * hr, hc) if axis == 0 else (g, hr, 2 * hc)

    def body(where_ref, a_ref, p_ref, b_ref, o_ref):
        s = a_ref[...] + p_ref[...]
        for k in range(3):
            s = s + b_ref[k].astype(F32)
        o_ref[...] = s

    return pl.pallas_call(
        body, name=name,
        grid_spec=_grid_spec((g, nr, nc),
                             [pl.BlockSpec((None, None, tr, tc), lambda a, i, j, where: (a, where[0]) + own(i, j, where)),
                              pl.BlockSpec((None, None, tr, tc), lambda a, i, j, where: (a, where[0], i, j)),
                              pl.BlockSpec((3, None, tr, tc), lambda a, i, j, where: (0, a, i, j))],
                             pl.BlockSpec((None, tr, tc), lambda a, i, j, where: (a,) + own(i, j, where))),
        out_shape=jax.ShapeDtypeStruct(full, F32), compiler_params=_params(("parallel", "parallel", "parallel")),
    )(where, grad, pair, recv)


def _adam_math(w, g, m, v):
    m2 = ADAM_B1 * m + (1.0 - ADAM_B1) * g
    v2 = ADAM_B2 * v + (1.0 - ADAM_B2) * (g * g)
    m_hat = m2 / (1.0 - ADAM_B1 ** ADAM_STEP)
    v_hat = v2 / (1.0 - ADAM_B2 ** ADAM_STEP)
    delta = -ADAM_LR * (m_hat / (jnp.sqrt(v_hat) + ADAM_EPS) + ADAM_WD * w)
    return delta, m2, v2


def adamw(grad, idx, w, m, v, name):
    _, r, cc = w.shape
    rg = grad.shape[1]
    tr, tc = _tiles_2d(r, cc, BLOCK_BYTES // 2)
    assert rg == r or tr == r
    gr = tr if rg == r else rg

    def body(g_ref, w_ref, m_ref, v_ref, go_ref, d_ref, mo_ref, vo_ref):
        g = g_ref[pl.ds(0, tr), :]
        delta, m2, v2 = _adam_math(w_ref[...], g, m_ref[...], v_ref[...])
        go_ref[...] = g
        d_ref[...] = delta
        mo_ref[...] = m2
        vo_ref[...] = v2

    blk = pl.BlockSpec((None, tr, tc), lambda i, j: (0, i, j))
    return pl.pallas_call(
        body, name=name, grid=(r // tr, cc // tc),
        in_specs=[pl.BlockSpec((None, gr, tc), lambda i, j: (idx, i, j)), blk, blk, blk], out_specs=[blk] * 4,
        out_shape=[jax.ShapeDtypeStruct(w.shape, F32)] * 4, compiler_params=_params(("parallel", "parallel")),
    )(grad, w, m, v)


def adamw_small(g, w, m, v):
    def body(g_ref, w_ref, m_ref, v_ref, d_ref, mo_ref, vo_ref):
        delta, m2, v2 = _adam_math(w_ref[...], g_ref[...], m_ref[...], v_ref[...])
        d_ref[...] = delta
        mo_ref[...] = m2
        vo_ref[...] = v2

    return pl.pallas_call(body, name="adamw_small", out_shape=[jax.ShapeDtypeStruct(w.shape, F32)] * 3)(g, w, m, v)


GAINS = ("ffn1_pre", "ffn1_post", "mix_pre", "mix_post", "ffn2_pre", "ffn2_post", "ple_pre", "ple_post")
WEIGHTS = ("ffn1_pre_g", "ffn1_post_g", "ffn1_w_gate", "ffn1_w_up", "ffn1_w_down", "mix_pre_g", "mix_post_g",
           "mix_w_in", "fox_f_bias", "hgrn_lb_logits", "hgrn_norm_g", "mix_w_proj_fox", "mix_w_proj_hgrn",
           "mix_w_out", "ffn2_pre_g", "ffn2_post_g", "ffn2_w_gate", "ffn2_w_up", "ffn2_w_down", "ple_pre_g",
           "ple_post_g", "ple_w_gate", "ple_w_proj")
GROUPS = dict(gu1=(("ffn1_w_gate", "ffn1_w_up"), 0), down1=(("ffn1_w_down",), 0), win=(("mix_w_in",), 1),
              proj=(("mix_w_proj_fox", "mix_w_proj_hgrn"), 0), out=(("mix_w_out",), 0),
              gu2=(("ffn2_w_gate", "ffn2_w_up"), 0), down2=(("ffn2_w_down",), 0), ple_gate=(("ple_w_gate",), 0),
              ple_proj=(("ple_w_proj",), 0))
TRANSPOSED = ("mix_w_in",)
ROW_BLOCKS = ("down1", "down2", "out", "ple_gate")
GATHER_SETS = (("gate1",), ("up1",), ("down1",), ("win",), ("proj", "out"), ("gu2", "down2", "ple_gate", "ple_proj"))
GATHER_GROUPS = dict(GROUPS, gate1=(("ffn1_w_gate",), 0), up1=(("ffn1_w_up",), 0))
GATE_UP_KEYS = dict(gu1=("gate1", "up1"), gu2=("gu2",))
REDUCE_SETS = (("ple_gate", "ple_proj", "down2", "gu2"), ("out", "proj", "win"), ("down1",), ("gu1",))


def _pad_row(a, width):
    a = a.reshape(1, -1)
    return jnp.pad(a, ((0, 0), (0, width - a.shape[1])))


def _pack_small(vals):
    d = D_MODEL
    rows = [vals[n + "_g"].reshape(1, d) for n in GAINS]
    rows.append(_pad_row(vals["fox_f_bias"], d))
    lg = vals["hgrn_lb_logits"]
    rows += [_pad_row(lg[0], d), _pad_row(lg[1], d), _pad_row(vals["hgrn_norm_g"], d)]
    slab = jnp.concatenate(rows, axis=0)
    return jnp.pad(slab, ((0, SLAB_ROWS - slab.shape[0]), (0, 0)))


def _unpack_small(slab):
    out = {n + "_g": slab[i:i + 1] for i, n in enumerate(GAINS)}
    out["fox_f_bias"] = slab[8:9, :HEADS]
    out["hgrn_lb_logits"] = slab[9:11, :WIDTH]
    out["hgrn_norm_g"] = slab[11:12, :HEAD_DIM]
    return out


def _split_in(win_t):
    lo = 3 * WIDTH
    main = jnp.concatenate([win_t[:lo], win_t[lo + HEADS:]], axis=0)
    fa = jnp.pad(win_t[lo:lo + HEADS], ((0, LANES - HEADS), (0, 0)))
    return main, fa


def _join_in(main, fa):
    lo = 3 * WIDTH
    return jnp.concatenate([main[:lo], fa[:HEADS], main[lo:]], axis=0)


def _as_block(name, a):
    return jnp.swapaxes(a, 1, 2) if name in TRANSPOSED else a


def _send_block(name, a, mark):
    blk = _as_block(name, a)[0]
    if mark is not None:
        blk = blk + mark[0, 0]
    return blk.astype(BF16)


def kernel(x, p, ffn1_pre_g, ffn1_post_g, ffn1_w_gate, ffn1_w_up, ffn1_w_down, mix_pre_g, mix_post_g, mix_w_in, fox_f_bias, hgrn_lb_logits, hgrn_norm_g, mix_w_proj_fox, mix_w_proj_hgrn, mix_w_out, ffn2_pre_g, ffn2_post_g, ffn2_w_gate, ffn2_w_up, ffn2_w_down, ple_pre_g, ple_post_g, ple_w_gate, ple_w_proj, loss_target, m_ffn1_pre_g, m_ffn1_post_g, m_ffn1_w_gate, m_ffn1_w_up, m_ffn1_w_down, m_mix_pre_g, m_mix_post_g, m_mix_w_in, m_fox_f_bias, m_hgrn_lb_logits, m_hgrn_norm_g, m_mix_w_proj_fox, m_mix_w_proj_hgrn, m_mix_w_out, m_ffn2_pre_g, m_ffn2_post_g, m_ffn2_w_gate, m_ffn2_w_up, m_ffn2_w_down, m_ple_pre_g, m_ple_post_g, m_ple_w_gate, m_ple_w_proj, v_ffn1_pre_g, v_ffn1_post_g, v_ffn1_w_gate, v_ffn1_w_up, v_ffn1_w_down, v_mix_pre_g, v_mix_post_g, v_mix_w_in, v_fox_f_bias, v_hgrn_lb_logits, v_hgrn_norm_g, v_mix_w_proj_fox, v_mix_w_proj_hgrn, v_mix_w_out, v_ffn2_pre_g, v_ffn2_post_g, v_ffn2_w_gate, v_ffn2_w_up, v_ffn2_w_down, v_ple_pre_g, v_ple_post_g, v_ple_w_gate, v_ple_w_proj):
    args = dict(locals())
    wts = {n: args[n] for n in WEIGHTS}
    mom = {n: args["m_" + n] for n in WEIGHTS}
    var = {n: args["v_" + n] for n in WEIGHTS}
    d = D_MODEL
    where = jnp.stack([2 * lax.axis_index("x") + lax.axis_index("y"), lax.axis_index("c")]).astype(jnp.int32)

    def start_sets(name, which, mark):
        srcs, lands, plans = [], [], []
        for si in which:
            blocks = []
            for g in GATHER_SETS[si]:
                names, axis = GATHER_GROUPS[g]
                for pos, n in enumerate(names):
                    blocks.append((len(srcs), len(lands), pos, axis))
                    srcs.append(_send_block(n, wts[n], mark))
                lands.append(lax.empty((len(names), N_CHIPS) + srcs[-1].shape, BF16))
            plans.append(blocks)
        sems, srcs, lands, mark = split_start(name, srcs, lands, [4 * len(b) for b in plans],
                                              lambda sr, lr: [_gather_plan(b)(sr, lr) for b in plans])
        out = {}
        for k, si in enumerate(which):
            s_idx = sorted({b[0] for b in plans[k]})
            l_idx = sorted({b[1] for b in plans[k]})
            local = [(s_idx.index(a), l_idx.index(b), pos, ax) for a, b, pos, ax in plans[k]]
            out[si] = (sems[k], [srcs[i] for i in s_idx], [lands[i] for i in l_idx], local)
        return out, mark

    flying, mark = start_sets("gather_start_0", [0], None)
    rest, all_started = start_sets("gather_start_1", list(range(1, len(GATHER_SETS))), mark)
    flying.update(rest)
    full, passing = {}, {}

    def set_of(key):
        g = "win" if key in ("in_main", "in_fa") else key
        return g, [g in s for s in GATHER_SETS].index(True)

    def arrive(si, after):
        sem, srcs, lands, local = flying.pop(si)
        _, got = split_wait("gather_wait_%d" % si, srcs, lands, sem, [after, all_started], _gather_arrivals(local))
        return got, [(b, pos, ax) for _, b, pos, ax in local]

    def early(key, after):
        g, si = set_of(key)
        if g in full or si not in flying:
            return None
        got, blocks = arrive(si, after)
        sem, got, _, mark = split_start("pass_start_%d" % si, got, [], [3 * len(blocks)],
                                        lambda sr, lr: [_pass_plan(blocks, False)(sr, lr)])
        passing[si] = (sem[0], got, blocks)
        return mark

    def land_set(si, after):
        if si in passing:
            sem, got, blocks = passing.pop(si)
            got, _ = split_wait("pass_wait_%d" % si, got, [], sem, [after], _pass_plan(blocks, True))
        else:
            got, blocks = arrive(si, after)
            got = gather_pass("gather_pass_%d" % si, got, blocks)
        for g, arr in zip(GATHER_SETS[si], got):
            if g == "win":
                full["win"] = arr
                full["in_main"], full["in_fa"] = _split_in(arr.reshape(-1, d))
            else:
                full[g] = arr.reshape(-1, d) if g in ROW_BLOCKS else arr

    def get_w(key, after=None):
        g, si = set_of(key)
        if g not in full:
            land_set(si, after)
        return full[key]

    grads, pairing, started = {}, [], {}
    rows4 = lambda a: a.reshape(1, N_CHIPS, a.shape[0] // N_CHIPS, a.shape[1])

    def emit(key, grad):
        if key in ("in_main", "in_fa"):
            grads[key] = grad
            if "in_main" not in grads or "in_fa" not in grads:
                return None
            key, grad = "win", rows4(_join_in(grads["in_main"], grads["in_fa"]))
        grads[key] = grad if grad.ndim == 4 else rows4(grad)
        for si, s in enumerate(REDUCE_SETS):
            if key in s and all(g in grads for g in s):
                axes = [GROUPS[g][1] for g in s]
                own = [grads[g] for g in s]
                zones = [lax.empty(_half_shape(a.shape, ax), F32) for a, ax in zip(own, axes)]
                plan = _pair_plan(axes)
                sem, own, zones, mark = split_start("pair_start_%d" % si, own, zones, [len(s)],
                                                    lambda sr, lr: [plan(sr, lr)])
                pairing.append((si, sem[0], own, zones, axes, plan))
                return mark
        return None

    def advance(value):
        mark = None
        while pairing:
            si, sem, own, zones, axes, plan = pairing.pop(0)
            s = REDUCE_SETS[si]
            own, recv = split_wait("pair_wait_%d" % si, own, zones, sem, [value], plan)
            parts = [pair_add(where, a, r, ax, "pair_add_" + g) for g, a, r, ax in zip(s, own, recv, axes)]
            zones = [lax.empty((3, q.shape[0]) + q.shape[2:], BF16) for q in parts]
            plan = _scatter_plan(len(s))
            sem, parts, zones, mark = split_start("scatter_start_%d" % si, parts, zones, [3 * len(s)],
                                                  lambda sr, lr: [plan(sr, lr)])
            started[si] = (sem[0], parts, zones, own, recv, axes, plan)
        return mark

    gains = {n: wts[n + "_g"] for n in GAINS}
    loss, dx, small = layer_step(x[0], p[0, 0].astype(BF16), loss_target[0], gains, _pad_row(fox_f_bias, LANES),
                                 hgrn_lb_logits, hgrn_norm_g, get_w, emit, advance, early)

    out_g, out_d, out_m, out_v = {}, {}, {}, {}
    after, crossing = None, []

    def finish(si, sem, halves, axes, mark):
        reduced, _ = split_wait("broadcast_wait_%d" % si, halves, [], sem, [mark], _broadcast_plan(axes, True))
        last = None
        for g, red in zip(REDUCE_SETS[si], reduced):
            for idx, n in enumerate(GROUPS[g][0]):
                res = adamw(red, idx, _as_block(n, wts[n]), _as_block(n, mom[n]), _as_block(n, var[n]), "adamw_" + n)
                out_g[n], out_d[n], out_m[n], out_v[n] = [_as_block(n, r) for r in res]
                last = res[1]
        return last

    for si, s in enumerate(REDUCE_SETS):
        sem, parts, zones, own, recv, axes, plan = started[si]
        _, zones = split_wait("scatter_wait_%d" % si, parts, zones, sem, [dx, after], plan)
        halves = [chip_add(where, a, r, z, ax, "chip_add_" + g) for g, a, r, z, ax in zip(s, own, recv, zones, axes)]
        plan = _broadcast_plan(axes, False)
        sem, halves, _, mark = split_start("broadcast_start_%d" % si, halves, [], [len(s)],
                                           lambda sr, lr: [plan(sr, lr)])
        crossing.append((si, sem[0], halves, axes))
        if len(crossing) > 1:
            after = finish(*crossing.pop(0), mark)
    while crossing:
        after = finish(*crossing.pop(0), after)

    small_named = {n + "_g": small[n] for n in GAINS}
    small_named.update(fox_f_bias=small["fox_bias"][:, :HEADS], hgrn_lb_logits=small["lb_logits"],
                       hgrn_norm_g=small["norm_g"])
    g_small = allreduce_small(_pack_small(small_named), after)
    d_small, m_small, v_small = adamw_small(g_small, _pack_small(wts), _pack_small(mom), _pack_small(var))

    for dst, slab in ((out_g, g_small), (out_d, d_small), (out_m, m_small), (out_v, v_small)):
        dst.update(_unpack_small(slab))

    total = lax.psum(loss[0, 0], ("x", "y", "c"))
    return (total, dx[None], *[out_g[n] for n in WEIGHTS], *[out_d[n] for n in WEIGHTS],
            *[out_m[n] for n in WEIGHTS], *[out_v[n] for n in WEIGHTS])
```

```python
import functools

import jax
import jax.numpy as jnp
from jax import lax
from jax.experimental import pallas as pl
from jax.experimental.pallas import tpu as pltpu

F32 = jnp.float32
BF16 = jnp.bfloat16

D_MODEL = 2048
SEQ = 2048
D_FF = 5632
PLE_DIM = 256
HEADS = 8
HEAD_DIM = 128
WIDTH = HEADS * HEAD_DIM
CHUNK = 64
SUB = 16
HGRN_HEADS_PER_STEP = 2
NORM_EPS = 1e-6
MACARON_SCALE = 0.5
N_CHIPS = 4

ADAM_LR = 0.001
ADAM_B1 = 0.9
ADAM_B2 = 0.999
ADAM_EPS = 1e-08
ADAM_WD = 0.01
ADAM_STEP = 10

LANES = 128
VMEM_LIMIT = 56 * 1024 * 1024
NEG_BIG = -1e30
MESH = pl.DeviceIdType.MESH


def _pick(n, cands):
    for c in cands:
        if c <= n and n % c == 0:
            return c
    return n


def _params(sem, vmem=VMEM_LIMIT):
    return pltpu.CompilerParams(dimension_semantics=sem, vmem_limit_bytes=vmem)


def _sigmoid(x):
    return 1.0 / (1.0 + jnp.exp(-x))


def _silu(x):
    return x * _sigmoid(x)


def _silu_grad(x):
    s = _sigmoid(x)
    return s * (1.0 + x * (1.0 - s))


_DN = {"nn": (((1,), (0,)), ((), ())), "nt": (((1,), (1,)), ((), ())), "tn": (((0,), (0,)), ((), ()))}


def _mm(a, b, *, mode, grid, a_spec, b_spec, o_spec, out_shape, acc_shape, name, after=(), init=None, into=None):
    nk = grid[2]
    dn = _DN[mode]
    after = [m for m in after if m is not None]
    extra = ([init] if init is not None else []) + ([into] if into is not None else []) + after
    n_extra = len(extra)

    def body(a_ref, b_ref, *rest):
        o_ref, acc_ref = rest[n_extra:]
        k = pl.program_id(2)

        @pl.when(k == 0)
        def _():
            acc_ref[...] = jnp.zeros_like(acc_ref) if init is None else rest[0][...].astype(F32)

        acc_ref[...] += lax.dot_general(a_ref[...].astype(BF16), b_ref[...].astype(BF16), dn,
                                        preferred_element_type=F32)

        @pl.when(k == nk - 1)
        def _():
            o_ref[...] = acc_ref[...].astype(o_ref.dtype)

    anywhere = pl.BlockSpec(memory_space=pl.ANY)
    return pl.pallas_call(
        body, name=name, grid=grid,
        in_specs=[a_spec, b_spec] + ([o_spec] if init is not None else []) + [anywhere] * (n_extra - (init is not None)),
        out_specs=o_spec, out_shape=out_shape, scratch_shapes=[pltpu.VMEM(acc_shape, F32)],
        input_output_aliases={} if into is None else {2 + (init is not None): 0},
        compiler_params=_params(("parallel", "parallel", "arbitrary")),
    )(a, b, *extra)


def mm_nn_2d(a, b, out_dtype, name, after=()):
    m, kk = a.shape
    n = b.shape[1]
    tm, tn = _pick(m, (512, 256, 128)), _pick(n, (1024, 512, 256, 128))
    tk = _pick(kk, (5632, 2816, 2048, 1408, 1024, 512, 256, 128))
    return _mm(a, b, mode="nn", grid=(m // tm, n // tn, kk // tk),
               a_spec=pl.BlockSpec((tm, tk), lambda i, j, k: (i, k)),
               b_spec=pl.BlockSpec((tk, tn), lambda i, j, k: (k, j)),
               o_spec=pl.BlockSpec((tm, tn), lambda i, j, k: (i, j)),
               out_shape=jax.ShapeDtypeStruct((m, n), out_dtype), acc_shape=(tm, tn), name=name, after=after)


def mm_nt_2d(a, b, out_dtype, name, after=()):
    m, c = a.shape
    n = b.shape[0]
    tm, tn, tk = _pick(m, (512, 256, 128)), _pick(n, (1408, 1024, 512, 256, 128)), _pick(c, (2048, 1408, 1024, 512, 256, 128))
    return _mm(a, b, mode="nt", grid=(m // tm, n // tn, c // tk),
               a_spec=pl.BlockSpec((tm, tk), lambda i, j, k: (i, k)),
               b_spec=pl.BlockSpec((tn, tk), lambda i, j, k: (j, k)),
               o_spec=pl.BlockSpec((tm, tn), lambda i, j, k: (i, j)),
               out_shape=jax.ShapeDtypeStruct((m, n), out_dtype), acc_shape=(tm, tn), name=name, after=after)


def mm_tn_2d(a, b, out_dtype, name):
    c, m = a.shape
    n = b.shape[1]
    tm, tn, tk = _pick(m, (1408, 1024, 512, 256, 128)), _pick(n, (1024, 512, 256, 128)), _pick(c, (2048, 1024, 512, 256, 128))
    return _mm(a, b, mode="tn", grid=(m // tm, n // tn, c // tk),
               a_spec=pl.BlockSpec((tk, tm), lambda i, j, k: (k, i)),
               b_spec=pl.BlockSpec((tk, tn), lambda i, j, k: (k, j)),
               o_spec=pl.BlockSpec((tm, tn), lambda i, j, k: (i, j)),
               out_shape=jax.ShapeDtypeStruct((m, n), out_dtype), acc_shape=(tm, tn), name=name)


def mm_nn_col(a, w, out_dtype, name, slot0=0, total=None, into=None):
    m, kk = a.shape
    g, jn, _, ns = w.shape
    total = g if total is None else total
    tm, tk = _pick(m, (512, 256, 128)), _pick(kk, (2048, 1024, 512, 256, 128))
    return _mm(a, w, mode="nn", grid=(m // tm, g * jn, kk // tk),
               a_spec=pl.BlockSpec((tm, tk), lambda i, j, k: (i, k)),
               b_spec=pl.BlockSpec((None, None, tk, ns), lambda i, j, k: (j // jn, j % jn, k, 0)),
               o_spec=pl.BlockSpec((None, tm, ns), lambda i, j, k: (slot0 + j // jn, i, j % jn)),
               out_shape=jax.ShapeDtypeStruct((total, m, jn * ns), out_dtype), acc_shape=(tm, ns), name=name,
               into=into)


def mm_nt_col(a, w, out_dtype, name, after=(), slot0=0, init=None):
    _, m, _ = a.shape
    g, jn, kk, ns = w.shape
    kc = 2 if jn % 2 == 0 else 1
    per = jn // kc
    nk = g * per
    tm, tn = _pick(m, (512, 256, 128)), _pick(kk, (1024, 512, 256, 128))
    extra = ([init] if init is not None else []) + [v for v in after if v is not None]

    def body(a_ref, w_ref, *rest):
        o_ref, acc_ref = rest[len(extra):]
        k = pl.program_id(2)
        part = None
        for c in range(kc):
            prod = lax.dot_general(a_ref[:, c * ns:(c + 1) * ns].astype(BF16), w_ref[c].astype(BF16), _DN["nt"],
                                   preferred_element_type=F32)
            part = prod if part is None else part + prod

        @pl.when(k == 0)
        def _():
            acc_ref[...] = part if init is None else rest[0][...].astype(F32) + part

        @pl.when(k > 0)
        def _():
            acc_ref[...] += part

        @pl.when(k == nk - 1)
        def _():
            o_ref[...] = acc_ref[...].astype(o_ref.dtype)

    o_spec = pl.BlockSpec((tm, tn), lambda i, j, k: (i, j))
    anywhere = pl.BlockSpec(memory_space=pl.ANY)
    return pl.pallas_call(
        body, name=name, grid=(m // tm, kk // tn, nk),
        in_specs=[pl.BlockSpec((None, tm, kc * ns), lambda i, j, k: (slot0 + k // per, i, k % per)),
                  pl.BlockSpec((None, kc, tn, ns), lambda i, j, k: (k // per, k % per, j, 0))]
        + ([o_spec] if init is not None else []) + [anywhere] * (len(extra) - (init is not None)),
        out_specs=o_spec, out_shape=jax.ShapeDtypeStruct((m, kk), out_dtype),
        scratch_shapes=[pltpu.VMEM((tm, tn), F32)],
        compiler_params=_params(("parallel", "parallel", "arbitrary")),
    )(a, w, *extra)


def mm_tn_col(a, b, jn, out_dtype, name):
    c, kk = a.shape
    g, _, n = b.shape
    ns = n // jn
    tm, tk = _pick(kk, (512, 256, 128)), _pick(c, (2048, 1024, 512, 256, 128))
    return _mm(a, b, mode="tn", grid=(kk // tm, g * jn, c // tk),
               a_spec=pl.BlockSpec((tk, tm), lambda i, j, k: (k, i)),
               b_spec=pl.BlockSpec((None, tk, ns), lambda i, j, k: (j // jn, k, j % jn)),
               o_spec=pl.BlockSpec((None, None, tm, ns), lambda i, j, k: (j // jn, j % jn, i, 0)),
               out_shape=jax.ShapeDtypeStruct((g, jn, kk, ns), out_dtype), acc_shape=(tm, ns), name=name)


def _rstd(x):
    return lax.rsqrt(jnp.mean(x * x, axis=-1, keepdims=True) + NORM_EPS)


def _rms_bwd(x, g, dy):
    r = _rstd(x)
    xn = x * r
    dyg = dy * g
    dx = r * (dyg - xn * jnp.mean(dyg * xn, axis=-1, keepdims=True))
    return dx, jnp.sum(dy * xn, axis=0, keepdims=True)


def _row_tile(t):
    return _pick(t, (256, 128, 64, 32, 16, 8))


def norm_in(h, g, name):
    t, d = h.shape
    tr = _row_tile(t)

    def body(h_ref, g_ref, u_ref):
        x = h_ref[...]
        u_ref[...] = (x * _rstd(x) * g_ref[...]).astype(BF16)

    return pl.pallas_call(
        body, name=name, grid=(t // tr,),
        in_specs=[pl.BlockSpec((tr, d), lambda i: (i, 0)), pl.BlockSpec((1, d), lambda i: (0, 0))],
        out_specs=pl.BlockSpec((tr, d), lambda i: (i, 0)),
        out_shape=jax.ShapeDtypeStruct((t, d), BF16), compiler_params=_params(("parallel",)),
    )(h, g)


def resid_norm(h, y, g, scale, g_next, name):
    t, d = h.shape
    tr = _row_tile(t)

    def body(h_ref, y_ref, g_ref, gn_ref, o_ref, u_ref):
        yv = y_ref[...]
        out = h_ref[...] + scale * (yv * _rstd(yv) * g_ref[...])
        o_ref[...] = out
        u_ref[...] = (out * _rstd(out) * gn_ref[...]).astype(BF16)

    row = pl.BlockSpec((tr, d), lambda i: (i, 0))
    vec = pl.BlockSpec((1, d), lambda i: (0, 0))
    return pl.pallas_call(
        body, name=name, grid=(t // tr,), in_specs=[row, row, vec, vec], out_specs=[row, row],
        out_shape=[jax.ShapeDtypeStruct((t, d), F32), jax.ShapeDtypeStruct((t, d), BF16)],
        compiler_params=_params(("parallel",)),
    )(h, y, g, g_next)


def post_bwd(dh, y, g, scale, name):
    t, d = dh.shape
    tr = _row_tile(t)

    def body(dh_ref, y_ref, g_ref, dy_ref, dg_ref):
        @pl.when(pl.program_id(0) == 0)
        def _():
            dg_ref[...] = jnp.zeros_like(dg_ref)

        dx, dg = _rms_bwd(y_ref[...], g_ref[...], scale * dh_ref[...])
        dy_ref[...] = dx.astype(BF16)
        dg_ref[...] += dg

    row = pl.BlockSpec((tr, d), lambda i: (i, 0))
    vec = pl.BlockSpec((1, d), lambda i: (0, 0))
    return pl.pallas_call(
        body, name=name, grid=(t // tr,), in_specs=[row, row, vec], out_specs=[row, vec],
        out_shape=[jax.ShapeDtypeStruct((t, d), BF16), jax.ShapeDtypeStruct((1, d), F32)],
        compiler_params=_params(("arbitrary",)),
    )(dh, y, g)


def pre_bwd(dh, h, g, dus, name, after=()):
    t, d = dh.shape
    tr = _row_tile(t)
    n_du = len(dus)
    after = [m for m in after if m is not None]

    def body(*refs):
        dh_ref, h_ref, g_ref = refs[:3]
        du_refs = refs[3:3 + n_du]
        o_ref, dg_ref = refs[3 + n_du + len(after):]

        @pl.when(pl.program_id(0) == 0)
        def _():
            dg_ref[...] = jnp.zeros_like(dg_ref)

        du = du_refs[0][...]
        for r in du_refs[1:]:
            du = du + r[...]
        dx, dg = _rms_bwd(h_ref[...], g_ref[...], du)
        o_ref[...] = dh_ref[...] + dx
        dg_ref[...] += dg

    row = pl.BlockSpec((tr, d), lambda i: (i, 0))
    vec = pl.BlockSpec((1, d), lambda i: (0, 0))
    return pl.pallas_call(
        body, name=name, grid=(t // tr,),
        in_specs=[row, row, vec] + [row] * n_du + [pl.BlockSpec(memory_space=pl.ANY)] * len(after),
        out_specs=[row, vec], out_shape=[jax.ShapeDtypeStruct((t, d), F32), jax.ShapeDtypeStruct((1, d), F32)],
        compiler_params=_params(("arbitrary",)),
    )(dh, h, g, *dus, *after)


def _ew_tiles(t, f):
    return _pick(t, (256, 128, 64, 32, 16, 8)), _pick(f, (1408, 1024, 512, 256, 128))


def swiglu_act(gu, name):
    _, t, f = gu.shape
    tr, tc = _ew_tiles(t, f)

    def body(gu_ref, o_ref):
        o_ref[...] = (_silu(gu_ref[0]) * gu_ref[1]).astype(BF16)

    return pl.pallas_call(
        body, name=name, grid=(t // tr, f // tc),
        in_specs=[pl.BlockSpec((2, tr, tc), lambda i, j: (0, i, j))],
        out_specs=pl.BlockSpec((tr, tc), lambda i, j: (i, j)),
        out_shape=jax.ShapeDtypeStruct((t, f), BF16), compiler_params=_params(("parallel", "parallel")),
    )(gu)


def swiglu_bwd(dact, gu, name):
    _, t, f = gu.shape
    tr, tc = _ew_tiles(t, f)

    def body(da_ref, gu_ref, o_ref):
        da = da_ref[...]
        gate = gu_ref[0]
        o_ref[0] = (da * gu_ref[1] * _silu_grad(gate)).astype(BF16)
        o_ref[1] = (da * _silu(gate)).astype(BF16)

    return pl.pallas_call(
        body, name=name, grid=(t // tr, f // tc),
        in_specs=[pl.BlockSpec((tr, tc), lambda i, j: (i, j)), pl.BlockSpec((2, tr, tc), lambda i, j: (0, i, j))],
        out_specs=pl.BlockSpec((2, tr, tc), lambda i, j: (0, i, j)),
        out_shape=jax.ShapeDtypeStruct((2, t, f), BF16), compiler_params=_params(("parallel", "parallel")),
    )(dact, gu)


def _col_blocks():
    w = WIDTH // LANES
    return dict(q_a=0, k_a=w, v_a=2 * w, q_b=3 * w, f_b=4 * w, i_b=5 * w, g_b=6 * w, gate_a=7 * w,
                gate_b=7 * w + D_MODEL // LANES)


def _tri(n, lower):
    r = lax.broadcasted_iota(jnp.int32, (n, n), 0)
    c = lax.broadcasted_iota(jnp.int32, (n, n), 1)
    return jnp.where((r >= c) if lower else (r <= c), 1.0, 0.0).astype(F32)


def _dot_hi(a, b):
    return jnp.dot(a, b, precision=lax.Precision.HIGHEST, preferred_element_type=F32)


def fox_prep(fa, bias, name):
    t, w = fa.shape
    tb = _pick(t, (256, 128, 64))

    def body(fa_ref, b_ref, c_ref, carry_ref):
        @pl.when(pl.program_id(0) == 0)
        def _():
            carry_ref[...] = jnp.zeros_like(carry_ref)

        z = fa_ref[...] + b_ref[...]
        lf = jnp.minimum(z, 0.0) - jnp.log(1.0 + jnp.exp(-jnp.abs(z)))
        c = _dot_hi(_tri(tb, True), lf) + carry_ref[...]
        c_ref[...] = c
        carry_ref[...] = carry_ref[...] + jnp.sum(lf, axis=0, keepdims=True)

    return pl.pallas_call(
        body, name=name, grid=(t // tb,),
        in_specs=[pl.BlockSpec((tb, w), lambda i: (i, 0)), pl.BlockSpec((1, w), lambda i: (0, 0))],
        out_specs=pl.BlockSpec((tb, w), lambda i: (i, 0)),
        out_shape=jax.ShapeDtypeStruct((t, w), F32), scratch_shapes=[pltpu.VMEM((1, w), F32)],
        compiler_params=_params(("arbitrary",)),
    )(fa, bias)


def fox_post_bwd(dc, fa, bias, name):
    t, w = fa.shape
    tb = _pick(t, (256, 128, 64))
    nb = t // tb

    def body(dc_ref, fa_ref, b_ref, dfa_ref, db_ref, carry_ref):
        @pl.when(pl.program_id(0) == 0)
        def _():
            carry_ref[...] = jnp.zeros_like(carry_ref)
            db_ref[...] = jnp.zeros_like(db_ref)

        dcv = dc_ref[...]
        dlf = _dot_hi(_tri(tb, False), dcv) + carry_ref[...]
        z = fa_ref[...] + b_ref[...]
        dz = dlf * _sigmoid(-z)
        dfa_ref[...] = dz.astype(BF16)
        db_ref[...] += jnp.sum(dz, axis=0, keepdims=True)
        carry_ref[...] = carry_ref[...] + jnp.sum(dcv, axis=0, keepdims=True)

    rev = pl.BlockSpec((tb, w), lambda i: (nb - 1 - i, 0))
    vec = pl.BlockSpec((1, w), lambda i: (0, 0))
    return pl.pallas_call(
        body, name=name, grid=(nb,), in_specs=[rev, rev, vec], out_specs=[rev, vec],
        out_shape=[jax.ShapeDtypeStruct((t, w), BF16), jax.ShapeDtypeStruct((1, w), F32)],
        scratch_shapes=[pltpu.VMEM((1, w), F32)], compiler_params=_params(("arbitrary",)),
    )(dc, fa, bias)


def _fox_probs(q_ref, k_ref, cc_ref, cr_ref, qi, tq, t):
    scale = HEAD_DIM ** -0.5
    s = lax.dot_general(q_ref[...].astype(BF16), k_ref[...].astype(BF16), _DN["nt"], preferred_element_type=F32)
    logits = s * scale + cc_ref[...] - cr_ref[...]
    qpos = qi * tq + lax.broadcasted_iota(jnp.int32, (tq, t), 0)
    kpos = lax.broadcasted_iota(jnp.int32, (tq, t), 1)
    logits = jnp.where(kpos <= qpos, logits, NEG_BIG)
    m = jnp.max(logits, axis=-1, keepdims=True)
    p = jnp.exp(logits - m)
    return p / jnp.sum(p, axis=-1, keepdims=True)


FOX_SEGMENTS = 4


def _fox_segments(t):
    tq = _pick(t, (256, 128))
    nseg = min(FOX_SEGMENTS, t // tq)
    return tq, nseg, t // tq // nseg


def _fox_specs(t, q0, kt, tq):
    cb = _col_blocks()
    dh = HEAD_DIM
    return [pl.BlockSpec((tq, dh), lambda h, i: (q0 + i, cb["q_a"] + h)),
            pl.BlockSpec((kt, dh), lambda h, i: (0, cb["k_a"] + h)),
            pl.BlockSpec((kt, dh), lambda h, i: (0, cb["v_a"] + h)),
            pl.BlockSpec((None, tq, 1), lambda h, i: (h, q0 + i, 0)),
            pl.BlockSpec((None, 1, kt), lambda h, i: (h, 0, 0))]


def fox_fwd(proj, c_col, c_row, name):
    t = proj.shape[0]
    tq, nseg, nq = _fox_segments(t)
    dh = HEAD_DIM

    def segment(out, r):
        q0, kt = r * nq, (r + 1) * nq * tq

        def body(q_ref, k_ref, v_ref, cc_ref, cr_ref, prev_ref, o_ref):
            p = _fox_probs(q_ref, k_ref, cc_ref, cr_ref, q0 + pl.program_id(1), tq, kt)
            o_ref[...] = jnp.dot(p.astype(BF16), v_ref[...].astype(BF16), preferred_element_type=F32).astype(BF16)

        return pl.pallas_call(
            body, name="%s_%d" % (name, r), grid=(HEADS, nq),
            in_specs=_fox_specs(t, q0, kt, tq) + [pl.BlockSpec(memory_space=pl.ANY)],
            out_specs=pl.BlockSpec((tq, dh), lambda h, i: (q0 + i, h)),
            out_shape=jax.ShapeDtypeStruct((t, WIDTH), BF16), input_output_aliases={5: 0},
            compiler_params=_params(("parallel", "parallel")),
        )(proj, proj, proj, c_col, c_row, out)

    out = lax.empty((t, WIDTH), BF16)
    for r in range(nseg):
        out = segment(out, r)
    return out


def fox_bwd(proj, c_col, c_row, do_ab, name):
    t = proj.shape[0]
    tq, nseg, nq = _fox_segments(t)
    dh = HEAD_DIM
    scale = HEAD_DIM ** -0.5

    def segment(acc, r):
        q0, kt = r * nq, (r + 1) * nq * tq

        def body(q_ref, k_ref, v_ref, cc_ref, cr_ref, do_ref, dqp_ref, dkp_ref, dvp_ref, dccp_ref, dcrp_ref,
                 dq_ref, dk_ref, dv_ref, dcc_ref, dcr_ref):
            @pl.when(pl.program_id(1) == 0)
            def _():
                dk_ref[...] = dkp_ref[...]
                dv_ref[...] = dvp_ref[...]
                dcr_ref[...] = dcrp_ref[...]

            p = _fox_probs(q_ref, k_ref, cc_ref, cr_ref, q0 + pl.program_id(1), tq, kt)
            dov = do_ref[...].astype(BF16)
            kb = k_ref[...].astype(BF16)
            dv_ref[...] += lax.dot_general(p.astype(BF16), dov, _DN["tn"], preferred_element_type=F32)
            dp = lax.dot_general(dov, v_ref[...].astype(BF16), _DN["nt"], preferred_element_type=F32)
            ds = p * (dp - jnp.sum(p * dp, axis=-1, keepdims=True))
            dcc_ref[...] = jnp.sum(ds, axis=-1, keepdims=True)
            dcr_ref[...] -= jnp.sum(ds, axis=0, keepdims=True)
            dss = (ds * scale).astype(BF16)
            dq_ref[...] = jnp.dot(dss, kb, preferred_element_type=F32).astype(BF16)
            dk_ref[...] += lax.dot_general(dss, q_ref[...].astype(BF16), _DN["tn"], preferred_element_type=F32)

        rows = pl.BlockSpec((tq, dh), lambda h, i: (q0 + i, h))
        keys = pl.BlockSpec((kt, dh), lambda h, i: (0, h))
        col = pl.BlockSpec((None, tq, 1), lambda h, i: (h, q0 + i, 0))
        row = pl.BlockSpec((None, 1, kt), lambda h, i: (h, 0, 0))
        anywhere = pl.BlockSpec(memory_space=pl.ANY)
        return pl.pallas_call(
            body, name="%s_%d" % (name, r), grid=(HEADS, nq),
            in_specs=_fox_specs(t, q0, kt, tq)
            + [pl.BlockSpec((None, tq, dh), lambda h, i: (0, q0 + i, h)), anywhere, keys, keys, anywhere, row],
            out_specs=[rows, keys, keys, col, row],
            out_shape=[jax.ShapeDtypeStruct(a.shape, a.dtype) for a in acc],
            input_output_aliases={6 + k: k for k in range(5)},
            compiler_params=_params(("parallel", "arbitrary")),
        )(proj, proj, proj, c_col, c_row, do_ab, *acc)

    acc = [lax.empty((t, WIDTH), BF16), jnp.zeros((t, WIDTH), F32), jnp.zeros((t, WIDTH), F32),
           lax.empty((HEADS, t, 1), F32), jnp.zeros((HEADS, 1, t), F32)]
    for r in range(nseg):
        acc = segment(acc, r)
    return acc


def _lower_bound(lg_ref):
    l0 = lg_ref[0:1, :]
    l1 = lg_ref[1:2, :]
    m = jnp.maximum(l0, l1)
    e0 = jnp.exp(l0 - m)
    e1 = jnp.exp(l1 - m)
    return e0 / (e0 + e1)


def _hgrn_inputs(qb_ref, fb_ref, lg_ref, q_s, k_s, cum_s):
    lb = _lower_bound(lg_ref)
    sig = _sigmoid(fb_ref[...])
    f = lb + (1.0 - lb) * sig
    q_s[...] = _silu(qb_ref[...])
    k_s[...] = 1.0 - f
    cum_s[...] = _dot_hi(_tri(CHUNK, True), jnp.log(f))
    return lb, sig, f


def _boundary(cum_s, a):
    if a == 0:
        return jnp.zeros((1, HEAD_DIM), F32)
    return cum_s[pl.ds(SUB * a - 1, 1), :]


def _hgrn_scores(q_s, k_s, cum_s):
    cum = cum_s[...]
    kk = k_s[...]
    lane = lax.broadcasted_iota(jnp.int32, (SUB, CHUNK), 1)
    row = lax.broadcasted_iota(jnp.int32, (SUB, 1), 0)
    blocks = []
    for a in range(CHUNK // SUB):
        rows = pl.ds(SUB * a, SUB)
        ca = _boundary(cum_s, a)
        cum_a = cum_s[rows, :]
        q_a = q_s[rows, :]
        qa = q_a * jnp.exp(cum_a - ca)
        ka = kk * jnp.exp(jnp.minimum(ca - cum, 0.0))
        blk = lax.dot_general(qa, ka, _DN["nt"], preferred_element_type=F32)
        blk = jnp.where(lane < SUB * a, blk, 0.0)
        for s in range(SUB):
            r = SUB * a + s
            e = jnp.exp(jnp.minimum(cum_a - cum_s[pl.ds(r, 1), :], 0.0))
            col = jnp.sum(q_a * k_s[pl.ds(r, 1), :] * e, axis=-1, keepdims=True)
            col = jnp.where(row >= s, col, 0.0)
            blk = jnp.where(lane == r, col, blk)
        blocks.append(blk)
    return jnp.concatenate(blocks, axis=0)


def hgrn_fwd(proj, lb_logits, name, after=()):
    after = [m for m in after if m is not None]
    t = proj.shape[0]
    n = t // CHUNK
    cb = _col_blocks()
    dh = HEAD_DIM

    hb = HGRN_HEADS_PER_STEP
    w = hb * dh

    def one_head(qb_ref, fb_ref, ib_ref, lg_ref, o_ref, st_ref, a_ref, state, q_s, k_s, cum_s):
        _hgrn_inputs(qb_ref, fb_ref, lg_ref, q_s, k_s, cum_s)
        st = state[...]
        st_ref[...] = st
        cum = cum_s[...]
        v = ib_ref[...]
        qe = q_s[...] * jnp.exp(cum)
        inter = lax.dot_general(qe, st, _DN["nt"], preferred_element_type=F32)
        a_mat = _hgrn_scores(q_s, k_s, cum_s)
        a_ref[...] = a_mat
        o_ref[...] = inter + jnp.dot(a_mat, v, preferred_element_type=F32)
        last = cum_s[pl.ds(CHUNK - 1, 1), :]
        kd = k_s[...] * jnp.exp(last - cum)
        state[...] = st * jnp.exp(last) + lax.dot_general(v, kd, _DN["tn"], preferred_element_type=F32)

    def body(qb_ref, fb_ref, ib_ref, lg_ref, *rest):
        o_ref, st_ref, a_ref = rest[len(after):len(after) + 3]
        scratch = rest[len(after) + 3:]

        @pl.when(pl.program_id(1) == 0)
        def _():
            for j in range(hb):
                scratch[4 * j][...] = jnp.zeros((dh, dh), F32)

        for j in range(hb):
            cols = (slice(None), pl.ds(j * dh, dh))
            one_head(qb_ref.at[cols], fb_ref.at[cols], ib_ref.at[cols], lg_ref.at[cols], o_ref.at[cols],
                     st_ref.at[j], a_ref.at[j], *scratch[4 * j:4 * j + 4])

    blk = lambda off: pl.BlockSpec((CHUNK, w), lambda h, i: (i, off // hb + h))
    return pl.pallas_call(
        body, name=name, grid=(HEADS // hb, n),
        in_specs=[blk(cb["q_b"]), blk(cb["f_b"]), blk(cb["i_b"]), pl.BlockSpec((2, w), lambda h, i: (0, h))]
        + [pl.BlockSpec(memory_space=pl.ANY)] * len(after),
        out_specs=[pl.BlockSpec((CHUNK, w), lambda h, i: (i, h)),
                   pl.BlockSpec((hb, None, dh, dh), lambda h, i: (h, i, 0, 0)),
                   pl.BlockSpec((hb, None, CHUNK, CHUNK), lambda h, i: (h, i, 0, 0))],
        out_shape=[jax.ShapeDtypeStruct((t, WIDTH), F32), jax.ShapeDtypeStruct((HEADS, n, dh, dh), F32),
                   jax.ShapeDtypeStruct((HEADS, n, CHUNK, CHUNK), F32)],
        scratch_shapes=([pltpu.VMEM((dh, dh), F32)] + [pltpu.VMEM((CHUNK, dh), F32)] * 3) * hb,
        compiler_params=_params(("parallel", "arbitrary")),
    )(proj, proj, proj, lb_logits, *after)


def hgrn_bwd(proj, lb_logits, states, scores, do, name):
    t = proj.shape[0]
    n = t // CHUNK
    cb = _col_blocks()
    dh = HEAD_DIM
    nsub = CHUNK // SUB

    hb = HGRN_HEADS_PER_STEP
    w = hb * dh

    def one_head(qb_ref, fb_ref, ib_ref, lg_ref, st_ref, a_ref, do_ref, dqb_ref, dfb_ref, dib_ref, dlb_ref,
                 dstate, q_s, k_s, cum_s, da_s, dq_s, dk_s):
        lb, sig, f = _hgrn_inputs(qb_ref, fb_ref, lg_ref, q_s, k_s, cum_s)
        st = st_ref[...]
        dst = dstate[...]
        cum = cum_s[...]
        q = q_s[...]
        kk = k_s[...]
        v = ib_ref[...]
        dov = do_ref[...]
        e_cum = jnp.exp(cum)
        qe = q * e_cum
        last = cum_s[pl.ds(CHUNK - 1, 1), :]
        e_last = jnp.exp(last)
        e_tail = jnp.exp(last - cum)
        kd = kk * e_tail

        a_mat = a_ref[...]
        tri = _tri(CHUNK, True)
        da_s[...] = lax.dot_general(dov, v, _DN["nt"], preferred_element_type=F32) * tri
        dv = (lax.dot_general(a_mat, dov, _DN["tn"], preferred_element_type=F32)
              + lax.dot_general(kd, dst, _DN["nt"], preferred_element_type=F32))
        dk_state = jnp.dot(v, dst, preferred_element_type=F32) * e_tail
        dq_inter = jnp.dot(dov, st, preferred_element_type=F32) * e_cum
        dstate[...] = dst * e_last + lax.dot_general(dov, qe, _DN["tn"], preferred_element_type=F32)

        lane = lax.broadcasted_iota(jnp.int32, (SUB, CHUNK), 1)
        row = lax.broadcasted_iota(jnp.int32, (SUB, 1), 0)
        dk_s[...] = jnp.zeros_like(dk_s)
        for a in range(nsub):
            rows = pl.ds(SUB * a, SUB)
            ca = _boundary(cum_s, a)
            cum_a = cum_s[rows, :]
            q_a = q_s[rows, :]
            ea = jnp.exp(cum_a - ca)
            eb = jnp.exp(jnp.minimum(ca - cum, 0.0))
            da_a = da_s[rows, :]
            da_off = jnp.where(lane < SUB * a, da_a, 0.0)
            dq_a = ea * jnp.dot(da_off, kk * eb, preferred_element_type=F32)
            dk_s[...] += eb * lax.dot_general(da_off, q_a * ea, _DN["tn"], preferred_element_type=F32)
            dk_rows = jnp.zeros((SUB, dh), F32)
            for s in range(SUB):
                r = SUB * a + s
                e = jnp.exp(jnp.minimum(cum_a - cum_s[pl.ds(r, 1), :], 0.0))
                dcol = jnp.sum(jnp.where(lane == r, da_a, 0.0), axis=-1, keepdims=True)
                dcol = jnp.where(row >= s, dcol, 0.0)
                w = dcol * e
                dq_a = dq_a + w * k_s[pl.ds(r, 1), :]
                dk_rows = jnp.where(row == s, jnp.sum(w * q_a, axis=0, keepdims=True), dk_rows)
            dq_s[rows, :] = dq_a
            dk_s[rows, :] += dk_rows

        dq = dq_inter + dq_s[...]
        dk = dk_s[...] + dk_state
        d_last = (jnp.sum(dst * st, axis=0, keepdims=True) * e_last
                  + jnp.sum(kk * dk_state, axis=0, keepdims=True))
        rowc = lax.broadcasted_iota(jnp.int32, (CHUNK, 1), 0)
        dcum = q * dq - kk * dk + jnp.where(rowc == CHUNK - 1, d_last, 0.0)
        dg = _dot_hi(_tri(CHUNK, False), dcum)
        df = dg / f - dk
        dqb_ref[...] = (dq * _silu_grad(qb_ref[...])).astype(BF16)
        dfb_ref[...] = (df * (1.0 - lb) * sig * (1.0 - sig)).astype(BF16)
        dib_ref[...] = dv.astype(BF16)
        dlb_ref[...] += jnp.sum(df * (1.0 - sig), axis=0, keepdims=True)

    def body(qb_ref, fb_ref, ib_ref, lg_ref, st_ref, a_ref, do_ref, dqb_ref, dfb_ref, dib_ref, dlb_ref, *scratch):
        @pl.when(pl.program_id(1) == 0)
        def _():
            for j in range(hb):
                scratch[7 * j][...] = jnp.zeros((dh, dh), F32)
            dlb_ref[...] = jnp.zeros_like(dlb_ref)

        for j in range(hb):
            cols = (slice(None), pl.ds(j * dh, dh))
            one_head(qb_ref.at[cols], fb_ref.at[cols], ib_ref.at[cols], lg_ref.at[cols], st_ref.at[j], a_ref.at[j],
                     do_ref.at[cols], dqb_ref.at[cols], dfb_ref.at[cols], dib_ref.at[cols], dlb_ref.at[cols],
                     *scratch[7 * j:7 * j + 7])

    blk = lambda off: pl.BlockSpec((CHUNK, w), lambda h, i: (n - 1 - i, off // hb + h))
    out_blk = pl.BlockSpec((CHUNK, w), lambda h, i: (n - 1 - i, h))
    return pl.pallas_call(
        body, name=name, grid=(HEADS // hb, n),
        in_specs=[blk(cb["q_b"]), blk(cb["f_b"]), blk(cb["i_b"]), pl.BlockSpec((2, w), lambda h, i: (0, h)),
                  pl.BlockSpec((hb, None, dh, dh), lambda h, i: (h, n - 1 - i, 0, 0)),
                  pl.BlockSpec((hb, None, CHUNK, CHUNK), lambda h, i: (h, n - 1 - i, 0, 0)), out_blk],
        out_specs=[out_blk, out_blk, out_blk, pl.BlockSpec((1, w), lambda h, i: (0, h))],
        out_shape=[jax.ShapeDtypeStruct((t, WIDTH), BF16)] * 3 + [jax.ShapeDtypeStruct((1, WIDTH), F32)],
        scratch_shapes=([pltpu.VMEM((dh, dh), F32)] + [pltpu.VMEM((CHUNK, dh), F32)] * 3
                        + [pltpu.VMEM((CHUNK, CHUNK), F32)] + [pltpu.VMEM((CHUNK, dh), F32)] * 2) * hb,
        compiler_params=_params(("parallel", "arbitrary")),
    )(proj, proj, proj, lb_logits, states, scores, do)


def lb_bwd(dlb, lb_logits, name):
    def body(dlb_ref, lg_ref, o_ref):
        p0 = _lower_bound(lg_ref)
        d0 = dlb_ref[...] * p0 * (1.0 - p0)
        o_ref[0:1, :] = d0
        o_ref[1:2, :] = -d0

    return pl.pallas_call(body, name=name, out_shape=jax.ShapeDtypeStruct(lb_logits.shape, F32))(dlb, lb_logits)


def gnorm_fwd(o_raw, proj, norm_g, name, after=()):
    after = [m for m in after if m is not None]
    t = o_raw.shape[0]
    tr = _row_tile(t)
    cb = _col_blocks()
    dh = HEAD_DIM

    gcol = cb["g_b"] * LANES // WIDTH

    def body(o_ref, gb_ref, g_ref, *rest):
        for h in range(HEADS):
            cols = (slice(None), pl.ds(h * dh, dh))
            x = o_ref[cols]
            rest[-1][cols] = (x * _rstd(x) * g_ref[...] * _silu(gb_ref[cols])).astype(BF16)

    return pl.pallas_call(
        body, name=name, grid=(t // tr,),
        in_specs=[pl.BlockSpec((tr, WIDTH), lambda i: (i, 0)), pl.BlockSpec((tr, WIDTH), lambda i: (i, gcol)),
                  pl.BlockSpec((1, dh), lambda i: (0, 0))] + [pl.BlockSpec(memory_space=pl.ANY)] * len(after),
        out_specs=pl.BlockSpec((tr, WIDTH), lambda i: (i, 0)),
        out_shape=jax.ShapeDtypeStruct((t, WIDTH), BF16), compiler_params=_params(("parallel",)),
    )(o_raw, proj, norm_g, *after)


def gnorm_bwd(dy_ab, o_raw, proj, norm_g, name):
    t = o_raw.shape[0]
    tr = _row_tile(t)
    cb = _col_blocks()
    dh = HEAD_DIM
    gcol = cb["g_b"] * LANES // WIDTH

    def body(dy_ref, o_ref, gb_ref, g_ref, do_ref, dgb_ref, dg_ref):
        @pl.when(pl.program_id(0) == 0)
        def _():
            dg_ref[...] = jnp.zeros_like(dg_ref)

        g = g_ref[...]
        acc = jnp.zeros_like(g)
        for h in range(HEADS):
            cols = (slice(None), pl.ds(h * dh, dh))
            x = o_ref[cols]
            gb = gb_ref[cols]
            dyv = dy_ref[cols]
            dx, dg = _rms_bwd(x, g, dyv * _silu(gb))
            do_ref[cols] = dx
            dgb_ref[cols] = (dyv * (x * _rstd(x) * g) * _silu_grad(gb)).astype(BF16)
            acc = acc + dg
        dg_ref[...] += acc

    rows = pl.BlockSpec((tr, WIDTH), lambda i: (i, 0))
    vec = pl.BlockSpec((1, dh), lambda i: (0, 0))
    return pl.pallas_call(
        body, name=name, grid=(t // tr,),
        in_specs=[pl.BlockSpec((None, tr, WIDTH), lambda i: (1, i, 0)), rows,
                  pl.BlockSpec((tr, WIDTH), lambda i: (i, gcol)), vec],
        out_specs=[rows, rows, vec],
        out_shape=[jax.ShapeDtypeStruct((t, WIDTH), F32), jax.ShapeDtypeStruct((t, WIDTH), BF16),
                   jax.ShapeDtypeStruct((1, dh), F32)],
        compiler_params=_params(("arbitrary",)),
    )(dy_ab, o_raw, proj, norm_g)


def merge_fwd(proj, y, name):
    _, t, d = y.shape
    tr = _row_tile(t)
    tc = _pick(d, (1024, 512, 256, 128))
    cb = _col_blocks()
    ga, gb = cb["gate_a"] * LANES // tc, cb["gate_b"] * LANES // tc

    def body(ga_ref, gb_ref, y_ref, o_ref):
        o_ref[...] = (_sigmoid(ga_ref[...]) * y_ref[0] + _sigmoid(gb_ref[...]) * y_ref[1]).astype(BF16)

    return pl.pallas_call(
        body, name=name, grid=(t // tr, d // tc),
        in_specs=[pl.BlockSpec((tr, tc), lambda i, j: (i, ga + j)), pl.BlockSpec((tr, tc), lambda i, j: (i, gb + j)),
                  pl.BlockSpec((2, tr, tc), lambda i, j: (0, i, j))],
        out_specs=pl.BlockSpec((tr, tc), lambda i, j: (i, j)),
        out_shape=jax.ShapeDtypeStruct((t, d), BF16), compiler_params=_params(("parallel", "parallel")),
    )(proj, proj, y)


def merge_bwd(dm, proj, y, name):
    _, t, d = y.shape
    tr = _row_tile(t)
    tc = _pick(d, (1024, 512, 256, 128))
    cb = _col_blocks()
    ga, gb = cb["gate_a"] * LANES // tc, cb["gate_b"] * LANES // tc

    def body(dm_ref, ga_ref, gb_ref, y_ref, dg_ref, dy_ref):
        dmv = dm_ref[...]
        for idx, g_ref in enumerate((ga_ref, gb_ref)):
            s = _sigmoid(g_ref[...])
            dg_ref[idx] = (dmv * y_ref[idx] * s * (1.0 - s)).astype(BF16)
            dy_ref[idx] = (dmv * s).astype(BF16)

    pair = pl.BlockSpec((2, tr, tc), lambda i, j: (0, i, j))
    return pl.pallas_call(
        body, name=name, grid=(t // tr, d // tc),
        in_specs=[pl.BlockSpec((tr, tc), lambda i, j: (i, j)), pl.BlockSpec((tr, tc), lambda i, j: (i, ga + j)),
                  pl.BlockSpec((tr, tc), lambda i, j: (i, gb + j)), pair],
        out_specs=[pair, pair],
        out_shape=[jax.ShapeDtypeStruct((2, t, d), BF16)] * 2, compiler_params=_params(("parallel", "parallel")),
    )(dm, proj, proj, y)


def ple_tail(h, a, b, g, target, name):
    t, d = h.shape
    tr = _row_tile(t)

    def body(h_ref, a_ref, b_ref, g_ref, t_ref, loss_ref, dh_ref, da_ref, db_ref, dg_ref):
        @pl.when(pl.program_id(0) == 0)
        def _():
            loss_ref[...] = jnp.zeros_like(loss_ref)
            dg_ref[...] = jnp.zeros_like(dg_ref)

        s = _sigmoid(a_ref[...])
        bv = b_ref[...]
        z = s * bv
        gv = g_ref[...]
        err = h_ref[...] + z * _rstd(z) * gv - t_ref[...]
        loss_ref[...] += 0.5 * jnp.sum(jnp.sum(err * err, axis=-1, keepdims=True), axis=0, keepdims=True) / d
        dh = err / d
        dh_ref[...] = dh
        dz, dg = _rms_bwd(z, gv, dh)
        da_ref[...] = (dz * bv * s * (1.0 - s)).astype(BF16)
        db_ref[...] = (dz * s).astype(BF16)
        dg_ref[...] += dg

    row = pl.BlockSpec((tr, d), lambda i: (i, 0))
    vec = pl.BlockSpec((1, d), lambda i: (0, 0))
    return pl.pallas_call(
        body, name=name, grid=(t // tr,), in_specs=[row, row, row, vec, row],
        out_specs=[pl.BlockSpec((1, 1), lambda i: (0, 0)), row, row, row, vec],
        out_shape=[jax.ShapeDtypeStruct((1, 1), F32), jax.ShapeDtypeStruct((t, d), F32),
                   jax.ShapeDtypeStruct((t, d), BF16), jax.ShapeDtypeStruct((t, d), BF16),
                   jax.ShapeDtypeStruct((1, d), F32)],
        compiler_params=_params(("arbitrary",)),
    )(h, a, b, g, target)


def _ffn_fwd(h, u, post_g, next_g, get_w, idx, tag):
    gu = None
    for i, key in enumerate(GATE_UP_KEYS["gu" + idx]):
        w = get_w(key, h if gu is None else gu)
        gu = mm_nn_col(u, w, F32, "%s_gate_up_%d" % (tag, i), slot0=i, total=2, into=gu)
    act = swiglu_act(gu, tag + "_act")
    y = mm_nn_2d(act, get_w("down" + idx, gu), F32, tag + "_down")
    out, u_next = resid_norm(h, y, post_g, MACARON_SCALE, next_g, tag + "_out")
    return out, u_next, (h, u, gu, act, y)


def _ffn_bwd(dh, saved, pre_g, post_g, get_w, emit, advance, idx, tag):
    h, u, gu, act, y = saved
    dy, d_post = post_bwd(dh, y, post_g, MACARON_SCALE, tag + "_post_bwd")
    m1 = emit("down" + idx, mm_tn_2d(act, dy, F32, tag + "_dw_down"))
    dact = mm_nt_2d(dy, get_w("down" + idx), F32, tag + "_dact", after=[m1])
    m2 = advance(dact)
    dgu = swiglu_bwd(dact, gu, tag + "_act_bwd")
    m3 = emit("gu" + idx, mm_tn_col(u, dgu, N_CHIPS, F32, tag + "_dw_gate_up"))
    du = None
    for i, key in enumerate(GATE_UP_KEYS["gu" + idx]):
        du = mm_nt_col(dgu, get_w(key), F32, "%s_du_%d" % (tag, i), after=[m2, m3] if du is None else (),
                       slot0=i, init=du)
    m4 = advance(du)
    dh_in, d_pre = pre_bwd(dh, h, pre_g, [du], tag + "_pre_bwd", after=[m4])
    return dh_in, d_pre, d_post


def _heads_col(a):
    t = a.shape[0]
    at = a[:, :HEADS].T
    return at.reshape(HEADS, t, 1), at.reshape(HEADS, 1, t)


def layer_step(x, p, target, gains, fox_bias, lb_logits, norm_g, get_w, emit, advance, early):
    t = x.shape[0]
    u1 = norm_in(x, gains["ffn1_pre"], "ffn1_norm")
    h1, u2, s1 = _ffn_fwd(x, u1, gains["ffn1_post"], gains["mix_pre"], get_w, "1", "ffn1")

    proj = mm_nt_2d(u2, get_w("in_main", h1), F32, "mix_in")
    fa = mm_nt_2d(u2, get_w("in_fa"), F32, "mix_in_fa")
    mark = early("proj", proj)
    c = fox_prep(fa, fox_bias, "fox_prep")
    c_col, c_row = _heads_col(c)
    o_a = fox_fwd(proj, c_col, c_row, "fox_fwd")
    o_raw, states, scores = hgrn_fwd(proj, lb_logits, "hgrn_fwd", after=[mark])
    mark = early("gu2", o_raw)
    o_b = gnorm_fwd(o_raw, proj, norm_g, "hgrn_norm", after=[mark])
    o_ab = jnp.stack([o_a, o_b])
    y_ab = _mm_branches(o_ab, get_w("proj", proj), "mix_proj")
    merged = merge_fwd(proj, y_ab, "mix_merge")
    mo = mm_nn_2d(merged, get_w("out"), F32, "mix_out")
    h2, u3 = resid_norm(h1, mo, gains["mix_post"], 1.0, gains["ffn2_pre"], "mix_resid")

    h3, u4, s3 = _ffn_fwd(h2, u3, gains["ffn2_post"], gains["ple_pre"], get_w, "2", "ffn2")

    a4 = mm_nn_2d(u4, get_w("ple_gate"), F32, "ple_gate")
    b4 = mm_nn_col(p, get_w("ple_proj"), F32, "ple_proj")[0]
    loss, dh4, da4, db4, d_ple_post = ple_tail(h3, a4, b4, gains["ple_post"], target, "ple_tail")

    marks = [emit("ple_gate", mm_tn_2d(u4, da4, F32, "ple_dw_gate")),
             emit("ple_proj", mm_tn_col(p, db4[None], N_CHIPS, F32, "ple_dw_proj"))]
    du4 = mm_nt_2d(da4, get_w("ple_gate"), F32, "ple_du", after=marks)
    dh3, d_ple_pre = pre_bwd(dh4, h3, gains["ple_pre"], [du4], "ple_pre_bwd", after=[advance(du4)])

    dh2, d_f2_pre, d_f2_post = _ffn_bwd(dh3, s3, gains["ffn2_pre"], gains["ffn2_post"], get_w, emit, advance,
                                        "2", "ffn2")

    dmo, d_mix_post = post_bwd(dh2, mo, gains["mix_post"], 1.0, "mix_post_bwd")
    marks = [emit("out", mm_tn_2d(merged, dmo, F32, "mix_dw_out"))]
    dmerged = mm_nt_2d(dmo, get_w("out"), F32, "mix_dmerged", after=marks)
    dgate, dy_ab = merge_bwd(dmerged, proj, y_ab, "mix_merge_bwd")
    marks = [advance(dmerged), emit("proj", _mm_branches_dw(o_ab, dy_ab, "mix_dw_proj"))]
    do_ab = _mm_branches_bwd(dy_ab, get_w("proj"), "mix_do")
    do_raw, dg_b, d_norm_g = gnorm_bwd(do_ab, o_raw, proj, norm_g, "hgrn_norm_bwd")
    dq_b, df_b, di_b, dlb = hgrn_bwd(proj, lb_logits, states, scores, do_raw, "hgrn_bwd")
    d_lb_logits = lb_bwd(dlb, lb_logits, "lb_bwd")
    dq_a, dk_a, dv_a, dc_col, dc_row = fox_bwd(proj, c_col, c_row, do_ab, "fox_bwd")
    dc = (dc_col.reshape(HEADS, t) + dc_row.reshape(HEADS, t)).T
    dc = jnp.pad(dc, ((0, 0), (0, LANES - HEADS)))
    dfa, d_fox_bias = fox_post_bwd(dc, fa, fox_bias, "fox_post_bwd")
    dproj = jnp.concatenate([dq_a, dk_a.astype(BF16), dv_a.astype(BF16), dq_b, df_b, di_b, dg_b,
                             dgate[0], dgate[1]], axis=1)
    marks.append(emit("in_main", mm_tn_2d(dproj, u2, F32, "mix_dw_in")))
    marks.append(emit("in_fa", mm_tn_2d(dfa, u2, F32, "mix_dw_in_fa")))
    du2a = mm_nn_2d(dproj, get_w("in_main"), F32, "mix_du", after=marks)
    du2b = mm_nn_2d(dfa, get_w("in_fa"), F32, "mix_du_fa")
    dh1, d_mix_pre = pre_bwd(dh2, h1, gains["mix_pre"], [du2a, du2b], "mix_pre_bwd", after=[advance(du2a)])

    dx, d_f1_pre, d_f1_post = _ffn_bwd(dh1, s1, gains["ffn1_pre"], gains["ffn1_post"], get_w, emit, advance,
                                       "1", "ffn1")

    small = dict(ffn1_pre=d_f1_pre, ffn1_post=d_f1_post, mix_pre=d_mix_pre, mix_post=d_mix_post,
                 ffn2_pre=d_f2_pre, ffn2_post=d_f2_post, ple_pre=d_ple_pre, ple_post=d_ple_post,
                 fox_bias=d_fox_bias, lb_logits=d_lb_logits, norm_g=d_norm_g)
    return loss, dx, small


def _mm_branches(o_ab, w_proj, name):
    g, t, kk = o_ab.shape
    _, jn, _, ns = w_proj.shape
    tm = _pick(t, (512, 256, 128))
    return _mm(o_ab, w_proj, mode="nn", grid=(t // tm, g * jn, 1),
               a_spec=pl.BlockSpec((None, tm, kk), lambda i, j, k: (j // jn, i, 0)),
               b_spec=pl.BlockSpec((None, None, kk, ns), lambda i, j, k: (j // jn, j % jn, 0, 0)),
               o_spec=pl.BlockSpec((None, tm, ns), lambda i, j, k: (j // jn, i, j % jn)),
               out_shape=jax.ShapeDtypeStruct((g, t, jn * ns), F32), acc_shape=(tm, ns), name=name)


def _mm_branches_bwd(dy_ab, w_proj, name):
    g, t, _ = dy_ab.shape
    _, jn, kk, ns = w_proj.shape
    tm = _pick(t, (512, 256, 128))
    return _mm(dy_ab, w_proj, mode="nt", grid=(t // tm, g, jn),
               a_spec=pl.BlockSpec((None, tm, ns), lambda i, j, k: (j, i, k)),
               b_spec=pl.BlockSpec((None, None, kk, ns), lambda i, j, k: (j, k, 0, 0)),
               o_spec=pl.BlockSpec((None, tm, kk), lambda i, j, k: (j, i, 0)),
               out_shape=jax.ShapeDtypeStruct((g, t, kk), F32), acc_shape=(tm, kk), name=name)


def _mm_branches_dw(o_ab, dy_ab, name):
    g, t, kk = o_ab.shape
    d = dy_ab.shape[2]
    jn = N_CHIPS
    ns = d // jn
    return _mm(o_ab, dy_ab, mode="tn", grid=(1, g * jn, 1),
               a_spec=pl.BlockSpec((None, t, kk), lambda i, j, k: (j // jn, 0, 0)),
               b_spec=pl.BlockSpec((None, t, ns), lambda i, j, k: (j // jn, 0, j % jn)),
               o_spec=pl.BlockSpec((None, None, kk, ns), lambda i, j, k: (j // jn, j % jn, 0, 0)),
               out_shape=jax.ShapeDtypeStruct((g, jn, kk, ns), F32), acc_shape=(kk, ns), name=name)


HBM_SPEC = pl.BlockSpec(memory_space=pltpu.HBM)
SEM_SPEC = pl.BlockSpec(memory_space=pltpu.SEMAPHORE)
ANY_SPEC = pl.BlockSpec(memory_space=pl.ANY)
EFFECT = pltpu.SideEffectType.DATAFLOW_SIDE_EFFECTING


def _in_hbm(a):
    return pltpu.with_memory_space_constraint(a, pltpu.HBM)


def _place():
    x, y, c = lax.axis_index("x"), lax.axis_index("y"), lax.axis_index("c")
    chips = [(1 - x, y), (x, 1 - y), (1 - x, 1 - y)]
    return x, y, c, chips


def _half(shape, which, axis):
    n = shape[-2 + axis] // 2
    cut = pl.ds(which * n, n)
    return (cut, slice(None)) if axis == 0 else (slice(None), cut)


def _half_shape(shape, axis):
    s = list(shape)
    s[len(s) - 2 + axis] //= 2
    return tuple(s)


def _remote(src, dst, send_sems, recv_sems, k, to):
    return pltpu.make_async_remote_copy(src_ref=src, dst_ref=dst, send_sem=send_sems.at[k], recv_sem=recv_sems.at[k],
                                        device_id=to, device_id_type=MESH)


def split_start(name, srcs, lands, counts, copies):
    ns, nl, nset = len(srcs), len(lands), len(counts)

    def body(*refs):
        src_refs, land_refs = refs[:ns], refs[ns:ns + nl]
        sems = refs[ns + nl:ns + nl + 2 * nset]
        for s, plan in enumerate(copies(src_refs, land_refs)):
            for k, (src, dst, to) in enumerate(plan):
                _remote(src, dst, sems[2 * s], sems[2 * s + 1], k, to).start()
        refs[-1][...] = jnp.zeros_like(refs[-1])

    out_shape = []
    for n in counts:
        out_shape += [pltpu.SemaphoreType.DMA((n,)), pltpu.SemaphoreType.DMA((n,))]
    out_shape += [pltpu.HBM(a.shape, a.dtype) for a in list(srcs) + list(lands)]
    out_shape.append(jax.ShapeDtypeStruct((8, LANES), F32))
    res = pl.pallas_call(
        body, name=name, out_shape=tuple(out_shape), in_specs=[HBM_SPEC] * (ns + nl),
        out_specs=tuple([SEM_SPEC] * (2 * nset) + [HBM_SPEC] * (ns + nl) + [pl.BlockSpec(memory_space=pltpu.VMEM)]),
        input_output_aliases={i: 2 * nset + i for i in range(ns + nl)},
        compiler_params=pltpu.CompilerParams(has_side_effects=EFFECT),
    )(*[_in_hbm(a) for a in list(srcs) + list(lands)])
    sems = [(res[2 * s], res[2 * s + 1]) for s in range(nset)]
    return sems, list(res[2 * nset:2 * nset + ns]), list(res[2 * nset + ns:-1]), res[-1]


def split_wait(name, srcs, lands, sems, afters, copies):
    afters = [a for a in afters if a is not None]
    ns, nl, na = len(srcs), len(lands), len(afters)

    def body(*refs):
        src_refs, land_refs = refs[:ns], refs[ns:ns + nl]
        send_sems, recv_sems = refs[ns + nl:ns + nl + 2]
        for k, (src, dst, to) in enumerate(copies(src_refs, land_refs)):
            cp = _remote(src, dst, send_sems, recv_sems, k, to)
            cp.wait_send()
            cp.wait_recv()

    res = pl.pallas_call(
        body, name=name, out_shape=tuple(pltpu.HBM(a.shape, a.dtype) for a in list(srcs) + list(lands)),
        in_specs=[HBM_SPEC] * (ns + nl) + [SEM_SPEC, SEM_SPEC] + [ANY_SPEC] * na,
        out_specs=tuple([HBM_SPEC] * (ns + nl)), input_output_aliases={i: i for i in range(ns + nl)},
        compiler_params=pltpu.CompilerParams(has_side_effects=EFFECT),
    )(*srcs, *lands, sems[0], sems[1], *afters)
    return list(res[:ns]), list(res[ns:])


def _gather_plan(blocks):
    def copies(src_refs, land_refs):
        x, y, c, chips = _place()
        j_me = 2 * x + y
        plan = []
        for si, li, g, axis in blocks:
            src, land = src_refs[si], land_refs[li].at[g]
            mine = _half(src.shape, c, axis)
            for px, py in chips:
                plan.append((src.at[mine], land.at[(j_me,) + mine], (px, py, c)))
            plan.append((src, land.at[j_me], (x, y, 1 - c)))
        return plan
    return copies


def _gather_arrivals(blocks):
    def copies(src_refs, land_refs):
        x, y, c, chips = _place()
        j_me = 2 * x + y
        plan = []
        for si, li, g, axis in blocks:
            src, land = src_refs[si], land_refs[li].at[g]
            mine = _half(src.shape, c, axis)
            for px, py in chips:
                plan.append((src.at[mine], land.at[(2 * px + py,) + mine], (px, py, c)))
            plan.append((src, land.at[j_me], (x, y, 1 - c)))
        return plan
    return copies


def _pass_plan(blocks, arrivals):
    def copies(src_refs, land_refs):
        x, y, c, chips = _place()
        plan = []
        for li, g, axis in blocks:
            land = src_refs[li].at[g]
            half = _half(land.shape[1:], (1 - c) if arrivals else c, axis)
            for px, py in chips:
                part = land.at[(2 * px + py,) + half]
                plan.append((part, part, (x, y, 1 - c)))
        return plan
    return copies


def gather_pass(name, lands, blocks):
    n = len(lands)

    def body(*refs):
        outs = refs[n:2 * n]
        send_sems, recv_sems = refs[2 * n:]
        x, y, c, chips = _place()
        sent = []
        for i, (li, g, axis) in enumerate(blocks):
            land = outs[li].at[g]
            mine = _half(land.shape[1:], c, axis)
            for k, (px, py) in enumerate(chips):
                part = land.at[(2 * px + py,) + mine]
                cp = _remote(part, part, send_sems, recv_sems, 3 * i + k, (x, y, 1 - c))
                cp.start()
                sent.append(cp)
        for i, (li, g, axis) in enumerate(blocks):
            land = outs[li].at[g]
            other = _half(land.shape[1:], 1 - c, axis)
            for k, (px, py) in enumerate(chips):
                part = land.at[(2 * px + py,) + other]
                _remote(part, part, send_sems, recv_sems, 3 * i + k, (x, y, 1 - c)).wait_recv()
        for cp in sent:
            cp.wait_send()

    m = 3 * len(blocks)
    return pl.pallas_call(
        body, name=name, in_specs=[ANY_SPEC] * n, out_specs=[ANY_SPEC] * n,
        out_shape=[jax.ShapeDtypeStruct(a.shape, a.dtype) for a in lands],
        input_output_aliases={i: i for i in range(n)},
        scratch_shapes=[pltpu.SemaphoreType.DMA((m,)), pltpu.SemaphoreType.DMA((m,))],
    )(*lands)


def _pair_plan(axes):
    def copies(src_refs, land_refs):
        x, y, c, _ = _place()
        return [(src_refs[i].at[(slice(None), slice(None)) + _half(src_refs[i].shape, 1 - c, a)], land_refs[i],
                 (x, y, 1 - c)) for i, a in enumerate(axes)]
    return copies


def _scatter_plan(n):
    def copies(src_refs, land_refs):
        x, y, c, chips = _place()
        return [(src_refs[i].at[:, 2 * px + py], land_refs[i].at[k], (px, py, c))
                for i in range(n) for k, (px, py) in enumerate(chips)]
    return copies


def _broadcast_plan(axes, arrivals):
    def copies(src_refs, land_refs):
        x, y, c, _ = _place()
        plan = []
        for i, a in enumerate(axes):
            part = src_refs[i].at[(slice(None),) + _half(src_refs[i].shape, (1 - c) if arrivals else c, a)]
            plan.append((part, part, (x, y, 1 - c)))
        return plan
    return copies


N_DEV = 8
SLAB_ROWS = 16


def allreduce_small(slab, after):
    def body(x_ref, after_ref, o_ref, land, send_sems, recv_sems):
        x, y, c, _ = _place()
        me = 4 * x + 2 * y + c
        land[me] = x_ref[...]
        copies = []
        for d in range(1, N_DEV):
            to = (me + d) % N_DEV
            cp = pltpu.make_async_remote_copy(
                src_ref=x_ref, dst_ref=land.at[me], send_sem=send_sems.at[d - 1], recv_sem=recv_sems.at[me],
                device_id=(to // 4, (to // 2) % 2, to % 2), device_id_type=MESH)
            cp.start()
            copies.append(cp)
        for d in range(1, N_DEV):
            frm = (me + d) % N_DEV
            pltpu.make_async_remote_copy(
                src_ref=x_ref, dst_ref=land.at[frm], send_sem=send_sems.at[d - 1], recv_sem=recv_sems.at[frm],
                device_id=(frm // 4, (frm // 2) % 2, frm % 2), device_id_type=MESH).wait_recv()
        for cp in copies:
            cp.wait_send()
        acc = land[0]
        for s in range(1, N_DEV):
            acc = acc + land[s]
        o_ref[...] = acc

    vm = pl.BlockSpec(memory_space=pltpu.VMEM)
    return pl.pallas_call(
        body, name="allreduce_small", in_specs=[vm, ANY_SPEC], out_specs=vm,
        out_shape=jax.ShapeDtypeStruct(slab.shape, F32),
        scratch_shapes=[pltpu.VMEM((N_DEV,) + slab.shape, F32), pltpu.SemaphoreType.DMA((N_DEV - 1,)),
                        pltpu.SemaphoreType.DMA((N_DEV,))],
    )(slab, after)


BLOCK_BYTES = 3 * 1024 * 1024


def _tiles_2d(r, c, budget=BLOCK_BYTES):
    if r % 8 == 0:
        tc = c if c % LANES else _pick(c, (2048, 1408, 1024, 512, 256, 128))
        tr = 8
        for cand in (512, 256, 128, 64, 32, 16, 8):
            if r % cand == 0 and cand * tc * 4 <= budget:
                tr = cand
                break
        if tr >= 64 or c % LANES or r * LANES * 4 > budget:
            return tr, tc
    tc = LANES
    for cand in (1024, 512, 256, 128):
        if c % cand == 0 and r * cand * 4 <= budget:
            tc = cand
            break
    return r, tc


def _grid_spec(grid, in_specs, out_specs):
    return pltpu.PrefetchScalarGridSpec(num_scalar_prefetch=1, grid=grid, in_specs=in_specs, out_specs=out_specs)


def _own(axis, nr, nc):
    if axis == 0:
        return lambda i, j, where: (where[1] * nr + i, j)
    return lambda i, j, where: (i, where[1] * nc + j)


def pair_add(where, grad, recv, axis, name):
    g, jn, hr, hc = recv.shape
    tr, tc = _tiles_2d(hr, hc)
    nr, nc = hr // tr, hc // tc
    own = _own(axis, nr, nc)
    others = jn - 1

    def body(where_ref, a_ref, b_ref, o_ref):
        o_ref[...] = (a_ref[...] + b_ref[...]).astype(BF16)

    def block(a, where):
        return a // others, (where[0] + 1 + a % others) % jn

    blk = pl.BlockSpec((None, None, tr, tc), lambda a, i, j, where: block(a, where) + (i, j))
    mine = pl.BlockSpec((None, None, tr, tc), lambda a, i, j, where: block(a, where) + own(i, j, where))
    return pl.pallas_call(
        body, name=name, grid_spec=_grid_spec((g * others, nr, nc), [mine, blk], blk),
        out_shape=jax.ShapeDtypeStruct(recv.shape, BF16),
        compiler_params=_params(("parallel", "parallel", "parallel")),
    )(where, grad, recv)


def chip_add(where, grad, pair, recv, axis, name):
    g, jn, hr, hc = pair.shape
    tr, tc = _tiles_2d(hr, hc)
    nr, nc = hr // tr, hc // tc
    own = _own(axis, nr, nc)
    full = (g, 2 * hr, hc) if axis == 0 else (g, hr, 2 * hc)

    def body(where_ref, a_ref, p_ref, b_ref, o_ref):
        s = a_ref[...] + p_ref[...]
        for k in range(3):
            s = s + b_ref[k].astype(F32)
        o_ref[...] = s

    return pl.pallas_call(
        body, name=name,
        grid_spec=_grid_spec((g, nr, nc),
                             [pl.BlockSpec((None, None, tr, tc), lambda a, i, j, where: (a, where[0]) + own(i, j, where)),
                              pl.BlockSpec((None, None, tr, tc), lambda a, i, j, where: (a, where[0], i, j)),
                              pl.BlockSpec((3, None, tr, tc), lambda a, i, j, where: (0, a, i, j))],
                             pl.BlockSpec((None, tr, tc), lambda a, i, j, where: (a,) + own(i, j, where))),
        out_shape=jax.ShapeDtypeStruct(full, F32), compiler_params=_params(("parallel", "parallel", "parallel")),
    )(where, grad, pair, recv)


def _adam_math(w, g, m, v):
    m2 = ADAM_B1 * m + (1.0 - ADAM_B1) * g
    v2 = ADAM_B2 * v + (1.0 - ADAM_B2) * (g * g)
    m_hat = m2 / (1.0 - ADAM_B1 ** ADAM_STEP)
    v_hat = v2 / (1.0 - ADAM_B2 ** ADAM_STEP)
    delta = -ADAM_LR * (m_hat / (jnp.sqrt(v_hat) + ADAM_EPS) + ADAM_WD * w)
    return delta, m2, v2


def adamw(grad, idx, w, m, v, name):
    _, r, cc = w.shape
    rg = grad.shape[1]
    tr, tc = _tiles_2d(r, cc, BLOCK_BYTES // 2)
    assert rg == r or tr == r
    gr = tr if rg == r else rg

    def body(g_ref, w_ref, m_ref, v_ref, go_ref, d_ref, mo_ref, vo_ref):
        g = g_ref[pl.ds(0, tr), :]
        delta, m2, v2 = _adam_math(w_ref[...], g, m_ref[...], v_ref[...])
        go_ref[...] = g
        d_ref[...] = delta
        mo_ref[...] = m2
        vo_ref[...] = v2

    blk = pl.BlockSpec((None, tr, tc), lambda i, j: (0, i, j))
    return pl.pallas_call(
        body, name=name, grid=(r // tr, cc // tc),
        in_specs=[pl.BlockSpec((None, gr, tc), lambda i, j: (idx, i, j)), blk, blk, blk], out_specs=[blk] * 4,
        out_shape=[jax.ShapeDtypeStruct(w.shape, F32)] * 4, compiler_params=_params(("parallel", "parallel")),
    )(grad, w, m, v)


def adamw_small(g, w, m, v):
    def body(g_ref, w_ref, m_ref, v_ref, d_ref, mo_ref, vo_ref):
        delta, m2, v2 = _adam_math(w_ref[...], g_ref[...], m_ref[...], v_ref[...])
        d_ref[...] = delta
        mo_ref[...] = m2
        vo_ref[...] = v2

    return pl.pallas_call(body, name="adamw_small", out_shape=[jax.ShapeDtypeStruct(w.shape, F32)] * 3)(g, w, m, v)


GAINS = ("ffn1_pre", "ffn1_post", "mix_pre", "mix_post", "ffn2_pre", "ffn2_post", "ple_pre", "ple_post")
WEIGHTS = ("ffn1_pre_g", "ffn1_post_g", "ffn1_w_gate", "ffn1_w_up", "ffn1_w_down", "mix_pre_g", "mix_post_g",
           "mix_w_in", "fox_f_bias", "hgrn_lb_logits", "hgrn_norm_g", "mix_w_proj_fox", "mix_w_proj_hgrn",
           "mix_w_out", "ffn2_pre_g", "ffn2_post_g", "ffn2_w_gate", "ffn2_w_up", "ffn2_w_down", "ple_pre_g",
           "ple_post_g", "ple_w_gate", "ple_w_proj")
GROUPS = dict(gu1=(("ffn1_w_gate", "ffn1_w_up"), 0), down1=(("ffn1_w_down",), 0), win=(("mix_w_in",), 1),
              proj=(("mix_w_proj_fox", "mix_w_proj_hgrn"), 0), out=(("mix_w_out",), 0),
              gu2=(("ffn2_w_gate", "ffn2_w_up"), 0), down2=(("ffn2_w_down",), 0), ple_gate=(("ple_w_gate",), 0),
              ple_proj=(("ple_w_proj",), 0))
TRANSPOSED = ("mix_w_in",)
ROW_BLOCKS = ("down1", "down2", "out", "ple_gate")
GATHER_SETS = (("gate1",), ("up1",), ("down1",), ("win",), ("proj", "out"), ("gu2", "down2", "ple_gate", "ple_proj"))
GATHER_GROUPS = dict(GROUPS, gate1=(("ffn1_w_gate",), 0), up1=(("ffn1_w_up",), 0))
GATE_UP_KEYS = dict(gu1=("gate1", "up1"), gu2=("gu2",))
REDUCE_SETS = (("ple_gate", "ple_proj", "down2", "gu2"), ("out", "proj", "win"), ("down1",), ("gu1",))


def _pad_row(a, width):
    a = a.reshape(1, -1)
    return jnp.pad(a, ((0, 0), (0, width - a.shape[1])))


def _pack_small(vals):
    d = D_MODEL
    rows = [vals[n + "_g"].reshape(1, d) for n in GAINS]
    rows.append(_pad_row(vals["fox_f_bias"], d))
    lg = vals["hgrn_lb_logits"]
    rows += [_pad_row(lg[0], d), _pad_row(lg[1], d), _pad_row(vals["hgrn_norm_g"], d)]
    slab = jnp.concatenate(rows, axis=0)
    return jnp.pad(slab, ((0, SLAB_ROWS - slab.shape[0]), (0, 0)))


def _unpack_small(slab):
    out = {n + "_g": slab[i:i + 1] for i, n in enumerate(GAINS)}
    out["fox_f_bias"] = slab[8:9, :HEADS]
    out["hgrn_lb_logits"] = slab[9:11, :WIDTH]
    out["hgrn_norm_g"] = slab[11:12, :HEAD_DIM]
    return out


def _split_in(win_t):
    lo = 3 * WIDTH
    main = jnp.concatenate([win_t[:lo], win_t[lo + HEADS:]], axis=0)
    fa = jnp.pad(win_t[lo:lo + HEADS], ((0, LANES - HEADS), (0, 0)))
    return main, fa


def _join_in(main, fa):
    lo = 3 * WIDTH
    return jnp.concatenate([main[:lo], fa[:HEADS], main[lo:]], axis=0)


def _as_block(name, a):
    return jnp.swapaxes(a, 1, 2) if name in TRANSPOSED else a


def _send_block(name, a, mark):
    blk = _as_block(name, a)[0]
    if mark is not None:
        blk = blk + mark[0, 0]
    return blk.astype(BF16)


def kernel(x, p, ffn1_pre_g, ffn1_post_g, ffn1_w_gate, ffn1_w_up, ffn1_w_down, mix_pre_g, mix_post_g, mix_w_in, fox_f_bias, hgrn_lb_logits, hgrn_norm_g, mix_w_proj_fox, mix_w_proj_hgrn, mix_w_out, ffn2_pre_g, ffn2_post_g, ffn2_w_gate, ffn2_w_up, ffn2_w_down, ple_pre_g, ple_post_g, ple_w_gate, ple_w_proj, loss_target, m_ffn1_pre_g, m_ffn1_post_g, m_ffn1_w_gate, m_ffn1_w_up, m_ffn1_w_down, m_mix_pre_g, m_mix_post_g, m_mix_w_in, m_fox_f_bias, m_hgrn_lb_logits, m_hgrn_norm_g, m_mix_w_proj_fox, m_mix_w_proj_hgrn, m_mix_w_out, m_ffn2_pre_g, m_ffn2_post_g, m_ffn2_w_gate, m_ffn2_w_up, m_ffn2_w_down, m_ple_pre_g, m_ple_post_g, m_ple_w_gate, m_ple_w_proj, v_ffn1_pre_g, v_ffn1_post_g, v_ffn1_w_gate, v_ffn1_w_up, v_ffn1_w_down, v_mix_pre_g, v_mix_post_g, v_mix_w_in, v_fox_f_bias, v_hgrn_lb_logits, v_hgrn_norm_g, v_mix_w_proj_fox, v_mix_w_proj_hgrn, v_mix_w_out, v_ffn2_pre_g, v_ffn2_post_g, v_ffn2_w_gate, v_ffn2_w_up, v_ffn2_w_down, v_ple_pre_g, v_ple_post_g, v_ple_w_gate, v_ple_w_proj):
    args = dict(locals())
    wts = {n: args[n] for n in WEIGHTS}
    mom = {n: args["m_" + n] for n in WEIGHTS}
    var = {n: args["v_" + n] for n in WEIGHTS}
    d = D_MODEL
    where = jnp.stack([2 * lax.axis_index("x") + lax.axis_index("y"), lax.axis_index("c")]).astype(jnp.int32)

    def start_sets(name, which, mark):
        srcs, lands, plans = [], [], []
        for si in which:
            blocks = []
            for g in GATHER_SETS[si]:
                names, axis = GATHER_GROUPS[g]
                for pos, n in enumerate(names):
                    blocks.append((len(srcs), len(lands), pos, axis))
                    srcs.append(_send_block(n, wts[n], mark))
                lands.append(lax.empty((len(names), N_CHIPS) + srcs[-1].shape, BF16))
            plans.append(blocks)
        sems, srcs, lands, mark = split_start(name, srcs, lands, [4 * len(b) for b in plans],
                                              lambda sr, lr: [_gather_plan(b)(sr, lr) for b in plans])
        out = {}
        for k, si in enumerate(which):
            s_idx = sorted({b[0] for b in plans[k]})
            l_idx = sorted({b[1] for b in plans[k]})
            local = [(s_idx.index(a), l_idx.index(b), pos, ax) for a, b, pos, ax in plans[k]]
            out[si] = (sems[k], [srcs[i] for i in s_idx], [lands[i] for i in l_idx], local)
        return out, mark

    flying, mark = start_sets("gather_start_0", [0], None)
    rest, all_started = start_sets("gather_start_1", list(range(1, len(GATHER_SETS))), mark)
    flying.update(rest)
    full, passing = {}, {}

    def set_of(key):
        g = "win" if key in ("in_main", "in_fa") else key
        return g, [g in s for s in GATHER_SETS].index(True)

    def arrive(si, after):
        sem, srcs, lands, local = flying.pop(si)
        _, got = split_wait("gather_wait_%d" % si, srcs, lands, sem, [after, all_started], _gather_arrivals(local))
        return got, [(b, pos, ax) for _, b, pos, ax in local]

    def early(key, after):
        g, si = set_of(key)
        if g in full or si not in flying:
            return None
        got, blocks = arrive(si, after)
        sem, got, _, mark = split_start("pass_start_%d" % si, got, [], [3 * len(blocks)],
                                        lambda sr, lr: [_pass_plan(blocks, False)(sr, lr)])
        passing[si] = (sem[0], got, blocks)
        return mark

    def land_set(si, after):
        if si in passing:
            sem, got, blocks = passing.pop(si)
            got, _ = split_wait("pass_wait_%d" % si, got, [], sem, [after], _pass_plan(blocks, True))
        else:
            got, blocks = arrive(si, after)
            got = gather_pass("gather_pass_%d" % si, got, blocks)
        for g, arr in zip(GATHER_SETS[si], got):
            if g == "win":
                full["win"] = arr
                full["in_main"], full["in_fa"] = _split_in(arr.reshape(-1, d))
            else:
                full[g] = arr.reshape(-1, d) if g in ROW_BLOCKS else arr

    def get_w(key, after=None):
        g, si = set_of(key)
        if g not in full:
            land_set(si, after)
        return full[key]

    grads, pairing, started = {}, [], {}
    rows4 = lambda a: a.reshape(1, N_CHIPS, a.shape[0] // N_CHIPS, a.shape[1])

    def emit(key, grad):
        if key in ("in_main", "in_fa"):
            grads[key] = grad
            if "in_main" not in grads or "in_fa" not in grads:
                return None
            key, grad = "win", rows4(_join_in(grads["in_main"], grads["in_fa"]))
        grads[key] = grad if grad.ndim == 4 else rows4(grad)
        for si, s in enumerate(REDUCE_SETS):
            if key in s and all(g in grads for g in s):
                axes = [GROUPS[g][1] for g in s]
                own = [grads[g] for g in s]
                zones = [lax.empty(_half_shape(a.shape, ax), F32) for a, ax in zip(own, axes)]
                plan = _pair_plan(axes)
                sem, own, zones, mark = split_start("pair_start_%d" % si, own, zones, [len(s)],
                                                    lambda sr, lr: [plan(sr, lr)])
                pairing.append((si, sem[0], own, zones, axes, plan))
                return mark
        return None

    def advance(value):
        mark = None
        while pairing:
            si, sem, own, zones, axes, plan = pairing.pop(0)
            s = REDUCE_SETS[si]
            own, recv = split_wait("pair_wait_%d" % si, own, zones, sem, [value], plan)
            parts = [pair_add(where, a, r, ax, "pair_add_" + g) for g, a, r, ax in zip(s, own, recv, axes)]
            zones = [lax.empty((3, q.shape[0]) + q.shape[2:], BF16) for q in parts]
            plan = _scatter_plan(len(s))
            sem, parts, zones, mark = split_start("scatter_start_%d" % si, parts, zones, [3 * len(s)],
                                                  lambda sr, lr: [plan(sr, lr)])
            started[si] = (sem[0], parts, zones, own, recv, axes, plan)
        return mark

    gains = {n: wts[n + "_g"] for n in GAINS}
    loss, dx, small = layer_step(x[0], p[0, 0].astype(BF16), loss_target[0], gains, _pad_row(fox_f_bias, LANES),
                                 hgrn_lb_logits, hgrn_norm_g, get_w, emit, advance, early)

    out_g, out_d, out_m, out_v = {}, {}, {}, {}
    after, crossing = None, []

    def finish(si, sem, halves, axes, mark):
        reduced, _ = split_wait("broadcast_wait_%d" % si, halves, [], sem, [mark], _broadcast_plan(axes, True))
        last = None
        for g, red in zip(REDUCE_SETS[si], reduced):
            for idx, n in enumerate(GROUPS[g][0]):
                res = adamw(red, idx, _as_block(n, wts[n]), _as_block(n, mom[n]), _as_block(n, var[n]), "adamw_" + n)
                out_g[n], out_d[n], out_m[n], out_v[n] = [_as_block(n, r) for r in res]
                last = res[1]
        return last

    last_set, mark = len(REDUCE_SETS) - 1, None
    for si, s in enumerate(REDUCE_SETS):
        if si == last_set:
            while len(crossing) > 1:
                after = finish(*crossing.pop(0), mark)
                mark = after
        sem, parts, zones, own, recv, axes, plan = started[si]
        _, zones = split_wait("scatter_wait_%d" % si, parts, zones, sem, [dx, after], plan)
        halves = [chip_add(where, a, r, z, ax, "chip_add_" + g) for g, a, r, z, ax in zip(s, own, recv, zones, axes)]
        plan = _broadcast_plan(axes, False)
        sem, halves, _, mark = split_start("broadcast_start_%d" % si, halves, [], [len(s)],
                                           lambda sr, lr: [plan(sr, lr)])
        crossing.append((si, sem[0], halves, axes))
        if si < last_set and len(crossing) > 2:
            after = finish(*crossing.pop(0), mark)
    while crossing:
        after = finish(*crossing.pop(0), mark)
        mark = after

    small_named = {n + "_g": small[n] for n in GAINS}
    small_named.update(fox_f_bias=small["fox_bias"][:, :HEADS], hgrn_lb_logits=small["lb_logits"],
                       hgrn_norm_g=small["norm_g"])
    g_small = allreduce_small(_pack_small(small_named), after)
    d_small, m_small, v_small = adamw_small(g_small, _pack_small(wts), _pack_small(mom), _pack_small(var))

    for dst, slab in ((out_g, g_small), (out_d, d_small), (out_m, m_small), (out_v, v_small)):
        dst.update(_unpack_small(slab))

    total = lax.psum(loss[0, 0], ("x", "y", "c"))
    return (total, dx[None], *[out_g[n] for n in WEIGHTS], *[out_d[n] for n in WEIGHTS],
            *[out_m[n] for n in WEIGHTS], *[out_v[n] for n in WEIGHTS])
```

```python
import functools

import jax
import jax.numpy as jnp
from jax import lax
from jax.experimental import pallas as pl
from jax.experimental.pallas import tpu as pltpu

F32 = jnp.float32
BF16 = jnp.bfloat16

D_MODEL = 2048
SEQ = 2048
D_FF = 5632
PLE_DIM = 256
HEADS = 8
HEAD_DIM = 128
WIDTH = HEADS * HEAD_DIM
CHUNK = 64
SUB = 16
HGRN_HEADS_PER_STEP = 4
NORM_EPS = 1e-6
MACARON_SCALE = 0.5
N_CHIPS = 4

ADAM_LR = 0.001
ADAM_B1 = 0.9
ADAM_B2 = 0.999
ADAM_EPS = 1e-08
ADAM_WD = 0.01
ADAM_STEP = 10

LANES = 128
VMEM_LIMIT = 56 * 1024 * 1024
NEG_BIG = -1e30
MESH = pl.DeviceIdType.MESH


def _pick(n, cands):
    for c in cands:
        if c <= n and n % c == 0:
            return c
    return n


def _params(sem, vmem=VMEM_LIMIT):
    return pltpu.CompilerParams(dimension_semantics=sem, vmem_limit_bytes=vmem)


def _sigmoid(x):
    return 1.0 / (1.0 + jnp.exp(-x))


def _silu(x):
    return x * _sigmoid(x)


def _silu_grad(x):
    s = _sigmoid(x)
    return s * (1.0 + x * (1.0 - s))


_DN = {"nn": (((1,), (0,)), ((), ())), "nt": (((1,), (1,)), ((), ())), "tn": (((0,), (0,)), ((), ()))}


def _mm(a, b, *, mode, grid, a_spec, b_spec, o_spec, out_shape, acc_shape, name, after=(), init=None, into=None):
    nk = grid[2]
    dn = _DN[mode]
    after = [m for m in after if m is not None]
    extra = ([init] if init is not None else []) + ([into] if into is not None else []) + after
    n_extra = len(extra)

    def body(a_ref, b_ref, *rest):
        o_ref, acc_ref = rest[n_extra:]
        k = pl.program_id(2)

        @pl.when(k == 0)
        def _():
            acc_ref[...] = jnp.zeros_like(acc_ref) if init is None else rest[0][...].astype(F32)

        acc_ref[...] += lax.dot_general(a_ref[...].astype(BF16), b_ref[...].astype(BF16), dn,
                                        preferred_element_type=F32)

        @pl.when(k == nk - 1)
        def _():
            o_ref[...] = acc_ref[...].astype(o_ref.dtype)

    anywhere = pl.BlockSpec(memory_space=pl.ANY)
    return pl.pallas_call(
        body, name=name, grid=grid,
        in_specs=[a_spec, b_spec] + ([o_spec] if init is not None else []) + [anywhere] * (n_extra - (init is not None)),
        out_specs=o_spec, out_shape=out_shape, scratch_shapes=[pltpu.VMEM(acc_shape, F32)],
        input_output_aliases={} if into is None else {2 + (init is not None): 0},
        compiler_params=_params(("parallel", "parallel", "arbitrary")),
    )(a, b, *extra)


def mm_nn_2d(a, b, out_dtype, name, after=()):
    m, kk = a.shape
    n = b.shape[1]
    tm, tn = _pick(m, (512, 256, 128)), _pick(n, (1024, 512, 256, 128))
    tk = _pick(kk, (5632, 2816, 2048, 1408, 1024, 512, 256, 128))
    return _mm(a, b, mode="nn", grid=(m // tm, n // tn, kk // tk),
               a_spec=pl.BlockSpec((tm, tk), lambda i, j, k: (i, k)),
               b_spec=pl.BlockSpec((tk, tn), lambda i, j, k: (k, j)),
               o_spec=pl.BlockSpec((tm, tn), lambda i, j, k: (i, j)),
               out_shape=jax.ShapeDtypeStruct((m, n), out_dtype), acc_shape=(tm, tn), name=name, after=after)


def mm_nt_2d(a, b, out_dtype, name, after=()):
    m, c = a.shape
    n = b.shape[0]
    tm, tn, tk = _pick(m, (512, 256, 128)), _pick(n, (1408, 1024, 512, 256, 128)), _pick(c, (2048, 1408, 1024, 512, 256, 128))
    return _mm(a, b, mode="nt", grid=(m // tm, n // tn, c // tk),
               a_spec=pl.BlockSpec((tm, tk), lambda i, j, k: (i, k)),
               b_spec=pl.BlockSpec((tn, tk), lambda i, j, k: (j, k)),
               o_spec=pl.BlockSpec((tm, tn), lambda i, j, k: (i, j)),
               out_shape=jax.ShapeDtypeStruct((m, n), out_dtype), acc_shape=(tm, tn), name=name, after=after)


def mm_tn_2d(a, b, out_dtype, name):
    c, m = a.shape
    n = b.shape[1]
    tm, tn, tk = _pick(m, (1408, 1024, 512, 256, 128)), _pick(n, (1024, 512, 256, 128)), _pick(c, (2048, 1024, 512, 256, 128))
    return _mm(a, b, mode="tn", grid=(m // tm, n // tn, c // tk),
               a_spec=pl.BlockSpec((tk, tm), lambda i, j, k: (k, i)),
               b_spec=pl.BlockSpec((tk, tn), lambda i, j, k: (k, j)),
               o_spec=pl.BlockSpec((tm, tn), lambda i, j, k: (i, j)),
               out_shape=jax.ShapeDtypeStruct((m, n), out_dtype), acc_shape=(tm, tn), name=name)


def mm_nn_col(a, w, out_dtype, name, slot0=0, total=None, into=None):
    m, kk = a.shape
    g, jn, _, ns = w.shape
    total = g if total is None else total
    tm, tk = _pick(m, (512, 256, 128)), _pick(kk, (2048, 1024, 512, 256, 128))
    return _mm(a, w, mode="nn", grid=(m // tm, g * jn, kk // tk),
               a_spec=pl.BlockSpec((tm, tk), lambda i, j, k: (i, k)),
               b_spec=pl.BlockSpec((None, None, tk, ns), lambda i, j, k: (j // jn, j % jn, k, 0)),
               o_spec=pl.BlockSpec((None, tm, ns), lambda i, j, k: (slot0 + j // jn, i, j % jn)),
               out_shape=jax.ShapeDtypeStruct((total, m, jn * ns), out_dtype), acc_shape=(tm, ns), name=name,
               into=into)


def mm_nt_col(a, w, out_dtype, name, after=(), slot0=0, init=None):
    _, m, _ = a.shape
    g, jn, kk, ns = w.shape
    kc = 2 if jn % 2 == 0 else 1
    per = jn // kc
    nk = g * per
    tm, tn = _pick(m, (512, 256, 128)), _pick(kk, (1024, 512, 256, 128))
    extra = ([init] if init is not None else []) + [v for v in after if v is not None]

    def body(a_ref, w_ref, *rest):
        o_ref, acc_ref = rest[len(extra):]
        k = pl.program_id(2)
        part = None
        for c in range(kc):
            prod = lax.dot_general(a_ref[:, c * ns:(c + 1) * ns].astype(BF16), w_ref[c].astype(BF16), _DN["nt"],
                                   preferred_element_type=F32)
            part = prod if part is None else part + prod

        @pl.when(k == 0)
        def _():
            acc_ref[...] = part if init is None else rest[0][...].astype(F32) + part

        @pl.when(k > 0)
        def _():
            acc_ref[...] += part

        @pl.when(k == nk - 1)
        def _():
            o_ref[...] = acc_ref[...].astype(o_ref.dtype)

    o_spec = pl.BlockSpec((tm, tn), lambda i, j, k: (i, j))
    anywhere = pl.BlockSpec(memory_space=pl.ANY)
    return pl.pallas_call(
        body, name=name, grid=(m // tm, kk // tn, nk),
        in_specs=[pl.BlockSpec((None, tm, kc * ns), lambda i, j, k: (slot0 + k // per, i, k % per)),
                  pl.BlockSpec((None, kc, tn, ns), lambda i, j, k: (k // per, k % per, j, 0))]
        + ([o_spec] if init is not None else []) + [anywhere] * (len(extra) - (init is not None)),
        out_specs=o_spec, out_shape=jax.ShapeDtypeStruct((m, kk), out_dtype),
        scratch_shapes=[pltpu.VMEM((tm, tn), F32)],
        compiler_params=_params(("parallel", "parallel", "arbitrary")),
    )(a, w, *extra)


def mm_tn_col(a, b, jn, out_dtype, name):
    c, kk = a.shape
    g, _, n = b.shape
    ns = n // jn
    tm, tk = _pick(kk, (512, 256, 128)), _pick(c, (2048, 1024, 512, 256, 128))
    return _mm(a, b, mode="tn", grid=(kk // tm, g * jn, c // tk),
               a_spec=pl.BlockSpec((tk, tm), lambda i, j, k: (k, i)),
               b_spec=pl.BlockSpec((None, tk, ns), lambda i, j, k: (j // jn, k, j % jn)),
               o_spec=pl.BlockSpec((None, None, tm, ns), lambda i, j, k: (j // jn, j % jn, i, 0)),
               out_shape=jax.ShapeDtypeStruct((g, jn, kk, ns), out_dtype), acc_shape=(tm, ns), name=name)


def _rstd(x):
    return lax.rsqrt(jnp.mean(x * x, axis=-1, keepdims=True) + NORM_EPS)


def _rms_bwd(x, g, dy):
    r = _rstd(x)
    xn = x * r
    dyg = dy * g
    dx = r * (dyg - xn * jnp.mean(dyg * xn, axis=-1, keepdims=True))
    return dx, jnp.sum(dy * xn, axis=0, keepdims=True)


def _row_tile(t):
    return _pick(t, (256, 128, 64, 32, 16, 8))


def norm_in(h, g, name):
    t, d = h.shape
    tr = _row_tile(t)

    def body(h_ref, g_ref, u_ref):
        x = h_ref[...]
        u_ref[...] = (x * _rstd(x) * g_ref[...]).astype(BF16)

    return pl.pallas_call(
        body, name=name, grid=(t // tr,),
        in_specs=[pl.BlockSpec((tr, d), lambda i: (i, 0)), pl.BlockSpec((1, d), lambda i: (0, 0))],
        out_specs=pl.BlockSpec((tr, d), lambda i: (i, 0)),
        out_shape=jax.ShapeDtypeStruct((t, d), BF16), compiler_params=_params(("parallel",)),
    )(h, g)


def resid_norm(h, y, g, scale, g_next, name):
    t, d = h.shape
    tr = _row_tile(t)

    def body(h_ref, y_ref, g_ref, gn_ref, o_ref, u_ref):
        yv = y_ref[...]
        out = h_ref[...] + scale * (yv * _rstd(yv) * g_ref[...])
        o_ref[...] = out
        u_ref[...] = (out * _rstd(out) * gn_ref[...]).astype(BF16)

    row = pl.BlockSpec((tr, d), lambda i: (i, 0))
    vec = pl.BlockSpec((1, d), lambda i: (0, 0))
    return pl.pallas_call(
        body, name=name, grid=(t // tr,), in_specs=[row, row, vec, vec], out_specs=[row, row],
        out_shape=[jax.ShapeDtypeStruct((t, d), F32), jax.ShapeDtypeStruct((t, d), BF16)],
        compiler_params=_params(("parallel",)),
    )(h, y, g, g_next)


def post_bwd(dh, y, g, scale, name):
    t, d = dh.shape
    tr = _row_tile(t)

    def body(dh_ref, y_ref, g_ref, dy_ref, dg_ref):
        @pl.when(pl.program_id(0) == 0)
        def _():
            dg_ref[...] = jnp.zeros_like(dg_ref)

        dx, dg = _rms_bwd(y_ref[...], g_ref[...], scale * dh_ref[...])
        dy_ref[...] = dx.astype(BF16)
        dg_ref[...] += dg

    row = pl.BlockSpec((tr, d), lambda i: (i, 0))
    vec = pl.BlockSpec((1, d), lambda i: (0, 0))
    return pl.pallas_call(
        body, name=name, grid=(t // tr,), in_specs=[row, row, vec], out_specs=[row, vec],
        out_shape=[jax.ShapeDtypeStruct((t, d), BF16), jax.ShapeDtypeStruct((1, d), F32)],
        compiler_params=_params(("arbitrary",)),
    )(dh, y, g)


def pre_bwd(dh, h, g, dus, name, after=()):
    t, d = dh.shape
    tr = _row_tile(t)
    n_du = len(dus)
    after = [m for m in after if m is not None]

    def body(*refs):
        dh_ref, h_ref, g_ref = refs[:3]
        du_refs = refs[3:3 + n_du]
        o_ref, dg_ref = refs[3 + n_du + len(after):]

        @pl.when(pl.program_id(0) == 0)
        def _():
            dg_ref[...] = jnp.zeros_like(dg_ref)

        du = du_refs[0][...]
        for r in du_refs[1:]:
            du = du + r[...]
        dx, dg = _rms_bwd(h_ref[...], g_ref[...], du)
        o_ref[...] = dh_ref[...] + dx
        dg_ref[...] += dg

    row = pl.BlockSpec((tr, d), lambda i: (i, 0))
    vec = pl.BlockSpec((1, d), lambda i: (0, 0))
    return pl.pallas_call(
        body, name=name, grid=(t // tr,),
        in_specs=[row, row, vec] + [row] * n_du + [pl.BlockSpec(memory_space=pl.ANY)] * len(after),
        out_specs=[row, vec], out_shape=[jax.ShapeDtypeStruct((t, d), F32), jax.ShapeDtypeStruct((1, d), F32)],
        compiler_params=_params(("arbitrary",)),
    )(dh, h, g, *dus, *after)


def _ew_tiles(t, f):
    return _pick(t, (256, 128, 64, 32, 16, 8)), _pick(f, (1408, 1024, 512, 256, 128))


def swiglu_act(gu, name):
    _, t, f = gu.shape
    tr, tc = _ew_tiles(t, f)

    def body(gu_ref, o_ref):
        o_ref[...] = (_silu(gu_ref[0]) * gu_ref[1]).astype(BF16)

    return pl.pallas_call(
        body, name=name, grid=(t // tr, f // tc),
        in_specs=[pl.BlockSpec((2, tr, tc), lambda i, j: (0, i, j))],
        out_specs=pl.BlockSpec((tr, tc), lambda i, j: (i, j)),
        out_shape=jax.ShapeDtypeStruct((t, f), BF16), compiler_params=_params(("parallel", "parallel")),
    )(gu)


def swiglu_bwd(dact, gu, name):
    _, t, f = gu.shape
    tr, tc = _ew_tiles(t, f)

    def body(da_ref, gu_ref, o_ref):
        da = da_ref[...]
        gate = gu_ref[0]
        o_ref[0] = (da * gu_ref[1] * _silu_grad(gate)).astype(BF16)
        o_ref[1] = (da * _silu(gate)).astype(BF16)

    return pl.pallas_call(
        body, name=name, grid=(t // tr, f // tc),
        in_specs=[pl.BlockSpec((tr, tc), lambda i, j: (i, j)), pl.BlockSpec((2, tr, tc), lambda i, j: (0, i, j))],
        out_specs=pl.BlockSpec((2, tr, tc), lambda i, j: (0, i, j)),
        out_shape=jax.ShapeDtypeStruct((2, t, f), BF16), compiler_params=_params(("parallel", "parallel")),
    )(dact, gu)


def _col_blocks():
    w = WIDTH // LANES
    return dict(q_a=0, k_a=w, v_a=2 * w, q_b=3 * w, f_b=4 * w, i_b=5 * w, g_b=6 * w, gate_a=7 * w,
                gate_b=7 * w + D_MODEL // LANES)


def _tri(n, lower):
    r = lax.broadcasted_iota(jnp.int32, (n, n), 0)
    c = lax.broadcasted_iota(jnp.int32, (n, n), 1)
    return jnp.where((r >= c) if lower else (r <= c), 1.0, 0.0).astype(F32)


def _dot_hi(a, b):
    return jnp.dot(a, b, precision=lax.Precision.HIGHEST, preferred_element_type=F32)


def fox_prep(fa, bias, name):
    t, w = fa.shape
    tb = _pick(t, (256, 128, 64))

    def body(fa_ref, b_ref, c_ref, carry_ref):
        @pl.when(pl.program_id(0) == 0)
        def _():
            carry_ref[...] = jnp.zeros_like(carry_ref)

        z = fa_ref[...] + b_ref[...]
        lf = jnp.minimum(z, 0.0) - jnp.log(1.0 + jnp.exp(-jnp.abs(z)))
        c = _dot_hi(_tri(tb, True), lf) + carry_ref[...]
        c_ref[...] = c
        carry_ref[...] = carry_ref[...] + jnp.sum(lf, axis=0, keepdims=True)

    return pl.pallas_call(
        body, name=name, grid=(t // tb,),
        in_specs=[pl.BlockSpec((tb, w), lambda i: (i, 0)), pl.BlockSpec((1, w), lambda i: (0, 0))],
        out_specs=pl.BlockSpec((tb, w), lambda i: (i, 0)),
        out_shape=jax.ShapeDtypeStruct((t, w), F32), scratch_shapes=[pltpu.VMEM((1, w), F32)],
        compiler_params=_params(("arbitrary",)),
    )(fa, bias)


def fox_post_bwd(dc, fa, bias, name):
    t, w = fa.shape
    tb = _pick(t, (256, 128, 64))
    nb = t // tb

    def body(dc_ref, fa_ref, b_ref, dfa_ref, db_ref, carry_ref):
        @pl.when(pl.program_id(0) == 0)
        def _():
            carry_ref[...] = jnp.zeros_like(carry_ref)
            db_ref[...] = jnp.zeros_like(db_ref)

        dcv = dc_ref[...]
        dlf = _dot_hi(_tri(tb, False), dcv) + carry_ref[...]
        z = fa_ref[...] + b_ref[...]
        dz = dlf * _sigmoid(-z)
        dfa_ref[...] = dz.astype(BF16)
        db_ref[...] += jnp.sum(dz, axis=0, keepdims=True)
        carry_ref[...] = carry_ref[...] + jnp.sum(dcv, axis=0, keepdims=True)

    rev = pl.BlockSpec((tb, w), lambda i: (nb - 1 - i, 0))
    vec = pl.BlockSpec((1, w), lambda i: (0, 0))
    return pl.pallas_call(
        body, name=name, grid=(nb,), in_specs=[rev, rev, vec], out_specs=[rev, vec],
        out_shape=[jax.ShapeDtypeStruct((t, w), BF16), jax.ShapeDtypeStruct((1, w), F32)],
        scratch_shapes=[pltpu.VMEM((1, w), F32)], compiler_params=_params(("arbitrary",)),
    )(dc, fa, bias)


def _fox_probs(q_ref, k_ref, cc_ref, cr_ref, qi, tq, t):
    scale = HEAD_DIM ** -0.5
    s = lax.dot_general(q_ref[...].astype(BF16), k_ref[...].astype(BF16), _DN["nt"], preferred_element_type=F32)
    logits = s * scale + cc_ref[...] - cr_ref[...]
    qpos = qi * tq + lax.broadcasted_iota(jnp.int32, (tq, t), 0)
    kpos = lax.broadcasted_iota(jnp.int32, (tq, t), 1)
    logits = jnp.where(kpos <= qpos, logits, NEG_BIG)
    m = jnp.max(logits, axis=-1, keepdims=True)
    p = jnp.exp(logits - m)
    return p / jnp.sum(p, axis=-1, keepdims=True)


FOX_SEGMENTS = 4


def _fox_segments(t):
    tq = _pick(t, (256, 128))
    nseg = min(FOX_SEGMENTS, t // tq)
    return tq, nseg, t // tq // nseg


def _fox_specs(t, q0, kt, tq):
    cb = _col_blocks()
    dh = HEAD_DIM
    return [pl.BlockSpec((tq, dh), lambda h, i: (q0 + i, cb["q_a"] + h)),
            pl.BlockSpec((kt, dh), lambda h, i: (0, cb["k_a"] + h)),
            pl.BlockSpec((kt, dh), lambda h, i: (0, cb["v_a"] + h)),
            pl.BlockSpec((None, tq, 1), lambda h, i: (h, q0 + i, 0)),
            pl.BlockSpec((None, 1, kt), lambda h, i: (h, 0, 0))]


def fox_fwd(proj, c_col, c_row, name):
    t = proj.shape[0]
    tq, nseg, nq = _fox_segments(t)
    dh = HEAD_DIM

    def segment(out, r):
        q0, kt = r * nq, (r + 1) * nq * tq

        def body(q_ref, k_ref, v_ref, cc_ref, cr_ref, prev_ref, o_ref):
            p = _fox_probs(q_ref, k_ref, cc_ref, cr_ref, q0 + pl.program_id(1), tq, kt)
            o_ref[...] = jnp.dot(p.astype(BF16), v_ref[...].astype(BF16), preferred_element_type=F32).astype(BF16)

        return pl.pallas_call(
            body, name="%s_%d" % (name, r), grid=(HEADS, nq),
            in_specs=_fox_specs(t, q0, kt, tq) + [pl.BlockSpec(memory_space=pl.ANY)],
            out_specs=pl.BlockSpec((tq, dh), lambda h, i: (q0 + i, h)),
            out_shape=jax.ShapeDtypeStruct((t, WIDTH), BF16), input_output_aliases={5: 0},
            compiler_params=_params(("parallel", "parallel")),
        )(proj, proj, proj, c_col, c_row, out)

    out = lax.empty((t, WIDTH), BF16)
    for r in range(nseg):
        out = segment(out, r)
    return out


def fox_bwd(proj, c_col, c_row, do_ab, name):
    t = proj.shape[0]
    tq, nseg, nq = _fox_segments(t)
    dh = HEAD_DIM
    scale = HEAD_DIM ** -0.5

    def segment(acc, r):
        q0, kt = r * nq, (r + 1) * nq * tq

        def body(q_ref, k_ref, v_ref, cc_ref, cr_ref, do_ref, dqp_ref, dkp_ref, dvp_ref, dccp_ref, dcrp_ref,
                 dq_ref, dk_ref, dv_ref, dcc_ref, dcr_ref):
            @pl.when(pl.program_id(1) == 0)
            def _():
                dk_ref[...] = dkp_ref[...]
                dv_ref[...] = dvp_ref[...]
                dcr_ref[...] = dcrp_ref[...]

            p = _fox_probs(q_ref, k_ref, cc_ref, cr_ref, q0 + pl.program_id(1), tq, kt)
            dov = do_ref[...].astype(BF16)
            kb = k_ref[...].astype(BF16)
            dv_ref[...] += lax.dot_general(p.astype(BF16), dov, _DN["tn"], preferred_element_type=F32)
            dp = lax.dot_general(dov, v_ref[...].astype(BF16), _DN["nt"], preferred_element_type=F32)
            ds = p * (dp - jnp.sum(p * dp, axis=-1, keepdims=True))
            dcc_ref[...] = jnp.sum(ds, axis=-1, keepdims=True)
            dcr_ref[...] -= jnp.sum(ds, axis=0, keepdims=True)
            dss = (ds * scale).astype(BF16)
            dq_ref[...] = jnp.dot(dss, kb, preferred_element_type=F32).astype(BF16)
            dk_ref[...] += lax.dot_general(dss, q_ref[...].astype(BF16), _DN["tn"], preferred_element_type=F32)

        rows = pl.BlockSpec((tq, dh), lambda h, i: (q0 + i, h))
        keys = pl.BlockSpec((kt, dh), lambda h, i: (0, h))
        col = pl.BlockSpec((None, tq, 1), lambda h, i: (h, q0 + i, 0))
        row = pl.BlockSpec((None, 1, kt), lambda h, i: (h, 0, 0))
        anywhere = pl.BlockSpec(memory_space=pl.ANY)
        return pl.pallas_call(
            body, name="%s_%d" % (name, r), grid=(HEADS, nq),
            in_specs=_fox_specs(t, q0, kt, tq)
            + [pl.BlockSpec((None, tq, dh), lambda h, i: (0, q0 + i, h)), anywhere, keys, keys, anywhere, row],
            out_specs=[rows, keys, keys, col, row],
            out_shape=[jax.ShapeDtypeStruct(a.shape, a.dtype) for a in acc],
            input_output_aliases={6 + k: k for k in range(5)},
            compiler_params=_params(("parallel", "arbitrary")),
        )(proj, proj, proj, c_col, c_row, do_ab, *acc)

    acc = [lax.empty((t, WIDTH), BF16), jnp.zeros((t, WIDTH), F32), jnp.zeros((t, WIDTH), F32),
           lax.empty((HEADS, t, 1), F32), jnp.zeros((HEADS, 1, t), F32)]
    for r in range(nseg):
        acc = segment(acc, r)
    return acc


def _lower_bound(lg_ref):
    l0 = lg_ref[0:1, :]
    l1 = lg_ref[1:2, :]
    m = jnp.maximum(l0, l1)
    e0 = jnp.exp(l0 - m)
    e1 = jnp.exp(l1 - m)
    return e0 / (e0 + e1)


def _hgrn_inputs(qb_ref, fb_ref, lg_ref, q_s, k_s, cum_s):
    lb = _lower_bound(lg_ref)
    sig = _sigmoid(fb_ref[...])
    f = lb + (1.0 - lb) * sig
    q_s[...] = _silu(qb_ref[...])
    k_s[...] = 1.0 - f
    cum_s[...] = _dot_hi(_tri(CHUNK, True), jnp.log(f))
    return lb, sig, f


def _boundary(cum_s, a):
    if a == 0:
        return jnp.zeros((1, HEAD_DIM), F32)
    return cum_s[pl.ds(SUB * a - 1, 1), :]


def _hgrn_scores(q_s, k_s, cum_s):
    cum = cum_s[...]
    kk = k_s[...]
    lane = lax.broadcasted_iota(jnp.int32, (SUB, CHUNK), 1)
    row = lax.broadcasted_iota(jnp.int32, (SUB, 1), 0)
    blocks = []
    for a in range(CHUNK // SUB):
        rows = pl.ds(SUB * a, SUB)
        ca = _boundary(cum_s, a)
        cum_a = cum_s[rows, :]
        q_a = q_s[rows, :]
        qa = q_a * jnp.exp(cum_a - ca)
        ka = kk * jnp.exp(jnp.minimum(ca - cum, 0.0))
        blk = lax.dot_general(qa, ka, _DN["nt"], preferred_element_type=F32)
        blk = jnp.where(lane < SUB * a, blk, 0.0)
        for s in range(SUB):
            r = SUB * a + s
            e = jnp.exp(jnp.minimum(cum_a - cum_s[pl.ds(r, 1), :], 0.0))
            col = jnp.sum(q_a * k_s[pl.ds(r, 1), :] * e, axis=-1, keepdims=True)
            col = jnp.where(row >= s, col, 0.0)
            blk = jnp.where(lane == r, col, blk)
        blocks.append(blk)
    return jnp.concatenate(blocks, axis=0)


def hgrn_fwd(proj, lb_logits, name, after=()):
    after = [m for m in after if m is not None]
    t = proj.shape[0]
    n = t // CHUNK
    cb = _col_blocks()
    dh = HEAD_DIM

    hb = min(HGRN_HEADS_PER_STEP, HEADS)
    w = hb * dh

    def one_head(qb_ref, fb_ref, ib_ref, lg_ref, o_ref, st_ref, a_ref, state, q_s, k_s, cum_s):
        _hgrn_inputs(qb_ref, fb_ref, lg_ref, q_s, k_s, cum_s)
        st = state[...]
        st_ref[...] = st
        cum = cum_s[...]
        v = ib_ref[...]
        qe = q_s[...] * jnp.exp(cum)
        inter = lax.dot_general(qe, st, _DN["nt"], preferred_element_type=F32)
        a_mat = _hgrn_scores(q_s, k_s, cum_s)
        a_ref[...] = a_mat
        o_ref[...] = inter + jnp.dot(a_mat, v, preferred_element_type=F32)
        last = cum_s[pl.ds(CHUNK - 1, 1), :]
        kd = k_s[...] * jnp.exp(last - cum)
        state[...] = st * jnp.exp(last) + lax.dot_general(v, kd, _DN["tn"], preferred_element_type=F32)

    def body(qb_ref, fb_ref, ib_ref, lg_ref, *rest):
        o_ref, st_ref, a_ref = rest[len(after):len(after) + 3]
        scratch = rest[len(after) + 3:]

        @pl.when(pl.program_id(1) == 0)
        def _():
            for j in range(hb):
                scratch[4 * j][...] = jnp.zeros((dh, dh), F32)

        for j in range(hb):
            cols = (slice(None), pl.ds(j * dh, dh))
            one_head(qb_ref.at[cols], fb_ref.at[cols], ib_ref.at[cols], lg_ref.at[cols], o_ref.at[cols],
                     st_ref.at[j], a_ref.at[j], *scratch[4 * j:4 * j + 4])

    blk = lambda off: pl.BlockSpec((CHUNK, w), lambda h, i: (i, off // hb + h))
    return pl.pallas_call(
        body, name=name, grid=(HEADS // hb, n),
        in_specs=[blk(cb["q_b"]), blk(cb["f_b"]), blk(cb["i_b"]), pl.BlockSpec((2, w), lambda h, i: (0, h))]
        + [pl.BlockSpec(memory_space=pl.ANY)] * len(after),
        out_specs=[pl.BlockSpec((CHUNK, w), lambda h, i: (i, h)),
                   pl.BlockSpec((hb, None, dh, dh), lambda h, i: (h, i, 0, 0)),
                   pl.BlockSpec((hb, None, CHUNK, CHUNK), lambda h, i: (h, i, 0, 0))],
        out_shape=[jax.ShapeDtypeStruct((t, WIDTH), F32), jax.ShapeDtypeStruct((HEADS, n, dh, dh), F32),
                   jax.ShapeDtypeStruct((HEADS, n, CHUNK, CHUNK), F32)],
        scratch_shapes=([pltpu.VMEM((dh, dh), F32)] + [pltpu.VMEM((CHUNK, dh), F32)] * 3) * hb,
        compiler_params=_params(("parallel", "arbitrary")),
    )(proj, proj, proj, lb_logits, *after)


def hgrn_bwd(proj, lb_logits, states, scores, do, name):
    t = proj.shape[0]
    n = t // CHUNK
    cb = _col_blocks()
    dh = HEAD_DIM
    nsub = CHUNK // SUB

    hb = min(HGRN_HEADS_PER_STEP, HEADS)
    w = hb * dh

    def one_head(qb_ref, fb_ref, ib_ref, lg_ref, st_ref, a_ref, do_ref, dqb_ref, dfb_ref, dib_ref, dlb_ref,
                 dstate, q_s, k_s, cum_s, da_s, dq_s, dk_s):
        lb, sig, f = _hgrn_inputs(qb_ref, fb_ref, lg_ref, q_s, k_s, cum_s)
        st = st_ref[...]
        dst = dstate[...]
        cum = cum_s[...]
        q = q_s[...]
        kk = k_s[...]
        v = ib_ref[...]
        dov = do_ref[...]
        e_cum = jnp.exp(cum)
        qe = q * e_cum
        last = cum_s[pl.ds(CHUNK - 1, 1), :]
        e_last = jnp.exp(last)
        e_tail = jnp.exp(last - cum)
        kd = kk * e_tail

        a_mat = a_ref[...]
        tri = _tri(CHUNK, True)
        da_s[...] = lax.dot_general(dov, v, _DN["nt"], preferred_element_type=F32) * tri
        dv = (lax.dot_general(a_mat, dov, _DN["tn"], preferred_element_type=F32)
              + lax.dot_general(kd, dst, _DN["nt"], preferred_element_type=F32))
        dk_state = jnp.dot(v, dst, preferred_element_type=F32) * e_tail
        dq_inter = jnp.dot(dov, st, preferred_element_type=F32) * e_cum
        dstate[...] = dst * e_last + lax.dot_general(dov, qe, _DN["tn"], preferred_element_type=F32)

        lane = lax.broadcasted_iota(jnp.int32, (SUB, CHUNK), 1)
        row = lax.broadcasted_iota(jnp.int32, (SUB, 1), 0)
        dk_s[...] = jnp.zeros_like(dk_s)
        for a in range(nsub):
            rows = pl.ds(SUB * a, SUB)
            ca = _boundary(cum_s, a)
            cum_a = cum_s[rows, :]
            q_a = q_s[rows, :]
            ea = jnp.exp(cum_a - ca)
            eb = jnp.exp(jnp.minimum(ca - cum, 0.0))
            da_a = da_s[rows, :]
            da_off = jnp.where(lane < SUB * a, da_a, 0.0)
            dq_a = ea * jnp.dot(da_off, kk * eb, preferred_element_type=F32)
            dk_s[...] += eb * lax.dot_general(da_off, q_a * ea, _DN["tn"], preferred_element_type=F32)
            dk_rows = jnp.zeros((SUB, dh), F32)
            for s in range(SUB):
                r = SUB * a + s
                e = jnp.exp(jnp.minimum(cum_a - cum_s[pl.ds(r, 1), :], 0.0))
                dcol = jnp.sum(jnp.where(lane == r, da_a, 0.0), axis=-1, keepdims=True)
                dcol = jnp.where(row >= s, dcol, 0.0)
                w = dcol * e
                dq_a = dq_a + w * k_s[pl.ds(r, 1), :]
                dk_rows = jnp.where(row == s, jnp.sum(w * q_a, axis=0, keepdims=True), dk_rows)
            dq_s[rows, :] = dq_a
            dk_s[rows, :] += dk_rows

        dq = dq_inter + dq_s[...]
        dk = dk_s[...] + dk_state
        d_last = (jnp.sum(dst * st, axis=0, keepdims=True) * e_last
                  + jnp.sum(kk * dk_state, axis=0, keepdims=True))
        rowc = lax.broadcasted_iota(jnp.int32, (CHUNK, 1), 0)
        dcum = q * dq - kk * dk + jnp.where(rowc == CHUNK - 1, d_last, 0.0)
        dg = _dot_hi(_tri(CHUNK, False), dcum)
        df = dg / f - dk
        dqb_ref[...] = (dq * _silu_grad(qb_ref[...])).astype(BF16)
        dfb_ref[...] = (df * (1.0 - lb) * sig * (1.0 - sig)).astype(BF16)
        dib_ref[...] = dv.astype(BF16)
        dlb_ref[...] += jnp.sum(df * (1.0 - sig), axis=0, keepdims=True)

    def body(qb_ref, fb_ref, ib_ref, lg_ref, st_ref, a_ref, do_ref, dqb_ref, dfb_ref, dib_ref, dlb_ref, *scratch):
        @pl.when(pl.program_id(1) == 0)
        def _():
            for j in range(hb):
                scratch[7 * j][...] = jnp.zeros((dh, dh), F32)
            dlb_ref[...] = jnp.zeros_like(dlb_ref)

        for j in range(hb):
            cols = (slice(None), pl.ds(j * dh, dh))
            one_head(qb_ref.at[cols], fb_ref.at[cols], ib_ref.at[cols], lg_ref.at[cols], st_ref.at[j], a_ref.at[j],
                     do_ref.at[cols], dqb_ref.at[cols], dfb_ref.at[cols], dib_ref.at[cols], dlb_ref.at[cols],
                     *scratch[7 * j:7 * j + 7])

    blk = lambda off: pl.BlockSpec((CHUNK, w), lambda h, i: (n - 1 - i, off // hb + h))
    out_blk = pl.BlockSpec((CHUNK, w), lambda h, i: (n - 1 - i, h))
    return pl.pallas_call(
        body, name=name, grid=(HEADS // hb, n),
        in_specs=[blk(cb["q_b"]), blk(cb["f_b"]), blk(cb["i_b"]), pl.BlockSpec((2, w), lambda h, i: (0, h)),
                  pl.BlockSpec((hb, None, dh, dh), lambda h, i: (h, n - 1 - i, 0, 0)),
                  pl.BlockSpec((hb, None, CHUNK, CHUNK), lambda h, i: (h, n - 1 - i, 0, 0)), out_blk],
        out_specs=[out_blk, out_blk, out_blk, pl.BlockSpec((1, w), lambda h, i: (0, h))],
        out_shape=[jax.ShapeDtypeStruct((t, WIDTH), BF16)] * 3 + [jax.ShapeDtypeStruct((1, WIDTH), F32)],
        scratch_shapes=([pltpu.VMEM((dh, dh), F32)] + [pltpu.VMEM((CHUNK, dh), F32)] * 3
                        + [pltpu.VMEM((CHUNK, CHUNK), F32)] + [pltpu.VMEM((CHUNK, dh), F32)] * 2) * hb,
        compiler_params=_params(("parallel", "arbitrary")),
    )(proj, proj, proj, lb_logits, states, scores, do)


def lb_bwd(dlb, lb_logits, name):
    def body(dlb_ref, lg_ref, o_ref):
        p0 = _lower_bound(lg_ref)
        d0 = dlb_ref[...] * p0 * (1.0 - p0)
        o_ref[0:1, :] = d0
        o_ref[1:2, :] = -d0

    return pl.pallas_call(body, name=name, out_shape=jax.ShapeDtypeStruct(lb_logits.shape, F32))(dlb, lb_logits)


def gnorm_fwd(o_raw, proj, norm_g, name, after=()):
    after = [m for m in after if m is not None]
    t = o_raw.shape[0]
    tr = _row_tile(t)
    cb = _col_blocks()
    dh = HEAD_DIM

    gcol = cb["g_b"] * LANES // WIDTH

    def body(o_ref, gb_ref, g_ref, *rest):
        for h in range(HEADS):
            cols = (slice(None), pl.ds(h * dh, dh))
            x = o_ref[cols]
            rest[-1][cols] = (x * _rstd(x) * g_ref[...] * _silu(gb_ref[cols])).astype(BF16)

    return pl.pallas_call(
        body, name=name, grid=(t // tr,),
        in_specs=[pl.BlockSpec((tr, WIDTH), lambda i: (i, 0)), pl.BlockSpec((tr, WIDTH), lambda i: (i, gcol)),
                  pl.BlockSpec((1, dh), lambda i: (0, 0))] + [pl.BlockSpec(memory_space=pl.ANY)] * len(after),
        out_specs=pl.BlockSpec((tr, WIDTH), lambda i: (i, 0)),
        out_shape=jax.ShapeDtypeStruct((t, WIDTH), BF16), compiler_params=_params(("parallel",)),
    )(o_raw, proj, norm_g, *after)


def gnorm_bwd(dy_ab, o_raw, proj, norm_g, name):
    t = o_raw.shape[0]
    tr = _row_tile(t)
    cb = _col_blocks()
    dh = HEAD_DIM
    gcol = cb["g_b"] * LANES // WIDTH

    def body(dy_ref, o_ref, gb_ref, g_ref, do_ref, dgb_ref, dg_ref):
        @pl.when(pl.program_id(0) == 0)
        def _():
            dg_ref[...] = jnp.zeros_like(dg_ref)

        g = g_ref[...]
        acc = jnp.zeros_like(g)
        for h in range(HEADS):
            cols = (slice(None), pl.ds(h * dh, dh))
            x = o_ref[cols]
            gb = gb_ref[cols]
            dyv = dy_ref[cols]
            dx, dg = _rms_bwd(x, g, dyv * _silu(gb))
            do_ref[cols] = dx
            dgb_ref[cols] = (dyv * (x * _rstd(x) * g) * _silu_grad(gb)).astype(BF16)
            acc = acc + dg
        dg_ref[...] += acc

    rows = pl.BlockSpec((tr, WIDTH), lambda i: (i, 0))
    vec = pl.BlockSpec((1, dh), lambda i: (0, 0))
    return pl.pallas_call(
        body, name=name, grid=(t // tr,),
        in_specs=[pl.BlockSpec((None, tr, WIDTH), lambda i: (1, i, 0)), rows,
                  pl.BlockSpec((tr, WIDTH), lambda i: (i, gcol)), vec],
        out_specs=[rows, rows, vec],
        out_shape=[jax.ShapeDtypeStruct((t, WIDTH), F32), jax.ShapeDtypeStruct((t, WIDTH), BF16),
                   jax.ShapeDtypeStruct((1, dh), F32)],
        compiler_params=_params(("arbitrary",)),
    )(dy_ab, o_raw, proj, norm_g)


def merge_fwd(proj, y, name):
    _, t, d = y.shape
    tr = _row_tile(t)
    tc = _pick(d, (1024, 512, 256, 128))
    cb = _col_blocks()
    ga, gb = cb["gate_a"] * LANES // tc, cb["gate_b"] * LANES // tc

    def body(ga_ref, gb_ref, y_ref, o_ref):
        o_ref[...] = (_sigmoid(ga_ref[...]) * y_ref[0] + _sigmoid(gb_ref[...]) * y_ref[1]).astype(BF16)

    return pl.pallas_call(
        body, name=name, grid=(t // tr, d // tc),
        in_specs=[pl.BlockSpec((tr, tc), lambda i, j: (i, ga + j)), pl.BlockSpec((tr, tc), lambda i, j: (i, gb + j)),
                  pl.BlockSpec((2, tr, tc), lambda i, j: (0, i, j))],
        out_specs=pl.BlockSpec((tr, tc), lambda i, j: (i, j)),
        out_shape=jax.ShapeDtypeStruct((t, d), BF16), compiler_params=_params(("parallel", "parallel")),
    )(proj, proj, y)


def merge_bwd(dm, proj, y, name):
    _, t, d = y.shape
    tr = _row_tile(t)
    tc = _pick(d, (1024, 512, 256, 128))
    cb = _col_blocks()
    ga, gb = cb["gate_a"] * LANES // tc, cb["gate_b"] * LANES // tc

    def body(dm_ref, ga_ref, gb_ref, y_ref, dg_ref, dy_ref):
        dmv = dm_ref[...]
        for idx, g_ref in enumerate((ga_ref, gb_ref)):
            s = _sigmoid(g_ref[...])
            dg_ref[idx] = (dmv * y_ref[idx] * s * (1.0 - s)).astype(BF16)
            dy_ref[idx] = (dmv * s).astype(BF16)

    pair = pl.BlockSpec((2, tr, tc), lambda i, j: (0, i, j))
    return pl.pallas_call(
        body, name=name, grid=(t // tr, d // tc),
        in_specs=[pl.BlockSpec((tr, tc), lambda i, j: (i, j)), pl.BlockSpec((tr, tc), lambda i, j: (i, ga + j)),
                  pl.BlockSpec((tr, tc), lambda i, j: (i, gb + j)), pair],
        out_specs=[pair, pair],
        out_shape=[jax.ShapeDtypeStruct((2, t, d), BF16)] * 2, compiler_params=_params(("parallel", "parallel")),
    )(dm, proj, proj, y)


def ple_tail(h, a, b, g, target, name):
    t, d = h.shape
    tr = _row_tile(t)

    def body(h_ref, a_ref, b_ref, g_ref, t_ref, loss_ref, dh_ref, da_ref, db_ref, dg_ref):
        @pl.when(pl.program_id(0) == 0)
        def _():
            loss_ref[...] = jnp.zeros_like(loss_ref)
            dg_ref[...] = jnp.zeros_like(dg_ref)

        s = _sigmoid(a_ref[...])
        bv = b_ref[...]
        z = s * bv
        gv = g_ref[...]
        err = h_ref[...] + z * _rstd(z) * gv - t_ref[...]
        loss_ref[...] += 0.5 * jnp.sum(jnp.sum(err * err, axis=-1, keepdims=True), axis=0, keepdims=True) / d
        dh = err / d
        dh_ref[...] = dh
        dz, dg = _rms_bwd(z, gv, dh)
        da_ref[...] = (dz * bv * s * (1.0 - s)).astype(BF16)
        db_ref[...] = (dz * s).astype(BF16)
        dg_ref[...] += dg

    row = pl.BlockSpec((tr, d), lambda i: (i, 0))
    vec = pl.BlockSpec((1, d), lambda i: (0, 0))
    return pl.pallas_call(
        body, name=name, grid=(t // tr,), in_specs=[row, row, row, vec, row],
        out_specs=[pl.BlockSpec((1, 1), lambda i: (0, 0)), row, row, row, vec],
        out_shape=[jax.ShapeDtypeStruct((1, 1), F32), jax.ShapeDtypeStruct((t, d), F32),
                   jax.ShapeDtypeStruct((t, d), BF16), jax.ShapeDtypeStruct((t, d), BF16),
                   jax.ShapeDtypeStruct((1, d), F32)],
        compiler_params=_params(("arbitrary",)),
    )(h, a, b, g, target)


def _ffn_fwd(h, u, post_g, next_g, get_w, idx, tag):
    gu = None
    for i, key in enumerate(GATE_UP_KEYS["gu" + idx]):
        w = get_w(key, h if gu is None else gu)
        gu = mm_nn_col(u, w, F32, "%s_gate_up_%d" % (tag, i), slot0=i, total=2, into=gu)
    act = swiglu_act(gu, tag + "_act")
    y = mm_nn_2d(act, get_w("down" + idx, gu), F32, tag + "_down")
    out, u_next = resid_norm(h, y, post_g, MACARON_SCALE, next_g, tag + "_out")
    return out, u_next, (h, u, gu, act, y)


def _ffn_bwd(dh, saved, pre_g, post_g, get_w, emit, advance, idx, tag):
    h, u, gu, act, y = saved
    dy, d_post = post_bwd(dh, y, post_g, MACARON_SCALE, tag + "_post_bwd")
    m1 = emit("down" + idx, mm_tn_2d(act, dy, F32, tag + "_dw_down"))
    dact = mm_nt_2d(dy, get_w("down" + idx), F32, tag + "_dact", after=[m1])
    m2 = advance(dact)
    dgu = swiglu_bwd(dact, gu, tag + "_act_bwd")
    m3 = emit("gu" + idx, mm_tn_col(u, dgu, N_CHIPS, F32, tag + "_dw_gate_up"))
    du = None
    for i, key in enumerate(GATE_UP_KEYS["gu" + idx]):
        du = mm_nt_col(dgu, get_w(key), F32, "%s_du_%d" % (tag, i), after=[m2, m3] if du is None else (),
                       slot0=i, init=du)
    m4 = advance(du)
    dh_in, d_pre = pre_bwd(dh, h, pre_g, [du], tag + "_pre_bwd", after=[m4])
    return dh_in, d_pre, d_post


def _heads_col(a):
    t = a.shape[0]
    at = a[:, :HEADS].T
    return at.reshape(HEADS, t, 1), at.reshape(HEADS, 1, t)


def layer_step(x, p, target, gains, fox_bias, lb_logits, norm_g, get_w, emit, advance, early):
    t = x.shape[0]
    u1 = norm_in(x, gains["ffn1_pre"], "ffn1_norm")
    h1, u2, s1 = _ffn_fwd(x, u1, gains["ffn1_post"], gains["mix_pre"], get_w, "1", "ffn1")

    proj = mm_nt_2d(u2, get_w("in_main", h1), F32, "mix_in")
    fa = mm_nt_2d(u2, get_w("in_fa"), F32, "mix_in_fa")
    mark = early("proj", proj)
    c = fox_prep(fa, fox_bias, "fox_prep")
    c_col, c_row = _heads_col(c)
    o_a = fox_fwd(proj, c_col, c_row, "fox_fwd")
    o_raw, states, scores = hgrn_fwd(proj, lb_logits, "hgrn_fwd", after=[mark])
    mark = early("gu2", o_raw)
    o_b = gnorm_fwd(o_raw, proj, norm_g, "hgrn_norm", after=[mark])
    o_ab = jnp.stack([o_a, o_b])
    y_ab = _mm_branches(o_ab, get_w("proj", proj), "mix_proj")
    merged = merge_fwd(proj, y_ab, "mix_merge")
    mo = mm_nn_2d(merged, get_w("out"), F32, "mix_out")
    h2, u3 = resid_norm(h1, mo, gains["mix_post"], 1.0, gains["ffn2_pre"], "mix_resid")

    h3, u4, s3 = _ffn_fwd(h2, u3, gains["ffn2_post"], gains["ple_pre"], get_w, "2", "ffn2")

    a4 = mm_nn_2d(u4, get_w("ple_gate"), F32, "ple_gate")
    b4 = mm_nn_col(p, get_w("ple_proj"), F32, "ple_proj")[0]
    loss, dh4, da4, db4, d_ple_post = ple_tail(h3, a4, b4, gains["ple_post"], target, "ple_tail")

    marks = [emit("ple_gate", mm_tn_2d(u4, da4, F32, "ple_dw_gate")),
             emit("ple_proj", mm_tn_col(p, db4[None], N_CHIPS, F32, "ple_dw_proj"))]
    du4 = mm_nt_2d(da4, get_w("ple_gate"), F32, "ple_du", after=marks)
    dh3, d_ple_pre = pre_bwd(dh4, h3, gains["ple_pre"], [du4], "ple_pre_bwd", after=[advance(du4)])

    dh2, d_f2_pre, d_f2_post = _ffn_bwd(dh3, s3, gains["ffn2_pre"], gains["ffn2_post"], get_w, emit, advance,
                                        "2", "ffn2")

    dmo, d_mix_post = post_bwd(dh2, mo, gains["mix_post"], 1.0, "mix_post_bwd")
    marks = [emit("out", mm_tn_2d(merged, dmo, F32, "mix_dw_out"))]
    dmerged = mm_nt_2d(dmo, get_w("out"), F32, "mix_dmerged", after=marks)
    dgate, dy_ab = merge_bwd(dmerged, proj, y_ab, "mix_merge_bwd")
    marks = [advance(dmerged), emit("proj", _mm_branches_dw(o_ab, dy_ab, "mix_dw_proj"))]
    do_ab = _mm_branches_bwd(dy_ab, get_w("proj"), "mix_do")
    do_raw, dg_b, d_norm_g = gnorm_bwd(do_ab, o_raw, proj, norm_g, "hgrn_norm_bwd")
    dq_b, df_b, di_b, dlb = hgrn_bwd(proj, lb_logits, states, scores, do_raw, "hgrn_bwd")
    d_lb_logits = lb_bwd(dlb, lb_logits, "lb_bwd")
    dq_a, dk_a, dv_a, dc_col, dc_row = fox_bwd(proj, c_col, c_row, do_ab, "fox_bwd")
    dc = (dc_col.reshape(HEADS, t) + dc_row.reshape(HEADS, t)).T
    dc = jnp.pad(dc, ((0, 0), (0, LANES - HEADS)))
    dfa, d_fox_bias = fox_post_bwd(dc, fa, fox_bias, "fox_post_bwd")
    dproj = jnp.concatenate([dq_a, dk_a.astype(BF16), dv_a.astype(BF16), dq_b, df_b, di_b, dg_b,
                             dgate[0], dgate[1]], axis=1)
    marks.append(emit("in_main", mm_tn_2d(dproj, u2, F32, "mix_dw_in")))
    marks.append(emit("in_fa", mm_tn_2d(dfa, u2, F32, "mix_dw_in_fa")))
    du2a = mm_nn_2d(dproj, get_w("in_main"), F32, "mix_du", after=marks)
    du2b = mm_nn_2d(dfa, get_w("in_fa"), F32, "mix_du_fa")
    dh1, d_mix_pre = pre_bwd(dh2, h1, gains["mix_pre"], [du2a, du2b], "mix_pre_bwd", after=[advance(du2a)])

    dx, d_f1_pre, d_f1_post = _ffn_bwd(dh1, s1, gains["ffn1_pre"], gains["ffn1_post"], get_w, emit, advance,
                                       "1", "ffn1")

    small = dict(ffn1_pre=d_f1_pre, ffn1_post=d_f1_post, mix_pre=d_mix_pre, mix_post=d_mix_post,
                 ffn2_pre=d_f2_pre, ffn2_post=d_f2_post, ple_pre=d_ple_pre, ple_post=d_ple_post,
                 fox_bias=d_fox_bias, lb_logits=d_lb_logits, norm_g=d_norm_g)
    return loss, dx, small


def _mm_branches(o_ab, w_proj, name):
    g, t, kk = o_ab.shape
    _, jn, _, ns = w_proj.shape
    tm = _pick(t, (512, 256, 128))
    return _mm(o_ab, w_proj, mode="nn", grid=(t // tm, g * jn, 1),
               a_spec=pl.BlockSpec((None, tm, kk), lambda i, j, k: (j // jn, i, 0)),
               b_spec=pl.BlockSpec((None, None, kk, ns), lambda i, j, k: (j // jn, j % jn, 0, 0)),
               o_spec=pl.BlockSpec((None, tm, ns), lambda i, j, k: (j // jn, i, j % jn)),
               out_shape=jax.ShapeDtypeStruct((g, t, jn * ns), F32), acc_shape=(tm, ns), name=name)


def _mm_branches_bwd(dy_ab, w_proj, name):
    g, t, _ = dy_ab.shape
    _, jn, kk, ns = w_proj.shape
    tm = _pick(t, (512, 256, 128))
    return _mm(dy_ab, w_proj, mode="nt", grid=(t // tm, g, jn),
               a_spec=pl.BlockSpec((None, tm, ns), lambda i, j, k: (j, i, k)),
               b_spec=pl.BlockSpec((None, None, kk, ns), lambda i, j, k: (j, k, 0, 0)),
               o_spec=pl.BlockSpec((None, tm, kk), lambda i, j, k: (j, i, 0)),
               out_shape=jax.ShapeDtypeStruct((g, t, kk), F32), acc_shape=(tm, kk), name=name)


def _mm_branches_dw(o_ab, dy_ab, name):
    g, t, kk = o_ab.shape
    d = dy_ab.shape[2]
    jn = N_CHIPS
    ns = d // jn
    return _mm(o_ab, dy_ab, mode="tn", grid=(1, g * jn, 1),
               a_spec=pl.BlockSpec((None, t, kk), lambda i, j, k: (j // jn, 0, 0)),
               b_spec=pl.BlockSpec((None, t, ns), lambda i, j, k: (j // jn, 0, j % jn)),
               o_spec=pl.BlockSpec((None, None, kk, ns), lambda i, j, k: (j // jn, j % jn, 0, 0)),
               out_shape=jax.ShapeDtypeStruct((g, jn, kk, ns), F32), acc_shape=(kk, ns), name=name)


HBM_SPEC = pl.BlockSpec(memory_space=pltpu.HBM)
SEM_SPEC = pl.BlockSpec(memory_space=pltpu.SEMAPHORE)
ANY_SPEC = pl.BlockSpec(memory_space=pl.ANY)
EFFECT = pltpu.SideEffectType.DATAFLOW_SIDE_EFFECTING


def _in_hbm(a):
    return pltpu.with_memory_space_constraint(a, pltpu.HBM)


def _place():
    x, y, c = lax.axis_index("x"), lax.axis_index("y"), lax.axis_index("c")
    chips = [(1 - x, y), (x, 1 - y), (1 - x, 1 - y)]
    return x, y, c, chips


def _half(shape, which, axis):
    n = shape[-2 + axis] // 2
    cut = pl.ds(which * n, n)
    return (cut, slice(None)) if axis == 0 else (slice(None), cut)


def _half_shape(shape, axis):
    s = list(shape)
    s[len(s) - 2 + axis] //= 2
    return tuple(s)


def _remote(src, dst, send_sems, recv_sems, k, to):
    return pltpu.make_async_remote_copy(src_ref=src, dst_ref=dst, send_sem=send_sems.at[k], recv_sem=recv_sems.at[k],
                                        device_id=to, device_id_type=MESH)


def split_start(name, srcs, lands, counts, copies):
    ns, nl, nset = len(srcs), len(lands), len(counts)

    def body(*refs):
        src_refs, land_refs = refs[:ns], refs[ns:ns + nl]
        sems = refs[ns + nl:ns + nl + 2 * nset]
        for s, plan in enumerate(copies(src_refs, land_refs)):
            for k, (src, dst, to) in enumerate(plan):
                _remote(src, dst, sems[2 * s], sems[2 * s + 1], k, to).start()
        refs[-1][...] = jnp.zeros_like(refs[-1])

    out_shape = []
    for n in counts:
        out_shape += [pltpu.SemaphoreType.DMA((n,)), pltpu.SemaphoreType.DMA((n,))]
    out_shape += [pltpu.HBM(a.shape, a.dtype) for a in list(srcs) + list(lands)]
    out_shape.append(jax.ShapeDtypeStruct((8, LANES), F32))
    res = pl.pallas_call(
        body, name=name, out_shape=tuple(out_shape), in_specs=[HBM_SPEC] * (ns + nl),
        out_specs=tuple([SEM_SPEC] * (2 * nset) + [HBM_SPEC] * (ns + nl) + [pl.BlockSpec(memory_space=pltpu.VMEM)]),
        input_output_aliases={i: 2 * nset + i for i in range(ns + nl)},
        compiler_params=pltpu.CompilerParams(has_side_effects=EFFECT),
    )(*[_in_hbm(a) for a in list(srcs) + list(lands)])
    sems = [(res[2 * s], res[2 * s + 1]) for s in range(nset)]
    return sems, list(res[2 * nset:2 * nset + ns]), list(res[2 * nset + ns:-1]), res[-1]


def split_wait(name, srcs, lands, sems, afters, copies):
    afters = [a for a in afters if a is not None]
    ns, nl, na = len(srcs), len(lands), len(afters)

    def body(*refs):
        src_refs, land_refs = refs[:ns], refs[ns:ns + nl]
        send_sems, recv_sems = refs[ns + nl:ns + nl + 2]
        for k, (src, dst, to) in enumerate(copies(src_refs, land_refs)):
            cp = _remote(src, dst, send_sems, recv_sems, k, to)
            cp.wait_send()
            cp.wait_recv()

    res = pl.pallas_call(
        body, name=name, out_shape=tuple(pltpu.HBM(a.shape, a.dtype) for a in list(srcs) + list(lands)),
        in_specs=[HBM_SPEC] * (ns + nl) + [SEM_SPEC, SEM_SPEC] + [ANY_SPEC] * na,
        out_specs=tuple([HBM_SPEC] * (ns + nl)), input_output_aliases={i: i for i in range(ns + nl)},
        compiler_params=pltpu.CompilerParams(has_side_effects=EFFECT),
    )(*srcs, *lands, sems[0], sems[1], *afters)
    return list(res[:ns]), list(res[ns:])


def _gather_plan(blocks):
    def copies(src_refs, land_refs):
        x, y, c, chips = _place()
        j_me = 2 * x + y
        plan = []
        for si, li, g, axis in blocks:
            src, land = src_refs[si], land_refs[li].at[g]
            mine = _half(src.shape, c, axis)
            for px, py in chips:
                plan.append((src.at[mine], land.at[(j_me,) + mine], (px, py, c)))
            plan.append((src, land.at[j_me], (x, y, 1 - c)))
        return plan
    return copies


def _gather_arrivals(blocks):
    def copies(src_refs, land_refs):
        x, y, c, chips = _place()
        j_me = 2 * x + y
        plan = []
        for si, li, g, axis in blocks:
            src, land = src_refs[si], land_refs[li].at[g]
            mine = _half(src.shape, c, axis)
            for px, py in chips:
                plan.append((src.at[mine], land.at[(2 * px + py,) + mine], (px, py, c)))
            plan.append((src, land.at[j_me], (x, y, 1 - c)))
        return plan
    return copies


def _pass_plan(blocks, arrivals):
    def copies(src_refs, land_refs):
        x, y, c, chips = _place()
        plan = []
        for li, g, axis in blocks:
            land = src_refs[li].at[g]
            half = _half(land.shape[1:], (1 - c) if arrivals else c, axis)
            for px, py in chips:
                part = land.at[(2 * px + py,) + half]
                plan.append((part, part, (x, y, 1 - c)))
        return plan
    return copies


def gather_pass(name, lands, blocks):
    n = len(lands)

    def body(*refs):
        outs = refs[n:2 * n]
        send_sems, recv_sems = refs[2 * n:]
        x, y, c, chips = _place()
        sent = []
        for i, (li, g, axis) in enumerate(blocks):
            land = outs[li].at[g]
            mine = _half(land.shape[1:], c, axis)
            for k, (px, py) in enumerate(chips):
                part = land.at[(2 * px + py,) + mine]
                cp = _remote(part, part, send_sems, recv_sems, 3 * i + k, (x, y, 1 - c))
                cp.start()
                sent.append(cp)
        for i, (li, g, axis) in enumerate(blocks):
            land = outs[li].at[g]
            other = _half(land.shape[1:], 1 - c, axis)
            for k, (px, py) in enumerate(chips):
                part = land.at[(2 * px + py,) + other]
                _remote(part, part, send_sems, recv_sems, 3 * i + k, (x, y, 1 - c)).wait_recv()
        for cp in sent:
            cp.wait_send()

    m = 3 * len(blocks)
    return pl.pallas_call(
        body, name=name, in_specs=[ANY_SPEC] * n, out_specs=[ANY_SPEC] * n,
        out_shape=[jax.ShapeDtypeStruct(a.shape, a.dtype) for a in lands],
        input_output_aliases={i: i for i in range(n)},
        scratch_shapes=[pltpu.SemaphoreType.DMA((m,)), pltpu.SemaphoreType.DMA((m,))],
    )(*lands)


def _pair_plan(axes):
    def copies(src_refs, land_refs):
        x, y, c, _ = _place()
        return [(src_refs[i].at[(slice(None), slice(None)) + _half(src_refs[i].shape, 1 - c, a)], land_refs[i],
                 (x, y, 1 - c)) for i, a in enumerate(axes)]
    return copies


def _scatter_plan(n):
    def copies(src_refs, land_refs):
        x, y, c, chips = _place()
        return [(src_refs[i].at[:, 2 * px + py], land_refs[i].at[k], (px, py, c))
                for i in range(n) for k, (px, py) in enumerate(chips)]
    return copies


def _broadcast_plan(axes, arrivals):
    def copies(src_refs, land_refs):
        x, y, c, _ = _place()
        plan = []
        for i, a in enumerate(axes):
            part = src_refs[i].at[(slice(None),) + _half(src_refs[i].shape, (1 - c) if arrivals else c, a)]
            plan.append((part, part, (x, y, 1 - c)))
        return plan
    return copies


N_DEV = 8
SLAB_ROWS = 16


def allreduce_small(slab, after):
    def body(x_ref, after_ref, o_ref, land, send_sems, recv_sems):
        x, y, c, _ = _place()
        me = 4 * x + 2 * y + c
        land[me] = x_ref[...]
        copies = []
        for d in range(1, N_DEV):
            to = (me + d) % N_DEV
            cp = pltpu.make_async_remote_copy(
                src_ref=x_ref, dst_ref=land.at[me], send_sem=send_sems.at[d - 1], recv_sem=recv_sems.at[me],
                device_id=(to // 4, (to // 2) % 2, to % 2), device_id_type=MESH)
            cp.start()
            copies.append(cp)
        for d in range(1, N_DEV):
            frm = (me + d) % N_DEV
            pltpu.make_async_remote_copy(
                src_ref=x_ref, dst_ref=land.at[frm], send_sem=send_sems.at[d - 1], recv_sem=recv_sems.at[frm],
                device_id=(frm // 4, (frm // 2) % 2, frm % 2), device_id_type=MESH).wait_recv()
        for cp in copies:
            cp.wait_send()
        acc = land[0]
        for s in range(1, N_DEV):
            acc = acc + land[s]
        o_ref[...] = acc

    vm = pl.BlockSpec(memory_space=pltpu.VMEM)
    return pl.pallas_call(
        body, name="allreduce_small", in_specs=[vm, ANY_SPEC], out_specs=vm,
        out_shape=jax.ShapeDtypeStruct(slab.shape, F32),
        scratch_shapes=[pltpu.VMEM((N_DEV,) + slab.shape, F32), pltpu.SemaphoreType.DMA((N_DEV - 1,)),
                        pltpu.SemaphoreType.DMA((N_DEV,))],
    )(slab, after)


BLOCK_BYTES = 3 * 1024 * 1024


def _tiles_2d(r, c, budget=BLOCK_BYTES):
    if r % 8 == 0:
        tc = c if c % LANES else _pick(c, (2048, 1408, 1024, 512, 256, 128))
        tr = 8
        for cand in (512, 256, 128, 64, 32, 16, 8):
            if r % cand == 0 and cand * tc * 4 <= budget:
                tr = cand
                break
        if tr >= 64 or c % LANES or r * LANES * 4 > budget:
            return tr, tc
    tc = LANES
    for cand in (1024, 512, 256, 128):
        if c % cand == 0 and r * cand * 4 <= budget:
            tc = cand
            break
    return r, tc


def _grid_spec(grid, in_specs, out_specs):
    return pltpu.PrefetchScalarGridSpec(num_scalar_prefetch=1, grid=grid, in_specs=in_specs, out_specs=out_specs)


def _own(axis, nr, nc):
    if axis == 0:
        return lambda i, j, where: (where[1] * nr + i, j)
    return lambda i, j, where: (i, where[1] * nc + j)


def pair_add(where, grad, recv, axis, name):
    g, jn, hr, hc = recv.shape
    tr, tc = _tiles_2d(hr, hc)
    nr, nc = hr // tr, hc // tc
    own = _own(axis, nr, nc)
    others = jn - 1

    def body(where_ref, a_ref, b_ref, o_ref):
        o_ref[...] = (a_ref[...] + b_ref[...]).astype(BF16)

    def block(a, where):
        return a // others, (where[0] + 1 + a % others) % jn

    blk = pl.BlockSpec((None, None, tr, tc), lambda a, i, j, where: block(a, where) + (i, j))
    mine = pl.BlockSpec((None, None, tr, tc), lambda a, i, j, where: block(a, where) + own(i, j, where))
    return pl.pallas_call(
        body, name=name, grid_spec=_grid_spec((g * others, nr, nc), [mine, blk], blk),
        out_shape=jax.ShapeDtypeStruct(recv.shape, BF16),
        compiler_params=_params(("parallel", "parallel", "parallel")),
    )(where, grad, recv)


def chip_add(where, grad, pair, recv, axis, name):
    g, jn, hr, hc = pair.shape
    tr, tc = _tiles_2d(hr, hc)
    nr, nc = hr // tr, hc // tc
    own = _own(axis, nr, nc)
    full = (g, 2 * hr, hc) if axis == 0 else (g, hr, 2 * hc)

    def body(where_ref, a_ref, p_ref, b_ref, o_ref):
        s = a_ref[...] + p_ref[...]
        for k in range(3):
            s = s + b_ref[k].astype(F32)
        o_ref[...] = s

    return pl.pallas_call(
        body, name=name,
        grid_spec=_grid_spec((g, nr, nc),
                             [pl.BlockSpec((None, None, tr, tc), lambda a, i, j, where: (a, where[0]) + own(i, j, where)),
                              pl.BlockSpec((None, None, tr, tc), lambda a, i, j, where: (a, where[0], i, j)),
                              pl.BlockSpec((3, None, tr, tc), lambda a, i, j, where: (0, a, i, j))],
                             pl.BlockSpec((None, tr, tc), lambda a, i, j, where: (a,) + own(i, j, where))),
        out_shape=jax.ShapeDtypeStruct(full, F32), compiler_params=_params(("parallel", "parallel", "parallel")),
    )(where, grad, pair, recv)


def _adam_math(w, g, m, v):
    m2 = ADAM_B1 * m + (1.0 - ADAM_B1) * g
    v2 = ADAM_B2 * v + (1.0 - ADAM_B2) * (g * g)
    m_hat = m2 / (1.0 - ADAM_B1 ** ADAM_STEP)
    v_hat = v2 / (1.0 - ADAM_B2 ** ADAM_STEP)
    delta = -ADAM_LR * (m_hat / (jnp.sqrt(v_hat) + ADAM_EPS) + ADAM_WD * w)
    return delta, m2, v2


def adamw(grad, idx, w, m, v, name):
    _, r, cc = w.shape
    rg = grad.shape[1]
    tr, tc = _tiles_2d(r, cc, BLOCK_BYTES // 2)
    assert rg == r or tr == r
    gr = tr if rg == r else rg

    def body(g_ref, w_ref, m_ref, v_ref, go_ref, d_ref, mo_ref, vo_ref):
        g = g_ref[pl.ds(0, tr), :]
        delta, m2, v2 = _adam_math(w_ref[...], g, m_ref[...], v_ref[...])
        go_ref[...] = g
        d_ref[...] = delta
        mo_ref[...] = m2
        vo_ref[...] = v2

    blk = pl.BlockSpec((None, tr, tc), lambda i, j: (0, i, j))
    return pl.pallas_call(
        body, name=name, grid=(r // tr, cc // tc),
        in_specs=[pl.BlockSpec((None, gr, tc), lambda i, j: (idx, i, j)), blk, blk, blk], out_specs=[blk] * 4,
        out_shape=[jax.ShapeDtypeStruct(w.shape, F32)] * 4, compiler_params=_params(("parallel", "parallel")),
    )(grad, w, m, v)


def adamw_small(g, w, m, v):
    def body(g_ref, w_ref, m_ref, v_ref, d_ref, mo_ref, vo_ref):
        delta, m2, v2 = _adam_math(w_ref[...], g_ref[...], m_ref[...], v_ref[...])
        d_ref[...] = delta
        mo_ref[...] = m2
        vo_ref[...] = v2

    return pl.pallas_call(body, name="adamw_small", out_shape=[jax.ShapeDtypeStruct(w.shape, F32)] * 3)(g, w, m, v)


GAINS = ("ffn1_pre", "ffn1_post", "mix_pre", "mix_post", "ffn2_pre", "ffn2_post", "ple_pre", "ple_post")
WEIGHTS = ("ffn1_pre_g", "ffn1_post_g", "ffn1_w_gate", "ffn1_w_up", "ffn1_w_down", "mix_pre_g", "mix_post_g",
           "mix_w_in", "fox_f_bias", "hgrn_lb_logits", "hgrn_norm_g", "mix_w_proj_fox", "mix_w_proj_hgrn",
           "mix_w_out", "ffn2_pre_g", "ffn2_post_g", "ffn2_w_gate", "ffn2_w_up", "ffn2_w_down", "ple_pre_g",
           "ple_post_g", "ple_w_gate", "ple_w_proj")
GROUPS = dict(gu1=(("ffn1_w_gate", "ffn1_w_up"), 0), down1=(("ffn1_w_down",), 0), win=(("mix_w_in",), 1),
              proj=(("mix_w_proj_fox", "mix_w_proj_hgrn"), 0), out=(("mix_w_out",), 0),
              gu2=(("ffn2_w_gate", "ffn2_w_up"), 0), down2=(("ffn2_w_down",), 0), ple_gate=(("ple_w_gate",), 0),
              ple_proj=(("ple_w_proj",), 0))
TRANSPOSED = ("mix_w_in",)
ROW_BLOCKS = ("down1", "down2", "out", "ple_gate")
GATHER_SETS = (("gate1",), ("up1",), ("down1",), ("win",), ("proj", "out"), ("gu2", "down2", "ple_gate", "ple_proj"))
GATHER_GROUPS = dict(GROUPS, gate1=(("ffn1_w_gate",), 0), up1=(("ffn1_w_up",), 0))
GATE_UP_KEYS = dict(gu1=("gate1", "up1"), gu2=("gu2",))
REDUCE_SETS = (("ple_gate", "ple_proj", "down2", "gu2"), ("out", "proj", "win"), ("down1",), ("gu1",))


def _pad_row(a, width):
    a = a.reshape(1, -1)
    return jnp.pad(a, ((0, 0), (0, width - a.shape[1])))


def _pack_small(vals):
    d = D_MODEL
    rows = [vals[n + "_g"].reshape(1, d) for n in GAINS]
    rows.append(_pad_row(vals["fox_f_bias"], d))
    lg = vals["hgrn_lb_logits"]
    rows += [_pad_row(lg[0], d), _pad_row(lg[1], d), _pad_row(vals["hgrn_norm_g"], d)]
    slab = jnp.concatenate(rows, axis=0)
    return jnp.pad(slab, ((0, SLAB_ROWS - slab.shape[0]), (0, 0)))


def _unpack_small(slab):
    out = {n + "_g": slab[i:i + 1] for i, n in enumerate(GAINS)}
    out["fox_f_bias"] = slab[8:9, :HEADS]
    out["hgrn_lb_logits"] = slab[9:11, :WIDTH]
    out["hgrn_norm_g"] = slab[11:12, :HEAD_DIM]
    return out


def _split_in(win_t):
    lo = 3 * WIDTH
    main = jnp.concatenate([win_t[:lo], win_t[lo + HEADS:]], axis=0)
    fa = jnp.pad(win_t[lo:lo + HEADS], ((0, LANES - HEADS), (0, 0)))
    return main, fa


def _join_in(main, fa):
    lo = 3 * WIDTH
    return jnp.concatenate([main[:lo], fa[:HEADS], main[lo:]], axis=0)


def _as_block(name, a):
    return jnp.swapaxes(a, 1, 2) if name in TRANSPOSED else a


def _send_block(name, a, mark):
    blk = _as_block(name, a)[0]
    if mark is not None:
        blk = blk + mark[0, 0]
    return blk.astype(BF16)


def kernel(x, p, ffn1_pre_g, ffn1_post_g, ffn1_w_gate, ffn1_w_up, ffn1_w_down, mix_pre_g, mix_post_g, mix_w_in, fox_f_bias, hgrn_lb_logits, hgrn_norm_g, mix_w_proj_fox, mix_w_proj_hgrn, mix_w_out, ffn2_pre_g, ffn2_post_g, ffn2_w_gate, ffn2_w_up, ffn2_w_down, ple_pre_g, ple_post_g, ple_w_gate, ple_w_proj, loss_target, m_ffn1_pre_g, m_ffn1_post_g, m_ffn1_w_gate, m_ffn1_w_up, m_ffn1_w_down, m_mix_pre_g, m_mix_post_g, m_mix_w_in, m_fox_f_bias, m_hgrn_lb_logits, m_hgrn_norm_g, m_mix_w_proj_fox, m_mix_w_proj_hgrn, m_mix_w_out, m_ffn2_pre_g, m_ffn2_post_g, m_ffn2_w_gate, m_ffn2_w_up, m_ffn2_w_down, m_ple_pre_g, m_ple_post_g, m_ple_w_gate, m_ple_w_proj, v_ffn1_pre_g, v_ffn1_post_g, v_ffn1_w_gate, v_ffn1_w_up, v_ffn1_w_down, v_mix_pre_g, v_mix_post_g, v_mix_w_in, v_fox_f_bias, v_hgrn_lb_logits, v_hgrn_norm_g, v_mix_w_proj_fox, v_mix_w_proj_hgrn, v_mix_w_out, v_ffn2_pre_g, v_ffn2_post_g, v_ffn2_w_gate, v_ffn2_w_up, v_ffn2_w_down, v_ple_pre_g, v_ple_post_g, v_ple_w_gate, v_ple_w_proj):
    args = dict(locals())
    wts = {n: args[n] for n in WEIGHTS}
    mom = {n: args["m_" + n] for n in WEIGHTS}
    var = {n: args["v_" + n] for n in WEIGHTS}
    d = D_MODEL
    where = jnp.stack([2 * lax.axis_index("x") + lax.axis_index("y"), lax.axis_index("c")]).astype(jnp.int32)

    def start_sets(name, which, mark):
        srcs, lands, plans = [], [], []
        for si in which:
            blocks = []
            for g in GATHER_SETS[si]:
                names, axis = GATHER_GROUPS[g]
                for pos, n in enumerate(names):
                    blocks.append((len(srcs), len(lands), pos, axis))
                    srcs.append(_send_block(n, wts[n], mark))
                lands.append(lax.empty((len(names), N_CHIPS) + srcs[-1].shape, BF16))
            plans.append(blocks)
        sems, srcs, lands, mark = split_start(name, srcs, lands, [4 * len(b) for b in plans],
                                              lambda sr, lr: [_gather_plan(b)(sr, lr) for b in plans])
        out = {}
        for k, si in enumerate(which):
            s_idx = sorted({b[0] for b in plans[k]})
            l_idx = sorted({b[1] for b in plans[k]})
            local = [(s_idx.index(a), l_idx.index(b), pos, ax) for a, b, pos, ax in plans[k]]
            out[si] = (sems[k], [srcs[i] for i in s_idx], [lands[i] for i in l_idx], local)
        return out, mark

    flying, mark = start_sets("gather_start_0", [0], None)
    rest, all_started = start_sets("gather_start_1", list(range(1, len(GATHER_SETS))), mark)
    flying.update(rest)
    full, passing = {}, {}

    def set_of(key):
        g = "win" if key in ("in_main", "in_fa") else key
        return g, [g in s for s in GATHER_SETS].index(True)

    def arrive(si, after):
        sem, srcs, lands, local = flying.pop(si)
        _, got = split_wait("gather_wait_%d" % si, srcs, lands, sem, [after, all_started], _gather_arrivals(local))
        return got, [(b, pos, ax) for _, b, pos, ax in local]

    def early(key, after):
        g, si = set_of(key)
        if g in full or si not in flying:
            return None
        got, blocks = arrive(si, after)
        sem, got, _, mark = split_start("pass_start_%d" % si, got, [], [3 * len(blocks)],
                                        lambda sr, lr: [_pass_plan(blocks, False)(sr, lr)])
        passing[si] = (sem[0], got, blocks)
        return mark

    def land_set(si, after):
        if si in passing:
            sem, got, blocks = passing.pop(si)
            got, _ = split_wait("pass_wait_%d" % si, got, [], sem, [after], _pass_plan(blocks, True))
        else:
            got, blocks = arrive(si, after)
            got = gather_pass("gather_pass_%d" % si, got, blocks)
        for g, arr in zip(GATHER_SETS[si], got):
            if g == "win":
                full["win"] = arr
                full["in_main"], full["in_fa"] = _split_in(arr.reshape(-1, d))
            else:
                full[g] = arr.reshape(-1, d) if g in ROW_BLOCKS else arr

    def get_w(key, after=None):
        g, si = set_of(key)
        if g not in full:
            land_set(si, after)
        return full[key]

    grads, pairing, started = {}, [], {}
    rows4 = lambda a: a.reshape(1, N_CHIPS, a.shape[0] // N_CHIPS, a.shape[1])

    def emit(key, grad):
        if key in ("in_main", "in_fa"):
            grads[key] = grad
            if "in_main" not in grads or "in_fa" not in grads:
                return None
            key, grad = "win", rows4(_join_in(grads["in_main"], grads["in_fa"]))
        grads[key] = grad if grad.ndim == 4 else rows4(grad)
        for si, s in enumerate(REDUCE_SETS):
            if key in s and all(g in grads for g in s):
                axes = [GROUPS[g][1] for g in s]
                own = [grads[g] for g in s]
                zones = [lax.empty(_half_shape(a.shape, ax), F32) for a, ax in zip(own, axes)]
                plan = _pair_plan(axes)
                sem, own, zones, mark = split_start("pair_start_%d" % si, own, zones, [len(s)],
                                                    lambda sr, lr: [plan(sr, lr)])
                pairing.append((si, sem[0], own, zones, axes, plan))
                return mark
        return None

    def advance(value):
        mark = None
        while pairing:
            si, sem, own, zones, axes, plan = pairing.pop(0)
            s = REDUCE_SETS[si]
            own, recv = split_wait("pair_wait_%d" % si, own, zones, sem, [value], plan)
            parts = [pair_add(where, a, r, ax, "pair_add_" + g) for g, a, r, ax in zip(s, own, recv, axes)]
            zones = [lax.empty((3, q.shape[0]) + q.shape[2:], BF16) for q in parts]
            plan = _scatter_plan(len(s))
            sem, parts, zones, mark = split_start("scatter_start_%d" % si, parts, zones, [3 * len(s)],
                                                  lambda sr, lr: [plan(sr, lr)])
            started[si] = (sem[0], parts, zones, own, recv, axes, plan)
        return mark

    gains = {n: wts[n + "_g"] for n in GAINS}
    loss, dx, small = layer_step(x[0], p[0, 0].astype(BF16), loss_target[0], gains, _pad_row(fox_f_bias, LANES),
                                 hgrn_lb_logits, hgrn_norm_g, get_w, emit, advance, early)

    out_g, out_d, out_m, out_v = {}, {}, {}, {}
    after, crossing = None, []

    def finish(si, sem, halves, axes, mark):
        reduced, _ = split_wait("broadcast_wait_%d" % si, halves, [], sem, [mark], _broadcast_plan(axes, True))
        last = None
        for g, red in zip(REDUCE_SETS[si], reduced):
            for idx, n in enumerate(GROUPS[g][0]):
                res = adamw(red, idx, _as_block(n, wts[n]), _as_block(n, mom[n]), _as_block(n, var[n]), "adamw_" + n)
                out_g[n], out_d[n], out_m[n], out_v[n] = [_as_block(n, r) for r in res]
                last = res[1]
        return last

    for si, s in enumerate(REDUCE_SETS):
        sem, parts, zones, own, recv, axes, plan = started[si]
        _, zones = split_wait("scatter_wait_%d" % si, parts, zones, sem, [dx, after], plan)
        halves = [chip_add(where, a, r, z, ax, "chip_add_" + g) for g, a, r, z, ax in zip(s, own, recv, zones, axes)]
        plan = _broadcast_plan(axes, False)
        sem, halves, _, mark = split_start("broadcast_start_%d" % si, halves, [], [len(s)],
                                           lambda sr, lr: [plan(sr, lr)])
        crossing.append((si, sem[0], halves, axes))
        if len(crossing) > 1:
            after = finish(*crossing.pop(0), mark)
    while crossing:
        after = finish(*crossing.pop(0), after)

    small_named = {n + "_g": small[n] for n in GAINS}
    small_named.update(fox_f_bias=small["fox_bias"][:, :HEADS], hgrn_lb_logits=small["lb_logits"],
                       hgrn_norm_g=small["norm_g"])
    g_small = allreduce_small(_pack_small(small_named), after)
    d_small, m_small, v_small = adamw_small(g_small, _pack_small(wts), _pack_small(mom), _pack_small(var))

    for dst, slab in ((out_g, g_small), (out_d, d_small), (out_m, m_small), (out_v, v_small)):
        dst.update(_unpack_small(slab))

    total = lax.psum(loss[0, 0], ("x", "y", "c"))
    return (total, dx[None], *[out_g[n] for n in WEIGHTS], *[out_d[n] for n in WEIGHTS],
            *[out_m[n] for n in WEIGHTS], *[out_v[n] for n in WEIGHTS])
```

```python
import functools

import jax
import jax.numpy as jnp
from jax import lax
from jax.experimental import pallas as pl
from jax.experimental.pallas import tpu as pltpu

F32 = jnp.float32
BF16 = jnp.bfloat16

D_MODEL = 2048
SEQ = 2048
D_FF = 5632
PLE_DIM = 256
HEADS = 8
HEAD_DIM = 128
WIDTH = HEADS * HEAD_DIM
CHUNK = 64
SUB = 16
HGRN_HEADS_PER_STEP = 8
NORM_EPS = 1e-6
MACARON_SCALE = 0.5
N_CHIPS = 4

ADAM_LR = 0.001
ADAM_B1 = 0.9
ADAM_B2 = 0.999
ADAM_EPS = 1e-08
ADAM_WD = 0.01
ADAM_STEP = 10

LANES = 128
VMEM_LIMIT = 56 * 1024 * 1024
NEG_BIG = -1e30
MESH = pl.DeviceIdType.MESH


def _pick(n, cands):
    for c in cands:
        if c <= n and n % c == 0:
            return c
    return n


def _params(sem, vmem=VMEM_LIMIT):
    return pltpu.CompilerParams(dimension_semantics=sem, vmem_limit_bytes=vmem)


def _sigmoid(x):
    return 1.0 / (1.0 + jnp.exp(-x))


def _silu(x):
    return x * _sigmoid(x)


def _silu_grad(x):
    s = _sigmoid(x)
    return s * (1.0 + x * (1.0 - s))


_DN = {"nn": (((1,), (0,)), ((), ())), "nt": (((1,), (1,)), ((), ())), "tn": (((0,), (0,)), ((), ()))}


def _mm(a, b, *, mode, grid, a_spec, b_spec, o_spec, out_shape, acc_shape, name, after=(), init=None, into=None):
    nk = grid[2]
    dn = _DN[mode]
    after = [m for m in after if m is not None]
    extra = ([init] if init is not None else []) + ([into] if into is not None else []) + after
    n_extra = len(extra)

    def body(a_ref, b_ref, *rest):
        o_ref, acc_ref = rest[n_extra:]
        k = pl.program_id(2)

        @pl.when(k == 0)
        def _():
            acc_ref[...] = jnp.zeros_like(acc_ref) if init is None else rest[0][...].astype(F32)

        acc_ref[...] += lax.dot_general(a_ref[...].astype(BF16), b_ref[...].astype(BF16), dn,
                                        preferred_element_type=F32)

        @pl.when(k == nk - 1)
        def _():
            o_ref[...] = acc_ref[...].astype(o_ref.dtype)

    anywhere = pl.BlockSpec(memory_space=pl.ANY)
    return pl.pallas_call(
        body, name=name, grid=grid,
        in_specs=[a_spec, b_spec] + ([o_spec] if init is not None else []) + [anywhere] * (n_extra - (init is not None)),
        out_specs=o_spec, out_shape=out_shape, scratch_shapes=[pltpu.VMEM(acc_shape, F32)],
        input_output_aliases={} if into is None else {2 + (init is not None): 0},
        compiler_params=_params(("parallel", "parallel", "arbitrary")),
    )(a, b, *extra)


def mm_nn_2d(a, b, out_dtype, name, after=()):
    m, kk = a.shape
    n = b.shape[1]
    tm, tn = _pick(m, (512, 256, 128)), _pick(n, (1024, 512, 256, 128))
    tk = _pick(kk, (5632, 2816, 2048, 1408, 1024, 512, 256, 128))
    return _mm(a, b, mode="nn", grid=(m // tm, n // tn, kk // tk),
               a_spec=pl.BlockSpec((tm, tk), lambda i, j, k: (i, k)),
               b_spec=pl.BlockSpec((tk, tn), lambda i, j, k: (k, j)),
               o_spec=pl.BlockSpec((tm, tn), lambda i, j, k: (i, j)),
               out_shape=jax.ShapeDtypeStruct((m, n), out_dtype), acc_shape=(tm, tn), name=name, after=after)


def mm_nt_2d(a, b, out_dtype, name, after=()):
    m, c = a.shape
    n = b.shape[0]
    tm, tn, tk = _pick(m, (512, 256, 128)), _pick(n, (1408, 1024, 512, 256, 128)), _pick(c, (2048, 1408, 1024, 512, 256, 128))
    return _mm(a, b, mode="nt", grid=(m // tm, n // tn, c // tk),
               a_spec=pl.BlockSpec((tm, tk), lambda i, j, k: (i, k)),
               b_spec=pl.BlockSpec((tn, tk), lambda i, j, k: (j, k)),
               o_spec=pl.BlockSpec((tm, tn), lambda i, j, k: (i, j)),
               out_shape=jax.ShapeDtypeStruct((m, n), out_dtype), acc_shape=(tm, tn), name=name, after=after)


def mm_tn_2d(a, b, out_dtype, name):
    c, m = a.shape
    n = b.shape[1]
    tm, tn, tk = _pick(m, (1408, 1024, 512, 256, 128)), _pick(n, (1024, 512, 256, 128)), _pick(c, (2048, 1024, 512, 256, 128))
    return _mm(a, b, mode="tn", grid=(m // tm, n // tn, c // tk),
               a_spec=pl.BlockSpec((tk, tm), lambda i, j, k: (k, i)),
               b_spec=pl.BlockSpec((tk, tn), lambda i, j, k: (k, j)),
               o_spec=pl.BlockSpec((tm, tn), lambda i, j, k: (i, j)),
               out_shape=jax.ShapeDtypeStruct((m, n), out_dtype), acc_shape=(tm, tn), name=name)


def mm_nn_col(a, w, out_dtype, name, slot0=0, total=None, into=None):
    m, kk = a.shape
    g, jn, _, ns = w.shape
    total = g if total is None else total
    tm, tk = _pick(m, (512, 256, 128)), _pick(kk, (2048, 1024, 512, 256, 128))
    return _mm(a, w, mode="nn", grid=(m // tm, g * jn, kk // tk),
               a_spec=pl.BlockSpec((tm, tk), lambda i, j, k: (i, k)),
               b_spec=pl.BlockSpec((None, None, tk, ns), lambda i, j, k: (j // jn, j % jn, k, 0)),
               o_spec=pl.BlockSpec((None, tm, ns), lambda i, j, k: (slot0 + j // jn, i, j % jn)),
               out_shape=jax.ShapeDtypeStruct((total, m, jn * ns), out_dtype), acc_shape=(tm, ns), name=name,
               into=into)


def mm_nt_col(a, w, out_dtype, name, after=(), slot0=0, init=None):
    _, m, _ = a.shape
    g, jn, kk, ns = w.shape
    kc = 2 if jn % 2 == 0 else 1
    per = jn // kc
    nk = g * per
    tm, tn = _pick(m, (512, 256, 128)), _pick(kk, (1024, 512, 256, 128))
    extra = ([init] if init is not None else []) + [v for v in after if v is not None]

    def body(a_ref, w_ref, *rest):
        o_ref, acc_ref = rest[len(extra):]
        k = pl.program_id(2)
        part = None
        for c in range(kc):
            prod = lax.dot_general(a_ref[:, c * ns:(c + 1) * ns].astype(BF16), w_ref[c].astype(BF16), _DN["nt"],
                                   preferred_element_type=F32)
            part = prod if part is None else part + prod

        @pl.when(k == 0)
        def _():
            acc_ref[...] = part if init is None else rest[0][...].astype(F32) + part

        @pl.when(k > 0)
        def _():
            acc_ref[...] += part

        @pl.when(k == nk - 1)
        def _():
            o_ref[...] = acc_ref[...].astype(o_ref.dtype)

    o_spec = pl.BlockSpec((tm, tn), lambda i, j, k: (i, j))
    anywhere = pl.BlockSpec(memory_space=pl.ANY)
    return pl.pallas_call(
        body, name=name, grid=(m // tm, kk // tn, nk),
        in_specs=[pl.BlockSpec((None, tm, kc * ns), lambda i, j, k: (slot0 + k // per, i, k % per)),
                  pl.BlockSpec((None, kc, tn, ns), lambda i, j, k: (k // per, k % per, j, 0))]
        + ([o_spec] if init is not None else []) + [anywhere] * (len(extra) - (init is not None)),
        out_specs=o_spec, out_shape=jax.ShapeDtypeStruct((m, kk), out_dtype),
        scratch_shapes=[pltpu.VMEM((tm, tn), F32)],
        compiler_params=_params(("parallel", "parallel", "arbitrary")),
    )(a, w, *extra)


def mm_tn_col(a, b, jn, out_dtype, name):
    c, kk = a.shape
    g, _, n = b.shape
    ns = n // jn
    tm, tk = _pick(kk, (512, 256, 128)), _pick(c, (2048, 1024, 512, 256, 128))
    return _mm(a, b, mode="tn", grid=(kk // tm, g * jn, c // tk),
               a_spec=pl.BlockSpec((tk, tm), lambda i, j, k: (k, i)),
               b_spec=pl.BlockSpec((None, tk, ns), lambda i, j, k: (j // jn, k, j % jn)),
               o_spec=pl.BlockSpec((None, None, tm, ns), lambda i, j, k: (j // jn, j % jn, i, 0)),
               out_shape=jax.ShapeDtypeStruct((g, jn, kk, ns), out_dtype), acc_shape=(tm, ns), name=name)


def _rstd(x):
    return lax.rsqrt(jnp.mean(x * x, axis=-1, keepdims=True) + NORM_EPS)


def _rms_bwd(x, g, dy):
    r = _rstd(x)
    xn = x * r
    dyg = dy * g
    dx = r * (dyg - xn * jnp.mean(dyg * xn, axis=-1, keepdims=True))
    return dx, jnp.sum(dy * xn, axis=0, keepdims=True)


def _row_tile(t):
    return _pick(t, (256, 128, 64, 32, 16, 8))


def norm_in(h, g, name):
    t, d = h.shape
    tr = _row_tile(t)

    def body(h_ref, g_ref, u_ref):
        x = h_ref[...]
        u_ref[...] = (x * _rstd(x) * g_ref[...]).astype(BF16)

    return pl.pallas_call(
        body, name=name, grid=(t // tr,),
        in_specs=[pl.BlockSpec((tr, d), lambda i: (i, 0)), pl.BlockSpec((1, d), lambda i: (0, 0))],
        out_specs=pl.BlockSpec((tr, d), lambda i: (i, 0)),
        out_shape=jax.ShapeDtypeStruct((t, d), BF16), compiler_params=_params(("parallel",)),
    )(h, g)


def resid_norm(h, y, g, scale, g_next, name):
    t, d = h.shape
    tr = _row_tile(t)

    def body(h_ref, y_ref, g_ref, gn_ref, o_ref, u_ref):
        yv = y_ref[...]
        out = h_ref[...] + scale * (yv * _rstd(yv) * g_ref[...])
        o_ref[...] = out
        u_ref[...] = (out * _rstd(out) * gn_ref[...]).astype(BF16)

    row = pl.BlockSpec((tr, d), lambda i: (i, 0))
    vec = pl.BlockSpec((1, d), lambda i: (0, 0))
    return pl.pallas_call(
        body, name=name, grid=(t // tr,), in_specs=[row, row, vec, vec], out_specs=[row, row],
        out_shape=[jax.ShapeDtypeStruct((t, d), F32), jax.ShapeDtypeStruct((t, d), BF16)],
        compiler_params=_params(("parallel",)),
    )(h, y, g, g_next)


def post_bwd(dh, y, g, scale, name):
    t, d = dh.shape
    tr = _row_tile(t)

    def body(dh_ref, y_ref, g_ref, dy_ref, dg_ref):
        @pl.when(pl.program_id(0) == 0)
        def _():
            dg_ref[...] = jnp.zeros_like(dg_ref)

        dx, dg = _rms_bwd(y_ref[...], g_ref[...], scale * dh_ref[...])
        dy_ref[...] = dx.astype(BF16)
        dg_ref[...] += dg

    row = pl.BlockSpec((tr, d), lambda i: (i, 0))
    vec = pl.BlockSpec((1, d), lambda i: (0, 0))
    return pl.pallas_call(
        body, name=name, grid=(t // tr,), in_specs=[row, row, vec], out_specs=[row, vec],
        out_shape=[jax.ShapeDtypeStruct((t, d), BF16), jax.ShapeDtypeStruct((1, d), F32)],
        compiler_params=_params(("arbitrary",)),
    )(dh, y, g)


def pre_bwd(dh, h, g, dus, name, after=()):
    t, d = dh.shape
    tr = _row_tile(t)
    n_du = len(dus)
    after = [m for m in after if m is not None]

    def body(*refs):
        dh_ref, h_ref, g_ref = refs[:3]
        du_refs = refs[3:3 + n_du]
        o_ref, dg_ref = refs[3 + n_du + len(after):]

        @pl.when(pl.program_id(0) == 0)
        def _():
            dg_ref[...] = jnp.zeros_like(dg_ref)

        du = du_refs[0][...]
        for r in du_refs[1:]:
            du = du + r[...]
        dx, dg = _rms_bwd(h_ref[...], g_ref[...], du)
        o_ref[...] = dh_ref[...] + dx
        dg_ref[...] += dg

    row = pl.BlockSpec((tr, d), lambda i: (i, 0))
    vec = pl.BlockSpec((1, d), lambda i: (0, 0))
    return pl.pallas_call(
        body, name=name, grid=(t // tr,),
        in_specs=[row, row, vec] + [row] * n_du + [pl.BlockSpec(memory_space=pl.ANY)] * len(after),
        out_specs=[row, vec], out_shape=[jax.ShapeDtypeStruct((t, d), F32), jax.ShapeDtypeStruct((1, d), F32)],
        compiler_params=_params(("arbitrary",)),
    )(dh, h, g, *dus, *after)


def _ew_tiles(t, f):
    return _pick(t, (256, 128, 64, 32, 16, 8)), _pick(f, (1408, 1024, 512, 256, 128))


def swiglu_act(gu, name):
    _, t, f = gu.shape
    tr, tc = _ew_tiles(t, f)

    def body(gu_ref, o_ref):
        o_ref[...] = (_silu(gu_ref[0]) * gu_ref[1]).astype(BF16)

    return pl.pallas_call(
        body, name=name, grid=(t // tr, f // tc),
        in_specs=[pl.BlockSpec((2, tr, tc), lambda i, j: (0, i, j))],
        out_specs=pl.BlockSpec((tr, tc), lambda i, j: (i, j)),
        out_shape=jax.ShapeDtypeStruct((t, f), BF16), compiler_params=_params(("parallel", "parallel")),
    )(gu)


def swiglu_bwd(dact, gu, name):
    _, t, f = gu.shape
    tr, tc = _ew_tiles(t, f)

    def body(da_ref, gu_ref, o_ref):
        da = da_ref[...]
        gate = gu_ref[0]
        o_ref[0] = (da * gu_ref[1] * _silu_grad(gate)).astype(BF16)
        o_ref[1] = (da * _silu(gate)).astype(BF16)

    return pl.pallas_call(
        body, name=name, grid=(t // tr, f // tc),
        in_specs=[pl.BlockSpec((tr, tc), lambda i, j: (i, j)), pl.BlockSpec((2, tr, tc), lambda i, j: (0, i, j))],
        out_specs=pl.BlockSpec((2, tr, tc), lambda i, j: (0, i, j)),
        out_shape=jax.ShapeDtypeStruct((2, t, f), BF16), compiler_params=_params(("parallel", "parallel")),
    )(dact, gu)


def _col_blocks():
    w = WIDTH // LANES
    return dict(q_a=0, k_a=w, v_a=2 * w, q_b=3 * w, f_b=4 * w, i_b=5 * w, g_b=6 * w, gate_a=7 * w,
                gate_b=7 * w + D_MODEL // LANES)


def _tri(n, lower):
    r = lax.broadcasted_iota(jnp.int32, (n, n), 0)
    c = lax.broadcasted_iota(jnp.int32, (n, n), 1)
    return jnp.where((r >= c) if lower else (r <= c), 1.0, 0.0).astype(F32)


def _dot_hi(a, b):
    return jnp.dot(a, b, precision=lax.Precision.HIGHEST, preferred_element_type=F32)


def fox_prep(fa, bias, name):
    t, w = fa.shape
    tb = _pick(t, (256, 128, 64))

    def body(fa_ref, b_ref, c_ref, carry_ref):
        @pl.when(pl.program_id(0) == 0)
        def _():
            carry_ref[...] = jnp.zeros_like(carry_ref)

        z = fa_ref[...] + b_ref[...]
        lf = jnp.minimum(z, 0.0) - jnp.log(1.0 + jnp.exp(-jnp.abs(z)))
        c = _dot_hi(_tri(tb, True), lf) + carry_ref[...]
        c_ref[...] = c
        carry_ref[...] = carry_ref[...] + jnp.sum(lf, axis=0, keepdims=True)

    return pl.pallas_call(
        body, name=name, grid=(t // tb,),
        in_specs=[pl.BlockSpec((tb, w), lambda i: (i, 0)), pl.BlockSpec((1, w), lambda i: (0, 0))],
        out_specs=pl.BlockSpec((tb, w), lambda i: (i, 0)),
        out_shape=jax.ShapeDtypeStruct((t, w), F32), scratch_shapes=[pltpu.VMEM((1, w), F32)],
        compiler_params=_params(("arbitrary",)),
    )(fa, bias)


def fox_post_bwd(dc, fa, bias, name):
    t, w = fa.shape
    tb = _pick(t, (256, 128, 64))
    nb = t // tb

    def body(dc_ref, fa_ref, b_ref, dfa_ref, db_ref, carry_ref):
        @pl.when(pl.program_id(0) == 0)
        def _():
            carry_ref[...] = jnp.zeros_like(carry_ref)
            db_ref[...] = jnp.zeros_like(db_ref)

        dcv = dc_ref[...]
        dlf = _dot_hi(_tri(tb, False), dcv) + carry_ref[...]
        z = fa_ref[...] + b_ref[...]
        dz = dlf * _sigmoid(-z)
        dfa_ref[...] = dz.astype(BF16)
        db_ref[...] += jnp.sum(dz, axis=0, keepdims=True)
        carry_ref[...] = carry_ref[...] + jnp.sum(dcv, axis=0, keepdims=True)

    rev = pl.BlockSpec((tb, w), lambda i: (nb - 1 - i, 0))
    vec = pl.BlockSpec((1, w), lambda i: (0, 0))
    return pl.pallas_call(
        body, name=name, grid=(nb,), in_specs=[rev, rev, vec], out_specs=[rev, vec],
        out_shape=[jax.ShapeDtypeStruct((t, w), BF16), jax.ShapeDtypeStruct((1, w), F32)],
        scratch_shapes=[pltpu.VMEM((1, w), F32)], compiler_params=_params(("arbitrary",)),
    )(dc, fa, bias)


def _fox_probs(q_ref, k_ref, cc_ref, cr_ref, qi, tq, t):
    scale = HEAD_DIM ** -0.5
    s = lax.dot_general(q_ref[...].astype(BF16), k_ref[...].astype(BF16), _DN["nt"], preferred_element_type=F32)
    logits = s * scale + cc_ref[...] - cr_ref[...]
    qpos = qi * tq + lax.broadcasted_iota(jnp.int32, (tq, t), 0)
    kpos = lax.broadcasted_iota(jnp.int32, (tq, t), 1)
    logits = jnp.where(kpos <= qpos, logits, NEG_BIG)
    m = jnp.max(logits, axis=-1, keepdims=True)
    p = jnp.exp(logits - m)
    return p / jnp.sum(p, axis=-1, keepdims=True)


FOX_SEGMENTS = 4


def _fox_segments(t):
    tq = _pick(t, (256, 128))
    nseg = min(FOX_SEGMENTS, t // tq)
    return tq, nseg, t // tq // nseg


def _fox_specs(t, q0, kt, tq):
    cb = _col_blocks()
    dh = HEAD_DIM
    return [pl.BlockSpec((tq, dh), lambda h, i: (q0 + i, cb["q_a"] + h)),
            pl.BlockSpec((kt, dh), lambda h, i: (0, cb["k_a"] + h)),
            pl.BlockSpec((kt, dh), lambda h, i: (0, cb["v_a"] + h)),
            pl.BlockSpec((None, tq, 1), lambda h, i: (h, q0 + i, 0)),
            pl.BlockSpec((None, 1, kt), lambda h, i: (h, 0, 0))]


def fox_fwd(proj, c_col, c_row, name):
    t = proj.shape[0]
    tq, nseg, nq = _fox_segments(t)
    dh = HEAD_DIM

    def segment(out, r):
        q0, kt = r * nq, (r + 1) * nq * tq

        def body(q_ref, k_ref, v_ref, cc_ref, cr_ref, prev_ref, o_ref):
            p = _fox_probs(q_ref, k_ref, cc_ref, cr_ref, q0 + pl.program_id(1), tq, kt)
            o_ref[...] = jnp.dot(p.astype(BF16), v_ref[...].astype(BF16), preferred_element_type=F32).astype(BF16)

        return pl.pallas_call(
            body, name="%s_%d" % (name, r), grid=(HEADS, nq),
            in_specs=_fox_specs(t, q0, kt, tq) + [pl.BlockSpec(memory_space=pl.ANY)],
            out_specs=pl.BlockSpec((tq, dh), lambda h, i: (q0 + i, h)),
            out_shape=jax.ShapeDtypeStruct((t, WIDTH), BF16), input_output_aliases={5: 0},
            compiler_params=_params(("parallel", "parallel")),
        )(proj, proj, proj, c_col, c_row, out)

    out = lax.empty((t, WIDTH), BF16)
    for r in range(nseg):
        out = segment(out, r)
    return out


def fox_bwd(proj, c_col, c_row, do_ab, name):
    t = proj.shape[0]
    tq, nseg, nq = _fox_segments(t)
    dh = HEAD_DIM
    scale = HEAD_DIM ** -0.5

    def segment(acc, r):
        q0, kt = r * nq, (r + 1) * nq * tq

        def body(q_ref, k_ref, v_ref, cc_ref, cr_ref, do_ref, dqp_ref, dkp_ref, dvp_ref, dccp_ref, dcrp_ref,
                 dq_ref, dk_ref, dv_ref, dcc_ref, dcr_ref):
            @pl.when(pl.program_id(1) == 0)
            def _():
                dk_ref[...] = dkp_ref[...]
                dv_ref[...] = dvp_ref[...]
                dcr_ref[...] = dcrp_ref[...]

            p = _fox_probs(q_ref, k_ref, cc_ref, cr_ref, q0 + pl.program_id(1), tq, kt)
            dov = do_ref[...].astype(BF16)
            kb = k_ref[...].astype(BF16)
            dv_ref[...] += lax.dot_general(p.astype(BF16), dov, _DN["tn"], preferred_element_type=F32)
            dp = lax.dot_general(dov, v_ref[...].astype(BF16), _DN["nt"], preferred_element_type=F32)
            ds = p * (dp - jnp.sum(p * dp, axis=-1, keepdims=True))
            dcc_ref[...] = jnp.sum(ds, axis=-1, keepdims=True)
            dcr_ref[...] -= jnp.sum(ds, axis=0, keepdims=True)
            dss = (ds * scale).astype(BF16)
            dq_ref[...] = jnp.dot(dss, kb, preferred_element_type=F32).astype(BF16)
            dk_ref[...] += lax.dot_general(dss, q_ref[...].astype(BF16), _DN["tn"], preferred_element_type=F32)

        rows = pl.BlockSpec((tq, dh), lambda h, i: (q0 + i, h))
        keys = pl.BlockSpec((kt, dh), lambda h, i: (0, h))
        col = pl.BlockSpec((None, tq, 1), lambda h, i: (h, q0 + i, 0))
        row = pl.BlockSpec((None, 1, kt), lambda h, i: (h, 0, 0))
        anywhere = pl.BlockSpec(memory_space=pl.ANY)
        return pl.pallas_call(
            body, name="%s_%d" % (name, r), grid=(HEADS, nq),
            in_specs=_fox_specs(t, q0, kt, tq)
            + [pl.BlockSpec((None, tq, dh), lambda h, i: (0, q0 + i, h)), anywhere, keys, keys, anywhere, row],
            out_specs=[rows, keys, keys, col, row],
            out_shape=[jax.ShapeDtypeStruct(a.shape, a.dtype) for a in acc],
            input_output_aliases={6 + k: k for k in range(5)},
            compiler_params=_params(("parallel", "arbitrary")),
        )(proj, proj, proj, c_col, c_row, do_ab, *acc)

    acc = [lax.empty((t, WIDTH), BF16), jnp.zeros((t, WIDTH), F32), jnp.zeros((t, WIDTH), F32),
           lax.empty((HEADS, t, 1), F32), jnp.zeros((HEADS, 1, t), F32)]
    for r in range(nseg):
        acc = segment(acc, r)
    return acc


def _lower_bound(lg_ref):
    l0 = lg_ref[0:1, :]
    l1 = lg_ref[1:2, :]
    m = jnp.maximum(l0, l1)
    e0 = jnp.exp(l0 - m)
    e1 = jnp.exp(l1 - m)
    return e0 / (e0 + e1)


def _hgrn_inputs(qb_ref, fb_ref, lg_ref, q_s, k_s, cum_s):
    lb = _lower_bound(lg_ref)
    sig = _sigmoid(fb_ref[...])
    f = lb + (1.0 - lb) * sig
    q_s[...] = _silu(qb_ref[...])
    k_s[...] = 1.0 - f
    cum_s[...] = _dot_hi(_tri(CHUNK, True), jnp.log(f))
    return lb, sig, f


def _boundary(cum_s, a):
    if a == 0:
        return jnp.zeros((1, HEAD_DIM), F32)
    return cum_s[pl.ds(SUB * a - 1, 1), :]


def _hgrn_scores(q_s, k_s, cum_s):
    cum = cum_s[...]
    kk = k_s[...]
    lane = lax.broadcasted_iota(jnp.int32, (SUB, CHUNK), 1)
    row = lax.broadcasted_iota(jnp.int32, (SUB, 1), 0)
    blocks = []
    for a in range(CHUNK // SUB):
        rows = pl.ds(SUB * a, SUB)
        ca = _boundary(cum_s, a)
        cum_a = cum_s[rows, :]
        q_a = q_s[rows, :]
        qa = q_a * jnp.exp(cum_a - ca)
        ka = kk * jnp.exp(jnp.minimum(ca - cum, 0.0))
        blk = lax.dot_general(qa, ka, _DN["nt"], preferred_element_type=F32)
        blk = jnp.where(lane < SUB * a, blk, 0.0)
        for s in range(SUB):
            r = SUB * a + s
            e = jnp.exp(jnp.minimum(cum_a - cum_s[pl.ds(r, 1), :], 0.0))
            col = jnp.sum(q_a * k_s[pl.ds(r, 1), :] * e, axis=-1, keepdims=True)
            col = jnp.where(row >= s, col, 0.0)
            blk = jnp.where(lane == r, col, blk)
        blocks.append(blk)
    return jnp.concatenate(blocks, axis=0)


def hgrn_fwd(proj, lb_logits, name, after=()):
    after = [m for m in after if m is not None]
    t = proj.shape[0]
    n = t // CHUNK
    cb = _col_blocks()
    dh = HEAD_DIM

    hb = min(HGRN_HEADS_PER_STEP, HEADS)
    w = hb * dh

    def one_head(qb_ref, fb_ref, ib_ref, lg_ref, o_ref, st_ref, a_ref, state, q_s, k_s, cum_s):
        _hgrn_inputs(qb_ref, fb_ref, lg_ref, q_s, k_s, cum_s)
        st = state[...]
        st_ref[...] = st
        cum = cum_s[...]
        v = ib_ref[...]
        qe = q_s[...] * jnp.exp(cum)
        inter = lax.dot_general(qe, st, _DN["nt"], preferred_element_type=F32)
        a_mat = _hgrn_scores(q_s, k_s, cum_s)
        a_ref[...] = a_mat
        o_ref[...] = inter + jnp.dot(a_mat, v, preferred_element_type=F32)
        last = cum_s[pl.ds(CHUNK - 1, 1), :]
        kd = k_s[...] * jnp.exp(last - cum)
        state[...] = st * jnp.exp(last) + lax.dot_general(v, kd, _DN["tn"], preferred_element_type=F32)

    def body(qb_ref, fb_ref, ib_ref, lg_ref, *rest):
        o_ref, st_ref, a_ref = rest[len(after):len(after) + 3]
        scratch = rest[len(after) + 3:]

        @pl.when(pl.program_id(1) == 0)
        def _():
            for j in range(hb):
                scratch[4 * j][...] = jnp.zeros((dh, dh), F32)

        for j in range(hb):
            cols = (slice(None), pl.ds(j * dh, dh))
            one_head(qb_ref.at[cols], fb_ref.at[cols], ib_ref.at[cols], lg_ref.at[cols], o_ref.at[cols],
                     st_ref.at[j], a_ref.at[j], *scratch[4 * j:4 * j + 4])

    blk = lambda off: pl.BlockSpec((CHUNK, w), lambda h, i: (i, off // hb + h))
    return pl.pallas_call(
        body, name=name, grid=(HEADS // hb, n),
        in_specs=[blk(cb["q_b"]), blk(cb["f_b"]), blk(cb["i_b"]), pl.BlockSpec((2, w), lambda h, i: (0, h))]
        + [pl.BlockSpec(memory_space=pl.ANY)] * len(after),
        out_specs=[pl.BlockSpec((CHUNK, w), lambda h, i: (i, h)),
                   pl.BlockSpec((hb, None, dh, dh), lambda h, i: (h, i, 0, 0)),
                   pl.BlockSpec((hb, None, CHUNK, CHUNK), lambda h, i: (h, i, 0, 0))],
        out_shape=[jax.ShapeDtypeStruct((t, WIDTH), F32), jax.ShapeDtypeStruct((HEADS, n, dh, dh), F32),
                   jax.ShapeDtypeStruct((HEADS, n, CHUNK, CHUNK), F32)],
        scratch_shapes=([pltpu.VMEM((dh, dh), F32)] + [pltpu.VMEM((CHUNK, dh), F32)] * 3) * hb,
        compiler_params=_params(("parallel", "arbitrary")),
    )(proj, proj, proj, lb_logits, *after)


def hgrn_bwd(proj, lb_logits, states, scores, do, name):
    t = proj.shape[0]
    n = t // CHUNK
    cb = _col_blocks()
    dh = HEAD_DIM
    nsub = CHUNK // SUB

    hb = min(HGRN_HEADS_PER_STEP, HEADS)
    w = hb * dh

    def one_head(qb_ref, fb_ref, ib_ref, lg_ref, st_ref, a_ref, do_ref, dqb_ref, dfb_ref, dib_ref, dlb_ref,
                 dstate, q_s, k_s, cum_s, da_s, dq_s, dk_s):
        lb, sig, f = _hgrn_inputs(qb_ref, fb_ref, lg_ref, q_s, k_s, cum_s)
        st = st_ref[...]
        dst = dstate[...]
        cum = cum_s[...]
        q = q_s[...]
        kk = k_s[...]
        v = ib_ref[...]
        dov = do_ref[...]
        e_cum = jnp.exp(cum)
        qe = q * e_cum
        last = cum_s[pl.ds(CHUNK - 1, 1), :]
        e_last = jnp.exp(last)
        e_tail = jnp.exp(last - cum)
        kd = kk * e_tail

        a_mat = a_ref[...]
        tri = _tri(CHUNK, True)
        da_s[...] = lax.dot_general(dov, v, _DN["nt"], preferred_element_type=F32) * tri
        dv = (lax.dot_general(a_mat, dov, _DN["tn"], preferred_element_type=F32)
              + lax.dot_general(kd, dst, _DN["nt"], preferred_element_type=F32))
        dk_state = jnp.dot(v, dst, preferred_element_type=F32) * e_tail
        dq_inter = jnp.dot(dov, st, preferred_element_type=F32) * e_cum
        dstate[...] = dst * e_last + lax.dot_general(dov, qe, _DN["tn"], preferred_element_type=F32)

        lane = lax.broadcasted_iota(jnp.int32, (SUB, CHUNK), 1)
        row = lax.broadcasted_iota(jnp.int32, (SUB, 1), 0)
        dk_s[...] = jnp.zeros_like(dk_s)
        for a in range(nsub):
            rows = pl.ds(SUB * a, SUB)
            ca = _boundary(cum_s, a)
            cum_a = cum_s[rows, :]
            q_a = q_s[rows, :]
            ea = jnp.exp(cum_a - ca)
            eb = jnp.exp(jnp.minimum(ca - cum, 0.0))
            da_a = da_s[rows, :]
            da_off = jnp.where(lane < SUB * a, da_a, 0.0)
            dq_a = ea * jnp.dot(da_off, kk * eb, preferred_element_type=F32)
            dk_s[...] += eb * lax.dot_general(da_off, q_a * ea, _DN["tn"], preferred_element_type=F32)
            dk_rows = jnp.zeros((SUB, dh), F32)
            for s in range(SUB):
                r = SUB * a + s
                e = jnp.exp(jnp.minimum(cum_a - cum_s[pl.ds(r, 1), :], 0.0))
                dcol = jnp.sum(jnp.where(lane == r, da_a, 0.0), axis=-1, keepdims=True)
                dcol = jnp.where(row >= s, dcol, 0.0)
                w = dcol * e
                dq_a = dq_a + w * k_s[pl.ds(r, 1), :]
                dk_rows = jnp.where(row == s, jnp.sum(w * q_a, axis=0, keepdims=True), dk_rows)
            dq_s[rows, :] = dq_a
            dk_s[rows, :] += dk_rows

        dq = dq_inter + dq_s[...]
        dk = dk_s[...] + dk_state
        d_last = (jnp.sum(dst * st, axis=0, keepdims=True) * e_last
                  + jnp.sum(kk * dk_state, axis=0, keepdims=True))
        rowc = lax.broadcasted_iota(jnp.int32, (CHUNK, 1), 0)
        dcum = q * dq - kk * dk + jnp.where(rowc == CHUNK - 1, d_last, 0.0)
        dg = _dot_hi(_tri(CHUNK, False), dcum)
        df = dg / f - dk
        dqb_ref[...] = (dq * _silu_grad(qb_ref[...])).astype(BF16)
        dfb_ref[...] = (df * (1.0 - lb) * sig * (1.0 - sig)).astype(BF16)
        dib_ref[...] = dv.astype(BF16)
        dlb_ref[...] += jnp.sum(df * (1.0 - sig), axis=0, keepdims=True)

    def body(qb_ref, fb_ref, ib_ref, lg_ref, st_ref, a_ref, do_ref, dqb_ref, dfb_ref, dib_ref, dlb_ref, *scratch):
        @pl.when(pl.program_id(1) == 0)
        def _():
            for j in range(hb):
                scratch[7 * j][...] = jnp.zeros((dh, dh), F32)
            dlb_ref[...] = jnp.zeros_like(dlb_ref)

        for j in range(hb):
            cols = (slice(None), pl.ds(j * dh, dh))
            one_head(qb_ref.at[cols], fb_ref.at[cols], ib_ref.at[cols], lg_ref.at[cols], st_ref.at[j], a_ref.at[j],
                     do_ref.at[cols], dqb_ref.at[cols], dfb_ref.at[cols], dib_ref.at[cols], dlb_ref.at[cols],
                     *scratch[7 * j:7 * j + 7])

    blk = lambda off: pl.BlockSpec((CHUNK, w), lambda h, i: (n - 1 - i, off // hb + h))
    out_blk = pl.BlockSpec((CHUNK, w), lambda h, i: (n - 1 - i, h))
    return pl.pallas_call(
        body, name=name, grid=(HEADS // hb, n),
        in_specs=[blk(cb["q_b"]), blk(cb["f_b"]), blk(cb["i_b"]), pl.BlockSpec((2, w), lambda h, i: (0, h)),
                  pl.BlockSpec((hb, None, dh, dh), lambda h, i: (h, n - 1 - i, 0, 0)),
                  pl.BlockSpec((hb, None, CHUNK, CHUNK), lambda h, i: (h, n - 1 - i, 0, 0)), out_blk],
        out_specs=[out_blk, out_blk, out_blk, pl.BlockSpec((1, w), lambda h, i: (0, h))],
        out_shape=[jax.ShapeDtypeStruct((t, WIDTH), BF16)] * 3 + [jax.ShapeDtypeStruct((1, WIDTH), F32)],
        scratch_shapes=([pltpu.VMEM((dh, dh), F32)] + [pltpu.VMEM((CHUNK, dh), F32)] * 3
                        + [pltpu.VMEM((CHUNK, CHUNK), F32)] + [pltpu.VMEM((CHUNK, dh), F32)] * 2) * hb,
        compiler_params=_params(("parallel", "arbitrary")),
    )(proj, proj, proj, lb_logits, states, scores, do)


def lb_bwd(dlb, lb_logits, name):
    def body(dlb_ref, lg_ref, o_ref):
        p0 = _lower_bound(lg_ref)
        d0 = dlb_ref[...] * p0 * (1.0 - p0)
        o_ref[0:1, :] = d0
        o_ref[1:2, :] = -d0

    return pl.pallas_call(body, name=name, out_shape=jax.ShapeDtypeStruct(lb_logits.shape, F32))(dlb, lb_logits)


def gnorm_fwd(o_raw, proj, norm_g, name, after=()):
    after = [m for m in after if m is not None]
    t = o_raw.shape[0]
    tr = _row_tile(t)
    cb = _col_blocks()
    dh = HEAD_DIM

    gcol = cb["g_b"] * LANES // WIDTH

    def body(o_ref, gb_ref, g_ref, *rest):
        for h in range(HEADS):
            cols = (slice(None), pl.ds(h * dh, dh))
            x = o_ref[cols]
            rest[-1][cols] = (x * _rstd(x) * g_ref[...] * _silu(gb_ref[cols])).astype(BF16)

    return pl.pallas_call(
        body, name=name, grid=(t // tr,),
        in_specs=[pl.BlockSpec((tr, WIDTH), lambda i: (i, 0)), pl.BlockSpec((tr, WIDTH), lambda i: (i, gcol)),
                  pl.BlockSpec((1, dh), lambda i: (0, 0))] + [pl.BlockSpec(memory_space=pl.ANY)] * len(after),
        out_specs=pl.BlockSpec((tr, WIDTH), lambda i: (i, 0)),
        out_shape=jax.ShapeDtypeStruct((t, WIDTH), BF16), compiler_params=_params(("parallel",)),
    )(o_raw, proj, norm_g, *after)


def gnorm_bwd(dy_ab, o_raw, proj, norm_g, name):
    t = o_raw.shape[0]
    tr = _row_tile(t)
    cb = _col_blocks()
    dh = HEAD_DIM
    gcol = cb["g_b"] * LANES // WIDTH

    def body(dy_ref, o_ref, gb_ref, g_ref, do_ref, dgb_ref, dg_ref):
        @pl.when(pl.program_id(0) == 0)
        def _():
            dg_ref[...] = jnp.zeros_like(dg_ref)

        g = g_ref[...]
        acc = jnp.zeros_like(g)
        for h in range(HEADS):
            cols = (slice(None), pl.ds(h * dh, dh))
            x = o_ref[cols]
            gb = gb_ref[cols]
            dyv = dy_ref[cols]
            dx, dg = _rms_bwd(x, g, dyv * _silu(gb))
            do_ref[cols] = dx
            dgb_ref[cols] = (dyv * (x * _rstd(x) * g) * _silu_grad(gb)).astype(BF16)
            acc = acc + dg
        dg_ref[...] += acc

    rows = pl.BlockSpec((tr, WIDTH), lambda i: (i, 0))
    vec = pl.BlockSpec((1, dh), lambda i: (0, 0))
    return pl.pallas_call(
        body, name=name, grid=(t // tr,),
        in_specs=[pl.BlockSpec((None, tr, WIDTH), lambda i: (1, i, 0)), rows,
                  pl.BlockSpec((tr, WIDTH), lambda i: (i, gcol)), vec],
        out_specs=[rows, rows, vec],
        out_shape=[jax.ShapeDtypeStruct((t, WIDTH), F32), jax.ShapeDtypeStruct((t, WIDTH), BF16),
                   jax.ShapeDtypeStruct((1, dh), F32)],
        compiler_params=_params(("arbitrary",)),
    )(dy_ab, o_raw, proj, norm_g)


def merge_fwd(proj, y, name):
    _, t, d = y.shape
    tr = _row_tile(t)
    tc = _pick(d, (1024, 512, 256, 128))
    cb = _col_blocks()
    ga, gb = cb["gate_a"] * LANES // tc, cb["gate_b"] * LANES // tc

    def body(ga_ref, gb_ref, y_ref, o_ref):
        o_ref[...] = (_sigmoid(ga_ref[...]) * y_ref[0] + _sigmoid(gb_ref[...]) * y_ref[1]).astype(BF16)

    return pl.pallas_call(
        body, name=name, grid=(t // tr, d // tc),
        in_specs=[pl.BlockSpec((tr, tc), lambda i, j: (i, ga + j)), pl.BlockSpec((tr, tc), lambda i, j: (i, gb + j)),
                  pl.BlockSpec((2, tr, tc), lambda i, j: (0, i, j))],
        out_specs=pl.BlockSpec((tr, tc), lambda i, j: (i, j)),
        out_shape=jax.ShapeDtypeStruct((t, d), BF16), compiler_params=_params(("parallel", "parallel")),
    )(proj, proj, y)


def merge_bwd(dm, proj, y, name):
    _, t, d = y.shape
    tr = _row_tile(t)
    tc = _pick(d, (1024, 512, 256, 128))
    cb = _col_blocks()
    ga, gb = cb["gate_a"] * LANES // tc, cb["gate_b"] * LANES // tc

    def body(dm_ref, ga_ref, gb_ref, y_ref, dg_ref, dy_ref):
        dmv = dm_ref[...]
        for idx, g_ref in enumerate((ga_ref, gb_ref)):
            s = _sigmoid(g_ref[...])
            dg_ref[idx] = (dmv * y_ref[idx] * s * (1.0 - s)).astype(BF16)
            dy_ref[idx] = (dmv * s).astype(BF16)

    pair = pl.BlockSpec((2, tr, tc), lambda i, j: (0, i, j))
    return pl.pallas_call(
        body, name=name, grid=(t // tr, d // tc),
        in_specs=[pl.BlockSpec((tr, tc), lambda i, j: (i, j)), pl.BlockSpec((tr, tc), lambda i, j: (i, ga + j)),
                  pl.BlockSpec((tr, tc), lambda i, j: (i, gb + j)), pair],
        out_specs=[pair, pair],
        out_shape=[jax.ShapeDtypeStruct((2, t, d), BF16)] * 2, compiler_params=_params(("parallel", "parallel")),
    )(dm, proj, proj, y)


def ple_tail(h, a, b, g, target, name):
    t, d = h.shape
    tr = _row_tile(t)

    def body(h_ref, a_ref, b_ref, g_ref, t_ref, loss_ref, dh_ref, da_ref, db_ref, dg_ref):
        @pl.when(pl.program_id(0) == 0)
        def _():
            loss_ref[...] = jnp.zeros_like(loss_ref)
            dg_ref[...] = jnp.zeros_like(dg_ref)

        s = _sigmoid(a_ref[...])
        bv = b_ref[...]
        z = s * bv
        gv = g_ref[...]
        err = h_ref[...] + z * _rstd(z) * gv - t_ref[...]
        loss_ref[...] += 0.5 * jnp.sum(jnp.sum(err * err, axis=-1, keepdims=True), axis=0, keepdims=True) / d
        dh = err / d
        dh_ref[...] = dh
        dz, dg = _rms_bwd(z, gv, dh)
        da_ref[...] = (dz * bv * s * (1.0 - s)).astype(BF16)
        db_ref[...] = (dz * s).astype(BF16)
        dg_ref[...] += dg

    row = pl.BlockSpec((tr, d), lambda i: (i, 0))
    vec = pl.BlockSpec((1, d), lambda i: (0, 0))
    return pl.pallas_call(
        body, name=name, grid=(t // tr,), in_specs=[row, row, row, vec, row],
        out_specs=[pl.BlockSpec((1, 1), lambda i: (0, 0)), row, row, row, vec],
        out_shape=[jax.ShapeDtypeStruct((1, 1), F32), jax.ShapeDtypeStruct((t, d), F32),
                   jax.ShapeDtypeStruct((t, d), BF16), jax.ShapeDtypeStruct((t, d), BF16),
                   jax.ShapeDtypeStruct((1, d), F32)],
        compiler_params=_params(("arbitrary",)),
    )(h, a, b, g, target)


def _ffn_fwd(h, u, post_g, next_g, get_w, idx, tag):
    gu = None
    for i, key in enumerate(GATE_UP_KEYS["gu" + idx]):
        w = get_w(key, h if gu is None else gu)
        gu = mm_nn_col(u, w, F32, "%s_gate_up_%d" % (tag, i), slot0=i, total=2, into=gu)
    act = swiglu_act(gu, tag + "_act")
    y = mm_nn_2d(act, get_w("down" + idx, gu), F32, tag + "_down")
    out, u_next = resid_norm(h, y, post_g, MACARON_SCALE, next_g, tag + "_out")
    return out, u_next, (h, u, gu, act, y)


def _ffn_bwd(dh, saved, pre_g, post_g, get_w, emit, advance, idx, tag):
    h, u, gu, act, y = saved
    dy, d_post = post_bwd(dh, y, post_g, MACARON_SCALE, tag + "_post_bwd")
    m1 = emit("down" + idx, mm_tn_2d(act, dy, F32, tag + "_dw_down"))
    dact = mm_nt_2d(dy, get_w("down" + idx), F32, tag + "_dact", after=[m1])
    m2 = advance(dact)
    dgu = swiglu_bwd(dact, gu, tag + "_act_bwd")
    m3 = emit("gu" + idx, mm_tn_col(u, dgu, N_CHIPS, F32, tag + "_dw_gate_up"))
    du = None
    for i, key in enumerate(GATE_UP_KEYS["gu" + idx]):
        du = mm_nt_col(dgu, get_w(key), F32, "%s_du_%d" % (tag, i), after=[m2, m3] if du is None else (),
                       slot0=i, init=du)
    m4 = advance(du)
    dh_in, d_pre = pre_bwd(dh, h, pre_g, [du], tag + "_pre_bwd", after=[m4])
    return dh_in, d_pre, d_post


def _heads_col(a):
    t = a.shape[0]
    at = a[:, :HEADS].T
    return at.reshape(HEADS, t, 1), at.reshape(HEADS, 1, t)


def layer_step(x, p, target, gains, fox_bias, lb_logits, norm_g, get_w, emit, advance, early):
    t = x.shape[0]
    u1 = norm_in(x, gains["ffn1_pre"], "ffn1_norm")
    h1, u2, s1 = _ffn_fwd(x, u1, gains["ffn1_post"], gains["mix_pre"], get_w, "1", "ffn1")

    proj = mm_nt_2d(u2, get_w("in_main", h1), F32, "mix_in")
    fa = mm_nt_2d(u2, get_w("in_fa"), F32, "mix_in_fa")
    mark = early("proj", proj)
    c = fox_prep(fa, fox_bias, "fox_prep")
    c_col, c_row = _heads_col(c)
    o_a = fox_fwd(proj, c_col, c_row, "fox_fwd")
    o_raw, states, scores = hgrn_fwd(proj, lb_logits, "hgrn_fwd", after=[mark])
    mark = early("gu2", o_raw)
    o_b = gnorm_fwd(o_raw, proj, norm_g, "hgrn_norm", after=[mark])
    o_ab = jnp.stack([o_a, o_b])
    y_ab = _mm_branches(o_ab, get_w("proj", proj), "mix_proj")
    merged = merge_fwd(proj, y_ab, "mix_merge")
    mo = mm_nn_2d(merged, get_w("out"), F32, "mix_out")
    h2, u3 = resid_norm(h1, mo, gains["mix_post"], 1.0, gains["ffn2_pre"], "mix_resid")

    h3, u4, s3 = _ffn_fwd(h2, u3, gains["ffn2_post"], gains["ple_pre"], get_w, "2", "ffn2")

    a4 = mm_nn_2d(u4, get_w("ple_gate"), F32, "ple_gate")
    b4 = mm_nn_col(p, get_w("ple_proj"), F32, "ple_proj")[0]
    loss, dh4, da4, db4, d_ple_post = ple_tail(h3, a4, b4, gains["ple_post"], target, "ple_tail")

    marks = [emit("ple_gate", mm_tn_2d(u4, da4, F32, "ple_dw_gate")),
             emit("ple_proj", mm_tn_col(p, db4[None], N_CHIPS, F32, "ple_dw_proj"))]
    du4 = mm_nt_2d(da4, get_w("ple_gate"), F32, "ple_du", after=marks)
    dh3, d_ple_pre = pre_bwd(dh4, h3, gains["ple_pre"], [du4], "ple_pre_bwd", after=[advance(du4)])

    dh2, d_f2_pre, d_f2_post = _ffn_bwd(dh3, s3, gains["ffn2_pre"], gains["ffn2_post"], get_w, emit, advance,
                                        "2", "ffn2")

    dmo, d_mix_post = post_bwd(dh2, mo, gains["mix_post"], 1.0, "mix_post_bwd")
    marks = [emit("out", mm_tn_2d(merged, dmo, F32, "mix_dw_out"))]
    dmerged = mm_nt_2d(dmo, get_w("out"), F32, "mix_dmerged", after=marks)
    dgate, dy_ab = merge_bwd(dmerged, proj, y_ab, "mix_merge_bwd")
    marks = [advance(dmerged), emit("proj", _mm_branches_dw(o_ab, dy_ab, "mix_dw_proj"))]
    do_ab = _mm_branches_bwd(dy_ab, get_w("proj"), "mix_do")
    do_raw, dg_b, d_norm_g = gnorm_bwd(do_ab, o_raw, proj, norm_g, "hgrn_norm_bwd")
    dq_b, df_b, di_b, dlb = hgrn_bwd(proj, lb_logits, states, scores, do_raw, "hgrn_bwd")
    d_lb_logits = lb_bwd(dlb, lb_logits, "lb_bwd")
    dq_a, dk_a, dv_a, dc_col, dc_row = fox_bwd(proj, c_col, c_row, do_ab, "fox_bwd")
    dc = (dc_col.reshape(HEADS, t) + dc_row.reshape(HEADS, t)).T
    dc = jnp.pad(dc, ((0, 0), (0, LANES - HEADS)))
    dfa, d_fox_bias = fox_post_bwd(dc, fa, fox_bias, "fox_post_bwd")
    dproj = jnp.concatenate([dq_a, dk_a.astype(BF16), dv_a.astype(BF16), dq_b, df_b, di_b, dg_b,
                             dgate[0], dgate[1]], axis=1)
    marks.append(emit("in_main", mm_tn_2d(dproj, u2, F32, "mix_dw_in")))
    marks.append(emit("in_fa", mm_tn_2d(dfa, u2, F32, "mix_dw_in_fa")))
    du2a = mm_nn_2d(dproj, get_w("in_main"), F32, "mix_du", after=marks)
    du2b = mm_nn_2d(dfa, get_w("in_fa"), F32, "mix_du_fa")
    dh1, d_mix_pre = pre_bwd(dh2, h1, gains["mix_pre"], [du2a, du2b], "mix_pre_bwd", after=[advance(du2a)])

    dx, d_f1_pre, d_f1_post = _ffn_bwd(dh1, s1, gains["ffn1_pre"], gains["ffn1_post"], get_w, emit, advance,
                                       "1", "ffn1")

    small = dict(ffn1_pre=d_f1_pre, ffn1_post=d_f1_post, mix_pre=d_mix_pre, mix_post=d_mix_post,
                 ffn2_pre=d_f2_pre, ffn2_post=d_f2_post, ple_pre=d_ple_pre, ple_post=d_ple_post,
                 fox_bias=d_fox_bias, lb_logits=d_lb_logits, norm_g=d_norm_g)
    return loss, dx, small


def _mm_branches(o_ab, w_proj, name):
    g, t, kk = o_ab.shape
    _, jn, _, ns = w_proj.shape
    tm = _pick(t, (512, 256, 128))
    return _mm(o_ab, w_proj, mode="nn", grid=(t // tm, g * jn, 1),
               a_spec=pl.BlockSpec((None, tm, kk), lambda i, j, k: (j // jn, i, 0)),
               b_spec=pl.BlockSpec((None, None, kk, ns), lambda i, j, k: (j // jn, j % jn, 0, 0)),
               o_spec=pl.BlockSpec((None, tm, ns), lambda i, j, k: (j // jn, i, j % jn)),
               out_shape=jax.ShapeDtypeStruct((g, t, jn * ns), F32), acc_shape=(tm, ns), name=name)


def _mm_branches_bwd(dy_ab, w_proj, name):
    g, t, _ = dy_ab.shape
    _, jn, kk, ns = w_proj.shape
    tm = _pick(t, (512, 256, 128))
    return _mm(dy_ab, w_proj, mode="nt", grid=(t // tm, g, jn),
               a_spec=pl.BlockSpec((None, tm, ns), lambda i, j, k: (j, i, k)),
               b_spec=pl.BlockSpec((None, None, kk, ns), lambda i, j, k: (j, k, 0, 0)),
               o_spec=pl.BlockSpec((None, tm, kk), lambda i, j, k: (j, i, 0)),
               out_shape=jax.ShapeDtypeStruct((g, t, kk), F32), acc_shape=(tm, kk), name=name)


def _mm_branches_dw(o_ab, dy_ab, name):
    g, t, kk = o_ab.shape
    d = dy_ab.shape[2]
    jn = N_CHIPS
    ns = d // jn
    return _mm(o_ab, dy_ab, mode="tn", grid=(1, g * jn, 1),
               a_spec=pl.BlockSpec((None, t, kk), lambda i, j, k: (j // jn, 0, 0)),
               b_spec=pl.BlockSpec((None, t, ns), lambda i, j, k: (j // jn, 0, j % jn)),
               o_spec=pl.BlockSpec((None, None, kk, ns), lambda i, j, k: (j // jn, j % jn, 0, 0)),
               out_shape=jax.ShapeDtypeStruct((g, jn, kk, ns), F32), acc_shape=(kk, ns), name=name)


HBM_SPEC = pl.BlockSpec(memory_space=pltpu.HBM)
SEM_SPEC = pl.BlockSpec(memory_space=pltpu.SEMAPHORE)
ANY_SPEC = pl.BlockSpec(memory_space=pl.ANY)
EFFECT = pltpu.SideEffectType.DATAFLOW_SIDE_EFFECTING


def _in_hbm(a):
    return pltpu.with_memory_space_constraint(a, pltpu.HBM)


def _place():
    x, y, c = lax.axis_index("x"), lax.axis_index("y"), lax.axis_index("c")
    chips = [(1 - x, y), (x, 1 - y), (1 - x, 1 - y)]
    return x, y, c, chips


def _half(shape, which, axis):
    n = shape[-2 + axis] // 2
    cut = pl.ds(which * n, n)
    return (cut, slice(None)) if axis == 0 else (slice(None), cut)


def _half_shape(shape, axis):
    s = list(shape)
    s[len(s) - 2 + axis] //= 2
    return tuple(s)


def _remote(src, dst, send_sems, recv_sems, k, to):
    return pltpu.make_async_remote_copy(src_ref=src, dst_ref=dst, send_sem=send_sems.at[k], recv_sem=recv_sems.at[k],
                                        device_id=to, device_id_type=MESH)


def split_start(name, srcs, lands, counts, copies):
    ns, nl, nset = len(srcs), len(lands), len(counts)

    def body(*refs):
        src_refs, land_refs = refs[:ns], refs[ns:ns + nl]
        sems = refs[ns + nl:ns + nl + 2 * nset]
        for s, plan in enumerate(copies(src_refs, land_refs)):
            for k, (src, dst, to) in enumerate(plan):
                _remote(src, dst, sems[2 * s], sems[2 * s + 1], k, to).start()
        refs[-1][...] = jnp.zeros_like(refs[-1])

    out_shape = []
    for n in counts:
        out_shape += [pltpu.SemaphoreType.DMA((n,)), pltpu.SemaphoreType.DMA((n,))]
    out_shape += [pltpu.HBM(a.shape, a.dtype) for a in list(srcs) + list(lands)]
    out_shape.append(jax.ShapeDtypeStruct((8, LANES), F32))
    res = pl.pallas_call(
        body, name=name, out_shape=tuple(out_shape), in_specs=[HBM_SPEC] * (ns + nl),
        out_specs=tuple([SEM_SPEC] * (2 * nset) + [HBM_SPEC] * (ns + nl) + [pl.BlockSpec(memory_space=pltpu.VMEM)]),
        input_output_aliases={i: 2 * nset + i for i in range(ns + nl)},
        compiler_params=pltpu.CompilerParams(has_side_effects=EFFECT),
    )(*[_in_hbm(a) for a in list(srcs) + list(lands)])
    sems = [(res[2 * s], res[2 * s + 1]) for s in range(nset)]
    return sems, list(res[2 * nset:2 * nset + ns]), list(res[2 * nset + ns:-1]), res[-1]


def split_wait(name, srcs, lands, sems, afters, copies):
    afters = [a for a in afters if a is not None]
    ns, nl, na = len(srcs), len(lands), len(afters)

    def body(*refs):
        src_refs, land_refs = refs[:ns], refs[ns:ns + nl]
        send_sems, recv_sems = refs[ns + nl:ns + nl + 2]
        for k, (src, dst, to) in enumerate(copies(src_refs, land_refs)):
            cp = _remote(src, dst, send_sems, recv_sems, k, to)
            cp.wait_send()
            cp.wait_recv()

    res = pl.pallas_call(
        body, name=name, out_shape=tuple(pltpu.HBM(a.shape, a.dtype) for a in list(srcs) + list(lands)),
        in_specs=[HBM_SPEC] * (ns + nl) + [SEM_SPEC, SEM_SPEC] + [ANY_SPEC] * na,
        out_specs=tuple([HBM_SPEC] * (ns + nl)), input_output_aliases={i: i for i in range(ns + nl)},
        compiler_params=pltpu.CompilerParams(has_side_effects=EFFECT),
    )(*srcs, *lands, sems[0], sems[1], *afters)
    return list(res[:ns]), list(res[ns:])


def _gather_plan(blocks):
    def copies(src_refs, land_refs):
        x, y, c, chips = _place()
        j_me = 2 * x + y
        plan = []
        for si, li, g, axis in blocks:
            src, land = src_refs[si], land_refs[li].at[g]
            mine = _half(src.shape, c, axis)
            for px, py in chips:
                plan.append((src.at[mine], land.at[(j_me,) + mine], (px, py, c)))
            plan.append((src, land.at[j_me], (x, y, 1 - c)))
        return plan
    return copies


def _gather_arrivals(blocks):
    def copies(src_refs, land_refs):
        x, y, c, chips = _place()
        j_me = 2 * x + y
        plan = []
        for si, li, g, axis in blocks:
            src, land = src_refs[si], land_refs[li].at[g]
            mine = _half(src.shape, c, axis)
            for px, py in chips:
                plan.append((src.at[mine], land.at[(2 * px + py,) + mine], (px, py, c)))
            plan.append((src, land.at[j_me], (x, y, 1 - c)))
        return plan
    return copies


def _pass_plan(blocks, arrivals):
    def copies(src_refs, land_refs):
        x, y, c, chips = _place()
        plan = []
        for li, g, axis in blocks:
            land = src_refs[li].at[g]
            half = _half(land.shape[1:], (1 - c) if arrivals else c, axis)
            for px, py in chips:
                part = land.at[(2 * px + py,) + half]
                plan.append((part, part, (x, y, 1 - c)))
        return plan
    return copies


def gather_pass(name, lands, blocks):
    n = len(lands)

    def body(*refs):
        outs = refs[n:2 * n]
        send_sems, recv_sems = refs[2 * n:]
        x, y, c, chips = _place()
        sent = []
        for i, (li, g, axis) in enumerate(blocks):
            land = outs[li].at[g]
            mine = _half(land.shape[1:], c, axis)
            for k, (px, py) in enumerate(chips):
                part = land.at[(2 * px + py,) + mine]
                cp = _remote(part, part, send_sems, recv_sems, 3 * i + k, (x, y, 1 - c))
                cp.start()
                sent.append(cp)
        for i, (li, g, axis) in enumerate(blocks):
            land = outs[li].at[g]
            other = _half(land.shape[1:], 1 - c, axis)
            for k, (px, py) in enumerate(chips):
                part = land.at[(2 * px + py,) + other]
                _remote(part, part, send_sems, recv_sems, 3 * i + k, (x, y, 1 - c)).wait_recv()
        for cp in sent:
            cp.wait_send()

    m = 3 * len(blocks)
    return pl.pallas_call(
        body, name=name, in_specs=[ANY_SPEC] * n, out_specs=[ANY_SPEC] * n,
        out_shape=[jax.ShapeDtypeStruct(a.shape, a.dtype) for a in lands],
        input_output_aliases={i: i for i in range(n)},
        scratch_shapes=[pltpu.SemaphoreType.DMA((m,)), pltpu.SemaphoreType.DMA((m,))],
    )(*lands)


def _pair_plan(axes):
    def copies(src_refs, land_refs):
        x, y, c, _ = _place()
        return [(src_refs[i].at[(slice(None), slice(None)) + _half(src_refs[i].shape, 1 - c, a)], land_refs[i],
                 (x, y, 1 - c)) for i, a in enumerate(axes)]
    return copies


def _scatter_plan(n):
    def copies(src_refs, land_refs):
        x, y, c, chips = _place()
        return [(src_refs[i].at[:, 2 * px + py], land_refs[i].at[k], (px, py, c))
                for i in range(n) for k, (px, py) in enumerate(chips)]
    return copies


def _broadcast_plan(axes, arrivals):
    def copies(src_refs, land_refs):
        x, y, c, _ = _place()
        plan = []
        for i, a in enumerate(axes):
            part = src_refs[i].at[(slice(None),) + _half(src_refs[i].shape, (1 - c) if arrivals else c, a)]
            plan.append((part, part, (x, y, 1 - c)))
        return plan
    return copies


N_DEV = 8
SLAB_ROWS = 16


def allreduce_small(slab, after):
    def body(x_ref, after_ref, o_ref, land, send_sems, recv_sems):
        x, y, c, _ = _place()
        me = 4 * x + 2 * y + c
        land[me] = x_ref[...]
        copies = []
        for d in range(1, N_DEV):
            to = (me + d) % N_DEV
            cp = pltpu.make_async_remote_copy(
                src_ref=x_ref, dst_ref=land.at[me], send_sem=send_sems.at[d - 1], recv_sem=recv_sems.at[me],
                device_id=(to // 4, (to // 2) % 2, to % 2), device_id_type=MESH)
            cp.start()
            copies.append(cp)
        for d in range(1, N_DEV):
            frm = (me + d) % N_DEV
            pltpu.make_async_remote_copy(
                src_ref=x_ref, dst_ref=land.at[frm], send_sem=send_sems.at[d - 1], recv_sem=recv_sems.at[frm],
                device_id=(frm // 4, (frm // 2) % 2, frm % 2), device_id_type=MESH).wait_recv()
        for cp in copies:
            cp.wait_send()
        acc = land[0]
        for s in range(1, N_DEV):
            acc = acc + land[s]
        o_ref[...] = acc

    vm = pl.BlockSpec(memory_space=pltpu.VMEM)
    return pl.pallas_call(
        body, name="allreduce_small", in_specs=[vm, ANY_SPEC], out_specs=vm,
        out_shape=jax.ShapeDtypeStruct(slab.shape, F32),
        scratch_shapes=[pltpu.VMEM((N_DEV,) + slab.shape, F32), pltpu.SemaphoreType.DMA((N_DEV - 1,)),
                        pltpu.SemaphoreType.DMA((N_DEV,))],
    )(slab, after)


BLOCK_BYTES = 3 * 1024 * 1024


def _tiles_2d(r, c, budget=BLOCK_BYTES):
    if r % 8 == 0:
        tc = c if c % LANES else _pick(c, (2048, 1408, 1024, 512, 256, 128))
        tr = 8
        for cand in (512, 256, 128, 64, 32, 16, 8):
            if r % cand == 0 and cand * tc * 4 <= budget:
                tr = cand
                break
        if tr >= 64 or c % LANES or r * LANES * 4 > budget:
            return tr, tc
    tc = LANES
    for cand in (1024, 512, 256, 128):
        if c % cand == 0 and r * cand * 4 <= budget:
            tc = cand
            break
    return r, tc


def _grid_spec(grid, in_specs, out_specs):
    return pltpu.PrefetchScalarGridSpec(num_scalar_prefetch=1, grid=grid, in_specs=in_specs, out_specs=out_specs)


def _own(axis, nr, nc):
    if axis == 0:
        return lambda i, j, where: (where[1] * nr + i, j)
    return lambda i, j, where: (i, where[1] * nc + j)


def pair_add(where, grad, recv, axis, name):
    g, jn, hr, hc = recv.shape
    tr, tc = _tiles_2d(hr, hc)
    nr, nc = hr // tr, hc // tc
    own = _own(axis, nr, nc)
    others = jn - 1

    def body(where_ref, a_ref, b_ref, o_ref):
        o_ref[...] = (a_ref[...] + b_ref[...]).astype(BF16)

    def block(a, where):
        return a // others, (where[0] + 1 + a % others) % jn

    blk = pl.BlockSpec((None, None, tr, tc), lambda a, i, j, where: block(a, where) + (i, j))
    mine = pl.BlockSpec((None, None, tr, tc), lambda a, i, j, where: block(a, where) + own(i, j, where))
    return pl.pallas_call(
        body, name=name, grid_spec=_grid_spec((g * others, nr, nc), [mine, blk], blk),
        out_shape=jax.ShapeDtypeStruct(recv.shape, BF16),
        compiler_params=_params(("parallel", "parallel", "parallel")),
    )(where, grad, recv)


def chip_add(where, grad, pair, recv, axis, name):
    g, jn, hr, hc = pair.shape
    tr, tc = _tiles_2d(hr, hc)
    nr, nc = hr // tr, hc // tc
    own = _own(axis, nr, nc)
    full = (g, 2 * hr, hc) if axis == 0 else (g, hr, 2 * hc)

    def body(where_ref, a_ref, p_ref, b_ref, o_ref):
        s = a_ref[...] + p_ref[...]
        for k in range(3):
            s = s + b_ref[k].astype(F32)
        o_ref[...] = s

    return pl.pallas_call(
        body, name=name,
        grid_spec=_grid_spec((g, nr, nc),
                             [pl.BlockSpec((None, None, tr, tc), lambda a, i, j, where: (a, where[0]) + own(i, j, where)),
                              pl.BlockSpec((None, None, tr, tc), lambda a, i, j, where: (a, where[0], i, j)),
                              pl.BlockSpec((3, None, tr, tc), lambda a, i, j, where: (0, a, i, j))],
                             pl.BlockSpec((None, tr, tc), lambda a, i, j, where: (a,) + own(i, j, where))),
        out_shape=jax.ShapeDtypeStruct(full, F32), compiler_params=_params(("parallel", "parallel", "parallel")),
    )(where, grad, pair, recv)


def _adam_math(w, g, m, v):
    m2 = ADAM_B1 * m + (1.0 - ADAM_B1) * g
    v2 = ADAM_B2 * v + (1.0 - ADAM_B2) * (g * g)
    m_hat = m2 / (1.0 - ADAM_B1 ** ADAM_STEP)
    v_hat = v2 / (1.0 - ADAM_B2 ** ADAM_STEP)
    delta = -ADAM_LR * (m_hat / (jnp.sqrt(v_hat) + ADAM_EPS) + ADAM_WD * w)
    return delta, m2, v2


def adamw(grad, idx, w, m, v, name):
    _, r, cc = w.shape
    rg = grad.shape[1]
    tr, tc = _tiles_2d(r, cc, BLOCK_BYTES // 2)
    assert rg == r or tr == r
    gr = tr if rg == r else rg

    def body(g_ref, w_ref, m_ref, v_ref, go_ref, d_ref, mo_ref, vo_ref):
        g = g_ref[pl.ds(0, tr), :]
        delta, m2, v2 = _adam_math(w_ref[...], g, m_ref[...], v_ref[...])
        go_ref[...] = g
        d_ref[...] = delta
        mo_ref[...] = m2
        vo_ref[...] = v2

    blk = pl.BlockSpec((None, tr, tc), lambda i, j: (0, i, j))
    return pl.pallas_call(
        body, name=name, grid=(r // tr, cc // tc),
        in_specs=[pl.BlockSpec((None, gr, tc), lambda i, j: (idx, i, j)), blk, blk, blk], out_specs=[blk] * 4,
        out_shape=[jax.ShapeDtypeStruct(w.shape, F32)] * 4, compiler_params=_params(("parallel", "parallel")),
    )(grad, w, m, v)


def adamw_small(g, w, m, v):
    def body(g_ref, w_ref, m_ref, v_ref, d_ref, mo_ref, vo_ref):
        delta, m2, v2 = _adam_math(w_ref[...], g_ref[...], m_ref[...], v_ref[...])
        d_ref[...] = delta
        mo_ref[...] = m2
        vo_ref[...] = v2

    return pl.pallas_call(body, name="adamw_small", out_shape=[jax.ShapeDtypeStruct(w.shape, F32)] * 3)(g, w, m, v)


GAINS = ("ffn1_pre", "ffn1_post", "mix_pre", "mix_post", "ffn2_pre", "ffn2_post", "ple_pre", "ple_post")
WEIGHTS = ("ffn1_pre_g", "ffn1_post_g", "ffn1_w_gate", "ffn1_w_up", "ffn1_w_down", "mix_pre_g", "mix_post_g",
           "mix_w_in", "fox_f_bias", "hgrn_lb_logits", "hgrn_norm_g", "mix_w_proj_fox", "mix_w_proj_hgrn",
           "mix_w_out", "ffn2_pre_g", "ffn2_post_g", "ffn2_w_gate", "ffn2_w_up", "ffn2_w_down", "ple_pre_g",
           "ple_post_g", "ple_w_gate", "ple_w_proj")
GROUPS = dict(gu1=(("ffn1_w_gate", "ffn1_w_up"), 0), down1=(("ffn1_w_down",), 0), win=(("mix_w_in",), 1),
              proj=(("mix_w_proj_fox", "mix_w_proj_hgrn"), 0), out=(("mix_w_out",), 0),
              gu2=(("ffn2_w_gate", "ffn2_w_up"), 0), down2=(("ffn2_w_down",), 0), ple_gate=(("ple_w_gate",), 0),
              ple_proj=(("ple_w_proj",), 0))
TRANSPOSED = ("mix_w_in",)
ROW_BLOCKS = ("down1", "down2", "out", "ple_gate")
GATHER_SETS = (("gate1",), ("up1",), ("down1",), ("win",), ("proj", "out"), ("gu2", "down2", "ple_gate", "ple_proj"))
GATHER_GROUPS = dict(GROUPS, gate1=(("ffn1_w_gate",), 0), up1=(("ffn1_w_up",), 0))
GATE_UP_KEYS = dict(gu1=("gate1", "up1"), gu2=("gu2",))
REDUCE_SETS = (("ple_gate", "ple_proj", "down2", "gu2"), ("out", "proj", "win"), ("down1",), ("gu1",))


def _pad_row(a, width):
    a = a.reshape(1, -1)
    return jnp.pad(a, ((0, 0), (0, width - a.shape[1])))


def _pack_small(vals):
    d = D_MODEL
    rows = [vals[n + "_g"].reshape(1, d) for n in GAINS]
    rows.append(_pad_row(vals["fox_f_bias"], d))
    lg = vals["hgrn_lb_logits"]
    rows += [_pad_row(lg[0], d), _pad_row(lg[1], d), _pad_row(vals["hgrn_norm_g"], d)]
    slab = jnp.concatenate(rows, axis=0)
    return jnp.pad(slab, ((0, SLAB_ROWS - slab.shape[0]), (0, 0)))


def _unpack_small(slab):
    out = {n + "_g": slab[i:i + 1] for i, n in enumerate(GAINS)}
    out["fox_f_bias"] = slab[8:9, :HEADS]
    out["hgrn_lb_logits"] = slab[9:11, :WIDTH]
    out["hgrn_norm_g"] = slab[11:12, :HEAD_DIM]
    return out


def _split_in(win_t):
    lo = 3 * WIDTH
    main = jnp.concatenate([win_t[:lo], win_t[lo + HEADS:]], axis=0)
    fa = jnp.pad(win_t[lo:lo + HEADS], ((0, LANES - HEADS), (0, 0)))
    return main, fa


def _join_in(main, fa):
    lo = 3 * WIDTH
    return jnp.concatenate([main[:lo], fa[:HEADS], main[lo:]], axis=0)


def _as_block(name, a):
    return jnp.swapaxes(a, 1, 2) if name in TRANSPOSED else a


def _send_block(name, a, mark):
    blk = _as_block(name, a)[0]
    if mark is not None:
        blk = blk + mark[0, 0]
    return blk.astype(BF16)


def kernel(x, p, ffn1_pre_g, ffn1_post_g, ffn1_w_gate, ffn1_w_up, ffn1_w_down, mix_pre_g, mix_post_g, mix_w_in, fox_f_bias, hgrn_lb_logits, hgrn_norm_g, mix_w_proj_fox, mix_w_proj_hgrn, mix_w_out, ffn2_pre_g, ffn2_post_g, ffn2_w_gate, ffn2_w_up, ffn2_w_down, ple_pre_g, ple_post_g, ple_w_gate, ple_w_proj, loss_target, m_ffn1_pre_g, m_ffn1_post_g, m_ffn1_w_gate, m_ffn1_w_up, m_ffn1_w_down, m_mix_pre_g, m_mix_post_g, m_mix_w_in, m_fox_f_bias, m_hgrn_lb_logits, m_hgrn_norm_g, m_mix_w_proj_fox, m_mix_w_proj_hgrn, m_mix_w_out, m_ffn2_pre_g, m_ffn2_post_g, m_ffn2_w_gate, m_ffn2_w_up, m_ffn2_w_down, m_ple_pre_g, m_ple_post_g, m_ple_w_gate, m_ple_w_proj, v_ffn1_pre_g, v_ffn1_post_g, v_ffn1_w_gate, v_ffn1_w_up, v_ffn1_w_down, v_mix_pre_g, v_mix_post_g, v_mix_w_in, v_fox_f_bias, v_hgrn_lb_logits, v_hgrn_norm_g, v_mix_w_proj_fox, v_mix_w_proj_hgrn, v_mix_w_out, v_ffn2_pre_g, v_ffn2_post_g, v_ffn2_w_gate, v_ffn2_w_up, v_ffn2_w_down, v_ple_pre_g, v_ple_post_g, v_ple_w_gate, v_ple_w_proj):
    args = dict(locals())
    wts = {n: args[n] for n in WEIGHTS}
    mom = {n: args["m_" + n] for n in WEIGHTS}
    var = {n: args["v_" + n] for n in WEIGHTS}
    d = D_MODEL
    where = jnp.stack([2 * lax.axis_index("x") + lax.axis_index("y"), lax.axis_index("c")]).astype(jnp.int32)

    def start_sets(name, which, mark):
        srcs, lands, plans = [], [], []
        for si in which:
            blocks = []
            for g in GATHER_SETS[si]:
                names, axis = GATHER_GROUPS[g]
                for pos, n in enumerate(names):
                    blocks.append((len(srcs), len(lands), pos, axis))
                    srcs.append(_send_block(n, wts[n], mark))
                lands.append(lax.empty((len(names), N_CHIPS) + srcs[-1].shape, BF16))
            plans.append(blocks)
        sems, srcs, lands, mark = split_start(name, srcs, lands, [4 * len(b) for b in plans],
                                              lambda sr, lr: [_gather_plan(b)(sr, lr) for b in plans])
        out = {}
        for k, si in enumerate(which):
            s_idx = sorted({b[0] for b in plans[k]})
            l_idx = sorted({b[1] for b in plans[k]})
            local = [(s_idx.index(a), l_idx.index(b), pos, ax) for a, b, pos, ax in plans[k]]
            out[si] = (sems[k], [srcs[i] for i in s_idx], [lands[i] for i in l_idx], local)
        return out, mark

    flying, mark = start_sets("gather_start_0", [0], None)
    rest, all_started = start_sets("gather_start_1", list(range(1, len(GATHER_SETS))), mark)
    flying.update(rest)
    full, passing = {}, {}

    def set_of(key):
        g = "win" if key in ("in_main", "in_fa") else key
        return g, [g in s for s in GATHER_SETS].index(True)

    def arrive(si, after):
        sem, srcs, lands, local = flying.pop(si)
        _, got = split_wait("gather_wait_%d" % si, srcs, lands, sem, [after, all_started], _gather_arrivals(local))
        return got, [(b, pos, ax) for _, b, pos, ax in local]

    def early(key, after):
        g, si = set_of(key)
        if g in full or si not in flying:
            return None
        got, blocks = arrive(si, after)
        sem, got, _, mark = split_start("pass_start_%d" % si, got, [], [3 * len(blocks)],
                                        lambda sr, lr: [_pass_plan(blocks, False)(sr, lr)])
        passing[si] = (sem[0], got, blocks)
        return mark

    def land_set(si, after):
        if si in passing:
            sem, got, blocks = passing.pop(si)
            got, _ = split_wait("pass_wait_%d" % si, got, [], sem, [after], _pass_plan(blocks, True))
        else:
            got, blocks = arrive(si, after)
            got = gather_pass("gather_pass_%d" % si, got, blocks)
        for g, arr in zip(GATHER_SETS[si], got):
            if g == "win":
                full["win"] = arr
                full["in_main"], full["in_fa"] = _split_in(arr.reshape(-1, d))
            else:
                full[g] = arr.reshape(-1, d) if g in ROW_BLOCKS else arr

    def get_w(key, after=None):
        g, si = set_of(key)
        if g not in full:
            land_set(si, after)
        return full[key]

    grads, pairing, started = {}, [], {}
    rows4 = lambda a: a.reshape(1, N_CHIPS, a.shape[0] // N_CHIPS, a.shape[1])

    def emit(key, grad):
        if key in ("in_main", "in_fa"):
            grads[key] = grad
            if "in_main" not in grads or "in_fa" not in grads:
                return None
            key, grad = "win", rows4(_join_in(grads["in_main"], grads["in_fa"]))
        grads[key] = grad if grad.ndim == 4 else rows4(grad)
        for si, s in enumerate(REDUCE_SETS):
            if key in s and all(g in grads for g in s):
                axes = [GROUPS[g][1] for g in s]
                own = [grads[g] for g in s]
                zones = [lax.empty(_half_shape(a.shape, ax), F32) for a, ax in zip(own, axes)]
                plan = _pair_plan(axes)
                sem, own, zones, mark = split_start("pair_start_%d" % si, own, zones, [len(s)],
                                                    lambda sr, lr: [plan(sr, lr)])
                pairing.append((si, sem[0], own, zones, axes, plan))
                return mark
        return None

    def advance(value):
        mark = None
        while pairing:
            si, sem, own, zones, axes, plan = pairing.pop(0)
            s = REDUCE_SETS[si]
            own, recv = split_wait("pair_wait_%d" % si, own, zones, sem, [value], plan)
            parts = [pair_add(where, a, r, ax, "pair_add_" + g) for g, a, r, ax in zip(s, own, recv, axes)]
            zones = [lax.empty((3, q.shape[0]) + q.shape[2:], BF16) for q in parts]
            plan = _scatter_plan(len(s))
            sem, parts, zones, mark = split_start("scatter_start_%d" % si, parts, zones, [3 * len(s)],
                                                  lambda sr, lr: [plan(sr, lr)])
            started[si] = (sem[0], parts, zones, own, recv, axes, plan)
        return mark

    gains = {n: wts[n + "_g"] for n in GAINS}
    loss, dx, small = layer_step(x[0], p[0, 0].astype(BF16), loss_target[0], gains, _pad_row(fox_f_bias, LANES),
                                 hgrn_lb_logits, hgrn_norm_g, get_w, emit, advance, early)

    out_g, out_d, out_m, out_v = {}, {}, {}, {}
    after, crossing = None, []

    def finish(si, sem, halves, axes, mark):
        reduced, _ = split_wait("broadcast_wait_%d" % si, halves, [], sem, [mark], _broadcast_plan(axes, True))
        last = None
        for g, red in zip(REDUCE_SETS[si], reduced):
            for idx, n in enumerate(GROUPS[g][0]):
                res = adamw(red, idx, _as_block(n, wts[n]), _as_block(n, mom[n]), _as_block(n, var[n]), "adamw_" + n)
                out_g[n], out_d[n], out_m[n], out_v[n] = [_as_block(n, r) for r in res]
                last = res[1]
        return last

    for si, s in enumerate(REDUCE_SETS):
        sem, parts, zones, own, recv, axes, plan = started[si]
        _, zones = split_wait("scatter_wait_%d" % si, parts, zones, sem, [dx, after], plan)
        halves = [chip_add(where, a, r, z, ax, "chip_add_" + g) for g, a, r, z, ax in zip(s, own, recv, zones, axes)]
        plan = _broadcast_plan(axes, False)
        sem, halves, _, mark = split_start("broadcast_start_%d" % si, halves, [], [len(s)],
                                           lambda sr, lr: [plan(sr, lr)])
        crossing.append((si, sem[0], halves, axes))
        if len(crossing) > 1:
            after = finish(*crossing.pop(0), mark)
    while crossing:
        after = finish(*crossing.pop(0), after)

    small_named = {n + "_g": small[n] for n in GAINS}
    small_named.update(fox_f_bias=small["fox_bias"][:, :HEADS], hgrn_lb_logits=small["lb_logits"],
                       hgrn_norm_g=small["norm_g"])
    g_small = allreduce_small(_pack_small(small_named), after)
    d_small, m_small, v_small = adamw_small(g_small, _pack_small(wts), _pack_small(mom), _pack_small(var))

    for dst, slab in ((out_g, g_small), (out_d, d_small), (out_m, m_small), (out_v, v_small)):
        dst.update(_unpack_small(slab))

    total = lax.psum(loss[0, 0], ("x", "y", "c"))
    return (total, dx[None], *[out_g[n] for n in WEIGHTS], *[out_d[n] for n in WEIGHTS],
            *[out_m[n] for n in WEIGHTS], *[out_v[n] for n in WEIGHTS])
```

```python
import functools

import jax
import jax.numpy as jnp
from jax import lax
from jax.experimental import pallas as pl
from jax.experimental.pallas import tpu as pltpu

F32 = jnp.float32
BF16 = jnp.bfloat16

D_MODEL = 2048
SEQ = 2048
D_FF = 5632
PLE_DIM = 256
HEADS = 8
HEAD_DIM = 128
WIDTH = HEADS * HEAD_DIM
CHUNK = 64
SUB = 16
HGRN_HEADS_PER_STEP = 8
NORM_EPS = 1e-6
MACARON_SCALE = 0.5
N_CHIPS = 4

ADAM_LR = 0.001
ADAM_B1 = 0.9
ADAM_B2 = 0.999
ADAM_EPS = 1e-08
ADAM_WD = 0.01
ADAM_STEP = 10

LANES = 128
VMEM_LIMIT = 56 * 1024 * 1024
NEG_BIG = -1e30
MESH = pl.DeviceIdType.MESH


def _pick(n, cands):
    for c in cands:
        if c <= n and n % c == 0:
            return c
    return n


def _params(sem, vmem=VMEM_LIMIT):
    return pltpu.CompilerParams(dimension_semantics=sem, vmem_limit_bytes=vmem)


def _sigmoid(x):
    return 1.0 / (1.0 + jnp.exp(-x))


def _silu(x):
    return x * _sigmoid(x)


def _silu_grad(x):
    s = _sigmoid(x)
    return s * (1.0 + x * (1.0 - s))


_DN = {"nn": (((1,), (0,)), ((), ())), "nt": (((1,), (1,)), ((), ())), "tn": (((0,), (0,)), ((), ()))}


def _mm(a, b, *, mode, grid, a_spec, b_spec, o_spec, out_shape, acc_shape, name, after=(), init=None, into=None):
    nk = grid[2]
    dn = _DN[mode]
    after = [m for m in after if m is not None]
    extra = ([init] if init is not None else []) + ([into] if into is not None else []) + after
    n_extra = len(extra)

    def body(a_ref, b_ref, *rest):
        o_ref, acc_ref = rest[n_extra:]
        k = pl.program_id(2)

        @pl.when(k == 0)
        def _():
            acc_ref[...] = jnp.zeros_like(acc_ref) if init is None else rest[0][...].astype(F32)

        acc_ref[...] += lax.dot_general(a_ref[...].astype(BF16), b_ref[...].astype(BF16), dn,
                                        preferred_element_type=F32)

        @pl.when(k == nk - 1)
        def _():
            o_ref[...] = acc_ref[...].astype(o_ref.dtype)

    anywhere = pl.BlockSpec(memory_space=pl.ANY)
    return pl.pallas_call(
        body, name=name, grid=grid,
        in_specs=[a_spec, b_spec] + ([o_spec] if init is not None else []) + [anywhere] * (n_extra - (init is not None)),
        out_specs=o_spec, out_shape=out_shape, scratch_shapes=[pltpu.VMEM(acc_shape, F32)],
        input_output_aliases={} if into is None else {2 + (init is not None): 0},
        compiler_params=_params(("parallel", "parallel", "arbitrary")),
    )(a, b, *extra)


def mm_nn_2d(a, b, out_dtype, name, after=()):
    m, kk = a.shape
    n = b.shape[1]
    tm, tn = _pick(m, (512, 256, 128)), _pick(n, (1024, 512, 256, 128))
    tk = _pick(kk, (5632, 2816, 2048, 1408, 1024, 512, 256, 128))
    return _mm(a, b, mode="nn", grid=(m // tm, n // tn, kk // tk),
               a_spec=pl.BlockSpec((tm, tk), lambda i, j, k: (i, k)),
               b_spec=pl.BlockSpec((tk, tn), lambda i, j, k: (k, j)),
               o_spec=pl.BlockSpec((tm, tn), lambda i, j, k: (i, j)),
               out_shape=jax.ShapeDtypeStruct((m, n), out_dtype), acc_shape=(tm, tn), name=name, after=after)


def mm_nt_2d(a, b, out_dtype, name, after=()):
    m, c = a.shape
    n = b.shape[0]
    tm, tn, tk = _pick(m, (512, 256, 128)), _pick(n, (1408, 1024, 512, 256, 128)), _pick(c, (2048, 1408, 1024, 512, 256, 128))
    return _mm(a, b, mode="nt", grid=(m // tm, n // tn, c // tk),
               a_spec=pl.BlockSpec((tm, tk), lambda i, j, k: (i, k)),
               b_spec=pl.BlockSpec((tn, tk), lambda i, j, k: (j, k)),
               o_spec=pl.BlockSpec((tm, tn), lambda i, j, k: (i, j)),
               out_shape=jax.ShapeDtypeStruct((m, n), out_dtype), acc_shape=(tm, tn), name=name, after=after)


def mm_tn_2d(a, b, out_dtype, name):
    c, m = a.shape
    n = b.shape[1]
    tm, tn, tk = _pick(m, (1408, 1024, 512, 256, 128)), _pick(n, (1024, 512, 256, 128)), _pick(c, (2048, 1024, 512, 256, 128))
    return _mm(a, b, mode="tn", grid=(m // tm, n // tn, c // tk),
               a_spec=pl.BlockSpec((tk, tm), lambda i, j, k: (k, i)),
               b_spec=pl.BlockSpec((tk, tn), lambda i, j, k: (k, j)),
               o_spec=pl.BlockSpec((tm, tn), lambda i, j, k: (i, j)),
               out_shape=jax.ShapeDtypeStruct((m, n), out_dtype), acc_shape=(tm, tn), name=name)


def mm_nn_col(a, w, out_dtype, name, slot0=0, total=None, into=None):
    m, kk = a.shape
    g, jn, _, ns = w.shape
    total = g if total is None else total
    tm, tk = _pick(m, (512, 256, 128)), _pick(kk, (2048, 1024, 512, 256, 128))
    return _mm(a, w, mode="nn", grid=(m // tm, g * jn, kk // tk),
               a_spec=pl.BlockSpec((tm, tk), lambda i, j, k: (i, k)),
               b_spec=pl.BlockSpec((None, None, tk, ns), lambda i, j, k: (j // jn, j % jn, k, 0)),
               o_spec=pl.BlockSpec((None, tm, ns), lambda i, j, k: (slot0 + j // jn, i, j % jn)),
               out_shape=jax.ShapeDtypeStruct((total, m, jn * ns), out_dtype), acc_shape=(tm, ns), name=name,
               into=into)


def mm_nt_col(a, w, out_dtype, name, after=(), slot0=0, init=None):
    _, m, _ = a.shape
    g, jn, kk, ns = w.shape
    kc = 2 if jn % 2 == 0 else 1
    per = jn // kc
    nk = g * per
    tm, tn = _pick(m, (512, 256, 128)), _pick(kk, (1024, 512, 256, 128))
    extra = ([init] if init is not None else []) + [v for v in after if v is not None]

    def body(a_ref, w_ref, *rest):
        o_ref, acc_ref = rest[len(extra):]
        k = pl.program_id(2)
        part = None
        for c in range(kc):
            prod = lax.dot_general(a_ref[:, c * ns:(c + 1) * ns].astype(BF16), w_ref[c].astype(BF16), _DN["nt"],
                                   preferred_element_type=F32)
            part = prod if part is None else part + prod

        @pl.when(k == 0)
        def _():
            acc_ref[...] = part if init is None else rest[0][...].astype(F32) + part

        @pl.when(k > 0)
        def _():
            acc_ref[...] += part

        @pl.when(k == nk - 1)
        def _():
            o_ref[...] = acc_ref[...].astype(o_ref.dtype)

    o_spec = pl.BlockSpec((tm, tn), lambda i, j, k: (i, j))
    anywhere = pl.BlockSpec(memory_space=pl.ANY)
    return pl.pallas_call(
        body, name=name, grid=(m // tm, kk // tn, nk),
        in_specs=[pl.BlockSpec((None, tm, kc * ns), lambda i, j, k: (slot0 + k // per, i, k % per)),
                  pl.BlockSpec((None, kc, tn, ns), lambda i, j, k: (k // per, k % per, j, 0))]
        + ([o_spec] if init is not None else []) + [anywhere] * (len(extra) - (init is not None)),
        out_specs=o_spec, out_shape=jax.ShapeDtypeStruct((m, kk), out_dtype),
        scratch_shapes=[pltpu.VMEM((tm, tn), F32)],
        compiler_params=_params(("parallel", "parallel", "arbitrary")),
    )(a, w, *extra)


def mm_tn_col(a, b, jn, out_dtype, name):
    c, kk = a.shape
    g, _, n = b.shape
    ns = n // jn
    tm, tk = _pick(kk, (512, 256, 128)), _pick(c, (2048, 1024, 512, 256, 128))
    return _mm(a, b, mode="tn", grid=(kk // tm, g * jn, c // tk),
               a_spec=pl.BlockSpec((tk, tm), lambda i, j, k: (k, i)),
               b_spec=pl.BlockSpec((None, tk, ns), lambda i, j, k: (j // jn, k, j % jn)),
               o_spec=pl.BlockSpec((None, None, tm, ns), lambda i, j, k: (j // jn, j % jn, i, 0)),
               out_shape=jax.ShapeDtypeStruct((g, jn, kk, ns), out_dtype), acc_shape=(tm, ns), name=name)


def _rstd(x):
    return lax.rsqrt(jnp.mean(x * x, axis=-1, keepdims=True) + NORM_EPS)


def _rms_bwd(x, g, dy):
    r = _rstd(x)
    xn = x * r
    dyg = dy * g
    dx = r * (dyg - xn * jnp.mean(dyg * xn, axis=-1, keepdims=True))
    return dx, jnp.sum(dy * xn, axis=0, keepdims=True)


def _row_tile(t):
    return _pick(t, (256, 128, 64, 32, 16, 8))


def norm_in(h, g, name):
    t, d = h.shape
    tr = _row_tile(t)

    def body(h_ref, g_ref, u_ref):
        x = h_ref[...]
        u_ref[...] = (x * _rstd(x) * g_ref[...]).astype(BF16)

    return pl.pallas_call(
        body, name=name, grid=(t // tr,),
        in_specs=[pl.BlockSpec((tr, d), lambda i: (i, 0)), pl.BlockSpec((1, d), lambda i: (0, 0))],
        out_specs=pl.BlockSpec((tr, d), lambda i: (i, 0)),
        out_shape=jax.ShapeDtypeStruct((t, d), BF16), compiler_params=_params(("parallel",)),
    )(h, g)


def resid_norm(h, y, g, scale, g_next, name):
    t, d = h.shape
    tr = _row_tile(t)

    def body(h_ref, y_ref, g_ref, gn_ref, o_ref, u_ref):
        yv = y_ref[...]
        out = h_ref[...] + scale * (yv * _rstd(yv) * g_ref[...])
        o_ref[...] = out
        u_ref[...] = (out * _rstd(out) * gn_ref[...]).astype(BF16)

    row = pl.BlockSpec((tr, d), lambda i: (i, 0))
    vec = pl.BlockSpec((1, d), lambda i: (0, 0))
    return pl.pallas_call(
        body, name=name, grid=(t // tr,), in_specs=[row, row, vec, vec], out_specs=[row, row],
        out_shape=[jax.ShapeDtypeStruct((t, d), F32), jax.ShapeDtypeStruct((t, d), BF16)],
        compiler_params=_params(("parallel",)),
    )(h, y, g, g_next)


def post_bwd(dh, y, g, scale, name):
    t, d = dh.shape
    tr = _row_tile(t)

    def body(dh_ref, y_ref, g_ref, dy_ref, dg_ref):
        @pl.when(pl.program_id(0) == 0)
        def _():
            dg_ref[...] = jnp.zeros_like(dg_ref)

        dx, dg = _rms_bwd(y_ref[...], g_ref[...], scale * dh_ref[...])
        dy_ref[...] = dx.astype(BF16)
        dg_ref[...] += dg

    row = pl.BlockSpec((tr, d), lambda i: (i, 0))
    vec = pl.BlockSpec((1, d), lambda i: (0, 0))
    return pl.pallas_call(
        body, name=name, grid=(t // tr,), in_specs=[row, row, vec], out_specs=[row, vec],
        out_shape=[jax.ShapeDtypeStruct((t, d), BF16), jax.ShapeDtypeStruct((1, d), F32)],
        compiler_params=_params(("arbitrary",)),
    )(dh, y, g)


def pre_bwd(dh, h, g, dus, name, after=()):
    t, d = dh.shape
    tr = _row_tile(t)
    n_du = len(dus)
    after = [m for m in after if m is not None]

    def body(*refs):
        dh_ref, h_ref, g_ref = refs[:3]
        du_refs = refs[3:3 + n_du]
        o_ref, dg_ref = refs[3 + n_du + len(after):]

        @pl.when(pl.program_id(0) == 0)
        def _():
            dg_ref[...] = jnp.zeros_like(dg_ref)

        du = du_refs[0][...]
        for r in du_refs[1:]:
            du = du + r[...]
        dx, dg = _rms_bwd(h_ref[...], g_ref[...], du)
        o_ref[...] = dh_ref[...] + dx
        dg_ref[...] += dg

    row = pl.BlockSpec((tr, d), lambda i: (i, 0))
    vec = pl.BlockSpec((1, d), lambda i: (0, 0))
    return pl.pallas_call(
        body, name=name, grid=(t // tr,),
        in_specs=[row, row, vec] + [row] * n_du + [pl.BlockSpec(memory_space=pl.ANY)] * len(after),
        out_specs=[row, vec], out_shape=[jax.ShapeDtypeStruct((t, d), F32), jax.ShapeDtypeStruct((1, d), F32)],
        compiler_params=_params(("arbitrary",)),
    )(dh, h, g, *dus, *after)


def _ew_tiles(t, f):
    return _pick(t, (256, 128, 64, 32, 16, 8)), _pick(f, (1408, 1024, 512, 256, 128))


def swiglu_act(gu, name):
    _, t, f = gu.shape
    tr, tc = _ew_tiles(t, f)

    def body(gu_ref, o_ref):
        o_ref[...] = (_silu(gu_ref[0]) * gu_ref[1]).astype(BF16)

    return pl.pallas_call(
        body, name=name, grid=(t // tr, f // tc),
        in_specs=[pl.BlockSpec((2, tr, tc), lambda i, j: (0, i, j))],
        out_specs=pl.BlockSpec((tr, tc), lambda i, j: (i, j)),
        out_shape=jax.ShapeDtypeStruct((t, f), BF16), compiler_params=_params(("parallel", "parallel")),
    )(gu)


def swiglu_bwd(dact, gu, name):
    _, t, f = gu.shape
    tr, tc = _ew_tiles(t, f)

    def body(da_ref, gu_ref, o_ref):
        da = da_ref[...]
        gate = gu_ref[0]
        o_ref[0] = (da * gu_ref[1] * _silu_grad(gate)).astype(BF16)
        o_ref[1] = (da * _silu(gate)).astype(BF16)

    return pl.pallas_call(
        body, name=name, grid=(t // tr, f // tc),
        in_specs=[pl.BlockSpec((tr, tc), lambda i, j: (i, j)), pl.BlockSpec((2, tr, tc), lambda i, j: (0, i, j))],
        out_specs=pl.BlockSpec((2, tr, tc), lambda i, j: (0, i, j)),
        out_shape=jax.ShapeDtypeStruct((2, t, f), BF16), compiler_params=_params(("parallel", "parallel")),
    )(dact, gu)


def _col_blocks():
    w = WIDTH // LANES
    return dict(q_a=0, k_a=w, v_a=2 * w, q_b=3 * w, f_b=4 * w, i_b=5 * w, g_b=6 * w, gate_a=7 * w,
                gate_b=7 * w + D_MODEL // LANES)


def _tri(n, lower):
    r = lax.broadcasted_iota(jnp.int32, (n, n), 0)
    c = lax.broadcasted_iota(jnp.int32, (n, n), 1)
    return jnp.where((r >= c) if lower else (r <= c), 1.0, 0.0).astype(F32)


def _dot_hi(a, b):
    return jnp.dot(a, b, precision=lax.Precision.HIGHEST, preferred_element_type=F32)


def fox_prep(fa, bias, name):
    t, w = fa.shape
    tb = _pick(t, (256, 128, 64))

    def body(fa_ref, b_ref, c_ref, carry_ref):
        @pl.when(pl.program_id(0) == 0)
        def _():
            carry_ref[...] = jnp.zeros_like(carry_ref)

        z = fa_ref[...] + b_ref[...]
        lf = jnp.minimum(z, 0.0) - jnp.log(1.0 + jnp.exp(-jnp.abs(z)))
        c = _dot_hi(_tri(tb, True), lf) + carry_ref[...]
        c_ref[...] = c
        carry_ref[...] = carry_ref[...] + jnp.sum(lf, axis=0, keepdims=True)

    return pl.pallas_call(
        body, name=name, grid=(t // tb,),
        in_specs=[pl.BlockSpec((tb, w), lambda i: (i, 0)), pl.BlockSpec((1, w), lambda i: (0, 0))],
        out_specs=pl.BlockSpec((tb, w), lambda i: (i, 0)),
        out_shape=jax.ShapeDtypeStruct((t, w), F32), scratch_shapes=[pltpu.VMEM((1, w), F32)],
        compiler_params=_params(("arbitrary",)),
    )(fa, bias)


def fox_post_bwd(dc, fa, bias, name):
    t, w = fa.shape
    tb = _pick(t, (256, 128, 64))
    nb = t // tb

    def body(dc_ref, fa_ref, b_ref, dfa_ref, db_ref, carry_ref):
        @pl.when(pl.program_id(0) == 0)
        def _():
            carry_ref[...] = jnp.zeros_like(carry_ref)
            db_ref[...] = jnp.zeros_like(db_ref)

        dcv = dc_ref[...]
        dlf = _dot_hi(_tri(tb, False), dcv) + carry_ref[...]
        z = fa_ref[...] + b_ref[...]
        dz = dlf * _sigmoid(-z)
        dfa_ref[...] = dz.astype(BF16)
        db_ref[...] += jnp.sum(dz, axis=0, keepdims=True)
        carry_ref[...] = carry_ref[...] + jnp.sum(dcv, axis=0, keepdims=True)

    rev = pl.BlockSpec((tb, w), lambda i: (nb - 1 - i, 0))
    vec = pl.BlockSpec((1, w), lambda i: (0, 0))
    return pl.pallas_call(
        body, name=name, grid=(nb,), in_specs=[rev, rev, vec], out_specs=[rev, vec],
        out_shape=[jax.ShapeDtypeStruct((t, w), BF16), jax.ShapeDtypeStruct((1, w), F32)],
        scratch_shapes=[pltpu.VMEM((1, w), F32)], compiler_params=_params(("arbitrary",)),
    )(dc, fa, bias)


def _fox_probs(q_ref, k_ref, cc_ref, cr_ref, qi, tq, t):
    scale = HEAD_DIM ** -0.5
    s = lax.dot_general(q_ref[...].astype(BF16), k_ref[...].astype(BF16), _DN["nt"], preferred_element_type=F32)
    logits = s * scale + cc_ref[...] - cr_ref[...]
    qpos = qi * tq + lax.broadcasted_iota(jnp.int32, (tq, t), 0)
    kpos = lax.broadcasted_iota(jnp.int32, (tq, t), 1)
    logits = jnp.where(kpos <= qpos, logits, NEG_BIG)
    m = jnp.max(logits, axis=-1, keepdims=True)
    p = jnp.exp(logits - m)
    return p / jnp.sum(p, axis=-1, keepdims=True)


FOX_SEGMENTS = 4


def _fox_segments(t):
    tq = _pick(t, (128,))
    nseg = min(FOX_SEGMENTS, t // tq)
    return tq, nseg, t // tq // nseg


def _fox_specs(t, q0, kt, tq):
    cb = _col_blocks()
    dh = HEAD_DIM
    return [pl.BlockSpec((tq, dh), lambda h, i: (q0 + i, cb["q_a"] + h)),
            pl.BlockSpec((kt, dh), lambda h, i: (0, cb["k_a"] + h)),
            pl.BlockSpec((kt, dh), lambda h, i: (0, cb["v_a"] + h)),
            pl.BlockSpec((None, tq, 1), lambda h, i: (h, q0 + i, 0)),
            pl.BlockSpec((None, 1, kt), lambda h, i: (h, 0, 0))]


def fox_fwd(proj, c_col, c_row, name):
    t = proj.shape[0]
    tq, nseg, nq = _fox_segments(t)
    dh = HEAD_DIM

    def segment(out, r):
        q0, kt = r * nq, (r + 1) * nq * tq

        def body(q_ref, k_ref, v_ref, cc_ref, cr_ref, prev_ref, o_ref):
            p = _fox_probs(q_ref, k_ref, cc_ref, cr_ref, q0 + pl.program_id(1), tq, kt)
            o_ref[...] = jnp.dot(p.astype(BF16), v_ref[...].astype(BF16), preferred_element_type=F32).astype(BF16)

        return pl.pallas_call(
            body, name="%s_%d" % (name, r), grid=(HEADS, nq),
            in_specs=_fox_specs(t, q0, kt, tq) + [pl.BlockSpec(memory_space=pl.ANY)],
            out_specs=pl.BlockSpec((tq, dh), lambda h, i: (q0 + i, h)),
            out_shape=jax.ShapeDtypeStruct((t, WIDTH), BF16), input_output_aliases={5: 0},
            compiler_params=_params(("parallel", "parallel")),
        )(proj, proj, proj, c_col, c_row, out)

    out = lax.empty((t, WIDTH), BF16)
    for r in range(nseg):
        out = segment(out, r)
    return out


def fox_bwd(proj, c_col, c_row, do_ab, name):
    t = proj.shape[0]
    tq, nseg, nq = _fox_segments(t)
    dh = HEAD_DIM
    scale = HEAD_DIM ** -0.5

    def segment(acc, r):
        q0, kt = r * nq, (r + 1) * nq * tq

        def body(q_ref, k_ref, v_ref, cc_ref, cr_ref, do_ref, dqp_ref, dkp_ref, dvp_ref, dccp_ref, dcrp_ref,
                 dq_ref, dk_ref, dv_ref, dcc_ref, dcr_ref):
            @pl.when(pl.program_id(1) == 0)
            def _():
                dk_ref[...] = dkp_ref[...]
                dv_ref[...] = dvp_ref[...]
                dcr_ref[...] = dcrp_ref[...]

            p = _fox_probs(q_ref, k_ref, cc_ref, cr_ref, q0 + pl.program_id(1), tq, kt)
            dov = do_ref[...].astype(BF16)
            kb = k_ref[...].astype(BF16)
            dv_ref[...] += lax.dot_general(p.astype(BF16), dov, _DN["tn"], preferred_element_type=F32)
            dp = lax.dot_general(dov, v_ref[...].astype(BF16), _DN["nt"], preferred_element_type=F32)
            ds = p * (dp - jnp.sum(p * dp, axis=-1, keepdims=True))
            dcc_ref[...] = jnp.sum(ds, axis=-1, keepdims=True)
            dcr_ref[...] -= jnp.sum(ds, axis=0, keepdims=True)
            dss = (ds * scale).astype(BF16)
            dq_ref[...] = jnp.dot(dss, kb, preferred_element_type=F32).astype(BF16)
            dk_ref[...] += lax.dot_general(dss, q_ref[...].astype(BF16), _DN["tn"], preferred_element_type=F32)

        rows = pl.BlockSpec((tq, dh), lambda h, i: (q0 + i, h))
        keys = pl.BlockSpec((kt, dh), lambda h, i: (0, h))
        col = pl.BlockSpec((None, tq, 1), lambda h, i: (h, q0 + i, 0))
        row = pl.BlockSpec((None, 1, kt), lambda h, i: (h, 0, 0))
        anywhere = pl.BlockSpec(memory_space=pl.ANY)
        return pl.pallas_call(
            body, name="%s_%d" % (name, r), grid=(HEADS, nq),
            in_specs=_fox_specs(t, q0, kt, tq)
            + [pl.BlockSpec((None, tq, dh), lambda h, i: (0, q0 + i, h)), anywhere, keys, keys, anywhere, row],
            out_specs=[rows, keys, keys, col, row],
            out_shape=[jax.ShapeDtypeStruct(a.shape, a.dtype) for a in acc],
            input_output_aliases={6 + k: k for k in range(5)},
            compiler_params=_params(("parallel", "arbitrary")),
        )(proj, proj, proj, c_col, c_row, do_ab, *acc)

    acc = [lax.empty((t, WIDTH), BF16), jnp.zeros((t, WIDTH), F32), jnp.zeros((t, WIDTH), F32),
           lax.empty((HEADS, t, 1), F32), jnp.zeros((HEADS, 1, t), F32)]
    for r in range(nseg):
        acc = segment(acc, r)
    return acc


def _lower_bound(lg_ref):
    l0 = lg_ref[0:1, :]
    l1 = lg_ref[1:2, :]
    m = jnp.maximum(l0, l1)
    e0 = jnp.exp(l0 - m)
    e1 = jnp.exp(l1 - m)
    return e0 / (e0 + e1)


def _hgrn_inputs(qb_ref, fb_ref, lg_ref, q_s, k_s, cum_s):
    lb = _lower_bound(lg_ref)
    sig = _sigmoid(fb_ref[...])
    f = lb + (1.0 - lb) * sig
    q_s[...] = _silu(qb_ref[...])
    k_s[...] = 1.0 - f
    cum_s[...] = _dot_hi(_tri(CHUNK, True), jnp.log(f))
    return lb, sig, f


def _boundary(cum_s, a):
    if a == 0:
        return jnp.zeros((1, HEAD_DIM), F32)
    return cum_s[pl.ds(SUB * a - 1, 1), :]


def _hgrn_scores(q_s, k_s, cum_s):
    cum = cum_s[...]
    kk = k_s[...]
    lane = lax.broadcasted_iota(jnp.int32, (SUB, CHUNK), 1)
    row = lax.broadcasted_iota(jnp.int32, (SUB, 1), 0)
    blocks = []
    for a in range(CHUNK // SUB):
        rows = pl.ds(SUB * a, SUB)
        ca = _boundary(cum_s, a)
        cum_a = cum_s[rows, :]
        q_a = q_s[rows, :]
        qa = q_a * jnp.exp(cum_a - ca)
        ka = kk * jnp.exp(jnp.minimum(ca - cum, 0.0))
        blk = lax.dot_general(qa, ka, _DN["nt"], preferred_element_type=F32)
        blk = jnp.where(lane < SUB * a, blk, 0.0)
        for s in range(SUB):
            r = SUB * a + s
            e = jnp.exp(jnp.minimum(cum_a - cum_s[pl.ds(r, 1), :], 0.0))
            col = jnp.sum(q_a * k_s[pl.ds(r, 1), :] * e, axis=-1, keepdims=True)
            col = jnp.where(row >= s, col, 0.0)
            blk = jnp.where(lane == r, col, blk)
        blocks.append(blk)
    return jnp.concatenate(blocks, axis=0)


def hgrn_fwd(proj, lb_logits, name, after=()):
    after = [m for m in after if m is not None]
    t = proj.shape[0]
    n = t // CHUNK
    cb = _col_blocks()
    dh = HEAD_DIM

    hb = min(HGRN_HEADS_PER_STEP, HEADS)
    w = hb * dh

    def one_head(qb_ref, fb_ref, ib_ref, lg_ref, o_ref, st_ref, a_ref, state, q_s, k_s, cum_s):
        _hgrn_inputs(qb_ref, fb_ref, lg_ref, q_s, k_s, cum_s)
        st = state[...]
        st_ref[...] = st
        cum = cum_s[...]
        v = ib_ref[...]
        qe = q_s[...] * jnp.exp(cum)
        inter = lax.dot_general(qe, st, _DN["nt"], preferred_element_type=F32)
        a_mat = _hgrn_scores(q_s, k_s, cum_s)
        a_ref[...] = a_mat
        o_ref[...] = inter + jnp.dot(a_mat, v, preferred_element_type=F32)
        last = cum_s[pl.ds(CHUNK - 1, 1), :]
        kd = k_s[...] * jnp.exp(last - cum)
        state[...] = st * jnp.exp(last) + lax.dot_general(v, kd, _DN["tn"], preferred_element_type=F32)

    def body(qb_ref, fb_ref, ib_ref, lg_ref, *rest):
        o_ref, st_ref, a_ref = rest[len(after):len(after) + 3]
        scratch = rest[len(after) + 3:]

        @pl.when(pl.program_id(1) == 0)
        def _():
            for j in range(hb):
                scratch[4 * j][...] = jnp.zeros((dh, dh), F32)

        for j in range(hb):
            cols = (slice(None), pl.ds(j * dh, dh))
            one_head(qb_ref.at[cols], fb_ref.at[cols], ib_ref.at[cols], lg_ref.at[cols], o_ref.at[cols],
                     st_ref.at[j], a_ref.at[j], *scratch[4 * j:4 * j + 4])

    blk = lambda off: pl.BlockSpec((CHUNK, w), lambda h, i: (i, off // hb + h))
    return pl.pallas_call(
        body, name=name, grid=(HEADS // hb, n),
        in_specs=[blk(cb["q_b"]), blk(cb["f_b"]), blk(cb["i_b"]), pl.BlockSpec((2, w), lambda h, i: (0, h))]
        + [pl.BlockSpec(memory_space=pl.ANY)] * len(after),
        out_specs=[pl.BlockSpec((CHUNK, w), lambda h, i: (i, h)),
                   pl.BlockSpec((hb, None, dh, dh), lambda h, i: (h, i, 0, 0)),
                   pl.BlockSpec((hb, None, CHUNK, CHUNK), lambda h, i: (h, i, 0, 0))],
        out_shape=[jax.ShapeDtypeStruct((t, WIDTH), F32), jax.ShapeDtypeStruct((HEADS, n, dh, dh), F32),
                   jax.ShapeDtypeStruct((HEADS, n, CHUNK, CHUNK), F32)],
        scratch_shapes=([pltpu.VMEM((dh, dh), F32)] + [pltpu.VMEM((CHUNK, dh), F32)] * 3) * hb,
        compiler_params=_params(("parallel", "arbitrary")),
    )(proj, proj, proj, lb_logits, *after)


def hgrn_bwd(proj, lb_logits, states, scores, do, name):
    t = proj.shape[0]
    n = t // CHUNK
    cb = _col_blocks()
    dh = HEAD_DIM
    nsub = CHUNK // SUB

    hb = min(HGRN_HEADS_PER_STEP, HEADS)
    w = hb * dh

    def one_head(qb_ref, fb_ref, ib_ref, lg_ref, st_ref, a_ref, do_ref, dqb_ref, dfb_ref, dib_ref, dlb_ref,
                 dstate, q_s, k_s, cum_s, da_s, dq_s, dk_s):
        lb, sig, f = _hgrn_inputs(qb_ref, fb_ref, lg_ref, q_s, k_s, cum_s)
        st = st_ref[...]
        dst = dstate[...]
        cum = cum_s[...]
        q = q_s[...]
        kk = k_s[...]
        v = ib_ref[...]
        dov = do_ref[...]
        e_cum = jnp.exp(cum)
        qe = q * e_cum
        last = cum_s[pl.ds(CHUNK - 1, 1), :]
        e_last = jnp.exp(last)
        e_tail = jnp.exp(last - cum)
        kd = kk * e_tail

        a_mat = a_ref[...]
        tri = _tri(CHUNK, True)
        da_s[...] = lax.dot_general(dov, v, _DN["nt"], preferred_element_type=F32) * tri
        dv = (lax.dot_general(a_mat, dov, _DN["tn"], preferred_element_type=F32)
              + lax.dot_general(kd, dst, _DN["nt"], preferred_element_type=F32))
        dk_state = jnp.dot(v, dst, preferred_element_type=F32) * e_tail
        dq_inter = jnp.dot(dov, st, preferred_element_type=F32) * e_cum
        dstate[...] = dst * e_last + lax.dot_general(dov, qe, _DN["tn"], preferred_element_type=F32)

        lane = lax.broadcasted_iota(jnp.int32, (SUB, CHUNK), 1)
        row = lax.broadcasted_iota(jnp.int32, (SUB, 1), 0)
        dk_s[...] = jnp.zeros_like(dk_s)
        for a in range(nsub):
            rows = pl.ds(SUB * a, SUB)
            ca = _boundary(cum_s, a)
            cum_a = cum_s[rows, :]
            q_a = q_s[rows, :]
            ea = jnp.exp(cum_a - ca)
            eb = jnp.exp(jnp.minimum(ca - cum, 0.0))
            da_a = da_s[rows, :]
            da_off = jnp.where(lane < SUB * a, da_a, 0.0)
            dq_a = ea * jnp.dot(da_off, kk * eb, preferred_element_type=F32)
            dk_s[...] += eb * lax.dot_general(da_off, q_a * ea, _DN["tn"], preferred_element_type=F32)
            dk_rows = jnp.zeros((SUB, dh), F32)
            for s in range(SUB):
                r = SUB * a + s
                e = jnp.exp(jnp.minimum(cum_a - cum_s[pl.ds(r, 1), :], 0.0))
                dcol = jnp.sum(jnp.where(lane == r, da_a, 0.0), axis=-1, keepdims=True)
                dcol = jnp.where(row >= s, dcol, 0.0)
                w = dcol * e
                dq_a = dq_a + w * k_s[pl.ds(r, 1), :]
                dk_rows = jnp.where(row == s, jnp.sum(w * q_a, axis=0, keepdims=True), dk_rows)
            dq_s[rows, :] = dq_a
            dk_s[rows, :] += dk_rows

        dq = dq_inter + dq_s[...]
        dk = dk_s[...] + dk_state
        d_last = (jnp.sum(dst * st, axis=0, keepdims=True) * e_last
                  + jnp.sum(kk * dk_state, axis=0, keepdims=True))
        rowc = lax.broadcasted_iota(jnp.int32, (CHUNK, 1), 0)
        dcum = q * dq - kk * dk + jnp.where(rowc == CHUNK - 1, d_last, 0.0)
        dg = _dot_hi(_tri(CHUNK, False), dcum)
        df = dg / f - dk
        dqb_ref[...] = (dq * _silu_grad(qb_ref[...])).astype(BF16)
        dfb_ref[...] = (df * (1.0 - lb) * sig * (1.0 - sig)).astype(BF16)
        dib_ref[...] = dv.astype(BF16)
        dlb_ref[...] += jnp.sum(df * (1.0 - sig), axis=0, keepdims=True)

    def body(qb_ref, fb_ref, ib_ref, lg_ref, st_ref, a_ref, do_ref, dqb_ref, dfb_ref, dib_ref, dlb_ref, *scratch):
        @pl.when(pl.program_id(1) == 0)
        def _():
            for j in range(hb):
                scratch[7 * j][...] = jnp.zeros((dh, dh), F32)
            dlb_ref[...] = jnp.zeros_like(dlb_ref)

        for j in range(hb):
            cols = (slice(None), pl.ds(j * dh, dh))
            one_head(qb_ref.at[cols], fb_ref.at[cols], ib_ref.at[cols], lg_ref.at[cols], st_ref.at[j], a_ref.at[j],
                     do_ref.at[cols], dqb_ref.at[cols], dfb_ref.at[cols], dib_ref.at[cols], dlb_ref.at[cols],
                     *scratch[7 * j:7 * j + 7])

    blk = lambda off: pl.BlockSpec((CHUNK, w), lambda h, i: (n - 1 - i, off // hb + h))
    out_blk = pl.BlockSpec((CHUNK, w), lambda h, i: (n - 1 - i, h))
    return pl.pallas_call(
        body, name=name, grid=(HEADS // hb, n),
        in_specs=[blk(cb["q_b"]), blk(cb["f_b"]), blk(cb["i_b"]), pl.BlockSpec((2, w), lambda h, i: (0, h)),
                  pl.BlockSpec((hb, None, dh, dh), lambda h, i: (h, n - 1 - i, 0, 0)),
                  pl.BlockSpec((hb, None, CHUNK, CHUNK), lambda h, i: (h, n - 1 - i, 0, 0)), out_blk],
        out_specs=[out_blk, out_blk, out_blk, pl.BlockSpec((1, w), lambda h, i: (0, h))],
        out_shape=[jax.ShapeDtypeStruct((t, WIDTH), BF16)] * 3 + [jax.ShapeDtypeStruct((1, WIDTH), F32)],
        scratch_shapes=([pltpu.VMEM((dh, dh), F32)] + [pltpu.VMEM((CHUNK, dh), F32)] * 3
                        + [pltpu.VMEM((CHUNK, CHUNK), F32)] + [pltpu.VMEM((CHUNK, dh), F32)] * 2) * hb,
        compiler_params=_params(("parallel", "arbitrary")),
    )(proj, proj, proj, lb_logits, states, scores, do)


def lb_bwd(dlb, lb_logits, name):
    def body(dlb_ref, lg_ref, o_ref):
        p0 = _lower_bound(lg_ref)
        d0 = dlb_ref[...] * p0 * (1.0 - p0)
        o_ref[0:1, :] = d0
        o_ref[1:2, :] = -d0

    return pl.pallas_call(body, name=name, out_shape=jax.ShapeDtypeStruct(lb_logits.shape, F32))(dlb, lb_logits)


def gnorm_fwd(o_raw, proj, norm_g, name, after=()):
    after = [m for m in after if m is not None]
    t = o_raw.shape[0]
    tr = _row_tile(t)
    cb = _col_blocks()
    dh = HEAD_DIM

    gcol = cb["g_b"] * LANES // WIDTH

    def body(o_ref, gb_ref, g_ref, *rest):
        for h in range(HEADS):
            cols = (slice(None), pl.ds(h * dh, dh))
            x = o_ref[cols]
            rest[-1][cols] = (x * _rstd(x) * g_ref[...] * _silu(gb_ref[cols])).astype(BF16)

    return pl.pallas_call(
        body, name=name, grid=(t // tr,),
        in_specs=[pl.BlockSpec((tr, WIDTH), lambda i: (i, 0)), pl.BlockSpec((tr, WIDTH), lambda i: (i, gcol)),
                  pl.BlockSpec((1, dh), lambda i: (0, 0))] + [pl.BlockSpec(memory_space=pl.ANY)] * len(after),
        out_specs=pl.BlockSpec((tr, WIDTH), lambda i: (i, 0)),
        out_shape=jax.ShapeDtypeStruct((t, WIDTH), BF16), compiler_params=_params(("parallel",)),
    )(o_raw, proj, norm_g, *after)


def gnorm_bwd(dy_ab, o_raw, proj, norm_g, name):
    t = o_raw.shape[0]
    tr = _row_tile(t)
    cb = _col_blocks()
    dh = HEAD_DIM
    gcol = cb["g_b"] * LANES // WIDTH

    def body(dy_ref, o_ref, gb_ref, g_ref, do_ref, dgb_ref, dg_ref):
        @pl.when(pl.program_id(0) == 0)
        def _():
            dg_ref[...] = jnp.zeros_like(dg_ref)

        g = g_ref[...]
        acc = jnp.zeros_like(g)
        for h in range(HEADS):
            cols = (slice(None), pl.ds(h * dh, dh))
            x = o_ref[cols]
            gb = gb_ref[cols]
            dyv = dy_ref[cols]
            dx, dg = _rms_bwd(x, g, dyv * _silu(gb))
            do_ref[cols] = dx
            dgb_ref[cols] = (dyv * (x * _rstd(x) * g) * _silu_grad(gb)).astype(BF16)
            acc = acc + dg
        dg_ref[...] += acc

    rows = pl.BlockSpec((tr, WIDTH), lambda i: (i, 0))
    vec = pl.BlockSpec((1, dh), lambda i: (0, 0))
    return pl.pallas_call(
        body, name=name, grid=(t // tr,),
        in_specs=[pl.BlockSpec((None, tr, WIDTH), lambda i: (1, i, 0)), rows,
                  pl.BlockSpec((tr, WIDTH), lambda i: (i, gcol)), vec],
        out_specs=[rows, rows, vec],
        out_shape=[jax.ShapeDtypeStruct((t, WIDTH), F32), jax.ShapeDtypeStruct((t, WIDTH), BF16),
                   jax.ShapeDtypeStruct((1, dh), F32)],
        compiler_params=_params(("arbitrary",)),
    )(dy_ab, o_raw, proj, norm_g)


def merge_fwd(proj, y, name):
    _, t, d = y.shape
    tr = _row_tile(t)
    tc = _pick(d, (1024, 512, 256, 128))
    cb = _col_blocks()
    ga, gb = cb["gate_a"] * LANES // tc, cb["gate_b"] * LANES // tc

    def body(ga_ref, gb_ref, y_ref, o_ref):
        o_ref[...] = (_sigmoid(ga_ref[...]) * y_ref[0] + _sigmoid(gb_ref[...]) * y_ref[1]).astype(BF16)

    return pl.pallas_call(
        body, name=name, grid=(t // tr, d // tc),
        in_specs=[pl.BlockSpec((tr, tc), lambda i, j: (i, ga + j)), pl.BlockSpec((tr, tc), lambda i, j: (i, gb + j)),
                  pl.BlockSpec((2, tr, tc), lambda i, j: (0, i, j))],
        out_specs=pl.BlockSpec((tr, tc), lambda i, j: (i, j)),
        out_shape=jax.ShapeDtypeStruct((t, d), BF16), compiler_params=_params(("parallel", "parallel")),
    )(proj, proj, y)


def merge_bwd(dm, proj, y, name):
    _, t, d = y.shape
    tr = _row_tile(t)
    tc = _pick(d, (1024, 512, 256, 128))
    cb = _col_blocks()
    ga, gb = cb["gate_a"] * LANES // tc, cb["gate_b"] * LANES // tc

    def body(dm_ref, ga_ref, gb_ref, y_ref, dg_ref, dy_ref):
        dmv = dm_ref[...]
        for idx, g_ref in enumerate((ga_ref, gb_ref)):
            s = _sigmoid(g_ref[...])
            dg_ref[idx] = (dmv * y_ref[idx] * s * (1.0 - s)).astype(BF16)
            dy_ref[idx] = (dmv * s).astype(BF16)

    pair = pl.BlockSpec((2, tr, tc), lambda i, j: (0, i, j))
    return pl.pallas_call(
        body, name=name, grid=(t // tr, d // tc),
        in_specs=[pl.BlockSpec((tr, tc), lambda i, j: (i, j)), pl.BlockSpec((tr, tc), lambda i, j: (i, ga + j)),
                  pl.BlockSpec((tr, tc), lambda i, j: (i, gb + j)), pair],
        out_specs=[pair, pair],
        out_shape=[jax.ShapeDtypeStruct((2, t, d), BF16)] * 2, compiler_params=_params(("parallel", "parallel")),
    )(dm, proj, proj, y)


def ple_tail(h, a, b, g, target, name):
    t, d = h.shape
    tr = _row_tile(t)

    def body(h_ref, a_ref, b_ref, g_ref, t_ref, loss_ref, dh_ref, da_ref, db_ref, dg_ref):
        @pl.when(pl.program_id(0) == 0)
        def _():
            loss_ref[...] = jnp.zeros_like(loss_ref)
            dg_ref[...] = jnp.zeros_like(dg_ref)

        s = _sigmoid(a_ref[...])
        bv = b_ref[...]
        z = s * bv
        gv = g_ref[...]
        err = h_ref[...] + z * _rstd(z) * gv - t_ref[...]
        loss_ref[...] += 0.5 * jnp.sum(jnp.sum(err * err, axis=-1, keepdims=True), axis=0, keepdims=True) / d
        dh = err / d
        dh_ref[...] = dh
        dz, dg = _rms_bwd(z, gv, dh)
        da_ref[...] = (dz * bv * s * (1.0 - s)).astype(BF16)
        db_ref[...] = (dz * s).astype(BF16)
        dg_ref[...] += dg

    row = pl.BlockSpec((tr, d), lambda i: (i, 0))
    vec = pl.BlockSpec((1, d), lambda i: (0, 0))
    return pl.pallas_call(
        body, name=name, grid=(t // tr,), in_specs=[row, row, row, vec, row],
        out_specs=[pl.BlockSpec((1, 1), lambda i: (0, 0)), row, row, row, vec],
        out_shape=[jax.ShapeDtypeStruct((1, 1), F32), jax.ShapeDtypeStruct((t, d), F32),
                   jax.ShapeDtypeStruct((t, d), BF16), jax.ShapeDtypeStruct((t, d), BF16),
                   jax.ShapeDtypeStruct((1, d), F32)],
        compiler_params=_params(("arbitrary",)),
    )(h, a, b, g, target)


def _ffn_fwd(h, u, post_g, next_g, get_w, idx, tag):
    gu = None
    for i, key in enumerate(GATE_UP_KEYS["gu" + idx]):
        w = get_w(key, h if gu is None else gu)
        gu = mm_nn_col(u, w, F32, "%s_gate_up_%d" % (tag, i), slot0=i, total=2, into=gu)
    act = swiglu_act(gu, tag + "_act")
    y = mm_nn_2d(act, get_w("down" + idx, gu), F32, tag + "_down")
    out, u_next = resid_norm(h, y, post_g, MACARON_SCALE, next_g, tag + "_out")
    return out, u_next, (h, u, gu, act, y)


def _ffn_bwd(dh, saved, pre_g, post_g, get_w, emit, advance, idx, tag):
    h, u, gu, act, y = saved
    dy, d_post = post_bwd(dh, y, post_g, MACARON_SCALE, tag + "_post_bwd")
    m1 = emit("down" + idx, mm_tn_2d(act, dy, F32, tag + "_dw_down"))
    dact = mm_nt_2d(dy, get_w("down" + idx), F32, tag + "_dact", after=[m1])
    m2 = advance(dact)
    dgu = swiglu_bwd(dact, gu, tag + "_act_bwd")
    m3 = emit("gu" + idx, mm_tn_col(u, dgu, N_CHIPS, F32, tag + "_dw_gate_up"))
    du = None
    for i, key in enumerate(GATE_UP_KEYS["gu" + idx]):
        du = mm_nt_col(dgu, get_w(key), F32, "%s_du_%d" % (tag, i), after=[m2, m3] if du is None else (),
                       slot0=i, init=du)
    m4 = advance(du)
    dh_in, d_pre = pre_bwd(dh, h, pre_g, [du], tag + "_pre_bwd", after=[m4])
    return dh_in, d_pre, d_post


def _heads_col(a):
    t = a.shape[0]
    at = a[:, :HEADS].T
    return at.reshape(HEADS, t, 1), at.reshape(HEADS, 1, t)


def layer_step(x, p, target, gains, fox_bias, lb_logits, norm_g, get_w, emit, advance, early):
    t = x.shape[0]
    u1 = norm_in(x, gains["ffn1_pre"], "ffn1_norm")
    h1, u2, s1 = _ffn_fwd(x, u1, gains["ffn1_post"], gains["mix_pre"], get_w, "1", "ffn1")

    proj = mm_nt_2d(u2, get_w("in_main", h1), F32, "mix_in")
    fa = mm_nt_2d(u2, get_w("in_fa"), F32, "mix_in_fa")
    mark = early("proj", proj)
    c = fox_prep(fa, fox_bias, "fox_prep")
    c_col, c_row = _heads_col(c)
    o_a = fox_fwd(proj, c_col, c_row, "fox_fwd")
    o_raw, states, scores = hgrn_fwd(proj, lb_logits, "hgrn_fwd", after=[mark])
    mark = early("gu2", o_raw)
    o_b = gnorm_fwd(o_raw, proj, norm_g, "hgrn_norm", after=[mark])
    o_ab = jnp.stack([o_a, o_b])
    y_ab = _mm_branches(o_ab, get_w("proj", proj), "mix_proj")
    merged = merge_fwd(proj, y_ab, "mix_merge")
    mo = mm_nn_2d(merged, get_w("out"), F32, "mix_out")
    h2, u3 = resid_norm(h1, mo, gains["mix_post"], 1.0, gains["ffn2_pre"], "mix_resid")

    h3, u4, s3 = _ffn_fwd(h2, u3, gains["ffn2_post"], gains["ple_pre"], get_w, "2", "ffn2")

    a4 = mm_nn_2d(u4, get_w("ple_gate"), F32, "ple_gate")
    b4 = mm_nn_col(p, get_w("ple_proj"), F32, "ple_proj")[0]
    loss, dh4, da4, db4, d_ple_post = ple_tail(h3, a4, b4, gains["ple_post"], target, "ple_tail")

    marks = [emit("ple_gate", mm_tn_2d(u4, da4, F32, "ple_dw_gate")),
             emit("ple_proj", mm_tn_col(p, db4[None], N_CHIPS, F32, "ple_dw_proj"))]
    du4 = mm_nt_2d(da4, get_w("ple_gate"), F32, "ple_du", after=marks)
    dh3, d_ple_pre = pre_bwd(dh4, h3, gains["ple_pre"], [du4], "ple_pre_bwd", after=[advance(du4)])

    dh2, d_f2_pre, d_f2_post = _ffn_bwd(dh3, s3, gains["ffn2_pre"], gains["ffn2_post"], get_w, emit, advance,
                                        "2", "ffn2")

    dmo, d_mix_post = post_bwd(dh2, mo, gains["mix_post"], 1.0, "mix_post_bwd")
    marks = [emit("out", mm_tn_2d(merged, dmo, F32, "mix_dw_out"))]
    dmerged = mm_nt_2d(dmo, get_w("out"), F32, "mix_dmerged", after=marks)
    dgate, dy_ab = merge_bwd(dmerged, proj, y_ab, "mix_merge_bwd")
    marks = [advance(dmerged), emit("proj", _mm_branches_dw(o_ab, dy_ab, "mix_dw_proj"))]
    do_ab = _mm_branches_bwd(dy_ab, get_w("proj"), "mix_do")
    do_raw, dg_b, d_norm_g = gnorm_bwd(do_ab, o_raw, proj, norm_g, "hgrn_norm_bwd")
    dq_b, df_b, di_b, dlb = hgrn_bwd(proj, lb_logits, states, scores, do_raw, "hgrn_bwd")
    d_lb_logits = lb_bwd(dlb, lb_logits, "lb_bwd")
    dq_a, dk_a, dv_a, dc_col, dc_row = fox_bwd(proj, c_col, c_row, do_ab, "fox_bwd")
    dc = (dc_col.reshape(HEADS, t) + dc_row.reshape(HEADS, t)).T
    dc = jnp.pad(dc, ((0, 0), (0, LANES - HEADS)))
    dfa, d_fox_bias = fox_post_bwd(dc, fa, fox_bias, "fox_post_bwd")
    dproj = jnp.concatenate([dq_a, dk_a.astype(BF16), dv_a.astype(BF16), dq_b, df_b, di_b, dg_b,
                             dgate[0], dgate[1]], axis=1)
    marks.append(emit("in_main", mm_tn_2d(dproj, u2, F32, "mix_dw_in")))
    marks.append(emit("in_fa", mm_tn_2d(dfa, u2, F32, "mix_dw_in_fa")))
    du2a = mm_nn_2d(dproj, get_w("in_main"), F32, "mix_du", after=marks)
    du2b = mm_nn_2d(dfa, get_w("in_fa"), F32, "mix_du_fa")
    dh1, d_mix_pre = pre_bwd(dh2, h1, gains["mix_pre"], [du2a, du2b], "mix_pre_bwd", after=[advance(du2a)])

    dx, d_f1_pre, d_f1_post = _ffn_bwd(dh1, s1, gains["ffn1_pre"], gains["ffn1_post"], get_w, emit, advance,
                                       "1", "ffn1")

    small = dict(ffn1_pre=d_f1_pre, ffn1_post=d_f1_post, mix_pre=d_mix_pre, mix_post=d_mix_post,
                 ffn2_pre=d_f2_pre, ffn2_post=d_f2_post, ple_pre=d_ple_pre, ple_post=d_ple_post,
                 fox_bias=d_fox_bias, lb_logits=d_lb_logits, norm_g=d_norm_g)
    return loss, dx, small


def _mm_branches(o_ab, w_proj, name):
    g, t, kk = o_ab.shape
    _, jn, _, ns = w_proj.shape
    tm = _pick(t, (512, 256, 128))
    return _mm(o_ab, w_proj, mode="nn", grid=(t // tm, g * jn, 1),
               a_spec=pl.BlockSpec((None, tm, kk), lambda i, j, k: (j // jn, i, 0)),
               b_spec=pl.BlockSpec((None, None, kk, ns), lambda i, j, k: (j // jn, j % jn, 0, 0)),
               o_spec=pl.BlockSpec((None, tm, ns), lambda i, j, k: (j // jn, i, j % jn)),
               out_shape=jax.ShapeDtypeStruct((g, t, jn * ns), F32), acc_shape=(tm, ns), name=name)


def _mm_branches_bwd(dy_ab, w_proj, name):
    g, t, _ = dy_ab.shape
    _, jn, kk, ns = w_proj.shape
    tm = _pick(t, (512, 256, 128))
    return _mm(dy_ab, w_proj, mode="nt", grid=(t // tm, g, jn),
               a_spec=pl.BlockSpec((None, tm, ns), lambda i, j, k: (j, i, k)),
               b_spec=pl.BlockSpec((None, None, kk, ns), lambda i, j, k: (j, k, 0, 0)),
               o_spec=pl.BlockSpec((None, tm, kk), lambda i, j, k: (j, i, 0)),
               out_shape=jax.ShapeDtypeStruct((g, t, kk), F32), acc_shape=(tm, kk), name=name)


def _mm_branches_dw(o_ab, dy_ab, name):
    g, t, kk = o_ab.shape
    d = dy_ab.shape[2]
    jn = N_CHIPS
    ns = d // jn
    return _mm(o_ab, dy_ab, mode="tn", grid=(1, g * jn, 1),
               a_spec=pl.BlockSpec((None, t, kk), lambda i, j, k: (j // jn, 0, 0)),
               b_spec=pl.BlockSpec((None, t, ns), lambda i, j, k: (j // jn, 0, j % jn)),
               o_spec=pl.BlockSpec((None, None, kk, ns), lambda i, j, k: (j // jn, j % jn, 0, 0)),
               out_shape=jax.ShapeDtypeStruct((g, jn, kk, ns), F32), acc_shape=(kk, ns), name=name)


HBM_SPEC = pl.BlockSpec(memory_space=pltpu.HBM)
SEM_SPEC = pl.BlockSpec(memory_space=pltpu.SEMAPHORE)
ANY_SPEC = pl.BlockSpec(memory_space=pl.ANY)
EFFECT = pltpu.SideEffectType.DATAFLOW_SIDE_EFFECTING


def _in_hbm(a):
    return pltpu.with_memory_space_constraint(a, pltpu.HBM)


def _place():
    x, y, c = lax.axis_index("x"), lax.axis_index("y"), lax.axis_index("c")
    chips = [(1 - x, y), (x, 1 - y), (1 - x, 1 - y)]
    return x, y, c, chips


def _half(shape, which, axis):
    n = shape[-2 + axis] // 2
    cut = pl.ds(which * n, n)
    return (cut, slice(None)) if axis == 0 else (slice(None), cut)


def _half_shape(shape, axis):
    s = list(shape)
    s[len(s) - 2 + axis] //= 2
    return tuple(s)


def _remote(src, dst, send_sems, recv_sems, k, to):
    return pltpu.make_async_remote_copy(src_ref=src, dst_ref=dst, send_sem=send_sems.at[k], recv_sem=recv_sems.at[k],
                                        device_id=to, device_id_type=MESH)


def split_start(name, srcs, lands, counts, copies):
    ns, nl, nset = len(srcs), len(lands), len(counts)

    def body(*refs):
        src_refs, land_refs = refs[:ns], refs[ns:ns + nl]
        sems = refs[ns + nl:ns + nl + 2 * nset]
        for s, plan in enumerate(copies(src_refs, land_refs)):
            for k, (src, dst, to) in enumerate(plan):
                _remote(src, dst, sems[2 * s], sems[2 * s + 1], k, to).start()
        refs[-1][...] = jnp.zeros_like(refs[-1])

    out_shape = []
    for n in counts:
        out_shape += [pltpu.SemaphoreType.DMA((n,)), pltpu.SemaphoreType.DMA((n,))]
    out_shape += [pltpu.HBM(a.shape, a.dtype) for a in list(srcs) + list(lands)]
    out_shape.append(jax.ShapeDtypeStruct((8, LANES), F32))
    res = pl.pallas_call(
        body, name=name, out_shape=tuple(out_shape), in_specs=[HBM_SPEC] * (ns + nl),
        out_specs=tuple([SEM_SPEC] * (2 * nset) + [HBM_SPEC] * (ns + nl) + [pl.BlockSpec(memory_space=pltpu.VMEM)]),
        input_output_aliases={i: 2 * nset + i for i in range(ns + nl)},
        compiler_params=pltpu.CompilerParams(has_side_effects=EFFECT),
    )(*[_in_hbm(a) for a in list(srcs) + list(lands)])
    sems = [(res[2 * s], res[2 * s + 1]) for s in range(nset)]
    return sems, list(res[2 * nset:2 * nset + ns]), list(res[2 * nset + ns:-1]), res[-1]


def split_wait(name, srcs, lands, sems, afters, copies):
    afters = [a for a in afters if a is not None]
    ns, nl, na = len(srcs), len(lands), len(afters)

    def body(*refs):
        src_refs, land_refs = refs[:ns], refs[ns:ns + nl]
        send_sems, recv_sems = refs[ns + nl:ns + nl + 2]
        for k, (src, dst, to) in enumerate(copies(src_refs, land_refs)):
            cp = _remote(src, dst, send_sems, recv_sems, k, to)
            cp.wait_send()
            cp.wait_recv()

    res = pl.pallas_call(
        body, name=name, out_shape=tuple(pltpu.HBM(a.shape, a.dtype) for a in list(srcs) + list(lands)),
        in_specs=[HBM_SPEC] * (ns + nl) + [SEM_SPEC, SEM_SPEC] + [ANY_SPEC] * na,
        out_specs=tuple([HBM_SPEC] * (ns + nl)), input_output_aliases={i: i for i in range(ns + nl)},
        compiler_params=pltpu.CompilerParams(has_side_effects=EFFECT),
    )(*srcs, *lands, sems[0], sems[1], *afters)
    return list(res[:ns]), list(res[ns:])


def _gather_plan(blocks):
    def copies(src_refs, land_refs):
        x, y, c, chips = _place()
        j_me = 2 * x + y
        plan = []
        for si, li, g, axis in blocks:
            src, land = src_refs[si], land_refs[li].at[g]
            mine = _half(src.shape, c, axis)
            for px, py in chips:
                plan.append((src.at[mine], land.at[(j_me,) + mine], (px, py, c)))
            plan.append((src, land.at[j_me], (x, y, 1 - c)))
        return plan
    return copies


def _gather_arrivals(blocks):
    def copies(src_refs, land_refs):
        x, y, c, chips = _place()
        j_me = 2 * x + y
        plan = []
        for si, li, g, axis in blocks:
            src, land = src_refs[si], land_refs[li].at[g]
            mine = _half(src.shape, c, axis)
            for px, py in chips:
                plan.append((src.at[mine], land.at[(2 * px + py,) + mine], (px, py, c)))
            plan.append((src, land.at[j_me], (x, y, 1 - c)))
        return plan
    return copies


def _pass_plan(blocks, arrivals):
    def copies(src_refs, land_refs):
        x, y, c, chips = _place()
        plan = []
        for li, g, axis in blocks:
            land = src_refs[li].at[g]
            half = _half(land.shape[1:], (1 - c) if arrivals else c, axis)
            for px, py in chips:
                part = land.at[(2 * px + py,) + half]
                plan.append((part, part, (x, y, 1 - c)))
        return plan
    return copies


def gather_pass(name, lands, blocks):
    n = len(lands)

    def body(*refs):
        outs = refs[n:2 * n]
        send_sems, recv_sems = refs[2 * n:]
        x, y, c, chips = _place()
        sent = []
        for i, (li, g, axis) in enumerate(blocks):
            land = outs[li].at[g]
            mine = _half(land.shape[1:], c, axis)
            for k, (px, py) in enumerate(chips):
                part = land.at[(2 * px + py,) + mine]
                cp = _remote(part, part, send_sems, recv_sems, 3 * i + k, (x, y, 1 - c))
                cp.start()
                sent.append(cp)
        for i, (li, g, axis) in enumerate(blocks):
            land = outs[li].at[g]
            other = _half(land.shape[1:], 1 - c, axis)
            for k, (px, py) in enumerate(chips):
                part = land.at[(2 * px + py,) + other]
                _remote(part, part, send_sems, recv_sems, 3 * i + k, (x, y, 1 - c)).wait_recv()
        for cp in sent:
            cp.wait_send()

    m = 3 * len(blocks)
    return pl.pallas_call(
        body, name=name, in_specs=[ANY_SPEC] * n, out_specs=[ANY_SPEC] * n,
        out_shape=[jax.ShapeDtypeStruct(a.shape, a.dtype) for a in lands],
        input_output_aliases={i: i for i in range(n)},
        scratch_shapes=[pltpu.SemaphoreType.DMA((m,)), pltpu.SemaphoreType.DMA((m,))],
    )(*lands)


def _pair_plan(axes):
    def copies(src_refs, land_refs):
        x, y, c, _ = _place()
        return [(src_refs[i].at[(slice(None), slice(None)) + _half(src_refs[i].shape, 1 - c, a)], land_refs[i],
                 (x, y, 1 - c)) for i, a in enumerate(axes)]
    return copies


def _scatter_plan(n):
    def copies(src_refs, land_refs):
        x, y, c, chips = _place()
        return [(src_refs[i].at[:, 2 * px + py], land_refs[i].at[k], (px, py, c))
                for i in range(n) for k, (px, py) in enumerate(chips)]
    return copies


def _broadcast_plan(axes, arrivals):
    def copies(src_refs, land_refs):
        x, y, c, _ = _place()
        plan = []
        for i, a in enumerate(axes):
            part = src_refs[i].at[(slice(None),) + _half(src_refs[i].shape, (1 - c) if arrivals else c, a)]
            plan.append((part, part, (x, y, 1 - c)))
        return plan
    return copies


N_DEV = 8
SLAB_ROWS = 16


def allreduce_small(slab, after):
    def body(x_ref, after_ref, o_ref, land, send_sems, recv_sems):
        x, y, c, _ = _place()
        me = 4 * x + 2 * y + c
        land[me] = x_ref[...]
        copies = []
        for d in range(1, N_DEV):
            to = (me + d) % N_DEV
            cp = pltpu.make_async_remote_copy(
                src_ref=x_ref, dst_ref=land.at[me], send_sem=send_sems.at[d - 1], recv_sem=recv_sems.at[me],
                device_id=(to // 4, (to // 2) % 2, to % 2), device_id_type=MESH)
            cp.start()
            copies.append(cp)
        for d in range(1, N_DEV):
            frm = (me + d) % N_DEV
            pltpu.make_async_remote_copy(
                src_ref=x_ref, dst_ref=land.at[frm], send_sem=send_sems.at[d - 1], recv_sem=recv_sems.at[frm],
                device_id=(frm // 4, (frm // 2) % 2, frm % 2), device_id_type=MESH).wait_recv()
        for cp in copies:
            cp.wait_send()
        acc = land[0]
        for s in range(1, N_DEV):
            acc = acc + land[s]
        o_ref[...] = acc

    vm = pl.BlockSpec(memory_space=pltpu.VMEM)
    return pl.pallas_call(
        body, name="allreduce_small", in_specs=[vm, ANY_SPEC], out_specs=vm,
        out_shape=jax.ShapeDtypeStruct(slab.shape, F32),
        scratch_shapes=[pltpu.VMEM((N_DEV,) + slab.shape, F32), pltpu.SemaphoreType.DMA((N_DEV - 1,)),
                        pltpu.SemaphoreType.DMA((N_DEV,))],
    )(slab, after)


BLOCK_BYTES = 3 * 1024 * 1024


def _tiles_2d(r, c, budget=BLOCK_BYTES):
    if r % 8 == 0:
        tc = c if c % LANES else _pick(c, (2048, 1408, 1024, 512, 256, 128))
        tr = 8
        for cand in (512, 256, 128, 64, 32, 16, 8):
            if r % cand == 0 and cand * tc * 4 <= budget:
                tr = cand
                break
        if tr >= 64 or c % LANES or r * LANES * 4 > budget:
            return tr, tc
    tc = LANES
    for cand in (1024, 512, 256, 128):
        if c % cand == 0 and r * cand * 4 <= budget:
            tc = cand
            break
    return r, tc


def _grid_spec(grid, in_specs, out_specs):
    return pltpu.PrefetchScalarGridSpec(num_scalar_prefetch=1, grid=grid, in_specs=in_specs, out_specs=out_specs)


def _own(axis, nr, nc):
    if axis == 0:
        return lambda i, j, where: (where[1] * nr + i, j)
    return lambda i, j, where: (i, where[1] * nc + j)


def pair_add(where, grad, recv, axis, name):
    g, jn, hr, hc = recv.shape
    tr, tc = _tiles_2d(hr, hc)
    nr, nc = hr // tr, hc // tc
    own = _own(axis, nr, nc)
    others = jn - 1

    def body(where_ref, a_ref, b_ref, o_ref):
        o_ref[...] = (a_ref[...] + b_ref[...]).astype(BF16)

    def block(a, where):
        return a // others, (where[0] + 1 + a % others) % jn

    blk = pl.BlockSpec((None, None, tr, tc), lambda a, i, j, where: block(a, where) + (i, j))
    mine = pl.BlockSpec((None, None, tr, tc), lambda a, i, j, where: block(a, where) + own(i, j, where))
    return pl.pallas_call(
        body, name=name, grid_spec=_grid_spec((g * others, nr, nc), [mine, blk], blk),
        out_shape=jax.ShapeDtypeStruct(recv.shape, BF16),
        compiler_params=_params(("parallel", "parallel", "parallel")),
    )(where, grad, recv)


def chip_add(where, grad, pair, recv, axis, name):
    g, jn, hr, hc = pair.shape
    tr, tc = _tiles_2d(hr, hc)
    nr, nc = hr // tr, hc // tc
    own = _own(axis, nr, nc)
    full = (g, 2 * hr, hc) if axis == 0 else (g, hr, 2 * hc)

    def body(where_ref, a_ref, p_ref, b_ref, o_ref):
        s = a_ref[...] + p_ref[...]
        for k in range(3):
            s = s + b_ref[k].astype(F32)
        o_ref[...] = s

    return pl.pallas_call(
        body, name=name,
        grid_spec=_grid_spec((g, nr, nc),
                             [pl.BlockSpec((None, None, tr, tc), lambda a, i, j, where: (a, where[0]) + own(i, j, where)),
                              pl.BlockSpec((None, None, tr, tc), lambda a, i, j, where: (a, where[0], i, j)),
                              pl.BlockSpec((3, None, tr, tc), lambda a, i, j, where: (0, a, i, j))],
                             pl.BlockSpec((None, tr, tc), lambda a, i, j, where: (a,) + own(i, j, where))),
        out_shape=jax.ShapeDtypeStruct(full, F32), compiler_params=_params(("parallel", "parallel", "parallel")),
    )(where, grad, pair, recv)


def _adam_math(w, g, m, v):
    m2 = ADAM_B1 * m + (1.0 - ADAM_B1) * g
    v2 = ADAM_B2 * v + (1.0 - ADAM_B2) * (g * g)
    m_hat = m2 / (1.0 - ADAM_B1 ** ADAM_STEP)
    v_hat = v2 / (1.0 - ADAM_B2 ** ADAM_STEP)
    delta = -ADAM_LR * (m_hat / (jnp.sqrt(v_hat) + ADAM_EPS) + ADAM_WD * w)
    return delta, m2, v2


def adamw(grad, idx, w, m, v, name):
    _, r, cc = w.shape
    rg = grad.shape[1]
    tr, tc = _tiles_2d(r, cc, BLOCK_BYTES // 2)
    assert rg == r or tr == r
    gr = tr if rg == r else rg

    def body(g_ref, w_ref, m_ref, v_ref, go_ref, d_ref, mo_ref, vo_ref):
        g = g_ref[pl.ds(0, tr), :]
        delta, m2, v2 = _adam_math(w_ref[...], g, m_ref[...], v_ref[...])
        go_ref[...] = g
        d_ref[...] = delta
        mo_ref[...] = m2
        vo_ref[...] = v2

    blk = pl.BlockSpec((None, tr, tc), lambda i, j: (0, i, j))
    return pl.pallas_call(
        body, name=name, grid=(r // tr, cc // tc),
        in_specs=[pl.BlockSpec((None, gr, tc), lambda i, j: (idx, i, j)), blk, blk, blk], out_specs=[blk] * 4,
        out_shape=[jax.ShapeDtypeStruct(w.shape, F32)] * 4, compiler_params=_params(("parallel", "parallel")),
    )(grad, w, m, v)


def adamw_small(g, w, m, v):
    def body(g_ref, w_ref, m_ref, v_ref, d_ref, mo_ref, vo_ref):
        delta, m2, v2 = _adam_math(w_ref[...], g_ref[...], m_ref[...], v_ref[...])
        d_ref[...] = delta
        mo_ref[...] = m2
        vo_ref[...] = v2

    return pl.pallas_call(body, name="adamw_small", out_shape=[jax.ShapeDtypeStruct(w.shape, F32)] * 3)(g, w, m, v)


GAINS = ("ffn1_pre", "ffn1_post", "mix_pre", "mix_post", "ffn2_pre", "ffn2_post", "ple_pre", "ple_post")
WEIGHTS = ("ffn1_pre_g", "ffn1_post_g", "ffn1_w_gate", "ffn1_w_up", "ffn1_w_down", "mix_pre_g", "mix_post_g",
           "mix_w_in", "fox_f_bias", "hgrn_lb_logits", "hgrn_norm_g", "mix_w_proj_fox", "mix_w_proj_hgrn",
           "mix_w_out", "ffn2_pre_g", "ffn2_post_g", "ffn2_w_gate", "ffn2_w_up", "ffn2_w_down", "ple_pre_g",
           "ple_post_g", "ple_w_gate", "ple_w_proj")
GROUPS = dict(gu1=(("ffn1_w_gate", "ffn1_w_up"), 0), down1=(("ffn1_w_down",), 0), win=(("mix_w_in",), 1),
              proj=(("mix_w_proj_fox", "mix_w_proj_hgrn"), 0), out=(("mix_w_out",), 0),
              gu2=(("ffn2_w_gate", "ffn2_w_up"), 0), down2=(("ffn2_w_down",), 0), ple_gate=(("ple_w_gate",), 0),
              ple_proj=(("ple_w_proj",), 0))
TRANSPOSED = ("mix_w_in",)
ROW_BLOCKS = ("down1", "down2", "out", "ple_gate")
GATHER_SETS = (("gate1",), ("up1",), ("down1",), ("win",), ("proj", "out"), ("gu2", "down2", "ple_gate", "ple_proj"))
GATHER_GROUPS = dict(GROUPS, gate1=(("ffn1_w_gate",), 0), up1=(("ffn1_w_up",), 0))
GATE_UP_KEYS = dict(gu1=("gate1", "up1"), gu2=("gu2",))
REDUCE_SETS = (("ple_gate", "ple_proj", "down2", "gu2"), ("out", "proj", "win"), ("down1",), ("gu1",))


def _pad_row(a, width):
    a = a.reshape(1, -1)
    return jnp.pad(a, ((0, 0), (0, width - a.shape[1])))


def _pack_small(vals):
    d = D_MODEL
    rows = [vals[n + "_g"].reshape(1, d) for n in GAINS]
    rows.append(_pad_row(vals["fox_f_bias"], d))
    lg = vals["hgrn_lb_logits"]
    rows += [_pad_row(lg[0], d), _pad_row(lg[1], d), _pad_row(vals["hgrn_norm_g"], d)]
    slab = jnp.concatenate(rows, axis=0)
    return jnp.pad(slab, ((0, SLAB_ROWS - slab.shape[0]), (0, 0)))


def _unpack_small(slab):
    out = {n + "_g": slab[i:i + 1] for i, n in enumerate(GAINS)}
    out["fox_f_bias"] = slab[8:9, :HEADS]
    out["hgrn_lb_logits"] = slab[9:11, :WIDTH]
    out["hgrn_norm_g"] = slab[11:12, :HEAD_DIM]
    return out


def _split_in(win_t):
    lo = 3 * WIDTH
    main = jnp.concatenate([win_t[:lo], win_t[lo + HEADS:]], axis=0)
    fa = jnp.pad(win_t[lo:lo + HEADS], ((0, LANES - HEADS), (0, 0)))
    return main, fa


def _join_in(main, fa):
    lo = 3 * WIDTH
    return jnp.concatenate([main[:lo], fa[:HEADS], main[lo:]], axis=0)


def _as_block(name, a):
    return jnp.swapaxes(a, 1, 2) if name in TRANSPOSED else a


def _send_block(name, a, mark):
    blk = _as_block(name, a)[0]
    if mark is not None:
        blk = blk + mark[0, 0]
    return blk.astype(BF16)


def kernel(x, p, ffn1_pre_g, ffn1_post_g, ffn1_w_gate, ffn1_w_up, ffn1_w_down, mix_pre_g, mix_post_g, mix_w_in, fox_f_bias, hgrn_lb_logits, hgrn_norm_g, mix_w_proj_fox, mix_w_proj_hgrn, mix_w_out, ffn2_pre_g, ffn2_post_g, ffn2_w_gate, ffn2_w_up, ffn2_w_down, ple_pre_g, ple_post_g, ple_w_gate, ple_w_proj, loss_target, m_ffn1_pre_g, m_ffn1_post_g, m_ffn1_w_gate, m_ffn1_w_up, m_ffn1_w_down, m_mix_pre_g, m_mix_post_g, m_mix_w_in, m_fox_f_bias, m_hgrn_lb_logits, m_hgrn_norm_g, m_mix_w_proj_fox, m_mix_w_proj_hgrn, m_mix_w_out, m_ffn2_pre_g, m_ffn2_post_g, m_ffn2_w_gate, m_ffn2_w_up, m_ffn2_w_down, m_ple_pre_g, m_ple_post_g, m_ple_w_gate, m_ple_w_proj, v_ffn1_pre_g, v_ffn1_post_g, v_ffn1_w_gate, v_ffn1_w_up, v_ffn1_w_down, v_mix_pre_g, v_mix_post_g, v_mix_w_in, v_fox_f_bias, v_hgrn_lb_logits, v_hgrn_norm_g, v_mix_w_proj_fox, v_mix_w_proj_hgrn, v_mix_w_out, v_ffn2_pre_g, v_ffn2_post_g, v_ffn2_w_gate, v_ffn2_w_up, v_ffn2_w_down, v_ple_pre_g, v_ple_post_g, v_ple_w_gate, v_ple_w_proj):
    args = dict(locals())
    wts = {n: args[n] for n in WEIGHTS}
    mom = {n: args["m_" + n] for n in WEIGHTS}
    var = {n: args["v_" + n] for n in WEIGHTS}
    d = D_MODEL
    where = jnp.stack([2 * lax.axis_index("x") + lax.axis_index("y"), lax.axis_index("c")]).astype(jnp.int32)

    def start_sets(name, which, mark):
        srcs, lands, plans = [], [], []
        for si in which:
            blocks = []
            for g in GATHER_SETS[si]:
                names, axis = GATHER_GROUPS[g]
                for pos, n in enumerate(names):
                    blocks.append((len(srcs), len(lands), pos, axis))
                    srcs.append(_send_block(n, wts[n], mark))
                lands.append(lax.empty((len(names), N_CHIPS) + srcs[-1].shape, BF16))
            plans.append(blocks)
        sems, srcs, lands, mark = split_start(name, srcs, lands, [4 * len(b) for b in plans],
                                              lambda sr, lr: [_gather_plan(b)(sr, lr) for b in plans])
        out = {}
        for k, si in enumerate(which):
            s_idx = sorted({b[0] for b in plans[k]})
            l_idx = sorted({b[1] for b in plans[k]})
            local = [(s_idx.index(a), l_idx.index(b), pos, ax) for a, b, pos, ax in plans[k]]
            out[si] = (sems[k], [srcs[i] for i in s_idx], [lands[i] for i in l_idx], local)
        return out, mark

    flying, mark = start_sets("gather_start_0", [0], None)
    rest, all_started = start_sets("gather_start_1", list(range(1, len(GATHER_SETS))), mark)
    flying.update(rest)
    full, passing = {}, {}

    def set_of(key):
        g = "win" if key in ("in_main", "in_fa") else key
        return g, [g in s for s in GATHER_SETS].index(True)

    def arrive(si, after):
        sem, srcs, lands, local = flying.pop(si)
        _, got = split_wait("gather_wait_%d" % si, srcs, lands, sem, [after, all_started], _gather_arrivals(local))
        return got, [(b, pos, ax) for _, b, pos, ax in local]

    def early(key, after):
        g, si = set_of(key)
        if g in full or si not in flying:
            return None
        got, blocks = arrive(si, after)
        sem, got, _, mark = split_start("pass_start_%d" % si, got, [], [3 * len(blocks)],
                                        lambda sr, lr: [_pass_plan(blocks, False)(sr, lr)])
        passing[si] = (sem[0], got, blocks)
        return mark

    def land_set(si, after):
        if si in passing:
            sem, got, blocks = passing.pop(si)
            got, _ = split_wait("pass_wait_%d" % si, got, [], sem, [after], _pass_plan(blocks, True))
        else:
            got, blocks = arrive(si, after)
            got = gather_pass("gather_pass_%d" % si, got, blocks)
        for g, arr in zip(GATHER_SETS[si], got):
            if g == "win":
                full["win"] = arr
                full["in_main"], full["in_fa"] = _split_in(arr.reshape(-1, d))
            else:
                full[g] = arr.reshape(-1, d) if g in ROW_BLOCKS else arr

    def get_w(key, after=None):
        g, si = set_of(key)
        if g not in full:
            land_set(si, after)
        return full[key]

    grads, pairing, started = {}, [], {}
    rows4 = lambda a: a.reshape(1, N_CHIPS, a.shape[0] // N_CHIPS, a.shape[1])

    def emit(key, grad):
        if key in ("in_main", "in_fa"):
            grads[key] = grad
            if "in_main" not in grads or "in_fa" not in grads:
                return None
            key, grad = "win", rows4(_join_in(grads["in_main"], grads["in_fa"]))
        grads[key] = grad if grad.ndim == 4 else rows4(grad)
        for si, s in enumerate(REDUCE_SETS):
            if key in s and all(g in grads for g in s):
                axes = [GROUPS[g][1] for g in s]
                own = [grads[g] for g in s]
                zones = [lax.empty(_half_shape(a.shape, ax), F32) for a, ax in zip(own, axes)]
                plan = _pair_plan(axes)
                sem, own, zones, mark = split_start("pair_start_%d" % si, own, zones, [len(s)],
                                                    lambda sr, lr: [plan(sr, lr)])
                pairing.append((si, sem[0], own, zones, axes, plan))
                return mark
        return None

    def advance(value):
        mark = None
        while pairing:
            si, sem, own, zones, axes, plan = pairing.pop(0)
            s = REDUCE_SETS[si]
            own, recv = split_wait("pair_wait_%d" % si, own, zones, sem, [value], plan)
            parts = [pair_add(where, a, r, ax, "pair_add_" + g) for g, a, r, ax in zip(s, own, recv, axes)]
            zones = [lax.empty((3, q.shape[0]) + q.shape[2:], BF16) for q in parts]
            plan = _scatter_plan(len(s))
            sem, parts, zones, mark = split_start("scatter_start_%d" % si, parts, zones, [3 * len(s)],
                                                  lambda sr, lr: [plan(sr, lr)])
            started[si] = (sem[0], parts, zones, own, recv, axes, plan)
        return mark

    gains = {n: wts[n + "_g"] for n in GAINS}
    loss, dx, small = layer_step(x[0], p[0, 0].astype(BF16), loss_target[0], gains, _pad_row(fox_f_bias, LANES),
                                 hgrn_lb_logits, hgrn_norm_g, get_w, emit, advance, early)

    out_g, out_d, out_m, out_v = {}, {}, {}, {}
    after, crossing = None, []

    def finish(si, sem, halves, axes, mark):
        reduced, _ = split_wait("broadcast_wait_%d" % si, halves, [], sem, [mark], _broadcast_plan(axes, True))
        last = None
        for g, red in zip(REDUCE_SETS[si], reduced):
            for idx, n in enumerate(GROUPS[g][0]):
                res = adamw(red, idx, _as_block(n, wts[n]), _as_block(n, mom[n]), _as_block(n, var[n]), "adamw_" + n)
                out_g[n], out_d[n], out_m[n], out_v[n] = [_as_block(n, r) for r in res]
                last = res[1]
        return last

    for si, s in enumerate(REDUCE_SETS):
        sem, parts, zones, own, recv, axes, plan = started[si]
        _, zones = split_wait("scatter_wait_%d" % si, parts, zones, sem, [dx, after], plan)
        halves = [chip_add(where, a, r, z, ax, "chip_add_" + g) for g, a, r, z, ax in zip(s, own, recv, zones, axes)]
        plan = _broadcast_plan(axes, False)
        sem, halves, _, mark = split_start("broadcast_start_%d" % si, halves, [], [len(s)],
                                           lambda sr, lr: [plan(sr, lr)])
        crossing.append((si, sem[0], halves, axes))
        if len(crossing) > 1:
            after = finish(*crossing.pop(0), mark)
    while crossing:
        after = finish(*crossing.pop(0), after)

    small_named = {n + "_g": small[n] for n in GAINS}
    small_named.update(fox_f_bias=small["fox_bias"][:, :HEADS], hgrn_lb_logits=small["lb_logits"],
                       hgrn_norm_g=small["norm_g"])
    g_small = allreduce_small(_pack_small(small_named), after)
    d_small, m_small, v_small = adamw_small(g_small, _pack_small(wts), _pack_small(mom), _pack_small(var))

    for dst, slab in ((out_g, g_small), (out_d, d_small), (out_m, m_small), (out_v, v_small)):
        dst.update(_unpack_small(slab))

    total = lax.psum(loss[0, 0], ("x", "y", "c"))
    return (total, dx[None], *[out_g[n] for n in WEIGHTS], *[out_d[n] for n in WEIGHTS],
            *[out_m[n] for n in WEIGHTS], *[out_v[n] for n in WEIGHTS])
```
